```python
import jax
import jax.numpy as jnp
from jax import lax
import numpy as np

D_MODEL = 1024
BATCH = 16
SEQ = 2048
DEPTH = 2

HEAD_DIM = 64
SB_HEADS = 8
FOX_HEADS = 8
DSA_HEADS = 8
DSA_KV_HEADS = 2
DSA_GROUP = DSA_HEADS // DSA_KV_HEADS
IDX_HEADS = 4
IDX_DIM = 64
BRANCH_WIDTH = SB_HEADS * HEAD_DIM
N_BRANCH = 3
D_FF = 4 * D_MODEL
ROPE_THETA = 500000.0
ROT_DIM = HEAD_DIM // 4
Q_BLOCK = 128
TOPK_MAX = 256
NORM_EPS = 1e-6

COL_SIZES = (
    SB_HEADS * HEAD_DIM, SB_HEADS * HEAD_DIM, SB_HEADS * HEAD_DIM,
    FOX_HEADS * HEAD_DIM, FOX_HEADS * HEAD_DIM, FOX_HEADS * HEAD_DIM,
    FOX_HEADS,
    DSA_HEADS * HEAD_DIM, DSA_KV_HEADS * HEAD_DIM, DSA_KV_HEADS * HEAD_DIM,
    IDX_HEADS * IDX_DIM, IDX_DIM, IDX_HEADS,
    N_BRANCH * D_MODEL,
)
COL_TOTAL = sum(COL_SIZES)
COL_SPLITS = tuple(sum(COL_SIZES[:i + 1]) for i in range(len(COL_SIZES) - 1))

kernel_name = 'hybrid_gated_sb_fox_dsa_trunk'


def _rmsnorm(x, g):
    xf = x.astype(jnp.float32)
    y = xf * lax.rsqrt(jnp.mean(xf * xf, axis=-1, keepdims=True) + NORM_EPS)
    return (y * g.astype(jnp.float32)).astype(x.dtype)


def _partial_rope(x, positions):
    half = ROT_DIM // 2
    inv_freq = ROPE_THETA ** (-jnp.arange(0, ROT_DIM, 2, dtype=jnp.float32) / ROT_DIM)
    ang = positions.astype(jnp.float32)[:, :, None] * inv_freq
    cos = jnp.cos(ang)[:, :, None, :]
    sin = jnp.sin(ang)[:, :, None, :]
    xf = x.astype(jnp.float32)
    x1, x2, rest = xf[..., :half], xf[..., half:ROT_DIM], xf[..., ROT_DIM:]
    out = jnp.concatenate([x1 * cos - x2 * sin, x2 * cos + x1 * sin, rest], axis=-1)
    return out.astype(x.dtype)


def _to_blocks(a):
    b, s = a.shape[0], a.shape[1]
    return jnp.moveaxis(a.reshape(b, s // Q_BLOCK, Q_BLOCK, *a.shape[2:]), 1, 0)


def _from_blocks(a):
    a = jnp.moveaxis(a, 0, 1)
    return a.reshape(a.shape[0], a.shape[1] * a.shape[2], *a.shape[3:])


def _stick_breaking_attention(q, k, v):
    b, s, h, dh = q.shape
    scale = dh ** -0.5
    kf = k.astype(jnp.float32)
    vf = v.astype(jnp.float32)
    s_idx = jnp.arange(s)
    t_blk = jnp.arange(s).reshape(s // Q_BLOCK, Q_BLOCK)

    def block(args):
        qb, tb = args
        z = jnp.einsum('bthd,bshd->bhts', qb.astype(jnp.float32), kf) * scale
        strict = s_idx[None, :] < tb[:, None]
        log_beta = jax.nn.log_sigmoid(z)
        log_1m = jnp.where(strict, jax.nn.log_sigmoid(-z), 0.0)
        later = lax.cumsum(log_1m, axis=3, reverse=True) - log_1m
        w = jnp.where(strict, jnp.exp(log_beta + later), 0.0)
        return jnp.einsum('bhts,bshd->bthd', w, vf)

    out = lax.map(block, (_to_blocks(q), t_blk))
    return _from_blocks(out).reshape(b, s, h * dh).astype(q.dtype)


def _forgetting_attention(q, k, v, log_f):
    b, s, h, dh = q.shape
    scale = dh ** -0.5
    kf = k.astype(jnp.float32)
    vf = v.astype(jnp.float32)
    c = jnp.cumsum(log_f, axis=1).transpose(0, 2, 1)
    c_blk = jnp.moveaxis(c.reshape(b, h, s // Q_BLOCK, Q_BLOCK), 2, 0)
    s_idx = jnp.arange(s)
    t_blk = jnp.arange(s).reshape(s // Q_BLOCK, Q_BLOCK)

    def block(args):
        qb, cb, tb = args
        z = jnp.einsum('bthd,bshd->bhts', qb.astype(jnp.float32), kf) * scale
        z = z + cb[..., None] - c[:, :, None, :]
        causal = s_idx[None, :] <= tb[:, None]
        p = jax.nn.softmax(jnp.where(causal, z, -jnp.inf), axis=-1)
        return jnp.einsum('bhts,bshd->bthd', p, vf)

    out = lax.map(block, (_to_blocks(q), c_blk, t_blk))
    return _from_blocks(out).reshape(b, s, h * dh).astype(q.dtype)


def _indexed_sparse_attention(q, k, v, q_idx, k_idx, w_idx, top_k):
    b, s, g, r, dh = q.shape
    scale = dh ** -0.5
    kif = k_idx.astype(jnp.float32)
    s_idx = jnp.arange(s)
    t_blk = jnp.arange(s).reshape(s // Q_BLOCK, Q_BLOCK)
    gather = jax.vmap(lambda a, i: a[i])

    def block(args):
        qb, qib, wb, tb = args
        dots = jnp.einsum('bthi,bsi->bths', qib.astype(jnp.float32), kif) * (IDX_DIM ** -0.5)
        score = jnp.einsum('bth,bths->bts', wb.astype(jnp.float32) * (IDX_HEADS ** -0.5),
                           jax.nn.relu(dots))
        causal = s_idx[None, :] <= tb[:, None]
        score = jnp.where(causal[None], score, -jnp.inf)
        _, sel = lax.top_k(score, top_k)
        valid = sel <= tb[None, :, None]
        k_sel = gather(k, sel).astype(jnp.float32)
        v_sel = gather(v, sel).astype(jnp.float32)
        z = jnp.einsum('btgrd,btkgd->btgrk', qb.astype(jnp.float32), k_sel) * scale
        z = jnp.where(valid[:, :, None, None, :], z, -jnp.inf)
        p = jax.nn.softmax(z, axis=-1)
        return jnp.einsum('btgrk,btkgd->btgrd', p, v_sel)

    out = lax.map(block, (_to_blocks(q), _to_blocks(q_idx), _to_blocks(w_idx), t_blk))
    return _from_blocks(out).reshape(b, s, g * r * dh).astype(q.dtype)


def _layer(x, positions, g_mix, w_in, b_forget, w_branch, w_out, g_mlp, w_up, w_down):
    b, s, _ = x.shape
    h = _rmsnorm(x, g_mix)
    proj = h @ w_in
    (qa, ka, va, qb, kb, vb, fl, qc, kc, vc, qi, ki, wi, gl) = jnp.split(proj, COL_SPLITS, axis=-1)

    def heads(a, n):
        return a.reshape(b, s, n, -1)

    br_a = _stick_breaking_attention(heads(qa, SB_HEADS), heads(ka, SB_HEADS), heads(va, SB_HEADS))
    log_f = jax.nn.log_sigmoid((fl + b_forget).astype(jnp.float32))
    br_b = _forgetting_attention(heads(qb, FOX_HEADS), heads(kb, FOX_HEADS),
                                 heads(vb, FOX_HEADS), log_f)
    qc = _partial_rope(heads(qc, DSA_HEADS), positions).reshape(b, s, DSA_KV_HEADS, DSA_GROUP, HEAD_DIM)
    kc = _partial_rope(heads(kc, DSA_KV_HEADS), positions)
    vc = heads(vc, DSA_KV_HEADS)
    qi = _partial_rope(heads(qi, IDX_HEADS), positions)
    ki = _partial_rope(ki[:, :, None, :], positions)[:, :, 0, :]
    top_k = min(TOPK_MAX, s // 4)
    br_c = _indexed_sparse_attention(qc, kc, vc, qi, ki, wi, top_k)

    gates = jax.nn.sigmoid(gl.reshape(b, s, N_BRANCH, D_MODEL))
    y = (gates[:, :, 0, :] * (br_a @ w_branch[0])
         + gates[:, :, 1, :] * (br_b @ w_branch[1])
         + gates[:, :, 2, :] * (br_c @ w_branch[2]))
    x = x + y @ w_out

    h2 = _rmsnorm(x, g_mlp)
    x = x + jnp.square(jax.nn.relu(h2 @ w_up)) @ w_down
    return x


def setup_inputs(seed: int = 0) -> dict:
    key = jax.random.key(seed)
    ks = jax.random.split(key, 12)
    f32 = jnp.float32
    x = jax.random.normal(ks[0], (BATCH, SEQ, D_MODEL), f32)
    positions = jnp.broadcast_to(jnp.arange(SEQ, dtype=jnp.int32), (BATCH, SEQ))
    g_mix = 1.0 + 0.02 * jax.random.normal(ks[1], (DEPTH, D_MODEL), f32)
    w_in = jax.random.normal(ks[2], (DEPTH, D_MODEL, COL_TOTAL), f32) * D_MODEL ** -0.5
    b_forget = jax.random.uniform(ks[3], (DEPTH, FOX_HEADS), f32, minval=1.0, maxval=4.0)
    w_branch = jax.random.normal(ks[4], (DEPTH, N_BRANCH, BRANCH_WIDTH, D_MODEL), f32) * BRANCH_WIDTH ** -0.5
    w_out = jax.random.normal(ks[5], (DEPTH, D_MODEL, D_MODEL), f32) * D_MODEL ** -0.5
    g_mlp = 1.0 + 0.02 * jax.random.normal(ks[6], (DEPTH, D_MODEL), f32)
    w_up = jax.random.normal(ks[7], (DEPTH, D_MODEL, D_FF), f32) * D_MODEL ** -0.5
    w_down = jax.random.normal(ks[8], (DEPTH, D_FF, D_MODEL), f32) * D_FF ** -0.5
    g_final = 1.0 + 0.02 * jax.random.normal(ks[9], (D_MODEL,), f32)
    return {'x': x, 'positions': positions, 'g_mix': g_mix, 'w_in': w_in,
            'b_forget': b_forget, 'w_branch': w_branch, 'w_out': w_out,
            'g_mlp': g_mlp, 'w_up': w_up, 'w_down': w_down, 'g_final': g_final}


def reference(x, positions, g_mix, w_in, b_forget, w_branch, w_out, g_mlp, w_up, w_down, g_final):
    for layer in range(DEPTH):
        x = _layer(x, positions, g_mix[layer], w_in[layer], b_forget[layer], w_branch[layer],
                   w_out[layer], g_mlp[layer], w_up[layer], w_down[layer])
    return _rmsnorm(x, g_final)
```

```python
import functools

import jax
import jax.numpy as jnp
from jax import lax
from jax.experimental import pallas as pl
from jax.experimental.pallas import tpu as pltpu

F32 = jnp.float32
MXU_DTYPE = jnp.bfloat16

HEAD_DIM = 64
SB_HEADS = 8
FOX_HEADS = 8
DSA_HEADS = 8
DSA_KV_HEADS = 2
DSA_GROUP = DSA_HEADS // DSA_KV_HEADS
IDX_HEADS = 4
IDX_DIM = 64
N_BRANCH = 3
ROPE_THETA = 500000.0
ROT_DIM = HEAD_DIM // 4
TOPK_MAX = 256
NORM_EPS = 1e-6

LANES = 128
BRANCH_WIDTH = SB_HEADS * HEAD_DIM
ATTN_SCALE = HEAD_DIM ** -0.5
assert IDX_DIM == HEAD_DIM
IDX_W_SCALE = IDX_HEADS ** -0.5

COL_QA, COL_KA, COL_VA = 0, 512, 1024
COL_QB, COL_KB, COL_VB = 1536, 2048, 2560
COL_QC = 3072
COL_KC = 3584
COL_QI = 3840
COL_VC = 4096
COL_KI = 4352
MAIN_COLS = 4480
PROJ_CHUNKS = ((0, 512, False), (512, 1024, False), (1024, 1536, False), (1536, 2048, False),
               (2048, 2560, False), (2560, 3072, False), (3072, 3584, True), (3584, 4096, True),
               (4096, 4352, False), (4352, 4480, True))
MISC_FL, MISC_WI = 0, 8

TQ = 256
TK = 256
NEG_BIG = -1e30

KEY_NEG_INF = -2139095041
KEY_END = 2139095041

VMEM_LIMIT = 56 * 1024 * 1024


def _cparams(sem):
    return pltpu.CompilerParams(dimension_semantics=sem, vmem_limit_bytes=VMEM_LIMIT)


def _rmsnorm(x, g):
    y = x * lax.rsqrt(jnp.mean(x * x, axis=-1, keepdims=True) + NORM_EPS)
    return y * g


def _dot(a, b):
    return jnp.dot(a, b, preferred_element_type=F32)


def _dot_nt(a, b):
    return lax.dot_general(a, b, (((1,), (1,)), ((), ())), preferred_element_type=F32)


def _split2(x):
    hi = x.astype(MXU_DTYPE)
    lo = (x - hi.astype(F32)).astype(MXU_DTYPE)
    return hi, lo


def _split3(x):
    hi = x.astype(MXU_DTYPE)
    r = x - hi.astype(F32)
    mid = r.astype(MXU_DTYPE)
    lo = (r - mid.astype(F32)).astype(MXU_DTYPE)
    return hi, mid, lo


def _proj_kernel(x_ref, pos_ref, g_ref, freq_ref, sg1_ref, sg2_ref, wm_ref, wx_ref,
                 main_ref, misc_ref):
    h = _rmsnorm(x_ref[...], g_ref[...]).astype(MXU_DTYPE)
    ang = pos_ref[...].astype(F32) * freq_ref[...]
    cos = jnp.cos(ang)
    sin = jnp.sin(ang)
    s_up = sin * sg1_ref[...]
    s_dn = sin * sg2_ref[...]
    half = ROT_DIM // 2
    for c0, c1, rope in PROJ_CHUNKS:
        p = _dot(h, wm_ref[:, c0:c1])
        if rope:
            for t in range((c1 - c0) // LANES):
                pt = p[:, t * LANES:(t + 1) * LANES]
                pt = (pt * cos + pltpu.roll(pt, LANES - half, 1) * s_up
                      + pltpu.roll(pt, half, 1) * s_dn)
                main_ref[:, c0 + t * LANES:c0 + (t + 1) * LANES] = pt.astype(main_ref.dtype)
        else:
            main_ref[:, c0:c1] = p.astype(main_ref.dtype)
    misc_ref[...] = _dot(h, wx_ref[...])


def _proj(x2d, pos2d, g, freq, sg1, sg2, w_main, w_misc, tm):
    n, d = x2d.shape
    const = lambda i: (0, 0)
    return pl.pallas_call(
        _proj_kernel,
        grid=(n // tm,),
        in_specs=[
            pl.BlockSpec((tm, d), lambda i: (i, 0)),
            pl.BlockSpec((tm, 1), lambda i: (i, 0)),
            pl.BlockSpec((1, d), const),
            pl.BlockSpec((1, LANES), const),
            pl.BlockSpec((1, LANES), const),
            pl.BlockSpec((1, LANES), const),
            pl.BlockSpec((d, MAIN_COLS), const),
            pl.BlockSpec((d, LANES), const),
        ],
        out_specs=[
            pl.BlockSpec((tm, MAIN_COLS), lambda i: (i, 0)),
            pl.BlockSpec((tm, LANES), lambda i: (i, 0)),
        ],
        out_shape=[
            jax.ShapeDtypeStruct((n, MAIN_COLS), MXU_DTYPE),
            jax.ShapeDtypeStruct((n, LANES), F32),
        ],
        compiler_params=_cparams(("parallel",)),
        name="proj",
    )(x2d, pos2d, g, freq, sg1, sg2, w_main, w_misc)


def _cumf_kernel(misc_ref, bias_ref, c_ref):
    s = misc_ref.shape[1]
    logit = misc_ref[0] + bias_ref[...]
    logf = jnp.minimum(logit, 0.0) - jnp.log(1.0 + jnp.exp(-jnp.abs(logit)))
    logf_t = logf.T
    r = lax.broadcasted_iota(jnp.int32, (TK, TK), 0)
    c = lax.broadcasted_iota(jnp.int32, (TK, TK), 1)
    incl = (r <= c).astype(MXU_DTYPE)
    carry = jnp.zeros((FOX_HEADS, 1), F32)
    for b in range(s // TK):
        seg = logf_t[0:FOX_HEADS, b * TK:(b + 1) * TK]
        hi, mid, lo = _split3(seg)
        cs = (_dot(lo, incl) + _dot(mid, incl)) + _dot(hi, incl) + carry
        c_ref[0, :, b * TK:(b + 1) * TK] = cs
        carry = cs[:, TK - 1:TK]


def _cumf(misc3d, bias_row):
    b, s, _ = misc3d.shape
    return pl.pallas_call(
        _cumf_kernel,
        grid=(b,),
        in_specs=[
            pl.BlockSpec((1, s, LANES), lambda i: (i, 0, 0)),
            pl.BlockSpec((1, LANES), lambda i: (0, 0)),
        ],
        out_specs=pl.BlockSpec((1, FOX_HEADS, s), lambda i: (i, 0, 0)),
        out_shape=jax.ShapeDtypeStruct((b, FOX_HEADS, s), F32),
        compiler_params=_cparams(("parallel",)),
        name="cumf",
    )(misc3d, bias_row)


def _head_q(q_tile, half):
    lane = lax.broadcasted_iota(jnp.int32, q_tile.shape, 1)
    keep = (lane // HEAD_DIM) == half
    return jnp.where(keep, q_tile.astype(F32) * ATTN_SCALE, 0.0).astype(MXU_DTYPE)


def _causal_tile(strict):
    r = lax.broadcasted_iota(jnp.int32, (TQ, TK), 0)
    c = lax.broadcasted_iota(jnp.int32, (TQ, TK), 1)
    return (c < r) if strict else (c <= r)


def _flash_head(qh, k_at, v_at, bias_at, n_full, diag_mask):
    def step(j, carry, mask):
        m, l, acc = carry
        s = _dot_nt(qh, k_at(j)) + bias_at(j)
        if mask is not None:
            s = jnp.where(mask, s, NEG_BIG)
        m_new = jnp.maximum(m, jnp.max(s, axis=1, keepdims=True))
        p = jnp.exp(s - m_new)
        alpha = jnp.exp(m - m_new)
        l = alpha * l + jnp.sum(p, axis=1, keepdims=True)
        acc = alpha * acc + _dot(p.astype(MXU_DTYPE), v_at(j))
        return m_new, l, acc

    init = (jnp.full((TQ, 1), NEG_BIG, F32), jnp.zeros((TQ, 1), F32), jnp.zeros((TQ, LANES), F32))
    carry = lax.fori_loop(0, n_full, lambda j, c: step(j, c, None), init)
    _, l, acc = step(n_full, carry, diag_mask)
    return acc / l


def _merge_halves(o_even, o_odd):
    lane = lax.broadcasted_iota(jnp.int32, o_even.shape, 1)
    return jnp.where(lane < HEAD_DIM, o_even, o_odd)


def _sb_kernel(q_ref, k_ref, v_ref, o_ref):
    i = pl.program_id(2)
    r = lax.broadcasted_iota(jnp.int32, (TK, TK), 0)
    c = lax.broadcasted_iota(jnp.int32, (TK, TK), 1)
    after = (r > c).astype(MXU_DTYPE)
    after2 = jnp.concatenate([after, after], axis=0)
    strict = _causal_tile(True)

    def head(half):
        qh = _head_q(q_ref[0], half)

        def step(j, carry, mask):
            acc, tail = carry
            kj = k_ref[0, pl.ds(pl.multiple_of(j * TK, TK), TK), :]
            vj = v_ref[0, pl.ds(pl.multiple_of(j * TK, TK), TK), :]
            z = _dot_nt(qh, kj)
            sp = jnp.log(1.0 + jnp.exp(-jnp.abs(z)))
            log_beta = jnp.minimum(z, 0.0) - sp
            neg_l1m = jnp.maximum(z, 0.0) + sp
            if mask is not None:
                neg_l1m = jnp.where(mask, neg_l1m, 0.0)
            hi, lo = _split2(neg_l1m)
            later = _dot(jnp.concatenate([hi, lo], axis=1), after2) + tail
            w = jnp.exp(log_beta - later)
            if mask is not None:
                w = jnp.where(mask, w, 0.0)
            acc = acc + _dot(w.astype(MXU_DTYPE), vj)
            tail = tail + jnp.sum(neg_l1m, axis=1, keepdims=True)
            return acc, tail

        carry = (jnp.zeros((TQ, LANES), F32), jnp.zeros((TQ, 1), F32))
        carry = step(i, carry, strict)
        acc, _ = lax.fori_loop(0, i, lambda t, cr: step(i - 1 - t, cr, None), carry)
        return acc

    o_ref[0] = _merge_halves(head(0), head(1)).astype(o_ref.dtype)


def _sb(main3d):
    b, s, _ = main3d.shape
    nq = s // TQ
    qb, kb, vb = COL_QA // LANES, COL_KA // LANES, COL_VA // LANES
    return pl.pallas_call(
        _sb_kernel,
        grid=(b, SB_HEADS // 2, nq),
        in_specs=[
            pl.BlockSpec((1, TQ, LANES), lambda bi, hp, i: (bi, i, qb + hp)),
            pl.BlockSpec((1, s, LANES), lambda bi, hp, i: (bi, 0, kb + hp)),
            pl.BlockSpec((1, s, LANES), lambda bi, hp, i: (bi, 0, vb + hp)),
        ],
        out_specs=pl.BlockSpec((1, TQ, LANES), lambda bi, hp, i: (bi, i, hp)),
        out_shape=jax.ShapeDtypeStruct((b, s, BRANCH_WIDTH), MXU_DTYPE),
        compiler_params=_cparams(("parallel", "parallel", "arbitrary")),
        name="sb",
    )(main3d, main3d, main3d)


def _fox_kernel(q_ref, k_ref, v_ref, c_ref, o_ref):
    i = pl.program_id(2)
    diag = _causal_tile(False)

    def head(half):
        qh = _head_q(q_ref[0], half)
        k_at = lambda j: k_ref[0, pl.ds(pl.multiple_of(j * TK, TK), TK), :]
        v_at = lambda j: v_ref[0, pl.ds(pl.multiple_of(j * TK, TK), TK), :]
        bias_at = lambda j: -c_ref[0, 0, half:half + 1, pl.ds(pl.multiple_of(j * TK, TK), TK)]
        return _flash_head(qh, k_at, v_at, bias_at, i, diag)

    o_ref[0] = _merge_halves(head(0), head(1)).astype(o_ref.dtype)


def _fox(main3d, c4d):
    b, s, _ = main3d.shape
    nq = s // TQ
    qb, kb, vb = COL_QB // LANES, COL_KB // LANES, COL_VB // LANES
    return pl.pallas_call(
        _fox_kernel,
        grid=(b, FOX_HEADS // 2, nq),
        in_specs=[
            pl.BlockSpec((1, TQ, LANES), lambda bi, hp, i: (bi, i, qb + hp)),
            pl.BlockSpec((1, s, LANES), lambda bi, hp, i: (bi, 0, kb + hp)),
            pl.BlockSpec((1, s, LANES), lambda bi, hp, i: (bi, 0, vb + hp)),
            pl.BlockSpec((1, 1, 2, s), lambda bi, hp, i: (bi, hp, 0, 0)),
        ],
        out_specs=pl.BlockSpec((1, TQ, LANES), lambda bi, hp, i: (bi, i, hp)),
        out_shape=jax.ShapeDtypeStruct((b, s, BRANCH_WIDTH), MXU_DTYPE),
        compiler_params=_cparams(("parallel", "parallel", "arbitrary")),
        name="fox",
    )(main3d, main3d, main3d, c4d)


def _key_to_f32(key):
    bits = key ^ ((key >> 31) & jnp.int32(0x7FFFFFFF))
    return lax.bitcast_convert_type(bits, F32)


def _dsa_kernel(q_ref, k_ref, v_ref, qi_ref, ki_ref, misc_ref, o_ref, score_ref, bias_ref, *, top_k):
    i = pl.program_id(1)
    diag = _causal_tile(False)
    at = lambda j: pl.ds(pl.multiple_of(j * TK, TK), TK)

    wi = misc_ref[0][:, MISC_WI:MISC_WI + IDX_HEADS] * IDX_W_SCALE
    qih = [_head_q(qi_ref[0][:, (h // 2) * LANES:(h // 2 + 1) * LANES], h % 2) for h in range(IDX_HEADS)]

    def score_block(j, mask):
        kij = ki_ref[0, at(j), :]
        sc = None
        for h in range(IDX_HEADS):
            term = wi[:, h:h + 1] * jnp.maximum(_dot_nt(qih[h], kij), 0.0)
            sc = term if sc is None else sc + term
        if mask is not None:
            sc = jnp.where(mask, sc, -jnp.inf)
        score_ref[:, at(j)] = sc

    def score_loop(j, _):
        score_block(j, None)
        return 0

    lax.fori_loop(0, i, score_loop, 0)
    score_block(i, diag)

    def count(pred):
        def body(j, acc):
            return acc + pred(score_ref[:, at(j)], j).astype(jnp.int32)
        acc = lax.fori_loop(0, i + 1, body, jnp.zeros((TQ, TK), jnp.int32))
        return jnp.sum(acc, axis=1, keepdims=True)

    def bisect(_, lh):
        lo, hi = lh
        mid = (lo >> 1) + (hi >> 1) + (lo & hi & 1)
        thr = _key_to_f32(mid)
        ok = count(lambda s, j: s >= thr) >= top_k
        return jnp.where(ok, mid, lo), jnp.where(ok, hi, mid)

    lo0 = jnp.full((TQ, 1), KEY_NEG_INF, jnp.int32)
    hi0 = jnp.full((TQ, 1), KEY_END, jnp.int32)
    lo, _ = lax.fori_loop(0, 32, bisect, (lo0, hi0))
    thr = _key_to_f32(lo)

    need = top_k - count(lambda s, j: s > thr)
    col = lax.broadcasted_iota(jnp.int32, (TQ, TK), 1)

    def bisect_idx(_, lh):
        jl, jh = lh
        mid = (jl + jh) >> 1
        ok = count(lambda s, j: (s == thr) & (col + j * TK <= mid)) >= need
        return jnp.where(ok, jl, mid), jnp.where(ok, mid, jh)

    n_idx_steps = (score_ref.shape[1] - 1).bit_length()
    jl0 = jnp.full((TQ, 1), -1, jnp.int32)
    jh0 = jnp.full((TQ, 1), score_ref.shape[1] - 1, jnp.int32)
    _, last_tie = lax.fori_loop(0, n_idx_steps, bisect_idx, (jl0, jh0))

    def bias_block(j, mask):
        s = score_ref[:, at(j)]
        sel = (s > thr) | ((s == thr) & (col + j * TK <= last_tie))
        if mask is not None:
            sel = sel & mask
        bias_ref[:, at(j)] = jnp.where(sel, 0.0, NEG_BIG)

    def bias_loop(j, _):
        bias_block(j, None)
        return 0

    lax.fori_loop(0, i, bias_loop, 0)
    bias_block(i, diag)

    bias_at = lambda j: bias_ref[:, at(j)]
    for g in range(DSA_KV_HEADS):
        k_at = lambda j, g=g: k_ref[0, at(j), g * LANES:(g + 1) * LANES]
        v_at = lambda j, g=g: v_ref[0, at(j), g * LANES:(g + 1) * LANES]
        for t in range(DSA_GROUP // 2):
            tile = g * (DSA_GROUP // 2) + t
            q_tile = q_ref[0][:, tile * LANES:(tile + 1) * LANES]
            outs = [_flash_head(_head_q(q_tile, half), k_at, v_at, bias_at, i, None) for half in range(2)]
            o_ref[0, :, tile * LANES:(tile + 1) * LANES] = _merge_halves(*outs).astype(o_ref.dtype)


def _dsa(main3d, misc3d, top_k):
    b, s, _ = main3d.shape
    nq = s // TQ
    assert top_k <= TK, "the first key block must hold at least top_k keys"
    return pl.pallas_call(
        functools.partial(_dsa_kernel, top_k=top_k),
        grid=(b, nq),
        in_specs=[
            pl.BlockSpec((1, TQ, 4 * LANES), lambda bi, i: (bi, i, COL_QC // (4 * LANES))),
            pl.BlockSpec((1, s, 2 * LANES), lambda bi, i: (bi, 0, COL_KC // (2 * LANES))),
            pl.BlockSpec((1, s, 2 * LANES), lambda bi, i: (bi, 0, COL_VC // (2 * LANES))),
            pl.BlockSpec((1, TQ, 2 * LANES), lambda bi, i: (bi, i, COL_QI // (2 * LANES))),
            pl.BlockSpec((1, s, LANES), lambda bi, i: (bi, 0, COL_KI // LANES)),
            pl.BlockSpec((1, TQ, LANES), lambda bi, i: (bi, i, 0)),
        ],
        out_specs=pl.BlockSpec((1, TQ, BRANCH_WIDTH), lambda bi, i: (bi, i, 0)),
        out_shape=jax.ShapeDtypeStruct((b, s, BRANCH_WIDTH), MXU_DTYPE),
        scratch_shapes=[pltpu.VMEM((TQ, s), F32), pltpu.VMEM((TQ, s), F32)],
        compiler_params=_cparams(("parallel", "arbitrary")),
        name="dsa",
    )(main3d, main3d, main3d, main3d, main3d, misc3d)


def _merge_kernel(x_ref, a_ref, b_ref, c_ref, g_ref, wg_ref, wb_ref, wo_ref, o_ref):
    x = x_ref[...]
    d = x.shape[1]
    h = _rmsnorm(x, g_ref[...]).astype(MXU_DTYPE)
    y = None
    for n, br_ref in enumerate((a_ref, b_ref, c_ref)):
        gate = jax.nn.sigmoid(_dot(h, wg_ref[:, n * d:(n + 1) * d]))
        term = gate * _dot(br_ref[...], wb_ref[n])
        y = term if y is None else y + term
    o_ref[...] = x + _dot(y.astype(MXU_DTYPE), wo_ref[...])


def _merge(x2d, br_a, br_b, br_c, g, w_gate, w_branch, w_out, tm):
    n, d = x2d.shape
    bw = br_a.shape[1]
    row = lambda i: (i, 0)
    const = lambda i: (0, 0)
    return pl.pallas_call(
        _merge_kernel,
        grid=(n // tm,),
        in_specs=[
            pl.BlockSpec((tm, d), row),
            pl.BlockSpec((tm, bw), row),
            pl.BlockSpec((tm, bw), row),
            pl.BlockSpec((tm, bw), row),
            pl.BlockSpec((1, d), const),
            pl.BlockSpec((d, N_BRANCH * d), const),
            pl.BlockSpec((N_BRANCH, bw, d), lambda i: (0, 0, 0)),
            pl.BlockSpec((d, d), const),
        ],
        out_specs=pl.BlockSpec((tm, d), row),
        out_shape=jax.ShapeDtypeStruct((n, d), F32),
        compiler_params=_cparams(("parallel",)),
        name="merge",
    )(x2d, br_a, br_b, br_c, g, w_gate, w_branch, w_out)


def _mlp_kernel(x_ref, g_ref, wu_ref, wd_ref, gf_ref, o_ref, *, ff_chunk, final_norm):
    x = x_ref[...]
    h = _rmsnorm(x, g_ref[...]).astype(MXU_DTYPE)
    acc = x
    for c in range(wu_ref.shape[1] // ff_chunk):
        u = jnp.maximum(_dot(h, wu_ref[:, c * ff_chunk:(c + 1) * ff_chunk]), 0.0)
        acc = acc + _dot((u * u).astype(MXU_DTYPE), wd_ref[c * ff_chunk:(c + 1) * ff_chunk, :])
    if final_norm:
        acc = _rmsnorm(acc, gf_ref[...])
    o_ref[...] = acc


def _mlp(x2d, g, w_up, w_down, g_final, tm, final_norm):
    n, d = x2d.shape
    ff = w_up.shape[1]
    row = lambda i: (i, 0)
    const = lambda i: (0, 0)
    return pl.pallas_call(
        functools.partial(_mlp_kernel, ff_chunk=min(ff, 1024), final_norm=final_norm),
        grid=(n // tm,),
        in_specs=[
            pl.BlockSpec((tm, d), row),
            pl.BlockSpec((1, d), const),
            pl.BlockSpec((d, ff), const),
            pl.BlockSpec((ff, d), const),
            pl.BlockSpec((1, d), const),
        ],
        out_specs=pl.BlockSpec((tm, d), row),
        out_shape=jax.ShapeDtypeStruct((n, d), F32),
        compiler_params=_cparams(("parallel",)),
        name="mlp",
    )(x2d, g, w_up, w_down, g_final)


def _pack_w_in(w):
    d = w.shape[0]
    o = 6 * BRANCH_WIDTH
    fl = w[:, o:o + FOX_HEADS]; o += FOX_HEADS
    qc = w[:, o:o + DSA_HEADS * HEAD_DIM]; o += DSA_HEADS * HEAD_DIM
    kc = w[:, o:o + DSA_KV_HEADS * HEAD_DIM]; o += DSA_KV_HEADS * HEAD_DIM
    vc = w[:, o:o + DSA_KV_HEADS * HEAD_DIM]; o += DSA_KV_HEADS * HEAD_DIM
    qi = w[:, o:o + IDX_HEADS * IDX_DIM]; o += IDX_HEADS * IDX_DIM
    ki = w[:, o:o + IDX_DIM]; o += IDX_DIM
    wi = w[:, o:o + IDX_HEADS]; o += IDX_HEADS
    gates = w[:, o:]

    def dup_heads(a):
        parts = []
        for g in range(a.shape[1] // HEAD_DIM):
            hd = a[:, g * HEAD_DIM:(g + 1) * HEAD_DIM]
            parts += [hd, hd]
        return jnp.concatenate(parts, axis=1)

    main = jnp.concatenate([w[:, :6 * BRANCH_WIDTH], qc, dup_heads(kc), qi, dup_heads(vc), dup_heads(ki)],
                           axis=1)
    misc = jnp.concatenate([fl, wi, jnp.zeros((d, LANES - FOX_HEADS - IDX_HEADS), w.dtype)], axis=1)
    return main.astype(MXU_DTYPE), misc.astype(MXU_DTYPE), gates.astype(MXU_DTYPE)


def _rope_rows():
    lane = jnp.arange(LANES) % HEAD_DIM
    half = ROT_DIM // 2
    inv_freq = ROPE_THETA ** (-jnp.arange(0, ROT_DIM, 2, dtype=F32) / ROT_DIM)
    freq = jnp.where(lane < ROT_DIM, inv_freq[lane % half], 0.0).astype(F32)
    sg1 = jnp.where(lane < half, -1.0, 0.0).astype(F32)
    sg2 = jnp.where((lane >= half) & (lane < ROT_DIM), 1.0, 0.0).astype(F32)
    return freq[None, :], sg1[None, :], sg2[None, :]


def kernel(x, positions, g_mix, w_in, b_forget, w_branch, w_out, g_mlp, w_up, w_down, g_final):
    b, s, d = x.shape
    n = b * s
    depth = w_in.shape[0]
    top_k = min(TOPK_MAX, s // 4)
    tm = min(512, n)
    freq, sg1, sg2 = _rope_rows()
    pos2d = positions.reshape(n, 1)
    x2d = x.reshape(n, d)
    for layer in range(depth):
        w_main, w_misc, w_gate = _pack_w_in(w_in[layer])
        bias_row = jnp.concatenate([b_forget[layer], jnp.zeros((LANES - FOX_HEADS,), F32)])[None, :]
        main, misc = _proj(x2d, pos2d, g_mix[layer][None, :], freq, sg1, sg2, w_main, w_misc, tm)
        main3d = main.reshape(b, s, MAIN_COLS)
        misc3d = misc.reshape(b, s, LANES)
        cum = _cumf(misc3d, bias_row).reshape(b, FOX_HEADS // 2, 2, s)
        br_a = _sb(main3d).reshape(n, BRANCH_WIDTH)
        br_b = _fox(main3d, cum).reshape(n, BRANCH_WIDTH)
        br_c = _dsa(main3d, misc3d, top_k).reshape(n, BRANCH_WIDTH)
        x2d = _merge(x2d, br_a, br_b, br_c, g_mix[layer][None, :], w_gate,
                     w_branch[layer].astype(MXU_DTYPE), w_out[layer].astype(MXU_DTYPE), tm)
        x2d = _mlp(x2d, g_mlp[layer][None, :], w_up[layer].astype(MXU_DTYPE),
                   w_down[layer].astype(MXU_DTYPE), g_final[None, :], tm, layer == depth - 1)
    return x2d.reshape(b, s, d)
```

```python
import functools

import jax
import jax.numpy as jnp
from jax import lax
from jax.experimental import pallas as pl
from jax.experimental.pallas import tpu as pltpu

F32 = jnp.float32
MXU_DTYPE = jnp.bfloat16

HEAD_DIM = 64
SB_HEADS = 8
FOX_HEADS = 8
DSA_HEADS = 8
DSA_KV_HEADS = 2
DSA_GROUP = DSA_HEADS // DSA_KV_HEADS
IDX_HEADS = 4
IDX_DIM = 64
N_BRANCH = 3
ROPE_THETA = 500000.0
ROT_DIM = HEAD_DIM // 4
TOPK_MAX = 256
NORM_EPS = 1e-6

LANES = 128
SUBLANES = 8
BRANCH_WIDTH = SB_HEADS * HEAD_DIM
ATTN_SCALE = HEAD_DIM ** -0.5
assert IDX_DIM == HEAD_DIM
IDX_W_SCALE = IDX_HEADS ** -0.5

COL_QA, COL_KA = 0, 512
COL_QB, COL_KB = 1024, 1536
COL_QC = 2048
COL_KC = 2560
COL_QI = 2816
COL_KI = 3072
MAIN_COLS = 3200
ROPE_START = COL_QC
ROW_VA, ROW_VB, ROW_VC = 0, 512, 1024
VT_ROWS = 1152
MISC_FL, MISC_WI = 0, 8

TQ = 256
TK = 256
HEADS_PER_STEP = 4
NEG_BIG = -1e30

KEY_NEG_INF = -2139095041
KEY_END = 2139095041

VMEM_LIMIT = 56 * 1024 * 1024


def _cparams(sem):
    return pltpu.CompilerParams(dimension_semantics=sem, vmem_limit_bytes=VMEM_LIMIT)


def _rmsnorm(x, g):
    y = x * lax.rsqrt(jnp.mean(x * x, axis=-1, keepdims=True) + NORM_EPS)
    return y * g


def _dot(a, b):
    return jnp.dot(a, b, preferred_element_type=F32)


def _dot_nt(a, b):
    return lax.dot_general(a, b, (((1,), (1,)), ((), ())), preferred_element_type=F32)


def _split2(x):
    hi = x.astype(MXU_DTYPE)
    lo = (x - hi.astype(F32)).astype(MXU_DTYPE)
    return hi, lo


def _split3(x):
    hi = x.astype(MXU_DTYPE)
    r = x - hi.astype(F32)
    mid = r.astype(MXU_DTYPE)
    lo = (r - mid.astype(F32)).astype(MXU_DTYPE)
    return hi, mid, lo


def _log_sigmoid_parts(z):
    sp = jnp.log(1.0 + jnp.exp(-jnp.abs(z)))
    return jnp.minimum(z, 0.0) - sp, jnp.maximum(z, 0.0) + sp


def _proj_kernel(x_ref, pos_ref, g_ref, freq_ref, sg1_ref, sg2_ref, wm_ref, wv_ref, wx_ref,
                 main_ref, vt_ref, misc_ref):
    h = _rmsnorm(x_ref[...], g_ref[...]).astype(MXU_DTYPE)
    ang = pos_ref[...].astype(F32) * freq_ref[...]
    cos = jnp.cos(ang)
    sin = jnp.sin(ang)
    s_up = sin * sg1_ref[...]
    s_dn = sin * sg2_ref[...]
    half = ROT_DIM // 2
    chunk = 4 * LANES
    for c0 in range(0, MAIN_COLS, chunk):
        c1 = min(c0 + chunk, MAIN_COLS)
        p = _dot(h, wm_ref[:, c0:c1])
        if c0 >= ROPE_START:
            for t in range((c1 - c0) // LANES):
                pt = p[:, t * LANES:(t + 1) * LANES]
                pt = (pt * cos + pltpu.roll(pt, LANES - half, 1) * s_up
                      + pltpu.roll(pt, half, 1) * s_dn)
                main_ref[:, c0 + t * LANES:c0 + (t + 1) * LANES] = pt.astype(main_ref.dtype)
        else:
            main_ref[:, c0:c1] = p.astype(main_ref.dtype)
    for r0 in range(0, VT_ROWS, LANES):
        pv = _dot(h, wv_ref[:, r0:r0 + LANES])
        vt_ref[0, r0:r0 + LANES, :] = pv.T.astype(vt_ref.dtype)
    misc_ref[...] = _dot(h, wx_ref[...])


def _proj(x2d, pos2d, g, freq, sg1, sg2, w_main, w_v, w_misc, tm, batch, seq):
    n, d = x2d.shape
    nsb = seq // tm
    const = lambda i: (0, 0)
    return pl.pallas_call(
        _proj_kernel,
        grid=(n // tm,),
        in_specs=[
            pl.BlockSpec((tm, d), lambda i: (i, 0)),
            pl.BlockSpec((tm, 1), lambda i: (i, 0)),
            pl.BlockSpec((1, d), const),
            pl.BlockSpec((1, LANES), const),
            pl.BlockSpec((1, LANES), const),
            pl.BlockSpec((1, LANES), const),
            pl.BlockSpec((d, MAIN_COLS), const),
            pl.BlockSpec((d, VT_ROWS), const),
            pl.BlockSpec((d, LANES), const),
        ],
        out_specs=[
            pl.BlockSpec((tm, MAIN_COLS), lambda i: (i, 0)),
            pl.BlockSpec((1, VT_ROWS, tm), lambda i: (i // nsb, 0, i % nsb)),
            pl.BlockSpec((tm, LANES), lambda i: (i, 0)),
        ],
        out_shape=[
            jax.ShapeDtypeStruct((n, MAIN_COLS), MXU_DTYPE),
            jax.ShapeDtypeStruct((batch, VT_ROWS, seq), MXU_DTYPE),
            jax.ShapeDtypeStruct((n, LANES), F32),
        ],
        compiler_params=_cparams(("parallel",)),
        name="proj",
    )(x2d, pos2d, g, freq, sg1, sg2, w_main, w_v, w_misc)


def _cumf_kernel(misc_ref, bias_ref, c_ref):
    s = misc_ref.shape[1]
    r = lax.broadcasted_iota(jnp.int32, (TK, TK), 0)
    c = lax.broadcasted_iota(jnp.int32, (TK, TK), 1)
    incl = (c <= r).astype(MXU_DTYPE)
    carry = jnp.zeros((1, LANES), F32)
    for b in range(s // TK):
        logit = misc_ref[0, b * TK:(b + 1) * TK, :] + bias_ref[...]
        logf, _ = _log_sigmoid_parts(logit)
        hi, mid, lo = _split3(logf)
        cs = (_dot(incl, lo) + _dot(incl, mid)) + _dot(incl, hi) + carry
        c_ref[0, b * TK:(b + 1) * TK, :] = cs
        carry = cs[TK - 1:TK, :]


def _cumf(misc3d, bias_row):
    b, s, _ = misc3d.shape
    return pl.pallas_call(
        _cumf_kernel,
        grid=(b,),
        in_specs=[
            pl.BlockSpec((1, s, LANES), lambda i: (i, 0, 0)),
            pl.BlockSpec((1, LANES), lambda i: (0, 0)),
        ],
        out_specs=pl.BlockSpec((1, s, LANES), lambda i: (i, 0, 0)),
        out_shape=jax.ShapeDtypeStruct((b, s, LANES), F32),
        compiler_params=_cparams(("parallel",)),
        name="cumf",
    )(misc3d, bias_row)


def _head_q(q_tile, half):
    lane = lax.broadcasted_iota(jnp.int32, q_tile.shape, 1)
    keep = (lane // HEAD_DIM) == half
    return jnp.where(keep, q_tile.astype(F32) * ATTN_SCALE, 0.0).astype(MXU_DTYPE)


def _causal_tile_t(strict):
    key = lax.broadcasted_iota(jnp.int32, (TK, TQ), 0)
    qry = lax.broadcasted_iota(jnp.int32, (TK, TQ), 1)
    return (key < qry) if strict else (key <= qry)


def _rows(j):
    return pl.ds(pl.multiple_of(j * TK, TK), TK)


def _flash_heads(heads, n_full, diag_mask):
    def add_bias(s, bias):
        if isinstance(bias, (list, tuple)):
            return jnp.concatenate([s[:, t * LANES:(t + 1) * LANES] + bias[t] for t in range(len(bias))],
                                   axis=1)
        return s + bias

    def step(j, carry, mask):
        zs = [_dot_nt(k_at(j), qh) for (qh, k_at, _, _) in heads]
        stats = []
        for z, (_, _, _, bias_at), (m, l, _) in zip(zs, heads, carry):
            s = add_bias(z, bias_at(j))
            if mask is not None:
                s = jnp.where(mask, s, NEG_BIG)
            m_new = jnp.maximum(m, jnp.max(s, axis=0, keepdims=True))
            p = jnp.exp(s - m_new)
            alpha = jnp.exp(m - m_new)
            l = alpha * l + jnp.sum(p, axis=0, keepdims=True)
            stats.append((m_new, l, alpha, p.astype(MXU_DTYPE)))
        out = []
        for (m_new, l, alpha, p), (_, _, vt_at, _), (_, _, acc) in zip(stats, heads, carry):
            out.append((m_new, l, alpha * acc + _dot(vt_at(j), p)))
        return tuple(out)

    init = tuple((jnp.full((1, TQ), NEG_BIG, F32), jnp.zeros((1, TQ), F32), jnp.zeros((HEAD_DIM, TQ), F32))
                 for _ in heads)
    carry = lax.fori_loop(0, n_full, lambda j, c: step(j, c, None), init)
    carry = step(n_full, carry, diag_mask)
    return [acc / l for (_, l, acc) in carry]


def _store_heads_t(o_ref, outs_t):
    for t in range(len(outs_t) // 2):
        pair_t = jnp.concatenate([outs_t[2 * t], outs_t[2 * t + 1]], axis=0)
        o_ref[0, :, t * LANES:(t + 1) * LANES] = pair_t.T.astype(o_ref.dtype)


def _sb_kernel(q_ref, k_ref, vt_ref, o_ref):
    i = pl.program_id(2)
    r = lax.broadcasted_iota(jnp.int32, (TK, TK), 0)
    c = lax.broadcasted_iota(jnp.int32, (TK, TK), 1)
    after = (c > r).astype(MXU_DTYPE)
    after2 = jnp.concatenate([after, after], axis=1)
    strict = _causal_tile_t(True)
    n_heads = HEADS_PER_STEP
    qhs = [_head_q(q_ref[0][:, (h // 2) * LANES:(h // 2 + 1) * LANES], h % 2) for h in range(n_heads)]

    def step(j, carry, mask):
        zs = [_dot_nt(k_ref[0, _rows(j), (h // 2) * LANES:(h // 2 + 1) * LANES], qhs[h])
              for h in range(n_heads)]
        parts = []
        for z in zs:
            log_beta, neg_l1m = _log_sigmoid_parts(z)
            if mask is not None:
                neg_l1m = jnp.where(mask, neg_l1m, 0.0)
            hi, lo = _split2(neg_l1m)
            parts.append((log_beta, jnp.sum(neg_l1m, axis=0, keepdims=True), jnp.concatenate([hi, lo], axis=0)))
        laters = [_dot(after2, hl) for (_, _, hl) in parts]
        ws = []
        for (log_beta, _, _), later, (_, tail) in zip(parts, laters, carry):
            w = jnp.exp(log_beta - (later + tail))
            if mask is not None:
                w = jnp.where(mask, w, 0.0)
            ws.append(w.astype(MXU_DTYPE))
        out = []
        for h, (w, (_, blk_sum, _), (acc, tail)) in enumerate(zip(ws, parts, carry)):
            vtj = vt_ref[0, h * HEAD_DIM:(h + 1) * HEAD_DIM, _rows(j)]
            out.append((acc + _dot(vtj, w), tail + blk_sum))
        return tuple(out)

    carry = tuple((jnp.zeros((HEAD_DIM, TQ), F32), jnp.zeros((1, TQ), F32)) for _ in range(n_heads))
    carry = step(i, carry, strict)
    carry = lax.fori_loop(0, i, lambda t, cr: step(i - 1 - t, cr, None), carry)
    _store_heads_t(o_ref, [acc for acc, _ in carry])


def _sb(main3d, vt3d):
    b, s, _ = main3d.shape
    nq = s // TQ
    w = HEADS_PER_STEP * HEAD_DIM
    qb, kb, vb = COL_QA // w, COL_KA // w, ROW_VA // w
    return pl.pallas_call(
        _sb_kernel,
        grid=(b, SB_HEADS // HEADS_PER_STEP, nq),
        in_specs=[
            pl.BlockSpec((1, TQ, w), lambda bi, hg, i: (bi, i, qb + hg)),
            pl.BlockSpec((1, s, w), lambda bi, hg, i: (bi, 0, kb + hg)),
            pl.BlockSpec((1, w, s), lambda bi, hg, i: (bi, vb + hg, 0)),
        ],
        out_specs=pl.BlockSpec((1, TQ, w), lambda bi, hg, i: (bi, i, hg)),
        out_shape=jax.ShapeDtypeStruct((b, s, BRANCH_WIDTH), MXU_DTYPE),
        compiler_params=_cparams(("parallel", "parallel", "arbitrary")),
        name="sb",
    )(main3d, main3d, vt3d)


def _fox_kernel(q_ref, k_ref, vt_ref, c_ref, o_ref, cb_ref):
    hg = pl.program_id(1)
    i = pl.program_id(2)
    n_heads = HEADS_PER_STEP

    @pl.when(i == 0)
    def _():
        c = c_ref[0]
        lane = lax.broadcasted_iota(jnp.int32, c.shape, 1)
        for h in range(n_heads):
            col = jnp.sum(jnp.where(lane == hg * n_heads + h, c, 0.0), axis=1, keepdims=True)
            cb_ref[h] = jnp.broadcast_to(-col, c.shape)

    heads = []
    for h in range(n_heads):
        t = h // 2
        qh = _head_q(q_ref[0][:, t * LANES:(t + 1) * LANES], h % 2)
        k_at = lambda j, t=t: k_ref[0, _rows(j), t * LANES:(t + 1) * LANES]
        vt_at = lambda j, h=h: vt_ref[0, h * HEAD_DIM:(h + 1) * HEAD_DIM, _rows(j)]
        bias_at = lambda j, h=h: [cb_ref[h, _rows(j), :]] * (TQ // LANES)
        heads.append((qh, k_at, vt_at, bias_at))
    _store_heads_t(o_ref, _flash_heads(heads, i, _causal_tile_t(False)))


def _fox(main3d, vt3d, c3d):
    b, s, _ = main3d.shape
    nq = s // TQ
    w = HEADS_PER_STEP * HEAD_DIM
    qb, kb, vb = COL_QB // w, COL_KB // w, ROW_VB // w
    return pl.pallas_call(
        _fox_kernel,
        grid=(b, FOX_HEADS // HEADS_PER_STEP, nq),
        in_specs=[
            pl.BlockSpec((1, TQ, w), lambda bi, hg, i: (bi, i, qb + hg)),
            pl.BlockSpec((1, s, w), lambda bi, hg, i: (bi, 0, kb + hg)),
            pl.BlockSpec((1, w, s), lambda bi, hg, i: (bi, vb + hg, 0)),
            pl.BlockSpec((1, s, LANES), lambda bi, hg, i: (bi, 0, 0)),
        ],
        out_specs=pl.BlockSpec((1, TQ, w), lambda bi, hg, i: (bi, i, hg)),
        out_shape=jax.ShapeDtypeStruct((b, s, BRANCH_WIDTH), MXU_DTYPE),
        scratch_shapes=[pltpu.VMEM((HEADS_PER_STEP, s, LANES), F32)],
        compiler_params=_cparams(("parallel", "parallel", "arbitrary")),
        name="fox",
    )(main3d, main3d, vt3d, c3d)


def _key_to_f32(key):
    bits = key ^ ((key >> 31) & jnp.int32(0x7FFFFFFF))
    return lax.bitcast_convert_type(bits, F32)


def _dsa_kernel(q_ref, k_ref, vt_ref, qi_ref, ki_ref, misc_ref, o_ref, score_ref, bias_ref, *, top_k):
    i = pl.program_id(1)
    diag = _causal_tile_t(False)
    seq = score_ref.shape[0]

    wi_t = misc_ref[0].T[MISC_WI:MISC_WI + IDX_HEADS, :] * IDX_W_SCALE
    qih = [_head_q(qi_ref[0][:, (h // 2) * LANES:(h // 2 + 1) * LANES], h % 2) for h in range(IDX_HEADS)]

    def score_block(j, mask):
        kij = ki_ref[0, _rows(j), :]
        sc = None
        for h in range(IDX_HEADS):
            term = wi_t[h:h + 1, :] * jnp.maximum(_dot_nt(kij, qih[h]), 0.0)
            sc = term if sc is None else sc + term
        if mask is not None:
            sc = jnp.where(mask, sc, -jnp.inf)
        score_ref[_rows(j), :] = sc

    def score_loop(j, _):
        score_block(j, None)
        return 0

    lax.fori_loop(0, i, score_loop, 0)
    score_block(i, diag)

    def count(pred):
        def body(j, acc):
            hit = pred(score_ref[_rows(j), :], j).astype(jnp.int32)
            return acc + jnp.sum(hit.reshape(TK // SUBLANES, SUBLANES, TQ), axis=0)
        acc = lax.fori_loop(0, i + 1, body, jnp.zeros((SUBLANES, TQ), jnp.int32))
        return jnp.sum(acc, axis=0, keepdims=True)

    def bisect(_, lh):
        lo, hi = lh
        mid = (lo >> 1) + (hi >> 1) + (lo & hi & 1)
        thr = _key_to_f32(mid)
        ok = count(lambda s, j: s >= thr) >= top_k
        return jnp.where(ok, mid, lo), jnp.where(ok, hi, mid)

    lo0 = jnp.full((1, TQ), KEY_NEG_INF, jnp.int32)
    hi0 = jnp.full((1, TQ), KEY_END, jnp.int32)
    lo, _ = lax.fori_loop(0, 32, bisect, (lo0, hi0))
    thr = _key_to_f32(lo)

    need = top_k - count(lambda s, j: s > thr)
    key_idx = lax.broadcasted_iota(jnp.int32, (TK, TQ), 0)

    def bisect_idx(_, lh):
        jl, jh = lh
        mid = (jl + jh) >> 1
        ok = count(lambda s, j: (s == thr) & (key_idx + j * TK <= mid)) >= need
        return jnp.where(ok, jl, mid), jnp.where(ok, mid, jh)

    n_idx_steps = (seq - 1).bit_length()
    jl0 = jnp.full((1, TQ), -1, jnp.int32)
    jh0 = jnp.full((1, TQ), seq - 1, jnp.int32)
    _, last_tie = lax.fori_loop(0, n_idx_steps, bisect_idx, (jl0, jh0))

    def bias_block(j, mask):
        s = score_ref[_rows(j), :]
        sel = (s > thr) | ((s == thr) & (key_idx + j * TK <= last_tie))
        if mask is not None:
            sel = sel & mask
        bias_ref[_rows(j), :] = jnp.where(sel, 0.0, NEG_BIG)

    def bias_loop(j, _):
        bias_block(j, None)
        return 0

    lax.fori_loop(0, i, bias_loop, 0)
    bias_block(i, diag)

    bias_at = lambda j: bias_ref[_rows(j), :]
    outs = []
    for g in range(DSA_KV_HEADS):
        k_at = lambda j, g=g: k_ref[0, _rows(j), g * LANES:(g + 1) * LANES]
        vt_at = lambda j, g=g: vt_ref[0, g * HEAD_DIM:(g + 1) * HEAD_DIM, _rows(j)]
        heads = []
        for r in range(DSA_GROUP):
            h = g * DSA_GROUP + r
            qh = _head_q(q_ref[0][:, (h // 2) * LANES:(h // 2 + 1) * LANES], h % 2)
            heads.append((qh, k_at, vt_at, bias_at))
        outs += _flash_heads(heads, i, None)
    _store_heads_t(o_ref, outs)


def _dsa(main3d, vt3d, misc3d, top_k):
    b, s, _ = main3d.shape
    nq = s // TQ
    assert top_k <= TK, "the first key block must hold at least top_k keys"
    return pl.pallas_call(
        functools.partial(_dsa_kernel, top_k=top_k),
        grid=(b, nq),
        in_specs=[
            pl.BlockSpec((1, TQ, 4 * LANES), lambda bi, i: (bi, i, COL_QC // (4 * LANES))),
            pl.BlockSpec((1, s, 2 * LANES), lambda bi, i: (bi, 0, COL_KC // (2 * LANES))),
            pl.BlockSpec((1, LANES, s), lambda bi, i: (bi, ROW_VC // LANES, 0)),
            pl.BlockSpec((1, TQ, 2 * LANES), lambda bi, i: (bi, i, COL_QI // (2 * LANES))),
            pl.BlockSpec((1, s, LANES), lambda bi, i: (bi, 0, COL_KI // LANES)),
            pl.BlockSpec((1, TQ, LANES), lambda bi, i: (bi, i, 0)),
        ],
        out_specs=pl.BlockSpec((1, TQ, BRANCH_WIDTH), lambda bi, i: (bi, i, 0)),
        out_shape=jax.ShapeDtypeStruct((b, s, BRANCH_WIDTH), MXU_DTYPE),
        scratch_shapes=[pltpu.VMEM((s, TQ), F32), pltpu.VMEM((s, TQ), F32)],
        compiler_params=_cparams(("parallel", "arbitrary")),
        name="dsa",
    )(main3d, main3d, vt3d, main3d, main3d, misc3d)


def _merge_kernel(x_ref, a_ref, b_ref, c_ref, g_ref, wg_ref, wb_ref, wo_ref, o_ref):
    x = x_ref[...]
    d = x.shape[1]
    h = _rmsnorm(x, g_ref[...]).astype(MXU_DTYPE)
    y = None
    for n, br_ref in enumerate((a_ref, b_ref, c_ref)):
        gate = jax.nn.sigmoid(_dot(h, wg_ref[:, n * d:(n + 1) * d]))
        term = gate * _dot(br_ref[...], wb_ref[n])
        y = term if y is None else y + term
    o_ref[...] = x + _dot(y.astype(MXU_DTYPE), wo_ref[...])


def _merge(x2d, br_a, br_b, br_c, g, w_gate, w_branch, w_out, tm):
    n, d = x2d.shape
    bw = br_a.shape[1]
    row = lambda i: (i, 0)
    const = lambda i: (0, 0)
    return pl.pallas_call(
        _merge_kernel,
        grid=(n // tm,),
        in_specs=[
            pl.BlockSpec((tm, d), row),
            pl.BlockSpec((tm, bw), row),
            pl.BlockSpec((tm, bw), row),
            pl.BlockSpec((tm, bw), row),
            pl.BlockSpec((1, d), const),
            pl.BlockSpec((d, N_BRANCH * d), const),
            pl.BlockSpec((N_BRANCH, bw, d), lambda i: (0, 0, 0)),
            pl.BlockSpec((d, d), const),
        ],
        out_specs=pl.BlockSpec((tm, d), row),
        out_shape=jax.ShapeDtypeStruct((n, d), F32),
        compiler_params=_cparams(("parallel",)),
        name="merge",
    )(x2d, br_a, br_b, br_c, g, w_gate, w_branch, w_out)


def _mlp_kernel(x_ref, g_ref, wu_ref, wd_ref, gf_ref, o_ref, *, ff_chunk, final_norm):
    x = x_ref[...]
    h = _rmsnorm(x, g_ref[...]).astype(MXU_DTYPE)
    acc = x
    for c in range(wu_ref.shape[1] // ff_chunk):
        u = jnp.maximum(_dot(h, wu_ref[:, c * ff_chunk:(c + 1) * ff_chunk]), 0.0)
        acc = acc + _dot((u * u).astype(MXU_DTYPE), wd_ref[c * ff_chunk:(c + 1) * ff_chunk, :])
    if final_norm:
        acc = _rmsnorm(acc, gf_ref[...])
    o_ref[...] = acc


def _mlp(x2d, g, w_up, w_down, g_final, tm, final_norm):
    n, d = x2d.shape
    ff = w_up.shape[1]
    row = lambda i: (i, 0)
    const = lambda i: (0, 0)
    return pl.pallas_call(
        functools.partial(_mlp_kernel, ff_chunk=min(ff, 1024), final_norm=final_norm),
        grid=(n // tm,),
        in_specs=[
            pl.BlockSpec((tm, d), row),
            pl.BlockSpec((1, d), const),
            pl.BlockSpec((d, ff), const),
            pl.BlockSpec((ff, d), const),
            pl.BlockSpec((1, d), const),
        ],
        out_specs=pl.BlockSpec((tm, d), row),
        out_shape=jax.ShapeDtypeStruct((n, d), F32),
        compiler_params=_cparams(("parallel",)),
        name="mlp",
    )(x2d, g, w_up, w_down, g_final)


def _pack_w_in(w):
    d = w.shape[0]
    bw = BRANCH_WIDTH
    qa, ka, va = w[:, 0:bw], w[:, bw:2 * bw], w[:, 2 * bw:3 * bw]
    qb, kb, vb = w[:, 3 * bw:4 * bw], w[:, 4 * bw:5 * bw], w[:, 5 * bw:6 * bw]
    o = 6 * bw
    fl = w[:, o:o + FOX_HEADS]; o += FOX_HEADS
    qc = w[:, o:o + DSA_HEADS * HEAD_DIM]; o += DSA_HEADS * HEAD_DIM
    kc = w[:, o:o + DSA_KV_HEADS * HEAD_DIM]; o += DSA_KV_HEADS * HEAD_DIM
    vc = w[:, o:o + DSA_KV_HEADS * HEAD_DIM]; o += DSA_KV_HEADS * HEAD_DIM
    qi = w[:, o:o + IDX_HEADS * IDX_DIM]; o += IDX_HEADS * IDX_DIM
    ki = w[:, o:o + IDX_DIM]; o += IDX_DIM
    wi = w[:, o:o + IDX_HEADS]; o += IDX_HEADS
    gates = w[:, o:]

    def dup_heads(a):
        parts = []
        for g in range(a.shape[1] // HEAD_DIM):
            hd = a[:, g * HEAD_DIM:(g + 1) * HEAD_DIM]
            parts += [hd, hd]
        return jnp.concatenate(parts, axis=1)

    main = jnp.concatenate([qa, ka, qb, kb, qc, dup_heads(kc), qi, dup_heads(ki)], axis=1)
    vals = jnp.concatenate([va, vb, vc], axis=1)
    misc = jnp.concatenate([fl, wi, jnp.zeros((d, LANES - FOX_HEADS - IDX_HEADS), w.dtype)], axis=1)
    return (main.astype(MXU_DTYPE), vals.astype(MXU_DTYPE), misc.astype(MXU_DTYPE),
            gates.astype(MXU_DTYPE))


def _rope_rows():
    lane = jnp.arange(LANES) % HEAD_DIM
    half = ROT_DIM // 2
    inv_freq = ROPE_THETA ** (-jnp.arange(0, ROT_DIM, 2, dtype=F32) / ROT_DIM)
    freq = jnp.where(lane < ROT_DIM, inv_freq[lane % half], 0.0).astype(F32)
    sg1 = jnp.where(lane < half, -1.0, 0.0).astype(F32)
    sg2 = jnp.where((lane >= half) & (lane < ROT_DIM), 1.0, 0.0).astype(F32)
    return freq[None, :], sg1[None, :], sg2[None, :]


def kernel(x, positions, g_mix, w_in, b_forget, w_branch, w_out, g_mlp, w_up, w_down, g_final):
    b, s, d = x.shape
    n = b * s
    depth = w_in.shape[0]
    top_k = min(TOPK_MAX, s // 4)
    tm = min(512, s)
    freq, sg1, sg2 = _rope_rows()
    pos2d = positions.reshape(n, 1)
    x2d = x.reshape(n, d)
    for layer in range(depth):
        w_main, w_v, w_misc, w_gate = _pack_w_in(w_in[layer])
        bias_row = jnp.concatenate([b_forget[layer], jnp.zeros((LANES - FOX_HEADS,), F32)])[None, :]
        main, vt, misc = _proj(x2d, pos2d, g_mix[layer][None, :], freq, sg1, sg2, w_main, w_v, w_misc,
                               tm, b, s)
        main3d = main.reshape(b, s, MAIN_COLS)
        misc3d = misc.reshape(b, s, LANES)
        cum = _cumf(misc3d, bias_row)
        br_a = _sb(main3d, vt).reshape(n, BRANCH_WIDTH)
        br_b = _fox(main3d, vt, cum).reshape(n, BRANCH_WIDTH)
        br_c = _dsa(main3d, vt, misc3d, top_k).reshape(n, BRANCH_WIDTH)
        x2d = _merge(x2d, br_a, br_b, br_c, g_mix[layer][None, :], w_gate,
                     w_branch[layer].astype(MXU_DTYPE), w_out[layer].astype(MXU_DTYPE), tm)
        x2d = _mlp(x2d, g_mlp[layer][None, :], w_up[layer].astype(MXU_DTYPE),
                   w_down[layer].astype(MXU_DTYPE), g_final[None, :], tm, layer == depth - 1)
    return x2d.reshape(b, s, d)
```

```python
import functools

import jax
import jax.numpy as jnp
from jax import lax
from jax.experimental import pallas as pl
from jax.experimental.pallas import tpu as pltpu

F32 = jnp.float32
MXU_DTYPE = jnp.bfloat16

HEAD_DIM = 64
SB_HEADS = 8
FOX_HEADS = 8
DSA_HEADS = 8
DSA_KV_HEADS = 2
DSA_GROUP = DSA_HEADS // DSA_KV_HEADS
IDX_HEADS = 4
IDX_DIM = 64
N_BRANCH = 3
ROPE_THETA = 500000.0
ROT_DIM = HEAD_DIM // 4
TOPK_MAX = 256
NORM_EPS = 1e-6

LANES = 128
SUBLANES = 8
BRANCH_WIDTH = SB_HEADS * HEAD_DIM
ATTN_SCALE = HEAD_DIM ** -0.5
assert IDX_DIM == HEAD_DIM
IDX_W_SCALE = IDX_HEADS ** -0.5

COL_QA, COL_KA = 0, 512
COL_QB, COL_KB = 1024, 1536
COL_QC = 2048
COL_KC = 2560
COL_QI = 2816
COL_KI = 3072
MAIN_COLS = 3200
ROPE_START = COL_QC
ROW_VA, ROW_VB, ROW_VC = 0, 512, 1024
VT_ROWS = 1152
MISC_FL, MISC_WI = 0, 8

TQ = 256
TK = 256
HEADS_PER_STEP = 8
NEG_BIG = -1e30

KEY_NEG_INF = -2139095041
KEY_END = 2139095041

VMEM_LIMIT = 56 * 1024 * 1024


def _cparams(sem):
    return pltpu.CompilerParams(dimension_semantics=sem, vmem_limit_bytes=VMEM_LIMIT)


def _rmsnorm(x, g):
    y = x * lax.rsqrt(jnp.mean(x * x, axis=-1, keepdims=True) + NORM_EPS)
    return y * g


def _dot(a, b):
    return jnp.dot(a, b, preferred_element_type=F32)


def _dot_nt(a, b):
    return lax.dot_general(a, b, (((1,), (1,)), ((), ())), preferred_element_type=F32)


def _split2(x):
    hi = x.astype(MXU_DTYPE)
    lo = (x - hi.astype(F32)).astype(MXU_DTYPE)
    return hi, lo


def _split3(x):
    hi = x.astype(MXU_DTYPE)
    r = x - hi.astype(F32)
    mid = r.astype(MXU_DTYPE)
    lo = (r - mid.astype(F32)).astype(MXU_DTYPE)
    return hi, mid, lo


def _log_sigmoid_parts(z):
    sp = jnp.log(1.0 + jnp.exp(-jnp.abs(z)))
    return jnp.minimum(z, 0.0) - sp, jnp.maximum(z, 0.0) + sp


def _proj_kernel(x_ref, pos_ref, g_ref, freq_ref, sg1_ref, sg2_ref, wm_ref, wv_ref, wx_ref,
                 main_ref, vt_ref, misc_ref):
    h = _rmsnorm(x_ref[...], g_ref[...]).astype(MXU_DTYPE)
    ang = pos_ref[...].astype(F32) * freq_ref[...]
    cos = jnp.cos(ang)
    sin = jnp.sin(ang)
    s_up = sin * sg1_ref[...]
    s_dn = sin * sg2_ref[...]
    half = ROT_DIM // 2
    chunk = 4 * LANES
    for c0 in range(0, MAIN_COLS, chunk):
        c1 = min(c0 + chunk, MAIN_COLS)
        p = _dot(h, wm_ref[:, c0:c1])
        if c0 >= ROPE_START:
            for t in range((c1 - c0) // LANES):
                pt = p[:, t * LANES:(t + 1) * LANES]
                pt = (pt * cos + pltpu.roll(pt, LANES - half, 1) * s_up
                      + pltpu.roll(pt, half, 1) * s_dn)
                main_ref[:, c0 + t * LANES:c0 + (t + 1) * LANES] = pt.astype(main_ref.dtype)
        else:
            main_ref[:, c0:c1] = p.astype(main_ref.dtype)
    for r0 in range(0, VT_ROWS, LANES):
        pv = _dot(h, wv_ref[:, r0:r0 + LANES])
        vt_ref[0, r0:r0 + LANES, :] = pv.T.astype(vt_ref.dtype)
    misc_ref[...] = _dot(h, wx_ref[...])


def _proj(x2d, pos2d, g, freq, sg1, sg2, w_main, w_v, w_misc, tm, batch, seq):
    n, d = x2d.shape
    nsb = seq // tm
    const = lambda i: (0, 0)
    return pl.pallas_call(
        _proj_kernel,
        grid=(n // tm,),
        in_specs=[
            pl.BlockSpec((tm, d), lambda i: (i, 0)),
            pl.BlockSpec((tm, 1), lambda i: (i, 0)),
            pl.BlockSpec((1, d), const),
            pl.BlockSpec((1, LANES), const),
            pl.BlockSpec((1, LANES), const),
            pl.BlockSpec((1, LANES), const),
            pl.BlockSpec((d, MAIN_COLS), const),
            pl.BlockSpec((d, VT_ROWS), const),
            pl.BlockSpec((d, LANES), const),
        ],
        out_specs=[
            pl.BlockSpec((tm, MAIN_COLS), lambda i: (i, 0)),
            pl.BlockSpec((1, VT_ROWS, tm), lambda i: (i // nsb, 0, i % nsb)),
            pl.BlockSpec((tm, LANES), lambda i: (i, 0)),
        ],
        out_shape=[
            jax.ShapeDtypeStruct((n, MAIN_COLS), MXU_DTYPE),
            jax.ShapeDtypeStruct((batch, VT_ROWS, seq), MXU_DTYPE),
            jax.ShapeDtypeStruct((n, LANES), F32),
        ],
        compiler_params=_cparams(("parallel",)),
        name="proj",
    )(x2d, pos2d, g, freq, sg1, sg2, w_main, w_v, w_misc)


def _cumf_kernel(misc_ref, bias_ref, c_ref):
    s = misc_ref.shape[1]
    r = lax.broadcasted_iota(jnp.int32, (TK, TK), 0)
    c = lax.broadcasted_iota(jnp.int32, (TK, TK), 1)
    incl = (c <= r).astype(MXU_DTYPE)
    carry = jnp.zeros((1, LANES), F32)
    for b in range(s // TK):
        logit = misc_ref[0, b * TK:(b + 1) * TK, :] + bias_ref[...]
        logf, _ = _log_sigmoid_parts(logit)
        hi, mid, lo = _split3(logf)
        cs = (_dot(incl, lo) + _dot(incl, mid)) + _dot(incl, hi) + carry
        c_ref[0, b * TK:(b + 1) * TK, :] = cs
        carry = cs[TK - 1:TK, :]


def _cumf(misc3d, bias_row):
    b, s, _ = misc3d.shape
    return pl.pallas_call(
        _cumf_kernel,
        grid=(b,),
        in_specs=[
            pl.BlockSpec((1, s, LANES), lambda i: (i, 0, 0)),
            pl.BlockSpec((1, LANES), lambda i: (0, 0)),
        ],
        out_specs=pl.BlockSpec((1, s, LANES), lambda i: (i, 0, 0)),
        out_shape=jax.ShapeDtypeStruct((b, s, LANES), F32),
        compiler_params=_cparams(("parallel",)),
        name="cumf",
    )(misc3d, bias_row)


def _head_q(q_tile, half):
    lane = lax.broadcasted_iota(jnp.int32, q_tile.shape, 1)
    keep = (lane // HEAD_DIM) == half
    return jnp.where(keep, q_tile.astype(F32) * ATTN_SCALE, 0.0).astype(MXU_DTYPE)


def _causal_tile_t(strict):
    key = lax.broadcasted_iota(jnp.int32, (TK, TQ), 0)
    qry = lax.broadcasted_iota(jnp.int32, (TK, TQ), 1)
    return (key < qry) if strict else (key <= qry)


def _rows(j):
    return pl.ds(pl.multiple_of(j * TK, TK), TK)


def _flash_heads(heads, n_full, diag_mask):
    def add_bias(s, bias):
        if isinstance(bias, (list, tuple)):
            return jnp.concatenate([s[:, t * LANES:(t + 1) * LANES] + bias[t] for t in range(len(bias))],
                                   axis=1)
        return s + bias

    def step(j, carry, mask):
        zs = [_dot_nt(k_at(j), qh) for (qh, k_at, _, _) in heads]
        stats = []
        for z, (_, _, _, bias_at), (m, l, _) in zip(zs, heads, carry):
            s = add_bias(z, bias_at(j))
            if mask is not None:
                s = jnp.where(mask, s, NEG_BIG)
            m_new = jnp.maximum(m, jnp.max(s, axis=0, keepdims=True))
            p = jnp.exp(s - m_new)
            alpha = jnp.exp(m - m_new)
            l = alpha * l + jnp.sum(p, axis=0, keepdims=True)
            stats.append((m_new, l, alpha, p.astype(MXU_DTYPE)))
        out = []
        for (m_new, l, alpha, p), (_, _, vt_at, _), (_, _, acc) in zip(stats, heads, carry):
            out.append((m_new, l, alpha * acc + _dot(vt_at(j), p)))
        return tuple(out)

    init = tuple((jnp.full((1, TQ), NEG_BIG, F32), jnp.zeros((1, TQ), F32), jnp.zeros((HEAD_DIM, TQ), F32))
                 for _ in heads)
    carry = lax.fori_loop(0, n_full, lambda j, c: step(j, c, None), init)
    carry = step(n_full, carry, diag_mask)
    return [acc / l for (_, l, acc) in carry]


def _store_heads_t(o_ref, outs_t):
    for t in range(len(outs_t) // 2):
        pair_t = jnp.concatenate([outs_t[2 * t], outs_t[2 * t + 1]], axis=0)
        o_ref[0, :, t * LANES:(t + 1) * LANES] = pair_t.T.astype(o_ref.dtype)


def _sb_kernel(q_ref, k_ref, vt_ref, o_ref):
    i = pl.program_id(2)
    r = lax.broadcasted_iota(jnp.int32, (TK, TK), 0)
    c = lax.broadcasted_iota(jnp.int32, (TK, TK), 1)
    after = (c > r).astype(MXU_DTYPE)
    after2 = jnp.concatenate([after, after], axis=1)
    strict = _causal_tile_t(True)
    n_heads = HEADS_PER_STEP
    qhs = [_head_q(q_ref[0][:, (h // 2) * LANES:(h // 2 + 1) * LANES], h % 2) for h in range(n_heads)]

    def step(j, carry, mask):
        zs = [_dot_nt(k_ref[0, _rows(j), (h // 2) * LANES:(h // 2 + 1) * LANES], qhs[h])
              for h in range(n_heads)]
        parts = []
        for z in zs:
            log_beta, neg_l1m = _log_sigmoid_parts(z)
            if mask is not None:
                neg_l1m = jnp.where(mask, neg_l1m, 0.0)
            hi, lo = _split2(neg_l1m)
            parts.append((log_beta, jnp.sum(neg_l1m, axis=0, keepdims=True), jnp.concatenate([hi, lo], axis=0)))
        laters = [_dot(after2, hl) for (_, _, hl) in parts]
        ws = []
        for (log_beta, _, _), later, (_, tail) in zip(parts, laters, carry):
            w = jnp.exp(log_beta - (later + tail))
            if mask is not None:
                w = jnp.where(mask, w, 0.0)
            ws.append(w.astype(MXU_DTYPE))
        out = []
        for h, (w, (_, blk_sum, _), (acc, tail)) in enumerate(zip(ws, parts, carry)):
            vtj = vt_ref[0, h * HEAD_DIM:(h + 1) * HEAD_DIM, _rows(j)]
            out.append((acc + _dot(vtj, w), tail + blk_sum))
        return tuple(out)

    carry = tuple((jnp.zeros((HEAD_DIM, TQ), F32), jnp.zeros((1, TQ), F32)) for _ in range(n_heads))
    carry = step(i, carry, strict)
    carry = lax.fori_loop(0, i, lambda t, cr: step(i - 1 - t, cr, None), carry)
    _store_heads_t(o_ref, [acc for acc, _ in carry])


def _sb(main3d, vt3d):
    b, s, _ = main3d.shape
    nq = s // TQ
    w = HEADS_PER_STEP * HEAD_DIM
    qb, kb, vb = COL_QA // w, COL_KA // w, ROW_VA // w
    return pl.pallas_call(
        _sb_kernel,
        grid=(b, SB_HEADS // HEADS_PER_STEP, nq),
        in_specs=[
            pl.BlockSpec((1, TQ, w), lambda bi, hg, i: (bi, i, qb + hg)),
            pl.BlockSpec((1, s, w), lambda bi, hg, i: (bi, 0, kb + hg)),
            pl.BlockSpec((1, w, s), lambda bi, hg, i: (bi, vb + hg, 0)),
        ],
        out_specs=pl.BlockSpec((1, TQ, w), lambda bi, hg, i: (bi, i, hg)),
        out_shape=jax.ShapeDtypeStruct((b, s, BRANCH_WIDTH), MXU_DTYPE),
        compiler_params=_cparams(("parallel", "parallel", "arbitrary")),
        name="sb",
    )(main3d, main3d, vt3d)


def _fox_kernel(q_ref, k_ref, vt_ref, c_ref, o_ref, cb_ref):
    hg = pl.program_id(1)
    i = pl.program_id(2)
    n_heads = HEADS_PER_STEP

    @pl.when(i == 0)
    def _():
        c = c_ref[0]
        lane = lax.broadcasted_iota(jnp.int32, c.shape, 1)
        for h in range(n_heads):
            col = jnp.sum(jnp.where(lane == hg * n_heads + h, c, 0.0), axis=1, keepdims=True)
            cb_ref[h] = jnp.broadcast_to(-col, c.shape)

    heads = []
    for h in range(n_heads):
        t = h // 2
        qh = _head_q(q_ref[0][:, t * LANES:(t + 1) * LANES], h % 2)
        k_at = lambda j, t=t: k_ref[0, _rows(j), t * LANES:(t + 1) * LANES]
        vt_at = lambda j, h=h: vt_ref[0, h * HEAD_DIM:(h + 1) * HEAD_DIM, _rows(j)]
        bias_at = lambda j, h=h: [cb_ref[h, _rows(j), :]] * (TQ // LANES)
        heads.append((qh, k_at, vt_at, bias_at))
    _store_heads_t(o_ref, _flash_heads(heads, i, _causal_tile_t(False)))


def _fox(main3d, vt3d, c3d):
    b, s, _ = main3d.shape
    nq = s // TQ
    w = HEADS_PER_STEP * HEAD_DIM
    qb, kb, vb = COL_QB // w, COL_KB // w, ROW_VB // w
    return pl.pallas_call(
        _fox_kernel,
        grid=(b, FOX_HEADS // HEADS_PER_STEP, nq),
        in_specs=[
            pl.BlockSpec((1, TQ, w), lambda bi, hg, i: (bi, i, qb + hg)),
            pl.BlockSpec((1, s, w), lambda bi, hg, i: (bi, 0, kb + hg)),
            pl.BlockSpec((1, w, s), lambda bi, hg, i: (bi, vb + hg, 0)),
            pl.BlockSpec((1, s, LANES), lambda bi, hg, i: (bi, 0, 0)),
        ],
        out_specs=pl.BlockSpec((1, TQ, w), lambda bi, hg, i: (bi, i, hg)),
        out_shape=jax.ShapeDtypeStruct((b, s, BRANCH_WIDTH), MXU_DTYPE),
        scratch_shapes=[pltpu.VMEM((HEADS_PER_STEP, s, LANES), F32)],
        compiler_params=_cparams(("parallel", "parallel", "arbitrary")),
        name="fox",
    )(main3d, main3d, vt3d, c3d)


def _key_to_f32(key):
    bits = key ^ ((key >> 31) & jnp.int32(0x7FFFFFFF))
    return lax.bitcast_convert_type(bits, F32)


def _dsa_kernel(q_ref, k_ref, vt_ref, qi_ref, ki_ref, misc_ref, o_ref, score_ref, bias_ref, *, top_k):
    i = pl.program_id(1)
    diag = _causal_tile_t(False)
    seq = score_ref.shape[0]

    wi_t = misc_ref[0].T[MISC_WI:MISC_WI + IDX_HEADS, :] * IDX_W_SCALE
    qih = [_head_q(qi_ref[0][:, (h // 2) * LANES:(h // 2 + 1) * LANES], h % 2) for h in range(IDX_HEADS)]

    def score_block(j, mask):
        kij = ki_ref[0, _rows(j), :]
        sc = None
        for h in range(IDX_HEADS):
            term = wi_t[h:h + 1, :] * jnp.maximum(_dot_nt(kij, qih[h]), 0.0)
            sc = term if sc is None else sc + term
        if mask is not None:
            sc = jnp.where(mask, sc, -jnp.inf)
        score_ref[_rows(j), :] = sc

    def score_loop(j, _):
        score_block(j, None)
        return 0

    lax.fori_loop(0, i, score_loop, 0)
    score_block(i, diag)

    def count(pred):
        def body(j, acc):
            hit = pred(score_ref[_rows(j), :], j).astype(jnp.int32)
            return acc + jnp.sum(hit.reshape(TK // SUBLANES, SUBLANES, TQ), axis=0)
        acc = lax.fori_loop(0, i + 1, body, jnp.zeros((SUBLANES, TQ), jnp.int32))
        return jnp.sum(acc, axis=0, keepdims=True)

    def unfinished(state):
        lo, hi, _ = state
        return jnp.max(((hi - 1) != lo).astype(jnp.int32)) > 0

    def bisect(state):
        lo, hi, n_lo = state
        mid = (lo >> 1) + (hi >> 1) + (lo & hi & 1)
        thr = _key_to_f32(mid)
        n_mid = count(lambda s, j: s >= thr)
        ok = n_mid >= top_k
        hi = jnp.where(ok, jnp.where(n_mid == top_k, mid + 1, hi), mid)
        return jnp.where(ok, mid, lo), hi, jnp.where(ok, n_mid, n_lo)

    lo0 = jnp.full((1, TQ), KEY_NEG_INF, jnp.int32)
    hi0 = jnp.full((1, TQ), KEY_END, jnp.int32)
    n0 = jnp.zeros((1, TQ), jnp.int32) + (i + 1) * TK
    lo, _, n_lo = lax.while_loop(unfinished, bisect, (lo0, hi0, n0))
    thr = _key_to_f32(lo)

    key_idx = lax.broadcasted_iota(jnp.int32, (TK, TQ), 0)
    n_idx_steps = (seq - 1).bit_length()

    def break_ties():
        need = top_k - count(lambda s, j: s > thr)

        def bisect_idx(_, lh):
            jl, jh = lh
            mid = (jl + jh) >> 1
            ok = count(lambda s, j: (s == thr) & (key_idx + j * TK <= mid)) >= need
            return jnp.where(ok, jl, mid), jnp.where(ok, mid, jh)

        jl0 = jnp.full((1, TQ), -1, jnp.int32)
        jh0 = jnp.full((1, TQ), seq - 1, jnp.int32)
        return lax.fori_loop(0, n_idx_steps, bisect_idx, (jl0, jh0))[1]

    last_tie = lax.cond(jnp.max(n_lo) > top_k, break_ties, lambda: jnp.full((1, TQ), seq - 1, jnp.int32))

    def bias_block(j, mask):
        s = score_ref[_rows(j), :]
        sel = (s > thr) | ((s == thr) & (key_idx + j * TK <= last_tie))
        if mask is not None:
            sel = sel & mask
        bias_ref[_rows(j), :] = jnp.where(sel, 0.0, NEG_BIG)

    def bias_loop(j, _):
        bias_block(j, None)
        return 0

    lax.fori_loop(0, i, bias_loop, 0)
    bias_block(i, diag)

    bias_at = lambda j: bias_ref[_rows(j), :]
    heads = []
    for h in range(DSA_HEADS):
        g = h // DSA_GROUP
        k_at = lambda j, g=g: k_ref[0, _rows(j), g * LANES:(g + 1) * LANES]
        vt_at = lambda j, g=g: vt_ref[0, g * HEAD_DIM:(g + 1) * HEAD_DIM, _rows(j)]
        qh = _head_q(q_ref[0][:, (h // 2) * LANES:(h // 2 + 1) * LANES], h % 2)
        heads.append((qh, k_at, vt_at, bias_at))
    _store_heads_t(o_ref, _flash_heads(heads, i, None))


def _dsa(main3d, vt3d, misc3d, top_k):
    b, s, _ = main3d.shape
    nq = s // TQ
    assert top_k <= TK, "the first key block must hold at least top_k keys"
    return pl.pallas_call(
        functools.partial(_dsa_kernel, top_k=top_k),
        grid=(b, nq),
        in_specs=[
            pl.BlockSpec((1, TQ, 4 * LANES), lambda bi, i: (bi, i, COL_QC // (4 * LANES))),
            pl.BlockSpec((1, s, 2 * LANES), lambda bi, i: (bi, 0, COL_KC // (2 * LANES))),
            pl.BlockSpec((1, LANES, s), lambda bi, i: (bi, ROW_VC // LANES, 0)),
            pl.BlockSpec((1, TQ, 2 * LANES), lambda bi, i: (bi, i, COL_QI // (2 * LANES))),
            pl.BlockSpec((1, s, LANES), lambda bi, i: (bi, 0, COL_KI // LANES)),
            pl.BlockSpec((1, TQ, LANES), lambda bi, i: (bi, i, 0)),
        ],
        out_specs=pl.BlockSpec((1, TQ, BRANCH_WIDTH), lambda bi, i: (bi, i, 0)),
        out_shape=jax.ShapeDtypeStruct((b, s, BRANCH_WIDTH), MXU_DTYPE),
        scratch_shapes=[pltpu.VMEM((s, TQ), F32), pltpu.VMEM((s, TQ), F32)],
        compiler_params=_cparams(("parallel", "arbitrary")),
        name="dsa",
    )(main3d, main3d, vt3d, main3d, main3d, misc3d)


def _merge_kernel(x_ref, a_ref, b_ref, c_ref, g_ref, wg_ref, wb_ref, wo_ref, o_ref):
    x = x_ref[...]
    d = x.shape[1]
    h = _rmsnorm(x, g_ref[...]).astype(MXU_DTYPE)
    y = None
    for n, br_ref in enumerate((a_ref, b_ref, c_ref)):
        gate = jax.nn.sigmoid(_dot(h, wg_ref[:, n * d:(n + 1) * d]))
        term = gate * _dot(br_ref[...], wb_ref[n])
        y = term if y is None else y + term
    o_ref[...] = x + _dot(y.astype(MXU_DTYPE), wo_ref[...])


def _merge(x2d, br_a, br_b, br_c, g, w_gate, w_branch, w_out, tm):
    n, d = x2d.shape
    bw = br_a.shape[1]
    row = lambda i: (i, 0)
    const = lambda i: (0, 0)
    return pl.pallas_call(
        _merge_kernel,
        grid=(n // tm,),
        in_specs=[
            pl.BlockSpec((tm, d), row),
            pl.BlockSpec((tm, bw), row),
            pl.BlockSpec((tm, bw), row),
            pl.BlockSpec((tm, bw), row),
            pl.BlockSpec((1, d), const),
            pl.BlockSpec((d, N_BRANCH * d), const),
            pl.BlockSpec((N_BRANCH, bw, d), lambda i: (0, 0, 0)),
            pl.BlockSpec((d, d), const),
        ],
        out_specs=pl.BlockSpec((tm, d), row),
        out_shape=jax.ShapeDtypeStruct((n, d), F32),
        compiler_params=_cparams(("parallel",)),
        name="merge",
    )(x2d, br_a, br_b, br_c, g, w_gate, w_branch, w_out)


def _mlp_kernel(x_ref, g_ref, wu_ref, wd_ref, gf_ref, o_ref, *, ff_chunk, final_norm):
    x = x_ref[...]
    h = _rmsnorm(x, g_ref[...]).astype(MXU_DTYPE)
    acc = x
    for c in range(wu_ref.shape[1] // ff_chunk):
        u = jnp.maximum(_dot(h, wu_ref[:, c * ff_chunk:(c + 1) * ff_chunk]), 0.0)
        acc = acc + _dot((u * u).astype(MXU_DTYPE), wd_ref[c * ff_chunk:(c + 1) * ff_chunk, :])
    if final_norm:
        acc = _rmsnorm(acc, gf_ref[...])
    o_ref[...] = acc


def _mlp(x2d, g, w_up, w_down, g_final, tm, final_norm):
    n, d = x2d.shape
    ff = w_up.shape[1]
    row = lambda i: (i, 0)
    const = lambda i: (0, 0)
    return pl.pallas_call(
        functools.partial(_mlp_kernel, ff_chunk=min(ff, 1024), final_norm=final_norm),
        grid=(n // tm,),
        in_specs=[
            pl.BlockSpec((tm, d), row),
            pl.BlockSpec((1, d), const),
            pl.BlockSpec((d, ff), const),
            pl.BlockSpec((ff, d), const),
            pl.BlockSpec((1, d), const),
        ],
        out_specs=pl.BlockSpec((tm, d), row),
        out_shape=jax.ShapeDtypeStruct((n, d), F32),
        compiler_params=_cparams(("parallel",)),
        name="mlp",
    )(x2d, g, w_up, w_down, g_final)


def _w_in_columns():
    sizes = (("qa", BRANCH_WIDTH), ("ka", BRANCH_WIDTH), ("va", BRANCH_WIDTH),
             ("qb", BRANCH_WIDTH), ("kb", BRANCH_WIDTH), ("vb", BRANCH_WIDTH), ("fl", FOX_HEADS),
             ("qc", DSA_HEADS * HEAD_DIM), ("kc", DSA_KV_HEADS * HEAD_DIM), ("vc", DSA_KV_HEADS * HEAD_DIM),
             ("qi", IDX_HEADS * IDX_DIM), ("ki", IDX_DIM), ("wi", IDX_HEADS), ("gates", N_BRANCH * 1024))
    cols, o = {}, 0
    for name, width in sizes:
        cols[name] = (o, o + width)
        o += width
    return cols, o


def _pack_kernel(w_ref, main_ref, vals_ref, misc_ref, gates_ref):
    cols, _ = _w_in_columns()

    def src(name, lo=0, hi=None):
        a, b = cols[name]
        return w_ref[0, :, a + lo:(b if hi is None else a + hi)]

    def put(ref, start, value):
        ref[0, :, start:start + value.shape[1]] = value.astype(ref.dtype)

    put(main_ref, COL_QA, src("qa"))
    put(main_ref, COL_KA, src("ka"))
    put(main_ref, COL_QB, src("qb"))
    put(main_ref, COL_KB, src("kb"))
    put(main_ref, COL_QC, src("qc"))
    for g in range(DSA_KV_HEADS):
        kg = src("kc", g * HEAD_DIM, (g + 1) * HEAD_DIM)
        put(main_ref, COL_KC + g * LANES, jnp.concatenate([kg, kg], axis=1))
    put(main_ref, COL_QI, src("qi"))
    ki = src("ki")
    put(main_ref, COL_KI, jnp.concatenate([ki, ki], axis=1))
    put(vals_ref, ROW_VA, src("va"))
    put(vals_ref, ROW_VB, src("vb"))
    put(vals_ref, ROW_VC, src("vc"))
    pad = jnp.zeros((w_ref.shape[1], LANES - FOX_HEADS - IDX_HEADS), F32)
    put(misc_ref, 0, jnp.concatenate([src("fl"), src("wi"), pad], axis=1))
    put(gates_ref, 0, src("gates"))


def _pack_w_in(w_in):
    depth, d, total = w_in.shape
    cols, end = _w_in_columns()
    assert end == total and cols["gates"][1] - cols["gates"][0] == N_BRANCH * d
    rows = 256
    blk = lambda width: pl.BlockSpec((1, rows, width), lambda l, r: (l, r, 0))
    widths = (MAIN_COLS, VT_ROWS, LANES, N_BRANCH * d)
    return pl.pallas_call(
        _pack_kernel,
        grid=(depth, d // rows),
        in_specs=[blk(total)],
        out_specs=[blk(wd) for wd in widths],
        out_shape=[jax.ShapeDtypeStruct((depth, d, wd), MXU_DTYPE) for wd in widths],
        compiler_params=_cparams(("parallel", "parallel")),
        name="pack",
    )(w_in)


def _rope_rows():
    lane = jnp.arange(LANES) % HEAD_DIM
    half = ROT_DIM // 2
    inv_freq = ROPE_THETA ** (-jnp.arange(0, ROT_DIM, 2, dtype=F32) / ROT_DIM)
    freq = jnp.where(lane < ROT_DIM, inv_freq[lane % half], 0.0).astype(F32)
    sg1 = jnp.where(lane < half, -1.0, 0.0).astype(F32)
    sg2 = jnp.where((lane >= half) & (lane < ROT_DIM), 1.0, 0.0).astype(F32)
    return freq[None, :], sg1[None, :], sg2[None, :]


def kernel(x, positions, g_mix, w_in, b_forget, w_branch, w_out, g_mlp, w_up, w_down, g_final):
    b, s, d = x.shape
    n = b * s
    depth = w_in.shape[0]
    top_k = min(TOPK_MAX, s // 4)
    tm = min(512, s)
    freq, sg1, sg2 = _rope_rows()
    pos2d = positions.reshape(n, 1)
    x2d = x.reshape(n, d)
    w_main_all, w_v_all, w_misc_all, w_gate_all = _pack_w_in(w_in)
    for layer in range(depth):
        w_main, w_v, w_misc, w_gate = (w_main_all[layer], w_v_all[layer], w_misc_all[layer],
                                       w_gate_all[layer])
        bias_row = jnp.concatenate([b_forget[layer], jnp.zeros((LANES - FOX_HEADS,), F32)])[None, :]
        main, vt, misc = _proj(x2d, pos2d, g_mix[layer][None, :], freq, sg1, sg2, w_main, w_v, w_misc,
                               tm, b, s)
        main3d = main.reshape(b, s, MAIN_COLS)
        misc3d = misc.reshape(b, s, LANES)
        cum = _cumf(misc3d, bias_row)
        br_a = _sb(main3d, vt).reshape(n, BRANCH_WIDTH)
        br_b = _fox(main3d, vt, cum).reshape(n, BRANCH_WIDTH)
        br_c = _dsa(main3d, vt, misc3d, top_k).reshape(n, BRANCH_WIDTH)
        x2d = _merge(x2d, br_a, br_b, br_c, g_mix[layer][None, :], w_gate,
                     w_branch[layer].astype(MXU_DTYPE), w_out[layer].astype(MXU_DTYPE), tm)
        x2d = _mlp(x2d, g_mlp[layer][None, :], w_up[layer].astype(MXU_DTYPE),
                   w_down[layer].astype(MXU_DTYPE), g_final[None, :], tm, layer == depth - 1)
    return x2d.reshape(b, s, d)
```

```python
import functools

import jax
import jax.numpy as jnp
from jax import lax
from jax.experimental import pallas as pl
from jax.experimental.pallas import tpu as pltpu

F32 = jnp.float32
MXU_DTYPE = jnp.bfloat16

HEAD_DIM = 64
SB_HEADS = 8
FOX_HEADS = 8
DSA_HEADS = 8
DSA_KV_HEADS = 2
DSA_GROUP = DSA_HEADS // DSA_KV_HEADS
IDX_HEADS = 4
IDX_DIM = 64
N_BRANCH = 3
ROPE_THETA = 500000.0
ROT_DIM = HEAD_DIM // 4
TOPK_MAX = 256
NORM_EPS = 1e-6

LANES = 128
SUBLANES = 8
BRANCH_WIDTH = SB_HEADS * HEAD_DIM
ATTN_SCALE = HEAD_DIM ** -0.5
assert IDX_DIM == HEAD_DIM
IDX_W_SCALE = IDX_HEADS ** -0.5

COL_QA, COL_KA = 0, 512
COL_QB, COL_KB = 1024, 1536
COL_QC = 2048
COL_KC = 2560
COL_QI = 2816
COL_KI = 3072
MAIN_COLS = 3200
ROPE_START = COL_QC
ROW_VA, ROW_VB, ROW_VC = 0, 512, 1024
VT_ROWS = 1152
MISC_FL, MISC_WI = 0, 8

TQ = 256
TK = 256
HEADS_PER_STEP = 8
NEG_BIG = -1e30

KEY_NEG_INF = -2139095041
KEY_END = 2139095041

VMEM_LIMIT = 56 * 1024 * 1024


def _cparams(sem):
    return pltpu.CompilerParams(dimension_semantics=sem, vmem_limit_bytes=VMEM_LIMIT)


def _rmsnorm(x, g):
    y = x * lax.rsqrt(jnp.mean(x * x, axis=-1, keepdims=True) + NORM_EPS)
    return y * g


def _dot(a, b):
    return jnp.dot(a, b, preferred_element_type=F32)


def _dot_nt(a, b):
    return lax.dot_general(a, b, (((1,), (1,)), ((), ())), preferred_element_type=F32)


def _split2(x):
    hi = x.astype(MXU_DTYPE)
    lo = (x - hi.astype(F32)).astype(MXU_DTYPE)
    return hi, lo


def _split3(x):
    hi = x.astype(MXU_DTYPE)
    r = x - hi.astype(F32)
    mid = r.astype(MXU_DTYPE)
    lo = (r - mid.astype(F32)).astype(MXU_DTYPE)
    return hi, mid, lo


def _log_sigmoid_parts(z):
    sp = jnp.log(1.0 + jnp.exp(-jnp.abs(z)))
    return jnp.minimum(z, 0.0) - sp, jnp.maximum(z, 0.0) + sp


def _proj_kernel(x_ref, pos_ref, g_ref, freq_ref, sg1_ref, sg2_ref, wm_ref, wv_ref, wx_ref,
                 main_ref, vt_ref, misc_ref):
    h = _rmsnorm(x_ref[...], g_ref[...]).astype(MXU_DTYPE)
    ang = pos_ref[...].astype(F32) * freq_ref[...]
    cos = jnp.cos(ang)
    sin = jnp.sin(ang)
    s_up = sin * sg1_ref[...]
    s_dn = sin * sg2_ref[...]
    half = ROT_DIM // 2
    chunk = 4 * LANES
    for c0 in range(0, MAIN_COLS, chunk):
        c1 = min(c0 + chunk, MAIN_COLS)
        p = _dot(h, wm_ref[:, c0:c1])
        if c0 >= ROPE_START:
            for t in range((c1 - c0) // LANES):
                pt = p[:, t * LANES:(t + 1) * LANES]
                pt = (pt * cos + pltpu.roll(pt, LANES - half, 1) * s_up
                      + pltpu.roll(pt, half, 1) * s_dn)
                main_ref[:, c0 + t * LANES:c0 + (t + 1) * LANES] = pt.astype(main_ref.dtype)
        else:
            main_ref[:, c0:c1] = p.astype(main_ref.dtype)
    for r0 in range(0, VT_ROWS, LANES):
        pv = _dot(h, wv_ref[:, r0:r0 + LANES])
        vt_ref[0, r0:r0 + LANES, :] = pv.T.astype(vt_ref.dtype)
    misc_ref[...] = _dot(h, wx_ref[...])


def _proj(x2d, pos2d, g, freq, sg1, sg2, w_main, w_v, w_misc, tm, batch, seq):
    n, d = x2d.shape
    nsb = seq // tm
    const = lambda i: (0, 0)
    return pl.pallas_call(
        _proj_kernel,
        grid=(n // tm,),
        in_specs=[
            pl.BlockSpec((tm, d), lambda i: (i, 0)),
            pl.BlockSpec((tm, 1), lambda i: (i, 0)),
            pl.BlockSpec((1, d), const),
            pl.BlockSpec((1, LANES), const),
            pl.BlockSpec((1, LANES), const),
            pl.BlockSpec((1, LANES), const),
            pl.BlockSpec((d, MAIN_COLS), const),
            pl.BlockSpec((d, VT_ROWS), const),
            pl.BlockSpec((d, LANES), const),
        ],
        out_specs=[
            pl.BlockSpec((tm, MAIN_COLS), lambda i: (i, 0)),
            pl.BlockSpec((1, VT_ROWS, tm), lambda i: (i // nsb, 0, i % nsb)),
            pl.BlockSpec((tm, LANES), lambda i: (i, 0)),
        ],
        out_shape=[
            jax.ShapeDtypeStruct((n, MAIN_COLS), MXU_DTYPE),
            jax.ShapeDtypeStruct((batch, VT_ROWS, seq), MXU_DTYPE),
            jax.ShapeDtypeStruct((n, LANES), F32),
        ],
        compiler_params=_cparams(("parallel",)),
        name="proj",
    )(x2d, pos2d, g, freq, sg1, sg2, w_main, w_v, w_misc)


def _cumf_kernel(misc_ref, bias_ref, c_ref):
    s = misc_ref.shape[1]
    r = lax.broadcasted_iota(jnp.int32, (TK, TK), 0)
    c = lax.broadcasted_iota(jnp.int32, (TK, TK), 1)
    incl = (c <= r).astype(MXU_DTYPE)
    carry = jnp.zeros((1, LANES), F32)
    for b in range(s // TK):
        logit = misc_ref[0, b * TK:(b + 1) * TK, :] + bias_ref[...]
        logf, _ = _log_sigmoid_parts(logit)
        hi, mid, lo = _split3(logf)
        cs = (_dot(incl, lo) + _dot(incl, mid)) + _dot(incl, hi) + carry
        c_ref[0, b * TK:(b + 1) * TK, :] = cs
        carry = cs[TK - 1:TK, :]


def _cumf(misc3d, bias_row):
    b, s, _ = misc3d.shape
    return pl.pallas_call(
        _cumf_kernel,
        grid=(b,),
        in_specs=[
            pl.BlockSpec((1, s, LANES), lambda i: (i, 0, 0)),
            pl.BlockSpec((1, LANES), lambda i: (0, 0)),
        ],
        out_specs=pl.BlockSpec((1, s, LANES), lambda i: (i, 0, 0)),
        out_shape=jax.ShapeDtypeStruct((b, s, LANES), F32),
        compiler_params=_cparams(("parallel",)),
        name="cumf",
    )(misc3d, bias_row)


def _head_q(q_tile, half):
    lane = lax.broadcasted_iota(jnp.int32, q_tile.shape, 1)
    keep = (lane // HEAD_DIM) == half
    return jnp.where(keep, q_tile.astype(F32) * ATTN_SCALE, 0.0).astype(MXU_DTYPE)


def _causal_tile_t(strict):
    key = lax.broadcasted_iota(jnp.int32, (TK, TQ), 0)
    qry = lax.broadcasted_iota(jnp.int32, (TK, TQ), 1)
    return (key < qry) if strict else (key <= qry)


def _rows(j):
    return pl.ds(pl.multiple_of(j * TK, TK), TK)


def _flash_heads(heads, n_full, diag_mask):
    def add_bias(s, bias):
        if isinstance(bias, (list, tuple)):
            return jnp.concatenate([s[:, t * LANES:(t + 1) * LANES] + bias[t] for t in range(len(bias))],
                                   axis=1)
        return s + bias

    def step(j, carry, mask):
        zs = [_dot_nt(k_at(j), qh) for (qh, k_at, _, _) in heads]
        stats = []
        for z, (_, _, _, bias_at), (m, l, _) in zip(zs, heads, carry):
            s = add_bias(z, bias_at(j))
            if mask is not None:
                s = jnp.where(mask, s, NEG_BIG)
            m_new = jnp.maximum(m, jnp.max(s, axis=0, keepdims=True))
            p = jnp.exp(s - m_new)
            alpha = jnp.exp(m - m_new)
            l = alpha * l + jnp.sum(p, axis=0, keepdims=True)
            stats.append((m_new, l, alpha, p.astype(MXU_DTYPE)))
        out = []
        for (m_new, l, alpha, p), (_, _, vt_at, _), (_, _, acc) in zip(stats, heads, carry):
            out.append((m_new, l, alpha * acc + _dot(vt_at(j), p)))
        return tuple(out)

    init = tuple((jnp.full((1, TQ), NEG_BIG, F32), jnp.zeros((1, TQ), F32), jnp.zeros((HEAD_DIM, TQ), F32))
                 for _ in heads)
    carry = lax.fori_loop(0, n_full, lambda j, c: step(j, c, None), init)
    carry = step(n_full, carry, diag_mask)
    return [acc / l for (_, l, acc) in carry]


def _store_heads_t(o_ref, outs_t):
    for t in range(len(outs_t) // 2):
        pair_t = jnp.concatenate([outs_t[2 * t], outs_t[2 * t + 1]], axis=0)
        o_ref[0, :, t * LANES:(t + 1) * LANES] = pair_t.T.astype(o_ref.dtype)


def _sb_kernel(q_ref, k_ref, vt_ref, o_ref):
    i = pl.program_id(2)
    r = lax.broadcasted_iota(jnp.int32, (TK, TK), 0)
    c = lax.broadcasted_iota(jnp.int32, (TK, TK), 1)
    after = (c > r).astype(MXU_DTYPE)
    after2 = jnp.concatenate([after, after], axis=1)
    strict = _causal_tile_t(True)
    n_heads = HEADS_PER_STEP
    qhs = [_head_q(q_ref[0][:, (h // 2) * LANES:(h // 2 + 1) * LANES], h % 2) for h in range(n_heads)]

    def step(j, carry, mask):
        zs = [_dot_nt(k_ref[0, _rows(j), (h // 2) * LANES:(h // 2 + 1) * LANES], qhs[h])
              for h in range(n_heads)]
        parts = []
        for z in zs:
            log_beta, neg_l1m = _log_sigmoid_parts(z)
            if mask is not None:
                neg_l1m = jnp.where(mask, neg_l1m, 0.0)
            hi, lo = _split2(neg_l1m)
            parts.append((log_beta, jnp.sum(neg_l1m, axis=0, keepdims=True), jnp.concatenate([hi, lo], axis=0)))
        laters = [_dot(after2, hl) for (_, _, hl) in parts]
        ws = []
        for (log_beta, _, _), later, (_, tail) in zip(parts, laters, carry):
            w = jnp.exp(log_beta - (later + tail))
            if mask is not None:
                w = jnp.where(mask, w, 0.0)
            ws.append(w.astype(MXU_DTYPE))
        out = []
        for h, (w, (_, blk_sum, _), (acc, tail)) in enumerate(zip(ws, parts, carry)):
            vtj = vt_ref[0, h * HEAD_DIM:(h + 1) * HEAD_DIM, _rows(j)]
            out.append((acc + _dot(vtj, w), tail + blk_sum))
        return tuple(out)

    carry = tuple((jnp.zeros((HEAD_DIM, TQ), F32), jnp.zeros((1, TQ), F32)) for _ in range(n_heads))
    carry = step(i, carry, strict)
    carry = lax.fori_loop(0, i, lambda t, cr: step(i - 1 - t, cr, None), carry)
    _store_heads_t(o_ref, [acc for acc, _ in carry])


def _sb(main3d, vt3d):
    b, s, _ = main3d.shape
    nq = s // TQ
    w = HEADS_PER_STEP * HEAD_DIM
    qb, kb, vb = COL_QA // w, COL_KA // w, ROW_VA // w
    return pl.pallas_call(
        _sb_kernel,
        grid=(b, SB_HEADS // HEADS_PER_STEP, nq),
        in_specs=[
            pl.BlockSpec((1, TQ, w), lambda bi, hg, i: (bi, i, qb + hg)),
            pl.BlockSpec((1, s, w), lambda bi, hg, i: (bi, 0, kb + hg)),
            pl.BlockSpec((1, w, s), lambda bi, hg, i: (bi, vb + hg, 0)),
        ],
        out_specs=pl.BlockSpec((1, TQ, w), lambda bi, hg, i: (bi, i, hg)),
        out_shape=jax.ShapeDtypeStruct((b, s, BRANCH_WIDTH), MXU_DTYPE),
        compiler_params=_cparams(("parallel", "parallel", "arbitrary")),
        name="sb",
    )(main3d, main3d, vt3d)


def _fox_kernel(q_ref, k_ref, vt_ref, c_ref, o_ref, cb_ref):
    hg = pl.program_id(1)
    i = pl.program_id(2)
    n_heads = HEADS_PER_STEP

    @pl.when(i == 0)
    def _():
        c = c_ref[0]
        lane = lax.broadcasted_iota(jnp.int32, c.shape, 1)
        for h in range(n_heads):
            col = jnp.sum(jnp.where(lane == hg * n_heads + h, c, 0.0), axis=1, keepdims=True)
            cb_ref[h] = jnp.broadcast_to(-col, c.shape)

    heads = []
    for h in range(n_heads):
        t = h // 2
        qh = _head_q(q_ref[0][:, t * LANES:(t + 1) * LANES], h % 2)
        k_at = lambda j, t=t: k_ref[0, _rows(j), t * LANES:(t + 1) * LANES]
        vt_at = lambda j, h=h: vt_ref[0, h * HEAD_DIM:(h + 1) * HEAD_DIM, _rows(j)]
        bias_at = lambda j, h=h: [cb_ref[h, _rows(j), :]] * (TQ // LANES)
        heads.append((qh, k_at, vt_at, bias_at))
    _store_heads_t(o_ref, _flash_heads(heads, i, _causal_tile_t(False)))


def _fox(main3d, vt3d, c3d):
    b, s, _ = main3d.shape
    nq = s // TQ
    w = HEADS_PER_STEP * HEAD_DIM
    qb, kb, vb = COL_QB // w, COL_KB // w, ROW_VB // w
    return pl.pallas_call(
        _fox_kernel,
        grid=(b, FOX_HEADS // HEADS_PER_STEP, nq),
        in_specs=[
            pl.BlockSpec((1, TQ, w), lambda bi, hg, i: (bi, i, qb + hg)),
            pl.BlockSpec((1, s, w), lambda bi, hg, i: (bi, 0, kb + hg)),
            pl.BlockSpec((1, w, s), lambda bi, hg, i: (bi, vb + hg, 0)),
            pl.BlockSpec((1, s, LANES), lambda bi, hg, i: (bi, 0, 0)),
        ],
        out_specs=pl.BlockSpec((1, TQ, w), lambda bi, hg, i: (bi, i, hg)),
        out_shape=jax.ShapeDtypeStruct((b, s, BRANCH_WIDTH), MXU_DTYPE),
        scratch_shapes=[pltpu.VMEM((HEADS_PER_STEP, s, LANES), F32)],
        compiler_params=_cparams(("parallel", "parallel", "arbitrary")),
        name="fox",
    )(main3d, main3d, vt3d, c3d)


def _key_to_f32(key):
    bits = key ^ ((key >> 31) & jnp.int32(0x7FFFFFFF))
    return lax.bitcast_convert_type(bits, F32)


def _dsa_kernel(q_ref, k_ref, vt_ref, qi_ref, ki_ref, misc_ref, o_ref, score_ref, bias_ref, *, top_k):
    i = pl.program_id(1)
    diag = _causal_tile_t(False)
    seq = score_ref.shape[0]

    wi_t = misc_ref[0].T[MISC_WI:MISC_WI + IDX_HEADS, :] * IDX_W_SCALE
    qih = [_head_q(qi_ref[0][:, (h // 2) * LANES:(h // 2 + 1) * LANES], h % 2) for h in range(IDX_HEADS)]

    def score_block(j, mask):
        kij = ki_ref[0, _rows(j), :]
        sc = None
        for h in range(IDX_HEADS):
            term = wi_t[h:h + 1, :] * jnp.maximum(_dot_nt(kij, qih[h]), 0.0)
            sc = term if sc is None else sc + term
        if mask is not None:
            sc = jnp.where(mask, sc, -jnp.inf)
        score_ref[_rows(j), :] = sc

    def score_loop(j, _):
        score_block(j, None)
        return 0

    lax.fori_loop(0, i, score_loop, 0)
    score_block(i, diag)

    def count(pred):
        def body(j, acc):
            hit = pred(score_ref[_rows(j), :], j).astype(jnp.int32)
            return acc + jnp.sum(hit.reshape(TK // SUBLANES, SUBLANES, TQ), axis=0)
        acc = lax.fori_loop(0, i + 1, body, jnp.zeros((SUBLANES, TQ), jnp.int32))
        return jnp.sum(acc, axis=0, keepdims=True)

    def bisect(_, lh):
        lo, hi = lh
        mid = (lo >> 1) + (hi >> 1) + (lo & hi & 1)
        thr = _key_to_f32(mid)
        ok = count(lambda s, j: s >= thr) >= top_k
        return jnp.where(ok, mid, lo), jnp.where(ok, hi, mid)

    lo0 = jnp.full((1, TQ), KEY_NEG_INF, jnp.int32)
    hi0 = jnp.full((1, TQ), KEY_END, jnp.int32)
    lo, _ = lax.fori_loop(0, 32, bisect, (lo0, hi0))
    thr = _key_to_f32(lo)

    need = (top_k - count(lambda s, j: s > thr)).astype(F32)
    kr = lax.broadcasted_iota(jnp.int32, (TK, TK), 0)
    kc = lax.broadcasted_iota(jnp.int32, (TK, TK), 1)
    upto = (kc <= kr).astype(MXU_DTYPE)

    def bias_block(j, mask, seen):
        s = score_ref[_rows(j), :]
        tie = s == thr
        rank = _dot(upto, jnp.where(tie, 1.0, 0.0).astype(MXU_DTYPE)) + seen
        sel = (s > thr) | (tie & (rank <= need))
        if mask is not None:
            sel = sel & mask
        bias_ref[_rows(j), :] = jnp.where(sel, 0.0, NEG_BIG)
        return rank[TK - 1:TK, :]

    seen = lax.fori_loop(0, i, lambda j, sn: bias_block(j, None, sn), jnp.zeros((1, TQ), F32))
    bias_block(i, diag, seen)

    bias_at = lambda j: bias_ref[_rows(j), :]
    heads = []
    for h in range(DSA_HEADS):
        g = h // DSA_GROUP
        k_at = lambda j, g=g: k_ref[0, _rows(j), g * LANES:(g + 1) * LANES]
        vt_at = lambda j, g=g: vt_ref[0, g * HEAD_DIM:(g + 1) * HEAD_DIM, _rows(j)]
        qh = _head_q(q_ref[0][:, (h // 2) * LANES:(h // 2 + 1) * LANES], h % 2)
        heads.append((qh, k_at, vt_at, bias_at))
    _store_heads_t(o_ref, _flash_heads(heads, i, None))


def _dsa(main3d, vt3d, misc3d, top_k):
    b, s, _ = main3d.shape
    nq = s // TQ
    assert top_k <= TK, "the first key block must hold at least top_k keys"
    return pl.pallas_call(
        functools.partial(_dsa_kernel, top_k=top_k),
        grid=(b, nq),
        in_specs=[
            pl.BlockSpec((1, TQ, 4 * LANES), lambda bi, i: (bi, i, COL_QC // (4 * LANES))),
            pl.BlockSpec((1, s, 2 * LANES), lambda bi, i: (bi, 0, COL_KC // (2 * LANES))),
            pl.BlockSpec((1, LANES, s), lambda bi, i: (bi, ROW_VC // LANES, 0)),
            pl.BlockSpec((1, TQ, 2 * LANES), lambda bi, i: (bi, i, COL_QI // (2 * LANES))),
            pl.BlockSpec((1, s, LANES), lambda bi, i: (bi, 0, COL_KI // LANES)),
            pl.BlockSpec((1, TQ, LANES), lambda bi, i: (bi, i, 0)),
        ],
        out_specs=pl.BlockSpec((1, TQ, BRANCH_WIDTH), lambda bi, i: (bi, i, 0)),
        out_shape=jax.ShapeDtypeStruct((b, s, BRANCH_WIDTH), MXU_DTYPE),
        scratch_shapes=[pltpu.VMEM((s, TQ), F32), pltpu.VMEM((s, TQ), F32)],
        compiler_params=_cparams(("parallel", "arbitrary")),
        name="dsa",
    )(main3d, main3d, vt3d, main3d, main3d, misc3d)


def _merge_kernel(x_ref, a_ref, b_ref, c_ref, g_ref, wg_ref, wb_ref, wo_ref, o_ref):
    x = x_ref[...]
    d = x.shape[1]
    h = _rmsnorm(x, g_ref[...]).astype(MXU_DTYPE)
    y = None
    for n, br_ref in enumerate((a_ref, b_ref, c_ref)):
        gate = jax.nn.sigmoid(_dot(h, wg_ref[:, n * d:(n + 1) * d]))
        term = gate * _dot(br_ref[...], wb_ref[n])
        y = term if y is None else y + term
    o_ref[...] = x + _dot(y.astype(MXU_DTYPE), wo_ref[...])


def _merge(x2d, br_a, br_b, br_c, g, w_gate, w_branch, w_out, tm):
    n, d = x2d.shape
    bw = br_a.shape[1]
    row = lambda i: (i, 0)
    const = lambda i: (0, 0)
    return pl.pallas_call(
        _merge_kernel,
        grid=(n // tm,),
        in_specs=[
            pl.BlockSpec((tm, d), row),
            pl.BlockSpec((tm, bw), row),
            pl.BlockSpec((tm, bw), row),
            pl.BlockSpec((tm, bw), row),
            pl.BlockSpec((1, d), const),
            pl.BlockSpec((d, N_BRANCH * d), const),
            pl.BlockSpec((N_BRANCH, bw, d), lambda i: (0, 0, 0)),
            pl.BlockSpec((d, d), const),
        ],
        out_specs=pl.BlockSpec((tm, d), row),
        out_shape=jax.ShapeDtypeStruct((n, d), F32),
        compiler_params=_cparams(("parallel",)),
        name="merge",
    )(x2d, br_a, br_b, br_c, g, w_gate, w_branch, w_out)


def _mlp_kernel(x_ref, g_ref, wu_ref, wd_ref, gf_ref, o_ref, *, ff_chunk, final_norm):
    x = x_ref[...]
    h = _rmsnorm(x, g_ref[...]).astype(MXU_DTYPE)
    acc = x
    for c in range(wu_ref.shape[1] // ff_chunk):
        u = jnp.maximum(_dot(h, wu_ref[:, c * ff_chunk:(c + 1) * ff_chunk]), 0.0)
        acc = acc + _dot((u * u).astype(MXU_DTYPE), wd_ref[c * ff_chunk:(c + 1) * ff_chunk, :])
    if final_norm:
        acc = _rmsnorm(acc, gf_ref[...])
    o_ref[...] = acc


def _mlp(x2d, g, w_up, w_down, g_final, tm, final_norm):
    n, d = x2d.shape
    ff = w_up.shape[1]
    row = lambda i: (i, 0)
    const = lambda i: (0, 0)
    return pl.pallas_call(
        functools.partial(_mlp_kernel, ff_chunk=min(ff, 1024), final_norm=final_norm),
        grid=(n // tm,),
        in_specs=[
            pl.BlockSpec((tm, d), row),
            pl.BlockSpec((1, d), const),
            pl.BlockSpec((d, ff), const),
            pl.BlockSpec((ff, d), const),
            pl.BlockSpec((1, d), const),
        ],
        out_specs=pl.BlockSpec((tm, d), row),
        out_shape=jax.ShapeDtypeStruct((n, d), F32),
        compiler_params=_cparams(("parallel",)),
        name="mlp",
    )(x2d, g, w_up, w_down, g_final)


def _w_in_columns():
    sizes = (("qa", BRANCH_WIDTH), ("ka", BRANCH_WIDTH), ("va", BRANCH_WIDTH),
             ("qb", BRANCH_WIDTH), ("kb", BRANCH_WIDTH), ("vb", BRANCH_WIDTH), ("fl", FOX_HEADS),
             ("qc", DSA_HEADS * HEAD_DIM), ("kc", DSA_KV_HEADS * HEAD_DIM), ("vc", DSA_KV_HEADS * HEAD_DIM),
             ("qi", IDX_HEADS * IDX_DIM), ("ki", IDX_DIM), ("wi", IDX_HEADS), ("gates", N_BRANCH * 1024))
    cols, o = {}, 0
    for name, width in sizes:
        cols[name] = (o, o + width)
        o += width
    return cols, o


def _pack_kernel(w_ref, main_ref, vals_ref, misc_ref, gates_ref):
    cols, _ = _w_in_columns()

    def src(name, lo=0, hi=None):
        a, b = cols[name]
        return w_ref[0, :, a + lo:(b if hi is None else a + hi)]

    def put(ref, start, value):
        ref[0, :, start:start + value.shape[1]] = value.astype(ref.dtype)

    put(main_ref, COL_QA, src("qa"))
    put(main_ref, COL_KA, src("ka"))
    put(main_ref, COL_QB, src("qb"))
    put(main_ref, COL_KB, src("kb"))
    put(main_ref, COL_QC, src("qc"))
    for g in range(DSA_KV_HEADS):
        kg = src("kc", g * HEAD_DIM, (g + 1) * HEAD_DIM)
        put(main_ref, COL_KC + g * LANES, jnp.concatenate([kg, kg], axis=1))
    put(main_ref, COL_QI, src("qi"))
    ki = src("ki")
    put(main_ref, COL_KI, jnp.concatenate([ki, ki], axis=1))
    put(vals_ref, ROW_VA, src("va"))
    put(vals_ref, ROW_VB, src("vb"))
    put(vals_ref, ROW_VC, src("vc"))
    pad = jnp.zeros((w_ref.shape[1], LANES - FOX_HEADS - IDX_HEADS), F32)
    put(misc_ref, 0, jnp.concatenate([src("fl"), src("wi"), pad], axis=1))
    put(gates_ref, 0, src("gates"))


def _pack_w_in(w_in):
    depth, d, total = w_in.shape
    cols, end = _w_in_columns()
    assert end == total and cols["gates"][1] - cols["gates"][0] == N_BRANCH * d
    rows = 256
    blk = lambda width: pl.BlockSpec((1, rows, width), lambda l, r: (l, r, 0))
    widths = (MAIN_COLS, VT_ROWS, LANES, N_BRANCH * d)
    return pl.pallas_call(
        _pack_kernel,
        grid=(depth, d // rows),
        in_specs=[blk(total)],
        out_specs=[blk(wd) for wd in widths],
        out_shape=[jax.ShapeDtypeStruct((depth, d, wd), MXU_DTYPE) for wd in widths],
        compiler_params=_cparams(("parallel", "parallel")),
        name="pack",
    )(w_in)


def _rope_rows():
    lane = jnp.arange(LANES) % HEAD_DIM
    half = ROT_DIM // 2
    inv_freq = ROPE_THETA ** (-jnp.arange(0, ROT_DIM, 2, dtype=F32) / ROT_DIM)
    freq = jnp.where(lane < ROT_DIM, inv_freq[lane % half], 0.0).astype(F32)
    sg1 = jnp.where(lane < half, -1.0, 0.0).astype(F32)
    sg2 = jnp.where((lane >= half) & (lane < ROT_DIM), 1.0, 0.0).astype(F32)
    return freq[None, :], sg1[None, :], sg2[None, :]


def kernel(x, positions, g_mix, w_in, b_forget, w_branch, w_out, g_mlp, w_up, w_down, g_final):
    b, s, d = x.shape
    n = b * s
    depth = w_in.shape[0]
    top_k = min(TOPK_MAX, s // 4)
    tm = min(512, s)
    freq, sg1, sg2 = _rope_rows()
    pos2d = positions.reshape(n, 1)
    x2d = x.reshape(n, d)
    w_main_all, w_v_all, w_misc_all, w_gate_all = _pack_w_in(w_in)
    for layer in range(depth):
        w_main, w_v, w_misc, w_gate = (w_main_all[layer], w_v_all[layer], w_misc_all[layer],
                                       w_gate_all[layer])
        bias_row = jnp.concatenate([b_forget[layer], jnp.zeros((LANES - FOX_HEADS,), F32)])[None, :]
        main, vt, misc = _proj(x2d, pos2d, g_mix[layer][None, :], freq, sg1, sg2, w_main, w_v, w_misc,
                               tm, b, s)
        main3d = main.reshape(b, s, MAIN_COLS)
        misc3d = misc.reshape(b, s, LANES)
        cum = _cumf(misc3d, bias_row)
        br_a = _sb(main3d, vt).reshape(n, BRANCH_WIDTH)
        br_b = _fox(main3d, vt, cum).reshape(n, BRANCH_WIDTH)
        br_c = _dsa(main3d, vt, misc3d, top_k).reshape(n, BRANCH_WIDTH)
        x2d = _merge(x2d, br_a, br_b, br_c, g_mix[layer][None, :], w_gate,
                     w_branch[layer].astype(MXU_DTYPE), w_out[layer].astype(MXU_DTYPE), tm)
        x2d = _mlp(x2d, g_mlp[layer][None, :], w_up[layer].astype(MXU_DTYPE),
                   w_down[layer].astype(MXU_DTYPE), g_final[None, :], tm, layer == depth - 1)
    return x2d.reshape(b, s, d)
```

```python
import functools

import jax
import jax.numpy as jnp
from jax import lax
from jax.experimental import pallas as pl
from jax.experimental.pallas import tpu as pltpu

F32 = jnp.float32
MXU_DTYPE = jnp.bfloat16

HEAD_DIM = 64
SB_HEADS = 8
FOX_HEADS = 8
DSA_HEADS = 8
DSA_KV_HEADS = 2
DSA_GROUP = DSA_HEADS // DSA_KV_HEADS
IDX_HEADS = 4
IDX_DIM = 64
N_BRANCH = 3
ROPE_THETA = 500000.0
ROT_DIM = HEAD_DIM // 4
TOPK_MAX = 256
NORM_EPS = 1e-6

LANES = 128
SUBLANES = 8
SUM_ROWS = 2 * SUBLANES
BRANCH_WIDTH = SB_HEADS * HEAD_DIM
ATTN_SCALE = HEAD_DIM ** -0.5
assert IDX_DIM == HEAD_DIM
IDX_W_SCALE = IDX_HEADS ** -0.5

COL_QA, COL_KA = 0, 512
COL_QB, COL_KB = 1024, 1536
COL_QC = 2048
COL_KC = 2560
COL_QI = 2816
COL_KI = 3072
MAIN_COLS = 3200
ROPE_START = COL_QC
ROW_VA, ROW_VB, ROW_VC = 0, 512, 1024
VT_ROWS = 1152
VALS_COLS = VT_ROWS + LANES
MISC_FL, MISC_WI = 0, 8

TQ = 256
TK = 256
HEADS_PER_STEP = 8
NEG_BIG = -1e30

KEY_NEG_INF = -2139095041
KEY_END = 2139095041

VMEM_LIMIT = 56 * 1024 * 1024


def _cparams(sem):
    return pltpu.CompilerParams(dimension_semantics=sem, vmem_limit_bytes=VMEM_LIMIT)


def _rmsnorm(x, g):
    y = x * lax.rsqrt(jnp.mean(x * x, axis=-1, keepdims=True) + NORM_EPS)
    return y * g


def _dot(a, b):
    return jnp.dot(a, b, preferred_element_type=F32)


def _dot_nt(a, b):
    return lax.dot_general(a, b, (((1,), (1,)), ((), ())), preferred_element_type=F32)


def _split2(x):
    hi = x.astype(MXU_DTYPE)
    lo = (x - hi.astype(F32)).astype(MXU_DTYPE)
    return hi, lo


def _split3(x):
    hi = x.astype(MXU_DTYPE)
    r = x - hi.astype(F32)
    mid = r.astype(MXU_DTYPE)
    lo = (r - mid.astype(F32)).astype(MXU_DTYPE)
    return hi, mid, lo


def _log_sigmoid_parts(z):
    sp = jnp.log(1.0 + jnp.exp(-jnp.abs(z)))
    return jnp.minimum(z, 0.0) - sp, jnp.maximum(z, 0.0) + sp


def _proj_kernel(x_ref, pos_ref, g_ref, freq_ref, sg1_ref, sg2_ref, wm_ref, wv_ref,
                 main_ref, vt_ref, misc_ref):
    h = _rmsnorm(x_ref[...], g_ref[...]).astype(MXU_DTYPE)
    ang = pos_ref[...].astype(F32) * freq_ref[...]
    cos = jnp.cos(ang)
    sin = jnp.sin(ang)
    s_up = sin * sg1_ref[...]
    s_dn = sin * sg2_ref[...]
    half = ROT_DIM // 2
    chunk = 4 * LANES
    for c0 in range(0, MAIN_COLS, chunk):
        c1 = min(c0 + chunk, MAIN_COLS)
        p = _dot(h, wm_ref[:, c0:c1])
        if c0 >= ROPE_START:
            for t in range((c1 - c0) // LANES):
                pt = p[:, t * LANES:(t + 1) * LANES]
                pt = (pt * cos + pltpu.roll(pt, LANES - half, 1) * s_up
                      + pltpu.roll(pt, half, 1) * s_dn)
                main_ref[:, c0 + t * LANES:c0 + (t + 1) * LANES] = pt.astype(main_ref.dtype)
        else:
            main_ref[:, c0:c1] = p.astype(main_ref.dtype)
    for c0 in range(0, VALS_COLS, 2 * LANES):
        pv = _dot(h, wv_ref[:, c0:c0 + 2 * LANES])
        for r0 in (c0, c0 + LANES):
            piece = pv[:, r0 - c0:r0 - c0 + LANES]
            if r0 < VT_ROWS:
                vt_ref[0, r0:r0 + LANES, :] = piece.T.astype(vt_ref.dtype)
            else:
                misc_ref[...] = piece


def _proj(x2d, pos2d, g, freq, sg1, sg2, w_main, w_v, tm, batch, seq):
    n, d = x2d.shape
    nsb = seq // tm
    const = lambda i: (0, 0)
    return pl.pallas_call(
        _proj_kernel,
        grid=(n // tm,),
        in_specs=[
            pl.BlockSpec((tm, d), lambda i: (i, 0)),
            pl.BlockSpec((tm, 1), lambda i: (i, 0)),
            pl.BlockSpec((1, d), const),
            pl.BlockSpec((1, LANES), const),
            pl.BlockSpec((1, LANES), const),
            pl.BlockSpec((1, LANES), const),
            pl.BlockSpec((d, MAIN_COLS), const),
            pl.BlockSpec((d, VALS_COLS), const),
        ],
        out_specs=[
            pl.BlockSpec((tm, MAIN_COLS), lambda i: (i, 0)),
            pl.BlockSpec((1, VT_ROWS, tm), lambda i: (i // nsb, 0, i % nsb)),
            pl.BlockSpec((tm, LANES), lambda i: (i, 0)),
        ],
        out_shape=[
            jax.ShapeDtypeStruct((n, MAIN_COLS), MXU_DTYPE),
            jax.ShapeDtypeStruct((batch, VT_ROWS, seq), MXU_DTYPE),
            jax.ShapeDtypeStruct((n, LANES), F32),
        ],
        compiler_params=_cparams(("parallel",)),
        name="proj",
    )(x2d, pos2d, g, freq, sg1, sg2, w_main, w_v)


def _cumf_kernel(misc_ref, bias_ref, c_ref):
    s = misc_ref.shape[1]
    r = lax.broadcasted_iota(jnp.int32, (TK, TK), 0)
    c = lax.broadcasted_iota(jnp.int32, (TK, TK), 1)
    incl = (c <= r).astype(MXU_DTYPE)
    carry = jnp.zeros((1, LANES), F32)
    for b in range(s // TK):
        logit = misc_ref[0, b * TK:(b + 1) * TK, :] + bias_ref[...]
        logf, _ = _log_sigmoid_parts(logit)
        hi, mid, lo = _split3(logf)
        cs = (_dot(incl, lo) + _dot(incl, mid)) + _dot(incl, hi) + carry
        c_ref[0, b * TK:(b + 1) * TK, :] = cs
        carry = cs[TK - 1:TK, :]


def _cumf(misc3d, bias_row):
    b, s, _ = misc3d.shape
    return pl.pallas_call(
        _cumf_kernel,
        grid=(b,),
        in_specs=[
            pl.BlockSpec((1, s, LANES), lambda i: (i, 0, 0)),
            pl.BlockSpec((1, LANES), lambda i: (0, 0)),
        ],
        out_specs=pl.BlockSpec((1, s, LANES), lambda i: (i, 0, 0)),
        out_shape=jax.ShapeDtypeStruct((b, s, LANES), F32),
        compiler_params=_cparams(("parallel",)),
        name="cumf",
    )(misc3d, bias_row)


def _head_q(q_tile, half):
    lane = lax.broadcasted_iota(jnp.int32, q_tile.shape, 1)
    keep = (lane // HEAD_DIM) == half
    return jnp.where(keep, q_tile.astype(F32) * ATTN_SCALE, 0.0).astype(MXU_DTYPE)


def _causal_tile_t(strict):
    key = lax.broadcasted_iota(jnp.int32, (TK, TQ), 0)
    qry = lax.broadcasted_iota(jnp.int32, (TK, TQ), 1)
    return (key < qry) if strict else (key <= qry)


def _rows(j):
    return pl.ds(pl.multiple_of(j * TK, TK), TK)


def _flash_heads(heads, n_full, diag_mask):
    def add_bias(s, bias):
        if isinstance(bias, (list, tuple)):
            return jnp.concatenate([s[:, t * LANES:(t + 1) * LANES] + bias[t] for t in range(len(bias))],
                                   axis=1)
        return s + bias

    ones = jnp.ones((SUM_ROWS, TK), MXU_DTYPE)

    def step(j, carry, mask):
        biased = []
        for (qh, k_at, _, bias_at), (m, _) in zip(heads, carry):
            s = add_bias(_dot_nt(k_at(j), qh), bias_at(j))
            if mask is not None:
                s = jnp.where(mask, s, NEG_BIG)
            biased.append((s, jnp.maximum(m, jnp.max(s, axis=0, keepdims=True))))
        stats = []
        for (s, m_new), (m, _) in zip(biased, carry):
            stats.append((m_new, jnp.exp(m - m_new), jnp.exp(s - m_new).astype(MXU_DTYPE)))
        out = []
        for (m_new, alpha, p), (_, _, vt_at, _), (_, acc) in zip(stats, heads, carry):
            vt1 = jnp.concatenate([vt_at(j), ones], axis=0)
            out.append((m_new, alpha * acc + _dot(vt1, p)))
        return tuple(out)

    init = tuple((jnp.full((1, TQ), NEG_BIG, F32), jnp.zeros((HEAD_DIM + SUM_ROWS, TQ), F32)) for _ in heads)
    carry = lax.fori_loop(0, n_full, lambda j, c: step(j, c, None), init)
    carry = step(n_full, carry, diag_mask)
    return [acc[:HEAD_DIM] / acc[HEAD_DIM:HEAD_DIM + 1] for (_, acc) in carry]


def _store_heads_t(o_ref, outs_t):
    for t in range(len(outs_t) // 2):
        pair_t = jnp.concatenate([outs_t[2 * t], outs_t[2 * t + 1]], axis=0)
        o_ref[0, :, t * LANES:(t + 1) * LANES] = pair_t.T.astype(o_ref.dtype)


def _sb_kernel(q_ref, k_ref, vt_ref, o_ref):
    i = pl.program_id(2)
    r = lax.broadcasted_iota(jnp.int32, (TK, TK), 0)
    c = lax.broadcasted_iota(jnp.int32, (TK, TK), 1)
    after = (c > r).astype(MXU_DTYPE)
    after2 = jnp.concatenate([after, after], axis=1)
    strict = _causal_tile_t(True)
    n_heads = HEADS_PER_STEP
    qhs = [_head_q(q_ref[0][:, (h // 2) * LANES:(h // 2 + 1) * LANES], h % 2) for h in range(n_heads)]

    def step(j, carry, mask):
        zs = [_dot_nt(k_ref[0, _rows(j), (h // 2) * LANES:(h // 2 + 1) * LANES], qhs[h])
              for h in range(n_heads)]
        parts = []
        for z in zs:
            log_beta, neg_l1m = _log_sigmoid_parts(z)
            if mask is not None:
                neg_l1m = jnp.where(mask, neg_l1m, 0.0)
            hi, lo = _split2(neg_l1m)
            parts.append((log_beta, neg_l1m[0:1, :], jnp.concatenate([hi, lo], axis=0)))
        laters = [_dot(after2, hl) for (_, _, hl) in parts]
        ws = []
        for (log_beta, _, _), later, (_, tail) in zip(parts, laters, carry):
            w = jnp.exp(log_beta - (later + tail))
            if mask is not None:
                w = jnp.where(mask, w, 0.0)
            ws.append(w.astype(MXU_DTYPE))
        out = []
        for h, (w, (_, first, _), later, (acc, tail)) in enumerate(zip(ws, parts, laters, carry)):
            vtj = vt_ref[0, h * HEAD_DIM:(h + 1) * HEAD_DIM, _rows(j)]
            out.append((acc + _dot(vtj, w), tail + (first + later[0:1, :])))
        return tuple(out)

    carry = tuple((jnp.zeros((HEAD_DIM, TQ), F32), jnp.zeros((1, TQ), F32)) for _ in range(n_heads))
    carry = step(i, carry, strict)
    carry = lax.fori_loop(0, i, lambda t, cr: step(i - 1 - t, cr, None), carry)
    _store_heads_t(o_ref, [acc for acc, _ in carry])


def _sb(main3d, vt3d):
    b, s, _ = main3d.shape
    nq = s // TQ
    w = HEADS_PER_STEP * HEAD_DIM
    qb, kb, vb = COL_QA // w, COL_KA // w, ROW_VA // w
    return pl.pallas_call(
        _sb_kernel,
        grid=(b, SB_HEADS // HEADS_PER_STEP, nq),
        in_specs=[
            pl.BlockSpec((1, TQ, w), lambda bi, hg, i: (bi, i, qb + hg)),
            pl.BlockSpec((1, s, w), lambda bi, hg, i: (bi, 0, kb + hg)),
            pl.BlockSpec((1, w, s), lambda bi, hg, i: (bi, vb + hg, 0)),
        ],
        out_specs=pl.BlockSpec((1, TQ, w), lambda bi, hg, i: (bi, i, hg)),
        out_shape=jax.ShapeDtypeStruct((b, s, BRANCH_WIDTH), MXU_DTYPE),
        compiler_params=_cparams(("parallel", "parallel", "arbitrary")),
        name="sb",
    )(main3d, main3d, vt3d)


def _fox_kernel(q_ref, k_ref, vt_ref, c_ref, o_ref, cb_ref):
    hg = pl.program_id(1)
    i = pl.program_id(2)
    n_heads = HEADS_PER_STEP

    @pl.when(i == 0)
    def _():
        c = c_ref[0]
        lane = lax.broadcasted_iota(jnp.int32, c.shape, 1)
        for h in range(n_heads):
            col = jnp.sum(jnp.where(lane == hg * n_heads + h, c, 0.0), axis=1, keepdims=True)
            cb_ref[h] = jnp.broadcast_to(-col, c.shape)

    heads = []
    for h in range(n_heads):
        t = h // 2
        qh = _head_q(q_ref[0][:, t * LANES:(t + 1) * LANES], h % 2)
        k_at = lambda j, t=t: k_ref[0, _rows(j), t * LANES:(t + 1) * LANES]
        vt_at = lambda j, h=h: vt_ref[0, h * HEAD_DIM:(h + 1) * HEAD_DIM, _rows(j)]
        bias_at = lambda j, h=h: [cb_ref[h, _rows(j), :]] * (TQ // LANES)
        heads.append((qh, k_at, vt_at, bias_at))
    _store_heads_t(o_ref, _flash_heads(heads, i, _causal_tile_t(False)))


def _fox(main3d, vt3d, c3d):
    b, s, _ = main3d.shape
    nq = s // TQ
    w = HEADS_PER_STEP * HEAD_DIM
    qb, kb, vb = COL_QB // w, COL_KB // w, ROW_VB // w
    return pl.pallas_call(
        _fox_kernel,
        grid=(b, FOX_HEADS // HEADS_PER_STEP, nq),
        in_specs=[
            pl.BlockSpec((1, TQ, w), lambda bi, hg, i: (bi, i, qb + hg)),
            pl.BlockSpec((1, s, w), lambda bi, hg, i: (bi, 0, kb + hg)),
            pl.BlockSpec((1, w, s), lambda bi, hg, i: (bi, vb + hg, 0)),
            pl.BlockSpec((1, s, LANES), lambda bi, hg, i: (bi, 0, 0)),
        ],
        out_specs=pl.BlockSpec((1, TQ, w), lambda bi, hg, i: (bi, i, hg)),
        out_shape=jax.ShapeDtypeStruct((b, s, BRANCH_WIDTH), MXU_DTYPE),
        scratch_shapes=[pltpu.VMEM((HEADS_PER_STEP, s, LANES), F32)],
        compiler_params=_cparams(("parallel", "parallel", "arbitrary")),
        name="fox",
    )(main3d, main3d, vt3d, c3d)


def _key_to_f32(key):
    bits = key ^ ((key >> 31) & jnp.int32(0x7FFFFFFF))
    return lax.bitcast_convert_type(bits, F32)


def _dsa_kernel(q_ref, k_ref, vt_ref, qi_ref, ki_ref, misc_ref, o_ref, score_ref, bias_ref, *, top_k):
    i = pl.program_id(1)
    diag = _causal_tile_t(False)
    seq = score_ref.shape[0]

    wi_t = misc_ref[0].T[MISC_WI:MISC_WI + IDX_HEADS, :] * IDX_W_SCALE
    qih = [_head_q(qi_ref[0][:, (h // 2) * LANES:(h // 2 + 1) * LANES], h % 2) for h in range(IDX_HEADS)]

    def score_block(j, mask):
        kij = ki_ref[0, _rows(j), :]
        sc = None
        for h in range(IDX_HEADS):
            term = wi_t[h:h + 1, :] * jnp.maximum(_dot_nt(kij, qih[h]), 0.0)
            sc = term if sc is None else sc + term
        if mask is not None:
            sc = jnp.where(mask, sc, -jnp.inf)
        score_ref[_rows(j), :] = sc

    def score_loop(j, _):
        score_block(j, None)
        return 0

    lax.fori_loop(0, i, score_loop, 0)
    score_block(i, diag)

    def count(pred):
        def body(j, acc):
            hit = pred(score_ref[_rows(j), :], j).astype(jnp.int32)
            return acc + jnp.sum(hit.reshape(TK // SUBLANES, SUBLANES, TQ), axis=0)
        acc = lax.fori_loop(0, i + 1, body, jnp.zeros((SUBLANES, TQ), jnp.int32))
        return jnp.sum(acc, axis=0, keepdims=True)

    def bisect(_, lh):
        lo, hi = lh
        mid = (lo >> 1) + (hi >> 1) + (lo & hi & 1)
        thr = _key_to_f32(mid)
        ok = count(lambda s, j: s >= thr) >= top_k
        return jnp.where(ok, mid, lo), jnp.where(ok, hi, mid)

    lo0 = jnp.full((1, TQ), KEY_NEG_INF, jnp.int32)
    hi0 = jnp.full((1, TQ), KEY_END, jnp.int32)
    lo, _ = lax.fori_loop(0, 32, bisect, (lo0, hi0))
    thr = _key_to_f32(lo)

    need = (top_k - count(lambda s, j: s > thr)).astype(F32)
    kr = lax.broadcasted_iota(jnp.int32, (TK, TK), 0)
    kc = lax.broadcasted_iota(jnp.int32, (TK, TK), 1)
    upto = (kc <= kr).astype(MXU_DTYPE)

    def bias_block(j, mask, seen):
        s = score_ref[_rows(j), :]
        tie = s == thr
        rank = _dot(upto, jnp.where(tie, 1.0, 0.0).astype(MXU_DTYPE)) + seen
        sel = (s > thr) | (tie & (rank <= need))
        if mask is not None:
            sel = sel & mask
        bias_ref[_rows(j), :] = jnp.where(sel, 0.0, NEG_BIG)
        return rank[TK - 1:TK, :]

    seen = lax.fori_loop(0, i, lambda j, sn: bias_block(j, None, sn), jnp.zeros((1, TQ), F32))
    bias_block(i, diag, seen)

    bias_at = lambda j: bias_ref[_rows(j), :]
    heads = []
    for h in range(DSA_HEADS):
        g = h // DSA_GROUP
        k_at = lambda j, g=g: k_ref[0, _rows(j), g * LANES:(g + 1) * LANES]
        vt_at = lambda j, g=g: vt_ref[0, g * HEAD_DIM:(g + 1) * HEAD_DIM, _rows(j)]
        qh = _head_q(q_ref[0][:, (h // 2) * LANES:(h // 2 + 1) * LANES], h % 2)
        heads.append((qh, k_at, vt_at, bias_at))
    _store_heads_t(o_ref, _flash_heads(heads, i, None))


def _dsa(main3d, vt3d, misc3d, top_k):
    b, s, _ = main3d.shape
    nq = s // TQ
    assert top_k <= TK, "the first key block must hold at least top_k keys"
    return pl.pallas_call(
        functools.partial(_dsa_kernel, top_k=top_k),
        grid=(b, nq),
        in_specs=[
            pl.BlockSpec((1, TQ, 4 * LANES), lambda bi, i: (bi, i, COL_QC // (4 * LANES))),
            pl.BlockSpec((1, s, 2 * LANES), lambda bi, i: (bi, 0, COL_KC // (2 * LANES))),
            pl.BlockSpec((1, LANES, s), lambda bi, i: (bi, ROW_VC // LANES, 0)),
            pl.BlockSpec((1, TQ, 2 * LANES), lambda bi, i: (bi, i, COL_QI // (2 * LANES))),
            pl.BlockSpec((1, s, LANES), lambda bi, i: (bi, 0, COL_KI // LANES)),
            pl.BlockSpec((1, TQ, LANES), lambda bi, i: (bi, i, 0)),
        ],
        out_specs=pl.BlockSpec((1, TQ, BRANCH_WIDTH), lambda bi, i: (bi, i, 0)),
        out_shape=jax.ShapeDtypeStruct((b, s, BRANCH_WIDTH), MXU_DTYPE),
        scratch_shapes=[pltpu.VMEM((s, TQ), F32), pltpu.VMEM((s, TQ), F32)],
        compiler_params=_cparams(("parallel", "arbitrary")),
        name="dsa",
    )(main3d, main3d, vt3d, main3d, main3d, misc3d)


def _merge_kernel(x_ref, a_ref, b_ref, c_ref, g_ref, wg_ref, wb_ref, wo_ref, o_ref):
    x = x_ref[...]
    d = x.shape[1]
    h = _rmsnorm(x, g_ref[...]).astype(MXU_DTYPE)
    y = None
    for n, br_ref in enumerate((a_ref, b_ref, c_ref)):
        gate = jax.nn.sigmoid(_dot(h, wg_ref[:, n * d:(n + 1) * d]))
        term = gate * _dot(br_ref[...], wb_ref[n])
        y = term if y is None else y + term
    o_ref[...] = x + _dot(y.astype(MXU_DTYPE), wo_ref[...])


def _merge(x2d, br_a, br_b, br_c, g, w_gate, w_branch, w_out, tm):
    n, d = x2d.shape
    bw = br_a.shape[1]
    row = lambda i: (i, 0)
    const = lambda i: (0, 0)
    return pl.pallas_call(
        _merge_kernel,
        grid=(n // tm,),
        in_specs=[
            pl.BlockSpec((tm, d), row),
            pl.BlockSpec((tm, bw), row),
            pl.BlockSpec((tm, bw), row),
            pl.BlockSpec((tm, bw), row),
            pl.BlockSpec((1, d), const),
            pl.BlockSpec((d, N_BRANCH * d), const),
            pl.BlockSpec((N_BRANCH, bw, d), lambda i: (0, 0, 0)),
            pl.BlockSpec((d, d), const),
        ],
        out_specs=pl.BlockSpec((tm, d), row),
        out_shape=jax.ShapeDtypeStruct((n, d), F32),
        compiler_params=_cparams(("parallel",)),
        name="merge",
    )(x2d, br_a, br_b, br_c, g, w_gate, w_branch, w_out)


def _mlp_kernel(x_ref, g_ref, wu_ref, wd_ref, gf_ref, o_ref, *, ff_chunk, final_norm):
    x = x_ref[...]
    h = _rmsnorm(x, g_ref[...]).astype(MXU_DTYPE)
    acc = x
    for c in range(wu_ref.shape[1] // ff_chunk):
        u = jnp.maximum(_dot(h, wu_ref[:, c * ff_chunk:(c + 1) * ff_chunk]), 0.0)
        acc = acc + _dot((u * u).astype(MXU_DTYPE), wd_ref[c * ff_chunk:(c + 1) * ff_chunk, :])
    if final_norm:
        acc = _rmsnorm(acc, gf_ref[...])
    o_ref[...] = acc


def _mlp(x2d, g, w_up, w_down, g_final, tm, final_norm):
    n, d = x2d.shape
    ff = w_up.shape[1]
    row = lambda i: (i, 0)
    const = lambda i: (0, 0)
    return pl.pallas_call(
        functools.partial(_mlp_kernel, ff_chunk=min(ff, 1024), final_norm=final_norm),
        grid=(n // tm,),
        in_specs=[
            pl.BlockSpec((tm, d), row),
            pl.BlockSpec((1, d), const),
            pl.BlockSpec((d, ff), const),
            pl.BlockSpec((ff, d), const),
            pl.BlockSpec((1, d), const),
        ],
        out_specs=pl.BlockSpec((tm, d), row),
        out_shape=jax.ShapeDtypeStruct((n, d), F32),
        compiler_params=_cparams(("parallel",)),
        name="mlp",
    )(x2d, g, w_up, w_down, g_final)


def _w_in_columns():
    sizes = (("qa", BRANCH_WIDTH), ("ka", BRANCH_WIDTH), ("va", BRANCH_WIDTH),
             ("qb", BRANCH_WIDTH), ("kb", BRANCH_WIDTH), ("vb", BRANCH_WIDTH), ("fl", FOX_HEADS),
             ("qc", DSA_HEADS * HEAD_DIM), ("kc", DSA_KV_HEADS * HEAD_DIM), ("vc", DSA_KV_HEADS * HEAD_DIM),
             ("qi", IDX_HEADS * IDX_DIM), ("ki", IDX_DIM), ("wi", IDX_HEADS), ("gates", N_BRANCH * 1024))
    cols, o = {}, 0
    for name, width in sizes:
        cols[name] = (o, o + width)
        o += width
    return cols, o


def _pack_kernel(w_ref, main_ref, vals_ref, gates_ref):
    cols, _ = _w_in_columns()

    def src(name, lo=0, hi=None):
        a, b = cols[name]
        return w_ref[0, :, a + lo:(b if hi is None else a + hi)]

    def put(ref, start, value):
        ref[0, :, start:start + value.shape[1]] = value.astype(ref.dtype)

    put(main_ref, COL_QA, src("qa"))
    put(main_ref, COL_KA, src("ka"))
    put(main_ref, COL_QB, src("qb"))
    put(main_ref, COL_KB, src("kb"))
    put(main_ref, COL_QC, src("qc"))
    for g in range(DSA_KV_HEADS):
        kg = src("kc", g * HEAD_DIM, (g + 1) * HEAD_DIM)
        put(main_ref, COL_KC + g * LANES, jnp.concatenate([kg, kg], axis=1))
    put(main_ref, COL_QI, src("qi"))
    ki = src("ki")
    put(main_ref, COL_KI, jnp.concatenate([ki, ki], axis=1))
    put(vals_ref, ROW_VA, src("va"))
    put(vals_ref, ROW_VB, src("vb"))
    put(vals_ref, ROW_VC, src("vc"))
    pad = jnp.zeros((w_ref.shape[1], LANES - FOX_HEADS - IDX_HEADS), F32)
    put(vals_ref, VT_ROWS, jnp.concatenate([src("fl"), src("wi"), pad], axis=1))
    put(gates_ref, 0, src("gates"))


def _pack_w_in(w_in):
    depth, d, total = w_in.shape
    cols, end = _w_in_columns()
    assert end == total and cols["gates"][1] - cols["gates"][0] == N_BRANCH * d
    rows = 256
    blk = lambda width: pl.BlockSpec((1, rows, width), lambda l, r: (l, r, 0))
    widths = (MAIN_COLS, VALS_COLS, N_BRANCH * d)
    return pl.pallas_call(
        _pack_kernel,
        grid=(depth, d // rows),
        in_specs=[blk(total)],
        out_specs=[blk(wd) for wd in widths],
        out_shape=[jax.ShapeDtypeStruct((depth, d, wd), MXU_DTYPE) for wd in widths],
        compiler_params=_cparams(("parallel", "parallel")),
        name="pack",
    )(w_in)


def _rope_rows():
    lane = jnp.arange(LANES) % HEAD_DIM
    half = ROT_DIM // 2
    inv_freq = ROPE_THETA ** (-jnp.arange(0, ROT_DIM, 2, dtype=F32) / ROT_DIM)
    freq = jnp.where(lane < ROT_DIM, inv_freq[lane % half], 0.0).astype(F32)
    sg1 = jnp.where(lane < half, -1.0, 0.0).astype(F32)
    sg2 = jnp.where((lane >= half) & (lane < ROT_DIM), 1.0, 0.0).astype(F32)
    return freq[None, :], sg1[None, :], sg2[None, :]


def kernel(x, positions, g_mix, w_in, b_forget, w_branch, w_out, g_mlp, w_up, w_down, g_final):
    b, s, d = x.shape
    n = b * s
    depth = w_in.shape[0]
    top_k = min(TOPK_MAX, s // 4)
    tm = min(512, s)
    freq, sg1, sg2 = _rope_rows()
    pos2d = positions.reshape(n, 1)
    x2d = x.reshape(n, d)
    w_main_all, w_v_all, w_gate_all = _pack_w_in(w_in)
    for layer in range(depth):
        w_main, w_v, w_gate = w_main_all[layer], w_v_all[layer], w_gate_all[layer]
        bias_row = jnp.concatenate([b_forget[layer], jnp.zeros((LANES - FOX_HEADS,), F32)])[None, :]
        main, vt, misc = _proj(x2d, pos2d, g_mix[layer][None, :], freq, sg1, sg2, w_main, w_v, tm, b, s)
        main3d = main.reshape(b, s, MAIN_COLS)
        misc3d = misc.reshape(b, s, LANES)
        cum = _cumf(misc3d, bias_row)
        br_a = _sb(main3d, vt).reshape(n, BRANCH_WIDTH)
        br_b = _fox(main3d, vt, cum).reshape(n, BRANCH_WIDTH)
        br_c = _dsa(main3d, vt, misc3d, top_k).reshape(n, BRANCH_WIDTH)
        x2d = _merge(x2d, br_a, br_b, br_c, g_mix[layer][None, :], w_gate,
                     w_branch[layer].astype(MXU_DTYPE), w_out[layer].astype(MXU_DTYPE), tm)
        x2d = _mlp(x2d, g_mlp[layer][None, :], w_up[layer].astype(MXU_DTYPE),
                   w_down[layer].astype(MXU_DTYPE), g_final[None, :], tm, layer == depth - 1)
    return x2d.reshape(b, s, d)
```

```python
import functools

import jax
import jax.numpy as jnp
from jax import lax
from jax.experimental import pallas as pl
from jax.experimental.pallas import tpu as pltpu

F32 = jnp.float32
MXU_DTYPE = jnp.bfloat16

HEAD_DIM = 64
SB_HEADS = 8
FOX_HEADS = 8
DSA_HEADS = 8
DSA_KV_HEADS = 2
DSA_GROUP = DSA_HEADS // DSA_KV_HEADS
IDX_HEADS = 4
IDX_DIM = 64
N_BRANCH = 3
ROPE_THETA = 500000.0
ROT_DIM = HEAD_DIM // 4
TOPK_MAX = 256
NORM_EPS = 1e-6

LANES = 128
SUBLANES = 8
SUM_ROWS = 2 * SUBLANES
BRANCH_WIDTH = SB_HEADS * HEAD_DIM
ATTN_SCALE = HEAD_DIM ** -0.5
assert IDX_DIM == HEAD_DIM
IDX_W_SCALE = IDX_HEADS ** -0.5

COL_QA, COL_KA = 0, 512
COL_QB, COL_KB = 1024, 1536
COL_QC = 2048
COL_KC = 2560
COL_QI = 2816
COL_KI = 3072
MAIN_COLS = 3200
ROPE_START = COL_QC
ROW_VA, ROW_VB, ROW_VC = 0, 512, 1024
VT_ROWS = 1152
VALS_COLS = VT_ROWS + LANES
MISC_FL, MISC_WI = 0, 8

TQ = 256
TK = 256
HEADS_PER_STEP = 8
NEG_BIG = -1e30

KEY_NEG_INF = -2139095041
KEY_END = 2139095041

VMEM_LIMIT = 56 * 1024 * 1024


def _cparams(sem):
    return pltpu.CompilerParams(dimension_semantics=sem, vmem_limit_bytes=VMEM_LIMIT)


def _rmsnorm(x, g):
    y = x * lax.rsqrt(jnp.mean(x * x, axis=-1, keepdims=True) + NORM_EPS)
    return y * g


def _dot(a, b):
    return jnp.dot(a, b, preferred_element_type=F32)


def _dot_nt(a, b):
    return lax.dot_general(a, b, (((1,), (1,)), ((), ())), preferred_element_type=F32)


def _split2(x):
    hi = x.astype(MXU_DTYPE)
    lo = (x - hi.astype(F32)).astype(MXU_DTYPE)
    return hi, lo


def _split3(x):
    hi = x.astype(MXU_DTYPE)
    r = x - hi.astype(F32)
    mid = r.astype(MXU_DTYPE)
    lo = (r - mid.astype(F32)).astype(MXU_DTYPE)
    return hi, mid, lo


def _log_sigmoid_parts(z):
    sp = jnp.log(1.0 + jnp.exp(-jnp.abs(z)))
    return jnp.minimum(z, 0.0) - sp, jnp.maximum(z, 0.0) + sp


def _proj_kernel(x_ref, pos_ref, g_ref, freq_ref, sg1_ref, sg2_ref, wm_ref, wv_ref,
                 main_ref, vt_ref, misc_ref):
    h = _rmsnorm(x_ref[...], g_ref[...]).astype(MXU_DTYPE)
    ang = pos_ref[...].astype(F32) * freq_ref[...]
    cos = jnp.cos(ang)
    sin = jnp.sin(ang)
    s_up = sin * sg1_ref[...]
    s_dn = sin * sg2_ref[...]
    half = ROT_DIM // 2
    chunk = 4 * LANES
    for c0 in range(0, MAIN_COLS, chunk):
        c1 = min(c0 + chunk, MAIN_COLS)
        p = _dot(h, wm_ref[:, c0:c1])
        if c0 >= ROPE_START:
            for t in range((c1 - c0) // LANES):
                pt = p[:, t * LANES:(t + 1) * LANES]
                pt = (pt * cos + pltpu.roll(pt, LANES - half, 1) * s_up
                      + pltpu.roll(pt, half, 1) * s_dn)
                main_ref[:, c0 + t * LANES:c0 + (t + 1) * LANES] = pt.astype(main_ref.dtype)
        else:
            main_ref[:, c0:c1] = p.astype(main_ref.dtype)
    for c0 in range(0, VALS_COLS, 2 * LANES):
        pv = _dot(h, wv_ref[:, c0:c0 + 2 * LANES])
        for r0 in (c0, c0 + LANES):
            piece = pv[:, r0 - c0:r0 - c0 + LANES]
            if r0 < VT_ROWS:
                vt_ref[0, r0:r0 + LANES, :] = piece.T.astype(vt_ref.dtype)
            else:
                misc_ref[...] = piece


def _proj(x2d, pos2d, g, freq, sg1, sg2, w_main, w_v, tm, batch, seq):
    n, d = x2d.shape
    nsb = seq // tm
    const = lambda i: (0, 0)
    return pl.pallas_call(
        _proj_kernel,
        grid=(n // tm,),
        in_specs=[
            pl.BlockSpec((tm, d), lambda i: (i, 0)),
            pl.BlockSpec((tm, 1), lambda i: (i, 0)),
            pl.BlockSpec((1, d), const),
            pl.BlockSpec((1, LANES), const),
            pl.BlockSpec((1, LANES), const),
            pl.BlockSpec((1, LANES), const),
            pl.BlockSpec((d, MAIN_COLS), const),
            pl.BlockSpec((d, VALS_COLS), const),
        ],
        out_specs=[
            pl.BlockSpec((tm, MAIN_COLS), lambda i: (i, 0)),
            pl.BlockSpec((1, VT_ROWS, tm), lambda i: (i // nsb, 0, i % nsb)),
            pl.BlockSpec((tm, LANES), lambda i: (i, 0)),
        ],
        out_shape=[
            jax.ShapeDtypeStruct((n, MAIN_COLS), MXU_DTYPE),
            jax.ShapeDtypeStruct((batch, VT_ROWS, seq), MXU_DTYPE),
            jax.ShapeDtypeStruct((n, LANES), F32),
        ],
        compiler_params=_cparams(("parallel",)),
        name="proj",
    )(x2d, pos2d, g, freq, sg1, sg2, w_main, w_v)


def _cumf_kernel(misc_ref, bias_ref, c_ref):
    s = misc_ref.shape[1]
    r = lax.broadcasted_iota(jnp.int32, (TK, TK), 0)
    c = lax.broadcasted_iota(jnp.int32, (TK, TK), 1)
    incl = (c <= r).astype(MXU_DTYPE)
    carry = jnp.zeros((1, LANES), F32)
    for b in range(s // TK):
        logit = misc_ref[0, b * TK:(b + 1) * TK, :] + bias_ref[...]
        logf, _ = _log_sigmoid_parts(logit)
        hi, mid, lo = _split3(logf)
        cs = (_dot(incl, lo) + _dot(incl, mid)) + _dot(incl, hi) + carry
        c_ref[0, b * TK:(b + 1) * TK, :] = cs
        carry = cs[TK - 1:TK, :]


def _cumf(misc3d, bias_row):
    b, s, _ = misc3d.shape
    return pl.pallas_call(
        _cumf_kernel,
        grid=(b,),
        in_specs=[
            pl.BlockSpec((1, s, LANES), lambda i: (i, 0, 0)),
            pl.BlockSpec((1, LANES), lambda i: (0, 0)),
        ],
        out_specs=pl.BlockSpec((1, s, LANES), lambda i: (i, 0, 0)),
        out_shape=jax.ShapeDtypeStruct((b, s, LANES), F32),
        compiler_params=_cparams(("parallel",)),
        name="cumf",
    )(misc3d, bias_row)


def _head_q(q_tile, half):
    lane = lax.broadcasted_iota(jnp.int32, q_tile.shape, 1)
    keep = (lane // HEAD_DIM) == half
    return jnp.where(keep, q_tile.astype(F32) * ATTN_SCALE, 0.0).astype(MXU_DTYPE)


def _causal_tile_t(strict):
    key = lax.broadcasted_iota(jnp.int32, (TK, TQ), 0)
    qry = lax.broadcasted_iota(jnp.int32, (TK, TQ), 1)
    return (key < qry) if strict else (key <= qry)


def _rows(j):
    return pl.ds(pl.multiple_of(j * TK, TK), TK)


def _lookahead_loop(n_last, bufs, issue, step, carry):
    buf_a, buf_b = bufs

    def pair(t, carry):
        n = 2 * t
        issue(n + 1, buf_b)
        carry = step(n, buf_a, carry, False)
        issue(n + 2, buf_a)
        return step(n + 1, buf_b, carry, False)

    def odd_tail(carry):
        issue(n_last, buf_b)
        carry = step(n_last - 1, buf_a, carry, False)
        return step(n_last, buf_b, carry, True)

    def even_tail(carry):
        return step(n_last, buf_a, carry, True)

    issue(0, buf_a)
    carry = lax.fori_loop(0, n_last // 2, pair, carry)
    return lax.cond(n_last % 2 == 1, odd_tail, even_tail, carry)


def _flash_heads(heads, n_full, diag_mask, z_bufs):
    def add_bias(s, bias):
        if isinstance(bias, (list, tuple)):
            return jnp.concatenate([s[:, t * LANES:(t + 1) * LANES] + bias[t] for t in range(len(bias))],
                                   axis=1)
        return s + bias

    ones = jnp.ones((SUM_ROWS, TK), MXU_DTYPE)

    def issue(j, buf):
        for h, (qh, k_at, _, _) in enumerate(heads):
            buf[h] = _dot_nt(k_at(j), qh)

    def step(j, buf, carry, mask):
        stats = []
        for h, ((_, _, _, bias_at), (m, _)) in enumerate(zip(heads, carry)):
            s = add_bias(buf[h], bias_at(j))
            if mask is not None:
                s = jnp.where(mask, s, NEG_BIG)
            m_new = jnp.maximum(m, jnp.max(s, axis=0, keepdims=True))
            stats.append((m_new, jnp.exp(m - m_new), jnp.exp(s - m_new).astype(MXU_DTYPE)))
        out = []
        for (m_new, alpha, p), (_, _, vt_at, _), (_, acc) in zip(stats, heads, carry):
            vt1 = jnp.concatenate([vt_at(j), ones], axis=0)
            out.append((m_new, alpha * acc + _dot(vt1, p)))
        return tuple(out)

    init = tuple((jnp.full((1, TQ), NEG_BIG, F32), jnp.zeros((HEAD_DIM + SUM_ROWS, TQ), F32)) for _ in heads)
    carry = _lookahead_loop(n_full, z_bufs, issue,
                            lambda j, buf, carry, last: step(j, buf, carry, diag_mask if last else None), init)
    return [acc[:HEAD_DIM] / acc[HEAD_DIM:HEAD_DIM + 1] for (_, acc) in carry]


def _store_heads_t(o_ref, outs_t):
    for t in range(len(outs_t) // 2):
        pair_t = jnp.concatenate([outs_t[2 * t], outs_t[2 * t + 1]], axis=0)
        o_ref[0, :, t * LANES:(t + 1) * LANES] = pair_t.T.astype(o_ref.dtype)


def _sb_kernel(q_ref, k_ref, vt_ref, o_ref, za_ref, zb_ref):
    i = pl.program_id(2)
    r = lax.broadcasted_iota(jnp.int32, (TK, TK), 0)
    c = lax.broadcasted_iota(jnp.int32, (TK, TK), 1)
    after = (c > r).astype(MXU_DTYPE)
    after2 = jnp.concatenate([after, after], axis=1)
    strict = _causal_tile_t(True)
    n_heads = HEADS_PER_STEP
    qhs = [_head_q(q_ref[0][:, (h // 2) * LANES:(h // 2 + 1) * LANES], h % 2) for h in range(n_heads)]

    def issue(j, buf):
        for h in range(n_heads):
            buf[h] = _dot_nt(k_ref[0, _rows(j), (h // 2) * LANES:(h // 2 + 1) * LANES], qhs[h])

    def step(j, buf, carry, mask):
        rows = _rows(j)
        parts = []
        for h in range(n_heads):
            log_beta, neg_l1m = _log_sigmoid_parts(buf[h])
            if mask is not None:
                neg_l1m = jnp.where(mask, neg_l1m, 0.0)
            hi, lo = _split2(neg_l1m)
            parts.append((log_beta, neg_l1m[0:1, :], jnp.concatenate([hi, lo], axis=0)))
        laters = [_dot(after2, hl) for (_, _, hl) in parts]
        ws = []
        for (log_beta, _, _), later, (_, tail) in zip(parts, laters, carry):
            w = jnp.exp(log_beta - (later + tail))
            if mask is not None:
                w = jnp.where(mask, w, 0.0)
            ws.append(w.astype(MXU_DTYPE))
        out = []
        for h, (w, (_, first, _), later, (acc, tail)) in enumerate(zip(ws, parts, laters, carry)):
            vtj = vt_ref[0, h * HEAD_DIM:(h + 1) * HEAD_DIM, rows]
            out.append((acc + _dot(vtj, w), tail + (first + later[0:1, :])))
        return tuple(out)

    carry = tuple((jnp.zeros((HEAD_DIM, TQ), F32), jnp.zeros((1, TQ), F32)) for _ in range(n_heads))
    issue(i, za_ref)
    carry = step(i, za_ref, carry, strict)
    carry = lax.cond(
        i > 0,
        lambda cr: _lookahead_loop(i - 1, (za_ref, zb_ref), lambda n, buf: issue(i - 1 - n, buf),
                                   lambda n, buf, c, last: step(i - 1 - n, buf, c, None), cr),
        lambda cr: cr, carry)
    _store_heads_t(o_ref, [acc for acc, _ in carry])


def _sb(main3d, vt3d):
    b, s, _ = main3d.shape
    nq = s // TQ
    w = HEADS_PER_STEP * HEAD_DIM
    qb, kb, vb = COL_QA // w, COL_KA // w, ROW_VA // w
    return pl.pallas_call(
        _sb_kernel,
        grid=(b, SB_HEADS // HEADS_PER_STEP, nq),
        in_specs=[
            pl.BlockSpec((1, TQ, w), lambda bi, hg, i: (bi, i, qb + hg)),
            pl.BlockSpec((1, s, w), lambda bi, hg, i: (bi, 0, kb + hg)),
            pl.BlockSpec((1, w, s), lambda bi, hg, i: (bi, vb + hg, 0)),
        ],
        out_specs=pl.BlockSpec((1, TQ, w), lambda bi, hg, i: (bi, i, hg)),
        out_shape=jax.ShapeDtypeStruct((b, s, BRANCH_WIDTH), MXU_DTYPE),
        scratch_shapes=[pltpu.VMEM((HEADS_PER_STEP, TK, TQ), F32), pltpu.VMEM((HEADS_PER_STEP, TK, TQ), F32)],
        compiler_params=_cparams(("parallel", "parallel", "arbitrary")),
        name="sb",
    )(main3d, main3d, vt3d)


def _fox_kernel(q_ref, k_ref, vt_ref, c_ref, o_ref, cb_ref, za_ref, zb_ref):
    hg = pl.program_id(1)
    i = pl.program_id(2)
    n_heads = HEADS_PER_STEP

    @pl.when(i == 0)
    def _():
        c = c_ref[0]
        lane = lax.broadcasted_iota(jnp.int32, c.shape, 1)
        for h in range(n_heads):
            col = jnp.sum(jnp.where(lane == hg * n_heads + h, c, 0.0), axis=1, keepdims=True)
            cb_ref[h] = jnp.broadcast_to(-col, c.shape)

    heads = []
    for h in range(n_heads):
        t = h // 2
        qh = _head_q(q_ref[0][:, t * LANES:(t + 1) * LANES], h % 2)
        k_at = lambda j, t=t: k_ref[0, _rows(j), t * LANES:(t + 1) * LANES]
        vt_at = lambda j, h=h: vt_ref[0, h * HEAD_DIM:(h + 1) * HEAD_DIM, _rows(j)]
        bias_at = lambda j, h=h: [cb_ref[h, _rows(j), :]] * (TQ // LANES)
        heads.append((qh, k_at, vt_at, bias_at))
    _store_heads_t(o_ref, _flash_heads(heads, i, _causal_tile_t(False), (za_ref, zb_ref)))


def _fox(main3d, vt3d, c3d):
    b, s, _ = main3d.shape
    nq = s // TQ
    w = HEADS_PER_STEP * HEAD_DIM
    qb, kb, vb = COL_QB // w, COL_KB // w, ROW_VB // w
    return pl.pallas_call(
        _fox_kernel,
        grid=(b, FOX_HEADS // HEADS_PER_STEP, nq),
        in_specs=[
            pl.BlockSpec((1, TQ, w), lambda bi, hg, i: (bi, i, qb + hg)),
            pl.BlockSpec((1, s, w), lambda bi, hg, i: (bi, 0, kb + hg)),
            pl.BlockSpec((1, w, s), lambda bi, hg, i: (bi, vb + hg, 0)),
            pl.BlockSpec((1, s, LANES), lambda bi, hg, i: (bi, 0, 0)),
        ],
        out_specs=pl.BlockSpec((1, TQ, w), lambda bi, hg, i: (bi, i, hg)),
        out_shape=jax.ShapeDtypeStruct((b, s, BRANCH_WIDTH), MXU_DTYPE),
        scratch_shapes=[pltpu.VMEM((HEADS_PER_STEP, s, LANES), F32),
                        pltpu.VMEM((HEADS_PER_STEP, TK, TQ), F32), pltpu.VMEM((HEADS_PER_STEP, TK, TQ), F32)],
        compiler_params=_cparams(("parallel", "parallel", "arbitrary")),
        name="fox",
    )(main3d, main3d, vt3d, c3d)


def _key_to_f32(key):
    bits = key ^ ((key >> 31) & jnp.int32(0x7FFFFFFF))
    return lax.bitcast_convert_type(bits, F32)


def _dsa_kernel(q_ref, k_ref, vt_ref, qi_ref, ki_ref, misc_ref, o_ref, score_ref, bias_ref, za_ref, zb_ref,
                *, top_k):
    i = pl.program_id(1)
    diag = _causal_tile_t(False)
    seq = score_ref.shape[0]

    wi_t = misc_ref[0].T[MISC_WI:MISC_WI + IDX_HEADS, :] * IDX_W_SCALE
    qih = [_head_q(qi_ref[0][:, (h // 2) * LANES:(h // 2 + 1) * LANES], h % 2) for h in range(IDX_HEADS)]

    def score_block(j, mask):
        kij = ki_ref[0, _rows(j), :]
        sc = None
        for h in range(IDX_HEADS):
            term = wi_t[h:h + 1, :] * jnp.maximum(_dot_nt(kij, qih[h]), 0.0)
            sc = term if sc is None else sc + term
        if mask is not None:
            sc = jnp.where(mask, sc, -jnp.inf)
        score_ref[_rows(j), :] = sc

    def score_loop(j, _):
        score_block(j, None)
        return 0

    lax.fori_loop(0, i, score_loop, 0)
    score_block(i, diag)

    def count(pred):
        def body(j, acc):
            hit = pred(score_ref[_rows(j), :], j).astype(jnp.int32)
            return acc + jnp.sum(hit.reshape(TK // SUBLANES, SUBLANES, TQ), axis=0)
        acc = lax.fori_loop(0, i + 1, body, jnp.zeros((SUBLANES, TQ), jnp.int32))
        return jnp.sum(acc, axis=0, keepdims=True)

    def bisect(_, lh):
        lo, hi = lh
        mid = (lo >> 1) + (hi >> 1) + (lo & hi & 1)
        thr = _key_to_f32(mid)
        ok = count(lambda s, j: s >= thr) >= top_k
        return jnp.where(ok, mid, lo), jnp.where(ok, hi, mid)

    lo0 = jnp.full((1, TQ), KEY_NEG_INF, jnp.int32)
    hi0 = jnp.full((1, TQ), KEY_END, jnp.int32)
    lo, _ = lax.fori_loop(0, 32, bisect, (lo0, hi0))
    thr = _key_to_f32(lo)

    need = (top_k - count(lambda s, j: s > thr)).astype(F32)
    kr = lax.broadcasted_iota(jnp.int32, (TK, TK), 0)
    kc = lax.broadcasted_iota(jnp.int32, (TK, TK), 1)
    upto = (kc <= kr).astype(MXU_DTYPE)

    def bias_block(j, mask, seen):
        s = score_ref[_rows(j), :]
        tie = s == thr
        rank = _dot(upto, jnp.where(tie, 1.0, 0.0).astype(MXU_DTYPE)) + seen
        sel = (s > thr) | (tie & (rank <= need))
        if mask is not None:
            sel = sel & mask
        bias_ref[_rows(j), :] = jnp.where(sel, 0.0, NEG_BIG)
        return rank[TK - 1:TK, :]

    seen = lax.fori_loop(0, i, lambda j, sn: bias_block(j, None, sn), jnp.zeros((1, TQ), F32))
    bias_block(i, diag, seen)

    bias_at = lambda j: bias_ref[_rows(j), :]
    heads = []
    for h in range(DSA_HEADS):
        g = h // DSA_GROUP
        k_at = lambda j, g=g: k_ref[0, _rows(j), g * LANES:(g + 1) * LANES]
        vt_at = lambda j, g=g: vt_ref[0, g * HEAD_DIM:(g + 1) * HEAD_DIM, _rows(j)]
        qh = _head_q(q_ref[0][:, (h // 2) * LANES:(h // 2 + 1) * LANES], h % 2)
        heads.append((qh, k_at, vt_at, bias_at))
    _store_heads_t(o_ref, _flash_heads(heads, i, None, (za_ref, zb_ref)))


def _dsa(main3d, vt3d, misc3d, top_k):
    b, s, _ = main3d.shape
    nq = s // TQ
    assert top_k <= TK, "the first key block must hold at least top_k keys"
    return pl.pallas_call(
        functools.partial(_dsa_kernel, top_k=top_k),
        grid=(b, nq),
        in_specs=[
            pl.BlockSpec((1, TQ, 4 * LANES), lambda bi, i: (bi, i, COL_QC // (4 * LANES))),
            pl.BlockSpec((1, s, 2 * LANES), lambda bi, i: (bi, 0, COL_KC // (2 * LANES))),
            pl.BlockSpec((1, LANES, s), lambda bi, i: (bi, ROW_VC // LANES, 0)),
            pl.BlockSpec((1, TQ, 2 * LANES), lambda bi, i: (bi, i, COL_QI // (2 * LANES))),
            pl.BlockSpec((1, s, LANES), lambda bi, i: (bi, 0, COL_KI // LANES)),
            pl.BlockSpec((1, TQ, LANES), lambda bi, i: (bi, i, 0)),
        ],
        out_specs=pl.BlockSpec((1, TQ, BRANCH_WIDTH), lambda bi, i: (bi, i, 0)),
        out_shape=jax.ShapeDtypeStruct((b, s, BRANCH_WIDTH), MXU_DTYPE),
        scratch_shapes=[pltpu.VMEM((s, TQ), F32), pltpu.VMEM((s, TQ), F32),
                        pltpu.VMEM((DSA_HEADS, TK, TQ), F32), pltpu.VMEM((DSA_HEADS, TK, TQ), F32)],
        compiler_params=_cparams(("parallel", "arbitrary")),
        name="dsa",
    )(main3d, main3d, vt3d, main3d, main3d, misc3d)


def _merge_kernel(x_ref, a_ref, b_ref, c_ref, g_ref, wg_ref, wb_ref, wo_ref, o_ref):
    x = x_ref[...]
    d = x.shape[1]
    h = _rmsnorm(x, g_ref[...]).astype(MXU_DTYPE)
    y = None
    for n, br_ref in enumerate((a_ref, b_ref, c_ref)):
        gate = jax.nn.sigmoid(_dot(h, wg_ref[:, n * d:(n + 1) * d]))
        term = gate * _dot(br_ref[...], wb_ref[n])
        y = term if y is None else y + term
    o_ref[...] = x + _dot(y.astype(MXU_DTYPE), wo_ref[...])


def _merge(x2d, br_a, br_b, br_c, g, w_gate, w_branch, w_out, tm):
    n, d = x2d.shape
    bw = br_a.shape[1]
    row = lambda i: (i, 0)
    const = lambda i: (0, 0)
    return pl.pallas_call(
        _merge_kernel,
        grid=(n // tm,),
        in_specs=[
            pl.BlockSpec((tm, d), row),
            pl.BlockSpec((tm, bw), row),
            pl.BlockSpec((tm, bw), row),
            pl.BlockSpec((tm, bw), row),
            pl.BlockSpec((1, d), const),
            pl.BlockSpec((d, N_BRANCH * d), const),
            pl.BlockSpec((N_BRANCH, bw, d), lambda i: (0, 0, 0)),
            pl.BlockSpec((d, d), const),
        ],
        out_specs=pl.BlockSpec((tm, d), row),
        out_shape=jax.ShapeDtypeStruct((n, d), F32),
        compiler_params=_cparams(("parallel",)),
        name="merge",
    )(x2d, br_a, br_b, br_c, g, w_gate, w_branch, w_out)


def _mlp_kernel(x_ref, g_ref, wu_ref, wd_ref, gf_ref, o_ref, *, ff_chunk, final_norm):
    x = x_ref[...]
    h = _rmsnorm(x, g_ref[...]).astype(MXU_DTYPE)
    acc = x
    for c in range(wu_ref.shape[1] // ff_chunk):
        u = jnp.maximum(_dot(h, wu_ref[:, c * ff_chunk:(c + 1) * ff_chunk]), 0.0)
        acc = acc + _dot((u * u).astype(MXU_DTYPE), wd_ref[c * ff_chunk:(c + 1) * ff_chunk, :])
    if final_norm:
        acc = _rmsnorm(acc, gf_ref[...])
    o_ref[...] = acc


def _mlp(x2d, g, w_up, w_down, g_final, tm, final_norm):
    n, d = x2d.shape
    ff = w_up.shape[1]
    row = lambda i: (i, 0)
    const = lambda i: (0, 0)
    return pl.pallas_call(
        functools.partial(_mlp_kernel, ff_chunk=min(ff, 1024), final_norm=final_norm),
        grid=(n // tm,),
        in_specs=[
            pl.BlockSpec((tm, d), row),
            pl.BlockSpec((1, d), const),
            pl.BlockSpec((d, ff), const),
            pl.BlockSpec((ff, d), const),
            pl.BlockSpec((1, d), const),
        ],
        out_specs=pl.BlockSpec((tm, d), row),
        out_shape=jax.ShapeDtypeStruct((n, d), F32),
        compiler_params=_cparams(("parallel",)),
        name="mlp",
    )(x2d, g, w_up, w_down, g_final)


def _w_in_columns():
    sizes = (("qa", BRANCH_WIDTH), ("ka", BRANCH_WIDTH), ("va", BRANCH_WIDTH),
             ("qb", BRANCH_WIDTH), ("kb", BRANCH_WIDTH), ("vb", BRANCH_WIDTH), ("fl", FOX_HEADS),
             ("qc", DSA_HEADS * HEAD_DIM), ("kc", DSA_KV_HEADS * HEAD_DIM), ("vc", DSA_KV_HEADS * HEAD_DIM),
             ("qi", IDX_HEADS * IDX_DIM), ("ki", IDX_DIM), ("wi", IDX_HEADS), ("gates", N_BRANCH * 1024))
    cols, o = {}, 0
    for name, width in sizes:
        cols[name] = (o, o + width)
        o += width
    return cols, o


def _pack_kernel(w_ref, main_ref, vals_ref, gates_ref):
    cols, _ = _w_in_columns()

    def src(name, lo=0, hi=None):
        a, b = cols[name]
        return w_ref[0, :, a + lo:(b if hi is None else a + hi)]

    def put(ref, start, value):
        ref[0, :, start:start + value.shape[1]] = value.astype(ref.dtype)

    put(main_ref, COL_QA, src("qa"))
    put(main_ref, COL_KA, src("ka"))
    put(main_ref, COL_QB, src("qb"))
    put(main_ref, COL_KB, src("kb"))
    put(main_ref, COL_QC, src("qc"))
    for g in range(DSA_KV_HEADS):
        kg = src("kc", g * HEAD_DIM, (g + 1) * HEAD_DIM)
        put(main_ref, COL_KC + g * LANES, jnp.concatenate([kg, kg], axis=1))
    put(main_ref, COL_QI, src("qi"))
    ki = src("ki")
    put(main_ref, COL_KI, jnp.concatenate([ki, ki], axis=1))
    put(vals_ref, ROW_VA, src("va"))
    put(vals_ref, ROW_VB, src("vb"))
    put(vals_ref, ROW_VC, src("vc"))
    pad = jnp.zeros((w_ref.shape[1], LANES - FOX_HEADS - IDX_HEADS), F32)
    put(vals_ref, VT_ROWS, jnp.concatenate([src("fl"), src("wi"), pad], axis=1))
    put(gates_ref, 0, src("gates"))


def _pack_w_in(w_in):
    depth, d, total = w_in.shape
    cols, end = _w_in_columns()
    assert end == total and cols["gates"][1] - cols["gates"][0] == N_BRANCH * d
    rows = 256
    blk = lambda width: pl.BlockSpec((1, rows, width), lambda l, r: (l, r, 0))
    widths = (MAIN_COLS, VALS_COLS, N_BRANCH * d)
    return pl.pallas_call(
        _pack_kernel,
        grid=(depth, d // rows),
        in_specs=[blk(total)],
        out_specs=[blk(wd) for wd in widths],
        out_shape=[jax.ShapeDtypeStruct((depth, d, wd), MXU_DTYPE) for wd in widths],
        compiler_params=_cparams(("parallel", "parallel")),
        name="pack",
    )(w_in)


def _rope_rows():
    lane = jnp.arange(LANES) % HEAD_DIM
    half = ROT_DIM // 2
    inv_freq = ROPE_THETA ** (-jnp.arange(0, ROT_DIM, 2, dtype=F32) / ROT_DIM)
    freq = jnp.where(lane < ROT_DIM, inv_freq[lane % half], 0.0).astype(F32)
    sg1 = jnp.where(lane < half, -1.0, 0.0).astype(F32)
    sg2 = jnp.where((lane >= half) & (lane < ROT_DIM), 1.0, 0.0).astype(F32)
    return freq[None, :], sg1[None, :], sg2[None, :]


def kernel(x, positions, g_mix, w_in, b_forget, w_branch, w_out, g_mlp, w_up, w_down, g_final):
    b, s, d = x.shape
    n = b * s
    depth = w_in.shape[0]
    top_k = min(TOPK_MAX, s // 4)
    tm = min(512, s)
    freq, sg1, sg2 = _rope_rows()
    pos2d = positions.reshape(n, 1)
    x2d = x.reshape(n, d)
    w_main_all, w_v_all, w_gate_all = _pack_w_in(w_in)
    for layer in range(depth):
        w_main, w_v, w_gate = w_main_all[layer], w_v_all[layer], w_gate_all[layer]
        bias_row = jnp.concatenate([b_forget[layer], jnp.zeros((LANES - FOX_HEADS,), F32)])[None, :]
        main, vt, misc = _proj(x2d, pos2d, g_mix[layer][None, :], freq, sg1, sg2, w_main, w_v, tm, b, s)
        main3d = main.reshape(b, s, MAIN_COLS)
        misc3d = misc.reshape(b, s, LANES)
        cum = _cumf(misc3d, bias_row)
        br_a = _sb(main3d, vt).reshape(n, BRANCH_WIDTH)
        br_b = _fox(main3d, vt, cum).reshape(n, BRANCH_WIDTH)
        br_c = _dsa(main3d, vt, misc3d, top_k).reshape(n, BRANCH_WIDTH)
        x2d = _merge(x2d, br_a, br_b, br_c, g_mix[layer][None, :], w_gate,
                     w_branch[layer].astype(MXU_DTYPE), w_out[layer].astype(MXU_DTYPE), tm)
        x2d = _mlp(x2d, g_mlp[layer][None, :], w_up[layer].astype(MXU_DTYPE),
                   w_down[layer].astype(MXU_DTYPE), g_final[None, :], tm, layer == depth - 1)
    return x2d.reshape(b, s, d)
```

```python
import functools

import jax
import jax.numpy as jnp
from jax import lax
from jax.experimental import pallas as pl
from jax.experimental.pallas import tpu as pltpu

F32 = jnp.float32
MXU_DTYPE = jnp.bfloat16

HEAD_DIM = 64
SB_HEADS = 8
FOX_HEADS = 8
DSA_HEADS = 8
DSA_KV_HEADS = 2
DSA_GROUP = DSA_HEADS // DSA_KV_HEADS
IDX_HEADS = 4
IDX_DIM = 64
N_BRANCH = 3
ROPE_THETA = 500000.0
ROT_DIM = HEAD_DIM // 4
TOPK_MAX = 256
NORM_EPS = 1e-6

LANES = 128
SUBLANES = 8
SUM_ROWS = 2 * SUBLANES
BRANCH_WIDTH = SB_HEADS * HEAD_DIM
ATTN_SCALE = HEAD_DIM ** -0.5
assert IDX_DIM == HEAD_DIM
IDX_W_SCALE = IDX_HEADS ** -0.5

COL_QA, COL_KA = 0, 512
COL_QB, COL_KB = 1024, 1536
COL_QC = 2048
COL_KC = 2560
COL_QI = 2816
COL_KI = 3072
MAIN_COLS = 3200
ROPE_START = COL_QC
ROW_VA, ROW_VB, ROW_VC = 0, 512, 1024
VT_ROWS = 1152
VALS_COLS = VT_ROWS + LANES
MISC_FL, MISC_WI = 0, 8

TQ = 256
TK = 256
HEADS_PER_STEP = 8
NEG_BIG = -1e30

VMEM_LIMIT = 56 * 1024 * 1024


def _cparams(sem):
    return pltpu.CompilerParams(dimension_semantics=sem, vmem_limit_bytes=VMEM_LIMIT)


def _rmsnorm(x, g):
    y = x * lax.rsqrt(jnp.mean(x * x, axis=-1, keepdims=True) + NORM_EPS)
    return y * g


def _dot(a, b):
    return jnp.dot(a, b, preferred_element_type=F32)


def _dot_nt(a, b):
    return lax.dot_general(a, b, (((1,), (1,)), ((), ())), preferred_element_type=F32)


def _split2(x):
    hi = x.astype(MXU_DTYPE)
    lo = (x - hi.astype(F32)).astype(MXU_DTYPE)
    return hi, lo


def _split3(x):
    hi = x.astype(MXU_DTYPE)
    r = x - hi.astype(F32)
    mid = r.astype(MXU_DTYPE)
    lo = (r - mid.astype(F32)).astype(MXU_DTYPE)
    return hi, mid, lo


def _log_sigmoid_parts(z):
    neg_abs = lax.bitcast_convert_type(lax.bitcast_convert_type(z, jnp.int32) | jnp.int32(-2 ** 31), F32)
    sp = jnp.log(1.0 + jnp.exp(neg_abs))
    return jnp.minimum(z, 0.0) - sp, jnp.maximum(z, 0.0) + sp


def _proj_kernel(x_ref, pos_ref, g_ref, freq_ref, sg1_ref, sg2_ref, wm_ref, wv_ref,
                 main_ref, vt_ref, misc_ref):
    h = _rmsnorm(x_ref[...], g_ref[...]).astype(MXU_DTYPE)
    ang = pos_ref[...].astype(F32) * freq_ref[...]
    cos = jnp.cos(ang)
    sin = jnp.sin(ang)
    s_up = sin * sg1_ref[...]
    s_dn = sin * sg2_ref[...]
    half = ROT_DIM // 2
    chunk = 4 * LANES
    for c0 in range(0, MAIN_COLS, chunk):
        c1 = min(c0 + chunk, MAIN_COLS)
        p = _dot(h, wm_ref[:, c0:c1])
        if c0 >= ROPE_START:
            for t in range((c1 - c0) // LANES):
                pt = p[:, t * LANES:(t + 1) * LANES]
                pt = (pt * cos + pltpu.roll(pt, LANES - half, 1) * s_up
                      + pltpu.roll(pt, half, 1) * s_dn)
                main_ref[:, c0 + t * LANES:c0 + (t + 1) * LANES] = pt.astype(main_ref.dtype)
        else:
            main_ref[:, c0:c1] = p.astype(main_ref.dtype)
    for c0 in range(0, VALS_COLS, 2 * LANES):
        pv = _dot(h, wv_ref[:, c0:c0 + 2 * LANES])
        for r0 in (c0, c0 + LANES):
            piece = pv[:, r0 - c0:r0 - c0 + LANES]
            if r0 < VT_ROWS:
                vt_ref[0, r0:r0 + LANES, :] = piece.T.astype(vt_ref.dtype)
            else:
                misc_ref[...] = piece


def _proj(x2d, pos2d, g, freq, sg1, sg2, w_main, w_v, tm, batch, seq):
    n, d = x2d.shape
    nsb = seq // tm
    const = lambda i: (0, 0)
    return pl.pallas_call(
        _proj_kernel,
        grid=(n // tm,),
        in_specs=[
            pl.BlockSpec((tm, d), lambda i: (i, 0)),
            pl.BlockSpec((tm, 1), lambda i: (i, 0)),
            pl.BlockSpec((1, d), const),
            pl.BlockSpec((1, LANES), const),
            pl.BlockSpec((1, LANES), const),
            pl.BlockSpec((1, LANES), const),
            pl.BlockSpec((d, MAIN_COLS), const),
            pl.BlockSpec((d, VALS_COLS), const),
        ],
        out_specs=[
            pl.BlockSpec((tm, MAIN_COLS), lambda i: (i, 0)),
            pl.BlockSpec((1, VT_ROWS, tm), lambda i: (i // nsb, 0, i % nsb)),
            pl.BlockSpec((tm, LANES), lambda i: (i, 0)),
        ],
        out_shape=[
            jax.ShapeDtypeStruct((n, MAIN_COLS), MXU_DTYPE),
            jax.ShapeDtypeStruct((batch, VT_ROWS, seq), MXU_DTYPE),
            jax.ShapeDtypeStruct((n, LANES), F32),
        ],
        compiler_params=_cparams(("parallel",)),
        name="proj",
    )(x2d, pos2d, g, freq, sg1, sg2, w_main, w_v)


def _cumf_kernel(misc_ref, bias_ref, c_ref):
    s = misc_ref.shape[1]
    r = lax.broadcasted_iota(jnp.int32, (TK, TK), 0)
    c = lax.broadcasted_iota(jnp.int32, (TK, TK), 1)
    incl = (c <= r).astype(MXU_DTYPE)
    carry = jnp.zeros((1, LANES), F32)
    for b in range(s // TK):
        logit = misc_ref[0, b * TK:(b + 1) * TK, :] + bias_ref[...]
        logf, _ = _log_sigmoid_parts(logit)
        hi, mid, lo = _split3(logf)
        cs = (_dot(incl, lo) + _dot(incl, mid)) + _dot(incl, hi) + carry
        c_ref[0, b * TK:(b + 1) * TK, :] = cs
        carry = cs[TK - 1:TK, :]


def _cumf(misc3d, bias_row):
    b, s, _ = misc3d.shape
    return pl.pallas_call(
        _cumf_kernel,
        grid=(b,),
        in_specs=[
            pl.BlockSpec((1, s, LANES), lambda i: (i, 0, 0)),
            pl.BlockSpec((1, LANES), lambda i: (0, 0)),
        ],
        out_specs=pl.BlockSpec((1, s, LANES), lambda i: (i, 0, 0)),
        out_shape=jax.ShapeDtypeStruct((b, s, LANES), F32),
        compiler_params=_cparams(("parallel",)),
        name="cumf",
    )(misc3d, bias_row)


def _head_q(q_tile, half):
    lane = lax.broadcasted_iota(jnp.int32, q_tile.shape, 1)
    keep = (lane // HEAD_DIM) == half
    return jnp.where(keep, q_tile.astype(F32) * ATTN_SCALE, 0.0).astype(MXU_DTYPE)


def _causal_tile_t(strict):
    key = lax.broadcasted_iota(jnp.int32, (TK, TQ), 0)
    qry = lax.broadcasted_iota(jnp.int32, (TK, TQ), 1)
    return (key < qry) if strict else (key <= qry)


def _rows(j):
    return pl.ds(pl.multiple_of(j * TK, TK), TK)


def _lookahead_loop(n_last, bufs, issue, step, carry, primed=False):
    buf_a, buf_b = bufs

    def pair(t, carry):
        n = 2 * t
        issue(n + 1, buf_b)
        carry = step(n, buf_a, carry, False)
        issue(n + 2, buf_a)
        return step(n + 1, buf_b, carry, False)

    def odd_tail(carry):
        issue(n_last, buf_b)
        carry = step(n_last - 1, buf_a, carry, False)
        return step(n_last, buf_b, carry, True)

    def even_tail(carry):
        return step(n_last, buf_a, carry, True)

    if not primed:
        issue(0, buf_a)
    carry = lax.fori_loop(0, n_last // 2, pair, carry)
    return lax.cond(n_last % 2 == 1, odd_tail, even_tail, carry)


def _flash_heads(heads, n_full, diag_mask, z_bufs):
    def add_bias(s, bias):
        if isinstance(bias, (list, tuple)):
            return jnp.concatenate([s[:, t * LANES:(t + 1) * LANES] + bias[t] for t in range(len(bias))],
                                   axis=1)
        return s + bias

    ones = jnp.ones((SUM_ROWS, TK), MXU_DTYPE)

    def issue(j, buf):
        for h, (qh, k_at, _, _) in enumerate(heads):
            buf[h] = _dot_nt(k_at(j), qh)

    def step(j, buf, carry, mask):
        stats = []
        for h, ((_, _, _, bias_at), (m, _)) in enumerate(zip(heads, carry)):
            s = add_bias(buf[h], bias_at(j))
            if mask is not None:
                s = jnp.where(mask, s, NEG_BIG)
            m_new = jnp.maximum(m, jnp.max(s, axis=0, keepdims=True))
            stats.append((m_new, jnp.exp(m - m_new), jnp.exp(s - m_new).astype(MXU_DTYPE)))
        out = []
        for (m_new, alpha, p), (_, _, vt_at, _), (_, acc) in zip(stats, heads, carry):
            vt1 = jnp.concatenate([vt_at(j), ones], axis=0)
            out.append((m_new, alpha * acc + _dot(vt1, p)))
        return tuple(out)

    init = tuple((jnp.full((1, TQ), NEG_BIG, F32), jnp.zeros((HEAD_DIM + SUM_ROWS, TQ), F32)) for _ in heads)
    carry = _lookahead_loop(n_full, z_bufs, issue,
                            lambda j, buf, carry, last: step(j, buf, carry, diag_mask if last else None), init)
    return [acc[:HEAD_DIM] / acc[HEAD_DIM:HEAD_DIM + 1] for (_, acc) in carry]


def _store_heads_t(o_ref, outs_t):
    for t in range(len(outs_t) // 2):
        pair_t = jnp.concatenate([outs_t[2 * t], outs_t[2 * t + 1]], axis=0)
        o_ref[0, :, t * LANES:(t + 1) * LANES] = pair_t.T.astype(o_ref.dtype)


def _sb_kernel(q_ref, k_ref, vt_ref, o_ref, za_ref, zb_ref):
    i = pl.program_id(2)
    r = lax.broadcasted_iota(jnp.int32, (TK, TK), 0)
    c = lax.broadcasted_iota(jnp.int32, (TK, TK), 1)
    after = (c > r).astype(MXU_DTYPE)
    after2 = jnp.concatenate([after, after], axis=1)
    strict = _causal_tile_t(True)
    n_heads = HEADS_PER_STEP
    qhs = [_head_q(q_ref[0][:, (h // 2) * LANES:(h // 2 + 1) * LANES], h % 2) for h in range(n_heads)]

    def issue(j, buf):
        for h in range(n_heads):
            buf[h] = _dot_nt(k_ref[0, _rows(j), (h // 2) * LANES:(h // 2 + 1) * LANES], qhs[h])

    def step(j, buf, carry, mask):
        rows = _rows(j)
        parts = []
        for h in range(n_heads):
            log_beta, neg_l1m = _log_sigmoid_parts(buf[h])
            if mask is not None:
                neg_l1m = jnp.where(mask, neg_l1m, 0.0)
            hi, lo = _split2(neg_l1m)
            parts.append((log_beta, neg_l1m[0:1, :], jnp.concatenate([hi, lo], axis=0)))
        laters = [_dot(after2, hl) for (_, _, hl) in parts]
        ws = []
        for (log_beta, _, _), later, (_, tail) in zip(parts, laters, carry):
            w = jnp.exp(log_beta - (later + tail))
            if mask is not None:
                w = jnp.where(mask, w, 0.0)
            ws.append(w.astype(MXU_DTYPE))
        out = []
        for h, (w, (_, first, _), later, (acc, tail)) in enumerate(zip(ws, parts, laters, carry)):
            vtj = vt_ref[0, h * HEAD_DIM:(h + 1) * HEAD_DIM, rows]
            out.append((acc + _dot(vtj, w), tail + (first + later[0:1, :])))
        return tuple(out)

    carry = tuple((jnp.zeros((HEAD_DIM, TQ), F32), jnp.zeros((1, TQ), F32)) for _ in range(n_heads))
    issue(i, za_ref)

    def with_earlier_blocks(cr):
        issue(i - 1, zb_ref)
        cr = step(i, za_ref, cr, strict)
        return _lookahead_loop(i - 1, (zb_ref, za_ref), lambda n, buf: issue(i - 1 - n, buf),
                               lambda n, buf, c, last: step(i - 1 - n, buf, c, None), cr, primed=True)

    carry = lax.cond(i > 0, with_earlier_blocks, lambda cr: step(i, za_ref, cr, strict), carry)
    _store_heads_t(o_ref, [acc for acc, _ in carry])


def _sb(main3d, vt3d):
    b, s, _ = main3d.shape
    nq = s // TQ
    w = HEADS_PER_STEP * HEAD_DIM
    qb, kb, vb = COL_QA // w, COL_KA // w, ROW_VA // w
    return pl.pallas_call(
        _sb_kernel,
        grid=(b, SB_HEADS // HEADS_PER_STEP, nq),
        in_specs=[
            pl.BlockSpec((1, TQ, w), lambda bi, hg, i: (bi, i, qb + hg)),
            pl.BlockSpec((1, s, w), lambda bi, hg, i: (bi, 0, kb + hg)),
            pl.BlockSpec((1, w, s), lambda bi, hg, i: (bi, vb + hg, 0)),
        ],
        out_specs=pl.BlockSpec((1, TQ, w), lambda bi, hg, i: (bi, i, hg)),
        out_shape=jax.ShapeDtypeStruct((b, s, BRANCH_WIDTH), MXU_DTYPE),
        scratch_shapes=[pltpu.VMEM((HEADS_PER_STEP, TK, TQ), F32), pltpu.VMEM((HEADS_PER_STEP, TK, TQ), F32)],
        compiler_params=_cparams(("parallel", "parallel", "arbitrary")),
        name="sb",
    )(main3d, main3d, vt3d)


def _fox_kernel(q_ref, k_ref, vt_ref, c_ref, o_ref, cb_ref, za_ref, zb_ref):
    hg = pl.program_id(1)
    i = pl.program_id(2)
    n_heads = HEADS_PER_STEP

    @pl.when(i == 0)
    def _():
        c = c_ref[0]
        lane = lax.broadcasted_iota(jnp.int32, c.shape, 1)
        for h in range(n_heads):
            col = jnp.sum(jnp.where(lane == hg * n_heads + h, c, 0.0), axis=1, keepdims=True)
            cb_ref[h] = jnp.broadcast_to(-col, c.shape)

    heads = []
    for h in range(n_heads):
        t = h // 2
        qh = _head_q(q_ref[0][:, t * LANES:(t + 1) * LANES], h % 2)
        k_at = lambda j, t=t: k_ref[0, _rows(j), t * LANES:(t + 1) * LANES]
        vt_at = lambda j, h=h: vt_ref[0, h * HEAD_DIM:(h + 1) * HEAD_DIM, _rows(j)]
        bias_at = lambda j, h=h: [cb_ref[h, _rows(j), :]] * (TQ // LANES)
        heads.append((qh, k_at, vt_at, bias_at))
    _store_heads_t(o_ref, _flash_heads(heads, i, _causal_tile_t(False), (za_ref, zb_ref)))


def _fox(main3d, vt3d, c3d):
    b, s, _ = main3d.shape
    nq = s // TQ
    w = HEADS_PER_STEP * HEAD_DIM
    qb, kb, vb = COL_QB // w, COL_KB // w, ROW_VB // w
    return pl.pallas_call(
        _fox_kernel,
        grid=(b, FOX_HEADS // HEADS_PER_STEP, nq),
        in_specs=[
            pl.BlockSpec((1, TQ, w), lambda bi, hg, i: (bi, i, qb + hg)),
            pl.BlockSpec((1, s, w), lambda bi, hg, i: (bi, 0, kb + hg)),
            pl.BlockSpec((1, w, s), lambda bi, hg, i: (bi, vb + hg, 0)),
            pl.BlockSpec((1, s, LANES), lambda bi, hg, i: (bi, 0, 0)),
        ],
        out_specs=pl.BlockSpec((1, TQ, w), lambda bi, hg, i: (bi, i, hg)),
        out_shape=jax.ShapeDtypeStruct((b, s, BRANCH_WIDTH), MXU_DTYPE),
        scratch_shapes=[pltpu.VMEM((HEADS_PER_STEP, s, LANES), F32),
                        pltpu.VMEM((HEADS_PER_STEP, TK, TQ), F32), pltpu.VMEM((HEADS_PER_STEP, TK, TQ), F32)],
        compiler_params=_cparams(("parallel", "parallel", "arbitrary")),
        name="fox",
    )(main3d, main3d, vt3d, c3d)


def _dsa_kernel(q_ref, k_ref, vt_ref, qi_ref, ki_ref, misc_ref, o_ref, key_ref, dig_ref, bias_ref,
                za_ref, zb_ref, *, top_k):
    i = pl.program_id(1)
    diag = _causal_tile_t(False)

    wi_t = misc_ref[0].T[MISC_WI:MISC_WI + IDX_HEADS, :] * IDX_W_SCALE
    qih = [_head_q(qi_ref[0][:, (h // 2) * LANES:(h // 2 + 1) * LANES], h % 2) for h in range(IDX_HEADS)]

    def score_block(j, mask):
        kij = ki_ref[0, _rows(j), :]
        sc = None
        for h in range(IDX_HEADS):
            term = wi_t[h:h + 1, :] * jnp.maximum(_dot_nt(kij, qih[h]), 0.0)
            sc = term if sc is None else sc + term
        if mask is not None:
            sc = jnp.where(mask, sc, -jnp.inf)
        bits = lax.bitcast_convert_type(jnp.where(sc == 0.0, 0.0, sc), jnp.int32)
        key_ref[_rows(j), :] = bits ^ ((bits >> 31) & jnp.int32(0x7FFFFFFF))

    def score_loop(j, _):
        score_block(j, None)
        return 0

    lax.fori_loop(0, i, score_loop, 0)
    score_block(i, diag)

    small = MXU_DTYPE
    group = 2 * SUBLANES

    def count_digits_ge(cand):
        def body(j, acc):
            hit = jnp.where(dig_ref[_rows(j), :] >= cand, jnp.ones((), small), jnp.zeros((), small))
            part = hit[0:group]
            for g in range(1, TK // group):
                part = part + hit[g * group:(g + 1) * group]
            return acc + part.astype(F32)
        acc = lax.fori_loop(0, i + 1, body, jnp.zeros((group, TQ), F32))
        return jnp.sum(acc, axis=0, keepdims=True)

    above = jnp.zeros((1, TQ), F32)
    prefix = None
    for shift in (24, 16, 8, 0):
        def fill(j, _, shift=shift, prefix=prefix):
            key = key_ref[_rows(j), :]
            if prefix is None:
                digit = (key >> shift) + 128
            else:
                digit = jnp.where((key >> (shift + 8)) == prefix, (key >> shift) & 0xFF, -1)
            dig_ref[_rows(j), :] = digit.astype(F32).astype(small)
            return 0

        lax.fori_loop(0, i + 1, fill, 0)

        def search(_, st, above=above):
            lo, hi, n_hi = st
            mid = (lo + hi) * 0.5
            n_mid = count_digits_ge(mid.astype(small))
            ok = above + n_mid >= top_k
            return jnp.where(ok, mid, lo), jnp.where(ok, hi, mid), jnp.where(ok, n_hi, n_mid)

        start = (jnp.zeros((1, TQ), F32), jnp.full((1, TQ), 256.0, F32), jnp.zeros((1, TQ), F32))
        digit, _, n_hi = lax.fori_loop(0, 8, search, start)
        above = above + n_hi
        digit = digit.astype(jnp.int32)
        prefix = digit - 128 if prefix is None else (prefix << 8) | digit
    thr_key = prefix

    need = top_k - above
    kr = lax.broadcasted_iota(jnp.int32, (TK, TK), 0)
    kc = lax.broadcasted_iota(jnp.int32, (TK, TK), 1)
    upto = (kc <= kr).astype(MXU_DTYPE)

    def bias_block(j, mask, seen):
        key = key_ref[_rows(j), :]
        tie = key == thr_key
        rank = _dot(upto, jnp.where(tie, 1.0, 0.0).astype(MXU_DTYPE)) + seen
        sel = (key > thr_key) | (tie & (rank <= need))
        if mask is not None:
            sel = sel & mask
        bias_ref[_rows(j), :] = jnp.where(sel, 0.0, NEG_BIG)
        return rank[TK - 1:TK, :]

    seen = lax.fori_loop(0, i, lambda j, sn: bias_block(j, None, sn), jnp.zeros((1, TQ), F32))
    bias_block(i, diag, seen)

    bias_at = lambda j: bias_ref[_rows(j), :]
    heads = []
    for h in range(DSA_HEADS):
        g = h // DSA_GROUP
        k_at = lambda j, g=g: k_ref[0, _rows(j), g * LANES:(g + 1) * LANES]
        vt_at = lambda j, g=g: vt_ref[0, g * HEAD_DIM:(g + 1) * HEAD_DIM, _rows(j)]
        qh = _head_q(q_ref[0][:, (h // 2) * LANES:(h // 2 + 1) * LANES], h % 2)
        heads.append((qh, k_at, vt_at, bias_at))
    _store_heads_t(o_ref, _flash_heads(heads, i, None, (za_ref, zb_ref)))


def _dsa(main3d, vt3d, misc3d, top_k):
    b, s, _ = main3d.shape
    nq = s // TQ
    assert top_k <= TK, "the first key block must hold at least top_k keys"
    return pl.pallas_call(
        functools.partial(_dsa_kernel, top_k=top_k),
        grid=(b, nq),
        in_specs=[
            pl.BlockSpec((1, TQ, 4 * LANES), lambda bi, i: (bi, i, COL_QC // (4 * LANES))),
            pl.BlockSpec((1, s, 2 * LANES), lambda bi, i: (bi, 0, COL_KC // (2 * LANES))),
            pl.BlockSpec((1, LANES, s), lambda bi, i: (bi, ROW_VC // LANES, 0)),
            pl.BlockSpec((1, TQ, 2 * LANES), lambda bi, i: (bi, i, COL_QI // (2 * LANES))),
            pl.BlockSpec((1, s, LANES), lambda bi, i: (bi, 0, COL_KI // LANES)),
            pl.BlockSpec((1, TQ, LANES), lambda bi, i: (bi, i, 0)),
        ],
        out_specs=pl.BlockSpec((1, TQ, BRANCH_WIDTH), lambda bi, i: (bi, i, 0)),
        out_shape=jax.ShapeDtypeStruct((b, s, BRANCH_WIDTH), MXU_DTYPE),
        scratch_shapes=[pltpu.VMEM((s, TQ), jnp.int32), pltpu.VMEM((s, TQ), MXU_DTYPE), pltpu.VMEM((s, TQ), F32),
                        pltpu.VMEM((DSA_HEADS, TK, TQ), F32), pltpu.VMEM((DSA_HEADS, TK, TQ), F32)],
        compiler_params=_cparams(("parallel", "arbitrary")),
        name="dsa",
    )(main3d, main3d, vt3d, main3d, main3d, misc3d)


def _merge_kernel(x_ref, a_ref, b_ref, c_ref, g_ref, wg_ref, wb_ref, wo_ref, o_ref):
    x = x_ref[...]
    d = x.shape[1]
    h = _rmsnorm(x, g_ref[...]).astype(MXU_DTYPE)
    y = None
    for n, br_ref in enumerate((a_ref, b_ref, c_ref)):
        gate = jax.nn.sigmoid(_dot(h, wg_ref[:, n * d:(n + 1) * d]))
        term = gate * _dot(br_ref[...], wb_ref[n])
        y = term if y is None else y + term
    o_ref[...] = x + _dot(y.astype(MXU_DTYPE), wo_ref[...])


def _merge(x2d, br_a, br_b, br_c, g, w_gate, w_branch, w_out, tm):
    n, d = x2d.shape
    bw = br_a.shape[1]
    row = lambda i: (i, 0)
    const = lambda i: (0, 0)
    return pl.pallas_call(
        _merge_kernel,
        grid=(n // tm,),
        in_specs=[
            pl.BlockSpec((tm, d), row),
            pl.BlockSpec((tm, bw), row),
            pl.BlockSpec((tm, bw), row),
            pl.BlockSpec((tm, bw), row),
            pl.BlockSpec((1, d), const),
            pl.BlockSpec((d, N_BRANCH * d), const),
            pl.BlockSpec((N_BRANCH, bw, d), lambda i: (0, 0, 0)),
            pl.BlockSpec((d, d), const),
        ],
        out_specs=pl.BlockSpec((tm, d), row),
        out_shape=jax.ShapeDtypeStruct((n, d), F32),
        compiler_params=_cparams(("parallel",)),
        name="merge",
    )(x2d, br_a, br_b, br_c, g, w_gate, w_branch, w_out)


def _mlp_kernel(x_ref, g_ref, wu_ref, wd_ref, gf_ref, o_ref, *, ff_chunk, final_norm):
    x = x_ref[...]
    h = _rmsnorm(x, g_ref[...]).astype(MXU_DTYPE)
    acc = x
    for c in range(wu_ref.shape[1] // ff_chunk):
        u = jnp.maximum(_dot(h, wu_ref[:, c * ff_chunk:(c + 1) * ff_chunk]), 0.0)
        acc = acc + _dot((u * u).astype(MXU_DTYPE), wd_ref[c * ff_chunk:(c + 1) * ff_chunk, :])
    if final_norm:
        acc = _rmsnorm(acc, gf_ref[...])
    o_ref[...] = acc


def _mlp(x2d, g, w_up, w_down, g_final, tm, final_norm):
    n, d = x2d.shape
    ff = w_up.shape[1]
    row = lambda i: (i, 0)
    const = lambda i: (0, 0)
    return pl.pallas_call(
        functools.partial(_mlp_kernel, ff_chunk=min(ff, 1024), final_norm=final_norm),
        grid=(n // tm,),
        in_specs=[
            pl.BlockSpec((tm, d), row),
            pl.BlockSpec((1, d), const),
            pl.BlockSpec((d, ff), const),
            pl.BlockSpec((ff, d), const),
            pl.BlockSpec((1, d), const),
        ],
        out_specs=pl.BlockSpec((tm, d), row),
        out_shape=jax.ShapeDtypeStruct((n, d), F32),
        compiler_params=_cparams(("parallel",)),
        name="mlp",
    )(x2d, g, w_up, w_down, g_final)


def _w_in_columns():
    sizes = (("qa", BRANCH_WIDTH), ("ka", BRANCH_WIDTH), ("va", BRANCH_WIDTH),
             ("qb", BRANCH_WIDTH), ("kb", BRANCH_WIDTH), ("vb", BRANCH_WIDTH), ("fl", FOX_HEADS),
             ("qc", DSA_HEADS * HEAD_DIM), ("kc", DSA_KV_HEADS * HEAD_DIM), ("vc", DSA_KV_HEADS * HEAD_DIM),
             ("qi", IDX_HEADS * IDX_DIM), ("ki", IDX_DIM), ("wi", IDX_HEADS), ("gates", N_BRANCH * 1024))
    cols, o = {}, 0
    for name, width in sizes:
        cols[name] = (o, o + width)
        o += width
    return cols, o


def _pack_kernel(w_ref, main_ref, vals_ref, gates_ref):
    cols, _ = _w_in_columns()

    def src(name, lo=0, hi=None):
        a, b = cols[name]
        return w_ref[0, :, a + lo:(b if hi is None else a + hi)]

    def put(ref, start, value):
        ref[0, :, start:start + value.shape[1]] = value.astype(ref.dtype)

    put(main_ref, COL_QA, src("qa"))
    put(main_ref, COL_KA, src("ka"))
    put(main_ref, COL_QB, src("qb"))
    put(main_ref, COL_KB, src("kb"))
    put(main_ref, COL_QC, src("qc"))
    for g in range(DSA_KV_HEADS):
        kg = src("kc", g * HEAD_DIM, (g + 1) * HEAD_DIM)
        put(main_ref, COL_KC + g * LANES, jnp.concatenate([kg, kg], axis=1))
    put(main_ref, COL_QI, src("qi"))
    ki = src("ki")
    put(main_ref, COL_KI, jnp.concatenate([ki, ki], axis=1))
    put(vals_ref, ROW_VA, src("va"))
    put(vals_ref, ROW_VB, src("vb"))
    put(vals_ref, ROW_VC, src("vc"))
    pad = jnp.zeros((w_ref.shape[1], LANES - FOX_HEADS - IDX_HEADS), F32)
    put(vals_ref, VT_ROWS, jnp.concatenate([src("fl"), src("wi"), pad], axis=1))
    put(gates_ref, 0, src("gates"))


def _pack_w_in(w_in):
    depth, d, total = w_in.shape
    cols, end = _w_in_columns()
    assert end == total and cols["gates"][1] - cols["gates"][0] == N_BRANCH * d
    rows = 256
    blk = lambda width: pl.BlockSpec((1, rows, width), lambda l, r: (l, r, 0))
    widths = (MAIN_COLS, VALS_COLS, N_BRANCH * d)
    return pl.pallas_call(
        _pack_kernel,
        grid=(depth, d // rows),
        in_specs=[blk(total)],
        out_specs=[blk(wd) for wd in widths],
        out_shape=[jax.ShapeDtypeStruct((depth, d, wd), MXU_DTYPE) for wd in widths],
        compiler_params=_cparams(("parallel", "parallel")),
        name="pack",
    )(w_in)


def _rope_rows():
    lane = jnp.arange(LANES) % HEAD_DIM
    half = ROT_DIM // 2
    inv_freq = ROPE_THETA ** (-jnp.arange(0, ROT_DIM, 2, dtype=F32) / ROT_DIM)
    freq = jnp.where(lane < ROT_DIM, inv_freq[lane % half], 0.0).astype(F32)
    sg1 = jnp.where(lane < half, -1.0, 0.0).astype(F32)
    sg2 = jnp.where((lane >= half) & (lane < ROT_DIM), 1.0, 0.0).astype(F32)
    return freq[None, :], sg1[None, :], sg2[None, :]


def kernel(x, positions, g_mix, w_in, b_forget, w_branch, w_out, g_mlp, w_up, w_down, g_final):
    b, s, d = x.shape
    n = b * s
    depth = w_in.shape[0]
    top_k = min(TOPK_MAX, s // 4)
    tm = min(512, s)
    freq, sg1, sg2 = _rope_rows()
    pos2d = positions.reshape(n, 1)
    x2d = x.reshape(n, d)
    w_main_all, w_v_all, w_gate_all = _pack_w_in(w_in)
    for layer in range(depth):
        w_main, w_v, w_gate = w_main_all[layer], w_v_all[layer], w_gate_all[layer]
        bias_row = jnp.concatenate([b_forget[layer], jnp.zeros((LANES - FOX_HEADS,), F32)])[None, :]
        main, vt, misc = _proj(x2d, pos2d, g_mix[layer][None, :], freq, sg1, sg2, w_main, w_v, tm, b, s)
        main3d = main.reshape(b, s, MAIN_COLS)
        misc3d = misc.reshape(b, s, LANES)
        cum = _cumf(misc3d, bias_row)
        br_a = _sb(main3d, vt).reshape(n, BRANCH_WIDTH)
        br_b = _fox(main3d, vt, cum).reshape(n, BRANCH_WIDTH)
        br_c = _dsa(main3d, vt, misc3d, top_k).reshape(n, BRANCH_WIDTH)
        x2d = _merge(x2d, br_a, br_b, br_c, g_mix[layer][None, :], w_gate,
                     w_branch[layer].astype(MXU_DTYPE), w_out[layer].astype(MXU_DTYPE), tm)
        x2d = _mlp(x2d, g_mlp[layer][None, :], w_up[layer].astype(MXU_DTYPE),
                   w_down[layer].astype(MXU_DTYPE), g_final[None, :], tm, layer == depth - 1)
    return x2d.reshape(b, s, d)
```

```python
import functools

import jax
import jax.numpy as jnp
from jax import lax
from jax.experimental import pallas as pl
from jax.experimental.pallas import tpu as pltpu

F32 = jnp.float32
MXU_DTYPE = jnp.bfloat16

HEAD_DIM = 64
SB_HEADS = 8
FOX_HEADS = 8
DSA_HEADS = 8
DSA_KV_HEADS = 2
DSA_GROUP = DSA_HEADS // DSA_KV_HEADS
IDX_HEADS = 4
IDX_DIM = 64
N_BRANCH = 3
ROPE_THETA = 500000.0
ROT_DIM = HEAD_DIM // 4
TOPK_MAX = 256
NORM_EPS = 1e-6

LANES = 128
SUBLANES = 8
SUM_ROWS = 2 * SUBLANES
BRANCH_WIDTH = SB_HEADS * HEAD_DIM
ATTN_SCALE = HEAD_DIM ** -0.5
assert IDX_DIM == HEAD_DIM
IDX_W_SCALE = IDX_HEADS ** -0.5

COL_QA, COL_KA = 0, 512
COL_QB, COL_KB = 1024, 1536
COL_QC = 2048
COL_KC = 2560
COL_QI = 2816
COL_KI = 3072
MAIN_COLS = 3200
ROPE_START = COL_QC
ROW_VA, ROW_VB, ROW_VC = 0, 512, 1024
VT_ROWS = 1152
VALS_COLS = VT_ROWS + LANES
MISC_FL, MISC_WI = 0, 8

TQ = 256
TK = 256
HEADS_PER_STEP = 8
NEG_BIG = -1e30

VMEM_LIMIT = 56 * 1024 * 1024


def _cparams(sem):
    return pltpu.CompilerParams(dimension_semantics=sem, vmem_limit_bytes=VMEM_LIMIT)


def _rmsnorm(x, g):
    y = x * lax.rsqrt(jnp.mean(x * x, axis=-1, keepdims=True) + NORM_EPS)
    return y * g


def _dot(a, b):
    return jnp.dot(a, b, preferred_element_type=F32)


def _dot_nt(a, b):
    return lax.dot_general(a, b, (((1,), (1,)), ((), ())), preferred_element_type=F32)


def _split2(x):
    hi = x.astype(MXU_DTYPE)
    lo = (x - hi.astype(F32)).astype(MXU_DTYPE)
    return hi, lo


def _split3(x):
    hi = x.astype(MXU_DTYPE)
    r = x - hi.astype(F32)
    mid = r.astype(MXU_DTYPE)
    lo = (r - mid.astype(F32)).astype(MXU_DTYPE)
    return hi, mid, lo


def _log_sigmoid_parts(z):
    neg_abs = lax.bitcast_convert_type(lax.bitcast_convert_type(z, jnp.int32) | jnp.int32(-2 ** 31), F32)
    sp = jnp.log(1.0 + jnp.exp(neg_abs))
    return jnp.minimum(z, 0.0) - sp, jnp.maximum(z, 0.0) + sp


def _proj_kernel(x_ref, pos_ref, g_ref, freq_ref, sg1_ref, sg2_ref, wm_ref, wv_ref,
                 main_ref, vt_ref, misc_ref):
    h = _rmsnorm(x_ref[...], g_ref[...]).astype(MXU_DTYPE)
    ang = pos_ref[...].astype(F32) * freq_ref[...]
    cos = jnp.cos(ang)
    sin = jnp.sin(ang)
    s_up = sin * sg1_ref[...]
    s_dn = sin * sg2_ref[...]
    half = ROT_DIM // 2
    chunk = 4 * LANES
    for c0 in range(0, MAIN_COLS, chunk):
        c1 = min(c0 + chunk, MAIN_COLS)
        p = _dot(h, wm_ref[:, c0:c1])
        if c0 >= ROPE_START:
            for t in range((c1 - c0) // LANES):
                pt = p[:, t * LANES:(t + 1) * LANES]
                pt = (pt * cos + pltpu.roll(pt, LANES - half, 1) * s_up
                      + pltpu.roll(pt, half, 1) * s_dn)
                main_ref[:, c0 + t * LANES:c0 + (t + 1) * LANES] = pt.astype(main_ref.dtype)
        else:
            main_ref[:, c0:c1] = p.astype(main_ref.dtype)
    for c0 in range(0, VALS_COLS, 2 * LANES):
        pv = _dot(h, wv_ref[:, c0:c0 + 2 * LANES])
        for r0 in (c0, c0 + LANES):
            piece = pv[:, r0 - c0:r0 - c0 + LANES]
            if r0 < VT_ROWS:
                vt_ref[0, r0:r0 + LANES, :] = piece.T.astype(vt_ref.dtype)
            else:
                misc_ref[...] = piece


def _proj(x2d, pos2d, g, freq, sg1, sg2, w_main, w_v, tm, batch, seq):
    n, d = x2d.shape
    nsb = seq // tm
    const = lambda i: (0, 0)
    return pl.pallas_call(
        _proj_kernel,
        grid=(n // tm,),
        in_specs=[
            pl.BlockSpec((tm, d), lambda i: (i, 0)),
            pl.BlockSpec((tm, 1), lambda i: (i, 0)),
            pl.BlockSpec((1, d), const),
            pl.BlockSpec((1, LANES), const),
            pl.BlockSpec((1, LANES), const),
            pl.BlockSpec((1, LANES), const),
            pl.BlockSpec((d, MAIN_COLS), const),
            pl.BlockSpec((d, VALS_COLS), const),
        ],
        out_specs=[
            pl.BlockSpec((tm, MAIN_COLS), lambda i: (i, 0)),
            pl.BlockSpec((1, VT_ROWS, tm), lambda i: (i // nsb, 0, i % nsb)),
            pl.BlockSpec((tm, LANES), lambda i: (i, 0)),
        ],
        out_shape=[
            jax.ShapeDtypeStruct((n, MAIN_COLS), MXU_DTYPE),
            jax.ShapeDtypeStruct((batch, VT_ROWS, seq), MXU_DTYPE),
            jax.ShapeDtypeStruct((n, LANES), F32),
        ],
        compiler_params=_cparams(("parallel",)),
        name="proj",
    )(x2d, pos2d, g, freq, sg1, sg2, w_main, w_v)


def _cumf_kernel(misc_ref, bias_ref, c_ref):
    s = misc_ref.shape[1]
    r = lax.broadcasted_iota(jnp.int32, (TK, TK), 0)
    c = lax.broadcasted_iota(jnp.int32, (TK, TK), 1)
    incl = (c <= r).astype(MXU_DTYPE)
    carry = jnp.zeros((1, LANES), F32)
    for b in range(s // TK):
        logit = misc_ref[0, b * TK:(b + 1) * TK, :] + bias_ref[...]
        logf, _ = _log_sigmoid_parts(logit)
        hi, mid, lo = _split3(logf)
        cs = (_dot(incl, lo) + _dot(incl, mid)) + _dot(incl, hi) + carry
        c_ref[0, b * TK:(b + 1) * TK, :] = cs
        carry = cs[TK - 1:TK, :]


def _cumf(misc3d, bias_row):
    b, s, _ = misc3d.shape
    return pl.pallas_call(
        _cumf_kernel,
        grid=(b,),
        in_specs=[
            pl.BlockSpec((1, s, LANES), lambda i: (i, 0, 0)),
            pl.BlockSpec((1, LANES), lambda i: (0, 0)),
        ],
        out_specs=pl.BlockSpec((1, s, LANES), lambda i: (i, 0, 0)),
        out_shape=jax.ShapeDtypeStruct((b, s, LANES), F32),
        compiler_params=_cparams(("parallel",)),
        name="cumf",
    )(misc3d, bias_row)


def _head_q(q_tile, half):
    lane = lax.broadcasted_iota(jnp.int32, q_tile.shape, 1)
    keep = (lane // HEAD_DIM) == half
    return jnp.where(keep, q_tile.astype(F32) * ATTN_SCALE, 0.0).astype(MXU_DTYPE)


def _causal_tile_t(strict):
    key = lax.broadcasted_iota(jnp.int32, (TK, TQ), 0)
    qry = lax.broadcasted_iota(jnp.int32, (TK, TQ), 1)
    return (key < qry) if strict else (key <= qry)


def _rows(j):
    return pl.ds(pl.multiple_of(j * TK, TK), TK)


def _lookahead_loop(n_last, bufs, issue, step, carry, primed=False):
    buf_a, buf_b = bufs

    def pair(t, carry):
        n = 2 * t
        issue(n + 1, buf_b)
        carry = step(n, buf_a, carry, False)
        issue(n + 2, buf_a)
        return step(n + 1, buf_b, carry, False)

    def odd_tail(carry):
        issue(n_last, buf_b)
        carry = step(n_last - 1, buf_a, carry, False)
        return step(n_last, buf_b, carry, True)

    def even_tail(carry):
        return step(n_last, buf_a, carry, True)

    if not primed:
        issue(0, buf_a)
    carry = lax.fori_loop(0, n_last // 2, pair, carry)
    return lax.cond(n_last % 2 == 1, odd_tail, even_tail, carry)


def _flash_heads(heads, n_full, diag_mask, z_bufs):
    def add_bias(s, bias):
        if isinstance(bias, (list, tuple)):
            return jnp.concatenate([s[:, t * LANES:(t + 1) * LANES] + bias[t] for t in range(len(bias))],
                                   axis=1)
        return s + bias

    ones = jnp.ones((SUM_ROWS, TK), MXU_DTYPE)

    def issue(j, buf):
        for h, (qh, k_at, _, _) in enumerate(heads):
            buf[h] = _dot_nt(k_at(j), qh)

    def step(j, buf, carry, mask):
        stats = []
        for h, ((_, _, _, bias_at), (m, _)) in enumerate(zip(heads, carry)):
            s = add_bias(buf[h], bias_at(j))
            if mask is not None:
                s = jnp.where(mask, s, NEG_BIG)
            m_new = jnp.maximum(m, jnp.max(s, axis=0, keepdims=True))
            stats.append((m_new, jnp.exp(m - m_new), jnp.exp(s - m_new).astype(MXU_DTYPE)))
        out = []
        for (m_new, alpha, p), (_, _, vt_at, _), (_, acc) in zip(stats, heads, carry):
            vt1 = jnp.concatenate([vt_at(j), ones], axis=0)
            out.append((m_new, alpha * acc + _dot(vt1, p)))
        return tuple(out)

    init = tuple((jnp.full((1, TQ), NEG_BIG, F32), jnp.zeros((HEAD_DIM + SUM_ROWS, TQ), F32)) for _ in heads)
    carry = _lookahead_loop(n_full, z_bufs, issue,
                            lambda j, buf, carry, last: step(j, buf, carry, diag_mask if last else None), init)
    return [acc[:HEAD_DIM] / acc[HEAD_DIM:HEAD_DIM + 1] for (_, acc) in carry]


def _store_heads_t(o_ref, outs_t):
    for t in range(len(outs_t) // 2):
        pair_t = jnp.concatenate([outs_t[2 * t], outs_t[2 * t + 1]], axis=0)
        o_ref[0, :, t * LANES:(t + 1) * LANES] = pair_t.T.astype(o_ref.dtype)


def _sb_kernel(q_ref, k_ref, vt_ref, o_ref, za_ref, zb_ref):
    i = pl.program_id(2)
    r = lax.broadcasted_iota(jnp.int32, (TK, TK), 0)
    c = lax.broadcasted_iota(jnp.int32, (TK, TK), 1)
    after = (c > r).astype(MXU_DTYPE)
    after2 = jnp.concatenate([after, after], axis=1)
    strict = _causal_tile_t(True)
    n_heads = HEADS_PER_STEP
    qhs = [_head_q(q_ref[0][:, (h // 2) * LANES:(h // 2 + 1) * LANES], h % 2) for h in range(n_heads)]

    def issue(j, buf):
        for h in range(n_heads):
            buf[h] = _dot_nt(k_ref[0, _rows(j), (h // 2) * LANES:(h // 2 + 1) * LANES], qhs[h])

    def step(j, buf, carry, mask):
        rows = _rows(j)
        parts = []
        for h in range(n_heads):
            log_beta, neg_l1m = _log_sigmoid_parts(buf[h])
            if mask is not None:
                neg_l1m = jnp.where(mask, neg_l1m, 0.0)
            hi, lo = _split2(neg_l1m)
            parts.append((log_beta, neg_l1m[0:1, :], jnp.concatenate([hi, lo], axis=0)))
        laters = [_dot(after2, hl) for (_, _, hl) in parts]
        ws = []
        for (log_beta, _, _), later, (_, tail) in zip(parts, laters, carry):
            w = jnp.exp(log_beta - (later + tail))
            if mask is not None:
                w = jnp.where(mask, w, 0.0)
            ws.append(w.astype(MXU_DTYPE))
        out = []
        for h, (w, (_, first, _), later, (acc, tail)) in enumerate(zip(ws, parts, laters, carry)):
            vtj = vt_ref[0, h * HEAD_DIM:(h + 1) * HEAD_DIM, rows]
            out.append((acc + _dot(vtj, w), tail + (first + later[0:1, :])))
        return tuple(out)

    carry = tuple((jnp.zeros((HEAD_DIM, TQ), F32), jnp.zeros((1, TQ), F32)) for _ in range(n_heads))
    issue(i, za_ref)

    def with_earlier_blocks(cr):
        issue(i - 1, zb_ref)
        cr = step(i, za_ref, cr, strict)
        return _lookahead_loop(i - 1, (zb_ref, za_ref), lambda n, buf: issue(i - 1 - n, buf),
                               lambda n, buf, c, last: step(i - 1 - n, buf, c, None), cr, primed=True)

    carry = lax.cond(i > 0, with_earlier_blocks, lambda cr: step(i, za_ref, cr, strict), carry)
    _store_heads_t(o_ref, [acc for acc, _ in carry])


def _sb(main3d, vt3d):
    b, s, _ = main3d.shape
    nq = s // TQ
    w = HEADS_PER_STEP * HEAD_DIM
    qb, kb, vb = COL_QA // w, COL_KA // w, ROW_VA // w
    return pl.pallas_call(
        _sb_kernel,
        grid=(b, SB_HEADS // HEADS_PER_STEP, nq),
        in_specs=[
            pl.BlockSpec((1, TQ, w), lambda bi, hg, i: (bi, i, qb + hg)),
            pl.BlockSpec((1, s, w), lambda bi, hg, i: (bi, 0, kb + hg)),
            pl.BlockSpec((1, w, s), lambda bi, hg, i: (bi, vb + hg, 0)),
        ],
        out_specs=pl.BlockSpec((1, TQ, w), lambda bi, hg, i: (bi, i, hg)),
        out_shape=jax.ShapeDtypeStruct((b, s, BRANCH_WIDTH), MXU_DTYPE),
        scratch_shapes=[pltpu.VMEM((HEADS_PER_STEP, TK, TQ), F32), pltpu.VMEM((HEADS_PER_STEP, TK, TQ), F32)],
        compiler_params=_cparams(("parallel", "parallel", "arbitrary")),
        name="sb",
    )(main3d, main3d, vt3d)


def _fox_kernel(q_ref, k_ref, vt_ref, c_ref, o_ref, cb_ref, za_ref, zb_ref):
    hg = pl.program_id(1)
    i = pl.program_id(2)
    n_heads = HEADS_PER_STEP

    @pl.when(i == 0)
    def _():
        c = c_ref[0]
        lane = lax.broadcasted_iota(jnp.int32, c.shape, 1)
        for h in range(n_heads):
            col = jnp.sum(jnp.where(lane == hg * n_heads + h, c, 0.0), axis=1, keepdims=True)
            cb_ref[h] = jnp.broadcast_to(-col, c.shape)

    heads = []
    for h in range(n_heads):
        t = h // 2
        qh = _head_q(q_ref[0][:, t * LANES:(t + 1) * LANES], h % 2)
        k_at = lambda j, t=t: k_ref[0, _rows(j), t * LANES:(t + 1) * LANES]
        vt_at = lambda j, h=h: vt_ref[0, h * HEAD_DIM:(h + 1) * HEAD_DIM, _rows(j)]
        bias_at = lambda j, h=h: [cb_ref[h, _rows(j), :]] * (TQ // LANES)
        heads.append((qh, k_at, vt_at, bias_at))
    _store_heads_t(o_ref, _flash_heads(heads, i, _causal_tile_t(False), (za_ref, zb_ref)))


def _fox(main3d, vt3d, c3d):
    b, s, _ = main3d.shape
    nq = s // TQ
    w = HEADS_PER_STEP * HEAD_DIM
    qb, kb, vb = COL_QB // w, COL_KB // w, ROW_VB // w
    return pl.pallas_call(
        _fox_kernel,
        grid=(b, FOX_HEADS // HEADS_PER_STEP, nq),
        in_specs=[
            pl.BlockSpec((1, TQ, w), lambda bi, hg, i: (bi, i, qb + hg)),
            pl.BlockSpec((1, s, w), lambda bi, hg, i: (bi, 0, kb + hg)),
            pl.BlockSpec((1, w, s), lambda bi, hg, i: (bi, vb + hg, 0)),
            pl.BlockSpec((1, s, LANES), lambda bi, hg, i: (bi, 0, 0)),
        ],
        out_specs=pl.BlockSpec((1, TQ, w), lambda bi, hg, i: (bi, i, hg)),
        out_shape=jax.ShapeDtypeStruct((b, s, BRANCH_WIDTH), MXU_DTYPE),
        scratch_shapes=[pltpu.VMEM((HEADS_PER_STEP, s, LANES), F32),
                        pltpu.VMEM((HEADS_PER_STEP, TK, TQ), F32), pltpu.VMEM((HEADS_PER_STEP, TK, TQ), F32)],
        compiler_params=_cparams(("parallel", "parallel", "arbitrary")),
        name="fox",
    )(main3d, main3d, vt3d, c3d)


def _dsa_kernel(q_ref, k_ref, vt_ref, qi_ref, ki_ref, misc_ref, o_ref, key_ref, dig_ref, bias_ref,
                za_ref, zb_ref, *, top_k):
    i = pl.program_id(1)
    diag = _causal_tile_t(False)

    wi_t = misc_ref[0].T[MISC_WI:MISC_WI + IDX_HEADS, :] * IDX_W_SCALE
    qih = [_head_q(qi_ref[0][:, (h // 2) * LANES:(h // 2 + 1) * LANES], h % 2) for h in range(IDX_HEADS)]

    def score_block(j, mask):
        kij = ki_ref[0, _rows(j), :]
        sc = None
        for h in range(IDX_HEADS):
            term = wi_t[h:h + 1, :] * jnp.maximum(_dot_nt(kij, qih[h]), 0.0)
            sc = term if sc is None else sc + term
        if mask is not None:
            sc = jnp.where(mask, sc, -jnp.inf)
        bits = lax.bitcast_convert_type(jnp.where(sc == 0.0, 0.0, sc), jnp.int32)
        key_ref[_rows(j), :] = bits ^ ((bits >> 31) & jnp.int32(0x7FFFFFFF))

    def score_pair(t, _):
        score_block(2 * t, None)
        score_block(2 * t + 1, None)
        return 0

    def score_odd_tail():
        score_block(i - 1, None)
        score_block(i, diag)
        return 0

    def score_even_tail():
        score_block(i, diag)
        return 0

    lax.fori_loop(0, i // 2, score_pair, 0)
    lax.cond(i % 2 == 1, score_odd_tail, score_even_tail)

    small = MXU_DTYPE
    group = 2 * SUBLANES

    def count_digits_ge(cand):
        def body(j, acc):
            hit = jnp.where(dig_ref[_rows(j), :] >= cand, jnp.ones((), small), jnp.zeros((), small))
            part = hit[0:group]
            for g in range(1, TK // group):
                part = part + hit[g * group:(g + 1) * group]
            return acc + part.astype(F32)
        acc = lax.fori_loop(0, i + 1, body, jnp.zeros((group, TQ), F32))
        return jnp.sum(acc, axis=0, keepdims=True)

    above = jnp.zeros((1, TQ), F32)
    prefix = None
    for shift in (24, 16, 8, 0):
        def fill(j, _, shift=shift, prefix=prefix):
            key = key_ref[_rows(j), :]
            if prefix is None:
                digit = (key >> shift) + 128
            else:
                digit = (key ^ (prefix << (shift + 8))) >> shift
                shares = lax.bitcast_convert_type(digit, jnp.uint32) < jnp.uint32(256)
                digit = jnp.where(shares, digit, -1)
            dig_ref[_rows(j), :] = digit.astype(F32).astype(small)
            return 0

        lax.fori_loop(0, i + 1, fill, 0)

        def search(_, st, above=above):
            lo, hi, n_hi = st
            mid = (lo + hi) * 0.5
            n_mid = count_digits_ge(mid.astype(small))
            ok = above + n_mid >= top_k
            return jnp.where(ok, mid, lo), jnp.where(ok, hi, mid), jnp.where(ok, n_hi, n_mid)

        start = (jnp.zeros((1, TQ), F32), jnp.full((1, TQ), 256.0, F32), jnp.zeros((1, TQ), F32))
        digit, _, n_hi = lax.fori_loop(0, 8, search, start)
        above = above + n_hi
        digit = digit.astype(jnp.int32)
        prefix = digit - 128 if prefix is None else (prefix << 8) | digit
    thr_key = prefix

    need = top_k - above
    kr = lax.broadcasted_iota(jnp.int32, (TK, TK), 0)
    kc = lax.broadcasted_iota(jnp.int32, (TK, TK), 1)
    upto = (kc <= kr).astype(MXU_DTYPE)

    def bias_block(j, mask, seen):
        key = key_ref[_rows(j), :]
        tie = key == thr_key
        rank = _dot(upto, jnp.where(tie, 1.0, 0.0).astype(MXU_DTYPE)) + seen
        sel = (key > thr_key) | (tie & (rank <= need))
        if mask is not None:
            sel = sel & mask
        bias_ref[_rows(j), :] = jnp.where(sel, 0.0, NEG_BIG)
        return rank[TK - 1:TK, :]

    def bias_pair(t, seen):
        return bias_block(2 * t + 1, None, bias_block(2 * t, None, seen))

    def bias_odd_tail(seen):
        bias_block(i, diag, bias_block(i - 1, None, seen))
        return 0

    def bias_even_tail(seen):
        bias_block(i, diag, seen)
        return 0

    seen = lax.fori_loop(0, i // 2, bias_pair, jnp.zeros((1, TQ), F32))
    lax.cond(i % 2 == 1, bias_odd_tail, bias_even_tail, seen)

    bias_at = lambda j: bias_ref[_rows(j), :]
    heads = []
    for h in range(DSA_HEADS):
        g = h // DSA_GROUP
        k_at = lambda j, g=g: k_ref[0, _rows(j), g * LANES:(g + 1) * LANES]
        vt_at = lambda j, g=g: vt_ref[0, g * HEAD_DIM:(g + 1) * HEAD_DIM, _rows(j)]
        qh = _head_q(q_ref[0][:, (h // 2) * LANES:(h // 2 + 1) * LANES], h % 2)
        heads.append((qh, k_at, vt_at, bias_at))
    _store_heads_t(o_ref, _flash_heads(heads, i, None, (za_ref, zb_ref)))


def _dsa(main3d, vt3d, misc3d, top_k):
    b, s, _ = main3d.shape
    nq = s // TQ
    assert top_k <= TK, "the first key block must hold at least top_k keys"
    return pl.pallas_call(
        functools.partial(_dsa_kernel, top_k=top_k),
        grid=(b, nq),
        in_specs=[
            pl.BlockSpec((1, TQ, 4 * LANES), lambda bi, i: (bi, i, COL_QC // (4 * LANES))),
            pl.BlockSpec((1, s, 2 * LANES), lambda bi, i: (bi, 0, COL_KC // (2 * LANES))),
            pl.BlockSpec((1, LANES, s), lambda bi, i: (bi, ROW_VC // LANES, 0)),
            pl.BlockSpec((1, TQ, 2 * LANES), lambda bi, i: (bi, i, COL_QI // (2 * LANES))),
            pl.BlockSpec((1, s, LANES), lambda bi, i: (bi, 0, COL_KI // LANES)),
            pl.BlockSpec((1, TQ, LANES), lambda bi, i: (bi, i, 0)),
        ],
        out_specs=pl.BlockSpec((1, TQ, BRANCH_WIDTH), lambda bi, i: (bi, i, 0)),
        out_shape=jax.ShapeDtypeStruct((b, s, BRANCH_WIDTH), MXU_DTYPE),
        scratch_shapes=[pltpu.VMEM((s, TQ), jnp.int32), pltpu.VMEM((s, TQ), MXU_DTYPE), pltpu.VMEM((s, TQ), F32),
                        pltpu.VMEM((DSA_HEADS, TK, TQ), F32), pltpu.VMEM((DSA_HEADS, TK, TQ), F32)],
        compiler_params=_cparams(("parallel", "arbitrary")),
        name="dsa",
    )(main3d, main3d, vt3d, main3d, main3d, misc3d)


def _merge_kernel(x_ref, a_ref, b_ref, c_ref, g_ref, wg_ref, wb_ref, wo_ref, o_ref):
    x = x_ref[...]
    d = x.shape[1]
    h = _rmsnorm(x, g_ref[...]).astype(MXU_DTYPE)
    y = None
    for n, br_ref in enumerate((a_ref, b_ref, c_ref)):
        gate = jax.nn.sigmoid(_dot(h, wg_ref[:, n * d:(n + 1) * d]))
        term = gate * _dot(br_ref[...], wb_ref[n])
        y = term if y is None else y + term
    o_ref[...] = x + _dot(y.astype(MXU_DTYPE), wo_ref[...])


def _merge(x2d, br_a, br_b, br_c, g, w_gate, w_branch, w_out, tm):
    n, d = x2d.shape
    bw = br_a.shape[1]
    row = lambda i: (i, 0)
    const = lambda i: (0, 0)
    return pl.pallas_call(
        _merge_kernel,
        grid=(n // tm,),
        in_specs=[
            pl.BlockSpec((tm, d), row),
            pl.BlockSpec((tm, bw), row),
            pl.BlockSpec((tm, bw), row),
            pl.BlockSpec((tm, bw), row),
            pl.BlockSpec((1, d), const),
            pl.BlockSpec((d, N_BRANCH * d), const),
            pl.BlockSpec((N_BRANCH, bw, d), lambda i: (0, 0, 0)),
            pl.BlockSpec((d, d), const),
        ],
        out_specs=pl.BlockSpec((tm, d), row),
        out_shape=jax.ShapeDtypeStruct((n, d), F32),
        compiler_params=_cparams(("parallel",)),
        name="merge",
    )(x2d, br_a, br_b, br_c, g, w_gate, w_branch, w_out)


def _mlp_kernel(x_ref, g_ref, wu_ref, wd_ref, gf_ref, o_ref, *, ff_chunk, final_norm):
    x = x_ref[...]
    h = _rmsnorm(x, g_ref[...]).astype(MXU_DTYPE)
    acc = x
    for c in range(wu_ref.shape[1] // ff_chunk):
        u = jnp.maximum(_dot(h, wu_ref[:, c * ff_chunk:(c + 1) * ff_chunk]), 0.0)
        acc = acc + _dot((u * u).astype(MXU_DTYPE), wd_ref[c * ff_chunk:(c + 1) * ff_chunk, :])
    if final_norm:
        acc = _rmsnorm(acc, gf_ref[...])
    o_ref[...] = acc


def _mlp(x2d, g, w_up, w_down, g_final, tm, final_norm):
    n, d = x2d.shape
    ff = w_up.shape[1]
    row = lambda i: (i, 0)
    const = lambda i: (0, 0)
    return pl.pallas_call(
        functools.partial(_mlp_kernel, ff_chunk=min(ff, 1024), final_norm=final_norm),
        grid=(n // tm,),
        in_specs=[
            pl.BlockSpec((tm, d), row),
            pl.BlockSpec((1, d), const),
            pl.BlockSpec((d, ff), const),
            pl.BlockSpec((ff, d), const),
            pl.BlockSpec((1, d), const),
        ],
        out_specs=pl.BlockSpec((tm, d), row),
        out_shape=jax.ShapeDtypeStruct((n, d), F32),
        compiler_params=_cparams(("parallel",)),
        name="mlp",
    )(x2d, g, w_up, w_down, g_final)


def _w_in_columns():
    sizes = (("qa", BRANCH_WIDTH), ("ka", BRANCH_WIDTH), ("va", BRANCH_WIDTH),
             ("qb", BRANCH_WIDTH), ("kb", BRANCH_WIDTH), ("vb", BRANCH_WIDTH), ("fl", FOX_HEADS),
             ("qc", DSA_HEADS * HEAD_DIM), ("kc", DSA_KV_HEADS * HEAD_DIM), ("vc", DSA_KV_HEADS * HEAD_DIM),
             ("qi", IDX_HEADS * IDX_DIM), ("ki", IDX_DIM), ("wi", IDX_HEADS), ("gates", N_BRANCH * 1024))
    cols, o = {}, 0
    for name, width in sizes:
        cols[name] = (o, o + width)
        o += width
    return cols, o


def _pack_kernel(w_ref, main_ref, vals_ref, gates_ref):
    cols, _ = _w_in_columns()

    def src(name, lo=0, hi=None):
        a, b = cols[name]
        return w_ref[0, :, a + lo:(b if hi is None else a + hi)]

    def put(ref, start, value):
        ref[0, :, start:start + value.shape[1]] = value.astype(ref.dtype)

    put(main_ref, COL_QA, src("qa"))
    put(main_ref, COL_KA, src("ka"))
    put(main_ref, COL_QB, src("qb"))
    put(main_ref, COL_KB, src("kb"))
    put(main_ref, COL_QC, src("qc"))
    for g in range(DSA_KV_HEADS):
        kg = src("kc", g * HEAD_DIM, (g + 1) * HEAD_DIM)
        put(main_ref, COL_KC + g * LANES, jnp.concatenate([kg, kg], axis=1))
    put(main_ref, COL_QI, src("qi"))
    ki = src("ki")
    put(main_ref, COL_KI, jnp.concatenate([ki, ki], axis=1))
    put(vals_ref, ROW_VA, src("va"))
    put(vals_ref, ROW_VB, src("vb"))
    put(vals_ref, ROW_VC, src("vc"))
    pad = jnp.zeros((w_ref.shape[1], LANES - FOX_HEADS - IDX_HEADS), F32)
    put(vals_ref, VT_ROWS, jnp.concatenate([src("fl"), src("wi"), pad], axis=1))
    put(gates_ref, 0, src("gates"))


def _pack_w_in(w_in):
    depth, d, total = w_in.shape
    cols, end = _w_in_columns()
    assert end == total and cols["gates"][1] - cols["gates"][0] == N_BRANCH * d
    rows = 256
    blk = lambda width: pl.BlockSpec((1, rows, width), lambda l, r: (l, r, 0))
    widths = (MAIN_COLS, VALS_COLS, N_BRANCH * d)
    return pl.pallas_call(
        _pack_kernel,
        grid=(depth, d // rows),
        in_specs=[blk(total)],
        out_specs=[blk(wd) for wd in widths],
        out_shape=[jax.ShapeDtypeStruct((depth, d, wd), MXU_DTYPE) for wd in widths],
        compiler_params=_cparams(("parallel", "parallel")),
        name="pack",
    )(w_in)


def _rope_rows():
    lane = jnp.arange(LANES) % HEAD_DIM
    half = ROT_DIM // 2
    inv_freq = ROPE_THETA ** (-jnp.arange(0, ROT_DIM, 2, dtype=F32) / ROT_DIM)
    freq = jnp.where(lane < ROT_DIM, inv_freq[lane % half], 0.0).astype(F32)
    sg1 = jnp.where(lane < half, -1.0, 0.0).astype(F32)
    sg2 = jnp.where((lane >= half) & (lane < ROT_DIM), 1.0, 0.0).astype(F32)
    return freq[None, :], sg1[None, :], sg2[None, :]


def kernel(x, positions, g_mix, w_in, b_forget, w_branch, w_out, g_mlp, w_up, w_down, g_final):
    b, s, d = x.shape
    n = b * s
    depth = w_in.shape[0]
    top_k = min(TOPK_MAX, s // 4)
    tm = min(512, s)
    freq, sg1, sg2 = _rope_rows()
    pos2d = positions.reshape(n, 1)
    x2d = x.reshape(n, d)
    w_main_all, w_v_all, w_gate_all = _pack_w_in(w_in)
    for layer in range(depth):
        w_main, w_v, w_gate = w_main_all[layer], w_v_all[layer], w_gate_all[layer]
        bias_row = jnp.concatenate([b_forget[layer], jnp.zeros((LANES - FOX_HEADS,), F32)])[None, :]
        main, vt, misc = _proj(x2d, pos2d, g_mix[layer][None, :], freq, sg1, sg2, w_main, w_v, tm, b, s)
        main3d = main.reshape(b, s, MAIN_COLS)
        misc3d = misc.reshape(b, s, LANES)
        cum = _cumf(misc3d, bias_row)
        br_a = _sb(main3d, vt).reshape(n, BRANCH_WIDTH)
        br_b = _fox(main3d, vt, cum).reshape(n, BRANCH_WIDTH)
        br_c = _dsa(main3d, vt, misc3d, top_k).reshape(n, BRANCH_WIDTH)
        x2d = _merge(x2d, br_a, br_b, br_c, g_mix[layer][None, :], w_gate,
                     w_branch[layer].astype(MXU_DTYPE), w_out[layer].astype(MXU_DTYPE), tm)
        x2d = _mlp(x2d, g_mlp[layer][None, :], w_up[layer].astype(MXU_DTYPE),
                   w_down[layer].astype(MXU_DTYPE), g_final[None, :], tm, layer == depth - 1)
    return x2d.reshape(b, s, d)
```

```python
import functools

import jax
import jax.numpy as jnp
from jax import lax
from jax.experimental import pallas as pl
from jax.experimental.pallas import tpu as pltpu

F32 = jnp.float32
MXU_DTYPE = jnp.bfloat16

HEAD_DIM = 64
SB_HEADS = 8
FOX_HEADS = 8
DSA_HEADS = 8
DSA_KV_HEADS = 2
DSA_GROUP = DSA_HEADS // DSA_KV_HEADS
IDX_HEADS = 4
IDX_DIM = 64
N_BRANCH = 3
ROPE_THETA = 500000.0
ROT_DIM = HEAD_DIM // 4
TOPK_MAX = 256
NORM_EPS = 1e-6

LANES = 128
SUBLANES = 8
SUM_ROWS = 2 * SUBLANES
BRANCH_WIDTH = SB_HEADS * HEAD_DIM
ATTN_SCALE = HEAD_DIM ** -0.5
assert IDX_DIM == HEAD_DIM
IDX_W_SCALE = IDX_HEADS ** -0.5

COL_QA, COL_KA = 0, 512
COL_QB, COL_KB = 1024, 1536
COL_QC = 2048
COL_KC = 2560
COL_QI = 2816
COL_KI = 3072
MAIN_COLS = 3200
ROPE_START = COL_QC
ROW_VA, ROW_VB, ROW_VC = 0, 512, 1024
VT_ROWS = 1152
VALS_COLS = VT_ROWS + LANES
MISC_FL, MISC_WI = 0, 8

TQ = 256
TK = 256
HEADS_PER_STEP = 8
NEG_BIG = -1e30

VMEM_LIMIT = 56 * 1024 * 1024


def _cparams(sem):
    return pltpu.CompilerParams(dimension_semantics=sem, vmem_limit_bytes=VMEM_LIMIT)


def _rmsnorm(x, g):
    y = x * lax.rsqrt(jnp.mean(x * x, axis=-1, keepdims=True) + NORM_EPS)
    return y * g


def _dot(a, b):
    return jnp.dot(a, b, preferred_element_type=F32)


def _dot_nt(a, b):
    return lax.dot_general(a, b, (((1,), (1,)), ((), ())), preferred_element_type=F32)


def _split2(x):
    hi = x.astype(MXU_DTYPE)
    lo = (x - hi.astype(F32)).astype(MXU_DTYPE)
    return hi, lo


def _split3(x):
    hi = x.astype(MXU_DTYPE)
    r = x - hi.astype(F32)
    mid = r.astype(MXU_DTYPE)
    lo = (r - mid.astype(F32)).astype(MXU_DTYPE)
    return hi, mid, lo


def _log_sigmoid_parts(z):
    neg_abs = lax.bitcast_convert_type(lax.bitcast_convert_type(z, jnp.int32) | jnp.int32(-2 ** 31), F32)
    sp = jnp.log(1.0 + jnp.exp(neg_abs))
    return jnp.minimum(z, 0.0) - sp, jnp.maximum(z, 0.0) + sp


def _proj_kernel(x_ref, pos_ref, g_ref, freq_ref, sg1_ref, sg2_ref, wm_ref, wv_ref,
                 main_ref, vt_ref, misc_ref):
    h = _rmsnorm(x_ref[...], g_ref[...]).astype(MXU_DTYPE)
    ang = pos_ref[...].astype(F32) * freq_ref[...]
    cos = jnp.cos(ang)
    sin = jnp.sin(ang)
    s_up = sin * sg1_ref[...]
    s_dn = sin * sg2_ref[...]
    half = ROT_DIM // 2
    chunk = 4 * LANES
    for c0 in range(0, MAIN_COLS, chunk):
        c1 = min(c0 + chunk, MAIN_COLS)
        p = _dot(h, wm_ref[:, c0:c1])
        if c0 >= ROPE_START:
            for t in range((c1 - c0) // LANES):
                pt = p[:, t * LANES:(t + 1) * LANES]
                pt = (pt * cos + pltpu.roll(pt, LANES - half, 1) * s_up
                      + pltpu.roll(pt, half, 1) * s_dn)
                main_ref[:, c0 + t * LANES:c0 + (t + 1) * LANES] = pt.astype(main_ref.dtype)
        else:
            main_ref[:, c0:c1] = p.astype(main_ref.dtype)
    for c0 in range(0, VALS_COLS, 2 * LANES):
        pv = _dot(h, wv_ref[:, c0:c0 + 2 * LANES])
        for r0 in (c0, c0 + LANES):
            piece = pv[:, r0 - c0:r0 - c0 + LANES]
            if r0 < VT_ROWS:
                vt_ref[0, r0:r0 + LANES, :] = piece.T.astype(vt_ref.dtype)
            else:
                misc_ref[...] = piece


def _proj(x2d, pos2d, g, freq, sg1, sg2, w_main, w_v, tm, batch, seq):
    n, d = x2d.shape
    nsb = seq // tm
    const = lambda i: (0, 0)
    return pl.pallas_call(
        _proj_kernel,
        grid=(n // tm,),
        in_specs=[
            pl.BlockSpec((tm, d), lambda i: (i, 0)),
            pl.BlockSpec((tm, 1), lambda i: (i, 0)),
            pl.BlockSpec((1, d), const),
            pl.BlockSpec((1, LANES), const),
            pl.BlockSpec((1, LANES), const),
            pl.BlockSpec((1, LANES), const),
            pl.BlockSpec((d, MAIN_COLS), const),
            pl.BlockSpec((d, VALS_COLS), const),
        ],
        out_specs=[
            pl.BlockSpec((tm, MAIN_COLS), lambda i: (i, 0)),
            pl.BlockSpec((1, VT_ROWS, tm), lambda i: (i // nsb, 0, i % nsb)),
            pl.BlockSpec((tm, LANES), lambda i: (i, 0)),
        ],
        out_shape=[
            jax.ShapeDtypeStruct((n, MAIN_COLS), MXU_DTYPE),
            jax.ShapeDtypeStruct((batch, VT_ROWS, seq), MXU_DTYPE),
            jax.ShapeDtypeStruct((n, LANES), F32),
        ],
        compiler_params=_cparams(("parallel",)),
        name="proj",
    )(x2d, pos2d, g, freq, sg1, sg2, w_main, w_v)


def _cumf_kernel(misc_ref, bias_ref, c_ref):
    s = misc_ref.shape[1]
    r = lax.broadcasted_iota(jnp.int32, (TK, TK), 0)
    c = lax.broadcasted_iota(jnp.int32, (TK, TK), 1)
    incl = (c <= r).astype(MXU_DTYPE)
    carry = jnp.zeros((1, LANES), F32)
    for b in range(s // TK):
        logit = misc_ref[0, b * TK:(b + 1) * TK, :] + bias_ref[...]
        logf, _ = _log_sigmoid_parts(logit)
        hi, mid, lo = _split3(logf)
        cs = (_dot(incl, lo) + _dot(incl, mid)) + _dot(incl, hi) + carry
        c_ref[0, b * TK:(b + 1) * TK, :] = cs
        carry = cs[TK - 1:TK, :]


def _cumf(misc3d, bias_row):
    b, s, _ = misc3d.shape
    return pl.pallas_call(
        _cumf_kernel,
        grid=(b,),
        in_specs=[
            pl.BlockSpec((1, s, LANES), lambda i: (i, 0, 0)),
            pl.BlockSpec((1, LANES), lambda i: (0, 0)),
        ],
        out_specs=pl.BlockSpec((1, s, LANES), lambda i: (i, 0, 0)),
        out_shape=jax.ShapeDtypeStruct((b, s, LANES), F32),
        compiler_params=_cparams(("parallel",)),
        name="cumf",
    )(misc3d, bias_row)


def _head_q(q_tile, half):
    lane = lax.broadcasted_iota(jnp.int32, q_tile.shape, 1)
    keep = (lane // HEAD_DIM) == half
    return jnp.where(keep, q_tile.astype(F32) * ATTN_SCALE, 0.0).astype(MXU_DTYPE)


def _causal_tile_t(strict):
    key = lax.broadcasted_iota(jnp.int32, (TK, TQ), 0)
    qry = lax.broadcasted_iota(jnp.int32, (TK, TQ), 1)
    return (key < qry) if strict else (key <= qry)


def _rows(j):
    return pl.ds(pl.multiple_of(j * TK, TK), TK)


def _lookahead_loop(n_last, bufs, issue, step, carry, primed=False):
    buf_a, buf_b = bufs

    def pair(t, carry):
        n = 2 * t
        issue(n + 1, buf_b)
        carry = step(n, buf_a, carry, False)
        issue(n + 2, buf_a)
        return step(n + 1, buf_b, carry, False)

    def odd_tail(carry):
        issue(n_last, buf_b)
        carry = step(n_last - 1, buf_a, carry, False)
        return step(n_last, buf_b, carry, True)

    def even_tail(carry):
        return step(n_last, buf_a, carry, True)

    if not primed:
        issue(0, buf_a)
    carry = lax.fori_loop(0, n_last // 2, pair, carry)
    return lax.cond(n_last % 2 == 1, odd_tail, even_tail, carry)


def _flash_heads(heads, n_full, diag_mask, z_bufs, acc_ref):
    def add_bias(s, bias):
        if isinstance(bias, (list, tuple)):
            return jnp.concatenate([s[:, t * LANES:(t + 1) * LANES] + bias[t] for t in range(len(bias))],
                                   axis=1)
        return s + bias

    ones = jnp.ones((SUM_ROWS, TK), MXU_DTYPE)

    def issue(j, buf):
        for h, (qh, k_at, _, _) in enumerate(heads):
            buf[h] = _dot_nt(k_at(j), qh)

    def step(j, buf, maxes, mask):
        stats = []
        for h, ((_, _, _, bias_at), m) in enumerate(zip(heads, maxes)):
            s = add_bias(buf[h], bias_at(j))
            if mask is not None:
                s = jnp.where(mask, s, NEG_BIG)
            m_new = jnp.maximum(m, jnp.max(s, axis=0, keepdims=True))
            stats.append((m_new, jnp.exp(m - m_new), jnp.exp(s - m_new).astype(MXU_DTYPE)))
        for h, ((_, alpha, p), (_, _, vt_at, _)) in enumerate(zip(stats, heads)):
            vt1 = jnp.concatenate([vt_at(j), ones], axis=0)
            acc_ref[h] = alpha * acc_ref[h] + _dot(vt1, p)
        return tuple(m_new for m_new, _, _ in stats)

    acc_ref[...] = jnp.zeros(acc_ref.shape, F32)
    init = tuple(jnp.full((1, TQ), NEG_BIG, F32) for _ in heads)
    _lookahead_loop(n_full, z_bufs, issue,
                    lambda j, buf, maxes, last: step(j, buf, maxes, diag_mask if last else None), init)
    return [acc_ref[h, :HEAD_DIM, :] / acc_ref[h, HEAD_DIM:HEAD_DIM + 1, :] for h in range(len(heads))]


def _store_heads_t(o_ref, outs_t):
    for t in range(len(outs_t) // 2):
        pair_t = jnp.concatenate([outs_t[2 * t], outs_t[2 * t + 1]], axis=0)
        o_ref[0, :, t * LANES:(t + 1) * LANES] = pair_t.T.astype(o_ref.dtype)


def _sb_kernel(q_ref, k_ref, vt_ref, o_ref, za_ref, zb_ref, acc_ref):
    i = pl.program_id(2)
    r = lax.broadcasted_iota(jnp.int32, (TK, TK), 0)
    c = lax.broadcasted_iota(jnp.int32, (TK, TK), 1)
    after = (c > r).astype(MXU_DTYPE)
    after2 = jnp.concatenate([after, after], axis=1)
    strict = _causal_tile_t(True)
    n_heads = HEADS_PER_STEP
    qhs = [_head_q(q_ref[0][:, (h // 2) * LANES:(h // 2 + 1) * LANES], h % 2) for h in range(n_heads)]

    def issue(j, buf):
        for h in range(n_heads):
            buf[h] = _dot_nt(k_ref[0, _rows(j), (h // 2) * LANES:(h // 2 + 1) * LANES], qhs[h])

    def step(j, buf, carry, mask):
        rows = _rows(j)
        parts = []
        for h in range(n_heads):
            log_beta, neg_l1m = _log_sigmoid_parts(buf[h])
            if mask is not None:
                neg_l1m = jnp.where(mask, neg_l1m, 0.0)
            hi, lo = _split2(neg_l1m)
            parts.append((log_beta, neg_l1m[0:1, :], jnp.concatenate([hi, lo], axis=0)))
        laters = [_dot(after2, hl) for (_, _, hl) in parts]
        ws = []
        for (log_beta, _, _), later, tail in zip(parts, laters, carry):
            w = jnp.exp(log_beta - (later + tail))
            if mask is not None:
                w = jnp.where(mask, w, 0.0)
            ws.append(w.astype(MXU_DTYPE))
        out = []
        for h, (w, (_, first, _), later, tail) in enumerate(zip(ws, parts, laters, carry)):
            vtj = vt_ref[0, h * HEAD_DIM:(h + 1) * HEAD_DIM, rows]
            acc_ref[h] = acc_ref[h] + _dot(vtj, w)
            out.append(tail + (first + later[0:1, :]))
        return tuple(out)

    acc_ref[...] = jnp.zeros(acc_ref.shape, F32)
    carry = tuple(jnp.zeros((1, TQ), F32) for _ in range(n_heads))
    issue(i, za_ref)

    def with_earlier_blocks(cr):
        issue(i - 1, zb_ref)
        cr = step(i, za_ref, cr, strict)
        return _lookahead_loop(i - 1, (zb_ref, za_ref), lambda n, buf: issue(i - 1 - n, buf),
                               lambda n, buf, c, last: step(i - 1 - n, buf, c, None), cr, primed=True)

    lax.cond(i > 0, with_earlier_blocks, lambda cr: step(i, za_ref, cr, strict), carry)
    _store_heads_t(o_ref, [acc_ref[h] for h in range(n_heads)])


def _sb(main3d, vt3d):
    b, s, _ = main3d.shape
    nq = s // TQ
    w = HEADS_PER_STEP * HEAD_DIM
    qb, kb, vb = COL_QA // w, COL_KA // w, ROW_VA // w
    return pl.pallas_call(
        _sb_kernel,
        grid=(b, SB_HEADS // HEADS_PER_STEP, nq),
        in_specs=[
            pl.BlockSpec((1, TQ, w), lambda bi, hg, i: (bi, i, qb + hg)),
            pl.BlockSpec((1, s, w), lambda bi, hg, i: (bi, 0, kb + hg)),
            pl.BlockSpec((1, w, s), lambda bi, hg, i: (bi, vb + hg, 0)),
        ],
        out_specs=pl.BlockSpec((1, TQ, w), lambda bi, hg, i: (bi, i, hg)),
        out_shape=jax.ShapeDtypeStruct((b, s, BRANCH_WIDTH), MXU_DTYPE),
        scratch_shapes=[pltpu.VMEM((HEADS_PER_STEP, TK, TQ), F32), pltpu.VMEM((HEADS_PER_STEP, TK, TQ), F32),
                        pltpu.VMEM((HEADS_PER_STEP, HEAD_DIM, TQ), F32)],
        compiler_params=_cparams(("parallel", "parallel", "arbitrary")),
        name="sb",
    )(main3d, main3d, vt3d)


def _fox_kernel(q_ref, k_ref, vt_ref, c_ref, o_ref, cb_ref, za_ref, zb_ref, acc_ref):
    hg = pl.program_id(1)
    i = pl.program_id(2)
    n_heads = HEADS_PER_STEP

    @pl.when(i == 0)
    def _():
        c = c_ref[0]
        lane = lax.broadcasted_iota(jnp.int32, c.shape, 1)
        for h in range(n_heads):
            col = jnp.sum(jnp.where(lane == hg * n_heads + h, c, 0.0), axis=1, keepdims=True)
            cb_ref[h] = jnp.broadcast_to(-col, c.shape)

    heads = []
    for h in range(n_heads):
        t = h // 2
        qh = _head_q(q_ref[0][:, t * LANES:(t + 1) * LANES], h % 2)
        k_at = lambda j, t=t: k_ref[0, _rows(j), t * LANES:(t + 1) * LANES]
        vt_at = lambda j, h=h: vt_ref[0, h * HEAD_DIM:(h + 1) * HEAD_DIM, _rows(j)]
        bias_at = lambda j, h=h: [cb_ref[h, _rows(j), :]] * (TQ // LANES)
        heads.append((qh, k_at, vt_at, bias_at))
    _store_heads_t(o_ref, _flash_heads(heads, i, _causal_tile_t(False), (za_ref, zb_ref), acc_ref))


def _fox(main3d, vt3d, c3d):
    b, s, _ = main3d.shape
    nq = s // TQ
    w = HEADS_PER_STEP * HEAD_DIM
    qb, kb, vb = COL_QB // w, COL_KB // w, ROW_VB // w
    return pl.pallas_call(
        _fox_kernel,
        grid=(b, FOX_HEADS // HEADS_PER_STEP, nq),
        in_specs=[
            pl.BlockSpec((1, TQ, w), lambda bi, hg, i: (bi, i, qb + hg)),
            pl.BlockSpec((1, s, w), lambda bi, hg, i: (bi, 0, kb + hg)),
            pl.BlockSpec((1, w, s), lambda bi, hg, i: (bi, vb + hg, 0)),
            pl.BlockSpec((1, s, LANES), lambda bi, hg, i: (bi, 0, 0)),
        ],
        out_specs=pl.BlockSpec((1, TQ, w), lambda bi, hg, i: (bi, i, hg)),
        out_shape=jax.ShapeDtypeStruct((b, s, BRANCH_WIDTH), MXU_DTYPE),
        scratch_shapes=[pltpu.VMEM((HEADS_PER_STEP, s, LANES), F32),
                        pltpu.VMEM((HEADS_PER_STEP, TK, TQ), F32), pltpu.VMEM((HEADS_PER_STEP, TK, TQ), F32),
                        pltpu.VMEM((HEADS_PER_STEP, HEAD_DIM + SUM_ROWS, TQ), F32)],
        compiler_params=_cparams(("parallel", "parallel", "arbitrary")),
        name="fox",
    )(main3d, main3d, vt3d, c3d)


def _dsa_kernel(q_ref, k_ref, vt_ref, qi_ref, ki_ref, misc_ref, o_ref, key_ref, dig_ref, bias_ref,
                za_ref, zb_ref, acc_ref, *, top_k):
    i = pl.program_id(1)
    diag = _causal_tile_t(False)

    wi_t = misc_ref[0].T[MISC_WI:MISC_WI + IDX_HEADS, :] * IDX_W_SCALE
    qih = [_head_q(qi_ref[0][:, (h // 2) * LANES:(h // 2 + 1) * LANES], h % 2) for h in range(IDX_HEADS)]

    def score_block(j, mask):
        kij = ki_ref[0, _rows(j), :]
        sc = None
        for h in range(IDX_HEADS):
            term = wi_t[h:h + 1, :] * jnp.maximum(_dot_nt(kij, qih[h]), 0.0)
            sc = term if sc is None else sc + term
        if mask is not None:
            sc = jnp.where(mask, sc, -jnp.inf)
        bits = lax.bitcast_convert_type(jnp.where(sc == 0.0, 0.0, sc), jnp.int32)
        key_ref[_rows(j), :] = bits ^ ((bits >> 31) & jnp.int32(0x7FFFFFFF))

    def score_pair(t, _):
        score_block(2 * t, None)
        score_block(2 * t + 1, None)
        return 0

    def score_odd_tail():
        score_block(i - 1, None)
        score_block(i, diag)
        return 0

    def score_even_tail():
        score_block(i, diag)
        return 0

    lax.fori_loop(0, i // 2, score_pair, 0)
    lax.cond(i % 2 == 1, score_odd_tail, score_even_tail)

    small = MXU_DTYPE
    group = 2 * SUBLANES

    def count_digits_ge(cand):
        def body(j, acc):
            hit = jnp.where(dig_ref[_rows(j), :] >= cand, jnp.ones((), small), jnp.zeros((), small))
            part = hit[0:group]
            for g in range(1, TK // group):
                part = part + hit[g * group:(g + 1) * group]
            return acc + part.astype(F32)
        acc = lax.fori_loop(0, i + 1, body, jnp.zeros((group, TQ), F32))
        return jnp.sum(acc, axis=0, keepdims=True)

    above = jnp.zeros((1, TQ), F32)
    prefix = None
    for shift in (24, 16, 8, 0):
        def fill(j, _, shift=shift, prefix=prefix):
            key = key_ref[_rows(j), :]
            if prefix is None:
                digit = (key >> shift) + 128
            else:
                digit = (key ^ (prefix << (shift + 8))) >> shift
                shares = lax.bitcast_convert_type(digit, jnp.uint32) < jnp.uint32(256)
                digit = jnp.where(shares, digit, -1)
            dig_ref[_rows(j), :] = digit.astype(F32).astype(small)
            return 0

        lax.fori_loop(0, i + 1, fill, 0)

        def search(_, st, above=above):
            lo, hi, n_hi = st
            mid = (lo + hi) * 0.5
            n_mid = count_digits_ge(mid.astype(small))
            ok = above + n_mid >= top_k
            return jnp.where(ok, mid, lo), jnp.where(ok, hi, mid), jnp.where(ok, n_hi, n_mid)

        start = (jnp.zeros((1, TQ), F32), jnp.full((1, TQ), 256.0, F32), jnp.zeros((1, TQ), F32))
        digit, _, n_hi = lax.fori_loop(0, 8, search, start)
        above = above + n_hi
        digit = digit.astype(jnp.int32)
        prefix = digit - 128 if prefix is None else (prefix << 8) | digit
    thr_key = prefix

    need = top_k - above
    kr = lax.broadcasted_iota(jnp.int32, (TK, TK), 0)
    kc = lax.broadcasted_iota(jnp.int32, (TK, TK), 1)
    upto = (kc <= kr).astype(MXU_DTYPE)

    def bias_block(j, mask, seen):
        key = key_ref[_rows(j), :]
        tie = key == thr_key
        rank = _dot(upto, jnp.where(tie, 1.0, 0.0).astype(MXU_DTYPE)) + seen
        sel = (key > thr_key) | (tie & (rank <= need))
        if mask is not None:
            sel = sel & mask
        bias_ref[_rows(j), :] = jnp.where(sel, 0.0, NEG_BIG)
        return rank[TK - 1:TK, :]

    def bias_pair(t, seen):
        return bias_block(2 * t + 1, None, bias_block(2 * t, None, seen))

    def bias_odd_tail(seen):
        bias_block(i, diag, bias_block(i - 1, None, seen))
        return 0

    def bias_even_tail(seen):
        bias_block(i, diag, seen)
        return 0

    seen = lax.fori_loop(0, i // 2, bias_pair, jnp.zeros((1, TQ), F32))
    lax.cond(i % 2 == 1, bias_odd_tail, bias_even_tail, seen)

    bias_at = lambda j: bias_ref[_rows(j), :]
    heads = []
    for h in range(DSA_HEADS):
        g = h // DSA_GROUP
        k_at = lambda j, g=g: k_ref[0, _rows(j), g * LANES:(g + 1) * LANES]
        vt_at = lambda j, g=g: vt_ref[0, g * HEAD_DIM:(g + 1) * HEAD_DIM, _rows(j)]
        qh = _head_q(q_ref[0][:, (h // 2) * LANES:(h // 2 + 1) * LANES], h % 2)
        heads.append((qh, k_at, vt_at, bias_at))
    _store_heads_t(o_ref, _flash_heads(heads, i, None, (za_ref, zb_ref), acc_ref))


def _dsa(main3d, vt3d, misc3d, top_k):
    b, s, _ = main3d.shape
    nq = s // TQ
    assert top_k <= TK, "the first key block must hold at least top_k keys"
    return pl.pallas_call(
        functools.partial(_dsa_kernel, top_k=top_k),
        grid=(b, nq),
        in_specs=[
            pl.BlockSpec((1, TQ, 4 * LANES), lambda bi, i: (bi, i, COL_QC // (4 * LANES))),
            pl.BlockSpec((1, s, 2 * LANES), lambda bi, i: (bi, 0, COL_KC // (2 * LANES))),
            pl.BlockSpec((1, LANES, s), lambda bi, i: (bi, ROW_VC // LANES, 0)),
            pl.BlockSpec((1, TQ, 2 * LANES), lambda bi, i: (bi, i, COL_QI // (2 * LANES))),
            pl.BlockSpec((1, s, LANES), lambda bi, i: (bi, 0, COL_KI // LANES)),
            pl.BlockSpec((1, TQ, LANES), lambda bi, i: (bi, i, 0)),
        ],
        out_specs=pl.BlockSpec((1, TQ, BRANCH_WIDTH), lambda bi, i: (bi, i, 0)),
        out_shape=jax.ShapeDtypeStruct((b, s, BRANCH_WIDTH), MXU_DTYPE),
        scratch_shapes=[pltpu.VMEM((s, TQ), jnp.int32), pltpu.VMEM((s, TQ), MXU_DTYPE), pltpu.VMEM((s, TQ), F32),
                        pltpu.VMEM((DSA_HEADS, TK, TQ), F32), pltpu.VMEM((DSA_HEADS, TK, TQ), F32),
                        pltpu.VMEM((DSA_HEADS, HEAD_DIM + SUM_ROWS, TQ), F32)],
        compiler_params=_cparams(("parallel", "arbitrary")),
        name="dsa",
    )(main3d, main3d, vt3d, main3d, main3d, misc3d)


def _merge_kernel(x_ref, a_ref, b_ref, c_ref, g_ref, wg_ref, wb_ref, wo_ref, o_ref):
    x = x_ref[...]
    d = x.shape[1]
    h = _rmsnorm(x, g_ref[...]).astype(MXU_DTYPE)
    y = None
    for n, br_ref in enumerate((a_ref, b_ref, c_ref)):
        gate = jax.nn.sigmoid(_dot(h, wg_ref[:, n * d:(n + 1) * d]))
        term = gate * _dot(br_ref[...], wb_ref[n])
        y = term if y is None else y + term
    o_ref[...] = x + _dot(y.astype(MXU_DTYPE), wo_ref[...])


def _merge(x2d, br_a, br_b, br_c, g, w_gate, w_branch, w_out, tm):
    n, d = x2d.shape
    bw = br_a.shape[1]
    row = lambda i: (i, 0)
    const = lambda i: (0, 0)
    return pl.pallas_call(
        _merge_kernel,
        grid=(n // tm,),
        in_specs=[
            pl.BlockSpec((tm, d), row),
            pl.BlockSpec((tm, bw), row),
            pl.BlockSpec((tm, bw), row),
            pl.BlockSpec((tm, bw), row),
            pl.BlockSpec((1, d), const),
            pl.BlockSpec((d, N_BRANCH * d), const),
            pl.BlockSpec((N_BRANCH, bw, d), lambda i: (0, 0, 0)),
            pl.BlockSpec((d, d), const),
        ],
        out_specs=pl.BlockSpec((tm, d), row),
        out_shape=jax.ShapeDtypeStruct((n, d), F32),
        compiler_params=_cparams(("parallel",)),
        name="merge",
    )(x2d, br_a, br_b, br_c, g, w_gate, w_branch, w_out)


def _mlp_kernel(x_ref, g_ref, wu_ref, wd_ref, gf_ref, o_ref, *, ff_chunk, final_norm):
    x = x_ref[...]
    h = _rmsnorm(x, g_ref[...]).astype(MXU_DTYPE)
    acc = x
    for c in range(wu_ref.shape[1] // ff_chunk):
        u = jnp.maximum(_dot(h, wu_ref[:, c * ff_chunk:(c + 1) * ff_chunk]), 0.0)
        acc = acc + _dot((u * u).astype(MXU_DTYPE), wd_ref[c * ff_chunk:(c + 1) * ff_chunk, :])
    if final_norm:
        acc = _rmsnorm(acc, gf_ref[...])
    o_ref[...] = acc


def _mlp(x2d, g, w_up, w_down, g_final, tm, final_norm):
    n, d = x2d.shape
    ff = w_up.shape[1]
    row = lambda i: (i, 0)
    const = lambda i: (0, 0)
    return pl.pallas_call(
        functools.partial(_mlp_kernel, ff_chunk=min(ff, 1024), final_norm=final_norm),
        grid=(n // tm,),
        in_specs=[
            pl.BlockSpec((tm, d), row),
            pl.BlockSpec((1, d), const),
            pl.BlockSpec((d, ff), const),
            pl.BlockSpec((ff, d), const),
            pl.BlockSpec((1, d), const),
        ],
        out_specs=pl.BlockSpec((tm, d), row),
        out_shape=jax.ShapeDtypeStruct((n, d), F32),
        compiler_params=_cparams(("parallel",)),
        name="mlp",
    )(x2d, g, w_up, w_down, g_final)


def _w_in_columns():
    sizes = (("qa", BRANCH_WIDTH), ("ka", BRANCH_WIDTH), ("va", BRANCH_WIDTH),
             ("qb", BRANCH_WIDTH), ("kb", BRANCH_WIDTH), ("vb", BRANCH_WIDTH), ("fl", FOX_HEADS),
             ("qc", DSA_HEADS * HEAD_DIM), ("kc", DSA_KV_HEADS * HEAD_DIM), ("vc", DSA_KV_HEADS * HEAD_DIM),
             ("qi", IDX_HEADS * IDX_DIM), ("ki", IDX_DIM), ("wi", IDX_HEADS), ("gates", N_BRANCH * 1024))
    cols, o = {}, 0
    for name, width in sizes:
        cols[name] = (o, o + width)
        o += width
    return cols, o


def _pack_kernel(w_ref, main_ref, vals_ref, gates_ref):
    cols, _ = _w_in_columns()

    def src(name, lo=0, hi=None):
        a, b = cols[name]
        return w_ref[0, :, a + lo:(b if hi is None else a + hi)]

    def put(ref, start, value):
        ref[0, :, start:start + value.shape[1]] = value.astype(ref.dtype)

    put(main_ref, COL_QA, src("qa"))
    put(main_ref, COL_KA, src("ka"))
    put(main_ref, COL_QB, src("qb"))
    put(main_ref, COL_KB, src("kb"))
    put(main_ref, COL_QC, src("qc"))
    for g in range(DSA_KV_HEADS):
        kg = src("kc", g * HEAD_DIM, (g + 1) * HEAD_DIM)
        put(main_ref, COL_KC + g * LANES, jnp.concatenate([kg, kg], axis=1))
    put(main_ref, COL_QI, src("qi"))
    ki = src("ki")
    put(main_ref, COL_KI, jnp.concatenate([ki, ki], axis=1))
    put(vals_ref, ROW_VA, src("va"))
    put(vals_ref, ROW_VB, src("vb"))
    put(vals_ref, ROW_VC, src("vc"))
    pad = jnp.zeros((w_ref.shape[1], LANES - FOX_HEADS - IDX_HEADS), F32)
    put(vals_ref, VT_ROWS, jnp.concatenate([src("fl"), src("wi"), pad], axis=1))
    put(gates_ref, 0, src("gates"))


def _pack_w_in(w_in):
    depth, d, total = w_in.shape
    cols, end = _w_in_columns()
    assert end == total and cols["gates"][1] - cols["gates"][0] == N_BRANCH * d
    rows = 256
    blk = lambda width: pl.BlockSpec((1, rows, width), lambda l, r: (l, r, 0))
    widths = (MAIN_COLS, VALS_COLS, N_BRANCH * d)
    return pl.pallas_call(
        _pack_kernel,
        grid=(depth, d // rows),
        in_specs=[blk(total)],
        out_specs=[blk(wd) for wd in widths],
        out_shape=[jax.ShapeDtypeStruct((depth, d, wd), MXU_DTYPE) for wd in widths],
        compiler_params=_cparams(("parallel", "parallel")),
        name="pack",
    )(w_in)


def _rope_rows():
    lane = jnp.arange(LANES) % HEAD_DIM
    half = ROT_DIM // 2
    inv_freq = ROPE_THETA ** (-jnp.arange(0, ROT_DIM, 2, dtype=F32) / ROT_DIM)
    freq = jnp.where(lane < ROT_DIM, inv_freq[lane % half], 0.0).astype(F32)
    sg1 = jnp.where(lane < half, -1.0, 0.0).astype(F32)
    sg2 = jnp.where((lane >= half) & (lane < ROT_DIM), 1.0, 0.0).astype(F32)
    return freq[None, :], sg1[None, :], sg2[None, :]


def kernel(x, positions, g_mix, w_in, b_forget, w_branch, w_out, g_mlp, w_up, w_down, g_final):
    b, s, d = x.shape
    n = b * s
    depth = w_in.shape[0]
    top_k = min(TOPK_MAX, s // 4)
    tm = min(512, s)
    freq, sg1, sg2 = _rope_rows()
    pos2d = positions.reshape(n, 1)
    x2d = x.reshape(n, d)
    w_main_all, w_v_all, w_gate_all = _pack_w_in(w_in)
    for layer in range(depth):
        w_main, w_v, w_gate = w_main_all[layer], w_v_all[layer], w_gate_all[layer]
        bias_row = jnp.concatenate([b_forget[layer], jnp.zeros((LANES - FOX_HEADS,), F32)])[None, :]
        main, vt, misc = _proj(x2d, pos2d, g_mix[layer][None, :], freq, sg1, sg2, w_main, w_v, tm, b, s)
        main3d = main.reshape(b, s, MAIN_COLS)
        misc3d = misc.reshape(b, s, LANES)
        cum = _cumf(misc3d, bias_row)
        br_a = _sb(main3d, vt).reshape(n, BRANCH_WIDTH)
        br_b = _fox(main3d, vt, cum).reshape(n, BRANCH_WIDTH)
        br_c = _dsa(main3d, vt, misc3d, top_k).reshape(n, BRANCH_WIDTH)
        x2d = _merge(x2d, br_a, br_b, br_c, g_mix[layer][None, :], w_gate,
                     w_branch[layer].astype(MXU_DTYPE), w_out[layer].astype(MXU_DTYPE), tm)
        x2d = _mlp(x2d, g_mlp[layer][None, :], w_up[layer].astype(MXU_DTYPE),
                   w_down[layer].astype(MXU_DTYPE), g_final[None, :], tm, layer == depth - 1)
    return x2d.reshape(b, s, d)
```

```python
import functools

import jax
import jax.numpy as jnp
from jax import lax
from jax.experimental import pallas as pl
from jax.experimental.pallas import tpu as pltpu

F32 = jnp.float32
MXU_DTYPE = jnp.bfloat16

HEAD_DIM = 64
SB_HEADS = 8
FOX_HEADS = 8
DSA_HEADS = 8
DSA_KV_HEADS = 2
DSA_GROUP = DSA_HEADS // DSA_KV_HEADS
IDX_HEADS = 4
IDX_DIM = 64
N_BRANCH = 3
ROPE_THETA = 500000.0
ROT_DIM = HEAD_DIM // 4
TOPK_MAX = 256
NORM_EPS = 1e-6

LANES = 128
SUBLANES = 8
SUM_ROWS = 2 * SUBLANES
BRANCH_WIDTH = SB_HEADS * HEAD_DIM
ATTN_SCALE = HEAD_DIM ** -0.5
assert IDX_DIM == HEAD_DIM
IDX_W_SCALE = IDX_HEADS ** -0.5

COL_QA, COL_KA = 0, 512
COL_QB, COL_KB = 1024, 1536
COL_QC = 2048
COL_KC = 2560
COL_QI = 2816
COL_KI = 3072
MAIN_COLS = 3200
ROPE_START = COL_QC
ROW_VA, ROW_VB, ROW_VC = 0, 512, 1024
VT_ROWS = 1152
VALS_COLS = VT_ROWS + LANES
MISC_FL, MISC_WI = 0, 8

TQ = 256
TK = 256
HEADS_PER_STEP = 8
NEG_BIG = -1e30

VMEM_LIMIT = 56 * 1024 * 1024


def _cparams(sem):
    return pltpu.CompilerParams(dimension_semantics=sem, vmem_limit_bytes=VMEM_LIMIT)


def _rmsnorm(x, g):
    y = x * lax.rsqrt(jnp.mean(x * x, axis=-1, keepdims=True) + NORM_EPS)
    return y * g


def _dot(a, b):
    return jnp.dot(a, b, preferred_element_type=F32)


def _dot_nt(a, b):
    return lax.dot_general(a, b, (((1,), (1,)), ((), ())), preferred_element_type=F32)


def _split2(x):
    hi = x.astype(MXU_DTYPE)
    lo = (x - hi.astype(F32)).astype(MXU_DTYPE)
    return hi, lo


def _split3(x):
    hi = x.astype(MXU_DTYPE)
    r = x - hi.astype(F32)
    mid = r.astype(MXU_DTYPE)
    lo = (r - mid.astype(F32)).astype(MXU_DTYPE)
    return hi, mid, lo


def _log_sigmoid_parts(z):
    neg_abs = lax.bitcast_convert_type(lax.bitcast_convert_type(z, jnp.int32) | jnp.int32(-2 ** 31), F32)
    sp = jnp.log(1.0 + jnp.exp(neg_abs))
    return jnp.minimum(z, 0.0) - sp, jnp.maximum(z, 0.0) + sp


def _proj_kernel(x_ref, pos_ref, g_ref, freq_ref, sg1_ref, sg2_ref, wm_ref, wv_ref,
                 main_ref, vt_ref, misc_ref):
    h = _rmsnorm(x_ref[...], g_ref[...]).astype(MXU_DTYPE)
    ang = pos_ref[...].astype(F32) * freq_ref[...]
    cos = jnp.cos(ang)
    sin = jnp.sin(ang)
    s_up = sin * sg1_ref[...]
    s_dn = sin * sg2_ref[...]
    half = ROT_DIM // 2
    chunk = 4 * LANES
    for c0 in range(0, MAIN_COLS, chunk):
        c1 = min(c0 + chunk, MAIN_COLS)
        p = _dot(h, wm_ref[:, c0:c1])
        if c0 >= ROPE_START:
            for t in range((c1 - c0) // LANES):
                pt = p[:, t * LANES:(t + 1) * LANES]
                pt = (pt * cos + pltpu.roll(pt, LANES - half, 1) * s_up
                      + pltpu.roll(pt, half, 1) * s_dn)
                main_ref[:, c0 + t * LANES:c0 + (t + 1) * LANES] = pt.astype(main_ref.dtype)
        else:
            main_ref[:, c0:c1] = p.astype(main_ref.dtype)
    for c0 in range(0, VALS_COLS, 2 * LANES):
        pv = _dot(h, wv_ref[:, c0:c0 + 2 * LANES])
        for r0 in (c0, c0 + LANES):
            piece = pv[:, r0 - c0:r0 - c0 + LANES]
            if r0 < VT_ROWS:
                vt_ref[0, r0:r0 + LANES, :] = piece.T.astype(vt_ref.dtype)
            else:
                misc_ref[...] = piece


def _proj(x2d, pos2d, g, freq, sg1, sg2, w_main, w_v, tm, batch, seq):
    n, d = x2d.shape
    nsb = seq // tm
    const = lambda i: (0, 0)
    return pl.pallas_call(
        _proj_kernel,
        grid=(n // tm,),
        in_specs=[
            pl.BlockSpec((tm, d), lambda i: (i, 0)),
            pl.BlockSpec((tm, 1), lambda i: (i, 0)),
            pl.BlockSpec((1, d), const),
            pl.BlockSpec((1, LANES), const),
            pl.BlockSpec((1, LANES), const),
            pl.BlockSpec((1, LANES), const),
            pl.BlockSpec((d, MAIN_COLS), const),
            pl.BlockSpec((d, VALS_COLS), const),
        ],
        out_specs=[
            pl.BlockSpec((tm, MAIN_COLS), lambda i: (i, 0)),
            pl.BlockSpec((1, VT_ROWS, tm), lambda i: (i // nsb, 0, i % nsb)),
            pl.BlockSpec((tm, LANES), lambda i: (i, 0)),
        ],
        out_shape=[
            jax.ShapeDtypeStruct((n, MAIN_COLS), MXU_DTYPE),
            jax.ShapeDtypeStruct((batch, VT_ROWS, seq), MXU_DTYPE),
            jax.ShapeDtypeStruct((n, LANES), F32),
        ],
        compiler_params=_cparams(("parallel",)),
        name="proj",
    )(x2d, pos2d, g, freq, sg1, sg2, w_main, w_v)


def _cumf_kernel(misc_ref, bias_ref, c_ref):
    s = misc_ref.shape[1]
    r = lax.broadcasted_iota(jnp.int32, (TK, TK), 0)
    c = lax.broadcasted_iota(jnp.int32, (TK, TK), 1)
    incl = (c <= r).astype(MXU_DTYPE)
    carry = jnp.zeros((1, LANES), F32)
    for b in range(s // TK):
        logit = misc_ref[0, b * TK:(b + 1) * TK, :] + bias_ref[...]
        logf, _ = _log_sigmoid_parts(logit)
        hi, mid, lo = _split3(logf)
        cs = (_dot(incl, lo) + _dot(incl, mid)) + _dot(incl, hi) + carry
        c_ref[0, b * TK:(b + 1) * TK, :] = cs
        carry = cs[TK - 1:TK, :]


def _cumf(misc3d, bias_row):
    b, s, _ = misc3d.shape
    return pl.pallas_call(
        _cumf_kernel,
        grid=(b,),
        in_specs=[
            pl.BlockSpec((1, s, LANES), lambda i: (i, 0, 0)),
            pl.BlockSpec((1, LANES), lambda i: (0, 0)),
        ],
        out_specs=pl.BlockSpec((1, s, LANES), lambda i: (i, 0, 0)),
        out_shape=jax.ShapeDtypeStruct((b, s, LANES), F32),
        compiler_params=_cparams(("parallel",)),
        name="cumf",
    )(misc3d, bias_row)


def _head_q(q_tile, half):
    lane = lax.broadcasted_iota(jnp.int32, q_tile.shape, 1)
    keep = (lane // HEAD_DIM) == half
    return jnp.where(keep, q_tile.astype(F32) * ATTN_SCALE, 0.0).astype(MXU_DTYPE)


def _causal_tile_t(strict):
    key = lax.broadcasted_iota(jnp.int32, (TK, TQ), 0)
    qry = lax.broadcasted_iota(jnp.int32, (TK, TQ), 1)
    return (key < qry) if strict else (key <= qry)


def _rows(j):
    return pl.ds(pl.multiple_of(j * TK, TK), TK)


def _lookahead_loop(n_last, bufs, issue, step, carry, primed=False):
    buf_a, buf_b = bufs

    def pair(t, carry):
        n = 2 * t
        issue(n + 1, buf_b)
        carry = step(n, buf_a, carry, False)
        issue(n + 2, buf_a)
        return step(n + 1, buf_b, carry, False)

    def odd_tail(carry):
        issue(n_last, buf_b)
        carry = step(n_last - 1, buf_a, carry, False)
        return step(n_last, buf_b, carry, True)

    def even_tail(carry):
        return step(n_last, buf_a, carry, True)

    if not primed:
        issue(0, buf_a)
    carry = lax.fori_loop(0, n_last // 2, pair, carry)
    return lax.cond(n_last % 2 == 1, odd_tail, even_tail, carry)


def _flash_heads(heads, n_full, diag_mask, z_bufs, acc_ref):
    half_k, half_q = TK // 2, TQ // 2

    def add_bias(s, bias):
        if isinstance(bias, (list, tuple)):
            return jnp.concatenate([s[:, t * LANES:(t + 1) * LANES] + bias[t] for t in range(len(bias))],
                                   axis=1)
        return s + bias

    def later_queries(bias):
        return bias[half_q // LANES:] if isinstance(bias, (list, tuple)) else bias[:, half_q:]

    ones = jnp.ones((SUM_ROWS, TK), MXU_DTYPE)

    def issue(j, buf):
        for h, (qh, k_at, _, _) in enumerate(heads):
            buf[h] = _dot_nt(k_at(_rows(j)), qh)

    def update(tiles, rows, maxes, lanes):
        stats = []
        for s, m in zip(tiles, maxes):
            m_new = jnp.maximum(m, jnp.max(s, axis=0, keepdims=True))
            stats.append((m_new, jnp.exp(m - m_new), jnp.exp(s - m_new).astype(MXU_DTYPE)))
        for h, ((_, alpha, p), (_, _, vt_at, _)) in enumerate(zip(stats, heads)):
            vt1 = jnp.concatenate([vt_at(rows), ones[:, :p.shape[0]]], axis=0)
            acc_ref[h, :, lanes] = alpha * acc_ref[h, :, lanes] + _dot(vt1, p)
        return tuple(m_new for m_new, _, _ in stats)

    def step(j, buf, maxes):
        rows = _rows(j)
        tiles = [add_bias(buf[h], bias_at(rows)) for h, (_, _, _, bias_at) in enumerate(heads)]
        return update(tiles, rows, maxes, slice(None))

    def diagonal_step(j, buf, maxes):
        start = pl.multiple_of(j * TK, TK)
        early, late = pl.ds(start, half_k), pl.ds(pl.multiple_of(start + half_k, half_k), half_k)

        def masked(s, mask):
            return s if diag_mask is None else jnp.where(mask, s, NEG_BIG)

        tiles = [masked(add_bias(buf[h, :half_k, :], bias_at(early)), None if diag_mask is None
                        else diag_mask[:half_k, :]) for h, (_, _, _, bias_at) in enumerate(heads)]
        maxes = update(tiles, early, maxes, slice(None))
        tiles = [masked(add_bias(buf[h, half_k:, half_q:], later_queries(bias_at(late))), None if diag_mask is None
                        else diag_mask[half_k:, half_q:]) for h, (_, _, _, bias_at) in enumerate(heads)]
        update(tiles, late, tuple(m[:, half_q:] for m in maxes), slice(half_q, TQ))
        return maxes

    acc_ref[...] = jnp.zeros(acc_ref.shape, F32)
    init = tuple(jnp.full((1, TQ), NEG_BIG, F32) for _ in heads)
    _lookahead_loop(n_full, z_bufs, issue,
                    lambda j, buf, maxes, last: (diagonal_step if last else step)(j, buf, maxes), init)
    return [acc_ref[h, :HEAD_DIM, :] / acc_ref[h, HEAD_DIM:HEAD_DIM + 1, :] for h in range(len(heads))]


def _store_heads_t(o_ref, outs_t):
    for t in range(len(outs_t) // 2):
        pair_t = jnp.concatenate([outs_t[2 * t], outs_t[2 * t + 1]], axis=0)
        o_ref[0, :, t * LANES:(t + 1) * LANES] = pair_t.T.astype(o_ref.dtype)


def _sb_kernel(q_ref, k_ref, vt_ref, o_ref, za_ref, zb_ref, acc_ref):
    i = pl.program_id(2)
    r = lax.broadcasted_iota(jnp.int32, (TK, TK), 0)
    c = lax.broadcasted_iota(jnp.int32, (TK, TK), 1)
    after = (c > r).astype(MXU_DTYPE)
    after2 = jnp.concatenate([after, after], axis=1)
    strict = _causal_tile_t(True)
    n_heads = HEADS_PER_STEP
    qhs = [_head_q(q_ref[0][:, (h // 2) * LANES:(h // 2 + 1) * LANES], h % 2) for h in range(n_heads)]

    def issue(j, buf):
        for h in range(n_heads):
            buf[h] = _dot_nt(k_ref[0, _rows(j), (h // 2) * LANES:(h // 2 + 1) * LANES], qhs[h])

    def step(j, buf, carry, mask):
        rows = _rows(j)
        parts = []
        for h in range(n_heads):
            log_beta, neg_l1m = _log_sigmoid_parts(buf[h])
            if mask is not None:
                neg_l1m = jnp.where(mask, neg_l1m, 0.0)
            hi, lo = _split2(neg_l1m)
            parts.append((log_beta, neg_l1m[0:1, :], jnp.concatenate([hi, lo], axis=0)))
        laters = [_dot(after2, hl) for (_, _, hl) in parts]
        ws = []
        for (log_beta, _, _), later in zip(parts, laters):
            w = jnp.exp(log_beta - later)
            if mask is not None:
                w = jnp.where(mask, w, 0.0)
            ws.append(w.astype(MXU_DTYPE))
        out = []
        for h, (w, (_, first, _), later, tail) in enumerate(zip(ws, parts, laters, carry)):
            vtj = vt_ref[0, h * HEAD_DIM:(h + 1) * HEAD_DIM, rows]
            acc_ref[h] = acc_ref[h] + jnp.exp(-tail) * _dot(vtj, w)
            out.append(tail + (first + later[0:1, :]))
        return tuple(out)

    acc_ref[...] = jnp.zeros(acc_ref.shape, F32)
    carry = tuple(jnp.zeros((1, TQ), F32) for _ in range(n_heads))
    issue(i, za_ref)

    def with_earlier_blocks(cr):
        issue(i - 1, zb_ref)
        cr = step(i, za_ref, cr, strict)
        return _lookahead_loop(i - 1, (zb_ref, za_ref), lambda n, buf: issue(i - 1 - n, buf),
                               lambda n, buf, c, last: step(i - 1 - n, buf, c, None), cr, primed=True)

    lax.cond(i > 0, with_earlier_blocks, lambda cr: step(i, za_ref, cr, strict), carry)
    _store_heads_t(o_ref, [acc_ref[h] for h in range(n_heads)])


def _sb(main3d, vt3d):
    b, s, _ = main3d.shape
    nq = s // TQ
    w = HEADS_PER_STEP * HEAD_DIM
    qb, kb, vb = COL_QA // w, COL_KA // w, ROW_VA // w
    return pl.pallas_call(
        _sb_kernel,
        grid=(b, SB_HEADS // HEADS_PER_STEP, nq),
        in_specs=[
            pl.BlockSpec((1, TQ, w), lambda bi, hg, i: (bi, i, qb + hg)),
            pl.BlockSpec((1, s, w), lambda bi, hg, i: (bi, 0, kb + hg)),
            pl.BlockSpec((1, w, s), lambda bi, hg, i: (bi, vb + hg, 0)),
        ],
        out_specs=pl.BlockSpec((1, TQ, w), lambda bi, hg, i: (bi, i, hg)),
        out_shape=jax.ShapeDtypeStruct((b, s, BRANCH_WIDTH), MXU_DTYPE),
        scratch_shapes=[pltpu.VMEM((HEADS_PER_STEP, TK, TQ), F32), pltpu.VMEM((HEADS_PER_STEP, TK, TQ), F32),
                        pltpu.VMEM((HEADS_PER_STEP, HEAD_DIM, TQ), F32)],
        compiler_params=_cparams(("parallel", "parallel", "arbitrary")),
        name="sb",
    )(main3d, main3d, vt3d)


def _fox_kernel(q_ref, k_ref, vt_ref, c_ref, o_ref, cb_ref, za_ref, zb_ref, acc_ref):
    hg = pl.program_id(1)
    i = pl.program_id(2)
    n_heads = HEADS_PER_STEP

    @pl.when(i == 0)
    def _():
        c = c_ref[0]
        lane = lax.broadcasted_iota(jnp.int32, c.shape, 1)
        for h in range(n_heads):
            col = jnp.sum(jnp.where(lane == hg * n_heads + h, c, 0.0), axis=1, keepdims=True)
            cb_ref[h] = jnp.broadcast_to(-col, c.shape)

    heads = []
    for h in range(n_heads):
        t = h // 2
        qh = _head_q(q_ref[0][:, t * LANES:(t + 1) * LANES], h % 2)
        k_at = lambda rows, t=t: k_ref[0, rows, t * LANES:(t + 1) * LANES]
        vt_at = lambda rows, h=h: vt_ref[0, h * HEAD_DIM:(h + 1) * HEAD_DIM, rows]
        bias_at = lambda rows, h=h: [cb_ref[h, rows, :]] * (TQ // LANES)
        heads.append((qh, k_at, vt_at, bias_at))
    _store_heads_t(o_ref, _flash_heads(heads, i, _causal_tile_t(False), (za_ref, zb_ref), acc_ref))


def _fox(main3d, vt3d, c3d):
    b, s, _ = main3d.shape
    nq = s // TQ
    w = HEADS_PER_STEP * HEAD_DIM
    qb, kb, vb = COL_QB // w, COL_KB // w, ROW_VB // w
    return pl.pallas_call(
        _fox_kernel,
        grid=(b, FOX_HEADS // HEADS_PER_STEP, nq),
        in_specs=[
            pl.BlockSpec((1, TQ, w), lambda bi, hg, i: (bi, i, qb + hg)),
            pl.BlockSpec((1, s, w), lambda bi, hg, i: (bi, 0, kb + hg)),
            pl.BlockSpec((1, w, s), lambda bi, hg, i: (bi, vb + hg, 0)),
            pl.BlockSpec((1, s, LANES), lambda bi, hg, i: (bi, 0, 0)),
        ],
        out_specs=pl.BlockSpec((1, TQ, w), lambda bi, hg, i: (bi, i, hg)),
        out_shape=jax.ShapeDtypeStruct((b, s, BRANCH_WIDTH), MXU_DTYPE),
        scratch_shapes=[pltpu.VMEM((HEADS_PER_STEP, s, LANES), F32),
                        pltpu.VMEM((HEADS_PER_STEP, TK, TQ), F32), pltpu.VMEM((HEADS_PER_STEP, TK, TQ), F32),
                        pltpu.VMEM((HEADS_PER_STEP, HEAD_DIM + SUM_ROWS, TQ), F32)],
        compiler_params=_cparams(("parallel", "parallel", "arbitrary")),
        name="fox",
    )(main3d, main3d, vt3d, c3d)


def _dsa_kernel(q_ref, k_ref, vt_ref, qi_ref, ki_ref, misc_ref, o_ref, key_ref, dig_ref, bias_ref,
                za_ref, zb_ref, acc_ref, *, top_k):
    i = pl.program_id(1)
    diag = _causal_tile_t(False)

    wi_t = misc_ref[0].T[MISC_WI:MISC_WI + IDX_HEADS, :] * IDX_W_SCALE
    qih = [_head_q(qi_ref[0][:, (h // 2) * LANES:(h // 2 + 1) * LANES], h % 2) for h in range(IDX_HEADS)]

    def score_block(j, mask):
        kij = ki_ref[0, _rows(j), :]
        sc = None
        for h in range(IDX_HEADS):
            term = wi_t[h:h + 1, :] * jnp.maximum(_dot_nt(kij, qih[h]), 0.0)
            sc = term if sc is None else sc + term
        if mask is not None:
            sc = jnp.where(mask, sc, -jnp.inf)
        bits = lax.bitcast_convert_type(jnp.where(sc == 0.0, 0.0, sc), jnp.int32)
        key_ref[_rows(j), :] = bits ^ ((bits >> 31) & jnp.int32(0x7FFFFFFF))

    def score_pair(t, _):
        score_block(2 * t, None)
        score_block(2 * t + 1, None)
        return 0

    def score_odd_tail():
        score_block(i - 1, None)
        score_block(i, diag)
        return 0

    def score_even_tail():
        score_block(i, diag)
        return 0

    lax.fori_loop(0, i // 2, score_pair, 0)
    lax.cond(i % 2 == 1, score_odd_tail, score_even_tail)

    small = MXU_DTYPE
    group = 2 * SUBLANES

    def count_digits_ge(cand):
        def body(j, acc):
            hit = jnp.where(dig_ref[_rows(j), :] >= cand, jnp.ones((), small), jnp.zeros((), small))
            part = hit[0:group]
            for g in range(1, TK // group):
                part = part + hit[g * group:(g + 1) * group]
            return acc + part.astype(F32)
        acc = lax.fori_loop(0, i + 1, body, jnp.zeros((group, TQ), F32))
        return jnp.sum(acc, axis=0, keepdims=True)

    above = jnp.zeros((1, TQ), F32)
    prefix = None
    for shift in (24, 16, 8, 0):
        def fill(j, _, shift=shift, prefix=prefix):
            key = key_ref[_rows(j), :]
            if prefix is None:
                digit = (key >> shift) + 128
            else:
                digit = (key ^ (prefix << (shift + 8))) >> shift
                shares = lax.bitcast_convert_type(digit, jnp.uint32) < jnp.uint32(256)
                digit = jnp.where(shares, digit, -1)
            dig_ref[_rows(j), :] = digit.astype(F32).astype(small)
            return 0

        lax.fori_loop(0, i + 1, fill, 0)

        def search(_, st, above=above):
            lo, hi, n_hi = st
            mid = (lo + hi) * 0.5
            n_mid = count_digits_ge(mid.astype(small))
            ok = above + n_mid >= top_k
            return jnp.where(ok, mid, lo), jnp.where(ok, hi, mid), jnp.where(ok, n_hi, n_mid)

        start = (jnp.zeros((1, TQ), F32), jnp.full((1, TQ), 256.0, F32), jnp.zeros((1, TQ), F32))
        digit, _, n_hi = lax.fori_loop(0, 8, search, start)
        above = above + n_hi
        digit = digit.astype(jnp.int32)
        prefix = digit - 128 if prefix is None else (prefix << 8) | digit
    thr_key = prefix

    need = top_k - above
    kr = lax.broadcasted_iota(jnp.int32, (TK, TK), 0)
    kc = lax.broadcasted_iota(jnp.int32, (TK, TK), 1)
    upto = (kc <= kr).astype(MXU_DTYPE)

    def bias_block(j, mask, seen):
        key = key_ref[_rows(j), :]
        tie = key == thr_key
        rank = _dot(upto, jnp.where(tie, 1.0, 0.0).astype(MXU_DTYPE)) + seen
        sel = (key > thr_key) | (tie & (rank <= need))
        if mask is not None:
            sel = sel & mask
        bias_ref[_rows(j), :] = jnp.where(sel, 0.0, NEG_BIG)
        return rank[TK - 1:TK, :]

    def bias_pair(t, seen):
        return bias_block(2 * t + 1, None, bias_block(2 * t, None, seen))

    def bias_odd_tail(seen):
        bias_block(i, diag, bias_block(i - 1, None, seen))
        return 0

    def bias_even_tail(seen):
        bias_block(i, diag, seen)
        return 0

    seen = lax.fori_loop(0, i // 2, bias_pair, jnp.zeros((1, TQ), F32))
    lax.cond(i % 2 == 1, bias_odd_tail, bias_even_tail, seen)

    bias_at = lambda rows: bias_ref[rows, :]
    heads = []
    for h in range(DSA_HEADS):
        g = h // DSA_GROUP
        k_at = lambda rows, g=g: k_ref[0, rows, g * LANES:(g + 1) * LANES]
        vt_at = lambda rows, g=g: vt_ref[0, g * HEAD_DIM:(g + 1) * HEAD_DIM, rows]
        qh = _head_q(q_ref[0][:, (h // 2) * LANES:(h // 2 + 1) * LANES], h % 2)
        heads.append((qh, k_at, vt_at, bias_at))
    _store_heads_t(o_ref, _flash_heads(heads, i, None, (za_ref, zb_ref), acc_ref))


def _dsa(main3d, vt3d, misc3d, top_k):
    b, s, _ = main3d.shape
    nq = s // TQ
    assert top_k <= TK, "the first key block must hold at least top_k keys"
    return pl.pallas_call(
        functools.partial(_dsa_kernel, top_k=top_k),
        grid=(b, nq),
        in_specs=[
            pl.BlockSpec((1, TQ, 4 * LANES), lambda bi, i: (bi, i, COL_QC // (4 * LANES))),
            pl.BlockSpec((1, s, 2 * LANES), lambda bi, i: (bi, 0, COL_KC // (2 * LANES))),
            pl.BlockSpec((1, LANES, s), lambda bi, i: (bi, ROW_VC // LANES, 0)),
            pl.BlockSpec((1, TQ, 2 * LANES), lambda bi, i: (bi, i, COL_QI // (2 * LANES))),
            pl.BlockSpec((1, s, LANES), lambda bi, i: (bi, 0, COL_KI // LANES)),
            pl.BlockSpec((1, TQ, LANES), lambda bi, i: (bi, i, 0)),
        ],
        out_specs=pl.BlockSpec((1, TQ, BRANCH_WIDTH), lambda bi, i: (bi, i, 0)),
        out_shape=jax.ShapeDtypeStruct((b, s, BRANCH_WIDTH), MXU_DTYPE),
        scratch_shapes=[pltpu.VMEM((s, TQ), jnp.int32), pltpu.VMEM((s, TQ), MXU_DTYPE), pltpu.VMEM((s, TQ), F32),
                        pltpu.VMEM((DSA_HEADS, TK, TQ), F32), pltpu.VMEM((DSA_HEADS, TK, TQ), F32),
                        pltpu.VMEM((DSA_HEADS, HEAD_DIM + SUM_ROWS, TQ), F32)],
        compiler_params=_cparams(("parallel", "arbitrary")),
        name="dsa",
    )(main3d, main3d, vt3d, main3d, main3d, misc3d)


def _merge_kernel(x_ref, a_ref, b_ref, c_ref, g_ref, wg_ref, wb_ref, wo_ref, o_ref):
    x = x_ref[...]
    d = x.shape[1]
    h = _rmsnorm(x, g_ref[...]).astype(MXU_DTYPE)
    y = None
    for n, br_ref in enumerate((a_ref, b_ref, c_ref)):
        gate = jax.nn.sigmoid(_dot(h, wg_ref[:, n * d:(n + 1) * d]))
        term = gate * _dot(br_ref[...], wb_ref[n])
        y = term if y is None else y + term
    o_ref[...] = x + _dot(y.astype(MXU_DTYPE), wo_ref[...])


def _merge(x2d, br_a, br_b, br_c, g, w_gate, w_branch, w_out, tm):
    n, d = x2d.shape
    bw = br_a.shape[1]
    row = lambda i: (i, 0)
    const = lambda i: (0, 0)
    return pl.pallas_call(
        _merge_kernel,
        grid=(n // tm,),
        in_specs=[
            pl.BlockSpec((tm, d), row),
            pl.BlockSpec((tm, bw), row),
            pl.BlockSpec((tm, bw), row),
            pl.BlockSpec((tm, bw), row),
            pl.BlockSpec((1, d), const),
            pl.BlockSpec((d, N_BRANCH * d), const),
            pl.BlockSpec((N_BRANCH, bw, d), lambda i: (0, 0, 0)),
            pl.BlockSpec((d, d), const),
        ],
        out_specs=pl.BlockSpec((tm, d), row),
        out_shape=jax.ShapeDtypeStruct((n, d), F32),
        compiler_params=_cparams(("parallel",)),
        name="merge",
    )(x2d, br_a, br_b, br_c, g, w_gate, w_branch, w_out)


def _mlp_kernel(x_ref, g_ref, wu_ref, wd_ref, gf_ref, o_ref, *, ff_chunk, final_norm):
    x = x_ref[...]
    h = _rmsnorm(x, g_ref[...]).astype(MXU_DTYPE)
    acc = x
    for c in range(wu_ref.shape[1] // ff_chunk):
        u = jnp.maximum(_dot(h, wu_ref[:, c * ff_chunk:(c + 1) * ff_chunk]), 0.0)
        acc = acc + _dot((u * u).astype(MXU_DTYPE), wd_ref[c * ff_chunk:(c + 1) * ff_chunk, :])
    if final_norm:
        acc = _rmsnorm(acc, gf_ref[...])
    o_ref[...] = acc


def _mlp(x2d, g, w_up, w_down, g_final, tm, final_norm):
    n, d = x2d.shape
    ff = w_up.shape[1]
    row = lambda i: (i, 0)
    const = lambda i: (0, 0)
    return pl.pallas_call(
        functools.partial(_mlp_kernel, ff_chunk=min(ff, 1024), final_norm=final_norm),
        grid=(n // tm,),
        in_specs=[
            pl.BlockSpec((tm, d), row),
            pl.BlockSpec((1, d), const),
            pl.BlockSpec((d, ff), const),
            pl.BlockSpec((ff, d), const),
            pl.BlockSpec((1, d), const),
        ],
        out_specs=pl.BlockSpec((tm, d), row),
        out_shape=jax.ShapeDtypeStruct((n, d), F32),
        compiler_params=_cparams(("parallel",)),
        name="mlp",
    )(x2d, g, w_up, w_down, g_final)


def _w_in_columns():
    sizes = (("qa", BRANCH_WIDTH), ("ka", BRANCH_WIDTH), ("va", BRANCH_WIDTH),
             ("qb", BRANCH_WIDTH), ("kb", BRANCH_WIDTH), ("vb", BRANCH_WIDTH), ("fl", FOX_HEADS),
             ("qc", DSA_HEADS * HEAD_DIM), ("kc", DSA_KV_HEADS * HEAD_DIM), ("vc", DSA_KV_HEADS * HEAD_DIM),
             ("qi", IDX_HEADS * IDX_DIM), ("ki", IDX_DIM), ("wi", IDX_HEADS), ("gates", N_BRANCH * 1024))
    cols, o = {}, 0
    for name, width in sizes:
        cols[name] = (o, o + width)
        o += width
    return cols, o


def _pack_kernel(w_ref, main_ref, vals_ref, gates_ref):
    cols, _ = _w_in_columns()

    def src(name, lo=0, hi=None):
        a, b = cols[name]
        return w_ref[0, :, a + lo:(b if hi is None else a + hi)]

    def put(ref, start, value):
        ref[0, :, start:start + value.shape[1]] = value.astype(ref.dtype)

    put(main_ref, COL_QA, src("qa"))
    put(main_ref, COL_KA, src("ka"))
    put(main_ref, COL_QB, src("qb"))
    put(main_ref, COL_KB, src("kb"))
    put(main_ref, COL_QC, src("qc"))
    for g in range(DSA_KV_HEADS):
        kg = src("kc", g * HEAD_DIM, (g + 1) * HEAD_DIM)
        put(main_ref, COL_KC + g * LANES, jnp.concatenate([kg, kg], axis=1))
    put(main_ref, COL_QI, src("qi"))
    ki = src("ki")
    put(main_ref, COL_KI, jnp.concatenate([ki, ki], axis=1))
    put(vals_ref, ROW_VA, src("va"))
    put(vals_ref, ROW_VB, src("vb"))
    put(vals_ref, ROW_VC, src("vc"))
    pad = jnp.zeros((w_ref.shape[1], LANES - FOX_HEADS - IDX_HEADS), F32)
    put(vals_ref, VT_ROWS, jnp.concatenate([src("fl"), src("wi"), pad], axis=1))
    put(gates_ref, 0, src("gates"))


def _pack_w_in(w_in):
    depth, d, total = w_in.shape
    cols, end = _w_in_columns()
    assert end == total and cols["gates"][1] - cols["gates"][0] == N_BRANCH * d
    rows = 256
    blk = lambda width: pl.BlockSpec((1, rows, width), lambda l, r: (l, r, 0))
    widths = (MAIN_COLS, VALS_COLS, N_BRANCH * d)
    return pl.pallas_call(
        _pack_kernel,
        grid=(depth, d // rows),
        in_specs=[blk(total)],
        out_specs=[blk(wd) for wd in widths],
        out_shape=[jax.ShapeDtypeStruct((depth, d, wd), MXU_DTYPE) for wd in widths],
        compiler_params=_cparams(("parallel", "parallel")),
        name="pack",
    )(w_in)


def _rope_rows():
    lane = jnp.arange(LANES) % HEAD_DIM
    half = ROT_DIM // 2
    inv_freq = ROPE_THETA ** (-jnp.arange(0, ROT_DIM, 2, dtype=F32) / ROT_DIM)
    freq = jnp.where(lane < ROT_DIM, inv_freq[lane % half], 0.0).astype(F32)
    sg1 = jnp.where(lane < half, -1.0, 0.0).astype(F32)
    sg2 = jnp.where((lane >= half) & (lane < ROT_DIM), 1.0, 0.0).astype(F32)
    return freq[None, :], sg1[None, :], sg2[None, :]


def kernel(x, positions, g_mix, w_in, b_forget, w_branch, w_out, g_mlp, w_up, w_down, g_final):
    b, s, d = x.shape
    n = b * s
    depth = w_in.shape[0]
    top_k = min(TOPK_MAX, s // 4)
    tm = min(512, s)
    freq, sg1, sg2 = _rope_rows()
    pos2d = positions.reshape(n, 1)
    x2d = x.reshape(n, d)
    w_main_all, w_v_all, w_gate_all = _pack_w_in(w_in)
    for layer in range(depth):
        w_main, w_v, w_gate = w_main_all[layer], w_v_all[layer], w_gate_all[layer]
        bias_row = jnp.concatenate([b_forget[layer], jnp.zeros((LANES - FOX_HEADS,), F32)])[None, :]
        main, vt, misc = _proj(x2d, pos2d, g_mix[layer][None, :], freq, sg1, sg2, w_main, w_v, tm, b, s)
        main3d = main.reshape(b, s, MAIN_COLS)
        misc3d = misc.reshape(b, s, LANES)
        cum = _cumf(misc3d, bias_row)
        br_a = _sb(main3d, vt).reshape(n, BRANCH_WIDTH)
        br_b = _fox(main3d, vt, cum).reshape(n, BRANCH_WIDTH)
        br_c = _dsa(main3d, vt, misc3d, top_k).reshape(n, BRANCH_WIDTH)
        x2d = _merge(x2d, br_a, br_b, br_c, g_mix[layer][None, :], w_gate,
                     w_branch[layer].astype(MXU_DTYPE), w_out[layer].astype(MXU_DTYPE), tm)
        x2d = _mlp(x2d, g_mlp[layer][None, :], w_up[layer].astype(MXU_DTYPE),
                   w_down[layer].astype(MXU_DTYPE), g_final[None, :], tm, layer == depth - 1)
    return x2d.reshape(b, s, d)
```

```python
import functools

import jax
import jax.numpy as jnp
from jax import lax
from jax.experimental import pallas as pl
from jax.experimental.pallas import tpu as pltpu

F32 = jnp.float32
MXU_DTYPE = jnp.bfloat16

HEAD_DIM = 64
SB_HEADS = 8
FOX_HEADS = 8
DSA_HEADS = 8
DSA_KV_HEADS = 2
DSA_GROUP = DSA_HEADS // DSA_KV_HEADS
IDX_HEADS = 4
IDX_DIM = 64
N_BRANCH = 3
ROPE_THETA = 500000.0
ROT_DIM = HEAD_DIM // 4
TOPK_MAX = 256
NORM_EPS = 1e-6

LANES = 128
SUBLANES = 8
SUM_ROWS = 2 * SUBLANES
BRANCH_WIDTH = SB_HEADS * HEAD_DIM
ATTN_SCALE = HEAD_DIM ** -0.5
assert IDX_DIM == HEAD_DIM
IDX_W_SCALE = IDX_HEADS ** -0.5

COL_QA, COL_KA = 0, 512
COL_QB, COL_KB = 1024, 1536
COL_QC = 2048
COL_KC = 2560
COL_QI = 2816
COL_KI = 3072
MAIN_COLS = 3200
ROPE_START = COL_QC
ROW_VA, ROW_VB, ROW_VC = 0, 512, 1024
VT_ROWS = 1152
VALS_COLS = VT_ROWS + LANES
MISC_FL, MISC_WI = 0, 8

TQ = 256
TK = 256
HEADS_PER_STEP = 8
NEG_BIG = -1e30

VMEM_LIMIT = 56 * 1024 * 1024


def _cparams(sem):
    return pltpu.CompilerParams(dimension_semantics=sem, vmem_limit_bytes=VMEM_LIMIT)


def _rmsnorm(x, g):
    y = x * lax.rsqrt(jnp.mean(x * x, axis=-1, keepdims=True) + NORM_EPS)
    return y * g


def _dot(a, b):
    return jnp.dot(a, b, preferred_element_type=F32)


def _dot_nt(a, b):
    return lax.dot_general(a, b, (((1,), (1,)), ((), ())), preferred_element_type=F32)


def _split2(x):
    hi = x.astype(MXU_DTYPE)
    lo = (x - hi.astype(F32)).astype(MXU_DTYPE)
    return hi, lo


def _split3(x):
    hi = x.astype(MXU_DTYPE)
    r = x - hi.astype(F32)
    mid = r.astype(MXU_DTYPE)
    lo = (r - mid.astype(F32)).astype(MXU_DTYPE)
    return hi, mid, lo


def _log_sigmoid_parts(z):
    neg_abs = lax.bitcast_convert_type(lax.bitcast_convert_type(z, jnp.int32) | jnp.int32(-2 ** 31), F32)
    sp = jnp.log(1.0 + jnp.exp(neg_abs))
    return jnp.minimum(z, 0.0) - sp, jnp.maximum(z, 0.0) + sp


def _proj_kernel(x_ref, pos_ref, g_ref, freq_ref, sg1_ref, sg2_ref, wm_ref, wv_ref,
                 main_ref, vt_ref, misc_ref):
    h = _rmsnorm(x_ref[...], g_ref[...]).astype(MXU_DTYPE)
    ang = pos_ref[...].astype(F32) * freq_ref[...]
    cos = jnp.cos(ang)
    sin = jnp.sin(ang)
    s_up = sin * sg1_ref[...]
    s_dn = sin * sg2_ref[...]
    half = ROT_DIM // 2
    chunk = 4 * LANES
    for c0 in range(0, MAIN_COLS, chunk):
        c1 = min(c0 + chunk, MAIN_COLS)
        p = _dot(h, wm_ref[:, c0:c1])
        if c0 >= ROPE_START:
            for t in range((c1 - c0) // LANES):
                pt = p[:, t * LANES:(t + 1) * LANES]
                pt = (pt * cos + pltpu.roll(pt, LANES - half, 1) * s_up
                      + pltpu.roll(pt, half, 1) * s_dn)
                main_ref[:, c0 + t * LANES:c0 + (t + 1) * LANES] = pt.astype(main_ref.dtype)
        else:
            main_ref[:, c0:c1] = p.astype(main_ref.dtype)
    for c0 in range(0, VALS_COLS, 2 * LANES):
        pv = _dot(h, wv_ref[:, c0:c0 + 2 * LANES])
        for r0 in (c0, c0 + LANES):
            piece = pv[:, r0 - c0:r0 - c0 + LANES]
            if r0 < VT_ROWS:
                vt_ref[0, r0:r0 + LANES, :] = piece.T.astype(vt_ref.dtype)
            else:
                misc_ref[...] = piece


def _proj(x2d, pos2d, g, freq, sg1, sg2, w_main, w_v, tm, batch, seq):
    n, d = x2d.shape
    nsb = seq // tm
    const = lambda i: (0, 0)
    return pl.pallas_call(
        _proj_kernel,
        grid=(n // tm,),
        in_specs=[
            pl.BlockSpec((tm, d), lambda i: (i, 0)),
            pl.BlockSpec((tm, 1), lambda i: (i, 0)),
            pl.BlockSpec((1, d), const),
            pl.BlockSpec((1, LANES), const),
            pl.BlockSpec((1, LANES), const),
            pl.BlockSpec((1, LANES), const),
            pl.BlockSpec((d, MAIN_COLS), const),
            pl.BlockSpec((d, VALS_COLS), const),
        ],
        out_specs=[
            pl.BlockSpec((tm, MAIN_COLS), lambda i: (i, 0)),
            pl.BlockSpec((1, VT_ROWS, tm), lambda i: (i // nsb, 0, i % nsb)),
            pl.BlockSpec((tm, LANES), lambda i: (i, 0)),
        ],
        out_shape=[
            jax.ShapeDtypeStruct((n, MAIN_COLS), MXU_DTYPE),
            jax.ShapeDtypeStruct((batch, VT_ROWS, seq), MXU_DTYPE),
            jax.ShapeDtypeStruct((n, LANES), F32),
        ],
        compiler_params=_cparams(("parallel",)),
        name="proj",
    )(x2d, pos2d, g, freq, sg1, sg2, w_main, w_v)


def _cumf_kernel(misc_ref, bias_ref, c_ref):
    s = misc_ref.shape[1]
    r = lax.broadcasted_iota(jnp.int32, (TK, TK), 0)
    c = lax.broadcasted_iota(jnp.int32, (TK, TK), 1)
    incl = (c <= r).astype(MXU_DTYPE)
    carry = jnp.zeros((1, LANES), F32)
    for b in range(s // TK):
        logit = misc_ref[0, b * TK:(b + 1) * TK, :] + bias_ref[...]
        logf, _ = _log_sigmoid_parts(logit)
        hi, mid, lo = _split3(logf)
        cs = (_dot(incl, lo) + _dot(incl, mid)) + _dot(incl, hi) + carry
        c_ref[0, b * TK:(b + 1) * TK, :] = cs
        carry = cs[TK - 1:TK, :]


def _cumf(misc3d, bias_row):
    b, s, _ = misc3d.shape
    return pl.pallas_call(
        _cumf_kernel,
        grid=(b,),
        in_specs=[
            pl.BlockSpec((1, s, LANES), lambda i: (i, 0, 0)),
            pl.BlockSpec((1, LANES), lambda i: (0, 0)),
        ],
        out_specs=pl.BlockSpec((1, s, LANES), lambda i: (i, 0, 0)),
        out_shape=jax.ShapeDtypeStruct((b, s, LANES), F32),
        compiler_params=_cparams(("parallel",)),
        name="cumf",
    )(misc3d, bias_row)


def _head_q(q_tile, half):
    lane = lax.broadcasted_iota(jnp.int32, q_tile.shape, 1)
    keep = (lane // HEAD_DIM) == half
    return jnp.where(keep, q_tile.astype(F32) * ATTN_SCALE, 0.0).astype(MXU_DTYPE)


def _causal_tile_t(strict):
    key = lax.broadcasted_iota(jnp.int32, (TK, TQ), 0)
    qry = lax.broadcasted_iota(jnp.int32, (TK, TQ), 1)
    return (key < qry) if strict else (key <= qry)


def _rows(j):
    return pl.ds(pl.multiple_of(j * TK, TK), TK)


def _lookahead_loop(n_last, bufs, issue, step, carry, primed=False):
    buf_a, buf_b = bufs

    def pair(t, carry):
        n = 2 * t
        issue(n + 1, buf_b)
        carry = step(n, buf_a, carry, False)
        issue(n + 2, buf_a)
        return step(n + 1, buf_b, carry, False)

    def odd_tail(carry):
        issue(n_last, buf_b)
        carry = step(n_last - 1, buf_a, carry, False)
        return step(n_last, buf_b, carry, True)

    def even_tail(carry):
        return step(n_last, buf_a, carry, True)

    if not primed:
        issue(0, buf_a)
    carry = lax.fori_loop(0, n_last // 2, pair, carry)
    return lax.cond(n_last % 2 == 1, odd_tail, even_tail, carry)


def _flash_heads(heads, n_full, diag_mask, z_bufs, acc_ref):
    half_k, half_q = TK // 2, TQ // 2

    def add_bias(s, bias):
        if isinstance(bias, (list, tuple)):
            return jnp.concatenate([s[:, t * LANES:(t + 1) * LANES] + bias[t] for t in range(len(bias))],
                                   axis=1)
        return s + bias

    def later_queries(bias):
        return bias[half_q // LANES:] if isinstance(bias, (list, tuple)) else bias[:, half_q:]

    ones = jnp.ones((SUM_ROWS, TK), MXU_DTYPE)

    def issue(j, buf):
        for h, (qh, k_at, _, _) in enumerate(heads):
            buf[h] = _dot_nt(k_at(_rows(j)), qh)

    def update(tiles, rows, maxes, lanes):
        stats = []
        for s, m in zip(tiles, maxes):
            m_new = jnp.maximum(m, jnp.max(s, axis=0, keepdims=True))
            stats.append((m_new, jnp.exp(m - m_new), jnp.exp(s - m_new).astype(MXU_DTYPE)))
        for h, ((_, alpha, p), (_, _, vt_at, _)) in enumerate(zip(stats, heads)):
            vt1 = jnp.concatenate([vt_at(rows), ones[:, :p.shape[0]]], axis=0)
            acc_ref[h, :, lanes] = alpha * acc_ref[h, :, lanes] + _dot(vt1, p)
        return tuple(m_new for m_new, _, _ in stats)

    def step(j, buf, maxes):
        rows = _rows(j)
        tiles = [add_bias(buf[h], bias_at(rows)) for h, (_, _, _, bias_at) in enumerate(heads)]
        return update(tiles, rows, maxes, slice(None))

    def diagonal_step(j, buf, maxes):
        start = pl.multiple_of(j * TK, TK)
        early, late = pl.ds(start, half_k), pl.ds(pl.multiple_of(start + half_k, half_k), half_k)

        def masked(s, mask):
            return s if diag_mask is None else jnp.where(mask, s, NEG_BIG)

        tiles = [masked(add_bias(buf[h, :half_k, :], bias_at(early)), None if diag_mask is None
                        else diag_mask[:half_k, :]) for h, (_, _, _, bias_at) in enumerate(heads)]
        maxes = update(tiles, early, maxes, slice(None))
        tiles = [masked(add_bias(buf[h, half_k:, half_q:], later_queries(bias_at(late))), None if diag_mask is None
                        else diag_mask[half_k:, half_q:]) for h, (_, _, _, bias_at) in enumerate(heads)]
        update(tiles, late, tuple(m[:, half_q:] for m in maxes), slice(half_q, TQ))
        return maxes

    acc_ref[...] = jnp.zeros(acc_ref.shape, F32)
    init = tuple(jnp.full((1, TQ), NEG_BIG, F32) for _ in heads)
    _lookahead_loop(n_full, z_bufs, issue,
                    lambda j, buf, maxes, last: (diagonal_step if last else step)(j, buf, maxes), init)
    return [acc_ref[h, :HEAD_DIM, :] / acc_ref[h, HEAD_DIM:HEAD_DIM + 1, :] for h in range(len(heads))]


def _store_heads_t(o_ref, outs_t):
    for t in range(len(outs_t) // 2):
        pair_t = jnp.concatenate([outs_t[2 * t], outs_t[2 * t + 1]], axis=0)
        o_ref[0, :, t * LANES:(t + 1) * LANES] = pair_t.T.astype(o_ref.dtype)


def _sb_kernel(q_ref, k_ref, vt_ref, o_ref, za_ref, zb_ref, acc_ref):
    i = pl.program_id(2)
    r = lax.broadcasted_iota(jnp.int32, (TK, TK), 0)
    c = lax.broadcasted_iota(jnp.int32, (TK, TK), 1)
    after = (c > r).astype(MXU_DTYPE)
    strict = _causal_tile_t(True)
    n_heads = HEADS_PER_STEP
    qhs = [_head_q(q_ref[0][:, (h // 2) * LANES:(h // 2 + 1) * LANES], h % 2) for h in range(n_heads)]

    def issue(j, buf):
        for h in range(n_heads):
            buf[h] = _dot_nt(k_ref[0, _rows(j), (h // 2) * LANES:(h // 2 + 1) * LANES], qhs[h])

    def update(tiles, rows, tails, lanes, mask):
        n_rows = tiles[0].shape[0]
        parts = []
        for z in tiles:
            log_beta, neg_l1m = _log_sigmoid_parts(z)
            if mask is not None:
                neg_l1m = jnp.where(mask, neg_l1m, 0.0)
            hi, lo = _split2(neg_l1m)
            parts.append((log_beta, neg_l1m[0:1, :], jnp.concatenate([hi, lo], axis=0)))
        after_rows = jnp.concatenate([after[:n_rows, :n_rows]] * 2, axis=1)
        laters = [_dot(after_rows, hl) for (_, _, hl) in parts]
        ws = []
        for (log_beta, _, _), later in zip(parts, laters):
            w = jnp.exp(log_beta - later)
            if mask is not None:
                w = jnp.where(mask, w, 0.0)
            ws.append(w.astype(MXU_DTYPE))
        out = []
        for h, (w, (_, first, _), later, tail) in enumerate(zip(ws, parts, laters, tails)):
            vtj = vt_ref[0, h * HEAD_DIM:(h + 1) * HEAD_DIM, rows]
            acc_ref[h, :, lanes] = acc_ref[h, :, lanes] + jnp.exp(-tail) * _dot(vtj, w)
            out.append(tail + (first + later[0:1, :]))
        return tuple(out)

    def step(j, buf, tails):
        return update([buf[h] for h in range(n_heads)], _rows(j), tails, slice(None), None)

    def diagonal_step(j, buf, tails):
        half_k, half_q = TK // 2, TQ // 2
        start = pl.multiple_of(j * TK, TK)
        early, late = pl.ds(start, half_k), pl.ds(pl.multiple_of(start + half_k, half_k), half_k)
        late_tails = update([buf[h, half_k:, half_q:] for h in range(n_heads)], late,
                            tuple(t[:, half_q:] for t in tails), slice(half_q, TQ), strict[half_k:, half_q:])
        tails = tuple(jnp.concatenate([t[:, :half_q], lt], axis=1) for t, lt in zip(tails, late_tails))
        return update([buf[h, :half_k, :] for h in range(n_heads)], early, tails, slice(None), strict[:half_k, :])

    acc_ref[...] = jnp.zeros(acc_ref.shape, F32)
    carry = tuple(jnp.zeros((1, TQ), F32) for _ in range(n_heads))
    issue(i, za_ref)

    def with_earlier_blocks(cr):
        issue(i - 1, zb_ref)
        cr = diagonal_step(i, za_ref, cr)
        return _lookahead_loop(i - 1, (zb_ref, za_ref), lambda n, buf: issue(i - 1 - n, buf),
                               lambda n, buf, c, last: step(i - 1 - n, buf, c), cr, primed=True)

    lax.cond(i > 0, with_earlier_blocks, lambda cr: diagonal_step(i, za_ref, cr), carry)
    _store_heads_t(o_ref, [acc_ref[h] for h in range(n_heads)])


def _sb(main3d, vt3d):
    b, s, _ = main3d.shape
    nq = s // TQ
    w = HEADS_PER_STEP * HEAD_DIM
    qb, kb, vb = COL_QA // w, COL_KA // w, ROW_VA // w
    return pl.pallas_call(
        _sb_kernel,
        grid=(b, SB_HEADS // HEADS_PER_STEP, nq),
        in_specs=[
            pl.BlockSpec((1, TQ, w), lambda bi, hg, i: (bi, i, qb + hg)),
            pl.BlockSpec((1, s, w), lambda bi, hg, i: (bi, 0, kb + hg)),
            pl.BlockSpec((1, w, s), lambda bi, hg, i: (bi, vb + hg, 0)),
        ],
        out_specs=pl.BlockSpec((1, TQ, w), lambda bi, hg, i: (bi, i, hg)),
        out_shape=jax.ShapeDtypeStruct((b, s, BRANCH_WIDTH), MXU_DTYPE),
        scratch_shapes=[pltpu.VMEM((HEADS_PER_STEP, TK, TQ), F32), pltpu.VMEM((HEADS_PER_STEP, TK, TQ), F32),
                        pltpu.VMEM((HEADS_PER_STEP, HEAD_DIM, TQ), F32)],
        compiler_params=_cparams(("parallel", "parallel", "arbitrary")),
        name="sb",
    )(main3d, main3d, vt3d)


def _fox_kernel(q_ref, k_ref, vt_ref, c_ref, o_ref, cb_ref, za_ref, zb_ref, acc_ref):
    hg = pl.program_id(1)
    i = pl.program_id(2)
    n_heads = HEADS_PER_STEP

    @pl.when(i == 0)
    def _():
        c = c_ref[0]
        lane = lax.broadcasted_iota(jnp.int32, c.shape, 1)
        for h in range(n_heads):
            col = jnp.sum(jnp.where(lane == hg * n_heads + h, c, 0.0), axis=1, keepdims=True)
            cb_ref[h] = jnp.broadcast_to(-col, c.shape)

    heads = []
    for h in range(n_heads):
        t = h // 2
        qh = _head_q(q_ref[0][:, t * LANES:(t + 1) * LANES], h % 2)
        k_at = lambda rows, t=t: k_ref[0, rows, t * LANES:(t + 1) * LANES]
        vt_at = lambda rows, h=h: vt_ref[0, h * HEAD_DIM:(h + 1) * HEAD_DIM, rows]
        bias_at = lambda rows, h=h: [cb_ref[h, rows, :]] * (TQ // LANES)
        heads.append((qh, k_at, vt_at, bias_at))
    _store_heads_t(o_ref, _flash_heads(heads, i, _causal_tile_t(False), (za_ref, zb_ref), acc_ref))


def _fox(main3d, vt3d, c3d):
    b, s, _ = main3d.shape
    nq = s // TQ
    w = HEADS_PER_STEP * HEAD_DIM
    qb, kb, vb = COL_QB // w, COL_KB // w, ROW_VB // w
    return pl.pallas_call(
        _fox_kernel,
        grid=(b, FOX_HEADS // HEADS_PER_STEP, nq),
        in_specs=[
            pl.BlockSpec((1, TQ, w), lambda bi, hg, i: (bi, i, qb + hg)),
            pl.BlockSpec((1, s, w), lambda bi, hg, i: (bi, 0, kb + hg)),
            pl.BlockSpec((1, w, s), lambda bi, hg, i: (bi, vb + hg, 0)),
            pl.BlockSpec((1, s, LANES), lambda bi, hg, i: (bi, 0, 0)),
        ],
        out_specs=pl.BlockSpec((1, TQ, w), lambda bi, hg, i: (bi, i, hg)),
        out_shape=jax.ShapeDtypeStruct((b, s, BRANCH_WIDTH), MXU_DTYPE),
        scratch_shapes=[pltpu.VMEM((HEADS_PER_STEP, s, LANES), F32),
                        pltpu.VMEM((HEADS_PER_STEP, TK, TQ), F32), pltpu.VMEM((HEADS_PER_STEP, TK, TQ), F32),
                        pltpu.VMEM((HEADS_PER_STEP, HEAD_DIM + SUM_ROWS, TQ), F32)],
        compiler_params=_cparams(("parallel", "parallel", "arbitrary")),
        name="fox",
    )(main3d, main3d, vt3d, c3d)


def _dsa_kernel(q_ref, k_ref, vt_ref, qi_ref, ki_ref, misc_ref, o_ref, key_ref, dig_ref, bias_ref,
                za_ref, zb_ref, acc_ref, *, top_k):
    i = pl.program_id(1)
    diag = _causal_tile_t(False)

    wi_t = misc_ref[0].T[MISC_WI:MISC_WI + IDX_HEADS, :] * IDX_W_SCALE
    qih = [_head_q(qi_ref[0][:, (h // 2) * LANES:(h // 2 + 1) * LANES], h % 2) for h in range(IDX_HEADS)]

    def score_block(j, mask):
        kij = ki_ref[0, _rows(j), :]
        sc = None
        for h in range(IDX_HEADS):
            term = wi_t[h:h + 1, :] * jnp.maximum(_dot_nt(kij, qih[h]), 0.0)
            sc = term if sc is None else sc + term
        if mask is not None:
            sc = jnp.where(mask, sc, -jnp.inf)
        bits = lax.bitcast_convert_type(jnp.where(sc == 0.0, 0.0, sc), jnp.int32)
        key_ref[_rows(j), :] = bits ^ ((bits >> 31) & jnp.int32(0x7FFFFFFF))

    def score_pair(t, _):
        score_block(2 * t, None)
        score_block(2 * t + 1, None)
        return 0

    def score_odd_tail():
        score_block(i - 1, None)
        score_block(i, diag)
        return 0

    def score_even_tail():
        score_block(i, diag)
        return 0

    lax.fori_loop(0, i // 2, score_pair, 0)
    lax.cond(i % 2 == 1, score_odd_tail, score_even_tail)

    small = MXU_DTYPE
    group = 2 * SUBLANES

    assert (key_ref.shape[0] // group) <= 256

    def count_digits_ge(cand):
        def hits(j):
            hit = jnp.where(dig_ref[_rows(j), :] >= cand, jnp.ones((), small), jnp.zeros((), small))
            part = hit[0:group]
            for g in range(1, TK // group):
                part = part + hit[g * group:(g + 1) * group]
            return part

        acc = lax.fori_loop(0, (i + 1) // 2, lambda t, a: a + (hits(2 * t) + hits(2 * t + 1)),
                            jnp.zeros((group, TQ), small))
        acc = lax.cond((i + 1) % 2 == 1, lambda a: a + hits(i), lambda a: a, acc)
        return jnp.sum(acc.astype(F32), axis=0, keepdims=True)

    above = jnp.zeros((1, TQ), F32)
    prefix = None
    for shift in (24, 16, 8, 0):
        def fill(j, _, shift=shift, prefix=prefix):
            key = key_ref[_rows(j), :]
            if prefix is None:
                digit = (key >> shift) + 128
            else:
                digit = (key ^ (prefix << (shift + 8))) >> shift
                shares = lax.bitcast_convert_type(digit, jnp.uint32) < jnp.uint32(256)
                digit = jnp.where(shares, digit, -1)
            dig_ref[_rows(j), :] = digit.astype(F32).astype(small)
            return 0

        lax.fori_loop(0, i + 1, fill, 0)

        def search(_, st, above=above):
            lo, hi, n_hi = st
            mid = (lo + hi) * 0.5
            n_mid = count_digits_ge(mid.astype(small))
            ok = above + n_mid >= top_k
            return jnp.where(ok, mid, lo), jnp.where(ok, hi, mid), jnp.where(ok, n_hi, n_mid)

        start = (jnp.zeros((1, TQ), F32), jnp.full((1, TQ), 256.0, F32), jnp.zeros((1, TQ), F32))
        digit, _, n_hi = lax.fori_loop(0, 8, search, start)
        above = above + n_hi
        digit = digit.astype(jnp.int32)
        prefix = digit - 128 if prefix is None else (prefix << 8) | digit
    thr_key = prefix

    need = top_k - above
    kr = lax.broadcasted_iota(jnp.int32, (TK, TK), 0)
    kc = lax.broadcasted_iota(jnp.int32, (TK, TK), 1)
    upto = (kc <= kr).astype(MXU_DTYPE)

    def bias_block(j, mask, seen):
        key = key_ref[_rows(j), :]
        tie = key == thr_key
        rank = _dot(upto, jnp.where(tie, 1.0, 0.0).astype(MXU_DTYPE)) + seen
        sel = (key > thr_key) | (tie & (rank <= need))
        if mask is not None:
            sel = sel & mask
        bias_ref[_rows(j), :] = jnp.where(sel, 0.0, NEG_BIG)
        return rank[TK - 1:TK, :]

    def bias_pair(t, seen):
        return bias_block(2 * t + 1, None, bias_block(2 * t, None, seen))

    def bias_odd_tail(seen):
        bias_block(i, diag, bias_block(i - 1, None, seen))
        return 0

    def bias_even_tail(seen):
        bias_block(i, diag, seen)
        return 0

    seen = lax.fori_loop(0, i // 2, bias_pair, jnp.zeros((1, TQ), F32))
    lax.cond(i % 2 == 1, bias_odd_tail, bias_even_tail, seen)

    bias_at = lambda rows: bias_ref[rows, :]
    heads = []
    for h in range(DSA_HEADS):
        g = h // DSA_GROUP
        k_at = lambda rows, g=g: k_ref[0, rows, g * LANES:(g + 1) * LANES]
        vt_at = lambda rows, g=g: vt_ref[0, g * HEAD_DIM:(g + 1) * HEAD_DIM, rows]
        qh = _head_q(q_ref[0][:, (h // 2) * LANES:(h // 2 + 1) * LANES], h % 2)
        heads.append((qh, k_at, vt_at, bias_at))
    _store_heads_t(o_ref, _flash_heads(heads, i, None, (za_ref, zb_ref), acc_ref))


def _dsa(main3d, vt3d, misc3d, top_k):
    b, s, _ = main3d.shape
    nq = s // TQ
    assert top_k <= TK, "the first key block must hold at least top_k keys"
    return pl.pallas_call(
        functools.partial(_dsa_kernel, top_k=top_k),
        grid=(b, nq),
        in_specs=[
            pl.BlockSpec((1, TQ, 4 * LANES), lambda bi, i: (bi, i, COL_QC // (4 * LANES))),
            pl.BlockSpec((1, s, 2 * LANES), lambda bi, i: (bi, 0, COL_KC // (2 * LANES))),
            pl.BlockSpec((1, LANES, s), lambda bi, i: (bi, ROW_VC // LANES, 0)),
            pl.BlockSpec((1, TQ, 2 * LANES), lambda bi, i: (bi, i, COL_QI // (2 * LANES))),
            pl.BlockSpec((1, s, LANES), lambda bi, i: (bi, 0, COL_KI // LANES)),
            pl.BlockSpec((1, TQ, LANES), lambda bi, i: (bi, i, 0)),
        ],
        out_specs=pl.BlockSpec((1, TQ, BRANCH_WIDTH), lambda bi, i: (bi, i, 0)),
        out_shape=jax.ShapeDtypeStruct((b, s, BRANCH_WIDTH), MXU_DTYPE),
        scratch_shapes=[pltpu.VMEM((s, TQ), jnp.int32), pltpu.VMEM((s, TQ), MXU_DTYPE), pltpu.VMEM((s, TQ), F32),
                        pltpu.VMEM((DSA_HEADS, TK, TQ), F32), pltpu.VMEM((DSA_HEADS, TK, TQ), F32),
                        pltpu.VMEM((DSA_HEADS, HEAD_DIM + SUM_ROWS, TQ), F32)],
        compiler_params=_cparams(("parallel", "arbitrary")),
        name="dsa",
    )(main3d, main3d, vt3d, main3d, main3d, misc3d)


def _merge_kernel(x_ref, a_ref, b_ref, c_ref, g_ref, wg_ref, wb_ref, wo_ref, o_ref):
    x = x_ref[...]
    d = x.shape[1]
    h = _rmsnorm(x, g_ref[...]).astype(MXU_DTYPE)
    y = None
    for n, br_ref in enumerate((a_ref, b_ref, c_ref)):
        gate = jax.nn.sigmoid(_dot(h, wg_ref[:, n * d:(n + 1) * d]))
        term = gate * _dot(br_ref[...], wb_ref[n])
        y = term if y is None else y + term
    o_ref[...] = x + _dot(y.astype(MXU_DTYPE), wo_ref[...])


def _merge(x2d, br_a, br_b, br_c, g, w_gate, w_branch, w_out, tm):
    n, d = x2d.shape
    bw = br_a.shape[1]
    row = lambda i: (i, 0)
    const = lambda i: (0, 0)
    return pl.pallas_call(
        _merge_kernel,
        grid=(n // tm,),
        in_specs=[
            pl.BlockSpec((tm, d), row),
            pl.BlockSpec((tm, bw), row),
            pl.BlockSpec((tm, bw), row),
            pl.BlockSpec((tm, bw), row),
            pl.BlockSpec((1, d), const),
            pl.BlockSpec((d, N_BRANCH * d), const),
            pl.BlockSpec((N_BRANCH, bw, d), lambda i: (0, 0, 0)),
            pl.BlockSpec((d, d), const),
        ],
        out_specs=pl.BlockSpec((tm, d), row),
        out_shape=jax.ShapeDtypeStruct((n, d), F32),
        compiler_params=_cparams(("parallel",)),
        name="merge",
    )(x2d, br_a, br_b, br_c, g, w_gate, w_branch, w_out)


def _mlp_kernel(x_ref, g_ref, wu_ref, wd_ref, gf_ref, o_ref, *, ff_chunk, final_norm):
    x = x_ref[...]
    h = _rmsnorm(x, g_ref[...]).astype(MXU_DTYPE)
    acc = x
    for c in range(wu_ref.shape[1] // ff_chunk):
        u = jnp.maximum(_dot(h, wu_ref[:, c * ff_chunk:(c + 1) * ff_chunk]), 0.0)
        acc = acc + _dot((u * u).astype(MXU_DTYPE), wd_ref[c * ff_chunk:(c + 1) * ff_chunk, :])
    if final_norm:
        acc = _rmsnorm(acc, gf_ref[...])
    o_ref[...] = acc


def _mlp(x2d, g, w_up, w_down, g_final, tm, final_norm):
    n, d = x2d.shape
    ff = w_up.shape[1]
    row = lambda i: (i, 0)
    const = lambda i: (0, 0)
    return pl.pallas_call(
        functools.partial(_mlp_kernel, ff_chunk=min(ff, 1024), final_norm=final_norm),
        grid=(n // tm,),
        in_specs=[
            pl.BlockSpec((tm, d), row),
            pl.BlockSpec((1, d), const),
            pl.BlockSpec((d, ff), const),
            pl.BlockSpec((ff, d), const),
            pl.BlockSpec((1, d), const),
        ],
        out_specs=pl.BlockSpec((tm, d), row),
        out_shape=jax.ShapeDtypeStruct((n, d), F32),
        compiler_params=_cparams(("parallel",)),
        name="mlp",
    )(x2d, g, w_up, w_down, g_final)


def _w_in_columns():
    sizes = (("qa", BRANCH_WIDTH), ("ka", BRANCH_WIDTH), ("va", BRANCH_WIDTH),
             ("qb", BRANCH_WIDTH), ("kb", BRANCH_WIDTH), ("vb", BRANCH_WIDTH), ("fl", FOX_HEADS),
             ("qc", DSA_HEADS * HEAD_DIM), ("kc", DSA_KV_HEADS * HEAD_DIM), ("vc", DSA_KV_HEADS * HEAD_DIM),
             ("qi", IDX_HEADS * IDX_DIM), ("ki", IDX_DIM), ("wi", IDX_HEADS), ("gates", N_BRANCH * 1024))
    cols, o = {}, 0
    for name, width in sizes:
        cols[name] = (o, o + width)
        o += width
    return cols, o


def _pack_kernel(w_ref, main_ref, vals_ref, gates_ref):
    cols, _ = _w_in_columns()

    def src(name, lo=0, hi=None):
        a, b = cols[name]
        return w_ref[0, :, a + lo:(b if hi is None else a + hi)]

    def put(ref, start, value):
        ref[0, :, start:start + value.shape[1]] = value.astype(ref.dtype)

    put(main_ref, COL_QA, src("qa"))
    put(main_ref, COL_KA, src("ka"))
    put(main_ref, COL_QB, src("qb"))
    put(main_ref, COL_KB, src("kb"))
    put(main_ref, COL_QC, src("qc"))
    for g in range(DSA_KV_HEADS):
        kg = src("kc", g * HEAD_DIM, (g + 1) * HEAD_DIM)
        put(main_ref, COL_KC + g * LANES, jnp.concatenate([kg, kg], axis=1))
    put(main_ref, COL_QI, src("qi"))
    ki = src("ki")
    put(main_ref, COL_KI, jnp.concatenate([ki, ki], axis=1))
    put(vals_ref, ROW_VA, src("va"))
    put(vals_ref, ROW_VB, src("vb"))
    put(vals_ref, ROW_VC, src("vc"))
    pad = jnp.zeros((w_ref.shape[1], LANES - FOX_HEADS - IDX_HEADS), F32)
    put(vals_ref, VT_ROWS, jnp.concatenate([src("fl"), src("wi"), pad], axis=1))
    put(gates_ref, 0, src("gates"))


def _pack_w_in(w_in):
    depth, d, total = w_in.shape
    cols, end = _w_in_columns()
    assert end == total and cols["gates"][1] - cols["gates"][0] == N_BRANCH * d
    rows = 256
    blk = lambda width: pl.BlockSpec((1, rows, width), lambda l, r: (l, r, 0))
    widths = (MAIN_COLS, VALS_COLS, N_BRANCH * d)
    return pl.pallas_call(
        _pack_kernel,
        grid=(depth, d // rows),
        in_specs=[blk(total)],
        out_specs=[blk(wd) for wd in widths],
        out_shape=[jax.ShapeDtypeStruct((depth, d, wd), MXU_DTYPE) for wd in widths],
        compiler_params=_cparams(("parallel", "parallel")),
        name="pack",
    )(w_in)


def _rope_rows():
    lane = jnp.arange(LANES) % HEAD_DIM
    half = ROT_DIM // 2
    inv_freq = ROPE_THETA ** (-jnp.arange(0, ROT_DIM, 2, dtype=F32) / ROT_DIM)
    freq = jnp.where(lane < ROT_DIM, inv_freq[lane % half], 0.0).astype(F32)
    sg1 = jnp.where(lane < half, -1.0, 0.0).astype(F32)
    sg2 = jnp.where((lane >= half) & (lane < ROT_DIM), 1.0, 0.0).astype(F32)
    return freq[None, :], sg1[None, :], sg2[None, :]


def kernel(x, positions, g_mix, w_in, b_forget, w_branch, w_out, g_mlp, w_up, w_down, g_final):
    b, s, d = x.shape
    n = b * s
    depth = w_in.shape[0]
    top_k = min(TOPK_MAX, s // 4)
    tm = min(512, s)
    freq, sg1, sg2 = _rope_rows()
    pos2d = positions.reshape(n, 1)
    x2d = x.reshape(n, d)
    w_main_all, w_v_all, w_gate_all = _pack_w_in(w_in)
    for layer in range(depth):
        w_main, w_v, w_gate = w_main_all[layer], w_v_all[layer], w_gate_all[layer]
        bias_row = jnp.concatenate([b_forget[layer], jnp.zeros((LANES - FOX_HEADS,), F32)])[None, :]
        main, vt, misc = _proj(x2d, pos2d, g_mix[layer][None, :], freq, sg1, sg2, w_main, w_v, tm, b, s)
        main3d = main.reshape(b, s, MAIN_COLS)
        misc3d = misc.reshape(b, s, LANES)
        cum = _cumf(misc3d, bias_row)
        br_a = _sb(main3d, vt).reshape(n, BRANCH_WIDTH)
        br_b = _fox(main3d, vt, cum).reshape(n, BRANCH_WIDTH)
        br_c = _dsa(main3d, vt, misc3d, top_k).reshape(n, BRANCH_WIDTH)
        x2d = _merge(x2d, br_a, br_b, br_c, g_mix[layer][None, :], w_gate,
                     w_branch[layer].astype(MXU_DTYPE), w_out[layer].astype(MXU_DTYPE), tm)
        x2d = _mlp(x2d, g_mlp[layer][None, :], w_up[layer].astype(MXU_DTYPE),
                   w_down[layer].astype(MXU_DTYPE), g_final[None, :], tm, layer == depth - 1)
    return x2d.reshape(b, s, d)
```

```python
import functools

import jax
import jax.numpy as jnp
from jax import lax
from jax.experimental import pallas as pl
from jax.experimental.pallas import tpu as pltpu

F32 = jnp.float32
MXU_DTYPE = jnp.bfloat16

HEAD_DIM = 64
SB_HEADS = 8
FOX_HEADS = 8
DSA_HEADS = 8
DSA_KV_HEADS = 2
DSA_GROUP = DSA_HEADS // DSA_KV_HEADS
IDX_HEADS = 4
IDX_DIM = 64
N_BRANCH = 3
ROPE_THETA = 500000.0
ROT_DIM = HEAD_DIM // 4
TOPK_MAX = 256
NORM_EPS = 1e-6

LANES = 128
SUBLANES = 8
SUM_ROWS = 2 * SUBLANES
BRANCH_WIDTH = SB_HEADS * HEAD_DIM
ATTN_SCALE = HEAD_DIM ** -0.5
assert IDX_DIM == HEAD_DIM
IDX_W_SCALE = IDX_HEADS ** -0.5

COL_QA, COL_KA = 0, 512
COL_QB, COL_KB = 1024, 1536
COL_QC = 2048
COL_KC = 2560
COL_QI = 2816
COL_KI = 3072
MAIN_COLS = 3200
ROPE_START = COL_QC
ROW_VA, ROW_VB, ROW_VC = 0, 512, 1024
VT_ROWS = 1152
VALS_COLS = VT_ROWS + LANES
MISC_FL, MISC_WI = 0, 8

TQ = 256
TK = 256
HEADS_PER_STEP = 8
NEG_BIG = -1e30

VMEM_LIMIT = 56 * 1024 * 1024


def _cparams(sem):
    return pltpu.CompilerParams(dimension_semantics=sem, vmem_limit_bytes=VMEM_LIMIT)


def _rmsnorm(x, g):
    y = x * lax.rsqrt(jnp.mean(x * x, axis=-1, keepdims=True) + NORM_EPS)
    return y * g


def _dot(a, b):
    return jnp.dot(a, b, preferred_element_type=F32)


def _dot_nt(a, b):
    return lax.dot_general(a, b, (((1,), (1,)), ((), ())), preferred_element_type=F32)


def _split2(x):
    hi = x.astype(MXU_DTYPE)
    lo = (x - hi.astype(F32)).astype(MXU_DTYPE)
    return hi, lo


def _split3(x):
    hi = x.astype(MXU_DTYPE)
    r = x - hi.astype(F32)
    mid = r.astype(MXU_DTYPE)
    lo = (r - mid.astype(F32)).astype(MXU_DTYPE)
    return hi, mid, lo


def _log_sigmoid_parts(z):
    neg_abs = lax.bitcast_convert_type(lax.bitcast_convert_type(z, jnp.int32) | jnp.int32(-2 ** 31), F32)
    sp = jnp.log(1.0 + jnp.exp(neg_abs))
    return jnp.minimum(z, 0.0) - sp, jnp.maximum(z, 0.0) + sp


def _proj_kernel(x_ref, pos_ref, g_ref, freq_ref, sg1_ref, sg2_ref, wm_ref, wv_ref,
                 main_ref, vt_ref, misc_ref):
    h = _rmsnorm(x_ref[...], g_ref[...]).astype(MXU_DTYPE)
    ang = pos_ref[...].astype(F32) * freq_ref[...]
    cos = jnp.cos(ang)
    sin = jnp.sin(ang)
    s_up = sin * sg1_ref[...]
    s_dn = sin * sg2_ref[...]
    half = ROT_DIM // 2
    chunk = 4 * LANES
    for c0 in range(0, MAIN_COLS, chunk):
        c1 = min(c0 + chunk, MAIN_COLS)
        p = _dot(h, wm_ref[:, c0:c1])
        if c0 >= ROPE_START:
            for t in range((c1 - c0) // LANES):
                pt = p[:, t * LANES:(t + 1) * LANES]
                pt = (pt * cos + pltpu.roll(pt, LANES - half, 1) * s_up
                      + pltpu.roll(pt, half, 1) * s_dn)
                main_ref[:, c0 + t * LANES:c0 + (t + 1) * LANES] = pt.astype(main_ref.dtype)
        else:
            main_ref[:, c0:c1] = p.astype(main_ref.dtype)
    for c0 in range(0, VALS_COLS, 2 * LANES):
        pv = _dot(h, wv_ref[:, c0:c0 + 2 * LANES])
        for r0 in (c0, c0 + LANES):
            piece = pv[:, r0 - c0:r0 - c0 + LANES]
            if r0 < VT_ROWS:
                vt_ref[0, r0:r0 + LANES, :] = piece.T.astype(vt_ref.dtype)
            else:
                misc_ref[...] = piece


def _proj(x2d, pos2d, g, freq, sg1, sg2, w_main, w_v, tm, batch, seq):
    n, d = x2d.shape
    nsb = seq // tm
    const = lambda i: (0, 0)
    return pl.pallas_call(
        _proj_kernel,
        grid=(n // tm,),
        in_specs=[
            pl.BlockSpec((tm, d), lambda i: (i, 0)),
            pl.BlockSpec((tm, 1), lambda i: (i, 0)),
            pl.BlockSpec((1, d), const),
            pl.BlockSpec((1, LANES), const),
            pl.BlockSpec((1, LANES), const),
            pl.BlockSpec((1, LANES), const),
            pl.BlockSpec((d, MAIN_COLS), const),
            pl.BlockSpec((d, VALS_COLS), const),
        ],
        out_specs=[
            pl.BlockSpec((tm, MAIN_COLS), lambda i: (i, 0)),
            pl.BlockSpec((1, VT_ROWS, tm), lambda i: (i // nsb, 0, i % nsb)),
            pl.BlockSpec((tm, LANES), lambda i: (i, 0)),
        ],
        out_shape=[
            jax.ShapeDtypeStruct((n, MAIN_COLS), MXU_DTYPE),
            jax.ShapeDtypeStruct((batch, VT_ROWS, seq), MXU_DTYPE),
            jax.ShapeDtypeStruct((n, LANES), F32),
        ],
        compiler_params=_cparams(("parallel",)),
        name="proj",
    )(x2d, pos2d, g, freq, sg1, sg2, w_main, w_v)


def _cumf_kernel(misc_ref, bias_ref, c_ref):
    s = misc_ref.shape[1]
    r = lax.broadcasted_iota(jnp.int32, (TK, TK), 0)
    c = lax.broadcasted_iota(jnp.int32, (TK, TK), 1)
    incl = (c <= r).astype(MXU_DTYPE)
    carry = jnp.zeros((1, LANES), F32)
    for b in range(s // TK):
        logit = misc_ref[0, b * TK:(b + 1) * TK, :] + bias_ref[...]
        logf, _ = _log_sigmoid_parts(logit)
        hi, mid, lo = _split3(logf)
        cs = (_dot(incl, lo) + _dot(incl, mid)) + _dot(incl, hi) + carry
        c_ref[0, b * TK:(b + 1) * TK, :] = cs
        carry = cs[TK - 1:TK, :]


def _cumf(misc3d, bias_row):
    b, s, _ = misc3d.shape
    return pl.pallas_call(
        _cumf_kernel,
        grid=(b,),
        in_specs=[
            pl.BlockSpec((1, s, LANES), lambda i: (i, 0, 0)),
            pl.BlockSpec((1, LANES), lambda i: (0, 0)),
        ],
        out_specs=pl.BlockSpec((1, s, LANES), lambda i: (i, 0, 0)),
        out_shape=jax.ShapeDtypeStruct((b, s, LANES), F32),
        compiler_params=_cparams(("parallel",)),
        name="cumf",
    )(misc3d, bias_row)


def _head_q(q_tile, half):
    lane = lax.broadcasted_iota(jnp.int32, q_tile.shape, 1)
    keep = (lane // HEAD_DIM) == half
    return jnp.where(keep, q_tile.astype(F32) * ATTN_SCALE, 0.0).astype(MXU_DTYPE)


def _causal_tile_t(strict):
    key = lax.broadcasted_iota(jnp.int32, (TK, TQ), 0)
    qry = lax.broadcasted_iota(jnp.int32, (TK, TQ), 1)
    return (key < qry) if strict else (key <= qry)


def _rows(j):
    return pl.ds(pl.multiple_of(j * TK, TK), TK)


def _lookahead_loop(n_last, bufs, issue, step, carry, finish, primed=False):
    buf_a, buf_b = bufs

    def pair(t, carry):
        n = 2 * t
        issue(n + 1, buf_b)
        carry = step(n, buf_a, carry, False)
        issue(n + 2, buf_a)
        return step(n + 1, buf_b, carry, False)

    def odd_tail(carry):
        issue(n_last, buf_b)
        carry = step(n_last - 1, buf_a, carry, False)
        step(n_last, buf_b, carry, True)
        finish()
        return 0

    def even_tail(carry):
        step(n_last, buf_a, carry, True)
        finish()
        return 0

    if not primed:
        issue(0, buf_a)
    carry = lax.fori_loop(0, n_last // 2, pair, carry)
    lax.cond(n_last % 2 == 1, odd_tail, even_tail, carry)


def _flash_heads(heads, n_full, diag_mask, z_bufs, acc_ref, o_ref):
    half_k, half_q = TK // 2, TQ // 2

    def add_bias(s, bias):
        if isinstance(bias, (list, tuple)):
            return jnp.concatenate([s[:, t * LANES:(t + 1) * LANES] + bias[t] for t in range(len(bias))],
                                   axis=1)
        return s + bias

    def later_queries(bias):
        return bias[half_q // LANES:] if isinstance(bias, (list, tuple)) else bias[:, half_q:]

    ones = jnp.ones((SUM_ROWS, TK), MXU_DTYPE)

    def issue(j, buf):
        for h, (qh, k_at, _, _) in enumerate(heads):
            buf[h] = _dot_nt(k_at(_rows(j)), qh)

    def update(tiles, rows, maxes, lanes):
        stats = []
        for s, m in zip(tiles, maxes):
            m_new = jnp.maximum(m, jnp.max(s, axis=0, keepdims=True))
            stats.append((m_new, jnp.exp(m - m_new), jnp.exp(s - m_new).astype(MXU_DTYPE)))
        for h, ((_, alpha, p), (_, _, vt_at, _)) in enumerate(zip(stats, heads)):
            vt1 = jnp.concatenate([vt_at(rows), ones[:, :p.shape[0]]], axis=0)
            acc_ref[h, :, lanes] = alpha * acc_ref[h, :, lanes] + _dot(vt1, p)
        return tuple(m_new for m_new, _, _ in stats)

    def step(j, buf, maxes):
        rows = _rows(j)
        tiles = [add_bias(buf[h], bias_at(rows)) for h, (_, _, _, bias_at) in enumerate(heads)]
        return update(tiles, rows, maxes, slice(None))

    def diagonal_step(j, buf, maxes):
        start = pl.multiple_of(j * TK, TK)
        early, late = pl.ds(start, half_k), pl.ds(pl.multiple_of(start + half_k, half_k), half_k)

        def masked(s, mask):
            return s if diag_mask is None else jnp.where(mask, s, NEG_BIG)

        tiles = [masked(add_bias(buf[h, :half_k, :], bias_at(early)), None if diag_mask is None
                        else diag_mask[:half_k, :]) for h, (_, _, _, bias_at) in enumerate(heads)]
        maxes = update(tiles, early, maxes, slice(None))
        tiles = [masked(add_bias(buf[h, half_k:, half_q:], later_queries(bias_at(late))), None if diag_mask is None
                        else diag_mask[half_k:, half_q:]) for h, (_, _, _, bias_at) in enumerate(heads)]
        update(tiles, late, tuple(m[:, half_q:] for m in maxes), slice(half_q, TQ))
        return maxes

    def finish():
        _store_heads_t(o_ref, [acc_ref[h, :HEAD_DIM, :] / acc_ref[h, HEAD_DIM:HEAD_DIM + 1, :]
                               for h in range(len(heads))])

    acc_ref[...] = jnp.zeros(acc_ref.shape, F32)
    init = tuple(jnp.full((1, TQ), NEG_BIG, F32) for _ in heads)
    _lookahead_loop(n_full, z_bufs, issue,
                    lambda j, buf, maxes, last: (diagonal_step if last else step)(j, buf, maxes), init, finish)


def _store_heads_t(o_ref, outs_t):
    for t in range(len(outs_t) // 2):
        pair_t = jnp.concatenate([outs_t[2 * t], outs_t[2 * t + 1]], axis=0)
        o_ref[0, :, t * LANES:(t + 1) * LANES] = pair_t.T.astype(o_ref.dtype)


def _sb_kernel(q_ref, k_ref, vt_ref, o_ref, za_ref, zb_ref, acc_ref):
    i = pl.program_id(2)
    r = lax.broadcasted_iota(jnp.int32, (TK, TK), 0)
    c = lax.broadcasted_iota(jnp.int32, (TK, TK), 1)
    after = (c > r).astype(MXU_DTYPE)
    strict = _causal_tile_t(True)
    n_heads = HEADS_PER_STEP
    qhs = [_head_q(q_ref[0][:, (h // 2) * LANES:(h // 2 + 1) * LANES], h % 2) for h in range(n_heads)]

    def issue(j, buf):
        for h in range(n_heads):
            buf[h] = _dot_nt(k_ref[0, _rows(j), (h // 2) * LANES:(h // 2 + 1) * LANES], qhs[h])

    def update(tiles, rows, tails, lanes, mask):
        n_rows = tiles[0].shape[0]
        parts = []
        for z in tiles:
            log_beta, neg_l1m = _log_sigmoid_parts(z)
            if mask is not None:
                neg_l1m = jnp.where(mask, neg_l1m, 0.0)
            hi, lo = _split2(neg_l1m)
            parts.append((log_beta, neg_l1m[0:1, :], jnp.concatenate([hi, lo], axis=0)))
        after_rows = jnp.concatenate([after[:n_rows, :n_rows]] * 2, axis=1)
        laters = [_dot(after_rows, hl) for (_, _, hl) in parts]
        ws = []
        for (log_beta, _, _), later in zip(parts, laters):
            w = jnp.exp(log_beta - later)
            if mask is not None:
                w = jnp.where(mask, w, 0.0)
            ws.append(w.astype(MXU_DTYPE))
        out = []
        for h, (w, (_, first, _), later, tail) in enumerate(zip(ws, parts, laters, tails)):
            vtj = vt_ref[0, h * HEAD_DIM:(h + 1) * HEAD_DIM, rows]
            acc_ref[h, :, lanes] = acc_ref[h, :, lanes] + jnp.exp(-tail) * _dot(vtj, w)
            out.append(tail + (first + later[0:1, :]))
        return tuple(out)

    def step(j, buf, tails):
        return update([buf[h] for h in range(n_heads)], _rows(j), tails, slice(None), None)

    def diagonal_step(j, buf, tails):
        half_k, half_q = TK // 2, TQ // 2
        start = pl.multiple_of(j * TK, TK)
        early, late = pl.ds(start, half_k), pl.ds(pl.multiple_of(start + half_k, half_k), half_k)
        late_tails = update([buf[h, half_k:, half_q:] for h in range(n_heads)], late,
                            tuple(t[:, half_q:] for t in tails), slice(half_q, TQ), strict[half_k:, half_q:])
        tails = tuple(jnp.concatenate([t[:, :half_q], lt], axis=1) for t, lt in zip(tails, late_tails))
        return update([buf[h, :half_k, :] for h in range(n_heads)], early, tails, slice(None), strict[:half_k, :])

    acc_ref[...] = jnp.zeros(acc_ref.shape, F32)
    carry = tuple(jnp.zeros((1, TQ), F32) for _ in range(n_heads))
    issue(i, za_ref)

    def finish():
        _store_heads_t(o_ref, [acc_ref[h] for h in range(n_heads)])

    def with_earlier_blocks(cr):
        issue(i - 1, zb_ref)
        cr = diagonal_step(i, za_ref, cr)
        _lookahead_loop(i - 1, (zb_ref, za_ref), lambda n, buf: issue(i - 1 - n, buf),
                        lambda n, buf, c, last: step(i - 1 - n, buf, c), cr, finish, primed=True)
        return 0

    def diagonal_only(cr):
        diagonal_step(i, za_ref, cr)
        finish()
        return 0

    lax.cond(i > 0, with_earlier_blocks, diagonal_only, carry)


def _sb(main3d, vt3d):
    b, s, _ = main3d.shape
    nq = s // TQ
    w = HEADS_PER_STEP * HEAD_DIM
    qb, kb, vb = COL_QA // w, COL_KA // w, ROW_VA // w
    return pl.pallas_call(
        _sb_kernel,
        grid=(b, SB_HEADS // HEADS_PER_STEP, nq),
        in_specs=[
            pl.BlockSpec((1, TQ, w), lambda bi, hg, i: (bi, i, qb + hg)),
            pl.BlockSpec((1, s, w), lambda bi, hg, i: (bi, 0, kb + hg)),
            pl.BlockSpec((1, w, s), lambda bi, hg, i: (bi, vb + hg, 0)),
        ],
        out_specs=pl.BlockSpec((1, TQ, w), lambda bi, hg, i: (bi, i, hg)),
        out_shape=jax.ShapeDtypeStruct((b, s, BRANCH_WIDTH), MXU_DTYPE),
        scratch_shapes=[pltpu.VMEM((HEADS_PER_STEP, TK, TQ), F32), pltpu.VMEM((HEADS_PER_STEP, TK, TQ), F32),
                        pltpu.VMEM((HEADS_PER_STEP, HEAD_DIM, TQ), F32)],
        compiler_params=_cparams(("parallel", "parallel", "arbitrary")),
        name="sb",
    )(main3d, main3d, vt3d)


def _fox_kernel(q_ref, k_ref, vt_ref, c_ref, o_ref, cb_ref, za_ref, zb_ref, acc_ref):
    hg = pl.program_id(1)
    i = pl.program_id(2)
    n_heads = HEADS_PER_STEP

    @pl.when(i == 0)
    def _():
        c = c_ref[0]
        lane = lax.broadcasted_iota(jnp.int32, c.shape, 1)
        for h in range(n_heads):
            if n_heads == FOX_HEADS:
                col = c[:, h:h + 1]
            else:
                col = jnp.sum(jnp.where(lane == hg * n_heads + h, c, 0.0), axis=1, keepdims=True)
            cb_ref[h] = jnp.broadcast_to(-col, c.shape)

    heads = []
    for h in range(n_heads):
        t = h // 2
        qh = _head_q(q_ref[0][:, t * LANES:(t + 1) * LANES], h % 2)
        k_at = lambda rows, t=t: k_ref[0, rows, t * LANES:(t + 1) * LANES]
        vt_at = lambda rows, h=h: vt_ref[0, h * HEAD_DIM:(h + 1) * HEAD_DIM, rows]
        bias_at = lambda rows, h=h: [cb_ref[h, rows, :]] * (TQ // LANES)
        heads.append((qh, k_at, vt_at, bias_at))
    _flash_heads(heads, i, _causal_tile_t(False), (za_ref, zb_ref), acc_ref, o_ref)


def _fox(main3d, vt3d, c3d):
    b, s, _ = main3d.shape
    nq = s // TQ
    w = HEADS_PER_STEP * HEAD_DIM
    qb, kb, vb = COL_QB // w, COL_KB // w, ROW_VB // w
    return pl.pallas_call(
        _fox_kernel,
        grid=(b, FOX_HEADS // HEADS_PER_STEP, nq),
        in_specs=[
            pl.BlockSpec((1, TQ, w), lambda bi, hg, i: (bi, i, qb + hg)),
            pl.BlockSpec((1, s, w), lambda bi, hg, i: (bi, 0, kb + hg)),
            pl.BlockSpec((1, w, s), lambda bi, hg, i: (bi, vb + hg, 0)),
            pl.BlockSpec((1, s, LANES), lambda bi, hg, i: (bi, 0, 0)),
        ],
        out_specs=pl.BlockSpec((1, TQ, w), lambda bi, hg, i: (bi, i, hg)),
        out_shape=jax.ShapeDtypeStruct((b, s, BRANCH_WIDTH), MXU_DTYPE),
        scratch_shapes=[pltpu.VMEM((HEADS_PER_STEP, s, LANES), F32),
                        pltpu.VMEM((HEADS_PER_STEP, TK, TQ), F32), pltpu.VMEM((HEADS_PER_STEP, TK, TQ), F32),
                        pltpu.VMEM((HEADS_PER_STEP, HEAD_DIM + SUM_ROWS, TQ), F32)],
        compiler_params=_cparams(("parallel", "parallel", "arbitrary")),
        name="fox",
    )(main3d, main3d, vt3d, c3d)


def _dsa_kernel(q_ref, k_ref, vt_ref, qi_ref, ki_ref, misc_ref, o_ref, key_ref, dig_ref, bias_ref,
                za_ref, zb_ref, acc_ref, *, top_k):
    i = pl.program_id(1)
    diag = _causal_tile_t(False)

    wi_t = misc_ref[0].T[MISC_WI:MISC_WI + IDX_HEADS, :] * IDX_W_SCALE
    qih = [_head_q(qi_ref[0][:, (h // 2) * LANES:(h // 2 + 1) * LANES], h % 2) for h in range(IDX_HEADS)]

    def score_block(j, mask):
        kij = ki_ref[0, _rows(j), :]
        sc = None
        for h in range(IDX_HEADS):
            term = wi_t[h:h + 1, :] * jnp.maximum(_dot_nt(kij, qih[h]), 0.0)
            sc = term if sc is None else sc + term
        if mask is not None:
            sc = jnp.where(mask, sc, -jnp.inf)
        bits = lax.bitcast_convert_type(jnp.where(sc == 0.0, 0.0, sc), jnp.int32)
        key_ref[_rows(j), :] = bits ^ ((bits >> 31) & jnp.int32(0x7FFFFFFF))

    def score_pair(t, _):
        score_block(2 * t, None)
        score_block(2 * t + 1, None)
        return 0

    def score_odd_tail():
        score_block(i - 1, None)
        score_block(i, diag)
        return 0

    def score_even_tail():
        score_block(i, diag)
        return 0

    lax.fori_loop(0, i // 2, score_pair, 0)
    lax.cond(i % 2 == 1, score_odd_tail, score_even_tail)

    small = MXU_DTYPE
    group = 2 * SUBLANES

    assert (key_ref.shape[0] // group) <= 256

    def count_digits_ge(cand):
        def hits(j):
            hit = jnp.where(dig_ref[_rows(j), :] >= cand, jnp.ones((), small), jnp.zeros((), small))
            part = hit[0:group]
            for g in range(1, TK // group):
                part = part + hit[g * group:(g + 1) * group]
            return part

        acc = lax.fori_loop(0, (i + 1) // 2, lambda t, a: a + (hits(2 * t) + hits(2 * t + 1)),
                            jnp.zeros((group, TQ), small))
        acc = lax.cond((i + 1) % 2 == 1, lambda a: a + hits(i), lambda a: a, acc)
        return jnp.sum(acc.astype(F32), axis=0, keepdims=True)

    above = jnp.zeros((1, TQ), F32)
    prefix = None
    for shift in (24, 16, 8, 0):
        def fill(j, _, shift=shift, prefix=prefix):
            key = key_ref[_rows(j), :]
            if prefix is None:
                digit = (key >> shift) + 128
            else:
                digit = (key ^ (prefix << (shift + 8))) >> shift
                shares = lax.bitcast_convert_type(digit, jnp.uint32) < jnp.uint32(256)
                digit = jnp.where(shares, digit, -1)
            dig_ref[_rows(j), :] = digit.astype(F32).astype(small)
            return 0

        lax.fori_loop(0, i + 1, fill, 0)

        def search(_, st, above=above):
            lo, hi, n_hi = st
            mid = (lo + hi) * 0.5
            n_mid = count_digits_ge(mid.astype(small))
            ok = above + n_mid >= top_k
            return jnp.where(ok, mid, lo), jnp.where(ok, hi, mid), jnp.where(ok, n_hi, n_mid)

        start = (jnp.zeros((1, TQ), F32), jnp.full((1, TQ), 256.0, F32), jnp.zeros((1, TQ), F32))
        digit, _, n_hi = lax.fori_loop(0, 8, search, start)
        above = above + n_hi
        digit = digit.astype(jnp.int32)
        prefix = digit - 128 if prefix is None else (prefix << 8) | digit
    thr_key = prefix

    need = top_k - above
    kr = lax.broadcasted_iota(jnp.int32, (TK, TK), 0)
    kc = lax.broadcasted_iota(jnp.int32, (TK, TK), 1)
    upto = (kc <= kr).astype(MXU_DTYPE)

    def bias_block(j, mask, seen):
        key = key_ref[_rows(j), :]
        tie = key == thr_key
        rank = _dot(upto, jnp.where(tie, 1.0, 0.0).astype(MXU_DTYPE)) + seen
        sel = (key > thr_key) | (tie & (rank <= need))
        if mask is not None:
            sel = sel & mask
        bias_ref[_rows(j), :] = jnp.where(sel, 0.0, NEG_BIG)
        return rank[TK - 1:TK, :]

    def bias_pair(t, seen):
        return bias_block(2 * t + 1, None, bias_block(2 * t, None, seen))

    def bias_odd_tail(seen):
        bias_block(i, diag, bias_block(i - 1, None, seen))
        return 0

    def bias_even_tail(seen):
        bias_block(i, diag, seen)
        return 0

    seen = lax.fori_loop(0, i // 2, bias_pair, jnp.zeros((1, TQ), F32))
    lax.cond(i % 2 == 1, bias_odd_tail, bias_even_tail, seen)

    bias_at = lambda rows: bias_ref[rows, :]
    heads = []
    for h in range(DSA_HEADS):
        g = h // DSA_GROUP
        k_at = lambda rows, g=g: k_ref[0, rows, g * LANES:(g + 1) * LANES]
        vt_at = lambda rows, g=g: vt_ref[0, g * HEAD_DIM:(g + 1) * HEAD_DIM, rows]
        qh = _head_q(q_ref[0][:, (h // 2) * LANES:(h // 2 + 1) * LANES], h % 2)
        heads.append((qh, k_at, vt_at, bias_at))
    _flash_heads(heads, i, None, (za_ref, zb_ref), acc_ref, o_ref)


def _dsa(main3d, vt3d, misc3d, top_k):
    b, s, _ = main3d.shape
    nq = s // TQ
    assert top_k <= TK, "the first key block must hold at least top_k keys"
    return pl.pallas_call(
        functools.partial(_dsa_kernel, top_k=top_k),
        grid=(b, nq),
        in_specs=[
            pl.BlockSpec((1, TQ, 4 * LANES), lambda bi, i: (bi, i, COL_QC // (4 * LANES))),
            pl.BlockSpec((1, s, 2 * LANES), lambda bi, i: (bi, 0, COL_KC // (2 * LANES))),
            pl.BlockSpec((1, LANES, s), lambda bi, i: (bi, ROW_VC // LANES, 0)),
            pl.BlockSpec((1, TQ, 2 * LANES), lambda bi, i: (bi, i, COL_QI // (2 * LANES))),
            pl.BlockSpec((1, s, LANES), lambda bi, i: (bi, 0, COL_KI // LANES)),
            pl.BlockSpec((1, TQ, LANES), lambda bi, i: (bi, i, 0)),
        ],
        out_specs=pl.BlockSpec((1, TQ, BRANCH_WIDTH), lambda bi, i: (bi, i, 0)),
        out_shape=jax.ShapeDtypeStruct((b, s, BRANCH_WIDTH), MXU_DTYPE),
        scratch_shapes=[pltpu.VMEM((s, TQ), jnp.int32), pltpu.VMEM((s, TQ), MXU_DTYPE), pltpu.VMEM((s, TQ), F32),
                        pltpu.VMEM((DSA_HEADS, TK, TQ), F32), pltpu.VMEM((DSA_HEADS, TK, TQ), F32),
                        pltpu.VMEM((DSA_HEADS, HEAD_DIM + SUM_ROWS, TQ), F32)],
        compiler_params=_cparams(("parallel", "arbitrary")),
        name="dsa",
    )(main3d, main3d, vt3d, main3d, main3d, misc3d)


def _merge_kernel(x_ref, a_ref, b_ref, c_ref, g_ref, wg_ref, wb_ref, wo_ref, o_ref):
    x = x_ref[...]
    d = x.shape[1]
    h = _rmsnorm(x, g_ref[...]).astype(MXU_DTYPE)
    y = None
    for n, br_ref in enumerate((a_ref, b_ref, c_ref)):
        gate = jax.nn.sigmoid(_dot(h, wg_ref[:, n * d:(n + 1) * d]))
        term = gate * _dot(br_ref[...], wb_ref[n])
        y = term if y is None else y + term
    o_ref[...] = x + _dot(y.astype(MXU_DTYPE), wo_ref[...])


def _merge(x2d, br_a, br_b, br_c, g, w_gate, w_branch, w_out, tm):
    n, d = x2d.shape
    bw = br_a.shape[1]
    row = lambda i: (i, 0)
    const = lambda i: (0, 0)
    return pl.pallas_call(
        _merge_kernel,
        grid=(n // tm,),
        in_specs=[
            pl.BlockSpec((tm, d), row),
            pl.BlockSpec((tm, bw), row),
            pl.BlockSpec((tm, bw), row),
            pl.BlockSpec((tm, bw), row),
            pl.BlockSpec((1, d), const),
            pl.BlockSpec((d, N_BRANCH * d), const),
            pl.BlockSpec((N_BRANCH, bw, d), lambda i: (0, 0, 0)),
            pl.BlockSpec((d, d), const),
        ],
        out_specs=pl.BlockSpec((tm, d), row),
        out_shape=jax.ShapeDtypeStruct((n, d), F32),
        compiler_params=_cparams(("parallel",)),
        name="merge",
    )(x2d, br_a, br_b, br_c, g, w_gate, w_branch, w_out)


def _mlp_kernel(x_ref, g_ref, wu_ref, wd_ref, gf_ref, o_ref, *, ff_chunk, final_norm):
    x = x_ref[...]
    h = _rmsnorm(x, g_ref[...]).astype(MXU_DTYPE)
    acc = x
    for c in range(wu_ref.shape[1] // ff_chunk):
        u = jnp.maximum(_dot(h, wu_ref[:, c * ff_chunk:(c + 1) * ff_chunk]), 0.0)
        acc = acc + _dot((u * u).astype(MXU_DTYPE), wd_ref[c * ff_chunk:(c + 1) * ff_chunk, :])
    if final_norm:
        acc = _rmsnorm(acc, gf_ref[...])
    o_ref[...] = acc


def _mlp(x2d, g, w_up, w_down, g_final, tm, final_norm):
    n, d = x2d.shape
    ff = w_up.shape[1]
    row = lambda i: (i, 0)
    const = lambda i: (0, 0)
    return pl.pallas_call(
        functools.partial(_mlp_kernel, ff_chunk=min(ff, 1024), final_norm=final_norm),
        grid=(n // tm,),
        in_specs=[
            pl.BlockSpec((tm, d), row),
            pl.BlockSpec((1, d), const),
            pl.BlockSpec((d, ff), const),
            pl.BlockSpec((ff, d), const),
            pl.BlockSpec((1, d), const),
        ],
        out_specs=pl.BlockSpec((tm, d), row),
        out_shape=jax.ShapeDtypeStruct((n, d), F32),
        compiler_params=_cparams(("parallel",)),
        name="mlp",
    )(x2d, g, w_up, w_down, g_final)


def _w_in_columns(d_model):
    sizes = (("qa", BRANCH_WIDTH), ("ka", BRANCH_WIDTH), ("va", BRANCH_WIDTH),
             ("qb", BRANCH_WIDTH), ("kb", BRANCH_WIDTH), ("vb", BRANCH_WIDTH), ("fl", FOX_HEADS),
             ("qc", DSA_HEADS * HEAD_DIM), ("kc", DSA_KV_HEADS * HEAD_DIM), ("vc", DSA_KV_HEADS * HEAD_DIM),
             ("qi", IDX_HEADS * IDX_DIM), ("ki", IDX_DIM), ("wi", IDX_HEADS), ("gates", N_BRANCH * d_model))
    cols, o = {}, 0
    for name, width in sizes:
        cols[name] = (o, o + width)
        o += width
    return cols, o


def _pack_kernel(w_ref, main_ref, vals_ref, gates_ref):
    cols, _ = _w_in_columns(gates_ref.shape[2] // N_BRANCH)

    def src(name, lo=0, hi=None):
        a, b = cols[name]
        return w_ref[0, :, a + lo:(b if hi is None else a + hi)]

    def put(ref, start, value):
        ref[0, :, start:start + value.shape[1]] = value.astype(ref.dtype)

    put(main_ref, COL_QA, src("qa"))
    put(main_ref, COL_KA, src("ka"))
    put(main_ref, COL_QB, src("qb"))
    put(main_ref, COL_KB, src("kb"))
    put(main_ref, COL_QC, src("qc"))
    for g in range(DSA_KV_HEADS):
        kg = src("kc", g * HEAD_DIM, (g + 1) * HEAD_DIM)
        put(main_ref, COL_KC + g * LANES, jnp.concatenate([kg, kg], axis=1))
    put(main_ref, COL_QI, src("qi"))
    ki = src("ki")
    put(main_ref, COL_KI, jnp.concatenate([ki, ki], axis=1))
    put(vals_ref, ROW_VA, src("va"))
    put(vals_ref, ROW_VB, src("vb"))
    put(vals_ref, ROW_VC, src("vc"))
    pad = jnp.zeros((w_ref.shape[1], LANES - FOX_HEADS - IDX_HEADS), F32)
    put(vals_ref, VT_ROWS, jnp.concatenate([src("fl"), src("wi"), pad], axis=1))
    put(gates_ref, 0, src("gates"))


def _pack_w_in(w_in):
    depth, d, total = w_in.shape
    _, end = _w_in_columns(d)
    assert end == total
    rows = 256
    blk = lambda width: pl.BlockSpec((1, rows, width), lambda l, r: (l, r, 0))
    widths = (MAIN_COLS, VALS_COLS, N_BRANCH * d)
    return pl.pallas_call(
        _pack_kernel,
        grid=(depth, d // rows),
        in_specs=[blk(total)],
        out_specs=[blk(wd) for wd in widths],
        out_shape=[jax.ShapeDtypeStruct((depth, d, wd), MXU_DTYPE) for wd in widths],
        compiler_params=_cparams(("parallel", "parallel")),
        name="pack",
    )(w_in)


def _rope_rows():
    lane = jnp.arange(LANES) % HEAD_DIM
    half = ROT_DIM // 2
    inv_freq = ROPE_THETA ** (-jnp.arange(0, ROT_DIM, 2, dtype=F32) / ROT_DIM)
    freq = jnp.where(lane < ROT_DIM, inv_freq[lane % half], 0.0).astype(F32)
    sg1 = jnp.where(lane < half, -1.0, 0.0).astype(F32)
    sg2 = jnp.where((lane >= half) & (lane < ROT_DIM), 1.0, 0.0).astype(F32)
    return freq[None, :], sg1[None, :], sg2[None, :]


def kernel(x, positions, g_mix, w_in, b_forget, w_branch, w_out, g_mlp, w_up, w_down, g_final):
    b, s, d = x.shape
    n = b * s
    depth = w_in.shape[0]
    top_k = min(TOPK_MAX, s // 4)
    tm = min(512, s)
    freq, sg1, sg2 = _rope_rows()
    pos2d = positions.reshape(n, 1)
    x2d = x.reshape(n, d)
    w_main_all, w_v_all, w_gate_all = _pack_w_in(w_in)
    for layer in range(depth):
        w_main, w_v, w_gate = w_main_all[layer], w_v_all[layer], w_gate_all[layer]
        bias_row = jnp.concatenate([b_forget[layer], jnp.zeros((LANES - FOX_HEADS,), F32)])[None, :]
        main, vt, misc = _proj(x2d, pos2d, g_mix[layer][None, :], freq, sg1, sg2, w_main, w_v, tm, b, s)
        main3d = main.reshape(b, s, MAIN_COLS)
        misc3d = misc.reshape(b, s, LANES)
        cum = _cumf(misc3d, bias_row)
        br_a = _sb(main3d, vt).reshape(n, BRANCH_WIDTH)
        br_b = _fox(main3d, vt, cum).reshape(n, BRANCH_WIDTH)
        br_c = _dsa(main3d, vt, misc3d, top_k).reshape(n, BRANCH_WIDTH)
        x2d = _merge(x2d, br_a, br_b, br_c, g_mix[layer][None, :], w_gate,
                     w_branch[layer].astype(MXU_DTYPE), w_out[layer].astype(MXU_DTYPE), tm)
        x2d = _mlp(x2d, g_mlp[layer][None, :], w_up[layer].astype(MXU_DTYPE),
                   w_down[layer].astype(MXU_DTYPE), g_final[None, :], tm, layer == depth - 1)
    return x2d.reshape(b, s, d)
```

```python
import functools

import jax
import jax.numpy as jnp
from jax import lax
from jax.experimental import pallas as pl
from jax.experimental.pallas import tpu as pltpu

F32 = jnp.float32
MXU_DTYPE = jnp.bfloat16

HEAD_DIM = 64
SB_HEADS = 8
FOX_HEADS = 8
DSA_HEADS = 8
DSA_KV_HEADS = 2
DSA_GROUP = DSA_HEADS // DSA_KV_HEADS
IDX_HEADS = 4
IDX_DIM = 64
N_BRANCH = 3
ROPE_THETA = 500000.0
ROT_DIM = HEAD_DIM // 4
TOPK_MAX = 256
NORM_EPS = 1e-6

LANES = 128
SUBLANES = 8
SUM_ROWS = 2 * SUBLANES
BRANCH_WIDTH = SB_HEADS * HEAD_DIM
ATTN_SCALE = HEAD_DIM ** -0.5
assert IDX_DIM == HEAD_DIM
IDX_W_SCALE = IDX_HEADS ** -0.5

COL_QA, COL_KA = 0, 512
COL_QB, COL_KB = 1024, 1536
COL_QC = 2048
COL_KC = 2560
COL_QI = 2816
COL_KI = 3072
MAIN_COLS = 3200
ROPE_START = COL_QC
ROW_VA, ROW_VB, ROW_VC = 0, 512, 1024
VT_ROWS = 1152
VALS_COLS = VT_ROWS + LANES
MISC_FL, MISC_WI = 0, 8

TQ = 256
TK = 256
HEADS_PER_STEP = 8
NEG_BIG = -1e30

VMEM_LIMIT = 56 * 1024 * 1024


def _cparams(sem):
    return pltpu.CompilerParams(dimension_semantics=sem, vmem_limit_bytes=VMEM_LIMIT)


def _rmsnorm(x, g):
    y = x * lax.rsqrt(jnp.mean(x * x, axis=-1, keepdims=True) + NORM_EPS)
    return y * g


def _dot(a, b):
    return jnp.dot(a, b, preferred_element_type=F32)


def _dot_nt(a, b):
    return lax.dot_general(a, b, (((1,), (1,)), ((), ())), preferred_element_type=F32)


def _split2(x):
    hi = x.astype(MXU_DTYPE)
    lo = (x - hi.astype(F32)).astype(MXU_DTYPE)
    return hi, lo


def _split3(x):
    hi = x.astype(MXU_DTYPE)
    r = x - hi.astype(F32)
    mid = r.astype(MXU_DTYPE)
    lo = (r - mid.astype(F32)).astype(MXU_DTYPE)
    return hi, mid, lo


def _log_sigmoid_parts(z):
    neg_abs = lax.bitcast_convert_type(lax.bitcast_convert_type(z, jnp.int32) | jnp.int32(-2 ** 31), F32)
    sp = jnp.log(1.0 + jnp.exp(neg_abs))
    return jnp.minimum(z, 0.0) - sp, jnp.maximum(z, 0.0) + sp


def _proj_kernel(x_ref, pos_ref, g_ref, freq_ref, sg1_ref, sg2_ref, wm_ref, wv_ref,
                 main_ref, vt_ref, misc_ref):
    h = _rmsnorm(x_ref[...], g_ref[...]).astype(MXU_DTYPE)
    ang = pos_ref[...].astype(F32) * freq_ref[...]
    cos = jnp.cos(ang)
    sin = jnp.sin(ang)
    s_up = sin * sg1_ref[...]
    s_dn = sin * sg2_ref[...]
    half = ROT_DIM // 2
    chunk = 4 * LANES
    for c0 in range(0, VALS_COLS, 2 * LANES):
        pv = _dot(h, wv_ref[:, c0:c0 + 2 * LANES])
        for r0 in (c0, c0 + LANES):
            piece = pv[:, r0 - c0:r0 - c0 + LANES]
            if r0 < VT_ROWS:
                vt_ref[0, r0:r0 + LANES, :] = piece.T.astype(vt_ref.dtype)
            else:
                misc_ref[...] = piece
    for c0 in list(range(ROPE_START, MAIN_COLS, chunk)) + list(range(0, ROPE_START, chunk)):
        c1 = min(c0 + chunk, MAIN_COLS)
        p = _dot(h, wm_ref[:, c0:c1])
        if c0 >= ROPE_START:
            for t in range((c1 - c0) // LANES):
                pt = p[:, t * LANES:(t + 1) * LANES]
                pt = (pt * cos + pltpu.roll(pt, LANES - half, 1) * s_up
                      + pltpu.roll(pt, half, 1) * s_dn)
                main_ref[:, c0 + t * LANES:c0 + (t + 1) * LANES] = pt.astype(main_ref.dtype)
        else:
            main_ref[:, c0:c1] = p.astype(main_ref.dtype)


def _proj(x2d, pos2d, g, freq, sg1, sg2, w_main, w_v, tm, batch, seq):
    n, d = x2d.shape
    nsb = seq // tm
    const = lambda i: (0, 0)
    return pl.pallas_call(
        _proj_kernel,
        grid=(n // tm,),
        in_specs=[
            pl.BlockSpec((tm, d), lambda i: (i, 0)),
            pl.BlockSpec((tm, 1), lambda i: (i, 0)),
            pl.BlockSpec((1, d), const),
            pl.BlockSpec((1, LANES), const),
            pl.BlockSpec((1, LANES), const),
            pl.BlockSpec((1, LANES), const),
            pl.BlockSpec((d, MAIN_COLS), const),
            pl.BlockSpec((d, VALS_COLS), const),
        ],
        out_specs=[
            pl.BlockSpec((tm, MAIN_COLS), lambda i: (i, 0)),
            pl.BlockSpec((1, VT_ROWS, tm), lambda i: (i // nsb, 0, i % nsb)),
            pl.BlockSpec((tm, LANES), lambda i: (i, 0)),
        ],
        out_shape=[
            jax.ShapeDtypeStruct((n, MAIN_COLS), MXU_DTYPE),
            jax.ShapeDtypeStruct((batch, VT_ROWS, seq), MXU_DTYPE),
            jax.ShapeDtypeStruct((n, LANES), F32),
        ],
        compiler_params=_cparams(("parallel",)),
        name="proj",
    )(x2d, pos2d, g, freq, sg1, sg2, w_main, w_v)


def _cumf_kernel(misc_ref, bias_ref, c_ref):
    s = misc_ref.shape[1]
    r = lax.broadcasted_iota(jnp.int32, (TK, TK), 0)
    c = lax.broadcasted_iota(jnp.int32, (TK, TK), 1)
    incl = (c <= r).astype(MXU_DTYPE)
    carry = jnp.zeros((1, LANES), F32)
    for b in range(s // TK):
        logit = misc_ref[0, b * TK:(b + 1) * TK, :] + bias_ref[...]
        logf, _ = _log_sigmoid_parts(logit)
        hi, mid, lo = _split3(logf)
        cs = (_dot(incl, lo) + _dot(incl, mid)) + _dot(incl, hi) + carry
        c_ref[0, b * TK:(b + 1) * TK, :] = cs
        carry = cs[TK - 1:TK, :]


def _cumf(misc3d, bias_row):
    b, s, _ = misc3d.shape
    return pl.pallas_call(
        _cumf_kernel,
        grid=(b,),
        in_specs=[
            pl.BlockSpec((1, s, LANES), lambda i: (i, 0, 0)),
            pl.BlockSpec((1, LANES), lambda i: (0, 0)),
        ],
        out_specs=pl.BlockSpec((1, s, LANES), lambda i: (i, 0, 0)),
        out_shape=jax.ShapeDtypeStruct((b, s, LANES), F32),
        compiler_params=_cparams(("parallel",)),
        name="cumf",
    )(misc3d, bias_row)


def _head_q(q_tile, half):
    lane = lax.broadcasted_iota(jnp.int32, q_tile.shape, 1)
    keep = (lane // HEAD_DIM) == half
    return jnp.where(keep, q_tile.astype(F32) * ATTN_SCALE, 0.0).astype(MXU_DTYPE)


def _causal_tile_t(strict):
    key = lax.broadcasted_iota(jnp.int32, (TK, TQ), 0)
    qry = lax.broadcasted_iota(jnp.int32, (TK, TQ), 1)
    return (key < qry) if strict else (key <= qry)


def _rows(j):
    return pl.ds(pl.multiple_of(j * TK, TK), TK)


def _lookahead_loop(n_last, bufs, issue, step, carry, finish, primed=False):
    buf_a, buf_b = bufs

    def pair(t, carry):
        n = 2 * t
        issue(n + 1, buf_b)
        carry = step(n, buf_a, carry, False)
        issue(n + 2, buf_a)
        return step(n + 1, buf_b, carry, False)

    def odd_tail(carry):
        issue(n_last, buf_b)
        carry = step(n_last - 1, buf_a, carry, False)
        step(n_last, buf_b, carry, True)
        finish()
        return 0

    def even_tail(carry):
        step(n_last, buf_a, carry, True)
        finish()
        return 0

    if not primed:
        issue(0, buf_a)
    carry = lax.fori_loop(0, n_last // 2, pair, carry)
    lax.cond(n_last % 2 == 1, odd_tail, even_tail, carry)


def _flash_heads(heads, n_full, diag_mask, z_bufs, acc_ref, o_ref):
    half_k, half_q = TK // 2, TQ // 2

    def add_bias(s, bias):
        if isinstance(bias, (list, tuple)):
            return jnp.concatenate([s[:, t * LANES:(t + 1) * LANES] + bias[t] for t in range(len(bias))],
                                   axis=1)
        return s + bias

    def later_queries(bias):
        return bias[half_q // LANES:] if isinstance(bias, (list, tuple)) else bias[:, half_q:]

    ones = jnp.ones((SUM_ROWS, TK), MXU_DTYPE)

    def issue(j, buf):
        for h, (qh, k_at, _, _) in enumerate(heads):
            buf[h] = _dot_nt(k_at(_rows(j)), qh)

    def update(tiles, rows, maxes, lanes):
        stats = []
        for s, m in zip(tiles, maxes):
            m_new = jnp.maximum(m, jnp.max(s, axis=0, keepdims=True))
            stats.append((m_new, jnp.exp(m - m_new), jnp.exp(s - m_new).astype(MXU_DTYPE)))
        for h, ((_, alpha, p), (_, _, vt_at, _)) in enumerate(zip(stats, heads)):
            vt1 = jnp.concatenate([vt_at(rows), ones[:, :p.shape[0]]], axis=0)
            acc_ref[h, :, lanes] = alpha * acc_ref[h, :, lanes] + _dot(vt1, p)
        return tuple(m_new for m_new, _, _ in stats)

    def step(j, buf, maxes):
        rows = _rows(j)
        tiles = [add_bias(buf[h], bias_at(rows)) for h, (_, _, _, bias_at) in enumerate(heads)]
        return update(tiles, rows, maxes, slice(None))

    def diagonal_step(j, buf, maxes):
        start = pl.multiple_of(j * TK, TK)
        early, late = pl.ds(start, half_k), pl.ds(pl.multiple_of(start + half_k, half_k), half_k)

        def masked(s, mask):
            return s if diag_mask is None else jnp.where(mask, s, NEG_BIG)

        tiles = [masked(add_bias(buf[h, :half_k, :], bias_at(early)), None if diag_mask is None
                        else diag_mask[:half_k, :]) for h, (_, _, _, bias_at) in enumerate(heads)]
        maxes = update(tiles, early, maxes, slice(None))
        tiles = [masked(add_bias(buf[h, half_k:, half_q:], later_queries(bias_at(late))), None if diag_mask is None
                        else diag_mask[half_k:, half_q:]) for h, (_, _, _, bias_at) in enumerate(heads)]
        update(tiles, late, tuple(m[:, half_q:] for m in maxes), slice(half_q, TQ))
        return maxes

    def finish():
        _store_heads_t(o_ref, [acc_ref[h, :HEAD_DIM, :] / acc_ref[h, HEAD_DIM:HEAD_DIM + 1, :]
                               for h in range(len(heads))])

    acc_ref[...] = jnp.zeros(acc_ref.shape, F32)
    init = tuple(jnp.full((1, TQ), NEG_BIG, F32) for _ in heads)
    _lookahead_loop(n_full, z_bufs, issue,
                    lambda j, buf, maxes, last: (diagonal_step if last else step)(j, buf, maxes), init, finish)


def _store_heads_t(o_ref, outs_t):
    for t in range(len(outs_t) // 2):
        pair_t = jnp.concatenate([outs_t[2 * t], outs_t[2 * t + 1]], axis=0)
        o_ref[0, :, t * LANES:(t + 1) * LANES] = pair_t.T.astype(o_ref.dtype)


def _sb_kernel(q_ref, k_ref, vt_ref, o_ref, za_ref, zb_ref, acc_ref):
    i = pl.program_id(2)
    r = lax.broadcasted_iota(jnp.int32, (TK, TK), 0)
    c = lax.broadcasted_iota(jnp.int32, (TK, TK), 1)
    after = (c > r).astype(MXU_DTYPE)
    strict = _causal_tile_t(True)
    n_heads = HEADS_PER_STEP
    qhs = [_head_q(q_ref[0][:, (h // 2) * LANES:(h // 2 + 1) * LANES], h % 2) for h in range(n_heads)]

    def issue(j, buf):
        for h in range(n_heads):
            buf[h] = _dot_nt(k_ref[0, _rows(j), (h // 2) * LANES:(h // 2 + 1) * LANES], qhs[h])

    def update(tiles, rows, tails, lanes, mask):
        n_rows = tiles[0].shape[0]
        parts = []
        for z in tiles:
            log_beta, neg_l1m = _log_sigmoid_parts(z)
            if mask is not None:
                neg_l1m = jnp.where(mask, neg_l1m, 0.0)
            hi, lo = _split2(neg_l1m)
            parts.append((log_beta, neg_l1m[0:1, :], jnp.concatenate([hi, lo], axis=0)))
        after_rows = jnp.concatenate([after[:n_rows, :n_rows]] * 2, axis=1)
        laters = [_dot(after_rows, hl) for (_, _, hl) in parts]
        ws = []
        for (log_beta, _, _), later in zip(parts, laters):
            w = jnp.exp(log_beta - later)
            if mask is not None:
                w = jnp.where(mask, w, 0.0)
            ws.append(w.astype(MXU_DTYPE))
        out = []
        for h, (w, (_, first, _), later, tail) in enumerate(zip(ws, parts, laters, tails)):
            vtj = vt_ref[0, h * HEAD_DIM:(h + 1) * HEAD_DIM, rows]
            acc_ref[h, :, lanes] = acc_ref[h, :, lanes] + jnp.exp(-tail) * _dot(vtj, w)
            out.append(tail + (first + later[0:1, :]))
        return tuple(out)

    def step(j, buf, tails):
        return update([buf[h] for h in range(n_heads)], _rows(j), tails, slice(None), None)

    def diagonal_step(j, buf, tails):
        half_k, half_q = TK // 2, TQ // 2
        start = pl.multiple_of(j * TK, TK)
        early, late = pl.ds(start, half_k), pl.ds(pl.multiple_of(start + half_k, half_k), half_k)
        late_tails = update([buf[h, half_k:, half_q:] for h in range(n_heads)], late,
                            tuple(t[:, half_q:] for t in tails), slice(half_q, TQ), strict[half_k:, half_q:])
        tails = tuple(jnp.concatenate([t[:, :half_q], lt], axis=1) for t, lt in zip(tails, late_tails))
        return update([buf[h, :half_k, :] for h in range(n_heads)], early, tails, slice(None), strict[:half_k, :])

    acc_ref[...] = jnp.zeros(acc_ref.shape, F32)
    carry = tuple(jnp.zeros((1, TQ), F32) for _ in range(n_heads))
    issue(i, za_ref)

    def finish():
        _store_heads_t(o_ref, [acc_ref[h] for h in range(n_heads)])

    def with_earlier_blocks(cr):
        issue(i - 1, zb_ref)
        cr = diagonal_step(i, za_ref, cr)
        _lookahead_loop(i - 1, (zb_ref, za_ref), lambda n, buf: issue(i - 1 - n, buf),
                        lambda n, buf, c, last: step(i - 1 - n, buf, c), cr, finish, primed=True)
        return 0

    def diagonal_only(cr):
        diagonal_step(i, za_ref, cr)
        finish()
        return 0

    lax.cond(i > 0, with_earlier_blocks, diagonal_only, carry)


def _sb(main3d, vt3d):
    b, s, _ = main3d.shape
    nq = s // TQ
    w = HEADS_PER_STEP * HEAD_DIM
    qb, kb, vb = COL_QA // w, COL_KA // w, ROW_VA // w
    return pl.pallas_call(
        _sb_kernel,
        grid=(b, SB_HEADS // HEADS_PER_STEP, nq),
        in_specs=[
            pl.BlockSpec((1, TQ, w), lambda bi, hg, i: (bi, i, qb + hg)),
            pl.BlockSpec((1, s, w), lambda bi, hg, i: (bi, 0, kb + hg)),
            pl.BlockSpec((1, w, s), lambda bi, hg, i: (bi, vb + hg, 0)),
        ],
        out_specs=pl.BlockSpec((1, TQ, w), lambda bi, hg, i: (bi, i, hg)),
        out_shape=jax.ShapeDtypeStruct((b, s, BRANCH_WIDTH), MXU_DTYPE),
        scratch_shapes=[pltpu.VMEM((HEADS_PER_STEP, TK, TQ), F32), pltpu.VMEM((HEADS_PER_STEP, TK, TQ), F32),
                        pltpu.VMEM((HEADS_PER_STEP, HEAD_DIM, TQ), F32)],
        compiler_params=_cparams(("parallel", "parallel", "arbitrary")),
        name="sb",
    )(main3d, main3d, vt3d)


def _fox_kernel(q_ref, k_ref, vt_ref, c_ref, o_ref, cb_ref, za_ref, zb_ref, acc_ref):
    hg = pl.program_id(1)
    i = pl.program_id(2)
    n_heads = HEADS_PER_STEP

    @pl.when(i == 0)
    def _():
        c = c_ref[0]
        lane = lax.broadcasted_iota(jnp.int32, c.shape, 1)
        for h in range(n_heads):
            if n_heads == FOX_HEADS:
                col = c[:, h:h + 1]
            else:
                col = jnp.sum(jnp.where(lane == hg * n_heads + h, c, 0.0), axis=1, keepdims=True)
            cb_ref[h] = jnp.broadcast_to(-col, c.shape)

    heads = []
    for h in range(n_heads):
        t = h // 2
        qh = _head_q(q_ref[0][:, t * LANES:(t + 1) * LANES], h % 2)
        k_at = lambda rows, t=t: k_ref[0, rows, t * LANES:(t + 1) * LANES]
        vt_at = lambda rows, h=h: vt_ref[0, h * HEAD_DIM:(h + 1) * HEAD_DIM, rows]
        bias_at = lambda rows, h=h: [cb_ref[h, rows, :]] * (TQ // LANES)
        heads.append((qh, k_at, vt_at, bias_at))
    _flash_heads(heads, i, _causal_tile_t(False), (za_ref, zb_ref), acc_ref, o_ref)


def _fox(main3d, vt3d, c3d):
    b, s, _ = main3d.shape
    nq = s // TQ
    w = HEADS_PER_STEP * HEAD_DIM
    qb, kb, vb = COL_QB // w, COL_KB // w, ROW_VB // w
    return pl.pallas_call(
        _fox_kernel,
        grid=(b, FOX_HEADS // HEADS_PER_STEP, nq),
        in_specs=[
            pl.BlockSpec((1, TQ, w), lambda bi, hg, i: (bi, i, qb + hg)),
            pl.BlockSpec((1, s, w), lambda bi, hg, i: (bi, 0, kb + hg)),
            pl.BlockSpec((1, w, s), lambda bi, hg, i: (bi, vb + hg, 0)),
            pl.BlockSpec((1, s, LANES), lambda bi, hg, i: (bi, 0, 0)),
        ],
        out_specs=pl.BlockSpec((1, TQ, w), lambda bi, hg, i: (bi, i, hg)),
        out_shape=jax.ShapeDtypeStruct((b, s, BRANCH_WIDTH), MXU_DTYPE),
        scratch_shapes=[pltpu.VMEM((HEADS_PER_STEP, s, LANES), F32),
                        pltpu.VMEM((HEADS_PER_STEP, TK, TQ), F32), pltpu.VMEM((HEADS_PER_STEP, TK, TQ), F32),
                        pltpu.VMEM((HEADS_PER_STEP, HEAD_DIM + SUM_ROWS, TQ), F32)],
        compiler_params=_cparams(("parallel", "parallel", "arbitrary")),
        name="fox",
    )(main3d, main3d, vt3d, c3d)


def _dsa_kernel(q_ref, k_ref, vt_ref, qi_ref, ki_ref, misc_ref, o_ref, key_ref, dig_ref, bias_ref,
                za_ref, zb_ref, acc_ref, *, top_k):
    i = pl.program_id(1)
    diag = _causal_tile_t(False)

    wi_t = misc_ref[0].T[MISC_WI:MISC_WI + IDX_HEADS, :] * IDX_W_SCALE
    qih = [_head_q(qi_ref[0][:, (h // 2) * LANES:(h // 2 + 1) * LANES], h % 2) for h in range(IDX_HEADS)]

    def score_block(j, mask):
        kij = ki_ref[0, _rows(j), :]
        sc = None
        for h in range(IDX_HEADS):
            term = wi_t[h:h + 1, :] * jnp.maximum(_dot_nt(kij, qih[h]), 0.0)
            sc = term if sc is None else sc + term
        if mask is not None:
            sc = jnp.where(mask, sc, -jnp.inf)
        bits = lax.bitcast_convert_type(jnp.where(sc == 0.0, 0.0, sc), jnp.int32)
        key_ref[_rows(j), :] = bits ^ ((bits >> 31) & jnp.int32(0x7FFFFFFF))

    def score_pair(t, _):
        score_block(2 * t, None)
        score_block(2 * t + 1, None)
        return 0

    def score_odd_tail():
        score_block(i - 1, None)
        score_block(i, diag)
        return 0

    def score_even_tail():
        score_block(i, diag)
        return 0

    lax.fori_loop(0, i // 2, score_pair, 0)
    lax.cond(i % 2 == 1, score_odd_tail, score_even_tail)

    small = MXU_DTYPE
    group = 2 * SUBLANES

    assert (key_ref.shape[0] // group) <= 256

    def count_digits_ge(cand):
        def hits(j):
            hit = jnp.where(dig_ref[_rows(j), :] >= cand, jnp.ones((), small), jnp.zeros((), small))
            part = hit[0:group]
            for g in range(1, TK // group):
                part = part + hit[g * group:(g + 1) * group]
            return part

        acc = lax.fori_loop(0, (i + 1) // 2, lambda t, a: a + (hits(2 * t) + hits(2 * t + 1)),
                            jnp.zeros((group, TQ), small))
        acc = lax.cond((i + 1) % 2 == 1, lambda a: a + hits(i), lambda a: a, acc)
        return jnp.sum(acc.astype(F32), axis=0, keepdims=True)

    above = jnp.zeros((1, TQ), F32)
    prefix = None
    for shift in (24, 16, 8, 0):
        def fill(j, _, shift=shift, prefix=prefix):
            key = key_ref[_rows(j), :]
            if prefix is None:
                digit = (key >> shift) + 128
            else:
                digit = (key ^ (prefix << (shift + 8))) >> shift
                shares = lax.bitcast_convert_type(digit, jnp.uint32) < jnp.uint32(256)
                digit = jnp.where(shares, digit, -1)
            dig_ref[_rows(j), :] = digit.astype(F32).astype(small)
            return 0

        lax.fori_loop(0, i + 1, fill, 0)

        def search(_, st, above=above):
            lo, hi, n_hi = st
            mid = (lo + hi) * 0.5
            n_mid = count_digits_ge(mid.astype(small))
            ok = above + n_mid >= top_k
            return jnp.where(ok, mid, lo), jnp.where(ok, hi, mid), jnp.where(ok, n_hi, n_mid)

        start = (jnp.zeros((1, TQ), F32), jnp.full((1, TQ), 256.0, F32), jnp.zeros((1, TQ), F32))
        digit, _, n_hi = lax.fori_loop(0, 8, search, start)
        above = above + n_hi
        digit = digit.astype(jnp.int32)
        prefix = digit - 128 if prefix is None else (prefix << 8) | digit
    thr_key = prefix

    need = top_k - above
    kr = lax.broadcasted_iota(jnp.int32, (TK, TK), 0)
    kc = lax.broadcasted_iota(jnp.int32, (TK, TK), 1)
    upto = (kc <= kr).astype(MXU_DTYPE)

    def bias_block(j, mask, seen):
        key = key_ref[_rows(j), :]
        tie = key == thr_key
        rank = _dot(upto, jnp.where(tie, 1.0, 0.0).astype(MXU_DTYPE)) + seen
        sel = (key > thr_key) | (tie & (rank <= need))
        if mask is not None:
            sel = sel & mask
        bias_ref[_rows(j), :] = jnp.where(sel, 0.0, NEG_BIG)
        return rank[TK - 1:TK, :]

    def bias_pair(t, seen):
        return bias_block(2 * t + 1, None, bias_block(2 * t, None, seen))

    def bias_odd_tail(seen):
        bias_block(i, diag, bias_block(i - 1, None, seen))
        return 0

    def bias_even_tail(seen):
        bias_block(i, diag, seen)
        return 0

    seen = lax.fori_loop(0, i // 2, bias_pair, jnp.zeros((1, TQ), F32))
    lax.cond(i % 2 == 1, bias_odd_tail, bias_even_tail, seen)

    bias_at = lambda rows: bias_ref[rows, :]
    heads = []
    for h in range(DSA_HEADS):
        g = h // DSA_GROUP
        k_at = lambda rows, g=g: k_ref[0, rows, g * LANES:(g + 1) * LANES]
        vt_at = lambda rows, g=g: vt_ref[0, g * HEAD_DIM:(g + 1) * HEAD_DIM, rows]
        qh = _head_q(q_ref[0][:, (h // 2) * LANES:(h // 2 + 1) * LANES], h % 2)
        heads.append((qh, k_at, vt_at, bias_at))
    _flash_heads(heads, i, None, (za_ref, zb_ref), acc_ref, o_ref)


def _dsa(main3d, vt3d, misc3d, top_k):
    b, s, _ = main3d.shape
    nq = s // TQ
    assert top_k <= TK, "the first key block must hold at least top_k keys"
    return pl.pallas_call(
        functools.partial(_dsa_kernel, top_k=top_k),
        grid=(b, nq),
        in_specs=[
            pl.BlockSpec((1, TQ, 4 * LANES), lambda bi, i: (bi, i, COL_QC // (4 * LANES))),
            pl.BlockSpec((1, s, 2 * LANES), lambda bi, i: (bi, 0, COL_KC // (2 * LANES))),
            pl.BlockSpec((1, LANES, s), lambda bi, i: (bi, ROW_VC // LANES, 0)),
            pl.BlockSpec((1, TQ, 2 * LANES), lambda bi, i: (bi, i, COL_QI // (2 * LANES))),
            pl.BlockSpec((1, s, LANES), lambda bi, i: (bi, 0, COL_KI // LANES)),
            pl.BlockSpec((1, TQ, LANES), lambda bi, i: (bi, i, 0)),
        ],
        out_specs=pl.BlockSpec((1, TQ, BRANCH_WIDTH), lambda bi, i: (bi, i, 0)),
        out_shape=jax.ShapeDtypeStruct((b, s, BRANCH_WIDTH), MXU_DTYPE),
        scratch_shapes=[pltpu.VMEM((s, TQ), jnp.int32), pltpu.VMEM((s, TQ), MXU_DTYPE), pltpu.VMEM((s, TQ), F32),
                        pltpu.VMEM((DSA_HEADS, TK, TQ), F32), pltpu.VMEM((DSA_HEADS, TK, TQ), F32),
                        pltpu.VMEM((DSA_HEADS, HEAD_DIM + SUM_ROWS, TQ), F32)],
        compiler_params=_cparams(("parallel", "arbitrary")),
        name="dsa",
    )(main3d, main3d, vt3d, main3d, main3d, misc3d)


def _merge_kernel(x_ref, a_ref, b_ref, c_ref, g_ref, wg_ref, wb_ref, wo_ref, o_ref):
    x = x_ref[...]
    d = x.shape[1]
    h = _rmsnorm(x, g_ref[...]).astype(MXU_DTYPE)
    y = None
    for n, br_ref in enumerate((a_ref, b_ref, c_ref)):
        gate = jax.nn.sigmoid(_dot(h, wg_ref[:, n * d:(n + 1) * d]))
        term = gate * _dot(br_ref[...], wb_ref[n])
        y = term if y is None else y + term
    o_ref[...] = x + _dot(y.astype(MXU_DTYPE), wo_ref[...])


def _merge(x2d, br_a, br_b, br_c, g, w_gate, w_branch, w_out, tm):
    n, d = x2d.shape
    bw = br_a.shape[1]
    row = lambda i: (i, 0)
    const = lambda i: (0, 0)
    return pl.pallas_call(
        _merge_kernel,
        grid=(n // tm,),
        in_specs=[
            pl.BlockSpec((tm, d), row),
            pl.BlockSpec((tm, bw), row),
            pl.BlockSpec((tm, bw), row),
            pl.BlockSpec((tm, bw), row),
            pl.BlockSpec((1, d), const),
            pl.BlockSpec((d, N_BRANCH * d), const),
            pl.BlockSpec((N_BRANCH, bw, d), lambda i: (0, 0, 0)),
            pl.BlockSpec((d, d), const),
        ],
        out_specs=pl.BlockSpec((tm, d), row),
        out_shape=jax.ShapeDtypeStruct((n, d), F32),
        compiler_params=_cparams(("parallel",)),
        name="merge",
    )(x2d, br_a, br_b, br_c, g, w_gate, w_branch, w_out)


def _mlp_kernel(x_ref, g_ref, wu_ref, wd_ref, gf_ref, o_ref, *, ff_chunk, final_norm):
    x = x_ref[...]
    h = _rmsnorm(x, g_ref[...]).astype(MXU_DTYPE)
    acc = x
    for c in range(wu_ref.shape[1] // ff_chunk):
        u = jnp.maximum(_dot(h, wu_ref[:, c * ff_chunk:(c + 1) * ff_chunk]), 0.0)
        acc = acc + _dot((u * u).astype(MXU_DTYPE), wd_ref[c * ff_chunk:(c + 1) * ff_chunk, :])
    if final_norm:
        acc = _rmsnorm(acc, gf_ref[...])
    o_ref[...] = acc


def _mlp(x2d, g, w_up, w_down, g_final, tm, final_norm):
    n, d = x2d.shape
    ff = w_up.shape[1]
    row = lambda i: (i, 0)
    const = lambda i: (0, 0)
    return pl.pallas_call(
        functools.partial(_mlp_kernel, ff_chunk=min(ff, 1024), final_norm=final_norm),
        grid=(n // tm,),
        in_specs=[
            pl.BlockSpec((tm, d), row),
            pl.BlockSpec((1, d), const),
            pl.BlockSpec((d, ff), const),
            pl.BlockSpec((ff, d), const),
            pl.BlockSpec((1, d), const),
        ],
        out_specs=pl.BlockSpec((tm, d), row),
        out_shape=jax.ShapeDtypeStruct((n, d), F32),
        compiler_params=_cparams(("parallel",)),
        name="mlp",
    )(x2d, g, w_up, w_down, g_final)


def _w_in_columns(d_model):
    sizes = (("qa", BRANCH_WIDTH), ("ka", BRANCH_WIDTH), ("va", BRANCH_WIDTH),
             ("qb", BRANCH_WIDTH), ("kb", BRANCH_WIDTH), ("vb", BRANCH_WIDTH), ("fl", FOX_HEADS),
             ("qc", DSA_HEADS * HEAD_DIM), ("kc", DSA_KV_HEADS * HEAD_DIM), ("vc", DSA_KV_HEADS * HEAD_DIM),
             ("qi", IDX_HEADS * IDX_DIM), ("ki", IDX_DIM), ("wi", IDX_HEADS), ("gates", N_BRANCH * d_model))
    cols, o = {}, 0
    for name, width in sizes:
        cols[name] = (o, o + width)
        o += width
    return cols, o


def _pack_kernel(w_ref, main_ref, vals_ref, gates_ref):
    cols, _ = _w_in_columns(gates_ref.shape[2] // N_BRANCH)

    def src(name, lo=0, hi=None):
        a, b = cols[name]
        return w_ref[0, :, a + lo:(b if hi is None else a + hi)]

    def put(ref, start, value):
        ref[0, :, start:start + value.shape[1]] = value.astype(ref.dtype)

    put(main_ref, COL_QA, src("qa"))
    put(main_ref, COL_KA, src("ka"))
    put(main_ref, COL_QB, src("qb"))
    put(main_ref, COL_KB, src("kb"))
    put(main_ref, COL_QC, src("qc"))
    for g in range(DSA_KV_HEADS):
        kg = src("kc", g * HEAD_DIM, (g + 1) * HEAD_DIM)
        put(main_ref, COL_KC + g * LANES, jnp.concatenate([kg, kg], axis=1))
    put(main_ref, COL_QI, src("qi"))
    ki = src("ki")
    put(main_ref, COL_KI, jnp.concatenate([ki, ki], axis=1))
    put(vals_ref, ROW_VA, src("va"))
    put(vals_ref, ROW_VB, src("vb"))
    put(vals_ref, ROW_VC, src("vc"))
    pad = jnp.zeros((w_ref.shape[1], LANES - FOX_HEADS - IDX_HEADS), F32)
    put(vals_ref, VT_ROWS, jnp.concatenate([src("fl"), src("wi"), pad], axis=1))
    put(gates_ref, 0, src("gates"))


def _pack_w_in(w_in):
    depth, d, total = w_in.shape
    _, end = _w_in_columns(d)
    assert end == total
    rows = 256
    blk = lambda width: pl.BlockSpec((1, rows, width), lambda l, r: (l, r, 0))
    widths = (MAIN_COLS, VALS_COLS, N_BRANCH * d)
    return pl.pallas_call(
        _pack_kernel,
        grid=(depth, d // rows),
        in_specs=[blk(total)],
        out_specs=[blk(wd) for wd in widths],
        out_shape=[jax.ShapeDtypeStruct((depth, d, wd), MXU_DTYPE) for wd in widths],
        compiler_params=_cparams(("parallel", "parallel")),
        name="pack",
    )(w_in)


def _rope_rows():
    lane = jnp.arange(LANES) % HEAD_DIM
    half = ROT_DIM // 2
    inv_freq = ROPE_THETA ** (-jnp.arange(0, ROT_DIM, 2, dtype=F32) / ROT_DIM)
    freq = jnp.where(lane < ROT_DIM, inv_freq[lane % half], 0.0).astype(F32)
    sg1 = jnp.where(lane < half, -1.0, 0.0).astype(F32)
    sg2 = jnp.where((lane >= half) & (lane < ROT_DIM), 1.0, 0.0).astype(F32)
    return freq[None, :], sg1[None, :], sg2[None, :]


def kernel(x, positions, g_mix, w_in, b_forget, w_branch, w_out, g_mlp, w_up, w_down, g_final):
    b, s, d = x.shape
    n = b * s
    depth = w_in.shape[0]
    top_k = min(TOPK_MAX, s // 4)
    tm = min(512, s)
    freq, sg1, sg2 = _rope_rows()
    pos2d = positions.reshape(n, 1)
    x2d = x.reshape(n, d)
    w_main_all, w_v_all, w_gate_all = _pack_w_in(w_in)
    for layer in range(depth):
        w_main, w_v, w_gate = w_main_all[layer], w_v_all[layer], w_gate_all[layer]
        bias_row = jnp.concatenate([b_forget[layer], jnp.zeros((LANES - FOX_HEADS,), F32)])[None, :]
        main, vt, misc = _proj(x2d, pos2d, g_mix[layer][None, :], freq, sg1, sg2, w_main, w_v, tm, b, s)
        main3d = main.reshape(b, s, MAIN_COLS)
        misc3d = misc.reshape(b, s, LANES)
        cum = _cumf(misc3d, bias_row)
        br_a = _sb(main3d, vt).reshape(n, BRANCH_WIDTH)
        br_b = _fox(main3d, vt, cum).reshape(n, BRANCH_WIDTH)
        br_c = _dsa(main3d, vt, misc3d, top_k).reshape(n, BRANCH_WIDTH)
        x2d = _merge(x2d, br_a, br_b, br_c, g_mix[layer][None, :], w_gate,
                     w_branch[layer].astype(MXU_DTYPE), w_out[layer].astype(MXU_DTYPE), tm)
        x2d = _mlp(x2d, g_mlp[layer][None, :], w_up[layer].astype(MXU_DTYPE),
                   w_down[layer].astype(MXU_DTYPE), g_final[None, :], tm, layer == depth - 1)
    return x2d.reshape(b, s, d)
```

```python
import functools

import jax
import jax.numpy as jnp
from jax import lax
from jax.experimental import pallas as pl
from jax.experimental.pallas import tpu as pltpu

F32 = jnp.float32
MXU_DTYPE = jnp.bfloat16

HEAD_DIM = 64
SB_HEADS = 8
FOX_HEADS = 8
DSA_HEADS = 8
DSA_KV_HEADS = 2
DSA_GROUP = DSA_HEADS // DSA_KV_HEADS
IDX_HEADS = 4
IDX_DIM = 64
N_BRANCH = 3
ROPE_THETA = 500000.0
ROT_DIM = HEAD_DIM // 4
TOPK_MAX = 256
NORM_EPS = 1e-6

LANES = 128
SUBLANES = 8
SUM_ROWS = 2 * SUBLANES
BRANCH_WIDTH = SB_HEADS * HEAD_DIM
ATTN_SCALE = HEAD_DIM ** -0.5
assert IDX_DIM == HEAD_DIM
IDX_W_SCALE = IDX_HEADS ** -0.5

COL_QA, COL_KA = 0, 512
COL_QB, COL_KB = 1024, 1536
COL_QC = 2048
COL_KC = 2560
COL_QI = 2816
COL_KI = 3072
MAIN_COLS = 3200
ROPE_START = COL_QC
ROW_VA, ROW_VB, ROW_VC = 0, 512, 1024
VT_ROWS = 1152
VALS_COLS = VT_ROWS + LANES
MISC_FL, MISC_WI = 0, 8

TQ = 256
TK = 256
HEADS_PER_STEP = 8
NEG_BIG = -1e30

VMEM_LIMIT = 56 * 1024 * 1024


def _cparams(sem):
    return pltpu.CompilerParams(dimension_semantics=sem, vmem_limit_bytes=VMEM_LIMIT)


def _rmsnorm(x, g):
    y = x * lax.rsqrt(jnp.mean(x * x, axis=-1, keepdims=True) + NORM_EPS)
    return y * g


def _dot(a, b):
    return jnp.dot(a, b, preferred_element_type=F32)


def _dot_nt(a, b):
    return lax.dot_general(a, b, (((1,), (1,)), ((), ())), preferred_element_type=F32)


def _split2(x):
    hi = x.astype(MXU_DTYPE)
    lo = (x - hi.astype(F32)).astype(MXU_DTYPE)
    return hi, lo


def _split3(x):
    hi = x.astype(MXU_DTYPE)
    r = x - hi.astype(F32)
    mid = r.astype(MXU_DTYPE)
    lo = (r - mid.astype(F32)).astype(MXU_DTYPE)
    return hi, mid, lo


def _log_sigmoid_parts(z):
    neg_abs = lax.bitcast_convert_type(lax.bitcast_convert_type(z, jnp.int32) | jnp.int32(-2 ** 31), F32)
    sp = jnp.log(1.0 + jnp.exp(neg_abs))
    log_sig = jnp.minimum(z, 0.0) - sp
    return log_sig, z - log_sig


def _proj_kernel(x_ref, pos_ref, g_ref, freq_ref, sg1_ref, sg2_ref, wm_ref, wv_ref,
                 main_ref, vt_ref, misc_ref):
    h = _rmsnorm(x_ref[...], g_ref[...]).astype(MXU_DTYPE)
    ang = pos_ref[...].astype(F32) * freq_ref[...]
    cos = jnp.cos(ang)
    sin = jnp.sin(ang)
    s_up = sin * sg1_ref[...]
    s_dn = sin * sg2_ref[...]
    half = ROT_DIM // 2
    chunk = 4 * LANES
    for c0 in range(0, VALS_COLS, 2 * LANES):
        pv = _dot(h, wv_ref[:, c0:c0 + 2 * LANES])
        for r0 in (c0, c0 + LANES):
            piece = pv[:, r0 - c0:r0 - c0 + LANES]
            if r0 < VT_ROWS:
                vt_ref[0, r0:r0 + LANES, :] = piece.T.astype(vt_ref.dtype)
            else:
                misc_ref[...] = piece
    for c0 in list(range(ROPE_START, MAIN_COLS, chunk)) + list(range(0, ROPE_START, chunk)):
        c1 = min(c0 + chunk, MAIN_COLS)
        p = _dot(h, wm_ref[:, c0:c1])
        if c0 >= ROPE_START:
            for t in range((c1 - c0) // LANES):
                pt = p[:, t * LANES:(t + 1) * LANES]
                pt = (pt * cos + pltpu.roll(pt, LANES - half, 1) * s_up
                      + pltpu.roll(pt, half, 1) * s_dn)
                main_ref[:, c0 + t * LANES:c0 + (t + 1) * LANES] = pt.astype(main_ref.dtype)
        else:
            main_ref[:, c0:c1] = p.astype(main_ref.dtype)


def _proj(x2d, pos2d, g, freq, sg1, sg2, w_main, w_v, tm, batch, seq):
    n, d = x2d.shape
    nsb = seq // tm
    const = lambda i: (0, 0)
    return pl.pallas_call(
        _proj_kernel,
        grid=(n // tm,),
        in_specs=[
            pl.BlockSpec((tm, d), lambda i: (i, 0)),
            pl.BlockSpec((tm, 1), lambda i: (i, 0)),
            pl.BlockSpec((1, d), const),
            pl.BlockSpec((1, LANES), const),
            pl.BlockSpec((1, LANES), const),
            pl.BlockSpec((1, LANES), const),
            pl.BlockSpec((d, MAIN_COLS), const),
            pl.BlockSpec((d, VALS_COLS), const),
        ],
        out_specs=[
            pl.BlockSpec((tm, MAIN_COLS), lambda i: (i, 0)),
            pl.BlockSpec((1, VT_ROWS, tm), lambda i: (i // nsb, 0, i % nsb)),
            pl.BlockSpec((tm, LANES), lambda i: (i, 0)),
        ],
        out_shape=[
            jax.ShapeDtypeStruct((n, MAIN_COLS), MXU_DTYPE),
            jax.ShapeDtypeStruct((batch, VT_ROWS, seq), MXU_DTYPE),
            jax.ShapeDtypeStruct((n, LANES), F32),
        ],
        compiler_params=_cparams(("parallel",)),
        name="proj",
    )(x2d, pos2d, g, freq, sg1, sg2, w_main, w_v)


def _cumf_kernel(misc_ref, bias_ref, c_ref):
    s = misc_ref.shape[1]
    r = lax.broadcasted_iota(jnp.int32, (TK, TK), 0)
    c = lax.broadcasted_iota(jnp.int32, (TK, TK), 1)
    incl = (c <= r).astype(MXU_DTYPE)
    carry = jnp.zeros((1, LANES), F32)
    for b in range(s // TK):
        logit = misc_ref[0, b * TK:(b + 1) * TK, :] + bias_ref[...]
        logf, _ = _log_sigmoid_parts(logit)
        hi, mid, lo = _split3(logf)
        cs = (_dot(incl, lo) + _dot(incl, mid)) + _dot(incl, hi) + carry
        c_ref[0, b * TK:(b + 1) * TK, :] = cs
        carry = cs[TK - 1:TK, :]


def _cumf(misc3d, bias_row):
    b, s, _ = misc3d.shape
    return pl.pallas_call(
        _cumf_kernel,
        grid=(b,),
        in_specs=[
            pl.BlockSpec((1, s, LANES), lambda i: (i, 0, 0)),
            pl.BlockSpec((1, LANES), lambda i: (0, 0)),
        ],
        out_specs=pl.BlockSpec((1, s, LANES), lambda i: (i, 0, 0)),
        out_shape=jax.ShapeDtypeStruct((b, s, LANES), F32),
        compiler_params=_cparams(("parallel",)),
        name="cumf",
    )(misc3d, bias_row)


def _head_q(q_tile, half):
    lane = lax.broadcasted_iota(jnp.int32, q_tile.shape, 1)
    keep = (lane // HEAD_DIM) == half
    return jnp.where(keep, q_tile.astype(F32) * ATTN_SCALE, 0.0).astype(MXU_DTYPE)


def _causal_tile_t(strict):
    key = lax.broadcasted_iota(jnp.int32, (TK, TQ), 0)
    qry = lax.broadcasted_iota(jnp.int32, (TK, TQ), 1)
    return (key < qry) if strict else (key <= qry)


def _rows(j):
    return pl.ds(pl.multiple_of(j * TK, TK), TK)


def _lookahead_loop(n_last, bufs, issue, step, carry, finish, primed=False):
    buf_a, buf_b = bufs

    def pair(t, carry):
        n = 2 * t
        issue(n + 1, buf_b)
        carry = step(n, buf_a, carry, False)
        issue(n + 2, buf_a)
        return step(n + 1, buf_b, carry, False)

    def odd_tail(carry):
        issue(n_last, buf_b)
        carry = step(n_last - 1, buf_a, carry, False)
        step(n_last, buf_b, carry, True)
        finish()
        return 0

    def even_tail(carry):
        step(n_last, buf_a, carry, True)
        finish()
        return 0

    if not primed:
        issue(0, buf_a)
    carry = lax.fori_loop(0, n_last // 2, pair, carry)
    lax.cond(n_last % 2 == 1, odd_tail, even_tail, carry)


def _flash_heads(heads, n_full, diag_mask, z_bufs, acc_ref, o_ref):
    half_k, half_q = TK // 2, TQ // 2

    def add_bias(s, bias):
        if isinstance(bias, (list, tuple)):
            return jnp.concatenate([s[:, t * LANES:(t + 1) * LANES] + bias[t] for t in range(len(bias))],
                                   axis=1)
        return s + bias

    def later_queries(bias):
        return bias[half_q // LANES:] if isinstance(bias, (list, tuple)) else bias[:, half_q:]

    ones = jnp.ones((SUM_ROWS, TK), MXU_DTYPE)

    def issue(j, buf):
        for h, (qh, k_at, _, _) in enumerate(heads):
            buf[h] = _dot_nt(k_at(_rows(j)), qh)

    def update(tiles, rows, maxes, lanes):
        stats = []
        for s, m in zip(tiles, maxes):
            m_new = jnp.maximum(m, jnp.max(s, axis=0, keepdims=True))
            stats.append((m_new, jnp.exp(m - m_new), jnp.exp(s - m_new).astype(MXU_DTYPE)))
        for h, ((_, alpha, p), (_, _, vt_at, _)) in enumerate(zip(stats, heads)):
            vt1 = jnp.concatenate([vt_at(rows), ones[:, :p.shape[0]]], axis=0)
            acc_ref[h, :, lanes] = alpha * acc_ref[h, :, lanes] + _dot(vt1, p)
        return tuple(m_new for m_new, _, _ in stats)

    def step(j, buf, maxes):
        rows = _rows(j)
        tiles = [add_bias(buf[h], bias_at(rows)) for h, (_, _, _, bias_at) in enumerate(heads)]
        return update(tiles, rows, maxes, slice(None))

    def diagonal_step(j, buf, maxes):
        start = pl.multiple_of(j * TK, TK)
        early, late = pl.ds(start, half_k), pl.ds(pl.multiple_of(start + half_k, half_k), half_k)

        def masked(s, mask):
            return s if diag_mask is None else jnp.where(mask, s, NEG_BIG)

        tiles = [masked(add_bias(buf[h, :half_k, :], bias_at(early)), None if diag_mask is None
                        else diag_mask[:half_k, :]) for h, (_, _, _, bias_at) in enumerate(heads)]
        maxes = update(tiles, early, maxes, slice(None))
        tiles = [masked(add_bias(buf[h, half_k:, half_q:], later_queries(bias_at(late))), None if diag_mask is None
                        else diag_mask[half_k:, half_q:]) for h, (_, _, _, bias_at) in enumerate(heads)]
        update(tiles, late, tuple(m[:, half_q:] for m in maxes), slice(half_q, TQ))
        return maxes

    def finish():
        _store_heads_t(o_ref, [acc_ref[h, :HEAD_DIM, :] / acc_ref[h, HEAD_DIM:HEAD_DIM + 1, :]
                               for h in range(len(heads))])

    acc_ref[...] = jnp.zeros(acc_ref.shape, F32)
    init = tuple(jnp.full((1, TQ), NEG_BIG, F32) for _ in heads)
    _lookahead_loop(n_full, z_bufs, issue,
                    lambda j, buf, maxes, last: (diagonal_step if last else step)(j, buf, maxes), init, finish)


def _store_heads_t(o_ref, outs_t):
    for t in range(len(outs_t) // 2):
        pair_t = jnp.concatenate([outs_t[2 * t], outs_t[2 * t + 1]], axis=0)
        o_ref[0, :, t * LANES:(t + 1) * LANES] = pair_t.T.astype(o_ref.dtype)


def _sb_kernel(q_ref, k_ref, vt_ref, o_ref, za_ref, zb_ref, acc_ref):
    i = pl.program_id(2)
    r = lax.broadcasted_iota(jnp.int32, (TK, TK), 0)
    c = lax.broadcasted_iota(jnp.int32, (TK, TK), 1)
    after = (c > r).astype(MXU_DTYPE)
    strict = _causal_tile_t(True)
    n_heads = HEADS_PER_STEP
    qhs = [_head_q(q_ref[0][:, (h // 2) * LANES:(h // 2 + 1) * LANES], h % 2) for h in range(n_heads)]

    def issue(j, buf):
        for h in range(n_heads):
            buf[h] = _dot_nt(k_ref[0, _rows(j), (h // 2) * LANES:(h // 2 + 1) * LANES], qhs[h])

    def update(tiles, rows, tails, lanes, mask):
        n_rows = tiles[0].shape[0]
        parts = []
        for z in tiles:
            log_beta, neg_l1m = _log_sigmoid_parts(z)
            if mask is not None:
                neg_l1m = jnp.where(mask, neg_l1m, 0.0)
            hi, lo = _split2(neg_l1m)
            parts.append((log_beta, neg_l1m[0:1, :], jnp.concatenate([hi, lo], axis=0)))
        after_rows = jnp.concatenate([after[:n_rows, :n_rows]] * 2, axis=1)
        laters = [_dot(after_rows, hl) for (_, _, hl) in parts]
        ws = []
        for (log_beta, _, _), later in zip(parts, laters):
            w = jnp.exp(log_beta - later)
            if mask is not None:
                w = jnp.where(mask, w, 0.0)
            ws.append(w.astype(MXU_DTYPE))
        out = []
        for h, (w, (_, first, _), later, tail) in enumerate(zip(ws, parts, laters, tails)):
            vtj = vt_ref[0, h * HEAD_DIM:(h + 1) * HEAD_DIM, rows]
            acc_ref[h, :, lanes] = acc_ref[h, :, lanes] + jnp.exp(-tail) * _dot(vtj, w)
            out.append(tail + (first + later[0:1, :]))
        return tuple(out)

    def step(j, buf, tails):
        return update([buf[h] for h in range(n_heads)], _rows(j), tails, slice(None), None)

    def diagonal_step(j, buf, tails):
        half_k, half_q = TK // 2, TQ // 2
        start = pl.multiple_of(j * TK, TK)
        early, late = pl.ds(start, half_k), pl.ds(pl.multiple_of(start + half_k, half_k), half_k)
        late_tails = update([buf[h, half_k:, half_q:] for h in range(n_heads)], late,
                            tuple(t[:, half_q:] for t in tails), slice(half_q, TQ), strict[half_k:, half_q:])
        tails = tuple(jnp.concatenate([t[:, :half_q], lt], axis=1) for t, lt in zip(tails, late_tails))
        return update([buf[h, :half_k, :] for h in range(n_heads)], early, tails, slice(None), strict[:half_k, :])

    acc_ref[...] = jnp.zeros(acc_ref.shape, F32)
    carry = tuple(jnp.zeros((1, TQ), F32) for _ in range(n_heads))
    issue(i, za_ref)

    def finish():
        _store_heads_t(o_ref, [acc_ref[h] for h in range(n_heads)])

    def with_earlier_blocks(cr):
        issue(i - 1, zb_ref)
        cr = diagonal_step(i, za_ref, cr)
        _lookahead_loop(i - 1, (zb_ref, za_ref), lambda n, buf: issue(i - 1 - n, buf),
                        lambda n, buf, c, last: step(i - 1 - n, buf, c), cr, finish, primed=True)
        return 0

    def diagonal_only(cr):
        diagonal_step(i, za_ref, cr)
        finish()
        return 0

    lax.cond(i > 0, with_earlier_blocks, diagonal_only, carry)


def _sb(main3d, vt3d):
    b, s, _ = main3d.shape
    nq = s // TQ
    w = HEADS_PER_STEP * HEAD_DIM
    qb, kb, vb = COL_QA // w, COL_KA // w, ROW_VA // w
    return pl.pallas_call(
        _sb_kernel,
        grid=(b, SB_HEADS // HEADS_PER_STEP, nq),
        in_specs=[
            pl.BlockSpec((1, TQ, w), lambda bi, hg, i: (bi, i, qb + hg)),
            pl.BlockSpec((1, s, w), lambda bi, hg, i: (bi, 0, kb + hg)),
            pl.BlockSpec((1, w, s), lambda bi, hg, i: (bi, vb + hg, 0)),
        ],
        out_specs=pl.BlockSpec((1, TQ, w), lambda bi, hg, i: (bi, i, hg)),
        out_shape=jax.ShapeDtypeStruct((b, s, BRANCH_WIDTH), MXU_DTYPE),
        scratch_shapes=[pltpu.VMEM((HEADS_PER_STEP, TK, TQ), F32), pltpu.VMEM((HEADS_PER_STEP, TK, TQ), F32),
                        pltpu.VMEM((HEADS_PER_STEP, HEAD_DIM, TQ), F32)],
        compiler_params=_cparams(("parallel", "parallel", "arbitrary")),
        name="sb",
    )(main3d, main3d, vt3d)


def _fox_kernel(q_ref, k_ref, vt_ref, c_ref, o_ref, cb_ref, za_ref, zb_ref, acc_ref):
    hg = pl.program_id(1)
    i = pl.program_id(2)
    n_heads = HEADS_PER_STEP

    @pl.when(i == 0)
    def _():
        c = c_ref[0]
        lane = lax.broadcasted_iota(jnp.int32, c.shape, 1)
        for h in range(n_heads):
            if n_heads == FOX_HEADS:
                col = c[:, h:h + 1]
            else:
                col = jnp.sum(jnp.where(lane == hg * n_heads + h, c, 0.0), axis=1, keepdims=True)
            cb_ref[h] = jnp.broadcast_to(-col, c.shape)

    heads = []
    for h in range(n_heads):
        t = h // 2
        qh = _head_q(q_ref[0][:, t * LANES:(t + 1) * LANES], h % 2)
        k_at = lambda rows, t=t: k_ref[0, rows, t * LANES:(t + 1) * LANES]
        vt_at = lambda rows, h=h: vt_ref[0, h * HEAD_DIM:(h + 1) * HEAD_DIM, rows]
        bias_at = lambda rows, h=h: [cb_ref[h, rows, :]] * (TQ // LANES)
        heads.append((qh, k_at, vt_at, bias_at))
    _flash_heads(heads, i, _causal_tile_t(False), (za_ref, zb_ref), acc_ref, o_ref)


def _fox(main3d, vt3d, c3d):
    b, s, _ = main3d.shape
    nq = s // TQ
    w = HEADS_PER_STEP * HEAD_DIM
    qb, kb, vb = COL_QB // w, COL_KB // w, ROW_VB // w
    return pl.pallas_call(
        _fox_kernel,
        grid=(b, FOX_HEADS // HEADS_PER_STEP, nq),
        in_specs=[
            pl.BlockSpec((1, TQ, w), lambda bi, hg, i: (bi, i, qb + hg)),
            pl.BlockSpec((1, s, w), lambda bi, hg, i: (bi, 0, kb + hg)),
            pl.BlockSpec((1, w, s), lambda bi, hg, i: (bi, vb + hg, 0)),
            pl.BlockSpec((1, s, LANES), lambda bi, hg, i: (bi, 0, 0)),
        ],
        out_specs=pl.BlockSpec((1, TQ, w), lambda bi, hg, i: (bi, i, hg)),
        out_shape=jax.ShapeDtypeStruct((b, s, BRANCH_WIDTH), MXU_DTYPE),
        scratch_shapes=[pltpu.VMEM((HEADS_PER_STEP, s, LANES), F32),
                        pltpu.VMEM((HEADS_PER_STEP, TK, TQ), F32), pltpu.VMEM((HEADS_PER_STEP, TK, TQ), F32),
                        pltpu.VMEM((HEADS_PER_STEP, HEAD_DIM + SUM_ROWS, TQ), F32)],
        compiler_params=_cparams(("parallel", "parallel", "arbitrary")),
        name="fox",
    )(main3d, main3d, vt3d, c3d)


def _dsa_kernel(q_ref, k_ref, vt_ref, qi_ref, ki_ref, misc_ref, o_ref, key_ref, dig_ref, bias_ref,
                za_ref, zb_ref, acc_ref, *, top_k):
    i = pl.program_id(1)
    diag = _causal_tile_t(False)

    wi_t = misc_ref[0].T[MISC_WI:MISC_WI + IDX_HEADS, :] * IDX_W_SCALE
    qih = [_head_q(qi_ref[0][:, (h // 2) * LANES:(h // 2 + 1) * LANES], h % 2) for h in range(IDX_HEADS)]

    def score_block(j, mask):
        kij = ki_ref[0, _rows(j), :]
        sc = None
        for h in range(IDX_HEADS):
            term = wi_t[h:h + 1, :] * jnp.maximum(_dot_nt(kij, qih[h]), 0.0)
            sc = term if sc is None else sc + term
        if mask is not None:
            sc = jnp.where(mask, sc, -jnp.inf)
        bits = lax.bitcast_convert_type(jnp.where(sc == 0.0, 0.0, sc), jnp.int32)
        key_ref[_rows(j), :] = bits ^ ((bits >> 31) & jnp.int32(0x7FFFFFFF))

    def score_pair(t, _):
        score_block(2 * t, None)
        score_block(2 * t + 1, None)
        return 0

    def score_odd_tail():
        score_block(i - 1, None)
        score_block(i, diag)
        return 0

    def score_even_tail():
        score_block(i, diag)
        return 0

    lax.fori_loop(0, i // 2, score_pair, 0)
    lax.cond(i % 2 == 1, score_odd_tail, score_even_tail)

    small = MXU_DTYPE
    group = 2 * SUBLANES

    assert (key_ref.shape[0] // group) <= 256

    def count_digits_ge(cand):
        def hits(j):
            hit = jnp.where(dig_ref[_rows(j), :] >= cand, jnp.ones((), small), jnp.zeros((), small))
            part = hit[0:group]
            for g in range(1, TK // group):
                part = part + hit[g * group:(g + 1) * group]
            return part

        acc = lax.fori_loop(0, (i + 1) // 2, lambda t, a: a + (hits(2 * t) + hits(2 * t + 1)),
                            jnp.zeros((group, TQ), small))
        acc = lax.cond((i + 1) % 2 == 1, lambda a: a + hits(i), lambda a: a, acc)
        return jnp.sum(acc.astype(F32), axis=0, keepdims=True)

    above = jnp.zeros((1, TQ), F32)
    prefix = None
    for shift in (24, 16, 8, 0):
        def fill(j, _, shift=shift, prefix=prefix):
            key = key_ref[_rows(j), :]
            if prefix is None:
                digit = (key >> shift) + 128
            else:
                digit = (key ^ (prefix << (shift + 8))) >> shift
                shares = lax.bitcast_convert_type(digit, jnp.uint32) < jnp.uint32(256)
                digit = jnp.where(shares, digit, -1)
            dig_ref[_rows(j), :] = digit.astype(F32).astype(small)
            return 0

        lax.fori_loop(0, i + 1, fill, 0)

        def search(_, st, above=above):
            lo, hi, n_hi = st
            mid = (lo + hi) * 0.5
            n_mid = count_digits_ge(mid.astype(small))
            ok = above + n_mid >= top_k
            return jnp.where(ok, mid, lo), jnp.where(ok, hi, mid), jnp.where(ok, n_hi, n_mid)

        start = (jnp.zeros((1, TQ), F32), jnp.full((1, TQ), 256.0, F32), jnp.zeros((1, TQ), F32))
        digit, _, n_hi = lax.fori_loop(0, 8, search, start)
        above = above + n_hi
        digit = digit.astype(jnp.int32)
        prefix = digit - 128 if prefix is None else (prefix << 8) | digit
    thr_key = prefix

    need = top_k - above
    kr = lax.broadcasted_iota(jnp.int32, (TK, TK), 0)
    kc = lax.broadcasted_iota(jnp.int32, (TK, TK), 1)
    upto = (kc <= kr).astype(MXU_DTYPE)

    def bias_block(j, mask, seen):
        key = key_ref[_rows(j), :]
        tie = key == thr_key
        rank = _dot(upto, jnp.where(tie, 1.0, 0.0).astype(MXU_DTYPE)) + seen
        sel = (key > thr_key) | (tie & (rank <= need))
        if mask is not None:
            sel = sel & mask
        bias_ref[_rows(j), :] = jnp.where(sel, 0.0, NEG_BIG)
        return rank[TK - 1:TK, :]

    def bias_pair(t, seen):
        return bias_block(2 * t + 1, None, bias_block(2 * t, None, seen))

    def bias_odd_tail(seen):
        bias_block(i, diag, bias_block(i - 1, None, seen))
        return 0

    def bias_even_tail(seen):
        bias_block(i, diag, seen)
        return 0

    seen = lax.fori_loop(0, i // 2, bias_pair, jnp.zeros((1, TQ), F32))
    lax.cond(i % 2 == 1, bias_odd_tail, bias_even_tail, seen)

    bias_at = lambda rows: bias_ref[rows, :]
    heads = []
    for h in range(DSA_HEADS):
        g = h // DSA_GROUP
        k_at = lambda rows, g=g: k_ref[0, rows, g * LANES:(g + 1) * LANES]
        vt_at = lambda rows, g=g: vt_ref[0, g * HEAD_DIM:(g + 1) * HEAD_DIM, rows]
        qh = _head_q(q_ref[0][:, (h // 2) * LANES:(h // 2 + 1) * LANES], h % 2)
        heads.append((qh, k_at, vt_at, bias_at))
    _flash_heads(heads, i, None, (za_ref, zb_ref), acc_ref, o_ref)


def _dsa(main3d, vt3d, misc3d, top_k):
    b, s, _ = main3d.shape
    nq = s // TQ
    assert top_k <= TK, "the first key block must hold at least top_k keys"
    return pl.pallas_call(
        functools.partial(_dsa_kernel, top_k=top_k),
        grid=(b, nq),
        in_specs=[
            pl.BlockSpec((1, TQ, 4 * LANES), lambda bi, i: (bi, i, COL_QC // (4 * LANES))),
            pl.BlockSpec((1, s, 2 * LANES), lambda bi, i: (bi, 0, COL_KC // (2 * LANES))),
            pl.BlockSpec((1, LANES, s), lambda bi, i: (bi, ROW_VC // LANES, 0)),
            pl.BlockSpec((1, TQ, 2 * LANES), lambda bi, i: (bi, i, COL_QI // (2 * LANES))),
            pl.BlockSpec((1, s, LANES), lambda bi, i: (bi, 0, COL_KI // LANES)),
            pl.BlockSpec((1, TQ, LANES), lambda bi, i: (bi, i, 0)),
        ],
        out_specs=pl.BlockSpec((1, TQ, BRANCH_WIDTH), lambda bi, i: (bi, i, 0)),
        out_shape=jax.ShapeDtypeStruct((b, s, BRANCH_WIDTH), MXU_DTYPE),
        scratch_shapes=[pltpu.VMEM((s, TQ), jnp.int32), pltpu.VMEM((s, TQ), MXU_DTYPE), pltpu.VMEM((s, TQ), F32),
                        pltpu.VMEM((DSA_HEADS, TK, TQ), F32), pltpu.VMEM((DSA_HEADS, TK, TQ), F32),
                        pltpu.VMEM((DSA_HEADS, HEAD_DIM + SUM_ROWS, TQ), F32)],
        compiler_params=_cparams(("parallel", "arbitrary")),
        name="dsa",
    )(main3d, main3d, vt3d, main3d, main3d, misc3d)


def _merge_kernel(x_ref, a_ref, b_ref, c_ref, g_ref, wg_ref, wb_ref, wo_ref, o_ref):
    x = x_ref[...]
    d = x.shape[1]
    h = _rmsnorm(x, g_ref[...]).astype(MXU_DTYPE)
    y = None
    for n, br_ref in enumerate((a_ref, b_ref, c_ref)):
        gate = jax.nn.sigmoid(_dot(h, wg_ref[:, n * d:(n + 1) * d]))
        term = gate * _dot(br_ref[...], wb_ref[n])
        y = term if y is None else y + term
    o_ref[...] = x + _dot(y.astype(MXU_DTYPE), wo_ref[...])


def _merge(x2d, br_a, br_b, br_c, g, w_gate, w_branch, w_out, tm):
    n, d = x2d.shape
    bw = br_a.shape[1]
    row = lambda i: (i, 0)
    const = lambda i: (0, 0)
    return pl.pallas_call(
        _merge_kernel,
        grid=(n // tm,),
        in_specs=[
            pl.BlockSpec((tm, d), row),
            pl.BlockSpec((tm, bw), row),
            pl.BlockSpec((tm, bw), row),
            pl.BlockSpec((tm, bw), row),
            pl.BlockSpec((1, d), const),
            pl.BlockSpec((d, N_BRANCH * d), const),
            pl.BlockSpec((N_BRANCH, bw, d), lambda i: (0, 0, 0)),
            pl.BlockSpec((d, d), const),
        ],
        out_specs=pl.BlockSpec((tm, d), row),
        out_shape=jax.ShapeDtypeStruct((n, d), F32),
        compiler_params=_cparams(("parallel",)),
        name="merge",
    )(x2d, br_a, br_b, br_c, g, w_gate, w_branch, w_out)


def _mlp_kernel(x_ref, g_ref, wu_ref, wd_ref, gf_ref, o_ref, *, ff_chunk, final_norm):
    x = x_ref[...]
    h = _rmsnorm(x, g_ref[...]).astype(MXU_DTYPE)
    acc = x
    for c in range(wu_ref.shape[1] // ff_chunk):
        u = jnp.maximum(_dot(h, wu_ref[:, c * ff_chunk:(c + 1) * ff_chunk]), 0.0)
        acc = acc + _dot((u * u).astype(MXU_DTYPE), wd_ref[c * ff_chunk:(c + 1) * ff_chunk, :])
    if final_norm:
        acc = _rmsnorm(acc, gf_ref[...])
    o_ref[...] = acc


def _mlp(x2d, g, w_up, w_down, g_final, tm, final_norm):
    n, d = x2d.shape
    ff = w_up.shape[1]
    row = lambda i: (i, 0)
    const = lambda i: (0, 0)
    return pl.pallas_call(
        functools.partial(_mlp_kernel, ff_chunk=min(ff, 1024), final_norm=final_norm),
        grid=(n // tm,),
        in_specs=[
            pl.BlockSpec((tm, d), row),
            pl.BlockSpec((1, d), const),
            pl.BlockSpec((d, ff), const),
            pl.BlockSpec((ff, d), const),
            pl.BlockSpec((1, d), const),
        ],
        out_specs=pl.BlockSpec((tm, d), row),
        out_shape=jax.ShapeDtypeStruct((n, d), F32),
        compiler_params=_cparams(("parallel",)),
        name="mlp",
    )(x2d, g, w_up, w_down, g_final)


def _w_in_columns(d_model):
    sizes = (("qa", BRANCH_WIDTH), ("ka", BRANCH_WIDTH), ("va", BRANCH_WIDTH),
             ("qb", BRANCH_WIDTH), ("kb", BRANCH_WIDTH), ("vb", BRANCH_WIDTH), ("fl", FOX_HEADS),
             ("qc", DSA_HEADS * HEAD_DIM), ("kc", DSA_KV_HEADS * HEAD_DIM), ("vc", DSA_KV_HEADS * HEAD_DIM),
             ("qi", IDX_HEADS * IDX_DIM), ("ki", IDX_DIM), ("wi", IDX_HEADS), ("gates", N_BRANCH * d_model))
    cols, o = {}, 0
    for name, width in sizes:
        cols[name] = (o, o + width)
        o += width
    return cols, o


def _pack_kernel(w_ref, main_ref, vals_ref, gates_ref):
    cols, _ = _w_in_columns(gates_ref.shape[2] // N_BRANCH)

    def src(name, lo=0, hi=None):
        a, b = cols[name]
        return w_ref[0, :, a + lo:(b if hi is None else a + hi)]

    def put(ref, start, value):
        ref[0, :, start:start + value.shape[1]] = value.astype(ref.dtype)

    put(main_ref, COL_QA, src("qa"))
    put(main_ref, COL_KA, src("ka"))
    put(main_ref, COL_QB, src("qb"))
    put(main_ref, COL_KB, src("kb"))
    put(main_ref, COL_QC, src("qc"))
    for g in range(DSA_KV_HEADS):
        kg = src("kc", g * HEAD_DIM, (g + 1) * HEAD_DIM)
        put(main_ref, COL_KC + g * LANES, jnp.concatenate([kg, kg], axis=1))
    put(main_ref, COL_QI, src("qi"))
    ki = src("ki")
    put(main_ref, COL_KI, jnp.concatenate([ki, ki], axis=1))
    put(vals_ref, ROW_VA, src("va"))
    put(vals_ref, ROW_VB, src("vb"))
    put(vals_ref, ROW_VC, src("vc"))
    pad = jnp.zeros((w_ref.shape[1], LANES - FOX_HEADS - IDX_HEADS), F32)
    put(vals_ref, VT_ROWS, jnp.concatenate([src("fl"), src("wi"), pad], axis=1))
    put(gates_ref, 0, src("gates"))


def _pack_w_in(w_in):
    depth, d, total = w_in.shape
    _, end = _w_in_columns(d)
    assert end == total
    rows = 256
    blk = lambda width: pl.BlockSpec((1, rows, width), lambda l, r: (l, r, 0))
    widths = (MAIN_COLS, VALS_COLS, N_BRANCH * d)
    return pl.pallas_call(
        _pack_kernel,
        grid=(depth, d // rows),
        in_specs=[blk(total)],
        out_specs=[blk(wd) for wd in widths],
        out_shape=[jax.ShapeDtypeStruct((depth, d, wd), MXU_DTYPE) for wd in widths],
        compiler_params=_cparams(("parallel", "parallel")),
        name="pack",
    )(w_in)


def _rope_rows():
    lane = jnp.arange(LANES) % HEAD_DIM
    half = ROT_DIM // 2
    inv_freq = ROPE_THETA ** (-jnp.arange(0, ROT_DIM, 2, dtype=F32) / ROT_DIM)
    freq = jnp.where(lane < ROT_DIM, inv_freq[lane % half], 0.0).astype(F32)
    sg1 = jnp.where(lane < half, -1.0, 0.0).astype(F32)
    sg2 = jnp.where((lane >= half) & (lane < ROT_DIM), 1.0, 0.0).astype(F32)
    return freq[None, :], sg1[None, :], sg2[None, :]


def kernel(x, positions, g_mix, w_in, b_forget, w_branch, w_out, g_mlp, w_up, w_down, g_final):
    b, s, d = x.shape
    n = b * s
    depth = w_in.shape[0]
    top_k = min(TOPK_MAX, s // 4)
    tm = min(512, s)
    freq, sg1, sg2 = _rope_rows()
    pos2d = positions.reshape(n, 1)
    x2d = x.reshape(n, d)
    w_main_all, w_v_all, w_gate_all = _pack_w_in(w_in)
    for layer in range(depth):
        w_main, w_v, w_gate = w_main_all[layer], w_v_all[layer], w_gate_all[layer]
        bias_row = jnp.concatenate([b_forget[layer], jnp.zeros((LANES - FOX_HEADS,), F32)])[None, :]
        main, vt, misc = _proj(x2d, pos2d, g_mix[layer][None, :], freq, sg1, sg2, w_main, w_v, tm, b, s)
        main3d = main.reshape(b, s, MAIN_COLS)
        misc3d = misc.reshape(b, s, LANES)
        cum = _cumf(misc3d, bias_row)
        br_a = _sb(main3d, vt).reshape(n, BRANCH_WIDTH)
        br_b = _fox(main3d, vt, cum).reshape(n, BRANCH_WIDTH)
        br_c = _dsa(main3d, vt, misc3d, top_k).reshape(n, BRANCH_WIDTH)
        x2d = _merge(x2d, br_a, br_b, br_c, g_mix[layer][None, :], w_gate,
                     w_branch[layer].astype(MXU_DTYPE), w_out[layer].astype(MXU_DTYPE), tm)
        x2d = _mlp(x2d, g_mlp[layer][None, :], w_up[layer].astype(MXU_DTYPE),
                   w_down[layer].astype(MXU_DTYPE), g_final[None, :], tm, layer == depth - 1)
    return x2d.reshape(b, s, d)
```

```python
import functools

import jax
import jax.numpy as jnp
from jax import lax
from jax.experimental import pallas as pl
from jax.experimental.pallas import tpu as pltpu

F32 = jnp.float32
MXU_DTYPE = jnp.bfloat16

HEAD_DIM = 64
SB_HEADS = 8
FOX_HEADS = 8
DSA_HEADS = 8
DSA_KV_HEADS = 2
DSA_GROUP = DSA_HEADS // DSA_KV_HEADS
IDX_HEADS = 4
IDX_DIM = 64
N_BRANCH = 3
ROPE_THETA = 500000.0
ROT_DIM = HEAD_DIM // 4
TOPK_MAX = 256
NORM_EPS = 1e-6

LANES = 128
SUBLANES = 8
SUM_ROWS = 2 * SUBLANES
BRANCH_WIDTH = SB_HEADS * HEAD_DIM
ATTN_SCALE = HEAD_DIM ** -0.5
assert IDX_DIM == HEAD_DIM
IDX_W_SCALE = IDX_HEADS ** -0.5

COL_QA, COL_KA = 0, 512
COL_QB, COL_KB = 1024, 1536
COL_QC = 2048
COL_KC = 2560
COL_QI = 2816
COL_KI = 3072
MAIN_COLS = 3200
ROPE_START = COL_QC
ROW_VA, ROW_VB, ROW_VC = 0, 512, 1024
VT_ROWS = 1152
VALS_COLS = VT_ROWS + LANES
MISC_FL, MISC_WI = 0, 8

TQ = 256
TK = 256
HEADS_PER_STEP = 8
NEG_BIG = -1e30

VMEM_LIMIT = 56 * 1024 * 1024


def _cparams(sem):
    return pltpu.CompilerParams(dimension_semantics=sem, vmem_limit_bytes=VMEM_LIMIT)


def _rmsnorm(x, g):
    y = x * lax.rsqrt(jnp.mean(x * x, axis=-1, keepdims=True) + NORM_EPS)
    return y * g


def _dot(a, b):
    return jnp.dot(a, b, preferred_element_type=F32)


def _dot_nt(a, b):
    return lax.dot_general(a, b, (((1,), (1,)), ((), ())), preferred_element_type=F32)


def _split2(x):
    hi = x.astype(MXU_DTYPE)
    lo = (x - hi.astype(F32)).astype(MXU_DTYPE)
    return hi, lo


def _split3(x):
    hi = x.astype(MXU_DTYPE)
    r = x - hi.astype(F32)
    mid = r.astype(MXU_DTYPE)
    lo = (r - mid.astype(F32)).astype(MXU_DTYPE)
    return hi, mid, lo


def _log_sigmoid_parts(z):
    neg_abs = lax.bitcast_convert_type(lax.bitcast_convert_type(z, jnp.int32) | jnp.int32(-2 ** 31), F32)
    sp = jnp.log(1.0 + jnp.exp(neg_abs))
    log_sig = jnp.minimum(z, 0.0) - sp
    return log_sig, z - log_sig


def _proj_kernel(x_ref, pos_ref, g_ref, freq_ref, sg1_ref, sg2_ref, wm_ref, wv_ref,
                 main_ref, vt_ref, misc_ref):
    h = _rmsnorm(x_ref[...], g_ref[...]).astype(MXU_DTYPE)
    ang = pos_ref[...].astype(F32) * freq_ref[...]
    cos = jnp.cos(ang)
    sin = jnp.sin(ang)
    s_up = sin * sg1_ref[...]
    s_dn = sin * sg2_ref[...]
    half = ROT_DIM // 2
    chunk = 4 * LANES
    for c0 in range(0, VALS_COLS, 2 * LANES):
        pv = _dot(h, wv_ref[:, c0:c0 + 2 * LANES])
        for r0 in (c0, c0 + LANES):
            piece = pv[:, r0 - c0:r0 - c0 + LANES]
            if r0 < VT_ROWS:
                vt_ref[0, r0:r0 + LANES, :] = piece.T.astype(vt_ref.dtype)
            else:
                misc_ref[...] = piece
    for c0 in list(range(ROPE_START, MAIN_COLS, chunk)) + list(range(0, ROPE_START, chunk)):
        c1 = min(c0 + chunk, MAIN_COLS)
        p = _dot(h, wm_ref[:, c0:c1])
        if c0 >= ROPE_START:
            for t in range((c1 - c0) // LANES):
                pt = p[:, t * LANES:(t + 1) * LANES]
                pt = (pt * cos + pltpu.roll(pt, LANES - half, 1) * s_up
                      + pltpu.roll(pt, half, 1) * s_dn)
                main_ref[:, c0 + t * LANES:c0 + (t + 1) * LANES] = pt.astype(main_ref.dtype)
        else:
            main_ref[:, c0:c1] = p.astype(main_ref.dtype)


def _proj(x2d, pos2d, g, freq, sg1, sg2, w_main, w_v, tm, batch, seq):
    n, d = x2d.shape
    nsb = seq // tm
    const = lambda i: (0, 0)
    return pl.pallas_call(
        _proj_kernel,
        grid=(n // tm,),
        in_specs=[
            pl.BlockSpec((tm, d), lambda i: (i, 0)),
            pl.BlockSpec((tm, 1), lambda i: (i, 0)),
            pl.BlockSpec((1, d), const),
            pl.BlockSpec((1, LANES), const),
            pl.BlockSpec((1, LANES), const),
            pl.BlockSpec((1, LANES), const),
            pl.BlockSpec((d, MAIN_COLS), const),
            pl.BlockSpec((d, VALS_COLS), const),
        ],
        out_specs=[
            pl.BlockSpec((tm, MAIN_COLS), lambda i: (i, 0)),
            pl.BlockSpec((1, VT_ROWS, tm), lambda i: (i // nsb, 0, i % nsb)),
            pl.BlockSpec((tm, LANES), lambda i: (i, 0)),
        ],
        out_shape=[
            jax.ShapeDtypeStruct((n, MAIN_COLS), MXU_DTYPE),
            jax.ShapeDtypeStruct((batch, VT_ROWS, seq), MXU_DTYPE),
            jax.ShapeDtypeStruct((n, LANES), F32),
        ],
        compiler_params=_cparams(("parallel",)),
        name="proj",
    )(x2d, pos2d, g, freq, sg1, sg2, w_main, w_v)


def _cumf_kernel(misc_ref, bias_ref, c_ref):
    s = misc_ref.shape[1]
    r = lax.broadcasted_iota(jnp.int32, (TK, TK), 0)
    c = lax.broadcasted_iota(jnp.int32, (TK, TK), 1)
    incl = (c <= r).astype(MXU_DTYPE)
    carry = jnp.zeros((1, LANES), F32)
    for b in range(s // TK):
        logit = misc_ref[0, b * TK:(b + 1) * TK, :] + bias_ref[...]
        logf, _ = _log_sigmoid_parts(logit)
        hi, mid, lo = _split3(logf)
        cs = (_dot(incl, lo) + _dot(incl, mid)) + _dot(incl, hi) + carry
        c_ref[0, b * TK:(b + 1) * TK, :] = cs
        carry = cs[TK - 1:TK, :]


def _cumf(misc3d, bias_row):
    b, s, _ = misc3d.shape
    return pl.pallas_call(
        _cumf_kernel,
        grid=(b,),
        in_specs=[
            pl.BlockSpec((1, s, LANES), lambda i: (i, 0, 0)),
            pl.BlockSpec((1, LANES), lambda i: (0, 0)),
        ],
        out_specs=pl.BlockSpec((1, s, LANES), lambda i: (i, 0, 0)),
        out_shape=jax.ShapeDtypeStruct((b, s, LANES), F32),
        compiler_params=_cparams(("parallel",)),
        name="cumf",
    )(misc3d, bias_row)


def _head_q(q_tile, half):
    lane = lax.broadcasted_iota(jnp.int32, q_tile.shape, 1)
    keep = (lane // HEAD_DIM) == half
    return jnp.where(keep, q_tile.astype(F32) * ATTN_SCALE, 0.0).astype(MXU_DTYPE)


def _causal_tile_t(strict):
    key = lax.broadcasted_iota(jnp.int32, (TK, TQ), 0)
    qry = lax.broadcasted_iota(jnp.int32, (TK, TQ), 1)
    return (key < qry) if strict else (key <= qry)


def _rows(j):
    return pl.ds(pl.multiple_of(j * TK, TK), TK)


def _lookahead_loop(n_last, bufs, issue, step, carry, finish, primed=False):
    buf_a, buf_b = bufs

    def pair(t, carry):
        n = 2 * t
        issue(n + 1, buf_b)
        carry = step(n, buf_a, carry, False)
        issue(n + 2, buf_a)
        return step(n + 1, buf_b, carry, False)

    def odd_tail(carry):
        issue(n_last, buf_b)
        carry = step(n_last - 1, buf_a, carry, False)
        step(n_last, buf_b, carry, True)
        finish()
        return 0

    def even_tail(carry):
        step(n_last, buf_a, carry, True)
        finish()
        return 0

    if not primed:
        issue(0, buf_a)
    carry = lax.fori_loop(0, n_last // 2, pair, carry)
    lax.cond(n_last % 2 == 1, odd_tail, even_tail, carry)


def _flash_heads(heads, n_full, diag_mask, z_bufs, acc_ref, o_ref):
    half_k, half_q = TK // 2, TQ // 2

    def add_bias(s, bias):
        if isinstance(bias, (list, tuple)):
            return jnp.concatenate([s[:, t * LANES:(t + 1) * LANES] + bias[t] for t in range(len(bias))],
                                   axis=1)
        return s + bias

    def later_queries(bias):
        return bias[half_q // LANES:] if isinstance(bias, (list, tuple)) else bias[:, half_q:]

    ones = jnp.ones((SUM_ROWS, TK), MXU_DTYPE)

    def issue(j, buf):
        for h, (qh, k_at, _, _) in enumerate(heads):
            buf[h] = _dot_nt(k_at(_rows(j)), qh)

    def update(tiles, rows, maxes, lanes):
        stats = []
        for s, m in zip(tiles, maxes):
            m_new = jnp.maximum(m, jnp.max(s, axis=0, keepdims=True))
            stats.append((m_new, jnp.exp(m - m_new), jnp.exp(s - m_new).astype(MXU_DTYPE)))
        for h, ((_, alpha, p), (_, _, vt_at, _)) in enumerate(zip(stats, heads)):
            vt1 = jnp.concatenate([vt_at(rows), ones[:, :p.shape[0]]], axis=0)
            acc_ref[h, :, lanes] = alpha * acc_ref[h, :, lanes] + _dot(vt1, p)
        return tuple(m_new for m_new, _, _ in stats)

    def step(j, buf, maxes):
        rows = _rows(j)
        tiles = [add_bias(buf[h], bias_at(rows)) for h, (_, _, _, bias_at) in enumerate(heads)]
        return update(tiles, rows, maxes, slice(None))

    def diagonal_step(j, buf, maxes):
        start = pl.multiple_of(j * TK, TK)
        early, late = pl.ds(start, half_k), pl.ds(pl.multiple_of(start + half_k, half_k), half_k)

        def masked(s, mask):
            return s if diag_mask is None else jnp.where(mask, s, NEG_BIG)

        tiles = [masked(add_bias(buf[h, :half_k, :], bias_at(early)), None if diag_mask is None
                        else diag_mask[:half_k, :]) for h, (_, _, _, bias_at) in enumerate(heads)]
        maxes = update(tiles, early, maxes, slice(None))
        tiles = [masked(add_bias(buf[h, half_k:, half_q:], later_queries(bias_at(late))), None if diag_mask is None
                        else diag_mask[half_k:, half_q:]) for h, (_, _, _, bias_at) in enumerate(heads)]
        update(tiles, late, tuple(m[:, half_q:] for m in maxes), slice(half_q, TQ))
        return maxes

    def finish():
        _store_heads_t(o_ref, [acc_ref[h, :HEAD_DIM, :] / acc_ref[h, HEAD_DIM:HEAD_DIM + 1, :]
                               for h in range(len(heads))])

    acc_ref[...] = jnp.zeros(acc_ref.shape, F32)
    init = tuple(jnp.full((1, TQ), NEG_BIG, F32) for _ in heads)
    _lookahead_loop(n_full, z_bufs, issue,
                    lambda j, buf, maxes, last: (diagonal_step if last else step)(j, buf, maxes), init, finish)


def _store_heads_t(o_ref, outs_t):
    for t in range(len(outs_t) // 2):
        pair_t = jnp.concatenate([outs_t[2 * t], outs_t[2 * t + 1]], axis=0)
        o_ref[0, :, t * LANES:(t + 1) * LANES] = pair_t.T.astype(o_ref.dtype)


def _sb_kernel(q_ref, k_ref, vt_ref, o_ref, za_ref, zb_ref, acc_ref):
    i = pl.program_id(2)
    r = lax.broadcasted_iota(jnp.int32, (TK, TK), 0)
    c = lax.broadcasted_iota(jnp.int32, (TK, TK), 1)
    after = (c > r).astype(MXU_DTYPE)
    strict = _causal_tile_t(True)
    n_heads = HEADS_PER_STEP
    qhs = [_head_q(q_ref[0][:, (h // 2) * LANES:(h // 2 + 1) * LANES], h % 2) for h in range(n_heads)]

    def issue(j, buf):
        for h in range(n_heads):
            buf[h] = _dot_nt(k_ref[0, _rows(j), (h // 2) * LANES:(h // 2 + 1) * LANES], qhs[h])

    def update(tiles, rows, tails, lanes, mask):
        n_rows = tiles[0].shape[0]
        parts = []
        for z in tiles:
            log_beta, neg_l1m = _log_sigmoid_parts(z)
            if mask is not None:
                neg_l1m = jnp.where(mask, neg_l1m, 0.0)
            hi, lo = _split2(neg_l1m)
            parts.append((log_beta, neg_l1m[0:1, :], jnp.concatenate([hi, lo], axis=0)))
        after_rows = jnp.concatenate([after[:n_rows, :n_rows]] * 2, axis=1)
        laters = [_dot(after_rows, hl) for (_, _, hl) in parts]
        ws = []
        for (log_beta, _, _), later in zip(parts, laters):
            w = jnp.exp(log_beta - later)
            if mask is not None:
                w = jnp.where(mask, w, 0.0)
            ws.append(w.astype(MXU_DTYPE))
        out = []
        for h, (w, (_, first, _), later, tail) in enumerate(zip(ws, parts, laters, tails)):
            vtj = vt_ref[0, h * HEAD_DIM:(h + 1) * HEAD_DIM, rows]
            acc_ref[h, :, lanes] = acc_ref[h, :, lanes] + jnp.exp(-tail) * _dot(vtj, w)
            out.append(tail + (first + later[0:1, :]))
        return tuple(out)

    def step(j, buf, tails):
        return update([buf[h] for h in range(n_heads)], _rows(j), tails, slice(None), None)

    def diagonal_step(j, buf, tails):
        half_k, half_q = TK // 2, TQ // 2
        start = pl.multiple_of(j * TK, TK)
        early, late = pl.ds(start, half_k), pl.ds(pl.multiple_of(start + half_k, half_k), half_k)
        late_tails = update([buf[h, half_k:, half_q:] for h in range(n_heads)], late,
                            tuple(t[:, half_q:] for t in tails), slice(half_q, TQ), strict[half_k:, half_q:])
        tails = tuple(jnp.concatenate([t[:, :half_q], lt], axis=1) for t, lt in zip(tails, late_tails))
        return update([buf[h, :half_k, :] for h in range(n_heads)], early, tails, slice(None), strict[:half_k, :])

    acc_ref[...] = jnp.zeros(acc_ref.shape, F32)
    carry = tuple(jnp.zeros((1, TQ), F32) for _ in range(n_heads))
    issue(i, za_ref)

    def finish():
        _store_heads_t(o_ref, [acc_ref[h] for h in range(n_heads)])

    def with_earlier_blocks(cr):
        issue(i - 1, zb_ref)
        cr = diagonal_step(i, za_ref, cr)
        _lookahead_loop(i - 1, (zb_ref, za_ref), lambda n, buf: issue(i - 1 - n, buf),
                        lambda n, buf, c, last: step(i - 1 - n, buf, c), cr, finish, primed=True)
        return 0

    def diagonal_only(cr):
        diagonal_step(i, za_ref, cr)
        finish()
        return 0

    lax.cond(i > 0, with_earlier_blocks, diagonal_only, carry)


def _sb(main3d, vt3d):
    b, s, _ = main3d.shape
    nq = s // TQ
    w = HEADS_PER_STEP * HEAD_DIM
    qb, kb, vb = COL_QA // w, COL_KA // w, ROW_VA // w
    return pl.pallas_call(
        _sb_kernel,
        grid=(b, SB_HEADS // HEADS_PER_STEP, nq),
        in_specs=[
            pl.BlockSpec((1, TQ, w), lambda bi, hg, i: (bi, i, qb + hg)),
            pl.BlockSpec((1, s, w), lambda bi, hg, i: (bi, 0, kb + hg)),
            pl.BlockSpec((1, w, s), lambda bi, hg, i: (bi, vb + hg, 0)),
        ],
        out_specs=pl.BlockSpec((1, TQ, w), lambda bi, hg, i: (bi, i, hg)),
        out_shape=jax.ShapeDtypeStruct((b, s, BRANCH_WIDTH), MXU_DTYPE),
        scratch_shapes=[pltpu.VMEM((HEADS_PER_STEP, TK, TQ), F32), pltpu.VMEM((HEADS_PER_STEP, TK, TQ), F32),
                        pltpu.VMEM((HEADS_PER_STEP, HEAD_DIM, TQ), F32)],
        compiler_params=_cparams(("parallel", "parallel", "arbitrary")),
        name="sb",
    )(main3d, main3d, vt3d)


def _fox_kernel(q_ref, k_ref, vt_ref, c_ref, o_ref, cb_ref, za_ref, zb_ref, acc_ref):
    hg = pl.program_id(1)
    i = pl.program_id(2)
    n_heads = HEADS_PER_STEP

    @pl.when(i == 0)
    def _():
        c = c_ref[0]
        lane = lax.broadcasted_iota(jnp.int32, c.shape, 1)
        for h in range(n_heads):
            if n_heads == FOX_HEADS:
                col = c[:, h:h + 1]
            else:
                col = jnp.sum(jnp.where(lane == hg * n_heads + h, c, 0.0), axis=1, keepdims=True)
            cb_ref[h] = jnp.broadcast_to(-col, c.shape)

    heads = []
    for h in range(n_heads):
        t = h // 2
        qh = _head_q(q_ref[0][:, t * LANES:(t + 1) * LANES], h % 2)
        k_at = lambda rows, t=t: k_ref[0, rows, t * LANES:(t + 1) * LANES]
        vt_at = lambda rows, h=h: vt_ref[0, h * HEAD_DIM:(h + 1) * HEAD_DIM, rows]
        bias_at = lambda rows, h=h: [cb_ref[h, rows, :]] * (TQ // LANES)
        heads.append((qh, k_at, vt_at, bias_at))
    _flash_heads(heads, i, _causal_tile_t(False), (za_ref, zb_ref), acc_ref, o_ref)


def _fox(main3d, vt3d, c3d):
    b, s, _ = main3d.shape
    nq = s // TQ
    w = HEADS_PER_STEP * HEAD_DIM
    qb, kb, vb = COL_QB // w, COL_KB // w, ROW_VB // w
    return pl.pallas_call(
        _fox_kernel,
        grid=(b, FOX_HEADS // HEADS_PER_STEP, nq),
        in_specs=[
            pl.BlockSpec((1, TQ, w), lambda bi, hg, i: (bi, i, qb + hg)),
            pl.BlockSpec((1, s, w), lambda bi, hg, i: (bi, 0, kb + hg)),
            pl.BlockSpec((1, w, s), lambda bi, hg, i: (bi, vb + hg, 0)),
            pl.BlockSpec((1, s, LANES), lambda bi, hg, i: (bi, 0, 0)),
        ],
        out_specs=pl.BlockSpec((1, TQ, w), lambda bi, hg, i: (bi, i, hg)),
        out_shape=jax.ShapeDtypeStruct((b, s, BRANCH_WIDTH), MXU_DTYPE),
        scratch_shapes=[pltpu.VMEM((HEADS_PER_STEP, s, LANES), F32),
                        pltpu.VMEM((HEADS_PER_STEP, TK, TQ), F32), pltpu.VMEM((HEADS_PER_STEP, TK, TQ), F32),
                        pltpu.VMEM((HEADS_PER_STEP, HEAD_DIM + SUM_ROWS, TQ), F32)],
        compiler_params=_cparams(("parallel", "parallel", "arbitrary")),
        name="fox",
    )(main3d, main3d, vt3d, c3d)


def _dsa_select(i, qi_ref, ki_ref, misc_ref, key_ref, dig_ref, bias_ref, top_k):
    diag = _causal_tile_t(False)

    wi_t = misc_ref[0].T[MISC_WI:MISC_WI + IDX_HEADS, :] * IDX_W_SCALE
    qih = [_head_q(qi_ref[0][:, (h // 2) * LANES:(h // 2 + 1) * LANES], h % 2) for h in range(IDX_HEADS)]

    def score_block(j, mask):
        kij = ki_ref[0, _rows(j), :]
        sc = None
        for h in range(IDX_HEADS):
            term = wi_t[h:h + 1, :] * jnp.maximum(_dot_nt(kij, qih[h]), 0.0)
            sc = term if sc is None else sc + term
        if mask is not None:
            sc = jnp.where(mask, sc, -jnp.inf)
        bits = lax.bitcast_convert_type(jnp.where(sc == 0.0, 0.0, sc), jnp.int32)
        key_ref[_rows(j), :] = bits ^ ((bits >> 31) & jnp.int32(0x7FFFFFFF))

    def score_pair(t, _):
        score_block(2 * t, None)
        score_block(2 * t + 1, None)
        return 0

    def score_odd_tail():
        score_block(i - 1, None)
        score_block(i, diag)
        return 0

    def score_even_tail():
        score_block(i, diag)
        return 0

    lax.fori_loop(0, i // 2, score_pair, 0)
    lax.cond(i % 2 == 1, score_odd_tail, score_even_tail)

    small = MXU_DTYPE
    group = 2 * SUBLANES

    assert (key_ref.shape[0] // group) <= 256

    def count_digits_ge(cand):
        def hits(j):
            hit = jnp.where(dig_ref[_rows(j), :] >= cand, jnp.ones((), small), jnp.zeros((), small))
            part = hit[0:group]
            for g in range(1, TK // group):
                part = part + hit[g * group:(g + 1) * group]
            return part

        acc = lax.fori_loop(0, (i + 1) // 2, lambda t, a: a + (hits(2 * t) + hits(2 * t + 1)),
                            jnp.zeros((group, TQ), small))
        acc = lax.cond((i + 1) % 2 == 1, lambda a: a + hits(i), lambda a: a, acc)
        return jnp.sum(acc.astype(F32), axis=0, keepdims=True)

    above = jnp.zeros((1, TQ), F32)
    prefix = None
    for shift in (24, 16, 8, 0):
        def fill(j, _, shift=shift, prefix=prefix):
            key = key_ref[_rows(j), :]
            if prefix is None:
                digit = (key >> shift) + 128
            else:
                digit = (key ^ (prefix << (shift + 8))) >> shift
                shares = lax.bitcast_convert_type(digit, jnp.uint32) < jnp.uint32(256)
                digit = jnp.where(shares, digit, -1)
            dig_ref[_rows(j), :] = digit.astype(F32).astype(small)
            return 0

        lax.fori_loop(0, i + 1, fill, 0)

        def search(_, st, above=above):
            lo, hi, n_hi = st
            mid = (lo + hi) * 0.5
            n_mid = count_digits_ge(mid.astype(small))
            ok = above + n_mid >= top_k
            return jnp.where(ok, mid, lo), jnp.where(ok, hi, mid), jnp.where(ok, n_hi, n_mid)

        start = (jnp.zeros((1, TQ), F32), jnp.full((1, TQ), 256.0, F32), jnp.zeros((1, TQ), F32))
        digit, _, n_hi = lax.fori_loop(0, 8, search, start)
        above = above + n_hi
        digit = digit.astype(jnp.int32)
        prefix = digit - 128 if prefix is None else (prefix << 8) | digit
    thr_key = prefix

    need = top_k - above
    kr = lax.broadcasted_iota(jnp.int32, (TK, TK), 0)
    kc = lax.broadcasted_iota(jnp.int32, (TK, TK), 1)
    upto = (kc <= kr).astype(MXU_DTYPE)

    def bias_block(j, mask, seen):
        key = key_ref[_rows(j), :]
        tie = key == thr_key
        rank = _dot(upto, jnp.where(tie, 1.0, 0.0).astype(MXU_DTYPE)) + seen
        sel = (key > thr_key) | (tie & (rank <= need))
        if mask is not None:
            sel = sel & mask
        bias_ref[_rows(j), :] = jnp.where(sel, 0.0, NEG_BIG)
        return rank[TK - 1:TK, :]

    def bias_pair(t, seen):
        return bias_block(2 * t + 1, None, bias_block(2 * t, None, seen))

    def bias_odd_tail(seen):
        bias_block(i, diag, bias_block(i - 1, None, seen))
        return 0

    def bias_even_tail(seen):
        bias_block(i, diag, seen)
        return 0

    seen = lax.fori_loop(0, i // 2, bias_pair, jnp.zeros((1, TQ), F32))
    lax.cond(i % 2 == 1, bias_odd_tail, bias_even_tail, seen)


def _dsa_kernel(q_ref, k_ref, vt_ref, qi_ref, ki_ref, misc_ref, o_ref, key_ref, dig_ref, bias_ref,
                za_ref, zb_ref, acc_ref, *, top_k):
    i = pl.program_id(1)

    def every_causal_key():
        def earlier_block(j, _):
            bias_ref[_rows(j), :] = jnp.zeros((TK, TQ), F32)
            return 0

        lax.fori_loop(0, i, earlier_block, 0)
        bias_ref[_rows(i), :] = jnp.where(_causal_tile_t(False), 0.0, NEG_BIG)
        return 0

    def searched_keys():
        _dsa_select(i, qi_ref, ki_ref, misc_ref, key_ref, dig_ref, bias_ref, top_k)
        return 0

    lax.cond((i + 1) * TQ <= top_k, every_causal_key, searched_keys)

    bias_at = lambda rows: bias_ref[rows, :]
    heads = []
    for h in range(DSA_HEADS):
        g = h // DSA_GROUP
        k_at = lambda rows, g=g: k_ref[0, rows, g * LANES:(g + 1) * LANES]
        vt_at = lambda rows, g=g: vt_ref[0, g * HEAD_DIM:(g + 1) * HEAD_DIM, rows]
        qh = _head_q(q_ref[0][:, (h // 2) * LANES:(h // 2 + 1) * LANES], h % 2)
        heads.append((qh, k_at, vt_at, bias_at))
    _flash_heads(heads, i, None, (za_ref, zb_ref), acc_ref, o_ref)


def _dsa(main3d, vt3d, misc3d, top_k):
    b, s, _ = main3d.shape
    nq = s // TQ
    assert top_k <= TK, "the first key block must hold at least top_k keys"
    return pl.pallas_call(
        functools.partial(_dsa_kernel, top_k=top_k),
        grid=(b, nq),
        in_specs=[
            pl.BlockSpec((1, TQ, 4 * LANES), lambda bi, i: (bi, i, COL_QC // (4 * LANES))),
            pl.BlockSpec((1, s, 2 * LANES), lambda bi, i: (bi, 0, COL_KC // (2 * LANES))),
            pl.BlockSpec((1, LANES, s), lambda bi, i: (bi, ROW_VC // LANES, 0)),
            pl.BlockSpec((1, TQ, 2 * LANES), lambda bi, i: (bi, i, COL_QI // (2 * LANES))),
            pl.BlockSpec((1, s, LANES), lambda bi, i: (bi, 0, COL_KI // LANES)),
            pl.BlockSpec((1, TQ, LANES), lambda bi, i: (bi, i, 0)),
        ],
        out_specs=pl.BlockSpec((1, TQ, BRANCH_WIDTH), lambda bi, i: (bi, i, 0)),
        out_shape=jax.ShapeDtypeStruct((b, s, BRANCH_WIDTH), MXU_DTYPE),
        scratch_shapes=[pltpu.VMEM((s, TQ), jnp.int32), pltpu.VMEM((s, TQ), MXU_DTYPE), pltpu.VMEM((s, TQ), F32),
                        pltpu.VMEM((DSA_HEADS, TK, TQ), F32), pltpu.VMEM((DSA_HEADS, TK, TQ), F32),
                        pltpu.VMEM((DSA_HEADS, HEAD_DIM + SUM_ROWS, TQ), F32)],
        compiler_params=_cparams(("parallel", "arbitrary")),
        name="dsa",
    )(main3d, main3d, vt3d, main3d, main3d, misc3d)


def _merge_kernel(x_ref, a_ref, b_ref, c_ref, g_ref, wg_ref, wb_ref, wo_ref, o_ref):
    x = x_ref[...]
    d = x.shape[1]
    h = _rmsnorm(x, g_ref[...]).astype(MXU_DTYPE)
    y = None
    for n, br_ref in enumerate((a_ref, b_ref, c_ref)):
        gate = jax.nn.sigmoid(_dot(h, wg_ref[:, n * d:(n + 1) * d]))
        term = gate * _dot(br_ref[...], wb_ref[n])
        y = term if y is None else y + term
    o_ref[...] = x + _dot(y.astype(MXU_DTYPE), wo_ref[...])


def _merge(x2d, br_a, br_b, br_c, g, w_gate, w_branch, w_out, tm):
    n, d = x2d.shape
    bw = br_a.shape[1]
    row = lambda i: (i, 0)
    const = lambda i: (0, 0)
    return pl.pallas_call(
        _merge_kernel,
        grid=(n // tm,),
        in_specs=[
            pl.BlockSpec((tm, d), row),
            pl.BlockSpec((tm, bw), row),
            pl.BlockSpec((tm, bw), row),
            pl.BlockSpec((tm, bw), row),
            pl.BlockSpec((1, d), const),
            pl.BlockSpec((d, N_BRANCH * d), const),
            pl.BlockSpec((N_BRANCH, bw, d), lambda i: (0, 0, 0)),
            pl.BlockSpec((d, d), const),
        ],
        out_specs=pl.BlockSpec((tm, d), row),
        out_shape=jax.ShapeDtypeStruct((n, d), F32),
        compiler_params=_cparams(("parallel",)),
        name="merge",
    )(x2d, br_a, br_b, br_c, g, w_gate, w_branch, w_out)


def _mlp_kernel(x_ref, g_ref, wu_ref, wd_ref, gf_ref, o_ref, *, ff_chunk, final_norm):
    x = x_ref[...]
    h = _rmsnorm(x, g_ref[...]).astype(MXU_DTYPE)
    acc = x
    for c in range(wu_ref.shape[1] // ff_chunk):
        u = jnp.maximum(_dot(h, wu_ref[:, c * ff_chunk:(c + 1) * ff_chunk]), 0.0)
        acc = acc + _dot((u * u).astype(MXU_DTYPE), wd_ref[c * ff_chunk:(c + 1) * ff_chunk, :])
    if final_norm:
        acc = _rmsnorm(acc, gf_ref[...])
    o_ref[...] = acc


def _mlp(x2d, g, w_up, w_down, g_final, tm, final_norm):
    n, d = x2d.shape
    ff = w_up.shape[1]
    row = lambda i: (i, 0)
    const = lambda i: (0, 0)
    return pl.pallas_call(
        functools.partial(_mlp_kernel, ff_chunk=min(ff, 1024), final_norm=final_norm),
        grid=(n // tm,),
        in_specs=[
            pl.BlockSpec((tm, d), row),
            pl.BlockSpec((1, d), const),
            pl.BlockSpec((d, ff), const),
            pl.BlockSpec((ff, d), const),
            pl.BlockSpec((1, d), const),
        ],
        out_specs=pl.BlockSpec((tm, d), row),
        out_shape=jax.ShapeDtypeStruct((n, d), F32),
        compiler_params=_cparams(("parallel",)),
        name="mlp",
    )(x2d, g, w_up, w_down, g_final)


def _w_in_columns(d_model):
    sizes = (("qa", BRANCH_WIDTH), ("ka", BRANCH_WIDTH), ("va", BRANCH_WIDTH),
             ("qb", BRANCH_WIDTH), ("kb", BRANCH_WIDTH), ("vb", BRANCH_WIDTH), ("fl", FOX_HEADS),
             ("qc", DSA_HEADS * HEAD_DIM), ("kc", DSA_KV_HEADS * HEAD_DIM), ("vc", DSA_KV_HEADS * HEAD_DIM),
             ("qi", IDX_HEADS * IDX_DIM), ("ki", IDX_DIM), ("wi", IDX_HEADS), ("gates", N_BRANCH * d_model))
    cols, o = {}, 0
    for name, width in sizes:
        cols[name] = (o, o + width)
        o += width
    return cols, o


def _pack_kernel(w_ref, main_ref, vals_ref, gates_ref):
    cols, _ = _w_in_columns(gates_ref.shape[2] // N_BRANCH)

    def src(name, lo=0, hi=None):
        a, b = cols[name]
        return w_ref[0, :, a + lo:(b if hi is None else a + hi)]

    def put(ref, start, value):
        ref[0, :, start:start + value.shape[1]] = value.astype(ref.dtype)

    put(main_ref, COL_QA, src("qa"))
    put(main_ref, COL_KA, src("ka"))
    put(main_ref, COL_QB, src("qb"))
    put(main_ref, COL_KB, src("kb"))
    put(main_ref, COL_QC, src("qc"))
    for g in range(DSA_KV_HEADS):
        kg = src("kc", g * HEAD_DIM, (g + 1) * HEAD_DIM)
        put(main_ref, COL_KC + g * LANES, jnp.concatenate([kg, kg], axis=1))
    put(main_ref, COL_QI, src("qi"))
    ki = src("ki")
    put(main_ref, COL_KI, jnp.concatenate([ki, ki], axis=1))
    put(vals_ref, ROW_VA, src("va"))
    put(vals_ref, ROW_VB, src("vb"))
    put(vals_ref, ROW_VC, src("vc"))
    pad = jnp.zeros((w_ref.shape[1], LANES - FOX_HEADS - IDX_HEADS), F32)
    put(vals_ref, VT_ROWS, jnp.concatenate([src("fl"), src("wi"), pad], axis=1))
    put(gates_ref, 0, src("gates"))


def _pack_w_in(w_in):
    depth, d, total = w_in.shape
    _, end = _w_in_columns(d)
    assert end == total
    rows = 256
    blk = lambda width: pl.BlockSpec((1, rows, width), lambda l, r: (l, r, 0))
    widths = (MAIN_COLS, VALS_COLS, N_BRANCH * d)
    return pl.pallas_call(
        _pack_kernel,
        grid=(depth, d // rows),
        in_specs=[blk(total)],
        out_specs=[blk(wd) for wd in widths],
        out_shape=[jax.ShapeDtypeStruct((depth, d, wd), MXU_DTYPE) for wd in widths],
        compiler_params=_cparams(("parallel", "parallel")),
        name="pack",
    )(w_in)


def _rope_rows():
    lane = jnp.arange(LANES) % HEAD_DIM
    half = ROT_DIM // 2
    inv_freq = ROPE_THETA ** (-jnp.arange(0, ROT_DIM, 2, dtype=F32) / ROT_DIM)
    freq = jnp.where(lane < ROT_DIM, inv_freq[lane % half], 0.0).astype(F32)
    sg1 = jnp.where(lane < half, -1.0, 0.0).astype(F32)
    sg2 = jnp.where((lane >= half) & (lane < ROT_DIM), 1.0, 0.0).astype(F32)
    return freq[None, :], sg1[None, :], sg2[None, :]


def kernel(x, positions, g_mix, w_in, b_forget, w_branch, w_out, g_mlp, w_up, w_down, g_final):
    b, s, d = x.shape
    n = b * s
    depth = w_in.shape[0]
    top_k = min(TOPK_MAX, s // 4)
    tm = min(512, s)
    freq, sg1, sg2 = _rope_rows()
    pos2d = positions.reshape(n, 1)
    x2d = x.reshape(n, d)
    w_main_all, w_v_all, w_gate_all = _pack_w_in(w_in)
    for layer in range(depth):
        w_main, w_v, w_gate = w_main_all[layer], w_v_all[layer], w_gate_all[layer]
        bias_row = jnp.concatenate([b_forget[layer], jnp.zeros((LANES - FOX_HEADS,), F32)])[None, :]
        main, vt, misc = _proj(x2d, pos2d, g_mix[layer][None, :], freq, sg1, sg2, w_main, w_v, tm, b, s)
        main3d = main.reshape(b, s, MAIN_COLS)
        misc3d = misc.reshape(b, s, LANES)
        cum = _cumf(misc3d, bias_row)
        br_a = _sb(main3d, vt).reshape(n, BRANCH_WIDTH)
        br_b = _fox(main3d, vt, cum).reshape(n, BRANCH_WIDTH)
        br_c = _dsa(main3d, vt, misc3d, top_k).reshape(n, BRANCH_WIDTH)
        x2d = _merge(x2d, br_a, br_b, br_c, g_mix[layer][None, :], w_gate,
                     w_branch[layer].astype(MXU_DTYPE), w_out[layer].astype(MXU_DTYPE), tm)
        x2d = _mlp(x2d, g_mlp[layer][None, :], w_up[layer].astype(MXU_DTYPE),
                   w_down[layer].astype(MXU_DTYPE), g_final[None, :], tm, layer == depth - 1)
    return x2d.reshape(b, s, d)
```

```python
import functools

import jax
import jax.numpy as jnp
from jax import lax
from jax.experimental import pallas as pl
from jax.experimental.pallas import tpu as pltpu

F32 = jnp.float32
MXU_DTYPE = jnp.bfloat16

HEAD_DIM = 64
SB_HEADS = 8
FOX_HEADS = 8
DSA_HEADS = 8
DSA_KV_HEADS = 2
DSA_GROUP = DSA_HEADS // DSA_KV_HEADS
IDX_HEADS = 4
IDX_DIM = 64
N_BRANCH = 3
ROPE_THETA = 500000.0
ROT_DIM = HEAD_DIM // 4
TOPK_MAX = 256
NORM_EPS = 1e-6

LANES = 128
SUBLANES = 8
SUM_ROWS = 2 * SUBLANES
BRANCH_WIDTH = SB_HEADS * HEAD_DIM
ATTN_SCALE = HEAD_DIM ** -0.5
assert IDX_DIM == HEAD_DIM
IDX_W_SCALE = IDX_HEADS ** -0.5

COL_QA, COL_KA = 0, 512
COL_QB, COL_KB = 1024, 1536
COL_QC = 2048
COL_KC = 2560
COL_QI = 2816
COL_KI = 3072
MAIN_COLS = 3200
ROPE_START = COL_QC
ROW_VA, ROW_VB, ROW_VC = 0, 512, 1024
VT_ROWS = 1152
VALS_COLS = VT_ROWS + LANES
MISC_FL, MISC_WI = 0, 8

TQ = 256
TK = 256
HEADS_PER_STEP = 8
NEG_BIG = -1e30

VMEM_LIMIT = 56 * 1024 * 1024


def _cparams(sem):
    return pltpu.CompilerParams(dimension_semantics=sem, vmem_limit_bytes=VMEM_LIMIT)


def _rmsnorm(x, g):
    y = x * lax.rsqrt(jnp.mean(x * x, axis=-1, keepdims=True) + NORM_EPS)
    return y * g


def _dot(a, b):
    return jnp.dot(a, b, preferred_element_type=F32)


def _dot_nt(a, b):
    return lax.dot_general(a, b, (((1,), (1,)), ((), ())), preferred_element_type=F32)


def _split2(x):
    hi = x.astype(MXU_DTYPE)
    lo = (x - hi.astype(F32)).astype(MXU_DTYPE)
    return hi, lo


def _split3(x):
    hi = x.astype(MXU_DTYPE)
    r = x - hi.astype(F32)
    mid = r.astype(MXU_DTYPE)
    lo = (r - mid.astype(F32)).astype(MXU_DTYPE)
    return hi, mid, lo


def _log_sigmoid_parts(z):
    neg_abs = lax.bitcast_convert_type(lax.bitcast_convert_type(z, jnp.int32) | jnp.int32(-2 ** 31), F32)
    sp = jnp.log(1.0 + jnp.exp(neg_abs))
    log_sig = jnp.minimum(z, 0.0) - sp
    return log_sig, z - log_sig


def _proj_kernel(x_ref, pos_ref, g_ref, freq_ref, sg1_ref, sg2_ref, wm_ref, wv_ref,
                 main_ref, vt_ref, misc_ref):
    h = _rmsnorm(x_ref[...], g_ref[...]).astype(MXU_DTYPE)
    ang = pos_ref[...].astype(F32) * freq_ref[...]
    cos = jnp.cos(ang)
    sin = jnp.sin(ang)
    s_up = sin * sg1_ref[...]
    s_dn = sin * sg2_ref[...]
    half = ROT_DIM // 2
    chunk = 4 * LANES
    for c0 in range(0, VALS_COLS, 2 * LANES):
        pv = _dot(h, wv_ref[:, c0:c0 + 2 * LANES])
        for r0 in (c0, c0 + LANES):
            piece = pv[:, r0 - c0:r0 - c0 + LANES]
            if r0 < VT_ROWS:
                vt_ref[0, r0:r0 + LANES, :] = piece.T.astype(vt_ref.dtype)
            else:
                misc_ref[...] = piece
    for c0 in list(range(ROPE_START, MAIN_COLS, chunk)) + list(range(0, ROPE_START, chunk)):
        c1 = min(c0 + chunk, MAIN_COLS)
        p = _dot(h, wm_ref[:, c0:c1])
        if c0 >= ROPE_START:
            for t in range((c1 - c0) // LANES):
                pt = p[:, t * LANES:(t + 1) * LANES]
                pt = (pt * cos + pltpu.roll(pt, LANES - half, 1) * s_up
                      + pltpu.roll(pt, half, 1) * s_dn)
                main_ref[:, c0 + t * LANES:c0 + (t + 1) * LANES] = pt.astype(main_ref.dtype)
        else:
            main_ref[:, c0:c1] = p.astype(main_ref.dtype)


def _proj(x2d, pos2d, g, freq, sg1, sg2, w_main, w_v, tm, batch, seq):
    n, d = x2d.shape
    nsb = seq // tm
    const = lambda i: (0, 0)
    return pl.pallas_call(
        _proj_kernel,
        grid=(n // tm,),
        in_specs=[
            pl.BlockSpec((tm, d), lambda i: (i, 0)),
            pl.BlockSpec((tm, 1), lambda i: (i, 0)),
            pl.BlockSpec((1, d), const),
            pl.BlockSpec((1, LANES), const),
            pl.BlockSpec((1, LANES), const),
            pl.BlockSpec((1, LANES), const),
            pl.BlockSpec((d, MAIN_COLS), const),
            pl.BlockSpec((d, VALS_COLS), const),
        ],
        out_specs=[
            pl.BlockSpec((tm, MAIN_COLS), lambda i: (i, 0)),
            pl.BlockSpec((1, VT_ROWS, tm), lambda i: (i // nsb, 0, i % nsb)),
            pl.BlockSpec((tm, LANES), lambda i: (i, 0)),
        ],
        out_shape=[
            jax.ShapeDtypeStruct((n, MAIN_COLS), MXU_DTYPE),
            jax.ShapeDtypeStruct((batch, VT_ROWS, seq), MXU_DTYPE),
            jax.ShapeDtypeStruct((n, LANES), F32),
        ],
        compiler_params=_cparams(("parallel",)),
        name="proj",
    )(x2d, pos2d, g, freq, sg1, sg2, w_main, w_v)


def _cumf_kernel(misc_ref, bias_ref, c_ref):
    s = misc_ref.shape[1]
    r = lax.broadcasted_iota(jnp.int32, (TK, TK), 0)
    c = lax.broadcasted_iota(jnp.int32, (TK, TK), 1)
    incl = (c <= r).astype(MXU_DTYPE)
    carry = jnp.zeros((1, LANES), F32)
    for b in range(s // TK):
        logit = misc_ref[0, b * TK:(b + 1) * TK, :] + bias_ref[...]
        logf, _ = _log_sigmoid_parts(logit)
        hi, mid, lo = _split3(logf)
        cs = (_dot(incl, lo) + _dot(incl, mid)) + _dot(incl, hi) + carry
        c_ref[0, b * TK:(b + 1) * TK, :] = cs
        carry = cs[TK - 1:TK, :]


def _cumf(misc3d, bias_row):
    b, s, _ = misc3d.shape
    return pl.pallas_call(
        _cumf_kernel,
        grid=(b,),
        in_specs=[
            pl.BlockSpec((1, s, LANES), lambda i: (i, 0, 0)),
            pl.BlockSpec((1, LANES), lambda i: (0, 0)),
        ],
        out_specs=pl.BlockSpec((1, s, LANES), lambda i: (i, 0, 0)),
        out_shape=jax.ShapeDtypeStruct((b, s, LANES), F32),
        compiler_params=_cparams(("parallel",)),
        name="cumf",
    )(misc3d, bias_row)


def _head_q(q_tile, half):
    lane = lax.broadcasted_iota(jnp.int32, q_tile.shape, 1)
    keep = (lane // HEAD_DIM) == half
    return jnp.where(keep, q_tile.astype(F32) * ATTN_SCALE, 0.0).astype(MXU_DTYPE)


def _causal_tile_t(strict):
    key = lax.broadcasted_iota(jnp.int32, (TK, TQ), 0)
    qry = lax.broadcasted_iota(jnp.int32, (TK, TQ), 1)
    return (key < qry) if strict else (key <= qry)


def _rows(j):
    return pl.ds(pl.multiple_of(j * TK, TK), TK)


def _lookahead_loop(n_last, bufs, issue, step, carry, finish, primed=False):
    buf_a, buf_b = bufs

    def pair(t, carry):
        n = 2 * t
        issue(n + 1, buf_b)
        carry = step(n, buf_a, carry, False)
        issue(n + 2, buf_a)
        return step(n + 1, buf_b, carry, False)

    def odd_tail(carry):
        issue(n_last, buf_b)
        carry = step(n_last - 1, buf_a, carry, False)
        step(n_last, buf_b, carry, True)
        finish()
        return 0

    def even_tail(carry):
        step(n_last, buf_a, carry, True)
        finish()
        return 0

    if not primed:
        issue(0, buf_a)
    carry = lax.fori_loop(0, n_last // 2, pair, carry)
    lax.cond(n_last % 2 == 1, odd_tail, even_tail, carry)


def _flash_heads(heads, n_full, diag_mask, z_bufs, acc_ref, o_ref):
    half_k, half_q = TK // 2, TQ // 2

    def add_bias(s, bias):
        if isinstance(bias, (list, tuple)):
            return jnp.concatenate([s[:, t * LANES:(t + 1) * LANES] + bias[t] for t in range(len(bias))],
                                   axis=1)
        return s + bias

    def later_queries(bias):
        return bias[half_q // LANES:] if isinstance(bias, (list, tuple)) else bias[:, half_q:]

    ones = jnp.ones((SUM_ROWS, TK), MXU_DTYPE)

    def issue(j, buf):
        for h, (qh, k_at, _, _) in enumerate(heads):
            buf[h] = _dot_nt(k_at(_rows(j)), qh)

    def update(tiles, rows, maxes, lanes):
        stats = []
        for s, m in zip(tiles, maxes):
            m_new = jnp.maximum(m, jnp.max(s, axis=0, keepdims=True))
            stats.append((m_new, jnp.exp(m - m_new), jnp.exp(s - m_new).astype(MXU_DTYPE)))
        for h, ((_, alpha, p), (_, _, vt_at, _)) in enumerate(zip(stats, heads)):
            vt1 = jnp.concatenate([vt_at(rows), ones[:, :p.shape[0]]], axis=0)
            acc_ref[h, :, lanes] = alpha * acc_ref[h, :, lanes] + _dot(vt1, p)
        return tuple(m_new for m_new, _, _ in stats)

    def step(j, buf, maxes):
        rows = _rows(j)
        tiles = [add_bias(buf[h], bias_at(rows)) for h, (_, _, _, bias_at) in enumerate(heads)]
        return update(tiles, rows, maxes, slice(None))

    def diagonal_step(j, buf, maxes):
        start = pl.multiple_of(j * TK, TK)
        early, late = pl.ds(start, half_k), pl.ds(pl.multiple_of(start + half_k, half_k), half_k)

        def masked(s, mask):
            return s if diag_mask is None else jnp.where(mask, s, NEG_BIG)

        tiles = [masked(add_bias(buf[h, :half_k, :], bias_at(early)), None if diag_mask is None
                        else diag_mask[:half_k, :]) for h, (_, _, _, bias_at) in enumerate(heads)]
        maxes = update(tiles, early, maxes, slice(None))
        tiles = [masked(add_bias(buf[h, half_k:, half_q:], later_queries(bias_at(late))), None if diag_mask is None
                        else diag_mask[half_k:, half_q:]) for h, (_, _, _, bias_at) in enumerate(heads)]
        late_maxes = update(tiles, late, tuple(m[:, half_q:] for m in maxes), slice(half_q, TQ))
        return tuple(jnp.concatenate([m[:, :half_q], lm], axis=1) for m, lm in zip(maxes, late_maxes))

    def finish():
        _store_heads_t(o_ref, [acc_ref[h, :HEAD_DIM, :] / acc_ref[h, HEAD_DIM:HEAD_DIM + 1, :]
                               for h in range(len(heads))])

    z_a, z_b = z_bufs
    acc_ref[...] = jnp.zeros(acc_ref.shape, F32)
    init = tuple(jnp.full((1, TQ), NEG_BIG, F32) for _ in heads)
    issue(n_full, z_a)

    def with_earlier_blocks(maxes):
        issue(0, z_b)
        maxes = diagonal_step(n_full, z_a, maxes)
        _lookahead_loop(n_full - 1, (z_b, z_a), issue, lambda j, buf, mx, last: step(j, buf, mx), maxes,
                        finish, primed=True)
        return 0

    def diagonal_only(maxes):
        diagonal_step(n_full, z_a, maxes)
        finish()
        return 0

    lax.cond(n_full > 0, with_earlier_blocks, diagonal_only, init)


def _store_heads_t(o_ref, outs_t):
    for t in range(len(outs_t) // 2):
        pair_t = jnp.concatenate([outs_t[2 * t], outs_t[2 * t + 1]], axis=0)
        o_ref[0, :, t * LANES:(t + 1) * LANES] = pair_t.T.astype(o_ref.dtype)


def _sb_kernel(q_ref, k_ref, vt_ref, o_ref, za_ref, zb_ref, acc_ref):
    i = pl.program_id(2)
    r = lax.broadcasted_iota(jnp.int32, (TK, TK), 0)
    c = lax.broadcasted_iota(jnp.int32, (TK, TK), 1)
    after = (c > r).astype(MXU_DTYPE)
    strict = _causal_tile_t(True)
    n_heads = HEADS_PER_STEP
    qhs = [_head_q(q_ref[0][:, (h // 2) * LANES:(h // 2 + 1) * LANES], h % 2) for h in range(n_heads)]

    def issue(j, buf):
        for h in range(n_heads):
            buf[h] = _dot_nt(k_ref[0, _rows(j), (h // 2) * LANES:(h // 2 + 1) * LANES], qhs[h])

    def update(tiles, rows, tails, lanes, mask):
        n_rows = tiles[0].shape[0]
        parts = []
        for z in tiles:
            log_beta, neg_l1m = _log_sigmoid_parts(z)
            if mask is not None:
                neg_l1m = jnp.where(mask, neg_l1m, 0.0)
            hi, lo = _split2(neg_l1m)
            parts.append((log_beta, neg_l1m[0:1, :], jnp.concatenate([hi, lo], axis=0)))
        after_rows = jnp.concatenate([after[:n_rows, :n_rows]] * 2, axis=1)
        laters = [_dot(after_rows, hl) for (_, _, hl) in parts]
        ws = []
        for (log_beta, _, _), later in zip(parts, laters):
            w = jnp.exp(log_beta - later)
            if mask is not None:
                w = jnp.where(mask, w, 0.0)
            ws.append(w.astype(MXU_DTYPE))
        out = []
        for h, (w, (_, first, _), later, tail) in enumerate(zip(ws, parts, laters, tails)):
            vtj = vt_ref[0, h * HEAD_DIM:(h + 1) * HEAD_DIM, rows]
            acc_ref[h, :, lanes] = acc_ref[h, :, lanes] + jnp.exp(-tail) * _dot(vtj, w)
            out.append(tail + (first + later[0:1, :]))
        return tuple(out)

    def step(j, buf, tails):
        return update([buf[h] for h in range(n_heads)], _rows(j), tails, slice(None), None)

    def diagonal_step(j, buf, tails):
        half_k, half_q = TK // 2, TQ // 2
        start = pl.multiple_of(j * TK, TK)
        early, late = pl.ds(start, half_k), pl.ds(pl.multiple_of(start + half_k, half_k), half_k)
        late_tails = update([buf[h, half_k:, half_q:] for h in range(n_heads)], late,
                            tuple(t[:, half_q:] for t in tails), slice(half_q, TQ), strict[half_k:, half_q:])
        tails = tuple(jnp.concatenate([t[:, :half_q], lt], axis=1) for t, lt in zip(tails, late_tails))
        return update([buf[h, :half_k, :] for h in range(n_heads)], early, tails, slice(None), strict[:half_k, :])

    acc_ref[...] = jnp.zeros(acc_ref.shape, F32)
    carry = tuple(jnp.zeros((1, TQ), F32) for _ in range(n_heads))
    issue(i, za_ref)

    def finish():
        _store_heads_t(o_ref, [acc_ref[h] for h in range(n_heads)])

    def with_earlier_blocks(cr):
        issue(i - 1, zb_ref)
        cr = diagonal_step(i, za_ref, cr)
        _lookahead_loop(i - 1, (zb_ref, za_ref), lambda n, buf: issue(i - 1 - n, buf),
                        lambda n, buf, c, last: step(i - 1 - n, buf, c), cr, finish, primed=True)
        return 0

    def diagonal_only(cr):
        diagonal_step(i, za_ref, cr)
        finish()
        return 0

    lax.cond(i > 0, with_earlier_blocks, diagonal_only, carry)


def _sb(main3d, vt3d):
    b, s, _ = main3d.shape
    nq = s // TQ
    w = HEADS_PER_STEP * HEAD_DIM
    qb, kb, vb = COL_QA // w, COL_KA // w, ROW_VA // w
    return pl.pallas_call(
        _sb_kernel,
        grid=(b, SB_HEADS // HEADS_PER_STEP, nq),
        in_specs=[
            pl.BlockSpec((1, TQ, w), lambda bi, hg, i: (bi, i, qb + hg)),
            pl.BlockSpec((1, s, w), lambda bi, hg, i: (bi, 0, kb + hg)),
            pl.BlockSpec((1, w, s), lambda bi, hg, i: (bi, vb + hg, 0)),
        ],
        out_specs=pl.BlockSpec((1, TQ, w), lambda bi, hg, i: (bi, i, hg)),
        out_shape=jax.ShapeDtypeStruct((b, s, BRANCH_WIDTH), MXU_DTYPE),
        scratch_shapes=[pltpu.VMEM((HEADS_PER_STEP, TK, TQ), F32), pltpu.VMEM((HEADS_PER_STEP, TK, TQ), F32),
                        pltpu.VMEM((HEADS_PER_STEP, HEAD_DIM, TQ), F32)],
        compiler_params=_cparams(("parallel", "parallel", "arbitrary")),
        name="sb",
    )(main3d, main3d, vt3d)


def _fox_kernel(q_ref, k_ref, vt_ref, c_ref, o_ref, cb_ref, za_ref, zb_ref, acc_ref):
    hg = pl.program_id(1)
    i = pl.program_id(2)
    n_heads = HEADS_PER_STEP

    @pl.when(i == 0)
    def _():
        c = c_ref[0]
        lane = lax.broadcasted_iota(jnp.int32, c.shape, 1)
        for h in range(n_heads):
            if n_heads == FOX_HEADS:
                col = c[:, h:h + 1]
            else:
                col = jnp.sum(jnp.where(lane == hg * n_heads + h, c, 0.0), axis=1, keepdims=True)
            cb_ref[h] = jnp.broadcast_to(-col, c.shape)

    heads = []
    for h in range(n_heads):
        t = h // 2
        qh = _head_q(q_ref[0][:, t * LANES:(t + 1) * LANES], h % 2)
        k_at = lambda rows, t=t: k_ref[0, rows, t * LANES:(t + 1) * LANES]
        vt_at = lambda rows, h=h: vt_ref[0, h * HEAD_DIM:(h + 1) * HEAD_DIM, rows]
        bias_at = lambda rows, h=h: [cb_ref[h, rows, :]] * (TQ // LANES)
        heads.append((qh, k_at, vt_at, bias_at))
    _flash_heads(heads, i, _causal_tile_t(False), (za_ref, zb_ref), acc_ref, o_ref)


def _fox(main3d, vt3d, c3d):
    b, s, _ = main3d.shape
    nq = s // TQ
    w = HEADS_PER_STEP * HEAD_DIM
    qb, kb, vb = COL_QB // w, COL_KB // w, ROW_VB // w
    return pl.pallas_call(
        _fox_kernel,
        grid=(b, FOX_HEADS // HEADS_PER_STEP, nq),
        in_specs=[
            pl.BlockSpec((1, TQ, w), lambda bi, hg, i: (bi, i, qb + hg)),
            pl.BlockSpec((1, s, w), lambda bi, hg, i: (bi, 0, kb + hg)),
            pl.BlockSpec((1, w, s), lambda bi, hg, i: (bi, vb + hg, 0)),
            pl.BlockSpec((1, s, LANES), lambda bi, hg, i: (bi, 0, 0)),
        ],
        out_specs=pl.BlockSpec((1, TQ, w), lambda bi, hg, i: (bi, i, hg)),
        out_shape=jax.ShapeDtypeStruct((b, s, BRANCH_WIDTH), MXU_DTYPE),
        scratch_shapes=[pltpu.VMEM((HEADS_PER_STEP, s, LANES), F32),
                        pltpu.VMEM((HEADS_PER_STEP, TK, TQ), F32), pltpu.VMEM((HEADS_PER_STEP, TK, TQ), F32),
                        pltpu.VMEM((HEADS_PER_STEP, HEAD_DIM + SUM_ROWS, TQ), F32)],
        compiler_params=_cparams(("parallel", "parallel", "arbitrary")),
        name="fox",
    )(main3d, main3d, vt3d, c3d)


def _dsa_select(i, qi_ref, ki_ref, misc_ref, key_ref, dig_ref, bias_ref, top_k):
    diag = _causal_tile_t(False)

    wi_t = misc_ref[0].T[MISC_WI:MISC_WI + IDX_HEADS, :] * IDX_W_SCALE
    qih = [_head_q(qi_ref[0][:, (h // 2) * LANES:(h // 2 + 1) * LANES], h % 2) for h in range(IDX_HEADS)]

    def score_block(j, mask):
        kij = ki_ref[0, _rows(j), :]
        sc = None
        for h in range(IDX_HEADS):
            term = wi_t[h:h + 1, :] * jnp.maximum(_dot_nt(kij, qih[h]), 0.0)
            sc = term if sc is None else sc + term
        if mask is not None:
            sc = jnp.where(mask, sc, -jnp.inf)
        bits = lax.bitcast_convert_type(jnp.where(sc == 0.0, 0.0, sc), jnp.int32)
        key_ref[_rows(j), :] = bits ^ ((bits >> 31) & jnp.int32(0x7FFFFFFF))

    def score_pair(t, _):
        score_block(2 * t, None)
        score_block(2 * t + 1, None)
        return 0

    def score_odd_tail():
        score_block(i - 1, None)
        score_block(i, diag)
        return 0

    def score_even_tail():
        score_block(i, diag)
        return 0

    lax.fori_loop(0, i // 2, score_pair, 0)
    lax.cond(i % 2 == 1, score_odd_tail, score_even_tail)

    small = MXU_DTYPE
    group = 2 * SUBLANES

    assert (key_ref.shape[0] // group) <= 256

    def count_digits_ge(cand):
        def hits(j):
            hit = jnp.where(dig_ref[_rows(j), :] >= cand, jnp.ones((), small), jnp.zeros((), small))
            part = hit[0:group]
            for g in range(1, TK // group):
                part = part + hit[g * group:(g + 1) * group]
            return part

        acc = lax.fori_loop(0, (i + 1) // 2, lambda t, a: a + (hits(2 * t) + hits(2 * t + 1)),
                            jnp.zeros((group, TQ), small))
        acc = lax.cond((i + 1) % 2 == 1, lambda a: a + hits(i), lambda a: a, acc)
        return jnp.sum(acc.astype(F32), axis=0, keepdims=True)

    above = jnp.zeros((1, TQ), F32)
    prefix = None
    for shift in (24, 16, 8, 0):
        def fill(j, _, shift=shift, prefix=prefix):
            key = key_ref[_rows(j), :]
            if prefix is None:
                digit = (key >> shift) + 128
            else:
                digit = (key ^ (prefix << (shift + 8))) >> shift
                shares = lax.bitcast_convert_type(digit, jnp.uint32) < jnp.uint32(256)
                digit = jnp.where(shares, digit, -1)
            dig_ref[_rows(j), :] = digit.astype(F32).astype(small)
            return 0

        lax.fori_loop(0, i + 1, fill, 0)

        def search(_, st, above=above):
            lo, hi, n_hi = st
            mid = (lo + hi) * 0.5
            n_mid = count_digits_ge(mid.astype(small))
            ok = above + n_mid >= top_k
            return jnp.where(ok, mid, lo), jnp.where(ok, hi, mid), jnp.where(ok, n_hi, n_mid)

        start = (jnp.zeros((1, TQ), F32), jnp.full((1, TQ), 256.0, F32), jnp.zeros((1, TQ), F32))
        digit, _, n_hi = lax.fori_loop(0, 8, search, start)
        above = above + n_hi
        digit = digit.astype(jnp.int32)
        prefix = digit - 128 if prefix is None else (prefix << 8) | digit
    thr_key = prefix

    need = top_k - above
    kr = lax.broadcasted_iota(jnp.int32, (TK, TK), 0)
    kc = lax.broadcasted_iota(jnp.int32, (TK, TK), 1)
    upto = (kc <= kr).astype(MXU_DTYPE)

    def bias_block(j, mask, seen):
        key = key_ref[_rows(j), :]
        tie = key == thr_key
        rank = _dot(upto, jnp.where(tie, 1.0, 0.0).astype(MXU_DTYPE)) + seen
        sel = (key > thr_key) | (tie & (rank <= need))
        if mask is not None:
            sel = sel & mask
        bias_ref[_rows(j), :] = jnp.where(sel, 0.0, NEG_BIG)
        return rank[TK - 1:TK, :]

    def bias_pair(t, seen):
        return bias_block(2 * t + 1, None, bias_block(2 * t, None, seen))

    def bias_odd_tail(seen):
        bias_block(i, diag, bias_block(i - 1, None, seen))
        return 0

    def bias_even_tail(seen):
        bias_block(i, diag, seen)
        return 0

    seen = lax.fori_loop(0, i // 2, bias_pair, jnp.zeros((1, TQ), F32))
    lax.cond(i % 2 == 1, bias_odd_tail, bias_even_tail, seen)


def _dsa_kernel(q_ref, k_ref, vt_ref, qi_ref, ki_ref, misc_ref, o_ref, key_ref, dig_ref, bias_ref,
                za_ref, zb_ref, acc_ref, *, top_k):
    i = pl.program_id(1)

    def every_causal_key():
        def earlier_block(j, _):
            bias_ref[_rows(j), :] = jnp.zeros((TK, TQ), F32)
            return 0

        lax.fori_loop(0, i, earlier_block, 0)
        bias_ref[_rows(i), :] = jnp.where(_causal_tile_t(False), 0.0, NEG_BIG)
        return 0

    def searched_keys():
        _dsa_select(i, qi_ref, ki_ref, misc_ref, key_ref, dig_ref, bias_ref, top_k)
        return 0

    lax.cond((i + 1) * TQ <= top_k, every_causal_key, searched_keys)

    bias_at = lambda rows: bias_ref[rows, :]
    heads = []
    for h in range(DSA_HEADS):
        g = h // DSA_GROUP
        k_at = lambda rows, g=g: k_ref[0, rows, g * LANES:(g + 1) * LANES]
        vt_at = lambda rows, g=g: vt_ref[0, g * HEAD_DIM:(g + 1) * HEAD_DIM, rows]
        qh = _head_q(q_ref[0][:, (h // 2) * LANES:(h // 2 + 1) * LANES], h % 2)
        heads.append((qh, k_at, vt_at, bias_at))
    _flash_heads(heads, i, None, (za_ref, zb_ref), acc_ref, o_ref)


def _dsa(main3d, vt3d, misc3d, top_k):
    b, s, _ = main3d.shape
    nq = s // TQ
    assert top_k <= TK, "the first key block must hold at least top_k keys"
    return pl.pallas_call(
        functools.partial(_dsa_kernel, top_k=top_k),
        grid=(b, nq),
        in_specs=[
            pl.BlockSpec((1, TQ, 4 * LANES), lambda bi, i: (bi, i, COL_QC // (4 * LANES))),
            pl.BlockSpec((1, s, 2 * LANES), lambda bi, i: (bi, 0, COL_KC // (2 * LANES))),
            pl.BlockSpec((1, LANES, s), lambda bi, i: (bi, ROW_VC // LANES, 0)),
            pl.BlockSpec((1, TQ, 2 * LANES), lambda bi, i: (bi, i, COL_QI // (2 * LANES))),
            pl.BlockSpec((1, s, LANES), lambda bi, i: (bi, 0, COL_KI // LANES)),
            pl.BlockSpec((1, TQ, LANES), lambda bi, i: (bi, i, 0)),
        ],
        out_specs=pl.BlockSpec((1, TQ, BRANCH_WIDTH), lambda bi, i: (bi, i, 0)),
        out_shape=jax.ShapeDtypeStruct((b, s, BRANCH_WIDTH), MXU_DTYPE),
        scratch_shapes=[pltpu.VMEM((s, TQ), jnp.int32), pltpu.VMEM((s, TQ), MXU_DTYPE), pltpu.VMEM((s, TQ), F32),
                        pltpu.VMEM((DSA_HEADS, TK, TQ), F32), pltpu.VMEM((DSA_HEADS, TK, TQ), F32),
                        pltpu.VMEM((DSA_HEADS, HEAD_DIM + SUM_ROWS, TQ), F32)],
        compiler_params=_cparams(("parallel", "arbitrary")),
        name="dsa",
    )(main3d, main3d, vt3d, main3d, main3d, misc3d)


def _merge_kernel(x_ref, a_ref, b_ref, c_ref, g_ref, wg_ref, wb_ref, wo_ref, o_ref):
    x = x_ref[...]
    d = x.shape[1]
    h = _rmsnorm(x, g_ref[...]).astype(MXU_DTYPE)
    y = None
    for n, br_ref in enumerate((a_ref, b_ref, c_ref)):
        gate = jax.nn.sigmoid(_dot(h, wg_ref[:, n * d:(n + 1) * d]))
        term = gate * _dot(br_ref[...], wb_ref[n])
        y = term if y is None else y + term
    o_ref[...] = x + _dot(y.astype(MXU_DTYPE), wo_ref[...])


def _merge(x2d, br_a, br_b, br_c, g, w_gate, w_branch, w_out, tm):
    n, d = x2d.shape
    bw = br_a.shape[1]
    row = lambda i: (i, 0)
    const = lambda i: (0, 0)
    return pl.pallas_call(
        _merge_kernel,
        grid=(n // tm,),
        in_specs=[
            pl.BlockSpec((tm, d), row),
            pl.BlockSpec((tm, bw), row),
            pl.BlockSpec((tm, bw), row),
            pl.BlockSpec((tm, bw), row),
            pl.BlockSpec((1, d), const),
            pl.BlockSpec((d, N_BRANCH * d), const),
            pl.BlockSpec((N_BRANCH, bw, d), lambda i: (0, 0, 0)),
            pl.BlockSpec((d, d), const),
        ],
        out_specs=pl.BlockSpec((tm, d), row),
        out_shape=jax.ShapeDtypeStruct((n, d), F32),
        compiler_params=_cparams(("parallel",)),
        name="merge",
    )(x2d, br_a, br_b, br_c, g, w_gate, w_branch, w_out)


def _mlp_kernel(x_ref, g_ref, wu_ref, wd_ref, gf_ref, o_ref, *, ff_chunk, final_norm):
    x = x_ref[...]
    h = _rmsnorm(x, g_ref[...]).astype(MXU_DTYPE)
    acc = x
    for c in range(wu_ref.shape[1] // ff_chunk):
        u = jnp.maximum(_dot(h, wu_ref[:, c * ff_chunk:(c + 1) * ff_chunk]), 0.0)
        acc = acc + _dot((u * u).astype(MXU_DTYPE), wd_ref[c * ff_chunk:(c + 1) * ff_chunk, :])
    if final_norm:
        acc = _rmsnorm(acc, gf_ref[...])
    o_ref[...] = acc


def _mlp(x2d, g, w_up, w_down, g_final, tm, final_norm):
    n, d = x2d.shape
    ff = w_up.shape[1]
    row = lambda i: (i, 0)
    const = lambda i: (0, 0)
    return pl.pallas_call(
        functools.partial(_mlp_kernel, ff_chunk=min(ff, 1024), final_norm=final_norm),
        grid=(n // tm,),
        in_specs=[
            pl.BlockSpec((tm, d), row),
            pl.BlockSpec((1, d), const),
            pl.BlockSpec((d, ff), const),
            pl.BlockSpec((ff, d), const),
            pl.BlockSpec((1, d), const),
        ],
        out_specs=pl.BlockSpec((tm, d), row),
        out_shape=jax.ShapeDtypeStruct((n, d), F32),
        compiler_params=_cparams(("parallel",)),
        name="mlp",
    )(x2d, g, w_up, w_down, g_final)


def _w_in_columns(d_model):
    sizes = (("qa", BRANCH_WIDTH), ("ka", BRANCH_WIDTH), ("va", BRANCH_WIDTH),
             ("qb", BRANCH_WIDTH), ("kb", BRANCH_WIDTH), ("vb", BRANCH_WIDTH), ("fl", FOX_HEADS),
             ("qc", DSA_HEADS * HEAD_DIM), ("kc", DSA_KV_HEADS * HEAD_DIM), ("vc", DSA_KV_HEADS * HEAD_DIM),
             ("qi", IDX_HEADS * IDX_DIM), ("ki", IDX_DIM), ("wi", IDX_HEADS), ("gates", N_BRANCH * d_model))
    cols, o = {}, 0
    for name, width in sizes:
        cols[name] = (o, o + width)
        o += width
    return cols, o


def _pack_kernel(w_ref, main_ref, vals_ref, gates_ref):
    cols, _ = _w_in_columns(gates_ref.shape[2] // N_BRANCH)

    def src(name, lo=0, hi=None):
        a, b = cols[name]
        return w_ref[0, :, a + lo:(b if hi is None else a + hi)]

    def put(ref, start, value):
        ref[0, :, start:start + value.shape[1]] = value.astype(ref.dtype)

    put(main_ref, COL_QA, src("qa"))
    put(main_ref, COL_KA, src("ka"))
    put(main_ref, COL_QB, src("qb"))
    put(main_ref, COL_KB, src("kb"))
    put(main_ref, COL_QC, src("qc"))
    for g in range(DSA_KV_HEADS):
        kg = src("kc", g * HEAD_DIM, (g + 1) * HEAD_DIM)
        put(main_ref, COL_KC + g * LANES, jnp.concatenate([kg, kg], axis=1))
    put(main_ref, COL_QI, src("qi"))
    ki = src("ki")
    put(main_ref, COL_KI, jnp.concatenate([ki, ki], axis=1))
    put(vals_ref, ROW_VA, src("va"))
    put(vals_ref, ROW_VB, src("vb"))
    put(vals_ref, ROW_VC, src("vc"))
    pad = jnp.zeros((w_ref.shape[1], LANES - FOX_HEADS - IDX_HEADS), F32)
    put(vals_ref, VT_ROWS, jnp.concatenate([src("fl"), src("wi"), pad], axis=1))
    put(gates_ref, 0, src("gates"))


def _pack_w_in(w_in):
    depth, d, total = w_in.shape
    _, end = _w_in_columns(d)
    assert end == total
    rows = 256
    blk = lambda width: pl.BlockSpec((1, rows, width), lambda l, r: (l, r, 0))
    widths = (MAIN_COLS, VALS_COLS, N_BRANCH * d)
    return pl.pallas_call(
        _pack_kernel,
        grid=(depth, d // rows),
        in_specs=[blk(total)],
        out_specs=[blk(wd) for wd in widths],
        out_shape=[jax.ShapeDtypeStruct((depth, d, wd), MXU_DTYPE) for wd in widths],
        compiler_params=_cparams(("parallel", "parallel")),
        name="pack",
    )(w_in)


def _rope_rows():
    lane = jnp.arange(LANES) % HEAD_DIM
    half = ROT_DIM // 2
    inv_freq = ROPE_THETA ** (-jnp.arange(0, ROT_DIM, 2, dtype=F32) / ROT_DIM)
    freq = jnp.where(lane < ROT_DIM, inv_freq[lane % half], 0.0).astype(F32)
    sg1 = jnp.where(lane < half, -1.0, 0.0).astype(F32)
    sg2 = jnp.where((lane >= half) & (lane < ROT_DIM), 1.0, 0.0).astype(F32)
    return freq[None, :], sg1[None, :], sg2[None, :]


def kernel(x, positions, g_mix, w_in, b_forget, w_branch, w_out, g_mlp, w_up, w_down, g_final):
    b, s, d = x.shape
    n = b * s
    depth = w_in.shape[0]
    top_k = min(TOPK_MAX, s // 4)
    tm = min(512, s)
    freq, sg1, sg2 = _rope_rows()
    pos2d = positions.reshape(n, 1)
    x2d = x.reshape(n, d)
    w_main_all, w_v_all, w_gate_all = _pack_w_in(w_in)
    for layer in range(depth):
        w_main, w_v, w_gate = w_main_all[layer], w_v_all[layer], w_gate_all[layer]
        bias_row = jnp.concatenate([b_forget[layer], jnp.zeros((LANES - FOX_HEADS,), F32)])[None, :]
        main, vt, misc = _proj(x2d, pos2d, g_mix[layer][None, :], freq, sg1, sg2, w_main, w_v, tm, b, s)
        main3d = main.reshape(b, s, MAIN_COLS)
        misc3d = misc.reshape(b, s, LANES)
        cum = _cumf(misc3d, bias_row)
        br_a = _sb(main3d, vt).reshape(n, BRANCH_WIDTH)
        br_b = _fox(main3d, vt, cum).reshape(n, BRANCH_WIDTH)
        br_c = _dsa(main3d, vt, misc3d, top_k).reshape(n, BRANCH_WIDTH)
        x2d = _merge(x2d, br_a, br_b, br_c, g_mix[layer][None, :], w_gate,
                     w_branch[layer].astype(MXU_DTYPE), w_out[layer].astype(MXU_DTYPE), tm)
        x2d = _mlp(x2d, g_mlp[layer][None, :], w_up[layer].astype(MXU_DTYPE),
                   w_down[layer].astype(MXU_DTYPE), g_final[None, :], tm, layer == depth - 1)
    return x2d.reshape(b, s, d)
```

```python
import functools

import jax
import jax.numpy as jnp
from jax import lax
from jax.experimental import pallas as pl
from jax.experimental.pallas import tpu as pltpu

F32 = jnp.float32
MXU_DTYPE = jnp.bfloat16

HEAD_DIM = 64
SB_HEADS = 8
FOX_HEADS = 8
DSA_HEADS = 8
DSA_KV_HEADS = 2
DSA_GROUP = DSA_HEADS // DSA_KV_HEADS
IDX_HEADS = 4
IDX_DIM = 64
N_BRANCH = 3
ROPE_THETA = 500000.0
ROT_DIM = HEAD_DIM // 4
TOPK_MAX = 256
NORM_EPS = 1e-6

LANES = 128
SUBLANES = 8
SUM_ROWS = 2 * SUBLANES
BRANCH_WIDTH = SB_HEADS * HEAD_DIM
ATTN_SCALE = HEAD_DIM ** -0.5
assert IDX_DIM == HEAD_DIM
IDX_W_SCALE = IDX_HEADS ** -0.5

COL_QA, COL_KA = 0, 512
COL_QB, COL_KB = 1024, 1536
COL_QC = 2048
COL_KC = 2560
COL_QI = 2816
COL_KI = 3072
MAIN_COLS = 3200
ROPE_START = COL_QC
ROW_VA, ROW_VB, ROW_VC = 0, 512, 1024
VT_ROWS = 1152
VALS_COLS = VT_ROWS + LANES
MISC_FL, MISC_WI = 0, 8

TQ = 256
TK = 256
HEADS_PER_STEP = 8
NEG_BIG = -1e30

VMEM_LIMIT = 56 * 1024 * 1024


def _cparams(sem):
    return pltpu.CompilerParams(dimension_semantics=sem, vmem_limit_bytes=VMEM_LIMIT)


def _rmsnorm(x, g):
    y = x * lax.rsqrt(jnp.mean(x * x, axis=-1, keepdims=True) + NORM_EPS)
    return y * g


def _dot(a, b):
    return jnp.dot(a, b, preferred_element_type=F32)


def _dot_nt(a, b):
    return lax.dot_general(a, b, (((1,), (1,)), ((), ())), preferred_element_type=F32)


def _split2(x):
    hi = x.astype(MXU_DTYPE)
    lo = (x - hi.astype(F32)).astype(MXU_DTYPE)
    return hi, lo


def _split3(x):
    hi = x.astype(MXU_DTYPE)
    r = x - hi.astype(F32)
    mid = r.astype(MXU_DTYPE)
    lo = (r - mid.astype(F32)).astype(MXU_DTYPE)
    return hi, mid, lo


def _log_sigmoid_parts(z):
    neg_abs = lax.bitcast_convert_type(lax.bitcast_convert_type(z, jnp.int32) | jnp.int32(-2 ** 31), F32)
    sp = jnp.log(1.0 + jnp.exp(neg_abs))
    log_sig = jnp.minimum(z, 0.0) - sp
    return log_sig, z - log_sig


def _proj_kernel(x_ref, pos_ref, g_ref, freq_ref, sg1_ref, sg2_ref, wm_ref, wv_ref,
                 main_ref, vt_ref, misc_ref):
    h = _rmsnorm(x_ref[...], g_ref[...]).astype(MXU_DTYPE)
    ang = pos_ref[...].astype(F32) * freq_ref[...]
    cos = jnp.cos(ang)
    sin = jnp.sin(ang)
    s_up = sin * sg1_ref[...]
    s_dn = sin * sg2_ref[...]
    half = ROT_DIM // 2
    chunk = 4 * LANES
    for c0 in range(0, VALS_COLS, 2 * LANES):
        pv = _dot(h, wv_ref[:, c0:c0 + 2 * LANES])
        for r0 in (c0, c0 + LANES):
            piece = pv[:, r0 - c0:r0 - c0 + LANES]
            if r0 < VT_ROWS:
                vt_ref[0, r0:r0 + LANES, :] = piece.T.astype(vt_ref.dtype)
            else:
                misc_ref[...] = piece
    for c0 in list(range(ROPE_START, MAIN_COLS, chunk)) + list(range(0, ROPE_START, chunk)):
        c1 = min(c0 + chunk, MAIN_COLS)
        p = _dot(h, wm_ref[:, c0:c1])
        if c0 >= ROPE_START:
            for t in range((c1 - c0) // LANES):
                pt = p[:, t * LANES:(t + 1) * LANES]
                pt = (pt * cos + pltpu.roll(pt, LANES - half, 1) * s_up
                      + pltpu.roll(pt, half, 1) * s_dn)
                main_ref[:, c0 + t * LANES:c0 + (t + 1) * LANES] = pt.astype(main_ref.dtype)
        else:
            main_ref[:, c0:c1] = p.astype(main_ref.dtype)


def _proj(x2d, pos2d, g, freq, sg1, sg2, w_main, w_v, tm, batch, seq):
    n, d = x2d.shape
    nsb = seq // tm
    const = lambda i: (0, 0)
    return pl.pallas_call(
        _proj_kernel,
        grid=(n // tm,),
        in_specs=[
            pl.BlockSpec((tm, d), lambda i: (i, 0)),
            pl.BlockSpec((tm, 1), lambda i: (i, 0)),
            pl.BlockSpec((1, d), const),
            pl.BlockSpec((1, LANES), const),
            pl.BlockSpec((1, LANES), const),
            pl.BlockSpec((1, LANES), const),
            pl.BlockSpec((d, MAIN_COLS), const),
            pl.BlockSpec((d, VALS_COLS), const),
        ],
        out_specs=[
            pl.BlockSpec((tm, MAIN_COLS), lambda i: (i, 0)),
            pl.BlockSpec((1, VT_ROWS, tm), lambda i: (i // nsb, 0, i % nsb)),
            pl.BlockSpec((tm, LANES), lambda i: (i, 0)),
        ],
        out_shape=[
            jax.ShapeDtypeStruct((n, MAIN_COLS), MXU_DTYPE),
            jax.ShapeDtypeStruct((batch, VT_ROWS, seq), MXU_DTYPE),
            jax.ShapeDtypeStruct((n, LANES), F32),
        ],
        compiler_params=_cparams(("parallel",)),
        name="proj",
    )(x2d, pos2d, g, freq, sg1, sg2, w_main, w_v)


def _cumf_kernel(misc_ref, bias_ref, c_ref):
    s = misc_ref.shape[1]
    r = lax.broadcasted_iota(jnp.int32, (TK, TK), 0)
    c = lax.broadcasted_iota(jnp.int32, (TK, TK), 1)
    incl = (c <= r).astype(MXU_DTYPE)
    carry = jnp.zeros((1, LANES), F32)
    for b in range(s // TK):
        logit = misc_ref[0, b * TK:(b + 1) * TK, :] + bias_ref[...]
        logf, _ = _log_sigmoid_parts(logit)
        hi, mid, lo = _split3(logf)
        cs = (_dot(incl, lo) + _dot(incl, mid)) + _dot(incl, hi) + carry
        c_ref[0, b * TK:(b + 1) * TK, :] = cs
        carry = cs[TK - 1:TK, :]


def _cumf(misc3d, bias_row):
    b, s, _ = misc3d.shape
    return pl.pallas_call(
        _cumf_kernel,
        grid=(b,),
        in_specs=[
            pl.BlockSpec((1, s, LANES), lambda i: (i, 0, 0)),
            pl.BlockSpec((1, LANES), lambda i: (0, 0)),
        ],
        out_specs=pl.BlockSpec((1, s, LANES), lambda i: (i, 0, 0)),
        out_shape=jax.ShapeDtypeStruct((b, s, LANES), F32),
        compiler_params=_cparams(("parallel",)),
        name="cumf",
    )(misc3d, bias_row)


def _head_q(q_tile, half):
    lane = lax.broadcasted_iota(jnp.int32, q_tile.shape, 1)
    keep = (lane // HEAD_DIM) == half
    return jnp.where(keep, q_tile.astype(F32) * ATTN_SCALE, 0.0).astype(MXU_DTYPE)


def _causal_tile_t(strict):
    key = lax.broadcasted_iota(jnp.int32, (TK, TQ), 0)
    qry = lax.broadcasted_iota(jnp.int32, (TK, TQ), 1)
    return (key < qry) if strict else (key <= qry)


def _rows(j):
    return pl.ds(pl.multiple_of(j * TK, TK), TK)


def _lookahead_loop(n_last, bufs, issue, step, carry, finish, primed=False):
    buf_a, buf_b = bufs

    def pair(t, carry):
        n = 2 * t
        issue(n + 1, buf_b)
        carry = step(n, buf_a, carry, False)
        issue(n + 2, buf_a)
        return step(n + 1, buf_b, carry, False)

    def odd_tail(carry):
        issue(n_last, buf_b)
        carry = step(n_last - 1, buf_a, carry, False)
        step(n_last, buf_b, carry, True)
        finish()
        return 0

    def even_tail(carry):
        step(n_last, buf_a, carry, True)
        finish()
        return 0

    if not primed:
        issue(0, buf_a)
    carry = lax.fori_loop(0, n_last // 2, pair, carry)
    lax.cond(n_last % 2 == 1, odd_tail, even_tail, carry)


def _flash_heads(heads, n_full, diag_mask, z_bufs, acc_ref, o_ref):
    half_k, half_q = TK // 2, TQ // 2

    def add_bias(s, bias):
        if isinstance(bias, (list, tuple)):
            return jnp.concatenate([s[:, t * LANES:(t + 1) * LANES] + bias[t] for t in range(len(bias))],
                                   axis=1)
        return s + bias

    ones = jnp.ones((SUM_ROWS, TK), MXU_DTYPE)

    def issue(j, buf):
        rows = _rows(j)
        for h, (qh, k_at, _, bias_at) in enumerate(heads):
            s = add_bias(_dot_nt(k_at(rows), qh), bias_at(rows))
            buf[h, :TK, :] = s
            buf[h, TK:, :] = jnp.max(s.reshape(TK // SUBLANES, SUBLANES, TQ), axis=0)

    def update(tiles, rows, maxes, lanes, tile_maxes=None):
        stats = []
        for h, (s, m) in enumerate(zip(tiles, maxes)):
            tile_max = jnp.max(s if tile_maxes is None else tile_maxes[h], axis=0, keepdims=True)
            m_new = jnp.maximum(m, tile_max)
            stats.append((m_new, jnp.exp(m - m_new), jnp.exp(s - m_new).astype(MXU_DTYPE)))
        for h, ((_, alpha, p), (_, _, vt_at, _)) in enumerate(zip(stats, heads)):
            vt1 = jnp.concatenate([vt_at(rows), ones[:, :p.shape[0]]], axis=0)
            acc_ref[h, :, lanes] = alpha * acc_ref[h, :, lanes] + _dot(vt1, p)
        return tuple(m_new for m_new, _, _ in stats)

    def step(j, buf, maxes):
        n = len(heads)
        return update([buf[h, :TK, :] for h in range(n)], _rows(j), maxes, slice(None),
                      [buf[h, TK:, :] for h in range(n)])

    def diagonal_step(j, buf, maxes):
        start = pl.multiple_of(j * TK, TK)
        early, late = pl.ds(start, half_k), pl.ds(pl.multiple_of(start + half_k, half_k), half_k)

        def masked(s, mask):
            return s if diag_mask is None else jnp.where(mask, s, NEG_BIG)

        tiles = [masked(buf[h, :half_k, :], None if diag_mask is None else diag_mask[:half_k, :])
                 for h in range(len(heads))]
        maxes = update(tiles, early, maxes, slice(None))
        tiles = [masked(buf[h, half_k:TK, half_q:], None if diag_mask is None else diag_mask[half_k:, half_q:])
                 for h in range(len(heads))]
        update(tiles, late, tuple(m[:, half_q:] for m in maxes), slice(half_q, TQ))
        return maxes

    def finish():
        _store_heads_t(o_ref, [acc_ref[h, :HEAD_DIM, :] / acc_ref[h, HEAD_DIM:HEAD_DIM + 1, :]
                               for h in range(len(heads))])

    acc_ref[...] = jnp.zeros(acc_ref.shape, F32)
    init = tuple(jnp.full((1, TQ), NEG_BIG, F32) for _ in heads)
    _lookahead_loop(n_full, z_bufs, issue,
                    lambda j, buf, maxes, last: (diagonal_step if last else step)(j, buf, maxes), init, finish)


def _store_heads_t(o_ref, outs_t):
    for t in range(len(outs_t) // 2):
        pair_t = jnp.concatenate([outs_t[2 * t], outs_t[2 * t + 1]], axis=0)
        o_ref[0, :, t * LANES:(t + 1) * LANES] = pair_t.T.astype(o_ref.dtype)


def _sb_kernel(q_ref, k_ref, vt_ref, o_ref, za_ref, zb_ref, acc_ref):
    i = pl.program_id(2)
    r = lax.broadcasted_iota(jnp.int32, (TK, TK), 0)
    c = lax.broadcasted_iota(jnp.int32, (TK, TK), 1)
    after = (c > r).astype(MXU_DTYPE)
    strict = _causal_tile_t(True)
    n_heads = HEADS_PER_STEP
    qhs = [_head_q(q_ref[0][:, (h // 2) * LANES:(h // 2 + 1) * LANES], h % 2) for h in range(n_heads)]

    def issue(j, buf):
        for h in range(n_heads):
            buf[h] = _dot_nt(k_ref[0, _rows(j), (h // 2) * LANES:(h // 2 + 1) * LANES], qhs[h])

    def update(tiles, rows, tails, lanes, mask):
        n_rows = tiles[0].shape[0]
        parts = []
        for z in tiles:
            log_beta, neg_l1m = _log_sigmoid_parts(z)
            if mask is not None:
                neg_l1m = jnp.where(mask, neg_l1m, 0.0)
            hi, lo = _split2(neg_l1m)
            parts.append((log_beta, neg_l1m[0:1, :], jnp.concatenate([hi, lo], axis=0)))
        after_rows = jnp.concatenate([after[:n_rows, :n_rows]] * 2, axis=1)
        laters = [_dot(after_rows, hl) for (_, _, hl) in parts]
        ws = []
        for (log_beta, _, _), later in zip(parts, laters):
            w = jnp.exp(log_beta - later)
            if mask is not None:
                w = jnp.where(mask, w, 0.0)
            ws.append(w.astype(MXU_DTYPE))
        out = []
        for h, (w, (_, first, _), later, tail) in enumerate(zip(ws, parts, laters, tails)):
            vtj = vt_ref[0, h * HEAD_DIM:(h + 1) * HEAD_DIM, rows]
            acc_ref[h, :, lanes] = acc_ref[h, :, lanes] + jnp.exp(-tail) * _dot(vtj, w)
            out.append(tail + (first + later[0:1, :]))
        return tuple(out)

    def step(j, buf, tails):
        return update([buf[h] for h in range(n_heads)], _rows(j), tails, slice(None), None)

    def diagonal_step(j, buf, tails):
        half_k, half_q = TK // 2, TQ // 2
        start = pl.multiple_of(j * TK, TK)
        early, late = pl.ds(start, half_k), pl.ds(pl.multiple_of(start + half_k, half_k), half_k)
        late_tails = update([buf[h, half_k:, half_q:] for h in range(n_heads)], late,
                            tuple(t[:, half_q:] for t in tails), slice(half_q, TQ), strict[half_k:, half_q:])
        tails = tuple(jnp.concatenate([t[:, :half_q], lt], axis=1) for t, lt in zip(tails, late_tails))
        return update([buf[h, :half_k, :] for h in range(n_heads)], early, tails, slice(None), strict[:half_k, :])

    acc_ref[...] = jnp.zeros(acc_ref.shape, F32)
    carry = tuple(jnp.zeros((1, TQ), F32) for _ in range(n_heads))
    issue(i, za_ref)

    def finish():
        _store_heads_t(o_ref, [acc_ref[h] for h in range(n_heads)])

    def with_earlier_blocks(cr):
        issue(i - 1, zb_ref)
        cr = diagonal_step(i, za_ref, cr)
        _lookahead_loop(i - 1, (zb_ref, za_ref), lambda n, buf: issue(i - 1 - n, buf),
                        lambda n, buf, c, last: step(i - 1 - n, buf, c), cr, finish, primed=True)
        return 0

    def diagonal_only(cr):
        diagonal_step(i, za_ref, cr)
        finish()
        return 0

    lax.cond(i > 0, with_earlier_blocks, diagonal_only, carry)


def _sb(main3d, vt3d):
    b, s, _ = main3d.shape
    nq = s // TQ
    w = HEADS_PER_STEP * HEAD_DIM
    qb, kb, vb = COL_QA // w, COL_KA // w, ROW_VA // w
    return pl.pallas_call(
        _sb_kernel,
        grid=(b, SB_HEADS // HEADS_PER_STEP, nq),
        in_specs=[
            pl.BlockSpec((1, TQ, w), lambda bi, hg, i: (bi, i, qb + hg)),
            pl.BlockSpec((1, s, w), lambda bi, hg, i: (bi, 0, kb + hg)),
            pl.BlockSpec((1, w, s), lambda bi, hg, i: (bi, vb + hg, 0)),
        ],
        out_specs=pl.BlockSpec((1, TQ, w), lambda bi, hg, i: (bi, i, hg)),
        out_shape=jax.ShapeDtypeStruct((b, s, BRANCH_WIDTH), MXU_DTYPE),
        scratch_shapes=[pltpu.VMEM((HEADS_PER_STEP, TK, TQ), F32), pltpu.VMEM((HEADS_PER_STEP, TK, TQ), F32),
                        pltpu.VMEM((HEADS_PER_STEP, HEAD_DIM, TQ), F32)],
        compiler_params=_cparams(("parallel", "parallel", "arbitrary")),
        name="sb",
    )(main3d, main3d, vt3d)


def _fox_kernel(q_ref, k_ref, vt_ref, c_ref, o_ref, cb_ref, za_ref, zb_ref, acc_ref):
    hg = pl.program_id(1)
    i = pl.program_id(2)
    n_heads = HEADS_PER_STEP

    @pl.when(i == 0)
    def _():
        c = c_ref[0]
        lane = lax.broadcasted_iota(jnp.int32, c.shape, 1)
        for h in range(n_heads):
            if n_heads == FOX_HEADS:
                col = c[:, h:h + 1]
            else:
                col = jnp.sum(jnp.where(lane == hg * n_heads + h, c, 0.0), axis=1, keepdims=True)
            cb_ref[h] = jnp.broadcast_to(-col, c.shape)

    heads = []
    for h in range(n_heads):
        t = h // 2
        qh = _head_q(q_ref[0][:, t * LANES:(t + 1) * LANES], h % 2)
        k_at = lambda rows, t=t: k_ref[0, rows, t * LANES:(t + 1) * LANES]
        vt_at = lambda rows, h=h: vt_ref[0, h * HEAD_DIM:(h + 1) * HEAD_DIM, rows]
        bias_at = lambda rows, h=h: [cb_ref[h, rows, :]] * (TQ // LANES)
        heads.append((qh, k_at, vt_at, bias_at))
    _flash_heads(heads, i, _causal_tile_t(False), (za_ref, zb_ref), acc_ref, o_ref)


def _fox(main3d, vt3d, c3d):
    b, s, _ = main3d.shape
    nq = s // TQ
    w = HEADS_PER_STEP * HEAD_DIM
    qb, kb, vb = COL_QB // w, COL_KB // w, ROW_VB // w
    return pl.pallas_call(
        _fox_kernel,
        grid=(b, FOX_HEADS // HEADS_PER_STEP, nq),
        in_specs=[
            pl.BlockSpec((1, TQ, w), lambda bi, hg, i: (bi, i, qb + hg)),
            pl.BlockSpec((1, s, w), lambda bi, hg, i: (bi, 0, kb + hg)),
            pl.BlockSpec((1, w, s), lambda bi, hg, i: (bi, vb + hg, 0)),
            pl.BlockSpec((1, s, LANES), lambda bi, hg, i: (bi, 0, 0)),
        ],
        out_specs=pl.BlockSpec((1, TQ, w), lambda bi, hg, i: (bi, i, hg)),
        out_shape=jax.ShapeDtypeStruct((b, s, BRANCH_WIDTH), MXU_DTYPE),
        scratch_shapes=[pltpu.VMEM((HEADS_PER_STEP, s, LANES), F32),
                        pltpu.VMEM((HEADS_PER_STEP, TK + SUBLANES, TQ), F32),
                        pltpu.VMEM((HEADS_PER_STEP, TK + SUBLANES, TQ), F32),
                        pltpu.VMEM((HEADS_PER_STEP, HEAD_DIM + SUM_ROWS, TQ), F32)],
        compiler_params=_cparams(("parallel", "parallel", "arbitrary")),
        name="fox",
    )(main3d, main3d, vt3d, c3d)


def _dsa_select(i, qi_ref, ki_ref, misc_ref, key_ref, dig_ref, bias_ref, top_k):
    diag = _causal_tile_t(False)

    wi_t = misc_ref[0].T[MISC_WI:MISC_WI + IDX_HEADS, :] * IDX_W_SCALE
    qih = [_head_q(qi_ref[0][:, (h // 2) * LANES:(h // 2 + 1) * LANES], h % 2) for h in range(IDX_HEADS)]

    def score_block(j, mask):
        kij = ki_ref[0, _rows(j), :]
        sc = None
        for h in range(IDX_HEADS):
            term = wi_t[h:h + 1, :] * jnp.maximum(_dot_nt(kij, qih[h]), 0.0)
            sc = term if sc is None else sc + term
        if mask is not None:
            sc = jnp.where(mask, sc, -jnp.inf)
        bits = lax.bitcast_convert_type(jnp.where(sc == 0.0, 0.0, sc), jnp.int32)
        key_ref[_rows(j), :] = bits ^ ((bits >> 31) & jnp.int32(0x7FFFFFFF))

    def score_pair(t, _):
        score_block(2 * t, None)
        score_block(2 * t + 1, None)
        return 0

    def score_odd_tail():
        score_block(i - 1, None)
        score_block(i, diag)
        return 0

    def score_even_tail():
        score_block(i, diag)
        return 0

    lax.fori_loop(0, i // 2, score_pair, 0)
    lax.cond(i % 2 == 1, score_odd_tail, score_even_tail)

    small = MXU_DTYPE
    group = 2 * SUBLANES

    assert (key_ref.shape[0] // group) <= 256

    def count_digits_ge(cand):
        def hits(j):
            hit = jnp.where(dig_ref[_rows(j), :] >= cand, jnp.ones((), small), jnp.zeros((), small))
            part = hit[0:group]
            for g in range(1, TK // group):
                part = part + hit[g * group:(g + 1) * group]
            return part

        acc = lax.fori_loop(0, (i + 1) // 2, lambda t, a: a + (hits(2 * t) + hits(2 * t + 1)),
                            jnp.zeros((group, TQ), small))
        acc = lax.cond((i + 1) % 2 == 1, lambda a: a + hits(i), lambda a: a, acc)
        return jnp.sum(acc.astype(F32), axis=0, keepdims=True)

    above = jnp.zeros((1, TQ), F32)
    prefix = None
    for shift in (24, 16, 8, 0):
        def fill(j, _, shift=shift, prefix=prefix):
            key = key_ref[_rows(j), :]
            if prefix is None:
                digit = (key >> shift) + 128
            else:
                digit = (key ^ (prefix << (shift + 8))) >> shift
                shares = lax.bitcast_convert_type(digit, jnp.uint32) < jnp.uint32(256)
                digit = jnp.where(shares, digit, -1)
            dig_ref[_rows(j), :] = digit.astype(F32).astype(small)
            return 0

        lax.fori_loop(0, i + 1, fill, 0)

        def search(_, st, above=above):
            lo, hi, n_hi = st
            mid = (lo + hi) * 0.5
            n_mid = count_digits_ge(mid.astype(small))
            ok = above + n_mid >= top_k
            return jnp.where(ok, mid, lo), jnp.where(ok, hi, mid), jnp.where(ok, n_hi, n_mid)

        start = (jnp.zeros((1, TQ), F32), jnp.full((1, TQ), 256.0, F32), jnp.zeros((1, TQ), F32))
        digit, _, n_hi = lax.fori_loop(0, 8, search, start)
        above = above + n_hi
        digit = digit.astype(jnp.int32)
        prefix = digit - 128 if prefix is None else (prefix << 8) | digit
    thr_key = prefix

    need = top_k - above
    kr = lax.broadcasted_iota(jnp.int32, (TK, TK), 0)
    kc = lax.broadcasted_iota(jnp.int32, (TK, TK), 1)
    upto = (kc <= kr).astype(MXU_DTYPE)

    def bias_block(j, mask, seen):
        key = key_ref[_rows(j), :]
        tie = key == thr_key
        rank = _dot(upto, jnp.where(tie, 1.0, 0.0).astype(MXU_DTYPE)) + seen
        sel = (key > thr_key) | (tie & (rank <= need))
        if mask is not None:
            sel = sel & mask
        bias_ref[_rows(j), :] = jnp.where(sel, 0.0, NEG_BIG)
        return rank[TK - 1:TK, :]

    def bias_pair(t, seen):
        return bias_block(2 * t + 1, None, bias_block(2 * t, None, seen))

    def bias_odd_tail(seen):
        bias_block(i, diag, bias_block(i - 1, None, seen))
        return 0

    def bias_even_tail(seen):
        bias_block(i, diag, seen)
        return 0

    seen = lax.fori_loop(0, i // 2, bias_pair, jnp.zeros((1, TQ), F32))
    lax.cond(i % 2 == 1, bias_odd_tail, bias_even_tail, seen)


def _dsa_kernel(q_ref, k_ref, vt_ref, qi_ref, ki_ref, misc_ref, o_ref, key_ref, dig_ref, bias_ref,
                za_ref, zb_ref, acc_ref, *, top_k):
    i = pl.program_id(1)

    def every_causal_key():
        def earlier_block(j, _):
            bias_ref[_rows(j), :] = jnp.zeros((TK, TQ), F32)
            return 0

        lax.fori_loop(0, i, earlier_block, 0)
        bias_ref[_rows(i), :] = jnp.where(_causal_tile_t(False), 0.0, NEG_BIG)
        return 0

    def searched_keys():
        _dsa_select(i, qi_ref, ki_ref, misc_ref, key_ref, dig_ref, bias_ref, top_k)
        return 0

    lax.cond((i + 1) * TQ <= top_k, every_causal_key, searched_keys)

    bias_at = lambda rows: bias_ref[rows, :]
    heads = []
    for h in range(DSA_HEADS):
        g = h // DSA_GROUP
        k_at = lambda rows, g=g: k_ref[0, rows, g * LANES:(g + 1) * LANES]
        vt_at = lambda rows, g=g: vt_ref[0, g * HEAD_DIM:(g + 1) * HEAD_DIM, rows]
        qh = _head_q(q_ref[0][:, (h // 2) * LANES:(h // 2 + 1) * LANES], h % 2)
        heads.append((qh, k_at, vt_at, bias_at))
    _flash_heads(heads, i, None, (za_ref, zb_ref), acc_ref, o_ref)


def _dsa(main3d, vt3d, misc3d, top_k):
    b, s, _ = main3d.shape
    nq = s // TQ
    assert top_k <= TK, "the first key block must hold at least top_k keys"
    return pl.pallas_call(
        functools.partial(_dsa_kernel, top_k=top_k),
        grid=(b, nq),
        in_specs=[
            pl.BlockSpec((1, TQ, 4 * LANES), lambda bi, i: (bi, i, COL_QC // (4 * LANES))),
            pl.BlockSpec((1, s, 2 * LANES), lambda bi, i: (bi, 0, COL_KC // (2 * LANES))),
            pl.BlockSpec((1, LANES, s), lambda bi, i: (bi, ROW_VC // LANES, 0)),
            pl.BlockSpec((1, TQ, 2 * LANES), lambda bi, i: (bi, i, COL_QI // (2 * LANES))),
            pl.BlockSpec((1, s, LANES), lambda bi, i: (bi, 0, COL_KI // LANES)),
            pl.BlockSpec((1, TQ, LANES), lambda bi, i: (bi, i, 0)),
        ],
        out_specs=pl.BlockSpec((1, TQ, BRANCH_WIDTH), lambda bi, i: (bi, i, 0)),
        out_shape=jax.ShapeDtypeStruct((b, s, BRANCH_WIDTH), MXU_DTYPE),
        scratch_shapes=[pltpu.VMEM((s, TQ), jnp.int32), pltpu.VMEM((s, TQ), MXU_DTYPE), pltpu.VMEM((s, TQ), F32),
                        pltpu.VMEM((DSA_HEADS, TK + SUBLANES, TQ), F32),
                        pltpu.VMEM((DSA_HEADS, TK + SUBLANES, TQ), F32),
                        pltpu.VMEM((DSA_HEADS, HEAD_DIM + SUM_ROWS, TQ), F32)],
        compiler_params=_cparams(("parallel", "arbitrary")),
        name="dsa",
    )(main3d, main3d, vt3d, main3d, main3d, misc3d)


def _merge_kernel(x_ref, a_ref, b_ref, c_ref, g_ref, wg_ref, wb_ref, wo_ref, o_ref):
    x = x_ref[...]
    d = x.shape[1]
    h = _rmsnorm(x, g_ref[...]).astype(MXU_DTYPE)
    y = None
    for n, br_ref in enumerate((a_ref, b_ref, c_ref)):
        gate = jax.nn.sigmoid(_dot(h, wg_ref[:, n * d:(n + 1) * d]))
        term = gate * _dot(br_ref[...], wb_ref[n])
        y = term if y is None else y + term
    o_ref[...] = x + _dot(y.astype(MXU_DTYPE), wo_ref[...])


def _merge(x2d, br_a, br_b, br_c, g, w_gate, w_branch, w_out, tm):
    n, d = x2d.shape
    bw = br_a.shape[1]
    row = lambda i: (i, 0)
    const = lambda i: (0, 0)
    return pl.pallas_call(
        _merge_kernel,
        grid=(n // tm,),
        in_specs=[
            pl.BlockSpec((tm, d), row),
            pl.BlockSpec((tm, bw), row),
            pl.BlockSpec((tm, bw), row),
            pl.BlockSpec((tm, bw), row),
            pl.BlockSpec((1, d), const),
            pl.BlockSpec((d, N_BRANCH * d), const),
            pl.BlockSpec((N_BRANCH, bw, d), lambda i: (0, 0, 0)),
            pl.BlockSpec((d, d), const),
        ],
        out_specs=pl.BlockSpec((tm, d), row),
        out_shape=jax.ShapeDtypeStruct((n, d), F32),
        compiler_params=_cparams(("parallel",)),
        name="merge",
    )(x2d, br_a, br_b, br_c, g, w_gate, w_branch, w_out)


def _mlp_kernel(x_ref, g_ref, wu_ref, wd_ref, gf_ref, o_ref, *, ff_chunk, final_norm):
    x = x_ref[...]
    h = _rmsnorm(x, g_ref[...]).astype(MXU_DTYPE)
    acc = x
    for c in range(wu_ref.shape[1] // ff_chunk):
        u = jnp.maximum(_dot(h, wu_ref[:, c * ff_chunk:(c + 1) * ff_chunk]), 0.0)
        acc = acc + _dot((u * u).astype(MXU_DTYPE), wd_ref[c * ff_chunk:(c + 1) * ff_chunk, :])
    if final_norm:
        acc = _rmsnorm(acc, gf_ref[...])
    o_ref[...] = acc


def _mlp(x2d, g, w_up, w_down, g_final, tm, final_norm):
    n, d = x2d.shape
    ff = w_up.shape[1]
    row = lambda i: (i, 0)
    const = lambda i: (0, 0)
    return pl.pallas_call(
        functools.partial(_mlp_kernel, ff_chunk=min(ff, 1024), final_norm=final_norm),
        grid=(n // tm,),
        in_specs=[
            pl.BlockSpec((tm, d), row),
            pl.BlockSpec((1, d), const),
            pl.BlockSpec((d, ff), const),
            pl.BlockSpec((ff, d), const),
            pl.BlockSpec((1, d), const),
        ],
        out_specs=pl.BlockSpec((tm, d), row),
        out_shape=jax.ShapeDtypeStruct((n, d), F32),
        compiler_params=_cparams(("parallel",)),
        name="mlp",
    )(x2d, g, w_up, w_down, g_final)


def _w_in_columns(d_model):
    sizes = (("qa", BRANCH_WIDTH), ("ka", BRANCH_WIDTH), ("va", BRANCH_WIDTH),
             ("qb", BRANCH_WIDTH), ("kb", BRANCH_WIDTH), ("vb", BRANCH_WIDTH), ("fl", FOX_HEADS),
             ("qc", DSA_HEADS * HEAD_DIM), ("kc", DSA_KV_HEADS * HEAD_DIM), ("vc", DSA_KV_HEADS * HEAD_DIM),
             ("qi", IDX_HEADS * IDX_DIM), ("ki", IDX_DIM), ("wi", IDX_HEADS), ("gates", N_BRANCH * d_model))
    cols, o = {}, 0
    for name, width in sizes:
        cols[name] = (o, o + width)
        o += width
    return cols, o


def _pack_kernel(w_ref, main_ref, vals_ref, gates_ref):
    cols, _ = _w_in_columns(gates_ref.shape[2] // N_BRANCH)

    def src(name, lo=0, hi=None):
        a, b = cols[name]
        return w_ref[0, :, a + lo:(b if hi is None else a + hi)]

    def put(ref, start, value):
        ref[0, :, start:start + value.shape[1]] = value.astype(ref.dtype)

    put(main_ref, COL_QA, src("qa"))
    put(main_ref, COL_KA, src("ka"))
    put(main_ref, COL_QB, src("qb"))
    put(main_ref, COL_KB, src("kb"))
    put(main_ref, COL_QC, src("qc"))
    for g in range(DSA_KV_HEADS):
        kg = src("kc", g * HEAD_DIM, (g + 1) * HEAD_DIM)
        put(main_ref, COL_KC + g * LANES, jnp.concatenate([kg, kg], axis=1))
    put(main_ref, COL_QI, src("qi"))
    ki = src("ki")
    put(main_ref, COL_KI, jnp.concatenate([ki, ki], axis=1))
    put(vals_ref, ROW_VA, src("va"))
    put(vals_ref, ROW_VB, src("vb"))
    put(vals_ref, ROW_VC, src("vc"))
    pad = jnp.zeros((w_ref.shape[1], LANES - FOX_HEADS - IDX_HEADS), F32)
    put(vals_ref, VT_ROWS, jnp.concatenate([src("fl"), src("wi"), pad], axis=1))
    put(gates_ref, 0, src("gates"))


def _pack_w_in(w_in):
    depth, d, total = w_in.shape
    _, end = _w_in_columns(d)
    assert end == total
    rows = 256
    blk = lambda width: pl.BlockSpec((1, rows, width), lambda l, r: (l, r, 0))
    widths = (MAIN_COLS, VALS_COLS, N_BRANCH * d)
    return pl.pallas_call(
        _pack_kernel,
        grid=(depth, d // rows),
        in_specs=[blk(total)],
        out_specs=[blk(wd) for wd in widths],
        out_shape=[jax.ShapeDtypeStruct((depth, d, wd), MXU_DTYPE) for wd in widths],
        compiler_params=_cparams(("parallel", "parallel")),
        name="pack",
    )(w_in)


def _rope_rows():
    lane = jnp.arange(LANES) % HEAD_DIM
    half = ROT_DIM // 2
    inv_freq = ROPE_THETA ** (-jnp.arange(0, ROT_DIM, 2, dtype=F32) / ROT_DIM)
    freq = jnp.where(lane < ROT_DIM, inv_freq[lane % half], 0.0).astype(F32)
    sg1 = jnp.where(lane < half, -1.0, 0.0).astype(F32)
    sg2 = jnp.where((lane >= half) & (lane < ROT_DIM), 1.0, 0.0).astype(F32)
    return freq[None, :], sg1[None, :], sg2[None, :]


def kernel(x, positions, g_mix, w_in, b_forget, w_branch, w_out, g_mlp, w_up, w_down, g_final):
    b, s, d = x.shape
    n = b * s
    depth = w_in.shape[0]
    top_k = min(TOPK_MAX, s // 4)
    tm = min(512, s)
    freq, sg1, sg2 = _rope_rows()
    pos2d = positions.reshape(n, 1)
    x2d = x.reshape(n, d)
    w_main_all, w_v_all, w_gate_all = _pack_w_in(w_in)
    for layer in range(depth):
        w_main, w_v, w_gate = w_main_all[layer], w_v_all[layer], w_gate_all[layer]
        bias_row = jnp.concatenate([b_forget[layer], jnp.zeros((LANES - FOX_HEADS,), F32)])[None, :]
        main, vt, misc = _proj(x2d, pos2d, g_mix[layer][None, :], freq, sg1, sg2, w_main, w_v, tm, b, s)
        main3d = main.reshape(b, s, MAIN_COLS)
        misc3d = misc.reshape(b, s, LANES)
        cum = _cumf(misc3d, bias_row)
        br_a = _sb(main3d, vt).reshape(n, BRANCH_WIDTH)
        br_b = _fox(main3d, vt, cum).reshape(n, BRANCH_WIDTH)
        br_c = _dsa(main3d, vt, misc3d, top_k).reshape(n, BRANCH_WIDTH)
        x2d = _merge(x2d, br_a, br_b, br_c, g_mix[layer][None, :], w_gate,
                     w_branch[layer].astype(MXU_DTYPE), w_out[layer].astype(MXU_DTYPE), tm)
        x2d = _mlp(x2d, g_mlp[layer][None, :], w_up[layer].astype(MXU_DTYPE),
                   w_down[layer].astype(MXU_DTYPE), g_final[None, :], tm, layer == depth - 1)
    return x2d.reshape(b, s, d)
```

```python
import functools

import jax
import jax.numpy as jnp
from jax import lax
from jax.experimental import pallas as pl
from jax.experimental.pallas import tpu as pltpu

F32 = jnp.float32
MXU_DTYPE = jnp.bfloat16

HEAD_DIM = 64
SB_HEADS = 8
FOX_HEADS = 8
DSA_HEADS = 8
DSA_KV_HEADS = 2
DSA_GROUP = DSA_HEADS // DSA_KV_HEADS
IDX_HEADS = 4
IDX_DIM = 64
N_BRANCH = 3
ROPE_THETA = 500000.0
ROT_DIM = HEAD_DIM // 4
TOPK_MAX = 256
NORM_EPS = 1e-6

LANES = 128
SUBLANES = 8
SUM_ROWS = 2 * SUBLANES
BRANCH_WIDTH = SB_HEADS * HEAD_DIM
ATTN_SCALE = HEAD_DIM ** -0.5
assert IDX_DIM == HEAD_DIM
IDX_W_SCALE = IDX_HEADS ** -0.5

COL_QA, COL_KA = 0, 512
COL_QB, COL_KB = 1024, 1536
COL_QC = 2048
COL_KC = 2560
COL_QI = 2816
COL_KI = 3072
MAIN_COLS = 3200
ROPE_START = COL_QC
ROW_VA, ROW_VB, ROW_VC = 0, 512, 1024
VT_ROWS = 1152
VALS_COLS = VT_ROWS + LANES
MISC_FL, MISC_WI = 0, 8

TQ = 256
TK = 256
HEADS_PER_STEP = 8
NEG_BIG = -1e30

VMEM_LIMIT = 56 * 1024 * 1024


def _cparams(sem):
    return pltpu.CompilerParams(dimension_semantics=sem, vmem_limit_bytes=VMEM_LIMIT)


def _rmsnorm(x, g):
    y = x * lax.rsqrt(jnp.mean(x * x, axis=-1, keepdims=True) + NORM_EPS)
    return y * g


def _dot(a, b):
    return jnp.dot(a, b, preferred_element_type=F32)


def _dot_nt(a, b):
    return lax.dot_general(a, b, (((1,), (1,)), ((), ())), preferred_element_type=F32)


def _split2(x):
    hi = x.astype(MXU_DTYPE)
    lo = (x - hi.astype(F32)).astype(MXU_DTYPE)
    return hi, lo


def _split3(x):
    hi = x.astype(MXU_DTYPE)
    r = x - hi.astype(F32)
    mid = r.astype(MXU_DTYPE)
    lo = (r - mid.astype(F32)).astype(MXU_DTYPE)
    return hi, mid, lo


def _log_sigmoid_parts(z):
    neg_abs = lax.bitcast_convert_type(lax.bitcast_convert_type(z, jnp.int32) | jnp.int32(-2 ** 31), F32)
    sp = jnp.log(1.0 + jnp.exp(neg_abs))
    log_sig = jnp.minimum(z, 0.0) - sp
    return log_sig, z - log_sig


def _proj_kernel(x_ref, pos_ref, g_ref, freq_ref, sg1_ref, sg2_ref, wm_ref, wv_ref,
                 main_ref, vt_ref, misc_ref):
    h = _rmsnorm(x_ref[...], g_ref[...]).astype(MXU_DTYPE)
    ang = pos_ref[...].astype(F32) * freq_ref[...]
    cos = jnp.cos(ang)
    sin = jnp.sin(ang)
    s_up = sin * sg1_ref[...]
    s_dn = sin * sg2_ref[...]
    half = ROT_DIM // 2
    chunk = 4 * LANES
    for c0 in range(0, VALS_COLS, 2 * LANES):
        pv = _dot(h, wv_ref[:, c0:c0 + 2 * LANES])
        for r0 in (c0, c0 + LANES):
            piece = pv[:, r0 - c0:r0 - c0 + LANES]
            if r0 < VT_ROWS:
                vt_ref[0, r0:r0 + LANES, :] = piece.T.astype(vt_ref.dtype)
            else:
                misc_ref[...] = piece
    for c0 in list(range(ROPE_START, MAIN_COLS, chunk)) + list(range(0, ROPE_START, chunk)):
        c1 = min(c0 + chunk, MAIN_COLS)
        p = _dot(h, wm_ref[:, c0:c1])
        if c0 >= ROPE_START:
            for t in range((c1 - c0) // LANES):
                pt = p[:, t * LANES:(t + 1) * LANES]
                pt = (pt * cos + pltpu.roll(pt, LANES - half, 1) * s_up
                      + pltpu.roll(pt, half, 1) * s_dn)
                main_ref[:, c0 + t * LANES:c0 + (t + 1) * LANES] = pt.astype(main_ref.dtype)
        else:
            main_ref[:, c0:c1] = p.astype(main_ref.dtype)


def _proj(x2d, pos2d, g, freq, sg1, sg2, w_main, w_v, tm, batch, seq):
    n, d = x2d.shape
    nsb = seq // tm
    const = lambda i: (0, 0)
    return pl.pallas_call(
        _proj_kernel,
        grid=(n // tm,),
        in_specs=[
            pl.BlockSpec((tm, d), lambda i: (i, 0)),
            pl.BlockSpec((tm, 1), lambda i: (i, 0)),
            pl.BlockSpec((1, d), const),
            pl.BlockSpec((1, LANES), const),
            pl.BlockSpec((1, LANES), const),
            pl.BlockSpec((1, LANES), const),
            pl.BlockSpec((d, MAIN_COLS), const),
            pl.BlockSpec((d, VALS_COLS), const),
        ],
        out_specs=[
            pl.BlockSpec((tm, MAIN_COLS), lambda i: (i, 0)),
            pl.BlockSpec((1, VT_ROWS, tm), lambda i: (i // nsb, 0, i % nsb)),
            pl.BlockSpec((tm, LANES), lambda i: (i, 0)),
        ],
        out_shape=[
            jax.ShapeDtypeStruct((n, MAIN_COLS), MXU_DTYPE),
            jax.ShapeDtypeStruct((batch, VT_ROWS, seq), MXU_DTYPE),
            jax.ShapeDtypeStruct((n, LANES), F32),
        ],
        compiler_params=_cparams(("parallel",)),
        name="proj",
    )(x2d, pos2d, g, freq, sg1, sg2, w_main, w_v)


def _cumf_kernel(misc_ref, bias_ref, c_ref):
    s = misc_ref.shape[1]
    r = lax.broadcasted_iota(jnp.int32, (TK, TK), 0)
    c = lax.broadcasted_iota(jnp.int32, (TK, TK), 1)
    incl = (c <= r).astype(MXU_DTYPE)
    carry = jnp.zeros((1, LANES), F32)
    for b in range(s // TK):
        logit = misc_ref[0, b * TK:(b + 1) * TK, :] + bias_ref[...]
        logf, _ = _log_sigmoid_parts(logit)
        hi, mid, lo = _split3(logf)
        cs = (_dot(incl, lo) + _dot(incl, mid)) + _dot(incl, hi) + carry
        c_ref[0, b * TK:(b + 1) * TK, :] = cs
        carry = cs[TK - 1:TK, :]


def _cumf(misc3d, bias_row):
    b, s, _ = misc3d.shape
    return pl.pallas_call(
        _cumf_kernel,
        grid=(b,),
        in_specs=[
            pl.BlockSpec((1, s, LANES), lambda i: (i, 0, 0)),
            pl.BlockSpec((1, LANES), lambda i: (0, 0)),
        ],
        out_specs=pl.BlockSpec((1, s, LANES), lambda i: (i, 0, 0)),
        out_shape=jax.ShapeDtypeStruct((b, s, LANES), F32),
        compiler_params=_cparams(("parallel",)),
        name="cumf",
    )(misc3d, bias_row)


def _head_q(q_tile, half):
    lane = lax.broadcasted_iota(jnp.int32, q_tile.shape, 1)
    keep = (lane // HEAD_DIM) == half
    return jnp.where(keep, q_tile.astype(F32) * ATTN_SCALE, 0.0).astype(MXU_DTYPE)


def _causal_tile_t(strict):
    key = lax.broadcasted_iota(jnp.int32, (TK, TQ), 0)
    qry = lax.broadcasted_iota(jnp.int32, (TK, TQ), 1)
    return (key < qry) if strict else (key <= qry)


def _rows(j):
    return pl.ds(pl.multiple_of(j * TK, TK), TK)


def _lookahead_loop(n_last, bufs, issue, step, carry, finish, primed=False):
    buf_a, buf_b = bufs

    def pair(t, carry):
        n = 2 * t
        issue(n + 1, buf_b)
        carry = step(n, buf_a, carry, False)
        issue(n + 2, buf_a)
        return step(n + 1, buf_b, carry, False)

    def odd_tail(carry):
        issue(n_last, buf_b)
        carry = step(n_last - 1, buf_a, carry, False)
        step(n_last, buf_b, carry, True)
        finish()
        return 0

    def even_tail(carry):
        step(n_last, buf_a, carry, True)
        finish()
        return 0

    if not primed:
        issue(0, buf_a)
    carry = lax.fori_loop(0, n_last // 2, pair, carry)
    lax.cond(n_last % 2 == 1, odd_tail, even_tail, carry)


def _flash_heads(heads, n_full, diag_mask, z_bufs, acc_ref, o_ref):
    half_k, half_q = TK // 2, TQ // 2

    def add_bias(s, bias):
        if isinstance(bias, (list, tuple)):
            return jnp.concatenate([s[:, t * LANES:(t + 1) * LANES] + bias[t] for t in range(len(bias))],
                                   axis=1)
        return s + bias

    ones = jnp.ones((SUM_ROWS, TK), MXU_DTYPE)

    def issue(j, buf):
        rows = _rows(j)
        for h, (qh, k_at, _, bias_at) in enumerate(heads):
            s = add_bias(_dot_nt(k_at(rows), qh), bias_at(rows))
            buf[h, :TK, :] = s
            buf[h, TK:, :] = jnp.max(s.reshape(TK // SUBLANES, SUBLANES, TQ), axis=0)

    def update(tiles, rows, maxes, lanes, tile_maxes=None):
        stats = []
        for h, (s, m) in enumerate(zip(tiles, maxes)):
            tile_max = jnp.max(s if tile_maxes is None else tile_maxes[h], axis=0, keepdims=True)
            m_new = jnp.maximum(m, tile_max)
            stats.append((m_new, jnp.exp(m - m_new), jnp.exp(s - m_new).astype(MXU_DTYPE)))
        for h, ((_, alpha, p), (_, _, vt_at, _)) in enumerate(zip(stats, heads)):
            vt1 = jnp.concatenate([vt_at(rows), ones[:, :p.shape[0]]], axis=0)
            acc_ref[h, :, lanes] = alpha * acc_ref[h, :, lanes] + _dot(vt1, p)
        return tuple(m_new for m_new, _, _ in stats)

    def step(j, buf, maxes):
        n = len(heads)
        return update([buf[h, :TK, :] for h in range(n)], _rows(j), maxes, slice(None),
                      [buf[h, TK:, :] for h in range(n)])

    def diagonal_step(j, buf, maxes):
        start = pl.multiple_of(j * TK, TK)
        early, late = pl.ds(start, half_k), pl.ds(pl.multiple_of(start + half_k, half_k), half_k)

        def masked(s, mask):
            return s if diag_mask is None else jnp.where(mask, s, NEG_BIG)

        tiles = [masked(buf[h, :half_k, :], None if diag_mask is None else diag_mask[:half_k, :])
                 for h in range(len(heads))]
        maxes = update(tiles, early, maxes, slice(None))
        tiles = [masked(buf[h, half_k:TK, half_q:], None if diag_mask is None else diag_mask[half_k:, half_q:])
                 for h in range(len(heads))]
        update(tiles, late, tuple(m[:, half_q:] for m in maxes), slice(half_q, TQ))
        return maxes

    def finish():
        _store_heads_t(o_ref, [acc_ref[h, :HEAD_DIM, :] / acc_ref[h, HEAD_DIM:HEAD_DIM + 1, :]
                               for h in range(len(heads))])

    acc_ref[...] = jnp.zeros(acc_ref.shape, F32)
    init = tuple(jnp.full((1, TQ), NEG_BIG, F32) for _ in heads)
    _lookahead_loop(n_full, z_bufs, issue,
                    lambda j, buf, maxes, last: (diagonal_step if last else step)(j, buf, maxes), init, finish)


def _store_heads_t(o_ref, outs_t):
    for t in range(len(outs_t) // 2):
        pair_t = jnp.concatenate([outs_t[2 * t], outs_t[2 * t + 1]], axis=0)
        o_ref[0, :, t * LANES:(t + 1) * LANES] = pair_t.T.astype(o_ref.dtype)


def _sb_kernel(q_ref, k_ref, vt_ref, o_ref, za_ref, zb_ref, acc_ref):
    i = pl.program_id(2)
    r = lax.broadcasted_iota(jnp.int32, (TK, TK), 0)
    c = lax.broadcasted_iota(jnp.int32, (TK, TK), 1)
    after = (c > r).astype(MXU_DTYPE)
    strict = _causal_tile_t(True)
    n_heads = HEADS_PER_STEP
    qhs = [_head_q(q_ref[0][:, (h // 2) * LANES:(h // 2 + 1) * LANES], h % 2) for h in range(n_heads)]

    def issue(j, buf):
        for h in range(n_heads):
            buf[h] = _dot_nt(k_ref[0, _rows(j), (h // 2) * LANES:(h // 2 + 1) * LANES], qhs[h])

    def update(tiles, rows, tails, lanes, mask):
        n_rows = tiles[0].shape[0]
        parts = []
        for z in tiles:
            log_beta, neg_l1m = _log_sigmoid_parts(z)
            if mask is not None:
                neg_l1m = jnp.where(mask, neg_l1m, 0.0)
            parts.append((log_beta, neg_l1m[0:1, :], neg_l1m.astype(MXU_DTYPE)))
        after_rows = after[:n_rows, :n_rows]
        laters = [_dot(after_rows, hl) for (_, _, hl) in parts]
        ws = []
        for (log_beta, _, _), later in zip(parts, laters):
            w = jnp.exp(log_beta - later)
            if mask is not None:
                w = jnp.where(mask, w, 0.0)
            ws.append(w.astype(MXU_DTYPE))
        out = []
        for h, (w, (_, first, _), later, tail) in enumerate(zip(ws, parts, laters, tails)):
            vtj = vt_ref[0, h * HEAD_DIM:(h + 1) * HEAD_DIM, rows]
            acc_ref[h, :, lanes] = acc_ref[h, :, lanes] + jnp.exp(-tail) * _dot(vtj, w)
            out.append(tail + (first + later[0:1, :]))
        return tuple(out)

    def step(j, buf, tails):
        return update([buf[h] for h in range(n_heads)], _rows(j), tails, slice(None), None)

    def diagonal_step(j, buf, tails):
        half_k, half_q = TK // 2, TQ // 2
        start = pl.multiple_of(j * TK, TK)
        early, late = pl.ds(start, half_k), pl.ds(pl.multiple_of(start + half_k, half_k), half_k)
        late_tails = update([buf[h, half_k:, half_q:] for h in range(n_heads)], late,
                            tuple(t[:, half_q:] for t in tails), slice(half_q, TQ), strict[half_k:, half_q:])
        tails = tuple(jnp.concatenate([t[:, :half_q], lt], axis=1) for t, lt in zip(tails, late_tails))
        return update([buf[h, :half_k, :] for h in range(n_heads)], early, tails, slice(None), strict[:half_k, :])

    acc_ref[...] = jnp.zeros(acc_ref.shape, F32)
    carry = tuple(jnp.zeros((1, TQ), F32) for _ in range(n_heads))
    issue(i, za_ref)

    def finish():
        _store_heads_t(o_ref, [acc_ref[h] for h in range(n_heads)])

    def with_earlier_blocks(cr):
        issue(i - 1, zb_ref)
        cr = diagonal_step(i, za_ref, cr)
        _lookahead_loop(i - 1, (zb_ref, za_ref), lambda n, buf: issue(i - 1 - n, buf),
                        lambda n, buf, c, last: step(i - 1 - n, buf, c), cr, finish, primed=True)
        return 0

    def diagonal_only(cr):
        diagonal_step(i, za_ref, cr)
        finish()
        return 0

    lax.cond(i > 0, with_earlier_blocks, diagonal_only, carry)


def _sb(main3d, vt3d):
    b, s, _ = main3d.shape
    nq = s // TQ
    w = HEADS_PER_STEP * HEAD_DIM
    qb, kb, vb = COL_QA // w, COL_KA // w, ROW_VA // w
    return pl.pallas_call(
        _sb_kernel,
        grid=(b, SB_HEADS // HEADS_PER_STEP, nq),
        in_specs=[
            pl.BlockSpec((1, TQ, w), lambda bi, hg, i: (bi, i, qb + hg)),
            pl.BlockSpec((1, s, w), lambda bi, hg, i: (bi, 0, kb + hg)),
            pl.BlockSpec((1, w, s), lambda bi, hg, i: (bi, vb + hg, 0)),
        ],
        out_specs=pl.BlockSpec((1, TQ, w), lambda bi, hg, i: (bi, i, hg)),
        out_shape=jax.ShapeDtypeStruct((b, s, BRANCH_WIDTH), MXU_DTYPE),
        scratch_shapes=[pltpu.VMEM((HEADS_PER_STEP, TK, TQ), F32), pltpu.VMEM((HEADS_PER_STEP, TK, TQ), F32),
                        pltpu.VMEM((HEADS_PER_STEP, HEAD_DIM, TQ), F32)],
        compiler_params=_cparams(("parallel", "parallel", "arbitrary")),
        name="sb",
    )(main3d, main3d, vt3d)


def _fox_kernel(q_ref, k_ref, vt_ref, c_ref, o_ref, cb_ref, za_ref, zb_ref, acc_ref):
    hg = pl.program_id(1)
    i = pl.program_id(2)
    n_heads = HEADS_PER_STEP

    @pl.when(i == 0)
    def _():
        c = c_ref[0]
        lane = lax.broadcasted_iota(jnp.int32, c.shape, 1)
        for h in range(n_heads):
            if n_heads == FOX_HEADS:
                col = c[:, h:h + 1]
            else:
                col = jnp.sum(jnp.where(lane == hg * n_heads + h, c, 0.0), axis=1, keepdims=True)
            cb_ref[h] = jnp.broadcast_to(-col, c.shape)

    heads = []
    for h in range(n_heads):
        t = h // 2
        qh = _head_q(q_ref[0][:, t * LANES:(t + 1) * LANES], h % 2)
        k_at = lambda rows, t=t: k_ref[0, rows, t * LANES:(t + 1) * LANES]
        vt_at = lambda rows, h=h: vt_ref[0, h * HEAD_DIM:(h + 1) * HEAD_DIM, rows]
        bias_at = lambda rows, h=h: [cb_ref[h, rows, :]] * (TQ // LANES)
        heads.append((qh, k_at, vt_at, bias_at))
    _flash_heads(heads, i, _causal_tile_t(False), (za_ref, zb_ref), acc_ref, o_ref)


def _fox(main3d, vt3d, c3d):
    b, s, _ = main3d.shape
    nq = s // TQ
    w = HEADS_PER_STEP * HEAD_DIM
    qb, kb, vb = COL_QB // w, COL_KB // w, ROW_VB // w
    return pl.pallas_call(
        _fox_kernel,
        grid=(b, FOX_HEADS // HEADS_PER_STEP, nq),
        in_specs=[
            pl.BlockSpec((1, TQ, w), lambda bi, hg, i: (bi, i, qb + hg)),
            pl.BlockSpec((1, s, w), lambda bi, hg, i: (bi, 0, kb + hg)),
            pl.BlockSpec((1, w, s), lambda bi, hg, i: (bi, vb + hg, 0)),
            pl.BlockSpec((1, s, LANES), lambda bi, hg, i: (bi, 0, 0)),
        ],
        out_specs=pl.BlockSpec((1, TQ, w), lambda bi, hg, i: (bi, i, hg)),
        out_shape=jax.ShapeDtypeStruct((b, s, BRANCH_WIDTH), MXU_DTYPE),
        scratch_shapes=[pltpu.VMEM((HEADS_PER_STEP, s, LANES), F32),
                        pltpu.VMEM((HEADS_PER_STEP, TK + SUBLANES, TQ), F32),
                        pltpu.VMEM((HEADS_PER_STEP, TK + SUBLANES, TQ), F32),
                        pltpu.VMEM((HEADS_PER_STEP, HEAD_DIM + SUM_ROWS, TQ), F32)],
        compiler_params=_cparams(("parallel", "parallel", "arbitrary")),
        name="fox",
    )(main3d, main3d, vt3d, c3d)


def _dsa_select(i, qi_ref, ki_ref, misc_ref, key_ref, dig_ref, bias_ref, top_k):
    diag = _causal_tile_t(False)

    wi_t = misc_ref[0].T[MISC_WI:MISC_WI + IDX_HEADS, :] * IDX_W_SCALE
    qih = [_head_q(qi_ref[0][:, (h // 2) * LANES:(h // 2 + 1) * LANES], h % 2) for h in range(IDX_HEADS)]

    def score_block(j, mask):
        kij = ki_ref[0, _rows(j), :]
        sc = None
        for h in range(IDX_HEADS):
            term = wi_t[h:h + 1, :] * jnp.maximum(_dot_nt(kij, qih[h]), 0.0)
            sc = term if sc is None else sc + term
        if mask is not None:
            sc = jnp.where(mask, sc, -jnp.inf)
        bits = lax.bitcast_convert_type(jnp.where(sc == 0.0, 0.0, sc), jnp.int32)
        key_ref[_rows(j), :] = bits ^ ((bits >> 31) & jnp.int32(0x7FFFFFFF))

    def score_pair(t, _):
        score_block(2 * t, None)
        score_block(2 * t + 1, None)
        return 0

    def score_odd_tail():
        score_block(i - 1, None)
        score_block(i, diag)
        return 0

    def score_even_tail():
        score_block(i, diag)
        return 0

    lax.fori_loop(0, i // 2, score_pair, 0)
    lax.cond(i % 2 == 1, score_odd_tail, score_even_tail)

    small = MXU_DTYPE
    group = 2 * SUBLANES

    assert (key_ref.shape[0] // group) <= 256

    def count_digits_ge(cand):
        def hits(j):
            hit = jnp.where(dig_ref[_rows(j), :] >= cand, jnp.ones((), small), jnp.zeros((), small))
            part = hit[0:group]
            for g in range(1, TK // group):
                part = part + hit[g * group:(g + 1) * group]
            return part

        acc = lax.fori_loop(0, (i + 1) // 2, lambda t, a: a + (hits(2 * t) + hits(2 * t + 1)),
                            jnp.zeros((group, TQ), small))
        acc = lax.cond((i + 1) % 2 == 1, lambda a: a + hits(i), lambda a: a, acc)
        return jnp.sum(acc.astype(F32), axis=0, keepdims=True)

    above = jnp.zeros((1, TQ), F32)
    prefix = None
    for shift in (24, 16, 8, 0):
        def fill(j, _, shift=shift, prefix=prefix):
            key = key_ref[_rows(j), :]
            if prefix is None:
                digit = (key >> shift) + 128
            else:
                digit = (key ^ (prefix << (shift + 8))) >> shift
                shares = lax.bitcast_convert_type(digit, jnp.uint32) < jnp.uint32(256)
                digit = jnp.where(shares, digit, -1)
            dig_ref[_rows(j), :] = digit.astype(F32).astype(small)
            return 0

        lax.fori_loop(0, i + 1, fill, 0)

        def search(_, st, above=above):
            lo, hi, n_hi = st
            mid = (lo + hi) * 0.5
            n_mid = count_digits_ge(mid.astype(small))
            ok = above + n_mid >= top_k
            return jnp.where(ok, mid, lo), jnp.where(ok, hi, mid), jnp.where(ok, n_hi, n_mid)

        start = (jnp.zeros((1, TQ), F32), jnp.full((1, TQ), 256.0, F32), jnp.zeros((1, TQ), F32))
        digit, _, n_hi = lax.fori_loop(0, 8, search, start)
        above = above + n_hi
        digit = digit.astype(jnp.int32)
        prefix = digit - 128 if prefix is None else (prefix << 8) | digit
    thr_key = prefix

    need = top_k - above
    kr = lax.broadcasted_iota(jnp.int32, (TK, TK), 0)
    kc = lax.broadcasted_iota(jnp.int32, (TK, TK), 1)
    upto = (kc <= kr).astype(MXU_DTYPE)

    def bias_block(j, mask, seen):
        key = key_ref[_rows(j), :]
        tie = key == thr_key
        rank = _dot(upto, jnp.where(tie, 1.0, 0.0).astype(MXU_DTYPE)) + seen
        sel = (key > thr_key) | (tie & (rank <= need))
        if mask is not None:
            sel = sel & mask
        bias_ref[_rows(j), :] = jnp.where(sel, 0.0, NEG_BIG)
        return rank[TK - 1:TK, :]

    def bias_pair(t, seen):
        return bias_block(2 * t + 1, None, bias_block(2 * t, None, seen))

    def bias_odd_tail(seen):
        bias_block(i, diag, bias_block(i - 1, None, seen))
        return 0

    def bias_even_tail(seen):
        bias_block(i, diag, seen)
        return 0

    seen = lax.fori_loop(0, i // 2, bias_pair, jnp.zeros((1, TQ), F32))
    lax.cond(i % 2 == 1, bias_odd_tail, bias_even_tail, seen)


def _dsa_kernel(q_ref, k_ref, vt_ref, qi_ref, ki_ref, misc_ref, o_ref, key_ref, dig_ref, bias_ref,
                za_ref, zb_ref, acc_ref, *, top_k):
    i = pl.program_id(1)

    def every_causal_key():
        def earlier_block(j, _):
            bias_ref[_rows(j), :] = jnp.zeros((TK, TQ), F32)
            return 0

        lax.fori_loop(0, i, earlier_block, 0)
        bias_ref[_rows(i), :] = jnp.where(_causal_tile_t(False), 0.0, NEG_BIG)
        return 0

    def searched_keys():
        _dsa_select(i, qi_ref, ki_ref, misc_ref, key_ref, dig_ref, bias_ref, top_k)
        return 0

    lax.cond((i + 1) * TQ <= top_k, every_causal_key, searched_keys)

    bias_at = lambda rows: bias_ref[rows, :]
    heads = []
    for h in range(DSA_HEADS):
        g = h // DSA_GROUP
        k_at = lambda rows, g=g: k_ref[0, rows, g * LANES:(g + 1) * LANES]
        vt_at = lambda rows, g=g: vt_ref[0, g * HEAD_DIM:(g + 1) * HEAD_DIM, rows]
        qh = _head_q(q_ref[0][:, (h // 2) * LANES:(h // 2 + 1) * LANES], h % 2)
        heads.append((qh, k_at, vt_at, bias_at))
    _flash_heads(heads, i, None, (za_ref, zb_ref), acc_ref, o_ref)


def _dsa(main3d, vt3d, misc3d, top_k):
    b, s, _ = main3d.shape
    nq = s // TQ
    assert top_k <= TK, "the first key block must hold at least top_k keys"
    return pl.pallas_call(
        functools.partial(_dsa_kernel, top_k=top_k),
        grid=(b, nq),
        in_specs=[
            pl.BlockSpec((1, TQ, 4 * LANES), lambda bi, i: (bi, i, COL_QC // (4 * LANES))),
            pl.BlockSpec((1, s, 2 * LANES), lambda bi, i: (bi, 0, COL_KC // (2 * LANES))),
            pl.BlockSpec((1, LANES, s), lambda bi, i: (bi, ROW_VC // LANES, 0)),
            pl.BlockSpec((1, TQ, 2 * LANES), lambda bi, i: (bi, i, COL_QI // (2 * LANES))),
            pl.BlockSpec((1, s, LANES), lambda bi, i: (bi, 0, COL_KI // LANES)),
            pl.BlockSpec((1, TQ, LANES), lambda bi, i: (bi, i, 0)),
        ],
        out_specs=pl.BlockSpec((1, TQ, BRANCH_WIDTH), lambda bi, i: (bi, i, 0)),
        out_shape=jax.ShapeDtypeStruct((b, s, BRANCH_WIDTH), MXU_DTYPE),
        scratch_shapes=[pltpu.VMEM((s, TQ), jnp.int32), pltpu.VMEM((s, TQ), MXU_DTYPE), pltpu.VMEM((s, TQ), F32),
                        pltpu.VMEM((DSA_HEADS, TK + SUBLANES, TQ), F32),
                        pltpu.VMEM((DSA_HEADS, TK + SUBLANES, TQ), F32),
                        pltpu.VMEM((DSA_HEADS, HEAD_DIM + SUM_ROWS, TQ), F32)],
        compiler_params=_cparams(("parallel", "arbitrary")),
        name="dsa",
    )(main3d, main3d, vt3d, main3d, main3d, misc3d)


def _merge_kernel(x_ref, a_ref, b_ref, c_ref, g_ref, wg_ref, wb_ref, wo_ref, o_ref):
    x = x_ref[...]
    d = x.shape[1]
    h = _rmsnorm(x, g_ref[...]).astype(MXU_DTYPE)
    y = None
    for n, br_ref in enumerate((a_ref, b_ref, c_ref)):
        gate = jax.nn.sigmoid(_dot(h, wg_ref[:, n * d:(n + 1) * d]))
        term = gate * _dot(br_ref[...], wb_ref[n])
        y = term if y is None else y + term
    o_ref[...] = x + _dot(y.astype(MXU_DTYPE), wo_ref[...])


def _merge(x2d, br_a, br_b, br_c, g, w_gate, w_branch, w_out, tm):
    n, d = x2d.shape
    bw = br_a.shape[1]
    row = lambda i: (i, 0)
    const = lambda i: (0, 0)
    return pl.pallas_call(
        _merge_kernel,
        grid=(n // tm,),
        in_specs=[
            pl.BlockSpec((tm, d), row),
            pl.BlockSpec((tm, bw), row),
            pl.BlockSpec((tm, bw), row),
            pl.BlockSpec((tm, bw), row),
            pl.BlockSpec((1, d), const),
            pl.BlockSpec((d, N_BRANCH * d), const),
            pl.BlockSpec((N_BRANCH, bw, d), lambda i: (0, 0, 0)),
            pl.BlockSpec((d, d), const),
        ],
        out_specs=pl.BlockSpec((tm, d), row),
        out_shape=jax.ShapeDtypeStruct((n, d), F32),
        compiler_params=_cparams(("parallel",)),
        name="merge",
    )(x2d, br_a, br_b, br_c, g, w_gate, w_branch, w_out)


def _mlp_kernel(x_ref, g_ref, wu_ref, wd_ref, gf_ref, o_ref, *, ff_chunk, final_norm):
    x = x_ref[...]
    h = _rmsnorm(x, g_ref[...]).astype(MXU_DTYPE)
    acc = x
    for c in range(wu_ref.shape[1] // ff_chunk):
        u = jnp.maximum(_dot(h, wu_ref[:, c * ff_chunk:(c + 1) * ff_chunk]), 0.0)
        acc = acc + _dot((u * u).astype(MXU_DTYPE), wd_ref[c * ff_chunk:(c + 1) * ff_chunk, :])
    if final_norm:
        acc = _rmsnorm(acc, gf_ref[...])
    o_ref[...] = acc


def _mlp(x2d, g, w_up, w_down, g_final, tm, final_norm):
    n, d = x2d.shape
    ff = w_up.shape[1]
    row = lambda i: (i, 0)
    const = lambda i: (0, 0)
    return pl.pallas_call(
        functools.partial(_mlp_kernel, ff_chunk=min(ff, 1024), final_norm=final_norm),
        grid=(n // tm,),
        in_specs=[
            pl.BlockSpec((tm, d), row),
            pl.BlockSpec((1, d), const),
            pl.BlockSpec((d, ff), const),
            pl.BlockSpec((ff, d), const),
            pl.BlockSpec((1, d), const),
        ],
        out_specs=pl.BlockSpec((tm, d), row),
        out_shape=jax.ShapeDtypeStruct((n, d), F32),
        compiler_params=_cparams(("parallel",)),
        name="mlp",
    )(x2d, g, w_up, w_down, g_final)


def _w_in_columns(d_model):
    sizes = (("qa", BRANCH_WIDTH), ("ka", BRANCH_WIDTH), ("va", BRANCH_WIDTH),
             ("qb", BRANCH_WIDTH), ("kb", BRANCH_WIDTH), ("vb", BRANCH_WIDTH), ("fl", FOX_HEADS),
             ("qc", DSA_HEADS * HEAD_DIM), ("kc", DSA_KV_HEADS * HEAD_DIM), ("vc", DSA_KV_HEADS * HEAD_DIM),
             ("qi", IDX_HEADS * IDX_DIM), ("ki", IDX_DIM), ("wi", IDX_HEADS), ("gates", N_BRANCH * d_model))
    cols, o = {}, 0
    for name, width in sizes:
        cols[name] = (o, o + width)
        o += width
    return cols, o


def _pack_kernel(w_ref, main_ref, vals_ref, gates_ref):
    cols, _ = _w_in_columns(gates_ref.shape[2] // N_BRANCH)

    def src(name, lo=0, hi=None):
        a, b = cols[name]
        return w_ref[0, :, a + lo:(b if hi is None else a + hi)]

    def put(ref, start, value):
        ref[0, :, start:start + value.shape[1]] = value.astype(ref.dtype)

    put(main_ref, COL_QA, src("qa"))
    put(main_ref, COL_KA, src("ka"))
    put(main_ref, COL_QB, src("qb"))
    put(main_ref, COL_KB, src("kb"))
    put(main_ref, COL_QC, src("qc"))
    for g in range(DSA_KV_HEADS):
        kg = src("kc", g * HEAD_DIM, (g + 1) * HEAD_DIM)
        put(main_ref, COL_KC + g * LANES, jnp.concatenate([kg, kg], axis=1))
    put(main_ref, COL_QI, src("qi"))
    ki = src("ki")
    put(main_ref, COL_KI, jnp.concatenate([ki, ki], axis=1))
    put(vals_ref, ROW_VA, src("va"))
    put(vals_ref, ROW_VB, src("vb"))
    put(vals_ref, ROW_VC, src("vc"))
    pad = jnp.zeros((w_ref.shape[1], LANES - FOX_HEADS - IDX_HEADS), F32)
    put(vals_ref, VT_ROWS, jnp.concatenate([src("fl"), src("wi"), pad], axis=1))
    put(gates_ref, 0, src("gates"))


def _pack_w_in(w_in):
    depth, d, total = w_in.shape
    _, end = _w_in_columns(d)
    assert end == total
    rows = 256
    blk = lambda width: pl.BlockSpec((1, rows, width), lambda l, r: (l, r, 0))
    widths = (MAIN_COLS, VALS_COLS, N_BRANCH * d)
    return pl.pallas_call(
        _pack_kernel,
        grid=(depth, d // rows),
        in_specs=[blk(total)],
        out_specs=[blk(wd) for wd in widths],
        out_shape=[jax.ShapeDtypeStruct((depth, d, wd), MXU_DTYPE) for wd in widths],
        compiler_params=_cparams(("parallel", "parallel")),
        name="pack",
    )(w_in)


def _rope_rows():
    lane = jnp.arange(LANES) % HEAD_DIM
    half = ROT_DIM // 2
    inv_freq = ROPE_THETA ** (-jnp.arange(0, ROT_DIM, 2, dtype=F32) / ROT_DIM)
    freq = jnp.where(lane < ROT_DIM, inv_freq[lane % half], 0.0).astype(F32)
    sg1 = jnp.where(lane < half, -1.0, 0.0).astype(F32)
    sg2 = jnp.where((lane >= half) & (lane < ROT_DIM), 1.0, 0.0).astype(F32)
    return freq[None, :], sg1[None, :], sg2[None, :]


def kernel(x, positions, g_mix, w_in, b_forget, w_branch, w_out, g_mlp, w_up, w_down, g_final):
    b, s, d = x.shape
    n = b * s
    depth = w_in.shape[0]
    top_k = min(TOPK_MAX, s // 4)
    tm = min(512, s)
    freq, sg1, sg2 = _rope_rows()
    pos2d = positions.reshape(n, 1)
    x2d = x.reshape(n, d)
    w_main_all, w_v_all, w_gate_all = _pack_w_in(w_in)
    for layer in range(depth):
        w_main, w_v, w_gate = w_main_all[layer], w_v_all[layer], w_gate_all[layer]
        bias_row = jnp.concatenate([b_forget[layer], jnp.zeros((LANES - FOX_HEADS,), F32)])[None, :]
        main, vt, misc = _proj(x2d, pos2d, g_mix[layer][None, :], freq, sg1, sg2, w_main, w_v, tm, b, s)
        main3d = main.reshape(b, s, MAIN_COLS)
        misc3d = misc.reshape(b, s, LANES)
        cum = _cumf(misc3d, bias_row)
        br_a = _sb(main3d, vt).reshape(n, BRANCH_WIDTH)
        br_b = _fox(main3d, vt, cum).reshape(n, BRANCH_WIDTH)
        br_c = _dsa(main3d, vt, misc3d, top_k).reshape(n, BRANCH_WIDTH)
        x2d = _merge(x2d, br_a, br_b, br_c, g_mix[layer][None, :], w_gate,
                     w_branch[layer].astype(MXU_DTYPE), w_out[layer].astype(MXU_DTYPE), tm)
        x2d = _mlp(x2d, g_mlp[layer][None, :], w_up[layer].astype(MXU_DTYPE),
                   w_down[layer].astype(MXU_DTYPE), g_final[None, :], tm, layer == depth - 1)
    return x2d.reshape(b, s, d)
```

```python
import functools

import jax
import jax.numpy as jnp
from jax import lax
from jax.experimental import pallas as pl
from jax.experimental.pallas import tpu as pltpu

F32 = jnp.float32
MXU_DTYPE = jnp.bfloat16

HEAD_DIM = 64
SB_HEADS = 8
FOX_HEADS = 8
DSA_HEADS = 8
DSA_KV_HEADS = 2
DSA_GROUP = DSA_HEADS // DSA_KV_HEADS
IDX_HEADS = 4
IDX_DIM = 64
N_BRANCH = 3
ROPE_THETA = 500000.0
ROT_DIM = HEAD_DIM // 4
TOPK_MAX = 256
NORM_EPS = 1e-6

LANES = 128
SUBLANES = 8
SUM_ROWS = 2 * SUBLANES
BRANCH_WIDTH = SB_HEADS * HEAD_DIM
ATTN_SCALE = HEAD_DIM ** -0.5
assert IDX_DIM == HEAD_DIM
IDX_W_SCALE = IDX_HEADS ** -0.5

COL_QA, COL_KA = 0, 512
COL_QB, COL_KB = 1024, 1536
COL_QC = 2048
COL_KC = 2560
COL_QI = 2816
COL_KI = 3072
MAIN_COLS = 3200
ROPE_START = COL_QC
ROW_VA, ROW_VB, ROW_VC = 0, 512, 1024
VT_ROWS = 1152
VALS_COLS = VT_ROWS + LANES
MISC_FL, MISC_WI = 0, 8

TQ = 256
TK = 256
HEADS_PER_STEP = 8
NEG_BIG = -1e30

VMEM_LIMIT = 56 * 1024 * 1024


def _cparams(sem):
    return pltpu.CompilerParams(dimension_semantics=sem, vmem_limit_bytes=VMEM_LIMIT)


def _rmsnorm(x, g):
    y = x * lax.rsqrt(jnp.mean(x * x, axis=-1, keepdims=True) + NORM_EPS)
    return y * g


def _dot(a, b):
    return jnp.dot(a, b, preferred_element_type=F32)


def _dot_nt(a, b):
    return lax.dot_general(a, b, (((1,), (1,)), ((), ())), preferred_element_type=F32)


def _split2(x):
    hi = x.astype(MXU_DTYPE)
    lo = (x - hi.astype(F32)).astype(MXU_DTYPE)
    return hi, lo


def _split3(x):
    hi = x.astype(MXU_DTYPE)
    r = x - hi.astype(F32)
    mid = r.astype(MXU_DTYPE)
    lo = (r - mid.astype(F32)).astype(MXU_DTYPE)
    return hi, mid, lo


def _log_sigmoid_parts(z):
    neg_abs = lax.bitcast_convert_type(lax.bitcast_convert_type(z, jnp.int32) | jnp.int32(-2 ** 31), F32)
    sp = jnp.log(1.0 + jnp.exp(neg_abs))
    log_sig = jnp.minimum(z, 0.0) - sp
    return log_sig, z - log_sig


def _proj_kernel(x_ref, pos_ref, g_ref, freq_ref, sg1_ref, sg2_ref, wm_ref, wv_ref,
                 main_ref, vt_ref, misc_ref):
    h = _rmsnorm(x_ref[...], g_ref[...]).astype(MXU_DTYPE)
    ang = pos_ref[...].astype(F32) * freq_ref[...]
    cos = jnp.cos(ang)
    sin = jnp.sin(ang)
    s_up = sin * sg1_ref[...]
    s_dn = sin * sg2_ref[...]
    half = ROT_DIM // 2
    chunk = 4 * LANES
    for c0 in range(0, VALS_COLS, 2 * LANES):
        pv = _dot(h, wv_ref[:, c0:c0 + 2 * LANES])
        for r0 in (c0, c0 + LANES):
            piece = pv[:, r0 - c0:r0 - c0 + LANES]
            if r0 < VT_ROWS:
                vt_ref[0, r0:r0 + LANES, :] = piece.T.astype(vt_ref.dtype)
            else:
                misc_ref[...] = piece
    for c0 in list(range(ROPE_START, MAIN_COLS, chunk)) + list(range(0, ROPE_START, chunk)):
        c1 = min(c0 + chunk, MAIN_COLS)
        p = _dot(h, wm_ref[:, c0:c1])
        if c0 >= ROPE_START:
            for t in range((c1 - c0) // LANES):
                pt = p[:, t * LANES:(t + 1) * LANES]
                pt = (pt * cos + pltpu.roll(pt, LANES - half, 1) * s_up
                      + pltpu.roll(pt, half, 1) * s_dn)
                main_ref[:, c0 + t * LANES:c0 + (t + 1) * LANES] = pt.astype(main_ref.dtype)
        else:
            main_ref[:, c0:c1] = p.astype(main_ref.dtype)


def _proj(x2d, pos2d, g, freq, sg1, sg2, w_main, w_v, tm, batch, seq):
    n, d = x2d.shape
    nsb = seq // tm
    const = lambda i: (0, 0)
    return pl.pallas_call(
        _proj_kernel,
        grid=(n // tm,),
        in_specs=[
            pl.BlockSpec((tm, d), lambda i: (i, 0)),
            pl.BlockSpec((tm, 1), lambda i: (i, 0)),
            pl.BlockSpec((1, d), const),
            pl.BlockSpec((1, LANES), const),
            pl.BlockSpec((1, LANES), const),
            pl.BlockSpec((1, LANES), const),
            pl.BlockSpec((d, MAIN_COLS), const),
            pl.BlockSpec((d, VALS_COLS), const),
        ],
        out_specs=[
            pl.BlockSpec((tm, MAIN_COLS), lambda i: (i, 0)),
            pl.BlockSpec((1, VT_ROWS, tm), lambda i: (i // nsb, 0, i % nsb)),
            pl.BlockSpec((tm, LANES), lambda i: (i, 0)),
        ],
        out_shape=[
            jax.ShapeDtypeStruct((n, MAIN_COLS), MXU_DTYPE),
            jax.ShapeDtypeStruct((batch, VT_ROWS, seq), MXU_DTYPE),
            jax.ShapeDtypeStruct((n, LANES), F32),
        ],
        compiler_params=_cparams(("parallel",)),
        name="proj",
    )(x2d, pos2d, g, freq, sg1, sg2, w_main, w_v)


def _cumf_kernel(misc_ref, bias_ref, c_ref):
    s = misc_ref.shape[1]
    r = lax.broadcasted_iota(jnp.int32, (TK, TK), 0)
    c = lax.broadcasted_iota(jnp.int32, (TK, TK), 1)
    incl = (c <= r).astype(MXU_DTYPE)
    carry = jnp.zeros((1, LANES), F32)
    for b in range(s // TK):
        logit = misc_ref[0, b * TK:(b + 1) * TK, :] + bias_ref[...]
        logf, _ = _log_sigmoid_parts(logit)
        hi, mid, lo = _split3(logf)
        cs = (_dot(incl, lo) + _dot(incl, mid)) + _dot(incl, hi) + carry
        c_ref[0, b * TK:(b + 1) * TK, :] = cs
        carry = cs[TK - 1:TK, :]


def _cumf(misc3d, bias_row):
    b, s, _ = misc3d.shape
    return pl.pallas_call(
        _cumf_kernel,
        grid=(b,),
        in_specs=[
            pl.BlockSpec((1, s, LANES), lambda i: (i, 0, 0)),
            pl.BlockSpec((1, LANES), lambda i: (0, 0)),
        ],
        out_specs=pl.BlockSpec((1, s, LANES), lambda i: (i, 0, 0)),
        out_shape=jax.ShapeDtypeStruct((b, s, LANES), F32),
        compiler_params=_cparams(("parallel",)),
        name="cumf",
    )(misc3d, bias_row)


def _head_q(q_tile, half):
    lane = lax.broadcasted_iota(jnp.int32, q_tile.shape, 1)
    keep = (lane // HEAD_DIM) == half
    return jnp.where(keep, q_tile.astype(F32) * ATTN_SCALE, 0.0).astype(MXU_DTYPE)


def _causal_tile_t(strict):
    key = lax.broadcasted_iota(jnp.int32, (TK, TQ), 0)
    qry = lax.broadcasted_iota(jnp.int32, (TK, TQ), 1)
    return (key < qry) if strict else (key <= qry)


def _rows(j):
    return pl.ds(pl.multiple_of(j * TK, TK), TK)


def _lookahead_loop(n_last, bufs, issue, step, carry, finish, primed=False, fused=None):
    buf_a, buf_b = bufs

    def ahead_and_step(n, buf, other, carry):
        if fused is not None:
            return fused(n, buf, other, carry)
        issue(n + 1, other)
        return step(n, buf, carry, False)

    def pair(t, carry):
        n = 2 * t
        carry = ahead_and_step(n, buf_a, buf_b, carry)
        return ahead_and_step(n + 1, buf_b, buf_a, carry)

    def odd_tail(carry):
        carry = ahead_and_step(n_last - 1, buf_a, buf_b, carry)
        step(n_last, buf_b, carry, True)
        finish()
        return 0

    def even_tail(carry):
        step(n_last, buf_a, carry, True)
        finish()
        return 0

    if not primed:
        issue(0, buf_a)
    carry = lax.fori_loop(0, n_last // 2, pair, carry)
    lax.cond(n_last % 2 == 1, odd_tail, even_tail, carry)


def _flash_heads(heads, n_full, diag_mask, z_bufs, acc_ref, o_ref):
    half_k, half_q = TK // 2, TQ // 2

    def add_bias(s, bias):
        if isinstance(bias, (list, tuple)):
            return jnp.concatenate([s[:, t * LANES:(t + 1) * LANES] + bias[t] for t in range(len(bias))],
                                   axis=1)
        return s + bias

    ones = jnp.ones((SUM_ROWS, TK), MXU_DTYPE)

    def issue_head(h, j, buf):
        rows = _rows(j)
        qh, k_at, _, bias_at = heads[h]
        s = add_bias(_dot_nt(k_at(rows), qh), bias_at(rows))
        buf[h, :TK, :] = s
        buf[h, TK:, :] = jnp.max(s.reshape(TK // SUBLANES, SUBLANES, TQ), axis=0)

    def issue(j, buf):
        for h in range(len(heads)):
            issue_head(h, j, buf)

    def step_and_issue(j, buf, other, maxes):
        rows = _rows(j)
        out = []
        for h, ((_, _, vt_at, _), m) in enumerate(zip(heads, maxes)):
            m_new = jnp.maximum(m, jnp.max(buf[h, TK:, :], axis=0, keepdims=True))
            p = jnp.exp(buf[h, :TK, :] - m_new).astype(MXU_DTYPE)
            vt1 = jnp.concatenate([vt_at(rows), ones], axis=0)
            acc_ref[h] = jnp.exp(m - m_new) * acc_ref[h] + _dot(vt1, p)
            issue_head(h, j + 1, other)
            out.append(m_new)
        return tuple(out)

    def update(tiles, rows, maxes, lanes, tile_maxes=None):
        stats = []
        for h, (s, m) in enumerate(zip(tiles, maxes)):
            tile_max = jnp.max(s if tile_maxes is None else tile_maxes[h], axis=0, keepdims=True)
            m_new = jnp.maximum(m, tile_max)
            stats.append((m_new, jnp.exp(m - m_new), jnp.exp(s - m_new).astype(MXU_DTYPE)))
        for h, ((_, alpha, p), (_, _, vt_at, _)) in enumerate(zip(stats, heads)):
            vt1 = jnp.concatenate([vt_at(rows), ones[:, :p.shape[0]]], axis=0)
            acc_ref[h, :, lanes] = alpha * acc_ref[h, :, lanes] + _dot(vt1, p)
        return tuple(m_new for m_new, _, _ in stats)

    def step(j, buf, maxes):
        n = len(heads)
        return update([buf[h, :TK, :] for h in range(n)], _rows(j), maxes, slice(None),
                      [buf[h, TK:, :] for h in range(n)])

    def diagonal_step(j, buf, maxes):
        start = pl.multiple_of(j * TK, TK)
        early, late = pl.ds(start, half_k), pl.ds(pl.multiple_of(start + half_k, half_k), half_k)

        def masked(s, mask):
            return s if diag_mask is None else jnp.where(mask, s, NEG_BIG)

        tiles = [masked(buf[h, :half_k, :], None if diag_mask is None else diag_mask[:half_k, :])
                 for h in range(len(heads))]
        maxes = update(tiles, early, maxes, slice(None))
        tiles = [masked(buf[h, half_k:TK, half_q:], None if diag_mask is None else diag_mask[half_k:, half_q:])
                 for h in range(len(heads))]
        update(tiles, late, tuple(m[:, half_q:] for m in maxes), slice(half_q, TQ))
        return maxes

    def finish():
        _store_heads_t(o_ref, [acc_ref[h, :HEAD_DIM, :] / acc_ref[h, HEAD_DIM:HEAD_DIM + 1, :]
                               for h in range(len(heads))])

    acc_ref[...] = jnp.zeros(acc_ref.shape, F32)
    init = tuple(jnp.full((1, TQ), NEG_BIG, F32) for _ in heads)
    _lookahead_loop(n_full, z_bufs, issue,
                    lambda j, buf, maxes, last: (diagonal_step if last else step)(j, buf, maxes), init, finish,
                    fused=step_and_issue)


def _store_heads_t(o_ref, outs_t):
    for t in range(len(outs_t) // 2):
        pair_t = jnp.concatenate([outs_t[2 * t], outs_t[2 * t + 1]], axis=0)
        o_ref[0, :, t * LANES:(t + 1) * LANES] = pair_t.T.astype(o_ref.dtype)


def _sb_kernel(q_ref, k_ref, vt_ref, o_ref, za_ref, zb_ref, acc_ref):
    i = pl.program_id(2)
    r = lax.broadcasted_iota(jnp.int32, (TK, TK), 0)
    c = lax.broadcasted_iota(jnp.int32, (TK, TK), 1)
    after = (c > r).astype(MXU_DTYPE)
    strict = _causal_tile_t(True)
    n_heads = HEADS_PER_STEP
    qhs = [_head_q(q_ref[0][:, (h // 2) * LANES:(h // 2 + 1) * LANES], h % 2) for h in range(n_heads)]

    def issue(j, buf):
        for h in range(n_heads):
            buf[h] = _dot_nt(k_ref[0, _rows(j), (h // 2) * LANES:(h // 2 + 1) * LANES], qhs[h])

    def update(tiles, rows, tails, lanes, mask):
        n_rows = tiles[0].shape[0]
        parts = []
        for z in tiles:
            log_beta, neg_l1m = _log_sigmoid_parts(z)
            if mask is not None:
                neg_l1m = jnp.where(mask, neg_l1m, 0.0)
            parts.append((log_beta, neg_l1m[0:1, :], neg_l1m.astype(MXU_DTYPE)))
        after_rows = after[:n_rows, :n_rows]
        laters = [_dot(after_rows, hl) for (_, _, hl) in parts]
        ws = []
        for (log_beta, _, _), later in zip(parts, laters):
            w = jnp.exp(log_beta - later)
            if mask is not None:
                w = jnp.where(mask, w, 0.0)
            ws.append(w.astype(MXU_DTYPE))
        out = []
        for h, (w, (_, first, _), later, tail) in enumerate(zip(ws, parts, laters, tails)):
            vtj = vt_ref[0, h * HEAD_DIM:(h + 1) * HEAD_DIM, rows]
            acc_ref[h, :, lanes] = acc_ref[h, :, lanes] + jnp.exp(-tail) * _dot(vtj, w)
            out.append(tail + (first + later[0:1, :]))
        return tuple(out)

    def step(j, buf, tails):
        return update([buf[h] for h in range(n_heads)], _rows(j), tails, slice(None), None)

    def diagonal_step(j, buf, tails):
        half_k, half_q = TK // 2, TQ // 2
        start = pl.multiple_of(j * TK, TK)
        early, late = pl.ds(start, half_k), pl.ds(pl.multiple_of(start + half_k, half_k), half_k)
        late_tails = update([buf[h, half_k:, half_q:] for h in range(n_heads)], late,
                            tuple(t[:, half_q:] for t in tails), slice(half_q, TQ), strict[half_k:, half_q:])
        tails = tuple(jnp.concatenate([t[:, :half_q], lt], axis=1) for t, lt in zip(tails, late_tails))
        return update([buf[h, :half_k, :] for h in range(n_heads)], early, tails, slice(None), strict[:half_k, :])

    acc_ref[...] = jnp.zeros(acc_ref.shape, F32)
    carry = tuple(jnp.zeros((1, TQ), F32) for _ in range(n_heads))
    issue(i, za_ref)

    def finish():
        _store_heads_t(o_ref, [acc_ref[h] for h in range(n_heads)])

    def with_earlier_blocks(cr):
        issue(i - 1, zb_ref)
        cr = diagonal_step(i, za_ref, cr)
        _lookahead_loop(i - 1, (zb_ref, za_ref), lambda n, buf: issue(i - 1 - n, buf),
                        lambda n, buf, c, last: step(i - 1 - n, buf, c), cr, finish, primed=True)
        return 0

    def diagonal_only(cr):
        diagonal_step(i, za_ref, cr)
        finish()
        return 0

    lax.cond(i > 0, with_earlier_blocks, diagonal_only, carry)


def _sb(main3d, vt3d):
    b, s, _ = main3d.shape
    nq = s // TQ
    w = HEADS_PER_STEP * HEAD_DIM
    qb, kb, vb = COL_QA // w, COL_KA // w, ROW_VA // w
    return pl.pallas_call(
        _sb_kernel,
        grid=(b, SB_HEADS // HEADS_PER_STEP, nq),
        in_specs=[
            pl.BlockSpec((1, TQ, w), lambda bi, hg, i: (bi, i, qb + hg)),
            pl.BlockSpec((1, s, w), lambda bi, hg, i: (bi, 0, kb + hg)),
            pl.BlockSpec((1, w, s), lambda bi, hg, i: (bi, vb + hg, 0)),
        ],
        out_specs=pl.BlockSpec((1, TQ, w), lambda bi, hg, i: (bi, i, hg)),
        out_shape=jax.ShapeDtypeStruct((b, s, BRANCH_WIDTH), MXU_DTYPE),
        scratch_shapes=[pltpu.VMEM((HEADS_PER_STEP, TK, TQ), F32), pltpu.VMEM((HEADS_PER_STEP, TK, TQ), F32),
                        pltpu.VMEM((HEADS_PER_STEP, HEAD_DIM, TQ), F32)],
        compiler_params=_cparams(("parallel", "parallel", "arbitrary")),
        name="sb",
    )(main3d, main3d, vt3d)


def _fox_kernel(q_ref, k_ref, vt_ref, c_ref, o_ref, cb_ref, za_ref, zb_ref, acc_ref):
    hg = pl.program_id(1)
    i = pl.program_id(2)
    n_heads = HEADS_PER_STEP

    @pl.when(i == 0)
    def _():
        c = c_ref[0]
        lane = lax.broadcasted_iota(jnp.int32, c.shape, 1)
        for h in range(n_heads):
            if n_heads == FOX_HEADS:
                col = c[:, h:h + 1]
            else:
                col = jnp.sum(jnp.where(lane == hg * n_heads + h, c, 0.0), axis=1, keepdims=True)
            cb_ref[h] = jnp.broadcast_to(-col, c.shape)

    heads = []
    for h in range(n_heads):
        t = h // 2
        qh = _head_q(q_ref[0][:, t * LANES:(t + 1) * LANES], h % 2)
        k_at = lambda rows, t=t: k_ref[0, rows, t * LANES:(t + 1) * LANES]
        vt_at = lambda rows, h=h: vt_ref[0, h * HEAD_DIM:(h + 1) * HEAD_DIM, rows]
        bias_at = lambda rows, h=h: [cb_ref[h, rows, :]] * (TQ // LANES)
        heads.append((qh, k_at, vt_at, bias_at))
    _flash_heads(heads, i, _causal_tile_t(False), (za_ref, zb_ref), acc_ref, o_ref)


def _fox(main3d, vt3d, c3d):
    b, s, _ = main3d.shape
    nq = s // TQ
    w = HEADS_PER_STEP * HEAD_DIM
    qb, kb, vb = COL_QB // w, COL_KB // w, ROW_VB // w
    return pl.pallas_call(
        _fox_kernel,
        grid=(b, FOX_HEADS // HEADS_PER_STEP, nq),
        in_specs=[
            pl.BlockSpec((1, TQ, w), lambda bi, hg, i: (bi, i, qb + hg)),
            pl.BlockSpec((1, s, w), lambda bi, hg, i: (bi, 0, kb + hg)),
            pl.BlockSpec((1, w, s), lambda bi, hg, i: (bi, vb + hg, 0)),
            pl.BlockSpec((1, s, LANES), lambda bi, hg, i: (bi, 0, 0)),
        ],
        out_specs=pl.BlockSpec((1, TQ, w), lambda bi, hg, i: (bi, i, hg)),
        out_shape=jax.ShapeDtypeStruct((b, s, BRANCH_WIDTH), MXU_DTYPE),
        scratch_shapes=[pltpu.VMEM((HEADS_PER_STEP, s, LANES), F32),
                        pltpu.VMEM((HEADS_PER_STEP, TK + SUBLANES, TQ), F32),
                        pltpu.VMEM((HEADS_PER_STEP, TK + SUBLANES, TQ), F32),
                        pltpu.VMEM((HEADS_PER_STEP, HEAD_DIM + SUM_ROWS, TQ), F32)],
        compiler_params=_cparams(("parallel", "parallel", "arbitrary")),
        name="fox",
    )(main3d, main3d, vt3d, c3d)


def _dsa_select(i, qi_ref, ki_ref, misc_ref, key_ref, dig_ref, bias_ref, top_k):
    diag = _causal_tile_t(False)

    wi_t = misc_ref[0].T[MISC_WI:MISC_WI + IDX_HEADS, :] * IDX_W_SCALE
    qih = [_head_q(qi_ref[0][:, (h // 2) * LANES:(h // 2 + 1) * LANES], h % 2) for h in range(IDX_HEADS)]

    def score_block(j, mask):
        kij = ki_ref[0, _rows(j), :]
        sc = None
        for h in range(IDX_HEADS):
            term = wi_t[h:h + 1, :] * jnp.maximum(_dot_nt(kij, qih[h]), 0.0)
            sc = term if sc is None else sc + term
        if mask is not None:
            sc = jnp.where(mask, sc, -jnp.inf)
        bits = lax.bitcast_convert_type(jnp.where(sc == 0.0, 0.0, sc), jnp.int32)
        key_ref[_rows(j), :] = bits ^ ((bits >> 31) & jnp.int32(0x7FFFFFFF))

    def score_pair(t, _):
        score_block(2 * t, None)
        score_block(2 * t + 1, None)
        return 0

    def score_odd_tail():
        score_block(i - 1, None)
        score_block(i, diag)
        return 0

    def score_even_tail():
        score_block(i, diag)
        return 0

    lax.fori_loop(0, i // 2, score_pair, 0)
    lax.cond(i % 2 == 1, score_odd_tail, score_even_tail)

    small = MXU_DTYPE
    group = 2 * SUBLANES

    assert (key_ref.shape[0] // group) <= 256

    def count_digits_ge(cand):
        def hits(j):
            hit = jnp.where(dig_ref[_rows(j), :] >= cand, jnp.ones((), small), jnp.zeros((), small))
            part = hit[0:group]
            for g in range(1, TK // group):
                part = part + hit[g * group:(g + 1) * group]
            return part

        acc = lax.fori_loop(0, (i + 1) // 2, lambda t, a: a + (hits(2 * t) + hits(2 * t + 1)),
                            jnp.zeros((group, TQ), small))
        acc = lax.cond((i + 1) % 2 == 1, lambda a: a + hits(i), lambda a: a, acc)
        return jnp.sum(acc.astype(F32), axis=0, keepdims=True)

    above = jnp.zeros((1, TQ), F32)
    prefix = None
    for shift in (24, 16, 8, 0):
        def fill(j, _, shift=shift, prefix=prefix):
            key = key_ref[_rows(j), :]
            if prefix is None:
                digit = (key >> shift) + 128
            else:
                digit = (key ^ (prefix << (shift + 8))) >> shift
                shares = lax.bitcast_convert_type(digit, jnp.uint32) < jnp.uint32(256)
                digit = jnp.where(shares, digit, -1)
            dig_ref[_rows(j), :] = digit.astype(F32).astype(small)
            return 0

        lax.fori_loop(0, i + 1, fill, 0)

        def search(_, st, above=above):
            lo, hi, n_hi = st
            mid = (lo + hi) * 0.5
            n_mid = count_digits_ge(mid.astype(small))
            ok = above + n_mid >= top_k
            return jnp.where(ok, mid, lo), jnp.where(ok, hi, mid), jnp.where(ok, n_hi, n_mid)

        start = (jnp.zeros((1, TQ), F32), jnp.full((1, TQ), 256.0, F32), jnp.zeros((1, TQ), F32))
        digit, _, n_hi = lax.fori_loop(0, 8, search, start)
        above = above + n_hi
        digit = digit.astype(jnp.int32)
        prefix = digit - 128 if prefix is None else (prefix << 8) | digit
    thr_key = prefix

    need = top_k - above
    kr = lax.broadcasted_iota(jnp.int32, (TK, TK), 0)
    kc = lax.broadcasted_iota(jnp.int32, (TK, TK), 1)
    upto = (kc <= kr).astype(MXU_DTYPE)

    def bias_block(j, mask, seen):
        key = key_ref[_rows(j), :]
        tie = key == thr_key
        rank = _dot(upto, jnp.where(tie, 1.0, 0.0).astype(MXU_DTYPE)) + seen
        sel = (key > thr_key) | (tie & (rank <= need))
        if mask is not None:
            sel = sel & mask
        bias_ref[_rows(j), :] = jnp.where(sel, 0.0, NEG_BIG)
        return rank[TK - 1:TK, :]

    def bias_pair(t, seen):
        return bias_block(2 * t + 1, None, bias_block(2 * t, None, seen))

    def bias_odd_tail(seen):
        bias_block(i, diag, bias_block(i - 1, None, seen))
        return 0

    def bias_even_tail(seen):
        bias_block(i, diag, seen)
        return 0

    seen = lax.fori_loop(0, i // 2, bias_pair, jnp.zeros((1, TQ), F32))
    lax.cond(i % 2 == 1, bias_odd_tail, bias_even_tail, seen)


def _dsa_kernel(q_ref, k_ref, vt_ref, qi_ref, ki_ref, misc_ref, o_ref, key_ref, dig_ref, bias_ref,
                za_ref, zb_ref, acc_ref, *, top_k):
    i = pl.program_id(1)

    def every_causal_key():
        def earlier_block(j, _):
            bias_ref[_rows(j), :] = jnp.zeros((TK, TQ), F32)
            return 0

        lax.fori_loop(0, i, earlier_block, 0)
        bias_ref[_rows(i), :] = jnp.where(_causal_tile_t(False), 0.0, NEG_BIG)
        return 0

    def searched_keys():
        _dsa_select(i, qi_ref, ki_ref, misc_ref, key_ref, dig_ref, bias_ref, top_k)
        return 0

    lax.cond((i + 1) * TQ <= top_k, every_causal_key, searched_keys)

    bias_at = lambda rows: bias_ref[rows, :]
    heads = []
    for h in range(DSA_HEADS):
        g = h // DSA_GROUP
        k_at = lambda rows, g=g: k_ref[0, rows, g * LANES:(g + 1) * LANES]
        vt_at = lambda rows, g=g: vt_ref[0, g * HEAD_DIM:(g + 1) * HEAD_DIM, rows]
        qh = _head_q(q_ref[0][:, (h // 2) * LANES:(h // 2 + 1) * LANES], h % 2)
        heads.append((qh, k_at, vt_at, bias_at))
    _flash_heads(heads, i, None, (za_ref, zb_ref), acc_ref, o_ref)


def _dsa(main3d, vt3d, misc3d, top_k):
    b, s, _ = main3d.shape
    nq = s // TQ
    assert top_k <= TK, "the first key block must hold at least top_k keys"
    return pl.pallas_call(
        functools.partial(_dsa_kernel, top_k=top_k),
        grid=(b, nq),
        in_specs=[
            pl.BlockSpec((1, TQ, 4 * LANES), lambda bi, i: (bi, i, COL_QC // (4 * LANES))),
            pl.BlockSpec((1, s, 2 * LANES), lambda bi, i: (bi, 0, COL_KC // (2 * LANES))),
            pl.BlockSpec((1, LANES, s), lambda bi, i: (bi, ROW_VC // LANES, 0)),
            pl.BlockSpec((1, TQ, 2 * LANES), lambda bi, i: (bi, i, COL_QI // (2 * LANES))),
            pl.BlockSpec((1, s, LANES), lambda bi, i: (bi, 0, COL_KI // LANES)),
            pl.BlockSpec((1, TQ, LANES), lambda bi, i: (bi, i, 0)),
        ],
        out_specs=pl.BlockSpec((1, TQ, BRANCH_WIDTH), lambda bi, i: (bi, i, 0)),
        out_shape=jax.ShapeDtypeStruct((b, s, BRANCH_WIDTH), MXU_DTYPE),
        scratch_shapes=[pltpu.VMEM((s, TQ), jnp.int32), pltpu.VMEM((s, TQ), MXU_DTYPE), pltpu.VMEM((s, TQ), F32),
                        pltpu.VMEM((DSA_HEADS, TK + SUBLANES, TQ), F32),
                        pltpu.VMEM((DSA_HEADS, TK + SUBLANES, TQ), F32),
                        pltpu.VMEM((DSA_HEADS, HEAD_DIM + SUM_ROWS, TQ), F32)],
        compiler_params=_cparams(("parallel", "arbitrary")),
        name="dsa",
    )(main3d, main3d, vt3d, main3d, main3d, misc3d)


def _merge_kernel(x_ref, a_ref, b_ref, c_ref, g_ref, wg_ref, wb_ref, wo_ref, o_ref):
    x = x_ref[...]
    d = x.shape[1]
    h = _rmsnorm(x, g_ref[...]).astype(MXU_DTYPE)
    y = None
    for n, br_ref in enumerate((a_ref, b_ref, c_ref)):
        gate = jax.nn.sigmoid(_dot(h, wg_ref[:, n * d:(n + 1) * d]))
        term = gate * _dot(br_ref[...], wb_ref[n])
        y = term if y is None else y + term
    o_ref[...] = x + _dot(y.astype(MXU_DTYPE), wo_ref[...])


def _merge(x2d, br_a, br_b, br_c, g, w_gate, w_branch, w_out, tm):
    n, d = x2d.shape
    bw = br_a.shape[1]
    row = lambda i: (i, 0)
    const = lambda i: (0, 0)
    return pl.pallas_call(
        _merge_kernel,
        grid=(n // tm,),
        in_specs=[
            pl.BlockSpec((tm, d), row),
            pl.BlockSpec((tm, bw), row),
            pl.BlockSpec((tm, bw), row),
            pl.BlockSpec((tm, bw), row),
            pl.BlockSpec((1, d), const),
            pl.BlockSpec((d, N_BRANCH * d), const),
            pl.BlockSpec((N_BRANCH, bw, d), lambda i: (0, 0, 0)),
            pl.BlockSpec((d, d), const),
        ],
        out_specs=pl.BlockSpec((tm, d), row),
        out_shape=jax.ShapeDtypeStruct((n, d), F32),
        compiler_params=_cparams(("parallel",)),
        name="merge",
    )(x2d, br_a, br_b, br_c, g, w_gate, w_branch, w_out)


def _mlp_kernel(x_ref, g_ref, wu_ref, wd_ref, gf_ref, o_ref, *, ff_chunk, final_norm):
    x = x_ref[...]
    h = _rmsnorm(x, g_ref[...]).astype(MXU_DTYPE)
    acc = x
    for c in range(wu_ref.shape[1] // ff_chunk):
        u = jnp.maximum(_dot(h, wu_ref[:, c * ff_chunk:(c + 1) * ff_chunk]), 0.0)
        acc = acc + _dot((u * u).astype(MXU_DTYPE), wd_ref[c * ff_chunk:(c + 1) * ff_chunk, :])
    if final_norm:
        acc = _rmsnorm(acc, gf_ref[...])
    o_ref[...] = acc


def _mlp(x2d, g, w_up, w_down, g_final, tm, final_norm):
    n, d = x2d.shape
    ff = w_up.shape[1]
    row = lambda i: (i, 0)
    const = lambda i: (0, 0)
    return pl.pallas_call(
        functools.partial(_mlp_kernel, ff_chunk=min(ff, 1024), final_norm=final_norm),
        grid=(n // tm,),
        in_specs=[
            pl.BlockSpec((tm, d), row),
            pl.BlockSpec((1, d), const),
            pl.BlockSpec((d, ff), const),
            pl.BlockSpec((ff, d), const),
            pl.BlockSpec((1, d), const),
        ],
        out_specs=pl.BlockSpec((tm, d), row),
        out_shape=jax.ShapeDtypeStruct((n, d), F32),
        compiler_params=_cparams(("parallel",)),
        name="mlp",
    )(x2d, g, w_up, w_down, g_final)


def _w_in_columns(d_model):
    sizes = (("qa", BRANCH_WIDTH), ("ka", BRANCH_WIDTH), ("va", BRANCH_WIDTH),
             ("qb", BRANCH_WIDTH), ("kb", BRANCH_WIDTH), ("vb", BRANCH_WIDTH), ("fl", FOX_HEADS),
             ("qc", DSA_HEADS * HEAD_DIM), ("kc", DSA_KV_HEADS * HEAD_DIM), ("vc", DSA_KV_HEADS * HEAD_DIM),
             ("qi", IDX_HEADS * IDX_DIM), ("ki", IDX_DIM), ("wi", IDX_HEADS), ("gates", N_BRANCH * d_model))
    cols, o = {}, 0
    for name, width in sizes:
        cols[name] = (o, o + width)
        o += width
    return cols, o


def _pack_kernel(w_ref, main_ref, vals_ref, gates_ref):
    cols, _ = _w_in_columns(gates_ref.shape[2] // N_BRANCH)

    def src(name, lo=0, hi=None):
        a, b = cols[name]
        return w_ref[0, :, a + lo:(b if hi is None else a + hi)]

    def put(ref, start, value):
        ref[0, :, start:start + value.shape[1]] = value.astype(ref.dtype)

    put(main_ref, COL_QA, src("qa"))
    put(main_ref, COL_KA, src("ka"))
    put(main_ref, COL_QB, src("qb"))
    put(main_ref, COL_KB, src("kb"))
    put(main_ref, COL_QC, src("qc"))
    for g in range(DSA_KV_HEADS):
        kg = src("kc", g * HEAD_DIM, (g + 1) * HEAD_DIM)
        put(main_ref, COL_KC + g * LANES, jnp.concatenate([kg, kg], axis=1))
    put(main_ref, COL_QI, src("qi"))
    ki = src("ki")
    put(main_ref, COL_KI, jnp.concatenate([ki, ki], axis=1))
    put(vals_ref, ROW_VA, src("va"))
    put(vals_ref, ROW_VB, src("vb"))
    put(vals_ref, ROW_VC, src("vc"))
    pad = jnp.zeros((w_ref.shape[1], LANES - FOX_HEADS - IDX_HEADS), F32)
    put(vals_ref, VT_ROWS, jnp.concatenate([src("fl"), src("wi"), pad], axis=1))
    put(gates_ref, 0, src("gates"))


def _pack_w_in(w_in):
    depth, d, total = w_in.shape
    _, end = _w_in_columns(d)
    assert end == total
    rows = 256
    blk = lambda width: pl.BlockSpec((1, rows, width), lambda l, r: (l, r, 0))
    widths = (MAIN_COLS, VALS_COLS, N_BRANCH * d)
    return pl.pallas_call(
        _pack_kernel,
        grid=(depth, d // rows),
        in_specs=[blk(total)],
        out_specs=[blk(wd) for wd in widths],
        out_shape=[jax.ShapeDtypeStruct((depth, d, wd), MXU_DTYPE) for wd in widths],
        compiler_params=_cparams(("parallel", "parallel")),
        name="pack",
    )(w_in)


def _rope_rows():
    lane = jnp.arange(LANES) % HEAD_DIM
    half = ROT_DIM // 2
    inv_freq = ROPE_THETA ** (-jnp.arange(0, ROT_DIM, 2, dtype=F32) / ROT_DIM)
    freq = jnp.where(lane < ROT_DIM, inv_freq[lane % half], 0.0).astype(F32)
    sg1 = jnp.where(lane < half, -1.0, 0.0).astype(F32)
    sg2 = jnp.where((lane >= half) & (lane < ROT_DIM), 1.0, 0.0).astype(F32)
    return freq[None, :], sg1[None, :], sg2[None, :]


def kernel(x, positions, g_mix, w_in, b_forget, w_branch, w_out, g_mlp, w_up, w_down, g_final):
    b, s, d = x.shape
    n = b * s
    depth = w_in.shape[0]
    top_k = min(TOPK_MAX, s // 4)
    tm = min(512, s)
    freq, sg1, sg2 = _rope_rows()
    pos2d = positions.reshape(n, 1)
    x2d = x.reshape(n, d)
    w_main_all, w_v_all, w_gate_all = _pack_w_in(w_in)
    for layer in range(depth):
        w_main, w_v, w_gate = w_main_all[layer], w_v_all[layer], w_gate_all[layer]
        bias_row = jnp.concatenate([b_forget[layer], jnp.zeros((LANES - FOX_HEADS,), F32)])[None, :]
        main, vt, misc = _proj(x2d, pos2d, g_mix[layer][None, :], freq, sg1, sg2, w_main, w_v, tm, b, s)
        main3d = main.reshape(b, s, MAIN_COLS)
        misc3d = misc.reshape(b, s, LANES)
        cum = _cumf(misc3d, bias_row)
        br_a = _sb(main3d, vt).reshape(n, BRANCH_WIDTH)
        br_b = _fox(main3d, vt, cum).reshape(n, BRANCH_WIDTH)
        br_c = _dsa(main3d, vt, misc3d, top_k).reshape(n, BRANCH_WIDTH)
        x2d = _merge(x2d, br_a, br_b, br_c, g_mix[layer][None, :], w_gate,
                     w_branch[layer].astype(MXU_DTYPE), w_out[layer].astype(MXU_DTYPE), tm)
        x2d = _mlp(x2d, g_mlp[layer][None, :], w_up[layer].astype(MXU_DTYPE),
                   w_down[layer].astype(MXU_DTYPE), g_final[None, :], tm, layer == depth - 1)
    return x2d.reshape(b, s, d)
```

```python
import functools

import jax
import jax.numpy as jnp
from jax import lax
from jax.experimental import pallas as pl
from jax.experimental.pallas import tpu as pltpu

F32 = jnp.float32
MXU_DTYPE = jnp.bfloat16

HEAD_DIM = 64
SB_HEADS = 8
FOX_HEADS = 8
DSA_HEADS = 8
DSA_KV_HEADS = 2
DSA_GROUP = DSA_HEADS // DSA_KV_HEADS
IDX_HEADS = 4
IDX_DIM = 64
N_BRANCH = 3
ROPE_THETA = 500000.0
ROT_DIM = HEAD_DIM // 4
TOPK_MAX = 256
NORM_EPS = 1e-6

LANES = 128
SUBLANES = 8
SUM_ROWS = 2 * SUBLANES
BRANCH_WIDTH = SB_HEADS * HEAD_DIM
ATTN_SCALE = HEAD_DIM ** -0.5
assert IDX_DIM == HEAD_DIM
IDX_W_SCALE = IDX_HEADS ** -0.5

COL_QA, COL_KA = 0, 512
COL_QB, COL_KB = 1024, 1536
COL_QC = 2048
COL_KC = 2560
COL_QI = 2816
COL_KI = 3072
MAIN_COLS = 3200
ROPE_START = COL_QC
ROW_VA, ROW_VB, ROW_VC = 0, 512, 1024
VT_ROWS = 1152
VALS_COLS = VT_ROWS + LANES
MISC_FL, MISC_WI = 0, 8

TQ = 256
TK = 256
HEADS_PER_STEP = 8
NEG_BIG = -1e30

VMEM_LIMIT = 56 * 1024 * 1024


def _cparams(sem):
    return pltpu.CompilerParams(dimension_semantics=sem, vmem_limit_bytes=VMEM_LIMIT)


def _rmsnorm(x, g):
    y = x * lax.rsqrt(jnp.mean(x * x, axis=-1, keepdims=True) + NORM_EPS)
    return y * g


def _dot(a, b):
    return jnp.dot(a, b, preferred_element_type=F32)


def _dot_nt(a, b):
    return lax.dot_general(a, b, (((1,), (1,)), ((), ())), preferred_element_type=F32)


def _split2(x):
    hi = x.astype(MXU_DTYPE)
    lo = (x - hi.astype(F32)).astype(MXU_DTYPE)
    return hi, lo


def _split3(x):
    hi = x.astype(MXU_DTYPE)
    r = x - hi.astype(F32)
    mid = r.astype(MXU_DTYPE)
    lo = (r - mid.astype(F32)).astype(MXU_DTYPE)
    return hi, mid, lo


def _log_sigmoid_parts(z):
    neg_abs = lax.bitcast_convert_type(lax.bitcast_convert_type(z, jnp.int32) | jnp.int32(-2 ** 31), F32)
    sp = jnp.log(1.0 + jnp.exp(neg_abs))
    log_sig = jnp.minimum(z, 0.0) - sp
    return log_sig, z - log_sig


def _proj_kernel(x_ref, pos_ref, g_ref, freq_ref, sg1_ref, sg2_ref, wm_ref, wv_ref,
                 main_ref, vt_ref, misc_ref):
    h = _rmsnorm(x_ref[...], g_ref[...]).astype(MXU_DTYPE)
    ang = pos_ref[...].astype(F32) * freq_ref[...]
    cos = jnp.cos(ang)
    sin = jnp.sin(ang)
    s_up = sin * sg1_ref[...]
    s_dn = sin * sg2_ref[...]
    half = ROT_DIM // 2
    chunk = 4 * LANES
    for c0 in range(0, VALS_COLS, 2 * LANES):
        pv = _dot(h, wv_ref[:, c0:c0 + 2 * LANES])
        for r0 in (c0, c0 + LANES):
            piece = pv[:, r0 - c0:r0 - c0 + LANES]
            if r0 < VT_ROWS:
                vt_ref[0, r0:r0 + LANES, :] = piece.T.astype(vt_ref.dtype)
            else:
                misc_ref[...] = piece
    for c0 in list(range(ROPE_START, MAIN_COLS, chunk)) + list(range(0, ROPE_START, chunk)):
        c1 = min(c0 + chunk, MAIN_COLS)
        p = _dot(h, wm_ref[:, c0:c1])
        if c0 >= ROPE_START:
            for t in range((c1 - c0) // LANES):
                pt = p[:, t * LANES:(t + 1) * LANES]
                pt = (pt * cos + pltpu.roll(pt, LANES - half, 1) * s_up
                      + pltpu.roll(pt, half, 1) * s_dn)
                main_ref[:, c0 + t * LANES:c0 + (t + 1) * LANES] = pt.astype(main_ref.dtype)
        else:
            main_ref[:, c0:c1] = p.astype(main_ref.dtype)


def _proj(x2d, pos2d, g, freq, sg1, sg2, w_main, w_v, tm, batch, seq):
    n, d = x2d.shape
    nsb = seq // tm
    const = lambda i: (0, 0)
    return pl.pallas_call(
        _proj_kernel,
        grid=(n // tm,),
        in_specs=[
            pl.BlockSpec((tm, d), lambda i: (i, 0)),
            pl.BlockSpec((tm, 1), lambda i: (i, 0)),
            pl.BlockSpec((1, d), const),
            pl.BlockSpec((1, LANES), const),
            pl.BlockSpec((1, LANES), const),
            pl.BlockSpec((1, LANES), const),
            pl.BlockSpec((d, MAIN_COLS), const),
            pl.BlockSpec((d, VALS_COLS), const),
        ],
        out_specs=[
            pl.BlockSpec((tm, MAIN_COLS), lambda i: (i, 0)),
            pl.BlockSpec((1, VT_ROWS, tm), lambda i: (i // nsb, 0, i % nsb)),
            pl.BlockSpec((tm, LANES), lambda i: (i, 0)),
        ],
        out_shape=[
            jax.ShapeDtypeStruct((n, MAIN_COLS), MXU_DTYPE),
            jax.ShapeDtypeStruct((batch, VT_ROWS, seq), MXU_DTYPE),
            jax.ShapeDtypeStruct((n, LANES), F32),
        ],
        compiler_params=_cparams(("parallel",)),
        name="proj",
    )(x2d, pos2d, g, freq, sg1, sg2, w_main, w_v)


def _cumf_kernel(misc_ref, bias_ref, c_ref):
    s = misc_ref.shape[1]
    r = lax.broadcasted_iota(jnp.int32, (TK, TK), 0)
    c = lax.broadcasted_iota(jnp.int32, (TK, TK), 1)
    incl = (c <= r).astype(MXU_DTYPE)
    carry = jnp.zeros((1, LANES), F32)
    for b in range(s // TK):
        logit = misc_ref[0, b * TK:(b + 1) * TK, :] + bias_ref[...]
        logf, _ = _log_sigmoid_parts(logit)
        hi, mid, lo = _split3(logf)
        cs = (_dot(incl, lo) + _dot(incl, mid)) + _dot(incl, hi) + carry
        c_ref[0, b * TK:(b + 1) * TK, :] = cs
        carry = cs[TK - 1:TK, :]


def _cumf(misc3d, bias_row):
    b, s, _ = misc3d.shape
    return pl.pallas_call(
        _cumf_kernel,
        grid=(b,),
        in_specs=[
            pl.BlockSpec((1, s, LANES), lambda i: (i, 0, 0)),
            pl.BlockSpec((1, LANES), lambda i: (0, 0)),
        ],
        out_specs=pl.BlockSpec((1, s, LANES), lambda i: (i, 0, 0)),
        out_shape=jax.ShapeDtypeStruct((b, s, LANES), F32),
        compiler_params=_cparams(("parallel",)),
        name="cumf",
    )(misc3d, bias_row)


def _head_q(q_tile, half):
    lane = lax.broadcasted_iota(jnp.int32, q_tile.shape, 1)
    keep = (lane // HEAD_DIM) == half
    return jnp.where(keep, q_tile.astype(F32) * ATTN_SCALE, 0.0).astype(MXU_DTYPE)


def _causal_tile_t(strict):
    key = lax.broadcasted_iota(jnp.int32, (TK, TQ), 0)
    qry = lax.broadcasted_iota(jnp.int32, (TK, TQ), 1)
    return (key < qry) if strict else (key <= qry)


def _rows(j):
    return pl.ds(pl.multiple_of(j * TK, TK), TK)


def _lookahead_loop(n_last, bufs, issue, step, carry, finish, primed=False, fused=None):
    buf_a, buf_b = bufs

    def ahead_and_step(n, buf, other, carry):
        if fused is not None:
            return fused(n, buf, other, carry)
        issue(n + 1, other)
        return step(n, buf, carry, False)

    def pair(t, carry):
        n = 2 * t
        carry = ahead_and_step(n, buf_a, buf_b, carry)
        return ahead_and_step(n + 1, buf_b, buf_a, carry)

    def odd_tail(carry):
        carry = ahead_and_step(n_last - 1, buf_a, buf_b, carry)
        step(n_last, buf_b, carry, True)
        finish()
        return 0

    def even_tail(carry):
        step(n_last, buf_a, carry, True)
        finish()
        return 0

    if not primed:
        issue(0, buf_a)
    carry = lax.fori_loop(0, n_last // 2, pair, carry)
    lax.cond(n_last % 2 == 1, odd_tail, even_tail, carry)


def _flash_heads(heads, n_full, diag_mask, z_bufs, acc_ref, o_ref):
    half_k, half_q = TK // 2, TQ // 2

    def add_bias(s, bias):
        if isinstance(bias, (list, tuple)):
            return jnp.concatenate([s[:, t * LANES:(t + 1) * LANES] + bias[t] for t in range(len(bias))],
                                   axis=1)
        return s + bias

    ones = jnp.ones((SUM_ROWS, TK), MXU_DTYPE)

    def issue_head(h, j, buf):
        rows = _rows(j)
        qh, k_at, _, bias_at = heads[h]
        s = add_bias(_dot_nt(k_at(rows), qh), bias_at(rows))
        buf[h, :TK, :] = s
        buf[h, TK:, :] = jnp.max(s.reshape(TK // SUBLANES, SUBLANES, TQ), axis=0)

    def issue(j, buf):
        for h in range(len(heads)):
            issue_head(h, j, buf)

    def step_and_issue(j, buf, other, maxes):
        rows = _rows(j)
        out = []
        for h, ((_, _, vt_at, _), m) in enumerate(zip(heads, maxes)):
            m_new = jnp.maximum(m, jnp.max(buf[h, TK:, :], axis=0, keepdims=True))
            p = jnp.exp(buf[h, :TK, :] - m_new).astype(MXU_DTYPE)
            vt1 = jnp.concatenate([vt_at(rows), ones], axis=0)
            acc_ref[h] = jnp.exp(m - m_new) * acc_ref[h] + _dot(vt1, p)
            issue_head(h, j + 1, other)
            out.append(m_new)
        return tuple(out)

    def update(tiles, rows, maxes, lanes, tile_maxes=None):
        out = []
        for h, (s, m, (_, _, vt_at, _)) in enumerate(zip(tiles, maxes, heads)):
            tile_max = jnp.max(s if tile_maxes is None else tile_maxes[h], axis=0, keepdims=True)
            m_new = jnp.maximum(m, tile_max)
            p = jnp.exp(s - m_new).astype(MXU_DTYPE)
            vt1 = jnp.concatenate([vt_at(rows), ones[:, :p.shape[0]]], axis=0)
            acc_ref[h, :, lanes] = jnp.exp(m - m_new) * acc_ref[h, :, lanes] + _dot(vt1, p)
            out.append(m_new)
        return tuple(out)

    def step(j, buf, maxes):
        n = len(heads)
        return update([buf[h, :TK, :] for h in range(n)], _rows(j), maxes, slice(None),
                      [buf[h, TK:, :] for h in range(n)])

    def diagonal_step(j, buf, maxes):
        start = pl.multiple_of(j * TK, TK)
        early, late = pl.ds(start, half_k), pl.ds(pl.multiple_of(start + half_k, half_k), half_k)

        def masked(s, mask):
            return s if diag_mask is None else jnp.where(mask, s, NEG_BIG)

        tiles = [masked(buf[h, :half_k, :], None if diag_mask is None else diag_mask[:half_k, :])
                 for h in range(len(heads))]
        maxes = update(tiles, early, maxes, slice(None))
        tiles = [masked(buf[h, half_k:TK, half_q:], None if diag_mask is None else diag_mask[half_k:, half_q:])
                 for h in range(len(heads))]
        update(tiles, late, tuple(m[:, half_q:] for m in maxes), slice(half_q, TQ))
        return maxes

    def finish():
        _store_heads_t(o_ref, [acc_ref[h, :HEAD_DIM, :] / acc_ref[h, HEAD_DIM:HEAD_DIM + 1, :]
                               for h in range(len(heads))])

    acc_ref[...] = jnp.zeros(acc_ref.shape, F32)
    init = tuple(jnp.full((1, TQ), NEG_BIG, F32) for _ in heads)
    _lookahead_loop(n_full, z_bufs, issue,
                    lambda j, buf, maxes, last: (diagonal_step if last else step)(j, buf, maxes), init, finish,
                    fused=step_and_issue)


def _store_heads_t(o_ref, outs_t):
    for t in range(len(outs_t) // 2):
        pair_t = jnp.concatenate([outs_t[2 * t], outs_t[2 * t + 1]], axis=0)
        o_ref[0, :, t * LANES:(t + 1) * LANES] = pair_t.T.astype(o_ref.dtype)


def _sb_kernel(q_ref, k_ref, vt_ref, o_ref, za_ref, zb_ref, acc_ref):
    i = pl.program_id(2)
    r = lax.broadcasted_iota(jnp.int32, (TK, TK), 0)
    c = lax.broadcasted_iota(jnp.int32, (TK, TK), 1)
    after = (c > r).astype(MXU_DTYPE)
    strict = _causal_tile_t(True)
    n_heads = HEADS_PER_STEP
    qhs = [_head_q(q_ref[0][:, (h // 2) * LANES:(h // 2 + 1) * LANES], h % 2) for h in range(n_heads)]

    def issue(j, buf):
        for h in range(n_heads):
            buf[h] = _dot_nt(k_ref[0, _rows(j), (h // 2) * LANES:(h // 2 + 1) * LANES], qhs[h])

    def update(tiles, rows, tails, lanes, mask):
        n_rows = tiles[0].shape[0]
        parts = []
        for z in tiles:
            log_beta, neg_l1m = _log_sigmoid_parts(z)
            if mask is not None:
                neg_l1m = jnp.where(mask, neg_l1m, 0.0)
            parts.append((log_beta, neg_l1m[0:1, :], neg_l1m.astype(MXU_DTYPE)))
        after_rows = after[:n_rows, :n_rows]
        laters = [_dot(after_rows, hl) for (_, _, hl) in parts]
        ws = []
        for (log_beta, _, _), later in zip(parts, laters):
            w = jnp.exp(log_beta - later)
            if mask is not None:
                w = jnp.where(mask, w, 0.0)
            ws.append(w.astype(MXU_DTYPE))
        out = []
        for h, (w, (_, first, _), later, tail) in enumerate(zip(ws, parts, laters, tails)):
            vtj = vt_ref[0, h * HEAD_DIM:(h + 1) * HEAD_DIM, rows]
            acc_ref[h, :, lanes] = acc_ref[h, :, lanes] + jnp.exp(-tail) * _dot(vtj, w)
            out.append(tail + (first + later[0:1, :]))
        return tuple(out)

    def step(j, buf, tails):
        return update([buf[h] for h in range(n_heads)], _rows(j), tails, slice(None), None)

    def diagonal_step(j, buf, tails):
        half_k, half_q = TK // 2, TQ // 2
        start = pl.multiple_of(j * TK, TK)
        early, late = pl.ds(start, half_k), pl.ds(pl.multiple_of(start + half_k, half_k), half_k)
        late_tails = update([buf[h, half_k:, half_q:] for h in range(n_heads)], late,
                            tuple(t[:, half_q:] for t in tails), slice(half_q, TQ), strict[half_k:, half_q:])
        tails = tuple(jnp.concatenate([t[:, :half_q], lt], axis=1) for t, lt in zip(tails, late_tails))
        return update([buf[h, :half_k, :] for h in range(n_heads)], early, tails, slice(None), strict[:half_k, :])

    acc_ref[...] = jnp.zeros(acc_ref.shape, F32)
    carry = tuple(jnp.zeros((1, TQ), F32) for _ in range(n_heads))
    issue(i, za_ref)

    def finish():
        _store_heads_t(o_ref, [acc_ref[h] for h in range(n_heads)])

    def with_earlier_blocks(cr):
        issue(i - 1, zb_ref)
        cr = diagonal_step(i, za_ref, cr)
        _lookahead_loop(i - 1, (zb_ref, za_ref), lambda n, buf: issue(i - 1 - n, buf),
                        lambda n, buf, c, last: step(i - 1 - n, buf, c), cr, finish, primed=True)
        return 0

    def diagonal_only(cr):
        diagonal_step(i, za_ref, cr)
        finish()
        return 0

    lax.cond(i > 0, with_earlier_blocks, diagonal_only, carry)


def _sb(main3d, vt3d):
    b, s, _ = main3d.shape
    nq = s // TQ
    w = HEADS_PER_STEP * HEAD_DIM
    qb, kb, vb = COL_QA // w, COL_KA // w, ROW_VA // w
    return pl.pallas_call(
        _sb_kernel,
        grid=(b, SB_HEADS // HEADS_PER_STEP, nq),
        in_specs=[
            pl.BlockSpec((1, TQ, w), lambda bi, hg, i: (bi, i, qb + hg)),
            pl.BlockSpec((1, s, w), lambda bi, hg, i: (bi, 0, kb + hg)),
            pl.BlockSpec((1, w, s), lambda bi, hg, i: (bi, vb + hg, 0)),
        ],
        out_specs=pl.BlockSpec((1, TQ, w), lambda bi, hg, i: (bi, i, hg)),
        out_shape=jax.ShapeDtypeStruct((b, s, BRANCH_WIDTH), MXU_DTYPE),
        scratch_shapes=[pltpu.VMEM((HEADS_PER_STEP, TK, TQ), F32), pltpu.VMEM((HEADS_PER_STEP, TK, TQ), F32),
                        pltpu.VMEM((HEADS_PER_STEP, HEAD_DIM, TQ), F32)],
        compiler_params=_cparams(("parallel", "parallel", "arbitrary")),
        name="sb",
    )(main3d, main3d, vt3d)


def _fox_kernel(q_ref, k_ref, vt_ref, c_ref, o_ref, cb_ref, za_ref, zb_ref, acc_ref):
    hg = pl.program_id(1)
    i = pl.program_id(2)
    n_heads = HEADS_PER_STEP

    @pl.when(i == 0)
    def _():
        c = c_ref[0]
        lane = lax.broadcasted_iota(jnp.int32, c.shape, 1)
        for h in range(n_heads):
            if n_heads == FOX_HEADS:
                col = c[:, h:h + 1]
            else:
                col = jnp.sum(jnp.where(lane == hg * n_heads + h, c, 0.0), axis=1, keepdims=True)
            cb_ref[h] = jnp.broadcast_to(-col, c.shape)

    heads = []
    for h in range(n_heads):
        t = h // 2
        qh = _head_q(q_ref[0][:, t * LANES:(t + 1) * LANES], h % 2)
        k_at = lambda rows, t=t: k_ref[0, rows, t * LANES:(t + 1) * LANES]
        vt_at = lambda rows, h=h: vt_ref[0, h * HEAD_DIM:(h + 1) * HEAD_DIM, rows]
        bias_at = lambda rows, h=h: [cb_ref[h, rows, :]] * (TQ // LANES)
        heads.append((qh, k_at, vt_at, bias_at))
    _flash_heads(heads, i, _causal_tile_t(False), (za_ref, zb_ref), acc_ref, o_ref)


def _fox(main3d, vt3d, c3d):
    b, s, _ = main3d.shape
    nq = s // TQ
    w = HEADS_PER_STEP * HEAD_DIM
    qb, kb, vb = COL_QB // w, COL_KB // w, ROW_VB // w
    return pl.pallas_call(
        _fox_kernel,
        grid=(b, FOX_HEADS // HEADS_PER_STEP, nq),
        in_specs=[
            pl.BlockSpec((1, TQ, w), lambda bi, hg, i: (bi, i, qb + hg)),
            pl.BlockSpec((1, s, w), lambda bi, hg, i: (bi, 0, kb + hg)),
            pl.BlockSpec((1, w, s), lambda bi, hg, i: (bi, vb + hg, 0)),
            pl.BlockSpec((1, s, LANES), lambda bi, hg, i: (bi, 0, 0)),
        ],
        out_specs=pl.BlockSpec((1, TQ, w), lambda bi, hg, i: (bi, i, hg)),
        out_shape=jax.ShapeDtypeStruct((b, s, BRANCH_WIDTH), MXU_DTYPE),
        scratch_shapes=[pltpu.VMEM((HEADS_PER_STEP, s, LANES), F32),
                        pltpu.VMEM((HEADS_PER_STEP, TK + SUBLANES, TQ), F32),
                        pltpu.VMEM((HEADS_PER_STEP, TK + SUBLANES, TQ), F32),
                        pltpu.VMEM((HEADS_PER_STEP, HEAD_DIM + SUM_ROWS, TQ), F32)],
        compiler_params=_cparams(("parallel", "parallel", "arbitrary")),
        name="fox",
    )(main3d, main3d, vt3d, c3d)


def _dsa_select(i, qi_ref, ki_ref, misc_ref, key_ref, dig_ref, bias_ref, top_k):
    diag = _causal_tile_t(False)

    wi_t = misc_ref[0].T[MISC_WI:MISC_WI + IDX_HEADS, :] * IDX_W_SCALE
    qih = [_head_q(qi_ref[0][:, (h // 2) * LANES:(h // 2 + 1) * LANES], h % 2) for h in range(IDX_HEADS)]

    def score_block(j, mask):
        kij = ki_ref[0, _rows(j), :]
        sc = None
        for h in range(IDX_HEADS):
            term = wi_t[h:h + 1, :] * jnp.maximum(_dot_nt(kij, qih[h]), 0.0)
            sc = term if sc is None else sc + term
        if mask is not None:
            sc = jnp.where(mask, sc, -jnp.inf)
        bits = lax.bitcast_convert_type(jnp.where(sc == 0.0, 0.0, sc), jnp.int32)
        key_ref[_rows(j), :] = bits ^ ((bits >> 31) & jnp.int32(0x7FFFFFFF))

    def score_pair(t, _):
        score_block(2 * t, None)
        score_block(2 * t + 1, None)
        return 0

    def score_odd_tail():
        score_block(i - 1, None)
        score_block(i, diag)
        return 0

    def score_even_tail():
        score_block(i, diag)
        return 0

    lax.fori_loop(0, i // 2, score_pair, 0)
    lax.cond(i % 2 == 1, score_odd_tail, score_even_tail)

    small = MXU_DTYPE
    group = 2 * SUBLANES

    assert (key_ref.shape[0] // group) <= 256

    def count_digits_ge(cand):
        def hits(j):
            hit = jnp.where(dig_ref[_rows(j), :] >= cand, jnp.ones((), small), jnp.zeros((), small))
            part = hit[0:group]
            for g in range(1, TK // group):
                part = part + hit[g * group:(g + 1) * group]
            return part

        acc = lax.fori_loop(0, (i + 1) // 2, lambda t, a: a + (hits(2 * t) + hits(2 * t + 1)),
                            jnp.zeros((group, TQ), small))
        acc = lax.cond((i + 1) % 2 == 1, lambda a: a + hits(i), lambda a: a, acc)
        return jnp.sum(acc.astype(F32), axis=0, keepdims=True)

    above = jnp.zeros((1, TQ), F32)
    prefix = None
    for shift in (24, 16, 8, 0):
        def fill(j, _, shift=shift, prefix=prefix):
            key = key_ref[_rows(j), :]
            if prefix is None:
                digit = (key >> shift) + 128
            else:
                digit = (key ^ (prefix << (shift + 8))) >> shift
                shares = lax.bitcast_convert_type(digit, jnp.uint32) < jnp.uint32(256)
                digit = jnp.where(shares, digit, -1)
            dig_ref[_rows(j), :] = digit.astype(F32).astype(small)
            return 0

        lax.fori_loop(0, i + 1, fill, 0)

        def search(_, st, above=above):
            lo, hi, n_hi = st
            mid = (lo + hi) * 0.5
            n_mid = count_digits_ge(mid.astype(small))
            ok = above + n_mid >= top_k
            return jnp.where(ok, mid, lo), jnp.where(ok, hi, mid), jnp.where(ok, n_hi, n_mid)

        start = (jnp.zeros((1, TQ), F32), jnp.full((1, TQ), 256.0, F32), jnp.zeros((1, TQ), F32))
        digit, _, n_hi = lax.fori_loop(0, 8, search, start)
        above = above + n_hi
        digit = digit.astype(jnp.int32)
        prefix = digit - 128 if prefix is None else (prefix << 8) | digit
    thr_key = prefix

    need = top_k - above
    kr = lax.broadcasted_iota(jnp.int32, (TK, TK), 0)
    kc = lax.broadcasted_iota(jnp.int32, (TK, TK), 1)
    upto = (kc <= kr).astype(MXU_DTYPE)

    def bias_block(j, mask, seen):
        key = key_ref[_rows(j), :]
        tie = key == thr_key
        rank = _dot(upto, jnp.where(tie, 1.0, 0.0).astype(MXU_DTYPE)) + seen
        sel = (key > thr_key) | (tie & (rank <= need))
        if mask is not None:
            sel = sel & mask
        bias_ref[_rows(j), :] = jnp.where(sel, 0.0, NEG_BIG)
        return rank[TK - 1:TK, :]

    def bias_pair(t, seen):
        return bias_block(2 * t + 1, None, bias_block(2 * t, None, seen))

    def bias_odd_tail(seen):
        bias_block(i, diag, bias_block(i - 1, None, seen))
        return 0

    def bias_even_tail(seen):
        bias_block(i, diag, seen)
        return 0

    seen = lax.fori_loop(0, i // 2, bias_pair, jnp.zeros((1, TQ), F32))
    lax.cond(i % 2 == 1, bias_odd_tail, bias_even_tail, seen)


def _dsa_kernel(q_ref, k_ref, vt_ref, qi_ref, ki_ref, misc_ref, o_ref, key_ref, dig_ref, bias_ref,
                za_ref, zb_ref, acc_ref, *, top_k):
    i = pl.program_id(1)

    def every_causal_key():
        def earlier_block(j, _):
            bias_ref[_rows(j), :] = jnp.zeros((TK, TQ), F32)
            return 0

        lax.fori_loop(0, i, earlier_block, 0)
        bias_ref[_rows(i), :] = jnp.where(_causal_tile_t(False), 0.0, NEG_BIG)
        return 0

    def searched_keys():
        _dsa_select(i, qi_ref, ki_ref, misc_ref, key_ref, dig_ref, bias_ref, top_k)
        return 0

    lax.cond((i + 1) * TQ <= top_k, every_causal_key, searched_keys)

    bias_at = lambda rows: bias_ref[rows, :]
    heads = []
    for h in range(DSA_HEADS):
        g = h // DSA_GROUP
        k_at = lambda rows, g=g: k_ref[0, rows, g * LANES:(g + 1) * LANES]
        vt_at = lambda rows, g=g: vt_ref[0, g * HEAD_DIM:(g + 1) * HEAD_DIM, rows]
        qh = _head_q(q_ref[0][:, (h // 2) * LANES:(h // 2 + 1) * LANES], h % 2)
        heads.append((qh, k_at, vt_at, bias_at))
    _flash_heads(heads, i, None, (za_ref, zb_ref), acc_ref, o_ref)


def _dsa(main3d, vt3d, misc3d, top_k):
    b, s, _ = main3d.shape
    nq = s // TQ
    assert top_k <= TK, "the first key block must hold at least top_k keys"
    return pl.pallas_call(
        functools.partial(_dsa_kernel, top_k=top_k),
        grid=(b, nq),
        in_specs=[
            pl.BlockSpec((1, TQ, 4 * LANES), lambda bi, i: (bi, i, COL_QC // (4 * LANES))),
            pl.BlockSpec((1, s, 2 * LANES), lambda bi, i: (bi, 0, COL_KC // (2 * LANES))),
            pl.BlockSpec((1, LANES, s), lambda bi, i: (bi, ROW_VC // LANES, 0)),
            pl.BlockSpec((1, TQ, 2 * LANES), lambda bi, i: (bi, i, COL_QI // (2 * LANES))),
            pl.BlockSpec((1, s, LANES), lambda bi, i: (bi, 0, COL_KI // LANES)),
            pl.BlockSpec((1, TQ, LANES), lambda bi, i: (bi, i, 0)),
        ],
        out_specs=pl.BlockSpec((1, TQ, BRANCH_WIDTH), lambda bi, i: (bi, i, 0)),
        out_shape=jax.ShapeDtypeStruct((b, s, BRANCH_WIDTH), MXU_DTYPE),
        scratch_shapes=[pltpu.VMEM((s, TQ), jnp.int32), pltpu.VMEM((s, TQ), MXU_DTYPE), pltpu.VMEM((s, TQ), F32),
                        pltpu.VMEM((DSA_HEADS, TK + SUBLANES, TQ), F32),
                        pltpu.VMEM((DSA_HEADS, TK + SUBLANES, TQ), F32),
                        pltpu.VMEM((DSA_HEADS, HEAD_DIM + SUM_ROWS, TQ), F32)],
        compiler_params=_cparams(("parallel", "arbitrary")),
        name="dsa",
    )(main3d, main3d, vt3d, main3d, main3d, misc3d)


def _merge_kernel(x_ref, a_ref, b_ref, c_ref, g_ref, wg_ref, wb_ref, wo_ref, o_ref):
    x = x_ref[...]
    d = x.shape[1]
    h = _rmsnorm(x, g_ref[...]).astype(MXU_DTYPE)
    y = None
    for n, br_ref in enumerate((a_ref, b_ref, c_ref)):
        gate = jax.nn.sigmoid(_dot(h, wg_ref[:, n * d:(n + 1) * d]))
        term = gate * _dot(br_ref[...], wb_ref[n])
        y = term if y is None else y + term
    o_ref[...] = x + _dot(y.astype(MXU_DTYPE), wo_ref[...])


def _merge(x2d, br_a, br_b, br_c, g, w_gate, w_branch, w_out, tm):
    n, d = x2d.shape
    bw = br_a.shape[1]
    row = lambda i: (i, 0)
    const = lambda i: (0, 0)
    return pl.pallas_call(
        _merge_kernel,
        grid=(n // tm,),
        in_specs=[
            pl.BlockSpec((tm, d), row),
            pl.BlockSpec((tm, bw), row),
            pl.BlockSpec((tm, bw), row),
            pl.BlockSpec((tm, bw), row),
            pl.BlockSpec((1, d), const),
            pl.BlockSpec((d, N_BRANCH * d), const),
            pl.BlockSpec((N_BRANCH, bw, d), lambda i: (0, 0, 0)),
            pl.BlockSpec((d, d), const),
        ],
        out_specs=pl.BlockSpec((tm, d), row),
        out_shape=jax.ShapeDtypeStruct((n, d), F32),
        compiler_params=_cparams(("parallel",)),
        name="merge",
    )(x2d, br_a, br_b, br_c, g, w_gate, w_branch, w_out)


def _mlp_kernel(x_ref, g_ref, wu_ref, wd_ref, gf_ref, o_ref, *, ff_chunk, final_norm):
    x = x_ref[...]
    h = _rmsnorm(x, g_ref[...]).astype(MXU_DTYPE)
    acc = x
    for c in range(wu_ref.shape[1] // ff_chunk):
        u = jnp.maximum(_dot(h, wu_ref[:, c * ff_chunk:(c + 1) * ff_chunk]), 0.0)
        acc = acc + _dot((u * u).astype(MXU_DTYPE), wd_ref[c * ff_chunk:(c + 1) * ff_chunk, :])
    if final_norm:
        acc = _rmsnorm(acc, gf_ref[...])
    o_ref[...] = acc


def _mlp(x2d, g, w_up, w_down, g_final, tm, final_norm):
    n, d = x2d.shape
    ff = w_up.shape[1]
    row = lambda i: (i, 0)
    const = lambda i: (0, 0)
    return pl.pallas_call(
        functools.partial(_mlp_kernel, ff_chunk=min(ff, 1024), final_norm=final_norm),
        grid=(n // tm,),
        in_specs=[
            pl.BlockSpec((tm, d), row),
            pl.BlockSpec((1, d), const),
            pl.BlockSpec((d, ff), const),
            pl.BlockSpec((ff, d), const),
            pl.BlockSpec((1, d), const),
        ],
        out_specs=pl.BlockSpec((tm, d), row),
        out_shape=jax.ShapeDtypeStruct((n, d), F32),
        compiler_params=_cparams(("parallel",)),
        name="mlp",
    )(x2d, g, w_up, w_down, g_final)


def _w_in_columns(d_model):
    sizes = (("qa", BRANCH_WIDTH), ("ka", BRANCH_WIDTH), ("va", BRANCH_WIDTH),
             ("qb", BRANCH_WIDTH), ("kb", BRANCH_WIDTH), ("vb", BRANCH_WIDTH), ("fl", FOX_HEADS),
             ("qc", DSA_HEADS * HEAD_DIM), ("kc", DSA_KV_HEADS * HEAD_DIM), ("vc", DSA_KV_HEADS * HEAD_DIM),
             ("qi", IDX_HEADS * IDX_DIM), ("ki", IDX_DIM), ("wi", IDX_HEADS), ("gates", N_BRANCH * d_model))
    cols, o = {}, 0
    for name, width in sizes:
        cols[name] = (o, o + width)
        o += width
    return cols, o


def _pack_kernel(w_ref, main_ref, vals_ref, gates_ref):
    cols, _ = _w_in_columns(gates_ref.shape[2] // N_BRANCH)

    def src(name, lo=0, hi=None):
        a, b = cols[name]
        return w_ref[0, :, a + lo:(b if hi is None else a + hi)]

    def put(ref, start, value):
        ref[0, :, start:start + value.shape[1]] = value.astype(ref.dtype)

    put(main_ref, COL_QA, src("qa"))
    put(main_ref, COL_KA, src("ka"))
    put(main_ref, COL_QB, src("qb"))
    put(main_ref, COL_KB, src("kb"))
    put(main_ref, COL_QC, src("qc"))
    for g in range(DSA_KV_HEADS):
        kg = src("kc", g * HEAD_DIM, (g + 1) * HEAD_DIM)
        put(main_ref, COL_KC + g * LANES, jnp.concatenate([kg, kg], axis=1))
    put(main_ref, COL_QI, src("qi"))
    ki = src("ki")
    put(main_ref, COL_KI, jnp.concatenate([ki, ki], axis=1))
    put(vals_ref, ROW_VA, src("va"))
    put(vals_ref, ROW_VB, src("vb"))
    put(vals_ref, ROW_VC, src("vc"))
    pad = jnp.zeros((w_ref.shape[1], LANES - FOX_HEADS - IDX_HEADS), F32)
    put(vals_ref, VT_ROWS, jnp.concatenate([src("fl"), src("wi"), pad], axis=1))
    put(gates_ref, 0, src("gates"))


def _pack_w_in(w_in):
    depth, d, total = w_in.shape
    _, end = _w_in_columns(d)
    assert end == total
    rows = 256
    blk = lambda width: pl.BlockSpec((1, rows, width), lambda l, r: (l, r, 0))
    widths = (MAIN_COLS, VALS_COLS, N_BRANCH * d)
    return pl.pallas_call(
        _pack_kernel,
        grid=(depth, d // rows),
        in_specs=[blk(total)],
        out_specs=[blk(wd) for wd in widths],
        out_shape=[jax.ShapeDtypeStruct((depth, d, wd), MXU_DTYPE) for wd in widths],
        compiler_params=_cparams(("parallel", "parallel")),
        name="pack",
    )(w_in)


def _rope_rows():
    lane = jnp.arange(LANES) % HEAD_DIM
    half = ROT_DIM // 2
    inv_freq = ROPE_THETA ** (-jnp.arange(0, ROT_DIM, 2, dtype=F32) / ROT_DIM)
    freq = jnp.where(lane < ROT_DIM, inv_freq[lane % half], 0.0).astype(F32)
    sg1 = jnp.where(lane < half, -1.0, 0.0).astype(F32)
    sg2 = jnp.where((lane >= half) & (lane < ROT_DIM), 1.0, 0.0).astype(F32)
    return freq[None, :], sg1[None, :], sg2[None, :]


def kernel(x, positions, g_mix, w_in, b_forget, w_branch, w_out, g_mlp, w_up, w_down, g_final):
    b, s, d = x.shape
    n = b * s
    depth = w_in.shape[0]
    top_k = min(TOPK_MAX, s // 4)
    tm = min(512, s)
    freq, sg1, sg2 = _rope_rows()
    pos2d = positions.reshape(n, 1)
    x2d = x.reshape(n, d)
    w_main_all, w_v_all, w_gate_all = _pack_w_in(w_in)
    for layer in range(depth):
        w_main, w_v, w_gate = w_main_all[layer], w_v_all[layer], w_gate_all[layer]
        bias_row = jnp.concatenate([b_forget[layer], jnp.zeros((LANES - FOX_HEADS,), F32)])[None, :]
        main, vt, misc = _proj(x2d, pos2d, g_mix[layer][None, :], freq, sg1, sg2, w_main, w_v, tm, b, s)
        main3d = main.reshape(b, s, MAIN_COLS)
        misc3d = misc.reshape(b, s, LANES)
        cum = _cumf(misc3d, bias_row)
        br_a = _sb(main3d, vt).reshape(n, BRANCH_WIDTH)
        br_b = _fox(main3d, vt, cum).reshape(n, BRANCH_WIDTH)
        br_c = _dsa(main3d, vt, misc3d, top_k).reshape(n, BRANCH_WIDTH)
        x2d = _merge(x2d, br_a, br_b, br_c, g_mix[layer][None, :], w_gate,
                     w_branch[layer].astype(MXU_DTYPE), w_out[layer].astype(MXU_DTYPE), tm)
        x2d = _mlp(x2d, g_mlp[layer][None, :], w_up[layer].astype(MXU_DTYPE),
                   w_down[layer].astype(MXU_DTYPE), g_final[None, :], tm, layer == depth - 1)
    return x2d.reshape(b, s, d)
```

```python
import functools

import jax
import jax.numpy as jnp
from jax import lax
from jax.experimental import pallas as pl
from jax.experimental.pallas import tpu as pltpu

F32 = jnp.float32
MXU_DTYPE = jnp.bfloat16

HEAD_DIM = 64
SB_HEADS = 8
FOX_HEADS = 8
DSA_HEADS = 8
DSA_KV_HEADS = 2
DSA_GROUP = DSA_HEADS // DSA_KV_HEADS
IDX_HEADS = 4
IDX_DIM = 64
N_BRANCH = 3
ROPE_THETA = 500000.0
ROT_DIM = HEAD_DIM // 4
TOPK_MAX = 256
NORM_EPS = 1e-6

LANES = 128
SUBLANES = 8
SUM_ROWS = 2 * SUBLANES
BRANCH_WIDTH = SB_HEADS * HEAD_DIM
ATTN_SCALE = HEAD_DIM ** -0.5
assert IDX_DIM == HEAD_DIM
IDX_W_SCALE = IDX_HEADS ** -0.5

COL_QA, COL_KA = 0, 512
COL_QB, COL_KB = 1024, 1536
COL_QC = 2048
COL_KC = 2560
COL_QI = 2816
COL_KI = 3072
MAIN_COLS = 3200
ROPE_START = COL_QC
ROW_VA, ROW_VB, ROW_VC = 0, 512, 1024
VT_ROWS = 1152
VALS_COLS = VT_ROWS + LANES
MISC_FL, MISC_WI = 0, 8

TQ = 256
TK = 256
HEADS_PER_STEP = 8
NEG_BIG = -1e30

VMEM_LIMIT = 56 * 1024 * 1024


def _cparams(sem):
    return pltpu.CompilerParams(dimension_semantics=sem, vmem_limit_bytes=VMEM_LIMIT)


def _rmsnorm(x, g):
    y = x * lax.rsqrt(jnp.mean(x * x, axis=-1, keepdims=True) + NORM_EPS)
    return y * g


def _dot(a, b):
    return jnp.dot(a, b, preferred_element_type=F32)


def _dot_nt(a, b):
    return lax.dot_general(a, b, (((1,), (1,)), ((), ())), preferred_element_type=F32)


def _split2(x):
    hi = x.astype(MXU_DTYPE)
    lo = (x - hi.astype(F32)).astype(MXU_DTYPE)
    return hi, lo


def _split3(x):
    hi = x.astype(MXU_DTYPE)
    r = x - hi.astype(F32)
    mid = r.astype(MXU_DTYPE)
    lo = (r - mid.astype(F32)).astype(MXU_DTYPE)
    return hi, mid, lo


def _log_sigmoid_parts(z):
    neg_abs = lax.bitcast_convert_type(lax.bitcast_convert_type(z, jnp.int32) | jnp.int32(-2 ** 31), F32)
    sp = jnp.log(1.0 + jnp.exp(neg_abs))
    log_sig = jnp.minimum(z, 0.0) - sp
    return log_sig, z - log_sig


def _proj_kernel(x_ref, pos_ref, g_ref, freq_ref, sg1_ref, sg2_ref, wm_ref, wv_ref,
                 main_ref, vt_ref, misc_ref):
    h = _rmsnorm(x_ref[...], g_ref[...]).astype(MXU_DTYPE)
    ang = pos_ref[...].astype(F32) * freq_ref[...]
    cos = jnp.cos(ang)
    sin = jnp.sin(ang)
    s_up = sin * sg1_ref[...]
    s_dn = sin * sg2_ref[...]
    half = ROT_DIM // 2
    chunk = 4 * LANES
    for c0 in range(0, VALS_COLS, 2 * LANES):
        pv = _dot(h, wv_ref[:, c0:c0 + 2 * LANES])
        for r0 in (c0, c0 + LANES):
            piece = pv[:, r0 - c0:r0 - c0 + LANES]
            if r0 < VT_ROWS:
                vt_ref[0, r0:r0 + LANES, :] = piece.T.astype(vt_ref.dtype)
            else:
                misc_ref[...] = piece
    for c0 in list(range(ROPE_START, MAIN_COLS, chunk)) + list(range(0, ROPE_START, chunk)):
        c1 = min(c0 + chunk, MAIN_COLS)
        p = _dot(h, wm_ref[:, c0:c1])
        if c0 >= ROPE_START:
            for t in range((c1 - c0) // LANES):
                pt = p[:, t * LANES:(t + 1) * LANES]
                pt = (pt * cos + pltpu.roll(pt, LANES - half, 1) * s_up
                      + pltpu.roll(pt, half, 1) * s_dn)
                main_ref[:, c0 + t * LANES:c0 + (t + 1) * LANES] = pt.astype(main_ref.dtype)
        else:
            main_ref[:, c0:c1] = p.astype(main_ref.dtype)


def _proj(x2d, pos2d, g, freq, sg1, sg2, w_main, w_v, tm, batch, seq):
    n, d = x2d.shape
    nsb = seq // tm
    const = lambda i: (0, 0)
    return pl.pallas_call(
        _proj_kernel,
        grid=(n // tm,),
        in_specs=[
            pl.BlockSpec((tm, d), lambda i: (i, 0)),
            pl.BlockSpec((tm, 1), lambda i: (i, 0)),
            pl.BlockSpec((1, d), const),
            pl.BlockSpec((1, LANES), const),
            pl.BlockSpec((1, LANES), const),
            pl.BlockSpec((1, LANES), const),
            pl.BlockSpec((d, MAIN_COLS), const),
            pl.BlockSpec((d, VALS_COLS), const),
        ],
        out_specs=[
            pl.BlockSpec((tm, MAIN_COLS), lambda i: (i, 0)),
            pl.BlockSpec((1, VT_ROWS, tm), lambda i: (i // nsb, 0, i % nsb)),
            pl.BlockSpec((tm, LANES), lambda i: (i, 0)),
        ],
        out_shape=[
            jax.ShapeDtypeStruct((n, MAIN_COLS), MXU_DTYPE),
            jax.ShapeDtypeStruct((batch, VT_ROWS, seq), MXU_DTYPE),
            jax.ShapeDtypeStruct((n, LANES), F32),
        ],
        compiler_params=_cparams(("parallel",)),
        name="proj",
    )(x2d, pos2d, g, freq, sg1, sg2, w_main, w_v)


def _cumf_kernel(misc_ref, bias_ref, c_ref):
    s = misc_ref.shape[1]
    r = lax.broadcasted_iota(jnp.int32, (TK, TK), 0)
    c = lax.broadcasted_iota(jnp.int32, (TK, TK), 1)
    incl = (c <= r).astype(MXU_DTYPE)
    carry = jnp.zeros((1, LANES), F32)
    for b in range(s // TK):
        logit = misc_ref[0, b * TK:(b + 1) * TK, :] + bias_ref[...]
        logf, _ = _log_sigmoid_parts(logit)
        hi, mid, lo = _split3(logf)
        cs = (_dot(incl, lo) + _dot(incl, mid)) + _dot(incl, hi) + carry
        c_ref[0, b * TK:(b + 1) * TK, :] = cs
        carry = cs[TK - 1:TK, :]


def _cumf(misc3d, bias_row):
    b, s, _ = misc3d.shape
    return pl.pallas_call(
        _cumf_kernel,
        grid=(b,),
        in_specs=[
            pl.BlockSpec((1, s, LANES), lambda i: (i, 0, 0)),
            pl.BlockSpec((1, LANES), lambda i: (0, 0)),
        ],
        out_specs=pl.BlockSpec((1, s, LANES), lambda i: (i, 0, 0)),
        out_shape=jax.ShapeDtypeStruct((b, s, LANES), F32),
        compiler_params=_cparams(("parallel",)),
        name="cumf",
    )(misc3d, bias_row)


def _head_q(q_tile, half):
    lane = lax.broadcasted_iota(jnp.int32, q_tile.shape, 1)
    keep = (lane // HEAD_DIM) == half
    return jnp.where(keep, q_tile.astype(F32) * ATTN_SCALE, 0.0).astype(MXU_DTYPE)


def _causal_tile_t(strict):
    key = lax.broadcasted_iota(jnp.int32, (TK, TQ), 0)
    qry = lax.broadcasted_iota(jnp.int32, (TK, TQ), 1)
    return (key < qry) if strict else (key <= qry)


def _rows(j):
    return pl.ds(pl.multiple_of(j * TK, TK), TK)


def _lookahead_loop(n_last, bufs, issue, step, carry, finish, primed=False, fused=None):
    buf_a, buf_b = bufs

    def ahead_and_step(n, buf, other, carry):
        if fused is not None:
            return fused(n, buf, other, carry)
        issue(n + 1, other)
        return step(n, buf, carry, False)

    def pair(t, carry):
        n = 2 * t
        carry = ahead_and_step(n, buf_a, buf_b, carry)
        return ahead_and_step(n + 1, buf_b, buf_a, carry)

    def odd_tail(carry):
        carry = ahead_and_step(n_last - 1, buf_a, buf_b, carry)
        step(n_last, buf_b, carry, True)
        finish()
        return 0

    def even_tail(carry):
        step(n_last, buf_a, carry, True)
        finish()
        return 0

    if not primed:
        issue(0, buf_a)
    carry = lax.fori_loop(0, n_last // 2, pair, carry)
    lax.cond(n_last % 2 == 1, odd_tail, even_tail, carry)


def _flash_heads(heads, n_full, diag_mask, z_bufs, acc_ref, o_ref):
    half_k, half_q = TK // 2, TQ // 2

    def add_bias(s, bias):
        if isinstance(bias, (list, tuple)):
            return jnp.concatenate([s[:, t * LANES:(t + 1) * LANES] + bias[t] for t in range(len(bias))],
                                   axis=1)
        return s + bias

    ones = jnp.ones((SUM_ROWS, TK), MXU_DTYPE)

    def issue_head(h, j, buf):
        rows = _rows(j)
        qh, k_at, _, bias_at = heads[h]
        s = add_bias(_dot_nt(k_at(rows), qh), bias_at(rows))
        buf[h, :TK, :] = s
        buf[h, TK:, :] = jnp.max(s.reshape(TK // SUBLANES, SUBLANES, TQ), axis=0)

    def issue(j, buf):
        for h in range(len(heads)):
            issue_head(h, j, buf)

    def step_and_issue(j, buf, other, maxes):
        rows = _rows(j)
        out = []
        for h, ((_, _, vt_at, _), m) in enumerate(zip(heads, maxes)):
            issue_head(h, j + 1, other)
            m_new = jnp.maximum(m, jnp.max(buf[h, TK:, :], axis=0, keepdims=True))
            p = jnp.exp(buf[h, :TK, :] - m_new).astype(MXU_DTYPE)
            vt1 = jnp.concatenate([vt_at(rows), ones], axis=0)
            acc_ref[h] = jnp.exp(m - m_new) * acc_ref[h] + _dot(vt1, p)
            out.append(m_new)
        return tuple(out)

    def update(tiles, rows, maxes, lanes, tile_maxes=None):
        stats = []
        for h, (s, m) in enumerate(zip(tiles, maxes)):
            tile_max = jnp.max(s if tile_maxes is None else tile_maxes[h], axis=0, keepdims=True)
            m_new = jnp.maximum(m, tile_max)
            stats.append((m_new, jnp.exp(m - m_new), jnp.exp(s - m_new).astype(MXU_DTYPE)))
        for h, ((_, alpha, p), (_, _, vt_at, _)) in enumerate(zip(stats, heads)):
            vt1 = jnp.concatenate([vt_at(rows), ones[:, :p.shape[0]]], axis=0)
            acc_ref[h, :, lanes] = alpha * acc_ref[h, :, lanes] + _dot(vt1, p)
        return tuple(m_new for m_new, _, _ in stats)

    def step(j, buf, maxes):
        n = len(heads)
        return update([buf[h, :TK, :] for h in range(n)], _rows(j), maxes, slice(None),
                      [buf[h, TK:, :] for h in range(n)])

    def diagonal_step(j, buf, maxes):
        start = pl.multiple_of(j * TK, TK)
        early, late = pl.ds(start, half_k), pl.ds(pl.multiple_of(start + half_k, half_k), half_k)

        def masked(s, mask):
            return s if diag_mask is None else jnp.where(mask, s, NEG_BIG)

        tiles = [masked(buf[h, :half_k, :], None if diag_mask is None else diag_mask[:half_k, :])
                 for h in range(len(heads))]
        maxes = update(tiles, early, maxes, slice(None))
        tiles = [masked(buf[h, half_k:TK, half_q:], None if diag_mask is None else diag_mask[half_k:, half_q:])
                 for h in range(len(heads))]
        update(tiles, late, tuple(m[:, half_q:] for m in maxes), slice(half_q, TQ))
        return maxes

    def finish():
        _store_heads_t(o_ref, [acc_ref[h, :HEAD_DIM, :] / acc_ref[h, HEAD_DIM:HEAD_DIM + 1, :]
                               for h in range(len(heads))])

    acc_ref[...] = jnp.zeros(acc_ref.shape, F32)
    init = tuple(jnp.full((1, TQ), NEG_BIG, F32) for _ in heads)
    _lookahead_loop(n_full, z_bufs, issue,
                    lambda j, buf, maxes, last: (diagonal_step if last else step)(j, buf, maxes), init, finish,
                    fused=step_and_issue)


def _store_heads_t(o_ref, outs_t):
    for t in range(len(outs_t) // 2):
        pair_t = jnp.concatenate([outs_t[2 * t], outs_t[2 * t + 1]], axis=0)
        o_ref[0, :, t * LANES:(t + 1) * LANES] = pair_t.T.astype(o_ref.dtype)


def _sb_kernel(q_ref, k_ref, vt_ref, o_ref, za_ref, zb_ref, acc_ref):
    i = pl.program_id(2)
    r = lax.broadcasted_iota(jnp.int32, (TK, TK), 0)
    c = lax.broadcasted_iota(jnp.int32, (TK, TK), 1)
    after = (c > r).astype(MXU_DTYPE)
    strict = _causal_tile_t(True)
    n_heads = HEADS_PER_STEP
    qhs = [_head_q(q_ref[0][:, (h // 2) * LANES:(h // 2 + 1) * LANES], h % 2) for h in range(n_heads)]

    def issue(j, buf):
        for h in range(n_heads):
            buf[h] = _dot_nt(k_ref[0, _rows(j), (h // 2) * LANES:(h // 2 + 1) * LANES], qhs[h])

    def update(tiles, rows, tails, lanes, mask):
        n_rows = tiles[0].shape[0]
        parts = []
        for z in tiles:
            log_beta, neg_l1m = _log_sigmoid_parts(z)
            if mask is not None:
                neg_l1m = jnp.where(mask, neg_l1m, 0.0)
            parts.append((log_beta, neg_l1m[0:1, :], neg_l1m.astype(MXU_DTYPE)))
        after_rows = after[:n_rows, :n_rows]
        laters = [_dot(after_rows, hl) for (_, _, hl) in parts]
        ws = []
        for (log_beta, _, _), later in zip(parts, laters):
            w = jnp.exp(log_beta - later)
            if mask is not None:
                w = jnp.where(mask, w, 0.0)
            ws.append(w.astype(MXU_DTYPE))
        out = []
        for h, (w, (_, first, _), later, tail) in enumerate(zip(ws, parts, laters, tails)):
            vtj = vt_ref[0, h * HEAD_DIM:(h + 1) * HEAD_DIM, rows]
            acc_ref[h, :, lanes] = acc_ref[h, :, lanes] + jnp.exp(-tail) * _dot(vtj, w)
            out.append(tail + (first + later[0:1, :]))
        return tuple(out)

    def step(j, buf, tails):
        return update([buf[h] for h in range(n_heads)], _rows(j), tails, slice(None), None)

    def diagonal_step(j, buf, tails):
        half_k, half_q = TK // 2, TQ // 2
        start = pl.multiple_of(j * TK, TK)
        early, late = pl.ds(start, half_k), pl.ds(pl.multiple_of(start + half_k, half_k), half_k)
        late_tails = update([buf[h, half_k:, half_q:] for h in range(n_heads)], late,
                            tuple(t[:, half_q:] for t in tails), slice(half_q, TQ), strict[half_k:, half_q:])
        tails = tuple(jnp.concatenate([t[:, :half_q], lt], axis=1) for t, lt in zip(tails, late_tails))
        return update([buf[h, :half_k, :] for h in range(n_heads)], early, tails, slice(None), strict[:half_k, :])

    acc_ref[...] = jnp.zeros(acc_ref.shape, F32)
    carry = tuple(jnp.zeros((1, TQ), F32) for _ in range(n_heads))
    issue(i, za_ref)

    def finish():
        _store_heads_t(o_ref, [acc_ref[h] for h in range(n_heads)])

    def with_earlier_blocks(cr):
        issue(i - 1, zb_ref)
        cr = diagonal_step(i, za_ref, cr)
        _lookahead_loop(i - 1, (zb_ref, za_ref), lambda n, buf: issue(i - 1 - n, buf),
                        lambda n, buf, c, last: step(i - 1 - n, buf, c), cr, finish, primed=True)
        return 0

    def diagonal_only(cr):
        diagonal_step(i, za_ref, cr)
        finish()
        return 0

    lax.cond(i > 0, with_earlier_blocks, diagonal_only, carry)


def _sb(main3d, vt3d):
    b, s, _ = main3d.shape
    nq = s // TQ
    w = HEADS_PER_STEP * HEAD_DIM
    qb, kb, vb = COL_QA // w, COL_KA // w, ROW_VA // w
    return pl.pallas_call(
        _sb_kernel,
        grid=(b, SB_HEADS // HEADS_PER_STEP, nq),
        in_specs=[
            pl.BlockSpec((1, TQ, w), lambda bi, hg, i: (bi, i, qb + hg)),
            pl.BlockSpec((1, s, w), lambda bi, hg, i: (bi, 0, kb + hg)),
            pl.BlockSpec((1, w, s), lambda bi, hg, i: (bi, vb + hg, 0)),
        ],
        out_specs=pl.BlockSpec((1, TQ, w), lambda bi, hg, i: (bi, i, hg)),
        out_shape=jax.ShapeDtypeStruct((b, s, BRANCH_WIDTH), MXU_DTYPE),
        scratch_shapes=[pltpu.VMEM((HEADS_PER_STEP, TK, TQ), F32), pltpu.VMEM((HEADS_PER_STEP, TK, TQ), F32),
                        pltpu.VMEM((HEADS_PER_STEP, HEAD_DIM, TQ), F32)],
        compiler_params=_cparams(("parallel", "parallel", "arbitrary")),
        name="sb",
    )(main3d, main3d, vt3d)


def _fox_kernel(q_ref, k_ref, vt_ref, c_ref, o_ref, cb_ref, za_ref, zb_ref, acc_ref):
    hg = pl.program_id(1)
    i = pl.program_id(2)
    n_heads = HEADS_PER_STEP

    @pl.when(i == 0)
    def _():
        c = c_ref[0]
        lane = lax.broadcasted_iota(jnp.int32, c.shape, 1)
        for h in range(n_heads):
            if n_heads == FOX_HEADS:
                col = c[:, h:h + 1]
            else:
                col = jnp.sum(jnp.where(lane == hg * n_heads + h, c, 0.0), axis=1, keepdims=True)
            cb_ref[h] = jnp.broadcast_to(-col, c.shape)

    heads = []
    for h in range(n_heads):
        t = h // 2
        qh = _head_q(q_ref[0][:, t * LANES:(t + 1) * LANES], h % 2)
        k_at = lambda rows, t=t: k_ref[0, rows, t * LANES:(t + 1) * LANES]
        vt_at = lambda rows, h=h: vt_ref[0, h * HEAD_DIM:(h + 1) * HEAD_DIM, rows]
        bias_at = lambda rows, h=h: [cb_ref[h, rows, :]] * (TQ // LANES)
        heads.append((qh, k_at, vt_at, bias_at))
    _flash_heads(heads, i, _causal_tile_t(False), (za_ref, zb_ref), acc_ref, o_ref)


def _fox(main3d, vt3d, c3d):
    b, s, _ = main3d.shape
    nq = s // TQ
    w = HEADS_PER_STEP * HEAD_DIM
    qb, kb, vb = COL_QB // w, COL_KB // w, ROW_VB // w
    return pl.pallas_call(
        _fox_kernel,
        grid=(b, FOX_HEADS // HEADS_PER_STEP, nq),
        in_specs=[
            pl.BlockSpec((1, TQ, w), lambda bi, hg, i: (bi, i, qb + hg)),
            pl.BlockSpec((1, s, w), lambda bi, hg, i: (bi, 0, kb + hg)),
            pl.BlockSpec((1, w, s), lambda bi, hg, i: (bi, vb + hg, 0)),
            pl.BlockSpec((1, s, LANES), lambda bi, hg, i: (bi, 0, 0)),
        ],
        out_specs=pl.BlockSpec((1, TQ, w), lambda bi, hg, i: (bi, i, hg)),
        out_shape=jax.ShapeDtypeStruct((b, s, BRANCH_WIDTH), MXU_DTYPE),
        scratch_shapes=[pltpu.VMEM((HEADS_PER_STEP, s, LANES), F32),
                        pltpu.VMEM((HEADS_PER_STEP, TK + SUBLANES, TQ), F32),
                        pltpu.VMEM((HEADS_PER_STEP, TK + SUBLANES, TQ), F32),
                        pltpu.VMEM((HEADS_PER_STEP, HEAD_DIM + SUM_ROWS, TQ), F32)],
        compiler_params=_cparams(("parallel", "parallel", "arbitrary")),
        name="fox",
    )(main3d, main3d, vt3d, c3d)


def _dsa_select(i, qi_ref, ki_ref, misc_ref, key_ref, dig_ref, bias_ref, top_k):
    diag = _causal_tile_t(False)

    wi_t = misc_ref[0].T[MISC_WI:MISC_WI + IDX_HEADS, :] * IDX_W_SCALE
    qih = [_head_q(qi_ref[0][:, (h // 2) * LANES:(h // 2 + 1) * LANES], h % 2) for h in range(IDX_HEADS)]

    def score_block(j, mask):
        kij = ki_ref[0, _rows(j), :]
        sc = None
        for h in range(IDX_HEADS):
            term = wi_t[h:h + 1, :] * jnp.maximum(_dot_nt(kij, qih[h]), 0.0)
            sc = term if sc is None else sc + term
        if mask is not None:
            sc = jnp.where(mask, sc, -jnp.inf)
        bits = lax.bitcast_convert_type(jnp.where(sc == 0.0, 0.0, sc), jnp.int32)
        key_ref[_rows(j), :] = bits ^ ((bits >> 31) & jnp.int32(0x7FFFFFFF))

    def score_pair(t, _):
        score_block(2 * t, None)
        score_block(2 * t + 1, None)
        return 0

    def score_odd_tail():
        score_block(i - 1, None)
        score_block(i, diag)
        return 0

    def score_even_tail():
        score_block(i, diag)
        return 0

    lax.fori_loop(0, i // 2, score_pair, 0)
    lax.cond(i % 2 == 1, score_odd_tail, score_even_tail)

    small = MXU_DTYPE
    group = 2 * SUBLANES

    assert (key_ref.shape[0] // group) <= 256

    def count_digits_ge(cand):
        def hits(j):
            hit = jnp.where(dig_ref[_rows(j), :] >= cand, jnp.ones((), small), jnp.zeros((), small))
            part = hit[0:group]
            for g in range(1, TK // group):
                part = part + hit[g * group:(g + 1) * group]
            return part

        acc = lax.fori_loop(0, (i + 1) // 2, lambda t, a: a + (hits(2 * t) + hits(2 * t + 1)),
                            jnp.zeros((group, TQ), small))
        acc = lax.cond((i + 1) % 2 == 1, lambda a: a + hits(i), lambda a: a, acc)
        return jnp.sum(acc.astype(F32), axis=0, keepdims=True)

    above = jnp.zeros((1, TQ), F32)
    prefix = None
    for shift in (24, 16, 8, 0):
        def fill(j, _, shift=shift, prefix=prefix):
            key = key_ref[_rows(j), :]
            if prefix is None:
                digit = (key >> shift) + 128
            else:
                digit = (key ^ (prefix << (shift + 8))) >> shift
                shares = lax.bitcast_convert_type(digit, jnp.uint32) < jnp.uint32(256)
                digit = jnp.where(shares, digit, -1)
            dig_ref[_rows(j), :] = digit.astype(F32).astype(small)
            return 0

        lax.fori_loop(0, i + 1, fill, 0)

        def search(_, st, above=above):
            lo, hi, n_hi = st
            mid = (lo + hi) * 0.5
            n_mid = count_digits_ge(mid.astype(small))
            ok = above + n_mid >= top_k
            return jnp.where(ok, mid, lo), jnp.where(ok, hi, mid), jnp.where(ok, n_hi, n_mid)

        start = (jnp.zeros((1, TQ), F32), jnp.full((1, TQ), 256.0, F32), jnp.zeros((1, TQ), F32))
        digit, _, n_hi = lax.fori_loop(0, 8, search, start)
        above = above + n_hi
        digit = digit.astype(jnp.int32)
        prefix = digit - 128 if prefix is None else (prefix << 8) | digit
    thr_key = prefix

    need = top_k - above
    kr = lax.broadcasted_iota(jnp.int32, (TK, TK), 0)
    kc = lax.broadcasted_iota(jnp.int32, (TK, TK), 1)
    upto = (kc <= kr).astype(MXU_DTYPE)

    def bias_block(j, mask, seen):
        key = key_ref[_rows(j), :]
        tie = key == thr_key
        rank = _dot(upto, jnp.where(tie, 1.0, 0.0).astype(MXU_DTYPE)) + seen
        sel = (key > thr_key) | (tie & (rank <= need))
        if mask is not None:
            sel = sel & mask
        bias_ref[_rows(j), :] = jnp.where(sel, 0.0, NEG_BIG)
        return rank[TK - 1:TK, :]

    def bias_pair(t, seen):
        return bias_block(2 * t + 1, None, bias_block(2 * t, None, seen))

    def bias_odd_tail(seen):
        bias_block(i, diag, bias_block(i - 1, None, seen))
        return 0

    def bias_even_tail(seen):
        bias_block(i, diag, seen)
        return 0

    seen = lax.fori_loop(0, i // 2, bias_pair, jnp.zeros((1, TQ), F32))
    lax.cond(i % 2 == 1, bias_odd_tail, bias_even_tail, seen)


def _dsa_kernel(q_ref, k_ref, vt_ref, qi_ref, ki_ref, misc_ref, o_ref, key_ref, dig_ref, bias_ref,
                za_ref, zb_ref, acc_ref, *, top_k):
    i = pl.program_id(1)

    def every_causal_key():
        def earlier_block(j, _):
            bias_ref[_rows(j), :] = jnp.zeros((TK, TQ), F32)
            return 0

        lax.fori_loop(0, i, earlier_block, 0)
        bias_ref[_rows(i), :] = jnp.where(_causal_tile_t(False), 0.0, NEG_BIG)
        return 0

    def searched_keys():
        _dsa_select(i, qi_ref, ki_ref, misc_ref, key_ref, dig_ref, bias_ref, top_k)
        return 0

    lax.cond((i + 1) * TQ <= top_k, every_causal_key, searched_keys)

    bias_at = lambda rows: bias_ref[rows, :]
    heads = []
    for h in range(DSA_HEADS):
        g = h // DSA_GROUP
        k_at = lambda rows, g=g: k_ref[0, rows, g * LANES:(g + 1) * LANES]
        vt_at = lambda rows, g=g: vt_ref[0, g * HEAD_DIM:(g + 1) * HEAD_DIM, rows]
        qh = _head_q(q_ref[0][:, (h // 2) * LANES:(h // 2 + 1) * LANES], h % 2)
        heads.append((qh, k_at, vt_at, bias_at))
    _flash_heads(heads, i, None, (za_ref, zb_ref), acc_ref, o_ref)


def _dsa(main3d, vt3d, misc3d, top_k):
    b, s, _ = main3d.shape
    nq = s // TQ
    assert top_k <= TK, "the first key block must hold at least top_k keys"
    return pl.pallas_call(
        functools.partial(_dsa_kernel, top_k=top_k),
        grid=(b, nq),
        in_specs=[
            pl.BlockSpec((1, TQ, 4 * LANES), lambda bi, i: (bi, i, COL_QC // (4 * LANES))),
            pl.BlockSpec((1, s, 2 * LANES), lambda bi, i: (bi, 0, COL_KC // (2 * LANES))),
            pl.BlockSpec((1, LANES, s), lambda bi, i: (bi, ROW_VC // LANES, 0)),
            pl.BlockSpec((1, TQ, 2 * LANES), lambda bi, i: (bi, i, COL_QI // (2 * LANES))),
            pl.BlockSpec((1, s, LANES), lambda bi, i: (bi, 0, COL_KI // LANES)),
            pl.BlockSpec((1, TQ, LANES), lambda bi, i: (bi, i, 0)),
        ],
        out_specs=pl.BlockSpec((1, TQ, BRANCH_WIDTH), lambda bi, i: (bi, i, 0)),
        out_shape=jax.ShapeDtypeStruct((b, s, BRANCH_WIDTH), MXU_DTYPE),
        scratch_shapes=[pltpu.VMEM((s, TQ), jnp.int32), pltpu.VMEM((s, TQ), MXU_DTYPE), pltpu.VMEM((s, TQ), F32),
                        pltpu.VMEM((DSA_HEADS, TK + SUBLANES, TQ), F32),
                        pltpu.VMEM((DSA_HEADS, TK + SUBLANES, TQ), F32),
                        pltpu.VMEM((DSA_HEADS, HEAD_DIM + SUM_ROWS, TQ), F32)],
        compiler_params=_cparams(("parallel", "arbitrary")),
        name="dsa",
    )(main3d, main3d, vt3d, main3d, main3d, misc3d)


def _merge_kernel(x_ref, a_ref, b_ref, c_ref, g_ref, wg_ref, wb_ref, wo_ref, o_ref):
    x = x_ref[...]
    d = x.shape[1]
    h = _rmsnorm(x, g_ref[...]).astype(MXU_DTYPE)
    y = None
    for n, br_ref in enumerate((a_ref, b_ref, c_ref)):
        gate = jax.nn.sigmoid(_dot(h, wg_ref[:, n * d:(n + 1) * d]))
        term = gate * _dot(br_ref[...], wb_ref[n])
        y = term if y is None else y + term
    o_ref[...] = x + _dot(y.astype(MXU_DTYPE), wo_ref[...])


def _merge(x2d, br_a, br_b, br_c, g, w_gate, w_branch, w_out, tm):
    n, d = x2d.shape
    bw = br_a.shape[1]
    row = lambda i: (i, 0)
    const = lambda i: (0, 0)
    return pl.pallas_call(
        _merge_kernel,
        grid=(n // tm,),
        in_specs=[
            pl.BlockSpec((tm, d), row),
            pl.BlockSpec((tm, bw), row),
            pl.BlockSpec((tm, bw), row),
            pl.BlockSpec((tm, bw), row),
            pl.BlockSpec((1, d), const),
            pl.BlockSpec((d, N_BRANCH * d), const),
            pl.BlockSpec((N_BRANCH, bw, d), lambda i: (0, 0, 0)),
            pl.BlockSpec((d, d), const),
        ],
        out_specs=pl.BlockSpec((tm, d), row),
        out_shape=jax.ShapeDtypeStruct((n, d), F32),
        compiler_params=_cparams(("parallel",)),
        name="merge",
    )(x2d, br_a, br_b, br_c, g, w_gate, w_branch, w_out)


def _mlp_kernel(x_ref, g_ref, wu_ref, wd_ref, gf_ref, o_ref, *, ff_chunk, final_norm):
    x = x_ref[...]
    h = _rmsnorm(x, g_ref[...]).astype(MXU_DTYPE)
    acc = x
    for c in range(wu_ref.shape[1] // ff_chunk):
        u = jnp.maximum(_dot(h, wu_ref[:, c * ff_chunk:(c + 1) * ff_chunk]), 0.0)
        acc = acc + _dot((u * u).astype(MXU_DTYPE), wd_ref[c * ff_chunk:(c + 1) * ff_chunk, :])
    if final_norm:
        acc = _rmsnorm(acc, gf_ref[...])
    o_ref[...] = acc


def _mlp(x2d, g, w_up, w_down, g_final, tm, final_norm):
    n, d = x2d.shape
    ff = w_up.shape[1]
    row = lambda i: (i, 0)
    const = lambda i: (0, 0)
    return pl.pallas_call(
        functools.partial(_mlp_kernel, ff_chunk=min(ff, 1024), final_norm=final_norm),
        grid=(n // tm,),
        in_specs=[
            pl.BlockSpec((tm, d), row),
            pl.BlockSpec((1, d), const),
            pl.BlockSpec((d, ff), const),
            pl.BlockSpec((ff, d), const),
            pl.BlockSpec((1, d), const),
        ],
        out_specs=pl.BlockSpec((tm, d), row),
        out_shape=jax.ShapeDtypeStruct((n, d), F32),
        compiler_params=_cparams(("parallel",)),
        name="mlp",
    )(x2d, g, w_up, w_down, g_final)


def _w_in_columns(d_model):
    sizes = (("qa", BRANCH_WIDTH), ("ka", BRANCH_WIDTH), ("va", BRANCH_WIDTH),
             ("qb", BRANCH_WIDTH), ("kb", BRANCH_WIDTH), ("vb", BRANCH_WIDTH), ("fl", FOX_HEADS),
             ("qc", DSA_HEADS * HEAD_DIM), ("kc", DSA_KV_HEADS * HEAD_DIM), ("vc", DSA_KV_HEADS * HEAD_DIM),
             ("qi", IDX_HEADS * IDX_DIM), ("ki", IDX_DIM), ("wi", IDX_HEADS), ("gates", N_BRANCH * d_model))
    cols, o = {}, 0
    for name, width in sizes:
        cols[name] = (o, o + width)
        o += width
    return cols, o


def _pack_kernel(w_ref, main_ref, vals_ref, gates_ref):
    cols, _ = _w_in_columns(gates_ref.shape[2] // N_BRANCH)

    def src(name, lo=0, hi=None):
        a, b = cols[name]
        return w_ref[0, :, a + lo:(b if hi is None else a + hi)]

    def put(ref, start, value):
        ref[0, :, start:start + value.shape[1]] = value.astype(ref.dtype)

    put(main_ref, COL_QA, src("qa"))
    put(main_ref, COL_KA, src("ka"))
    put(main_ref, COL_QB, src("qb"))
    put(main_ref, COL_KB, src("kb"))
    put(main_ref, COL_QC, src("qc"))
    for g in range(DSA_KV_HEADS):
        kg = src("kc", g * HEAD_DIM, (g + 1) * HEAD_DIM)
        put(main_ref, COL_KC + g * LANES, jnp.concatenate([kg, kg], axis=1))
    put(main_ref, COL_QI, src("qi"))
    ki = src("ki")
    put(main_ref, COL_KI, jnp.concatenate([ki, ki], axis=1))
    put(vals_ref, ROW_VA, src("va"))
    put(vals_ref, ROW_VB, src("vb"))
    put(vals_ref, ROW_VC, src("vc"))
    pad = jnp.zeros((w_ref.shape[1], LANES - FOX_HEADS - IDX_HEADS), F32)
    put(vals_ref, VT_ROWS, jnp.concatenate([src("fl"), src("wi"), pad], axis=1))
    put(gates_ref, 0, src("gates"))


def _pack_w_in(w_in):
    depth, d, total = w_in.shape
    _, end = _w_in_columns(d)
    assert end == total
    rows = 256
    blk = lambda width: pl.BlockSpec((1, rows, width), lambda l, r: (l, r, 0))
    widths = (MAIN_COLS, VALS_COLS, N_BRANCH * d)
    return pl.pallas_call(
        _pack_kernel,
        grid=(depth, d // rows),
        in_specs=[blk(total)],
        out_specs=[blk(wd) for wd in widths],
        out_shape=[jax.ShapeDtypeStruct((depth, d, wd), MXU_DTYPE) for wd in widths],
        compiler_params=_cparams(("parallel", "parallel")),
        name="pack",
    )(w_in)


def _rope_rows():
    lane = jnp.arange(LANES) % HEAD_DIM
    half = ROT_DIM // 2
    inv_freq = ROPE_THETA ** (-jnp.arange(0, ROT_DIM, 2, dtype=F32) / ROT_DIM)
    freq = jnp.where(lane < ROT_DIM, inv_freq[lane % half], 0.0).astype(F32)
    sg1 = jnp.where(lane < half, -1.0, 0.0).astype(F32)
    sg2 = jnp.where((lane >= half) & (lane < ROT_DIM), 1.0, 0.0).astype(F32)
    return freq[None, :], sg1[None, :], sg2[None, :]


def kernel(x, positions, g_mix, w_in, b_forget, w_branch, w_out, g_mlp, w_up, w_down, g_final):
    b, s, d = x.shape
    n = b * s
    depth = w_in.shape[0]
    top_k = min(TOPK_MAX, s // 4)
    tm = min(512, s)
    freq, sg1, sg2 = _rope_rows()
    pos2d = positions.reshape(n, 1)
    x2d = x.reshape(n, d)
    w_main_all, w_v_all, w_gate_all = _pack_w_in(w_in)
    for layer in range(depth):
        w_main, w_v, w_gate = w_main_all[layer], w_v_all[layer], w_gate_all[layer]
        bias_row = jnp.concatenate([b_forget[layer], jnp.zeros((LANES - FOX_HEADS,), F32)])[None, :]
        main, vt, misc = _proj(x2d, pos2d, g_mix[layer][None, :], freq, sg1, sg2, w_main, w_v, tm, b, s)
        main3d = main.reshape(b, s, MAIN_COLS)
        misc3d = misc.reshape(b, s, LANES)
        cum = _cumf(misc3d, bias_row)
        br_a = _sb(main3d, vt).reshape(n, BRANCH_WIDTH)
        br_b = _fox(main3d, vt, cum).reshape(n, BRANCH_WIDTH)
        br_c = _dsa(main3d, vt, misc3d, top_k).reshape(n, BRANCH_WIDTH)
        x2d = _merge(x2d, br_a, br_b, br_c, g_mix[layer][None, :], w_gate,
                     w_branch[layer].astype(MXU_DTYPE), w_out[layer].astype(MXU_DTYPE), tm)
        x2d = _mlp(x2d, g_mlp[layer][None, :], w_up[layer].astype(MXU_DTYPE),
                   w_down[layer].astype(MXU_DTYPE), g_final[None, :], tm, layer == depth - 1)
    return x2d.reshape(b, s, d)
```

```python
import functools

import jax
import jax.numpy as jnp
from jax import lax
from jax.experimental import pallas as pl
from jax.experimental.pallas import tpu as pltpu

F32 = jnp.float32
MXU_DTYPE = jnp.bfloat16

HEAD_DIM = 64
SB_HEADS = 8
FOX_HEADS = 8
DSA_HEADS = 8
DSA_KV_HEADS = 2
DSA_GROUP = DSA_HEADS // DSA_KV_HEADS
IDX_HEADS = 4
IDX_DIM = 64
N_BRANCH = 3
ROPE_THETA = 500000.0
ROT_DIM = HEAD_DIM // 4
TOPK_MAX = 256
NORM_EPS = 1e-6

LANES = 128
SUBLANES = 8
SUM_ROWS = 2 * SUBLANES
BRANCH_WIDTH = SB_HEADS * HEAD_DIM
ATTN_SCALE = HEAD_DIM ** -0.5
assert IDX_DIM == HEAD_DIM
IDX_W_SCALE = IDX_HEADS ** -0.5

COL_QA, COL_KA = 0, 512
COL_QB, COL_KB = 1024, 1536
COL_QC = 2048
COL_KC = 2560
COL_QI = 2816
COL_KI = 3072
MAIN_COLS = 3200
ROPE_START = COL_QC
ROW_VA, ROW_VB, ROW_VC = 0, 512, 1024
VT_ROWS = 1152
VALS_COLS = VT_ROWS + LANES
MISC_FL, MISC_WI = 0, 8

TQ = 256
TK = 256
HEADS_PER_STEP = 8
NEG_BIG = -1e30

VMEM_LIMIT = 56 * 1024 * 1024


def _cparams(sem):
    return pltpu.CompilerParams(dimension_semantics=sem, vmem_limit_bytes=VMEM_LIMIT)


def _rmsnorm(x, g):
    y = x * lax.rsqrt(jnp.mean(x * x, axis=-1, keepdims=True) + NORM_EPS)
    return y * g


def _dot(a, b):
    return jnp.dot(a, b, preferred_element_type=F32)


def _dot_nt(a, b):
    return lax.dot_general(a, b, (((1,), (1,)), ((), ())), preferred_element_type=F32)


def _split2(x):
    hi = x.astype(MXU_DTYPE)
    lo = (x - hi.astype(F32)).astype(MXU_DTYPE)
    return hi, lo


def _split3(x):
    hi = x.astype(MXU_DTYPE)
    r = x - hi.astype(F32)
    mid = r.astype(MXU_DTYPE)
    lo = (r - mid.astype(F32)).astype(MXU_DTYPE)
    return hi, mid, lo


def _log_sigmoid_parts(z):
    neg_abs = lax.bitcast_convert_type(lax.bitcast_convert_type(z, jnp.int32) | jnp.int32(-2 ** 31), F32)
    sp = jnp.log(1.0 + jnp.exp(neg_abs))
    log_sig = jnp.minimum(z, 0.0) - sp
    return log_sig, z - log_sig


def _proj_kernel(x_ref, pos_ref, g_ref, freq_ref, sg1_ref, sg2_ref, wm_ref, wv_ref,
                 main_ref, vt_ref, misc_ref):
    h = _rmsnorm(x_ref[...], g_ref[...]).astype(MXU_DTYPE)
    ang = pos_ref[...].astype(F32) * freq_ref[...]
    cos = jnp.cos(ang)
    sin = jnp.sin(ang)
    s_up = sin * sg1_ref[...]
    s_dn = sin * sg2_ref[...]
    half = ROT_DIM // 2
    chunk = 4 * LANES
    for c0 in range(0, VALS_COLS, 2 * LANES):
        pv = _dot(h, wv_ref[:, c0:c0 + 2 * LANES])
        for r0 in (c0, c0 + LANES):
            piece = pv[:, r0 - c0:r0 - c0 + LANES]
            if r0 < VT_ROWS:
                vt_ref[0, r0:r0 + LANES, :] = piece.T.astype(vt_ref.dtype)
            else:
                misc_ref[...] = piece
    for c0 in list(range(ROPE_START, MAIN_COLS, chunk)) + list(range(0, ROPE_START, chunk)):
        c1 = min(c0 + chunk, MAIN_COLS)
        p = _dot(h, wm_ref[:, c0:c1])
        if c0 >= ROPE_START:
            for t in range((c1 - c0) // LANES):
                pt = p[:, t * LANES:(t + 1) * LANES]
                pt = (pt * cos + pltpu.roll(pt, LANES - half, 1) * s_up
                      + pltpu.roll(pt, half, 1) * s_dn)
                main_ref[:, c0 + t * LANES:c0 + (t + 1) * LANES] = pt.astype(main_ref.dtype)
        else:
            main_ref[:, c0:c1] = p.astype(main_ref.dtype)


def _proj(x2d, pos2d, g, freq, sg1, sg2, w_main, w_v, tm, batch, seq):
    n, d = x2d.shape
    nsb = seq // tm
    const = lambda i: (0, 0)
    return pl.pallas_call(
        _proj_kernel,
        grid=(n // tm,),
        in_specs=[
            pl.BlockSpec((tm, d), lambda i: (i, 0)),
            pl.BlockSpec((tm, 1), lambda i: (i, 0)),
            pl.BlockSpec((1, d), const),
            pl.BlockSpec((1, LANES), const),
            pl.BlockSpec((1, LANES), const),
            pl.BlockSpec((1, LANES), const),
            pl.BlockSpec((d, MAIN_COLS), const),
            pl.BlockSpec((d, VALS_COLS), const),
        ],
        out_specs=[
            pl.BlockSpec((tm, MAIN_COLS), lambda i: (i, 0)),
            pl.BlockSpec((1, VT_ROWS, tm), lambda i: (i // nsb, 0, i % nsb)),
            pl.BlockSpec((tm, LANES), lambda i: (i, 0)),
        ],
        out_shape=[
            jax.ShapeDtypeStruct((n, MAIN_COLS), MXU_DTYPE),
            jax.ShapeDtypeStruct((batch, VT_ROWS, seq), MXU_DTYPE),
            jax.ShapeDtypeStruct((n, LANES), F32),
        ],
        compiler_params=_cparams(("parallel",)),
        name="proj",
    )(x2d, pos2d, g, freq, sg1, sg2, w_main, w_v)


def _cumf_kernel(misc_ref, bias_ref, c_ref):
    s = misc_ref.shape[1]
    r = lax.broadcasted_iota(jnp.int32, (TK, TK), 0)
    c = lax.broadcasted_iota(jnp.int32, (TK, TK), 1)
    incl = (c <= r).astype(MXU_DTYPE)
    carry = jnp.zeros((1, LANES), F32)
    for b in range(s // TK):
        logit = misc_ref[0, b * TK:(b + 1) * TK, :] + bias_ref[...]
        logf, _ = _log_sigmoid_parts(logit)
        hi, mid, lo = _split3(logf)
        cs = (_dot(incl, lo) + _dot(incl, mid)) + _dot(incl, hi) + carry
        c_ref[0, b * TK:(b + 1) * TK, :] = cs
        carry = cs[TK - 1:TK, :]


def _cumf(misc3d, bias_row):
    b, s, _ = misc3d.shape
    return pl.pallas_call(
        _cumf_kernel,
        grid=(b,),
        in_specs=[
            pl.BlockSpec((1, s, LANES), lambda i: (i, 0, 0)),
            pl.BlockSpec((1, LANES), lambda i: (0, 0)),
        ],
        out_specs=pl.BlockSpec((1, s, LANES), lambda i: (i, 0, 0)),
        out_shape=jax.ShapeDtypeStruct((b, s, LANES), F32),
        compiler_params=_cparams(("parallel",)),
        name="cumf",
    )(misc3d, bias_row)


def _head_q(q_tile, half):
    lane = lax.broadcasted_iota(jnp.int32, q_tile.shape, 1)
    keep = (lane // HEAD_DIM) == half
    return jnp.where(keep, q_tile.astype(F32) * ATTN_SCALE, 0.0).astype(MXU_DTYPE)


def _causal_tile_t(strict):
    key = lax.broadcasted_iota(jnp.int32, (TK, TQ), 0)
    qry = lax.broadcasted_iota(jnp.int32, (TK, TQ), 1)
    return (key < qry) if strict else (key <= qry)


def _rows(j):
    return pl.ds(pl.multiple_of(j * TK, TK), TK)


def _lookahead_loop(n_last, bufs, issue, step, carry, finish, primed=False, fused=None):
    buf_a, buf_b = bufs

    def ahead_and_step(n, buf, other, carry):
        if fused is not None:
            return fused(n, buf, other, carry)
        issue(n + 1, other)
        return step(n, buf, carry, False)

    def pair(t, carry):
        n = 2 * t
        carry = ahead_and_step(n, buf_a, buf_b, carry)
        return ahead_and_step(n + 1, buf_b, buf_a, carry)

    def odd_tail(carry):
        carry = ahead_and_step(n_last - 1, buf_a, buf_b, carry)
        step(n_last, buf_b, carry, True)
        finish()
        return 0

    def even_tail(carry):
        step(n_last, buf_a, carry, True)
        finish()
        return 0

    if not primed:
        issue(0, buf_a)
    carry = lax.fori_loop(0, n_last // 2, pair, carry)
    lax.cond(n_last % 2 == 1, odd_tail, even_tail, carry)


def _flash_heads(heads, n_full, diag_mask, z_bufs, acc_ref, o_ref):
    half_k, half_q = TK // 2, TQ // 2

    def add_bias(s, bias):
        if isinstance(bias, (list, tuple)):
            return jnp.concatenate([s[:, t * LANES:(t + 1) * LANES] + bias[t] for t in range(len(bias))],
                                   axis=1)
        return s + bias

    ones = jnp.ones((SUM_ROWS, TK), MXU_DTYPE)

    def issue_head(h, j, buf):
        rows = _rows(j)
        qh, k_at, _, bias_at = heads[h]
        s = add_bias(_dot_nt(k_at(rows), qh), bias_at(rows))
        buf[h, :TK, :] = s
        buf[h, TK:, :] = jnp.max(s.reshape(TK // SUBLANES, SUBLANES, TQ), axis=0)

    def issue(j, buf):
        for h in range(len(heads)):
            issue_head(h, j, buf)

    def step_and_issue(j, buf, other, maxes):
        rows = _rows(j)
        out = []
        for h, ((_, _, vt_at, _), m) in enumerate(zip(heads, maxes)):
            issue_head(h, j + 1, other)
            m_new = jnp.maximum(m, jnp.max(buf[h, TK:, :], axis=0, keepdims=True))
            p = jnp.exp(buf[h, :TK, :] - m_new).astype(MXU_DTYPE)
            vt1 = jnp.concatenate([vt_at(rows), ones], axis=0)
            acc_ref[h] = jnp.exp(m - m_new) * acc_ref[h] + _dot(vt1, p)
            out.append(m_new)
        return tuple(out)

    def update(tiles, rows, maxes, lanes):
        stats = []
        for s, m in zip(tiles, maxes):
            m_new = jnp.maximum(m, jnp.max(s, axis=0, keepdims=True))
            stats.append((m_new, jnp.exp(m - m_new), jnp.exp(s - m_new).astype(MXU_DTYPE)))
        for h, ((_, alpha, p), (_, _, vt_at, _)) in enumerate(zip(stats, heads)):
            vt1 = jnp.concatenate([vt_at(rows), ones[:, :p.shape[0]]], axis=0)
            acc_ref[h, :, lanes] = alpha * acc_ref[h, :, lanes] + _dot(vt1, p)
        return tuple(m_new for m_new, _, _ in stats)

    def diagonal_step(j, buf, maxes):
        start = pl.multiple_of(j * TK, TK)
        early, late = pl.ds(start, half_k), pl.ds(pl.multiple_of(start + half_k, half_k), half_k)

        def masked(s, mask):
            return s if diag_mask is None else jnp.where(mask, s, NEG_BIG)

        tiles = [masked(buf[h, :half_k, :], None if diag_mask is None else diag_mask[:half_k, :])
                 for h in range(len(heads))]
        maxes = update(tiles, early, maxes, slice(None))
        tiles = [masked(buf[h, half_k:TK, half_q:], None if diag_mask is None else diag_mask[half_k:, half_q:])
                 for h in range(len(heads))]
        update(tiles, late, tuple(m[:, half_q:] for m in maxes), slice(half_q, TQ))
        return maxes

    def finish():
        _store_heads_t(o_ref, [acc_ref[h, :HEAD_DIM, :] / acc_ref[h, HEAD_DIM:HEAD_DIM + 1, :]
                               for h in range(len(heads))])

    acc_ref[...] = jnp.zeros(acc_ref.shape, F32)
    init = tuple(jnp.full((1, TQ), NEG_BIG, F32) for _ in heads)
    _lookahead_loop(n_full, z_bufs, issue, lambda j, buf, maxes, last: diagonal_step(j, buf, maxes), init,
                    finish, fused=step_and_issue)


def _store_heads_t(o_ref, outs_t):
    for t in range(len(outs_t) // 2):
        pair_t = jnp.concatenate([outs_t[2 * t], outs_t[2 * t + 1]], axis=0)
        o_ref[0, :, t * LANES:(t + 1) * LANES] = pair_t.T.astype(o_ref.dtype)


def _sb_kernel(q_ref, k_ref, vt_ref, o_ref, za_ref, zb_ref, acc_ref):
    i = pl.program_id(2)
    r = lax.broadcasted_iota(jnp.int32, (TK, TK), 0)
    c = lax.broadcasted_iota(jnp.int32, (TK, TK), 1)
    after = (c > r).astype(MXU_DTYPE)
    strict = _causal_tile_t(True)
    n_heads = HEADS_PER_STEP
    qhs = [_head_q(q_ref[0][:, (h // 2) * LANES:(h // 2 + 1) * LANES], h % 2) for h in range(n_heads)]

    def issue(j, buf):
        for h in range(n_heads):
            buf[h] = _dot_nt(k_ref[0, _rows(j), (h // 2) * LANES:(h // 2 + 1) * LANES], qhs[h])

    def update(tiles, rows, tails, lanes, mask):
        n_rows = tiles[0].shape[0]
        parts = []
        for z in tiles:
            log_beta, neg_l1m = _log_sigmoid_parts(z)
            if mask is not None:
                neg_l1m = jnp.where(mask, neg_l1m, 0.0)
            parts.append((log_beta, neg_l1m[0:1, :], neg_l1m.astype(MXU_DTYPE)))
        after_rows = after[:n_rows, :n_rows]
        laters = [_dot(after_rows, hl) for (_, _, hl) in parts]
        ws = []
        for (log_beta, _, _), later in zip(parts, laters):
            w = jnp.exp(log_beta - later)
            if mask is not None:
                w = jnp.where(mask, w, 0.0)
            ws.append(w.astype(MXU_DTYPE))
        out = []
        for h, (w, (_, first, _), later, tail) in enumerate(zip(ws, parts, laters, tails)):
            vtj = vt_ref[0, h * HEAD_DIM:(h + 1) * HEAD_DIM, rows]
            acc_ref[h, :, lanes] = acc_ref[h, :, lanes] + jnp.exp(-tail) * _dot(vtj, w)
            out.append(tail + (first + later[0:1, :]))
        return tuple(out)

    def step(j, buf, tails):
        return update([buf[h] for h in range(n_heads)], _rows(j), tails, slice(None), None)

    def diagonal_step(j, buf, tails):
        half_k, half_q = TK // 2, TQ // 2
        start = pl.multiple_of(j * TK, TK)
        early, late = pl.ds(start, half_k), pl.ds(pl.multiple_of(start + half_k, half_k), half_k)
        late_tails = update([buf[h, half_k:, half_q:] for h in range(n_heads)], late,
                            tuple(t[:, half_q:] for t in tails), slice(half_q, TQ), strict[half_k:, half_q:])
        tails = tuple(jnp.concatenate([t[:, :half_q], lt], axis=1) for t, lt in zip(tails, late_tails))
        return update([buf[h, :half_k, :] for h in range(n_heads)], early, tails, slice(None), strict[:half_k, :])

    acc_ref[...] = jnp.zeros(acc_ref.shape, F32)
    carry = tuple(jnp.zeros((1, TQ), F32) for _ in range(n_heads))
    issue(i, za_ref)

    def finish():
        _store_heads_t(o_ref, [acc_ref[h] for h in range(n_heads)])

    def with_earlier_blocks(cr):
        issue(i - 1, zb_ref)
        cr = diagonal_step(i, za_ref, cr)
        _lookahead_loop(i - 1, (zb_ref, za_ref), lambda n, buf: issue(i - 1 - n, buf),
                        lambda n, buf, c, last: step(i - 1 - n, buf, c), cr, finish, primed=True)
        return 0

    def diagonal_only(cr):
        diagonal_step(i, za_ref, cr)
        finish()
        return 0

    lax.cond(i > 0, with_earlier_blocks, diagonal_only, carry)


def _sb(main3d, vt3d):
    b, s, _ = main3d.shape
    nq = s // TQ
    w = HEADS_PER_STEP * HEAD_DIM
    qb, kb, vb = COL_QA // w, COL_KA // w, ROW_VA // w
    return pl.pallas_call(
        _sb_kernel,
        grid=(b, SB_HEADS // HEADS_PER_STEP, nq),
        in_specs=[
            pl.BlockSpec((1, TQ, w), lambda bi, hg, i: (bi, i, qb + hg)),
            pl.BlockSpec((1, s, w), lambda bi, hg, i: (bi, 0, kb + hg)),
            pl.BlockSpec((1, w, s), lambda bi, hg, i: (bi, vb + hg, 0)),
        ],
        out_specs=pl.BlockSpec((1, TQ, w), lambda bi, hg, i: (bi, i, hg)),
        out_shape=jax.ShapeDtypeStruct((b, s, BRANCH_WIDTH), MXU_DTYPE),
        scratch_shapes=[pltpu.VMEM((HEADS_PER_STEP, TK, TQ), F32), pltpu.VMEM((HEADS_PER_STEP, TK, TQ), F32),
                        pltpu.VMEM((HEADS_PER_STEP, HEAD_DIM, TQ), F32)],
        compiler_params=_cparams(("parallel", "parallel", "arbitrary")),
        name="sb",
    )(main3d, main3d, vt3d)


def _fox_kernel(q_ref, k_ref, vt_ref, c_ref, o_ref, cb_ref, za_ref, zb_ref, acc_ref):
    hg = pl.program_id(1)
    i = pl.program_id(2)
    n_heads = HEADS_PER_STEP

    @pl.when(i == 0)
    def _():
        c = c_ref[0]
        lane = lax.broadcasted_iota(jnp.int32, c.shape, 1)
        for h in range(n_heads):
            if n_heads == FOX_HEADS:
                col = c[:, h:h + 1]
            else:
                col = jnp.sum(jnp.where(lane == hg * n_heads + h, c, 0.0), axis=1, keepdims=True)
            cb_ref[h] = jnp.broadcast_to(-col, c.shape)

    heads = []
    for h in range(n_heads):
        t = h // 2
        qh = _head_q(q_ref[0][:, t * LANES:(t + 1) * LANES], h % 2)
        k_at = lambda rows, t=t: k_ref[0, rows, t * LANES:(t + 1) * LANES]
        vt_at = lambda rows, h=h: vt_ref[0, h * HEAD_DIM:(h + 1) * HEAD_DIM, rows]
        bias_at = lambda rows, h=h: [cb_ref[h, rows, :]] * (TQ // LANES)
        heads.append((qh, k_at, vt_at, bias_at))
    _flash_heads(heads, i, _causal_tile_t(False), (za_ref, zb_ref), acc_ref, o_ref)


def _fox(main3d, vt3d, c3d):
    b, s, _ = main3d.shape
    nq = s // TQ
    w = HEADS_PER_STEP * HEAD_DIM
    qb, kb, vb = COL_QB // w, COL_KB // w, ROW_VB // w
    return pl.pallas_call(
        _fox_kernel,
        grid=(b, FOX_HEADS // HEADS_PER_STEP, nq),
        in_specs=[
            pl.BlockSpec((1, TQ, w), lambda bi, hg, i: (bi, i, qb + hg)),
            pl.BlockSpec((1, s, w), lambda bi, hg, i: (bi, 0, kb + hg)),
            pl.BlockSpec((1, w, s), lambda bi, hg, i: (bi, vb + hg, 0)),
            pl.BlockSpec((1, s, LANES), lambda bi, hg, i: (bi, 0, 0)),
        ],
        out_specs=pl.BlockSpec((1, TQ, w), lambda bi, hg, i: (bi, i, hg)),
        out_shape=jax.ShapeDtypeStruct((b, s, BRANCH_WIDTH), MXU_DTYPE),
        scratch_shapes=[pltpu.VMEM((HEADS_PER_STEP, s, LANES), F32),
                        pltpu.VMEM((HEADS_PER_STEP, TK + SUBLANES, TQ), F32),
                        pltpu.VMEM((HEADS_PER_STEP, TK + SUBLANES, TQ), F32),
                        pltpu.VMEM((HEADS_PER_STEP, HEAD_DIM + SUM_ROWS, TQ), F32)],
        compiler_params=_cparams(("parallel", "parallel", "arbitrary")),
        name="fox",
    )(main3d, main3d, vt3d, c3d)


def _dsa_select(i, qi_ref, ki_ref, misc_ref, key_ref, dig_ref, bias_ref, top_k):
    diag = _causal_tile_t(False)

    wi_t = misc_ref[0].T[MISC_WI:MISC_WI + IDX_HEADS, :] * IDX_W_SCALE
    qih = [_head_q(qi_ref[0][:, (h // 2) * LANES:(h // 2 + 1) * LANES], h % 2) for h in range(IDX_HEADS)]

    def score_block(j, mask):
        kij = ki_ref[0, _rows(j), :]
        sc = None
        for h in range(IDX_HEADS):
            term = wi_t[h:h + 1, :] * jnp.maximum(_dot_nt(kij, qih[h]), 0.0)
            sc = term if sc is None else sc + term
        if mask is not None:
            sc = jnp.where(mask, sc, -jnp.inf)
        bits = lax.bitcast_convert_type(jnp.where(sc == 0.0, 0.0, sc), jnp.int32)
        key_ref[_rows(j), :] = bits ^ ((bits >> 31) & jnp.int32(0x7FFFFFFF))

    def score_pair(t, _):
        score_block(2 * t, None)
        score_block(2 * t + 1, None)
        return 0

    def score_odd_tail():
        score_block(i - 1, None)
        score_block(i, diag)
        return 0

    def score_even_tail():
        score_block(i, diag)
        return 0

    lax.fori_loop(0, i // 2, score_pair, 0)
    lax.cond(i % 2 == 1, score_odd_tail, score_even_tail)

    small = MXU_DTYPE
    group = 2 * SUBLANES

    assert (key_ref.shape[0] // group) <= 256

    def count_digits_ge(cand):
        def hits(j):
            hit = jnp.where(dig_ref[_rows(j), :] >= cand, jnp.ones((), small), jnp.zeros((), small))
            part = hit[0:group]
            for g in range(1, TK // group):
                part = part + hit[g * group:(g + 1) * group]
            return part

        acc = lax.fori_loop(0, (i + 1) // 2, lambda t, a: a + (hits(2 * t) + hits(2 * t + 1)),
                            jnp.zeros((group, TQ), small))
        acc = lax.cond((i + 1) % 2 == 1, lambda a: a + hits(i), lambda a: a, acc)
        return jnp.sum(acc.astype(F32), axis=0, keepdims=True)

    above = jnp.zeros((1, TQ), F32)
    prefix = None
    for shift in (24, 16, 8, 0):
        def fill(j, _, shift=shift, prefix=prefix):
            key = key_ref[_rows(j), :]
            if prefix is None:
                digit = (key >> shift) + 128
            else:
                digit = (key ^ (prefix << (shift + 8))) >> shift
                shares = lax.bitcast_convert_type(digit, jnp.uint32) < jnp.uint32(256)
                digit = jnp.where(shares, digit, -1)
            dig_ref[_rows(j), :] = digit.astype(F32).astype(small)
            return 0

        lax.fori_loop(0, i + 1, fill, 0)

        def search(_, st, above=above):
            lo, hi, n_hi = st
            mid = (lo + hi) * 0.5
            n_mid = count_digits_ge(mid.astype(small))
            ok = above + n_mid >= top_k
            return jnp.where(ok, mid, lo), jnp.where(ok, hi, mid), jnp.where(ok, n_hi, n_mid)

        start = (jnp.zeros((1, TQ), F32), jnp.full((1, TQ), 256.0, F32), jnp.zeros((1, TQ), F32))
        digit, _, n_hi = lax.fori_loop(0, 8, search, start)
        above = above + n_hi
        digit = digit.astype(jnp.int32)
        prefix = digit - 128 if prefix is None else (prefix << 8) | digit
    thr_key = prefix

    need = top_k - above
    kr = lax.broadcasted_iota(jnp.int32, (TK, TK), 0)
    kc = lax.broadcasted_iota(jnp.int32, (TK, TK), 1)
    upto = (kc <= kr).astype(MXU_DTYPE)

    def bias_block(j, mask, seen):
        key = key_ref[_rows(j), :]
        tie = key == thr_key
        rank = _dot(upto, jnp.where(tie, 1.0, 0.0).astype(MXU_DTYPE)) + seen
        sel = (key > thr_key) | (tie & (rank <= need))
        if mask is not None:
            sel = sel & mask
        bias_ref[_rows(j), :] = jnp.where(sel, 0.0, NEG_BIG)
        return rank[TK - 1:TK, :]

    def bias_pair(t, seen):
        return bias_block(2 * t + 1, None, bias_block(2 * t, None, seen))

    def bias_odd_tail(seen):
        bias_block(i, diag, bias_block(i - 1, None, seen))
        return 0

    def bias_even_tail(seen):
        bias_block(i, diag, seen)
        return 0

    seen = lax.fori_loop(0, i // 2, bias_pair, jnp.zeros((1, TQ), F32))
    lax.cond(i % 2 == 1, bias_odd_tail, bias_even_tail, seen)


def _dsa_kernel(q_ref, k_ref, vt_ref, qi_ref, ki_ref, misc_ref, o_ref, key_ref, dig_ref, bias_ref,
                za_ref, zb_ref, acc_ref, *, top_k):
    i = pl.program_id(1)

    def every_causal_key():
        def earlier_block(j, _):
            bias_ref[_rows(j), :] = jnp.zeros((TK, TQ), F32)
            return 0

        lax.fori_loop(0, i, earlier_block, 0)
        bias_ref[_rows(i), :] = jnp.where(_causal_tile_t(False), 0.0, NEG_BIG)
        return 0

    def searched_keys():
        _dsa_select(i, qi_ref, ki_ref, misc_ref, key_ref, dig_ref, bias_ref, top_k)
        return 0

    lax.cond((i + 1) * TQ <= top_k, every_causal_key, searched_keys)

    bias_at = lambda rows: bias_ref[rows, :]
    heads = []
    for h in range(DSA_HEADS):
        g = h // DSA_GROUP
        k_at = lambda rows, g=g: k_ref[0, rows, g * LANES:(g + 1) * LANES]
        vt_at = lambda rows, g=g: vt_ref[0, g * HEAD_DIM:(g + 1) * HEAD_DIM, rows]
        qh = _head_q(q_ref[0][:, (h // 2) * LANES:(h // 2 + 1) * LANES], h % 2)
        heads.append((qh, k_at, vt_at, bias_at))
    _flash_heads(heads, i, None, (za_ref, zb_ref), acc_ref, o_ref)


def _dsa(main3d, vt3d, misc3d, top_k):
    b, s, _ = main3d.shape
    nq = s // TQ
    assert top_k <= TK, "the first key block must hold at least top_k keys"
    return pl.pallas_call(
        functools.partial(_dsa_kernel, top_k=top_k),
        grid=(b, nq),
        in_specs=[
            pl.BlockSpec((1, TQ, 4 * LANES), lambda bi, i: (bi, i, COL_QC // (4 * LANES))),
            pl.BlockSpec((1, s, 2 * LANES), lambda bi, i: (bi, 0, COL_KC // (2 * LANES))),
            pl.BlockSpec((1, LANES, s), lambda bi, i: (bi, ROW_VC // LANES, 0)),
            pl.BlockSpec((1, TQ, 2 * LANES), lambda bi, i: (bi, i, COL_QI // (2 * LANES))),
            pl.BlockSpec((1, s, LANES), lambda bi, i: (bi, 0, COL_KI // LANES)),
            pl.BlockSpec((1, TQ, LANES), lambda bi, i: (bi, i, 0)),
        ],
        out_specs=pl.BlockSpec((1, TQ, BRANCH_WIDTH), lambda bi, i: (bi, i, 0)),
        out_shape=jax.ShapeDtypeStruct((b, s, BRANCH_WIDTH), MXU_DTYPE),
        scratch_shapes=[pltpu.VMEM((s, TQ), jnp.int32), pltpu.VMEM((s, TQ), MXU_DTYPE), pltpu.VMEM((s, TQ), F32),
                        pltpu.VMEM((DSA_HEADS, TK + SUBLANES, TQ), F32),
                        pltpu.VMEM((DSA_HEADS, TK + SUBLANES, TQ), F32),
                        pltpu.VMEM((DSA_HEADS, HEAD_DIM + SUM_ROWS, TQ), F32)],
        compiler_params=_cparams(("parallel", "arbitrary")),
        name="dsa",
    )(main3d, main3d, vt3d, main3d, main3d, misc3d)


def _merge_kernel(x_ref, a_ref, b_ref, c_ref, g_ref, wg_ref, wb_ref, wo_ref, o_ref):
    x = x_ref[...]
    d = x.shape[1]
    h = _rmsnorm(x, g_ref[...]).astype(MXU_DTYPE)
    y = None
    for n, br_ref in enumerate((a_ref, b_ref, c_ref)):
        gate = jax.nn.sigmoid(_dot(h, wg_ref[:, n * d:(n + 1) * d]))
        term = gate * _dot(br_ref[...], wb_ref[n])
        y = term if y is None else y + term
    o_ref[...] = x + _dot(y.astype(MXU_DTYPE), wo_ref[...])


def _merge(x2d, br_a, br_b, br_c, g, w_gate, w_branch, w_out, tm):
    n, d = x2d.shape
    bw = br_a.shape[1]
    row = lambda i: (i, 0)
    const = lambda i: (0, 0)
    return pl.pallas_call(
        _merge_kernel,
        grid=(n // tm,),
        in_specs=[
            pl.BlockSpec((tm, d), row),
            pl.BlockSpec((tm, bw), row),
            pl.BlockSpec((tm, bw), row),
            pl.BlockSpec((tm, bw), row),
            pl.BlockSpec((1, d), const),
            pl.BlockSpec((d, N_BRANCH * d), const),
            pl.BlockSpec((N_BRANCH, bw, d), lambda i: (0, 0, 0)),
            pl.BlockSpec((d, d), const),
        ],
        out_specs=pl.BlockSpec((tm, d), row),
        out_shape=jax.ShapeDtypeStruct((n, d), F32),
        compiler_params=_cparams(("parallel",)),
        name="merge",
    )(x2d, br_a, br_b, br_c, g, w_gate, w_branch, w_out)


def _mlp_kernel(x_ref, g_ref, wu_ref, wd_ref, gf_ref, o_ref, *, ff_chunk, final_norm):
    x = x_ref[...]
    h = _rmsnorm(x, g_ref[...]).astype(MXU_DTYPE)
    acc = x
    for c in range(wu_ref.shape[1] // ff_chunk):
        u = jnp.maximum(_dot(h, wu_ref[:, c * ff_chunk:(c + 1) * ff_chunk]), 0.0)
        acc = acc + _dot((u * u).astype(MXU_DTYPE), wd_ref[c * ff_chunk:(c + 1) * ff_chunk, :])
    if final_norm:
        acc = _rmsnorm(acc, gf_ref[...])
    o_ref[...] = acc


def _mlp(x2d, g, w_up, w_down, g_final, tm, final_norm):
    n, d = x2d.shape
    ff = w_up.shape[1]
    row = lambda i: (i, 0)
    const = lambda i: (0, 0)
    return pl.pallas_call(
        functools.partial(_mlp_kernel, ff_chunk=min(ff, 1024), final_norm=final_norm),
        grid=(n // tm,),
        in_specs=[
            pl.BlockSpec((tm, d), row),
            pl.BlockSpec((1, d), const),
            pl.BlockSpec((d, ff), const),
            pl.BlockSpec((ff, d), const),
            pl.BlockSpec((1, d), const),
        ],
        out_specs=pl.BlockSpec((tm, d), row),
        out_shape=jax.ShapeDtypeStruct((n, d), F32),
        compiler_params=_cparams(("parallel",)),
        name="mlp",
    )(x2d, g, w_up, w_down, g_final)


def _w_in_columns(d_model):
    sizes = (("qa", BRANCH_WIDTH), ("ka", BRANCH_WIDTH), ("va", BRANCH_WIDTH),
             ("qb", BRANCH_WIDTH), ("kb", BRANCH_WIDTH), ("vb", BRANCH_WIDTH), ("fl", FOX_HEADS),
             ("qc", DSA_HEADS * HEAD_DIM), ("kc", DSA_KV_HEADS * HEAD_DIM), ("vc", DSA_KV_HEADS * HEAD_DIM),
             ("qi", IDX_HEADS * IDX_DIM), ("ki", IDX_DIM), ("wi", IDX_HEADS), ("gates", N_BRANCH * d_model))
    cols, o = {}, 0
    for name, width in sizes:
        cols[name] = (o, o + width)
        o += width
    return cols, o


def _pack_kernel(w_ref, main_ref, vals_ref, gates_ref):
    cols, _ = _w_in_columns(gates_ref.shape[2] // N_BRANCH)

    def src(name, lo=0, hi=None):
        a, b = cols[name]
        return w_ref[0, :, a + lo:(b if hi is None else a + hi)]

    def put(ref, start, value):
        ref[0, :, start:start + value.shape[1]] = value.astype(ref.dtype)

    put(main_ref, COL_QA, src("qa"))
    put(main_ref, COL_KA, src("ka"))
    put(main_ref, COL_QB, src("qb"))
    put(main_ref, COL_KB, src("kb"))
    put(main_ref, COL_QC, src("qc"))
    for g in range(DSA_KV_HEADS):
        kg = src("kc", g * HEAD_DIM, (g + 1) * HEAD_DIM)
        put(main_ref, COL_KC + g * LANES, jnp.concatenate([kg, kg], axis=1))
    put(main_ref, COL_QI, src("qi"))
    ki = src("ki")
    put(main_ref, COL_KI, jnp.concatenate([ki, ki], axis=1))
    put(vals_ref, ROW_VA, src("va"))
    put(vals_ref, ROW_VB, src("vb"))
    put(vals_ref, ROW_VC, src("vc"))
    pad = jnp.zeros((w_ref.shape[1], LANES - FOX_HEADS - IDX_HEADS), F32)
    put(vals_ref, VT_ROWS, jnp.concatenate([src("fl"), src("wi"), pad], axis=1))
    put(gates_ref, 0, src("gates"))


def _pack_w_in(w_in):
    depth, d, total = w_in.shape
    _, end = _w_in_columns(d)
    assert end == total
    rows = 256
    blk = lambda width: pl.BlockSpec((1, rows, width), lambda l, r: (l, r, 0))
    widths = (MAIN_COLS, VALS_COLS, N_BRANCH * d)
    return pl.pallas_call(
        _pack_kernel,
        grid=(depth, d // rows),
        in_specs=[blk(total)],
        out_specs=[blk(wd) for wd in widths],
        out_shape=[jax.ShapeDtypeStruct((depth, d, wd), MXU_DTYPE) for wd in widths],
        compiler_params=_cparams(("parallel", "parallel")),
        name="pack",
    )(w_in)


def _rope_rows():
    lane = jnp.arange(LANES) % HEAD_DIM
    half = ROT_DIM // 2
    inv_freq = ROPE_THETA ** (-jnp.arange(0, ROT_DIM, 2, dtype=F32) / ROT_DIM)
    freq = jnp.where(lane < ROT_DIM, inv_freq[lane % half], 0.0).astype(F32)
    sg1 = jnp.where(lane < half, -1.0, 0.0).astype(F32)
    sg2 = jnp.where((lane >= half) & (lane < ROT_DIM), 1.0, 0.0).astype(F32)
    return freq[None, :], sg1[None, :], sg2[None, :]


def kernel(x, positions, g_mix, w_in, b_forget, w_branch, w_out, g_mlp, w_up, w_down, g_final):
    b, s, d = x.shape
    n = b * s
    depth = w_in.shape[0]
    top_k = min(TOPK_MAX, s // 4)
    tm = min(512, s)
    freq, sg1, sg2 = _rope_rows()
    pos2d = positions.reshape(n, 1)
    x2d = x.reshape(n, d)
    w_main_all, w_v_all, w_gate_all = _pack_w_in(w_in)
    for layer in range(depth):
        w_main, w_v, w_gate = w_main_all[layer], w_v_all[layer], w_gate_all[layer]
        bias_row = jnp.concatenate([b_forget[layer], jnp.zeros((LANES - FOX_HEADS,), F32)])[None, :]
        main, vt, misc = _proj(x2d, pos2d, g_mix[layer][None, :], freq, sg1, sg2, w_main, w_v, tm, b, s)
        main3d = main.reshape(b, s, MAIN_COLS)
        misc3d = misc.reshape(b, s, LANES)
        cum = _cumf(misc3d, bias_row)
        br_a = _sb(main3d, vt).reshape(n, BRANCH_WIDTH)
        br_b = _fox(main3d, vt, cum).reshape(n, BRANCH_WIDTH)
        br_c = _dsa(main3d, vt, misc3d, top_k).reshape(n, BRANCH_WIDTH)
        x2d = _merge(x2d, br_a, br_b, br_c, g_mix[layer][None, :], w_gate,
                     w_branch[layer].astype(MXU_DTYPE), w_out[layer].astype(MXU_DTYPE), tm)
        x2d = _mlp(x2d, g_mlp[layer][None, :], w_up[layer].astype(MXU_DTYPE),
                   w_down[layer].astype(MXU_DTYPE), g_final[None, :], tm, layer == depth - 1)
    return x2d.reshape(b, s, d)
```

```python
import functools

import jax
import jax.numpy as jnp
from jax import lax
from jax.experimental import pallas as pl
from jax.experimental.pallas import tpu as pltpu

F32 = jnp.float32
MXU_DTYPE = jnp.bfloat16

HEAD_DIM = 64
SB_HEADS = 8
FOX_HEADS = 8
DSA_HEADS = 8
DSA_KV_HEADS = 2
DSA_GROUP = DSA_HEADS // DSA_KV_HEADS
IDX_HEADS = 4
IDX_DIM = 64
N_BRANCH = 3
ROPE_THETA = 500000.0
ROT_DIM = HEAD_DIM // 4
TOPK_MAX = 256
NORM_EPS = 1e-6

LANES = 128
SUBLANES = 8
SUM_ROWS = 2 * SUBLANES
BRANCH_WIDTH = SB_HEADS * HEAD_DIM
ATTN_SCALE = HEAD_DIM ** -0.5
assert IDX_DIM == HEAD_DIM
IDX_W_SCALE = IDX_HEADS ** -0.5

COL_QA, COL_KA = 0, 512
COL_QB, COL_KB = 1024, 1536
COL_QC = 2048
COL_KC = 2560
COL_QI = 2816
COL_KI = 3072
MAIN_COLS = 3200
ROPE_START = COL_QC
ROW_VA, ROW_VB, ROW_VC = 0, 512, 1024
VT_ROWS = 1152
VALS_COLS = VT_ROWS + LANES
MISC_WI = FOX_HEADS

TQ = 256
TK = 256
HEADS_PER_STEP = 8
NEG_BIG = -1e30

VMEM_LIMIT = 56 * 1024 * 1024


def _cparams(sem):
    return pltpu.CompilerParams(dimension_semantics=sem, vmem_limit_bytes=VMEM_LIMIT)


def _rmsnorm(x, g):
    y = x * lax.rsqrt(jnp.mean(x * x, axis=-1, keepdims=True) + NORM_EPS)
    return y * g


def _dot(a, b):
    return jnp.dot(a, b, preferred_element_type=F32)


def _dot_nt(a, b):
    return lax.dot_general(a, b, (((1,), (1,)), ((), ())), preferred_element_type=F32)


def _split3(x):
    hi = x.astype(MXU_DTYPE)
    r = x - hi.astype(F32)
    mid = r.astype(MXU_DTYPE)
    lo = (r - mid.astype(F32)).astype(MXU_DTYPE)
    return hi, mid, lo


def _log_sigmoid_parts(z):
    neg_abs = lax.bitcast_convert_type(lax.bitcast_convert_type(z, jnp.int32) | jnp.int32(-2 ** 31), F32)
    sp = jnp.log(1.0 + jnp.exp(neg_abs))
    log_sig = jnp.minimum(z, 0.0) - sp
    return log_sig, z - log_sig


def _proj_kernel(x_ref, pos_ref, g_ref, freq_ref, sg1_ref, sg2_ref, wm_ref, wv_ref,
                 main_ref, vt_ref, misc_ref):
    h = _rmsnorm(x_ref[...], g_ref[...]).astype(MXU_DTYPE)
    ang = pos_ref[...].astype(F32) * freq_ref[...]
    cos = jnp.cos(ang)
    sin = jnp.sin(ang)
    s_up = sin * sg1_ref[...]
    s_dn = sin * sg2_ref[...]
    half = ROT_DIM // 2
    chunk = 4 * LANES
    for c0 in range(0, VALS_COLS, 2 * LANES):
        pv = _dot(h, wv_ref[:, c0:c0 + 2 * LANES])
        for r0 in (c0, c0 + LANES):
            piece = pv[:, r0 - c0:r0 - c0 + LANES]
            if r0 < VT_ROWS:
                vt_ref[0, r0:r0 + LANES, :] = piece.T.astype(vt_ref.dtype)
            else:
                misc_ref[...] = piece
    for c0 in list(range(ROPE_START, MAIN_COLS, chunk)) + list(range(0, ROPE_START, chunk)):
        c1 = min(c0 + chunk, MAIN_COLS)
        p = _dot(h, wm_ref[:, c0:c1])
        if c0 >= ROPE_START:
            for t in range((c1 - c0) // LANES):
                pt = p[:, t * LANES:(t + 1) * LANES]
                pt = (pt * cos + pltpu.roll(pt, LANES - half, 1) * s_up
                      + pltpu.roll(pt, half, 1) * s_dn)
                main_ref[:, c0 + t * LANES:c0 + (t + 1) * LANES] = pt.astype(main_ref.dtype)
        else:
            main_ref[:, c0:c1] = p.astype(main_ref.dtype)


def _proj(x2d, pos2d, g, freq, sg1, sg2, w_main, w_v, tm, batch, seq):
    n, d = x2d.shape
    nsb = seq // tm
    const = lambda i: (0, 0)
    return pl.pallas_call(
        _proj_kernel,
        grid=(n // tm,),
        in_specs=[
            pl.BlockSpec((tm, d), lambda i: (i, 0)),
            pl.BlockSpec((tm, 1), lambda i: (i, 0)),
            pl.BlockSpec((1, d), const),
            pl.BlockSpec((1, LANES), const),
            pl.BlockSpec((1, LANES), const),
            pl.BlockSpec((1, LANES), const),
            pl.BlockSpec((d, MAIN_COLS), const),
            pl.BlockSpec((d, VALS_COLS), const),
        ],
        out_specs=[
            pl.BlockSpec((tm, MAIN_COLS), lambda i: (i, 0)),
            pl.BlockSpec((1, VT_ROWS, tm), lambda i: (i // nsb, 0, i % nsb)),
            pl.BlockSpec((tm, LANES), lambda i: (i, 0)),
        ],
        out_shape=[
            jax.ShapeDtypeStruct((n, MAIN_COLS), MXU_DTYPE),
            jax.ShapeDtypeStruct((batch, VT_ROWS, seq), MXU_DTYPE),
            jax.ShapeDtypeStruct((n, LANES), F32),
        ],
        compiler_params=_cparams(("parallel",)),
        name="proj",
    )(x2d, pos2d, g, freq, sg1, sg2, w_main, w_v)


def _cumf_kernel(misc_ref, bias_ref, c_ref):
    s = misc_ref.shape[1]
    r = lax.broadcasted_iota(jnp.int32, (TK, TK), 0)
    c = lax.broadcasted_iota(jnp.int32, (TK, TK), 1)
    incl = (c <= r).astype(MXU_DTYPE)
    carry = jnp.zeros((1, LANES), F32)
    for b in range(s // TK):
        logit = misc_ref[0, b * TK:(b + 1) * TK, :] + bias_ref[...]
        logf, _ = _log_sigmoid_parts(logit)
        hi, mid, lo = _split3(logf)
        cs = (_dot(incl, lo) + _dot(incl, mid)) + _dot(incl, hi) + carry
        c_ref[0, b * TK:(b + 1) * TK, :] = cs
        carry = cs[TK - 1:TK, :]


def _cumf(misc3d, bias_row):
    b, s, _ = misc3d.shape
    return pl.pallas_call(
        _cumf_kernel,
        grid=(b,),
        in_specs=[
            pl.BlockSpec((1, s, LANES), lambda i: (i, 0, 0)),
            pl.BlockSpec((1, LANES), lambda i: (0, 0)),
        ],
        out_specs=pl.BlockSpec((1, s, LANES), lambda i: (i, 0, 0)),
        out_shape=jax.ShapeDtypeStruct((b, s, LANES), F32),
        compiler_params=_cparams(("parallel",)),
        name="cumf",
    )(misc3d, bias_row)


def _head_q(q_tile, half):
    lane = lax.broadcasted_iota(jnp.int32, q_tile.shape, 1)
    keep = (lane // HEAD_DIM) == half
    return jnp.where(keep, q_tile.astype(F32) * ATTN_SCALE, 0.0).astype(MXU_DTYPE)


def _causal_tile_t(strict):
    key = lax.broadcasted_iota(jnp.int32, (TK, TQ), 0)
    qry = lax.broadcasted_iota(jnp.int32, (TK, TQ), 1)
    return (key < qry) if strict else (key <= qry)


def _rows(j):
    return pl.ds(pl.multiple_of(j * TK, TK), TK)


def _lookahead_loop(n_last, bufs, issue, step, carry, finish, primed=False, fused=None):
    buf_a, buf_b = bufs

    def ahead_and_step(n, buf, other, carry):
        if fused is not None:
            return fused(n, buf, other, carry)
        issue(n + 1, other)
        return step(n, buf, carry, False)

    def pair(t, carry):
        n = 2 * t
        carry = ahead_and_step(n, buf_a, buf_b, carry)
        return ahead_and_step(n + 1, buf_b, buf_a, carry)

    def odd_tail(carry):
        carry = ahead_and_step(n_last - 1, buf_a, buf_b, carry)
        step(n_last, buf_b, carry, True)
        finish()
        return 0

    def even_tail(carry):
        step(n_last, buf_a, carry, True)
        finish()
        return 0

    if not primed:
        issue(0, buf_a)
    carry = lax.fori_loop(0, n_last // 2, pair, carry)
    lax.cond(n_last % 2 == 1, odd_tail, even_tail, carry)


def _flash_heads(heads, n_full, diag_mask, z_bufs, acc_ref, o_ref):
    half_k, half_q = TK // 2, TQ // 2

    def add_bias(s, bias):
        if isinstance(bias, (list, tuple)):
            return jnp.concatenate([s[:, t * LANES:(t + 1) * LANES] + bias[t] for t in range(len(bias))],
                                   axis=1)
        return s + bias

    ones = jnp.ones((SUM_ROWS, TK), MXU_DTYPE)

    def issue_head(h, j, buf):
        rows = _rows(j)
        qh, k_at, _, bias_at = heads[h]
        s = add_bias(_dot_nt(k_at(rows), qh), bias_at(rows))
        buf[h, :TK, :] = s
        buf[h, TK:, :] = jnp.max(s.reshape(TK // SUBLANES, SUBLANES, TQ), axis=0)

    def issue(j, buf):
        for h in range(len(heads)):
            issue_head(h, j, buf)

    def step_and_issue(j, buf, other, maxes):
        rows = _rows(j)
        out = []
        for h, ((_, _, vt_at, _), m) in enumerate(zip(heads, maxes)):
            issue_head(h, j + 1, other)
            m_new = jnp.maximum(m, jnp.max(buf[h, TK:, :], axis=0, keepdims=True))
            p = jnp.exp(buf[h, :TK, :] - m_new).astype(MXU_DTYPE)
            vt1 = jnp.concatenate([vt_at(rows), ones], axis=0)
            acc_ref[h] = jnp.exp(m - m_new) * acc_ref[h] + _dot(vt1, p)
            out.append(m_new)
        return tuple(out)

    def update(tiles, rows, maxes, lanes):
        stats = []
        for s, m in zip(tiles, maxes):
            m_new = jnp.maximum(m, jnp.max(s, axis=0, keepdims=True))
            stats.append((m_new, jnp.exp(m - m_new), jnp.exp(s - m_new).astype(MXU_DTYPE)))
        for h, ((_, alpha, p), (_, _, vt_at, _)) in enumerate(zip(stats, heads)):
            vt1 = jnp.concatenate([vt_at(rows), ones[:, :p.shape[0]]], axis=0)
            acc_ref[h, :, lanes] = alpha * acc_ref[h, :, lanes] + _dot(vt1, p)
        return tuple(m_new for m_new, _, _ in stats)

    def diagonal_step(j, buf, maxes):
        start = pl.multiple_of(j * TK, TK)
        early, late = pl.ds(start, half_k), pl.ds(pl.multiple_of(start + half_k, half_k), half_k)

        def masked(s, mask):
            return s if diag_mask is None else jnp.where(mask, s, NEG_BIG)

        tiles = [masked(buf[h, :half_k, :], None if diag_mask is None else diag_mask[:half_k, :])
                 for h in range(len(heads))]
        maxes = update(tiles, early, maxes, slice(None))
        tiles = [masked(buf[h, half_k:TK, half_q:], None if diag_mask is None else diag_mask[half_k:, half_q:])
                 for h in range(len(heads))]
        update(tiles, late, tuple(m[:, half_q:] for m in maxes), slice(half_q, TQ))
        return maxes

    def finish():
        _store_heads_t(o_ref, [acc_ref[h, :HEAD_DIM, :] / acc_ref[h, HEAD_DIM:HEAD_DIM + 1, :]
                               for h in range(len(heads))])

    acc_ref[...] = jnp.zeros(acc_ref.shape, F32)
    init = tuple(jnp.full((1, TQ), NEG_BIG, F32) for _ in heads)
    _lookahead_loop(n_full, z_bufs, issue, lambda j, buf, maxes, last: diagonal_step(j, buf, maxes), init,
                    finish, fused=step_and_issue)


def _store_heads_t(o_ref, outs_t):
    for t in range(len(outs_t) // 2):
        pair_t = jnp.concatenate([outs_t[2 * t], outs_t[2 * t + 1]], axis=0)
        o_ref[0, :, t * LANES:(t + 1) * LANES] = pair_t.T.astype(o_ref.dtype)


def _sb_kernel(q_ref, k_ref, vt_ref, o_ref, za_ref, zb_ref, acc_ref):
    i = pl.program_id(2)
    r = lax.broadcasted_iota(jnp.int32, (TK, TK), 0)
    c = lax.broadcasted_iota(jnp.int32, (TK, TK), 1)
    after = (c > r).astype(MXU_DTYPE)
    strict = _causal_tile_t(True)
    n_heads = HEADS_PER_STEP
    qhs = [_head_q(q_ref[0][:, (h // 2) * LANES:(h // 2 + 1) * LANES], h % 2) for h in range(n_heads)]

    def issue(j, buf):
        for h in range(n_heads):
            buf[h] = _dot_nt(k_ref[0, _rows(j), (h // 2) * LANES:(h // 2 + 1) * LANES], qhs[h])

    def update(tiles, rows, tails, lanes, mask):
        n_rows = tiles[0].shape[0]
        parts = []
        for z in tiles:
            log_beta, neg_l1m = _log_sigmoid_parts(z)
            if mask is not None:
                neg_l1m = jnp.where(mask, neg_l1m, 0.0)
            parts.append((log_beta, neg_l1m[0:1, :], neg_l1m.astype(MXU_DTYPE)))
        after_rows = after[:n_rows, :n_rows]
        laters = [_dot(after_rows, hl) for (_, _, hl) in parts]
        ws = []
        for (log_beta, _, _), later in zip(parts, laters):
            w = jnp.exp(log_beta - later)
            if mask is not None:
                w = jnp.where(mask, w, 0.0)
            ws.append(w.astype(MXU_DTYPE))
        out = []
        for h, (w, (_, first, _), later, tail) in enumerate(zip(ws, parts, laters, tails)):
            vtj = vt_ref[0, h * HEAD_DIM:(h + 1) * HEAD_DIM, rows]
            acc_ref[h, :, lanes] = acc_ref[h, :, lanes] + jnp.exp(-tail) * _dot(vtj, w)
            out.append(tail + (first + later[0:1, :]))
        return tuple(out)

    def step(j, buf, tails):
        return update([buf[h] for h in range(n_heads)], _rows(j), tails, slice(None), None)

    def diagonal_step(j, buf, tails):
        half_k, half_q = TK // 2, TQ // 2
        start = pl.multiple_of(j * TK, TK)
        early, late = pl.ds(start, half_k), pl.ds(pl.multiple_of(start + half_k, half_k), half_k)
        late_tails = update([buf[h, half_k:, half_q:] for h in range(n_heads)], late,
                            tuple(t[:, half_q:] for t in tails), slice(half_q, TQ), strict[half_k:, half_q:])
        tails = tuple(jnp.concatenate([t[:, :half_q], lt], axis=1) for t, lt in zip(tails, late_tails))
        return update([buf[h, :half_k, :] for h in range(n_heads)], early, tails, slice(None), strict[:half_k, :])

    acc_ref[...] = jnp.zeros(acc_ref.shape, F32)
    carry = tuple(jnp.zeros((1, TQ), F32) for _ in range(n_heads))
    issue(i, za_ref)

    def finish():
        _store_heads_t(o_ref, [acc_ref[h] for h in range(n_heads)])

    def with_earlier_blocks(cr):
        issue(i - 1, zb_ref)
        cr = diagonal_step(i, za_ref, cr)
        _lookahead_loop(i - 1, (zb_ref, za_ref), lambda n, buf: issue(i - 1 - n, buf),
                        lambda n, buf, c, last: step(i - 1 - n, buf, c), cr, finish, primed=True)
        return 0

    def diagonal_only(cr):
        diagonal_step(i, za_ref, cr)
        finish()
        return 0

    lax.cond(i > 0, with_earlier_blocks, diagonal_only, carry)


def _sb(main3d, vt3d):
    b, s, _ = main3d.shape
    nq = s // TQ
    w = HEADS_PER_STEP * HEAD_DIM
    qb, kb, vb = COL_QA // w, COL_KA // w, ROW_VA // w
    return pl.pallas_call(
        _sb_kernel,
        grid=(b, SB_HEADS // HEADS_PER_STEP, nq),
        in_specs=[
            pl.BlockSpec((1, TQ, w), lambda bi, hg, i: (bi, i, qb + hg)),
            pl.BlockSpec((1, s, w), lambda bi, hg, i: (bi, 0, kb + hg)),
            pl.BlockSpec((1, w, s), lambda bi, hg, i: (bi, vb + hg, 0)),
        ],
        out_specs=pl.BlockSpec((1, TQ, w), lambda bi, hg, i: (bi, i, hg)),
        out_shape=jax.ShapeDtypeStruct((b, s, BRANCH_WIDTH), MXU_DTYPE),
        scratch_shapes=[pltpu.VMEM((HEADS_PER_STEP, TK, TQ), F32), pltpu.VMEM((HEADS_PER_STEP, TK, TQ), F32),
                        pltpu.VMEM((HEADS_PER_STEP, HEAD_DIM, TQ), F32)],
        compiler_params=_cparams(("parallel", "parallel", "arbitrary")),
        name="sb",
    )(main3d, main3d, vt3d)


def _fox_kernel(q_ref, k_ref, vt_ref, c_ref, o_ref, cb_ref, za_ref, zb_ref, acc_ref):
    hg = pl.program_id(1)
    i = pl.program_id(2)
    n_heads = HEADS_PER_STEP

    @pl.when(i == 0)
    def _():
        c = c_ref[0]
        lane = lax.broadcasted_iota(jnp.int32, c.shape, 1)
        for h in range(n_heads):
            if n_heads == FOX_HEADS:
                col = c[:, h:h + 1]
            else:
                col = jnp.sum(jnp.where(lane == hg * n_heads + h, c, 0.0), axis=1, keepdims=True)
            cb_ref[h] = jnp.broadcast_to(-col, c.shape)

    heads = []
    for h in range(n_heads):
        t = h // 2
        qh = _head_q(q_ref[0][:, t * LANES:(t + 1) * LANES], h % 2)
        k_at = lambda rows, t=t: k_ref[0, rows, t * LANES:(t + 1) * LANES]
        vt_at = lambda rows, h=h: vt_ref[0, h * HEAD_DIM:(h + 1) * HEAD_DIM, rows]
        bias_at = lambda rows, h=h: [cb_ref[h, rows, :]] * (TQ // LANES)
        heads.append((qh, k_at, vt_at, bias_at))
    _flash_heads(heads, i, _causal_tile_t(False), (za_ref, zb_ref), acc_ref, o_ref)


def _fox(main3d, vt3d, c3d):
    b, s, _ = main3d.shape
    nq = s // TQ
    w = HEADS_PER_STEP * HEAD_DIM
    qb, kb, vb = COL_QB // w, COL_KB // w, ROW_VB // w
    return pl.pallas_call(
        _fox_kernel,
        grid=(b, FOX_HEADS // HEADS_PER_STEP, nq),
        in_specs=[
            pl.BlockSpec((1, TQ, w), lambda bi, hg, i: (bi, i, qb + hg)),
            pl.BlockSpec((1, s, w), lambda bi, hg, i: (bi, 0, kb + hg)),
            pl.BlockSpec((1, w, s), lambda bi, hg, i: (bi, vb + hg, 0)),
            pl.BlockSpec((1, s, LANES), lambda bi, hg, i: (bi, 0, 0)),
        ],
        out_specs=pl.BlockSpec((1, TQ, w), lambda bi, hg, i: (bi, i, hg)),
        out_shape=jax.ShapeDtypeStruct((b, s, BRANCH_WIDTH), MXU_DTYPE),
        scratch_shapes=[pltpu.VMEM((HEADS_PER_STEP, s, LANES), F32),
                        pltpu.VMEM((HEADS_PER_STEP, TK + SUBLANES, TQ), F32),
                        pltpu.VMEM((HEADS_PER_STEP, TK + SUBLANES, TQ), F32),
                        pltpu.VMEM((HEADS_PER_STEP, HEAD_DIM + SUM_ROWS, TQ), F32)],
        compiler_params=_cparams(("parallel", "parallel", "arbitrary")),
        name="fox",
    )(main3d, main3d, vt3d, c3d)


def _dsa_select(i, qi_ref, ki_ref, misc_ref, key_ref, dig_ref, bias_ref, top_k):
    diag = _causal_tile_t(False)

    wi_t = misc_ref[0].T[MISC_WI:MISC_WI + IDX_HEADS, :] * IDX_W_SCALE
    qih = [_head_q(qi_ref[0][:, (h // 2) * LANES:(h // 2 + 1) * LANES], h % 2) for h in range(IDX_HEADS)]

    def score_block(j, mask):
        kij = ki_ref[0, _rows(j), :]
        sc = None
        for h in range(IDX_HEADS):
            term = wi_t[h:h + 1, :] * jnp.maximum(_dot_nt(kij, qih[h]), 0.0)
            sc = term if sc is None else sc + term
        if mask is not None:
            sc = jnp.where(mask, sc, -jnp.inf)
        bits = lax.bitcast_convert_type(jnp.where(sc == 0.0, 0.0, sc), jnp.int32)
        key_ref[_rows(j), :] = bits ^ ((bits >> 31) & jnp.int32(0x7FFFFFFF))

    def score_pair(t, _):
        score_block(2 * t, None)
        score_block(2 * t + 1, None)
        return 0

    def score_odd_tail():
        score_block(i - 1, None)
        score_block(i, diag)
        return 0

    def score_even_tail():
        score_block(i, diag)
        return 0

    lax.fori_loop(0, i // 2, score_pair, 0)
    lax.cond(i % 2 == 1, score_odd_tail, score_even_tail)

    small = MXU_DTYPE
    group = 2 * SUBLANES

    assert (key_ref.shape[0] // group) <= 256

    def count_digits_ge(cand):
        def hits(j):
            hit = jnp.where(dig_ref[_rows(j), :] >= cand, jnp.ones((), small), jnp.zeros((), small))
            part = hit[0:group]
            for g in range(1, TK // group):
                part = part + hit[g * group:(g + 1) * group]
            return part

        acc = lax.fori_loop(0, (i + 1) // 2, lambda t, a: a + (hits(2 * t) + hits(2 * t + 1)),
                            jnp.zeros((group, TQ), small))
        acc = lax.cond((i + 1) % 2 == 1, lambda a: a + hits(i), lambda a: a, acc)
        return jnp.sum(acc.astype(F32), axis=0, keepdims=True)

    above = jnp.zeros((1, TQ), F32)
    prefix = None
    for shift in (24, 16, 8, 0):
        def fill(j, _, shift=shift, prefix=prefix):
            key = key_ref[_rows(j), :]
            if prefix is None:
                digit = (key >> shift) + 128
            else:
                digit = (key ^ (prefix << (shift + 8))) >> shift
                shares = lax.bitcast_convert_type(digit, jnp.uint32) < jnp.uint32(256)
                digit = jnp.where(shares, digit, -1)
            dig_ref[_rows(j), :] = digit.astype(F32).astype(small)
            return 0

        lax.fori_loop(0, i + 1, fill, 0)

        def search(_, st, above=above):
            lo, hi, n_hi = st
            mid = (lo + hi) * 0.5
            n_mid = count_digits_ge(mid.astype(small))
            ok = above + n_mid >= top_k
            return jnp.where(ok, mid, lo), jnp.where(ok, hi, mid), jnp.where(ok, n_hi, n_mid)

        start = (jnp.zeros((1, TQ), F32), jnp.full((1, TQ), 256.0, F32), jnp.zeros((1, TQ), F32))
        digit, _, n_hi = lax.fori_loop(0, 8, search, start)
        above = above + n_hi
        digit = digit.astype(jnp.int32)
        prefix = digit - 128 if prefix is None else (prefix << 8) | digit
    thr_key = prefix

    need = top_k - above
    kr = lax.broadcasted_iota(jnp.int32, (TK, TK), 0)
    kc = lax.broadcasted_iota(jnp.int32, (TK, TK), 1)
    upto = (kc <= kr).astype(MXU_DTYPE)

    def bias_block(j, mask, seen):
        key = key_ref[_rows(j), :]
        tie = key == thr_key
        rank = _dot(upto, jnp.where(tie, 1.0, 0.0).astype(MXU_DTYPE)) + seen
        sel = (key > thr_key) | (tie & (rank <= need))
        if mask is not None:
            sel = sel & mask
        bias_ref[_rows(j), :] = jnp.where(sel, 0.0, NEG_BIG)
        return rank[TK - 1:TK, :]

    def bias_pair(t, seen):
        return bias_block(2 * t + 1, None, bias_block(2 * t, None, seen))

    def bias_odd_tail(seen):
        bias_block(i, diag, bias_block(i - 1, None, seen))
        return 0

    def bias_even_tail(seen):
        bias_block(i, diag, seen)
        return 0

    seen = lax.fori_loop(0, i // 2, bias_pair, jnp.zeros((1, TQ), F32))
    lax.cond(i % 2 == 1, bias_odd_tail, bias_even_tail, seen)


def _dsa_kernel(q_ref, k_ref, vt_ref, qi_ref, ki_ref, misc_ref, o_ref, key_ref, dig_ref, bias_ref,
                za_ref, zb_ref, acc_ref, *, top_k):
    i = pl.program_id(1)

    def every_causal_key():
        def earlier_block(j, _):
            bias_ref[_rows(j), :] = jnp.zeros((TK, TQ), F32)
            return 0

        lax.fori_loop(0, i, earlier_block, 0)
        bias_ref[_rows(i), :] = jnp.where(_causal_tile_t(False), 0.0, NEG_BIG)
        return 0

    def searched_keys():
        _dsa_select(i, qi_ref, ki_ref, misc_ref, key_ref, dig_ref, bias_ref, top_k)
        return 0

    lax.cond((i + 1) * TQ <= top_k, every_causal_key, searched_keys)

    bias_at = lambda rows: bias_ref[rows, :]
    heads = []
    for h in range(DSA_HEADS):
        g = h // DSA_GROUP
        k_at = lambda rows, g=g: k_ref[0, rows, g * LANES:(g + 1) * LANES]
        vt_at = lambda rows, g=g: vt_ref[0, g * HEAD_DIM:(g + 1) * HEAD_DIM, rows]
        qh = _head_q(q_ref[0][:, (h // 2) * LANES:(h // 2 + 1) * LANES], h % 2)
        heads.append((qh, k_at, vt_at, bias_at))
    _flash_heads(heads, i, None, (za_ref, zb_ref), acc_ref, o_ref)


def _dsa(main3d, vt3d, misc3d, top_k):
    b, s, _ = main3d.shape
    nq = s // TQ
    assert top_k <= TK, "the first key block must hold at least top_k keys"
    return pl.pallas_call(
        functools.partial(_dsa_kernel, top_k=top_k),
        grid=(b, nq),
        in_specs=[
            pl.BlockSpec((1, TQ, 4 * LANES), lambda bi, i: (bi, i, COL_QC // (4 * LANES))),
            pl.BlockSpec((1, s, 2 * LANES), lambda bi, i: (bi, 0, COL_KC // (2 * LANES))),
            pl.BlockSpec((1, LANES, s), lambda bi, i: (bi, ROW_VC // LANES, 0)),
            pl.BlockSpec((1, TQ, 2 * LANES), lambda bi, i: (bi, i, COL_QI // (2 * LANES))),
            pl.BlockSpec((1, s, LANES), lambda bi, i: (bi, 0, COL_KI // LANES)),
            pl.BlockSpec((1, TQ, LANES), lambda bi, i: (bi, i, 0)),
        ],
        out_specs=pl.BlockSpec((1, TQ, BRANCH_WIDTH), lambda bi, i: (bi, i, 0)),
        out_shape=jax.ShapeDtypeStruct((b, s, BRANCH_WIDTH), MXU_DTYPE),
        scratch_shapes=[pltpu.VMEM((s, TQ), jnp.int32), pltpu.VMEM((s, TQ), MXU_DTYPE), pltpu.VMEM((s, TQ), F32),
                        pltpu.VMEM((DSA_HEADS, TK + SUBLANES, TQ), F32),
                        pltpu.VMEM((DSA_HEADS, TK + SUBLANES, TQ), F32),
                        pltpu.VMEM((DSA_HEADS, HEAD_DIM + SUM_ROWS, TQ), F32)],
        compiler_params=_cparams(("parallel", "arbitrary")),
        name="dsa",
    )(main3d, main3d, vt3d, main3d, main3d, misc3d)


def _merge_kernel(x_ref, a_ref, b_ref, c_ref, g_ref, wg_ref, wb_ref, wo_ref, o_ref):
    x = x_ref[...]
    d = x.shape[1]
    h = _rmsnorm(x, g_ref[...]).astype(MXU_DTYPE)
    y = None
    for n, br_ref in enumerate((a_ref, b_ref, c_ref)):
        gate = jax.nn.sigmoid(_dot(h, wg_ref[:, n * d:(n + 1) * d]))
        term = gate * _dot(br_ref[...], wb_ref[n])
        y = term if y is None else y + term
    o_ref[...] = x + _dot(y.astype(MXU_DTYPE), wo_ref[...])


def _merge(x2d, br_a, br_b, br_c, g, w_gate, w_branch, w_out, tm):
    n, d = x2d.shape
    bw = br_a.shape[1]
    row = lambda i: (i, 0)
    const = lambda i: (0, 0)
    return pl.pallas_call(
        _merge_kernel,
        grid=(n // tm,),
        in_specs=[
            pl.BlockSpec((tm, d), row),
            pl.BlockSpec((tm, bw), row),
            pl.BlockSpec((tm, bw), row),
            pl.BlockSpec((tm, bw), row),
            pl.BlockSpec((1, d), const),
            pl.BlockSpec((d, N_BRANCH * d), const),
            pl.BlockSpec((N_BRANCH, bw, d), lambda i: (0, 0, 0)),
            pl.BlockSpec((d, d), const),
        ],
        out_specs=pl.BlockSpec((tm, d), row),
        out_shape=jax.ShapeDtypeStruct((n, d), F32),
        compiler_params=_cparams(("parallel",)),
        name="merge",
    )(x2d, br_a, br_b, br_c, g, w_gate, w_branch, w_out)


def _mlp_kernel(x_ref, g_ref, wu_ref, wd_ref, gf_ref, o_ref, *, ff_chunk, final_norm):
    x = x_ref[...]
    h = _rmsnorm(x, g_ref[...]).astype(MXU_DTYPE)
    acc = x
    for c in range(wu_ref.shape[1] // ff_chunk):
        u = jnp.maximum(_dot(h, wu_ref[:, c * ff_chunk:(c + 1) * ff_chunk]), 0.0)
        acc = acc + _dot((u * u).astype(MXU_DTYPE), wd_ref[c * ff_chunk:(c + 1) * ff_chunk, :])
    if final_norm:
        acc = _rmsnorm(acc, gf_ref[...])
    o_ref[...] = acc


def _mlp(x2d, g, w_up, w_down, g_final, tm, final_norm):
    n, d = x2d.shape
    ff = w_up.shape[1]
    row = lambda i: (i, 0)
    const = lambda i: (0, 0)
    return pl.pallas_call(
        functools.partial(_mlp_kernel, ff_chunk=min(ff, 1024), final_norm=final_norm),
        grid=(n // tm,),
        in_specs=[
            pl.BlockSpec((tm, d), row),
            pl.BlockSpec((1, d), const),
            pl.BlockSpec((d, ff), const),
            pl.BlockSpec((ff, d), const),
            pl.BlockSpec((1, d), const),
        ],
        out_specs=pl.BlockSpec((tm, d), row),
        out_shape=jax.ShapeDtypeStruct((n, d), F32),
        compiler_params=_cparams(("parallel",)),
        name="mlp",
    )(x2d, g, w_up, w_down, g_final)


def _w_in_columns(d_model):
    sizes = (("qa", BRANCH_WIDTH), ("ka", BRANCH_WIDTH), ("va", BRANCH_WIDTH),
             ("qb", BRANCH_WIDTH), ("kb", BRANCH_WIDTH), ("vb", BRANCH_WIDTH), ("fl", FOX_HEADS),
             ("qc", DSA_HEADS * HEAD_DIM), ("kc", DSA_KV_HEADS * HEAD_DIM), ("vc", DSA_KV_HEADS * HEAD_DIM),
             ("qi", IDX_HEADS * IDX_DIM), ("ki", IDX_DIM), ("wi", IDX_HEADS), ("gates", N_BRANCH * d_model))
    cols, o = {}, 0
    for name, width in sizes:
        cols[name] = (o, o + width)
        o += width
    return cols, o


def _pack_kernel(w_ref, main_ref, vals_ref, gates_ref):
    cols, _ = _w_in_columns(gates_ref.shape[2] // N_BRANCH)

    def src(name, lo=0, hi=None):
        a, b = cols[name]
        return w_ref[0, :, a + lo:(b if hi is None else a + hi)]

    def put(ref, start, value):
        ref[0, :, start:start + value.shape[1]] = value.astype(ref.dtype)

    put(main_ref, COL_QA, src("qa"))
    put(main_ref, COL_KA, src("ka"))
    put(main_ref, COL_QB, src("qb"))
    put(main_ref, COL_KB, src("kb"))
    put(main_ref, COL_QC, src("qc"))
    for g in range(DSA_KV_HEADS):
        kg = src("kc", g * HEAD_DIM, (g + 1) * HEAD_DIM)
        put(main_ref, COL_KC + g * LANES, jnp.concatenate([kg, kg], axis=1))
    put(main_ref, COL_QI, src("qi"))
    ki = src("ki")
    put(main_ref, COL_KI, jnp.concatenate([ki, ki], axis=1))
    put(vals_ref, ROW_VA, src("va"))
    put(vals_ref, ROW_VB, src("vb"))
    put(vals_ref, ROW_VC, src("vc"))
    pad = jnp.zeros((w_ref.shape[1], LANES - FOX_HEADS - IDX_HEADS), F32)
    put(vals_ref, VT_ROWS, jnp.concatenate([src("fl"), src("wi"), pad], axis=1))
    put(gates_ref, 0, src("gates"))


def _pack_w_in(w_in):
    depth, d, total = w_in.shape
    _, end = _w_in_columns(d)
    assert end == total
    rows = 256
    blk = lambda width: pl.BlockSpec((1, rows, width), lambda l, r: (l, r, 0))
    widths = (MAIN_COLS, VALS_COLS, N_BRANCH * d)
    return pl.pallas_call(
        _pack_kernel,
        grid=(depth, d // rows),
        in_specs=[blk(total)],
        out_specs=[blk(wd) for wd in widths],
        out_shape=[jax.ShapeDtypeStruct((depth, d, wd), MXU_DTYPE) for wd in widths],
        compiler_params=_cparams(("parallel", "parallel")),
        name="pack",
    )(w_in)


def _rope_rows():
    lane = jnp.arange(LANES) % HEAD_DIM
    half = ROT_DIM // 2
    inv_freq = ROPE_THETA ** (-jnp.arange(0, ROT_DIM, 2, dtype=F32) / ROT_DIM)
    freq = jnp.where(lane < ROT_DIM, inv_freq[lane % half], 0.0).astype(F32)
    sg1 = jnp.where(lane < half, -1.0, 0.0).astype(F32)
    sg2 = jnp.where((lane >= half) & (lane < ROT_DIM), 1.0, 0.0).astype(F32)
    return freq[None, :], sg1[None, :], sg2[None, :]


def kernel(x, positions, g_mix, w_in, b_forget, w_branch, w_out, g_mlp, w_up, w_down, g_final):
    b, s, d = x.shape
    n = b * s
    depth = w_in.shape[0]
    top_k = min(TOPK_MAX, s // 4)
    tm = min(512, s)
    freq, sg1, sg2 = _rope_rows()
    pos2d = positions.reshape(n, 1)
    x2d = x.reshape(n, d)
    w_main_all, w_v_all, w_gate_all = _pack_w_in(w_in)
    for layer in range(depth):
        w_main, w_v, w_gate = w_main_all[layer], w_v_all[layer], w_gate_all[layer]
        bias_row = jnp.concatenate([b_forget[layer], jnp.zeros((LANES - FOX_HEADS,), F32)])[None, :]
        main, vt, misc = _proj(x2d, pos2d, g_mix[layer][None, :], freq, sg1, sg2, w_main, w_v, tm, b, s)
        main3d = main.reshape(b, s, MAIN_COLS)
        misc3d = misc.reshape(b, s, LANES)
        cum = _cumf(misc3d, bias_row)
        br_a = _sb(main3d, vt).reshape(n, BRANCH_WIDTH)
        br_b = _fox(main3d, vt, cum).reshape(n, BRANCH_WIDTH)
        br_c = _dsa(main3d, vt, misc3d, top_k).reshape(n, BRANCH_WIDTH)
        x2d = _merge(x2d, br_a, br_b, br_c, g_mix[layer][None, :], w_gate,
                     w_branch[layer].astype(MXU_DTYPE), w_out[layer].astype(MXU_DTYPE), tm)
        x2d = _mlp(x2d, g_mlp[layer][None, :], w_up[layer].astype(MXU_DTYPE),
                   w_down[layer].astype(MXU_DTYPE), g_final[None, :], tm, layer == depth - 1)
    return x2d.reshape(b, s, d)
```

```python
import functools

import jax
import jax.numpy as jnp
from jax import lax
from jax.experimental import pallas as pl
from jax.experimental.pallas import tpu as pltpu

F32 = jnp.float32
MXU_DTYPE = jnp.bfloat16

HEAD_DIM = 64
SB_HEADS = 8
FOX_HEADS = 8
DSA_HEADS = 8
DSA_KV_HEADS = 2
DSA_GROUP = DSA_HEADS // DSA_KV_HEADS
IDX_HEADS = 4
IDX_DIM = 64
N_BRANCH = 3
ROPE_THETA = 500000.0
ROT_DIM = HEAD_DIM // 4
TOPK_MAX = 256
NORM_EPS = 1e-6

LANES = 128
SUBLANES = 8
SUM_ROWS = 2 * SUBLANES
BRANCH_WIDTH = SB_HEADS * HEAD_DIM
ATTN_SCALE = HEAD_DIM ** -0.5
assert IDX_DIM == HEAD_DIM
IDX_W_SCALE = IDX_HEADS ** -0.5

COL_QA, COL_KA = 0, 512
COL_QB, COL_KB = 1024, 1536
COL_QC = 2048
COL_KC = 2560
COL_QI = 2816
COL_KI = 3072
MAIN_COLS = 3200
ROPE_START = COL_QC
ROW_VA, ROW_VB, ROW_VC = 0, 512, 1024
VT_ROWS = 1152
VALS_COLS = VT_ROWS + LANES
MISC_WI = FOX_HEADS

TQ = 256
TK = 256
HEADS_PER_STEP = 8
NEG_BIG = -1e30

MIB = 1024 * 1024
VMEM_MIB = {"pack": 40, "proj": 40, "cumf": 16, "sb": 24, "fox": 40, "dsa": 32, "merge": 40, "mlp": 40}


def _cparams(sem, name):
    return pltpu.CompilerParams(dimension_semantics=sem, vmem_limit_bytes=VMEM_MIB[name] * MIB)


def _rmsnorm(x, g):
    y = x * lax.rsqrt(jnp.mean(x * x, axis=-1, keepdims=True) + NORM_EPS)
    return y * g


def _dot(a, b):
    return jnp.dot(a, b, preferred_element_type=F32)


def _dot_nt(a, b):
    return lax.dot_general(a, b, (((1,), (1,)), ((), ())), preferred_element_type=F32)


def _split3(x):
    hi = x.astype(MXU_DTYPE)
    r = x - hi.astype(F32)
    mid = r.astype(MXU_DTYPE)
    lo = (r - mid.astype(F32)).astype(MXU_DTYPE)
    return hi, mid, lo


def _log_sigmoid_parts(z):
    neg_abs = lax.bitcast_convert_type(lax.bitcast_convert_type(z, jnp.int32) | jnp.int32(-2 ** 31), F32)
    sp = jnp.log(1.0 + jnp.exp(neg_abs))
    log_sig = jnp.minimum(z, 0.0) - sp
    return log_sig, z - log_sig


def _proj_kernel(x_ref, pos_ref, g_ref, freq_ref, sg1_ref, sg2_ref, wm_ref, wv_ref,
                 main_ref, vt_ref, misc_ref):
    h = _rmsnorm(x_ref[...], g_ref[...]).astype(MXU_DTYPE)
    ang = pos_ref[...].astype(F32) * freq_ref[...]
    cos = jnp.cos(ang)
    sin = jnp.sin(ang)
    s_up = sin * sg1_ref[...]
    s_dn = sin * sg2_ref[...]
    half = ROT_DIM // 2
    chunk = 4 * LANES
    for c0 in range(0, VALS_COLS, 2 * LANES):
        pv = _dot(h, wv_ref[:, c0:c0 + 2 * LANES])
        for r0 in (c0, c0 + LANES):
            piece = pv[:, r0 - c0:r0 - c0 + LANES]
            if r0 < VT_ROWS:
                vt_ref[0, r0:r0 + LANES, :] = piece.T.astype(vt_ref.dtype)
            else:
                misc_ref[...] = piece
    for c0 in list(range(ROPE_START, MAIN_COLS, chunk)) + list(range(0, ROPE_START, chunk)):
        c1 = min(c0 + chunk, MAIN_COLS)
        p = _dot(h, wm_ref[:, c0:c1])
        if c0 >= ROPE_START:
            for t in range((c1 - c0) // LANES):
                pt = p[:, t * LANES:(t + 1) * LANES]
                pt = (pt * cos + pltpu.roll(pt, LANES - half, 1) * s_up
                      + pltpu.roll(pt, half, 1) * s_dn)
                main_ref[:, c0 + t * LANES:c0 + (t + 1) * LANES] = pt.astype(main_ref.dtype)
        else:
            main_ref[:, c0:c1] = p.astype(main_ref.dtype)


def _proj(x2d, pos2d, g, freq, sg1, sg2, w_main, w_v, tm, batch, seq):
    n, d = x2d.shape
    nsb = seq // tm
    const = lambda i: (0, 0)
    return pl.pallas_call(
        _proj_kernel,
        grid=(n // tm,),
        in_specs=[
            pl.BlockSpec((tm, d), lambda i: (i, 0)),
            pl.BlockSpec((tm, 1), lambda i: (i, 0)),
            pl.BlockSpec((1, d), const),
            pl.BlockSpec((1, LANES), const),
            pl.BlockSpec((1, LANES), const),
            pl.BlockSpec((1, LANES), const),
            pl.BlockSpec((d, MAIN_COLS), const),
            pl.BlockSpec((d, VALS_COLS), const),
        ],
        out_specs=[
            pl.BlockSpec((tm, MAIN_COLS), lambda i: (i, 0)),
            pl.BlockSpec((1, VT_ROWS, tm), lambda i: (i // nsb, 0, i % nsb)),
            pl.BlockSpec((tm, LANES), lambda i: (i, 0)),
        ],
        out_shape=[
            jax.ShapeDtypeStruct((n, MAIN_COLS), MXU_DTYPE),
            jax.ShapeDtypeStruct((batch, VT_ROWS, seq), MXU_DTYPE),
            jax.ShapeDtypeStruct((n, LANES), F32),
        ],
        compiler_params=_cparams(("parallel",), "proj"),
        name="proj",
    )(x2d, pos2d, g, freq, sg1, sg2, w_main, w_v)


def _cumf_kernel(misc_ref, bias_ref, c_ref):
    s = misc_ref.shape[1]
    r = lax.broadcasted_iota(jnp.int32, (TK, TK), 0)
    c = lax.broadcasted_iota(jnp.int32, (TK, TK), 1)
    incl = (c <= r).astype(MXU_DTYPE)
    carry = jnp.zeros((1, LANES), F32)
    for b in range(s // TK):
        logit = misc_ref[0, b * TK:(b + 1) * TK, :] + bias_ref[...]
        logf, _ = _log_sigmoid_parts(logit)
        hi, mid, lo = _split3(logf)
        cs = (_dot(incl, lo) + _dot(incl, mid)) + _dot(incl, hi) + carry
        c_ref[0, b * TK:(b + 1) * TK, :] = cs
        carry = cs[TK - 1:TK, :]


def _cumf(misc3d, bias_row):
    b, s, _ = misc3d.shape
    return pl.pallas_call(
        _cumf_kernel,
        grid=(b,),
        in_specs=[
            pl.BlockSpec((1, s, LANES), lambda i: (i, 0, 0)),
            pl.BlockSpec((1, LANES), lambda i: (0, 0)),
        ],
        out_specs=pl.BlockSpec((1, s, LANES), lambda i: (i, 0, 0)),
        out_shape=jax.ShapeDtypeStruct((b, s, LANES), F32),
        compiler_params=_cparams(("parallel",), "cumf"),
        name="cumf",
    )(misc3d, bias_row)


def _head_q(q_tile, half):
    lane = lax.broadcasted_iota(jnp.int32, q_tile.shape, 1)
    keep = (lane // HEAD_DIM) == half
    return jnp.where(keep, q_tile.astype(F32) * ATTN_SCALE, 0.0).astype(MXU_DTYPE)


def _causal_tile_t(strict):
    key = lax.broadcasted_iota(jnp.int32, (TK, TQ), 0)
    qry = lax.broadcasted_iota(jnp.int32, (TK, TQ), 1)
    return (key < qry) if strict else (key <= qry)


def _rows(j):
    return pl.ds(pl.multiple_of(j * TK, TK), TK)


def _lookahead_loop(n_last, bufs, issue, step, carry, finish, primed=False, fused=None):
    buf_a, buf_b = bufs

    def ahead_and_step(n, buf, other, carry):
        if fused is not None:
            return fused(n, buf, other, carry)
        issue(n + 1, other)
        return step(n, buf, carry, False)

    def pair(t, carry):
        n = 2 * t
        carry = ahead_and_step(n, buf_a, buf_b, carry)
        return ahead_and_step(n + 1, buf_b, buf_a, carry)

    def odd_tail(carry):
        carry = ahead_and_step(n_last - 1, buf_a, buf_b, carry)
        step(n_last, buf_b, carry, True)
        finish()
        return 0

    def even_tail(carry):
        step(n_last, buf_a, carry, True)
        finish()
        return 0

    if not primed:
        issue(0, buf_a)
    carry = lax.fori_loop(0, n_last // 2, pair, carry)
    lax.cond(n_last % 2 == 1, odd_tail, even_tail, carry)


def _flash_heads(heads, n_full, diag_mask, z_bufs, acc_ref, o_ref):
    half_k, half_q = TK // 2, TQ // 2

    def add_bias(s, bias):
        if isinstance(bias, (list, tuple)):
            return jnp.concatenate([s[:, t * LANES:(t + 1) * LANES] + bias[t] for t in range(len(bias))],
                                   axis=1)
        return s + bias

    ones = jnp.ones((SUM_ROWS, TK), MXU_DTYPE)

    def issue_head(h, j, buf):
        rows = _rows(j)
        qh, k_at, _, bias_at = heads[h]
        s = add_bias(_dot_nt(k_at(rows), qh), bias_at(rows))
        buf[h, :TK, :] = s
        buf[h, TK:, :] = jnp.max(s.reshape(TK // SUBLANES, SUBLANES, TQ), axis=0)

    def issue(j, buf):
        for h in range(len(heads)):
            issue_head(h, j, buf)

    def step_and_issue(j, buf, other, maxes):
        rows = _rows(j)
        out = []
        for h, ((_, _, vt_at, _), m) in enumerate(zip(heads, maxes)):
            issue_head(h, j + 1, other)
            m_new = jnp.maximum(m, jnp.max(buf[h, TK:, :], axis=0, keepdims=True))
            p = jnp.exp(buf[h, :TK, :] - m_new).astype(MXU_DTYPE)
            vt1 = jnp.concatenate([vt_at(rows), ones], axis=0)
            acc_ref[h] = jnp.exp(m - m_new) * acc_ref[h] + _dot(vt1, p)
            out.append(m_new)
        return tuple(out)

    def update(tiles, rows, maxes, lanes):
        stats = []
        for s, m in zip(tiles, maxes):
            m_new = jnp.maximum(m, jnp.max(s, axis=0, keepdims=True))
            stats.append((m_new, jnp.exp(m - m_new), jnp.exp(s - m_new).astype(MXU_DTYPE)))
        for h, ((_, alpha, p), (_, _, vt_at, _)) in enumerate(zip(stats, heads)):
            vt1 = jnp.concatenate([vt_at(rows), ones[:, :p.shape[0]]], axis=0)
            acc_ref[h, :, lanes] = alpha * acc_ref[h, :, lanes] + _dot(vt1, p)
        return tuple(m_new for m_new, _, _ in stats)

    def diagonal_step(j, buf, maxes):
        start = pl.multiple_of(j * TK, TK)
        early, late = pl.ds(start, half_k), pl.ds(pl.multiple_of(start + half_k, half_k), half_k)

        def masked(s, mask):
            return s if diag_mask is None else jnp.where(mask, s, NEG_BIG)

        tiles = [masked(buf[h, :half_k, :], None if diag_mask is None else diag_mask[:half_k, :])
                 for h in range(len(heads))]
        maxes = update(tiles, early, maxes, slice(None))
        tiles = [masked(buf[h, half_k:TK, half_q:], None if diag_mask is None else diag_mask[half_k:, half_q:])
                 for h in range(len(heads))]
        update(tiles, late, tuple(m[:, half_q:] for m in maxes), slice(half_q, TQ))
        return maxes

    def finish():
        _store_heads_t(o_ref, [acc_ref[h, :HEAD_DIM, :] / acc_ref[h, HEAD_DIM:HEAD_DIM + 1, :]
                               for h in range(len(heads))])

    acc_ref[...] = jnp.zeros(acc_ref.shape, F32)
    init = tuple(jnp.full((1, TQ), NEG_BIG, F32) for _ in heads)
    _lookahead_loop(n_full, z_bufs, issue, lambda j, buf, maxes, last: diagonal_step(j, buf, maxes), init,
                    finish, fused=step_and_issue)


def _store_heads_t(o_ref, outs_t):
    for t in range(len(outs_t) // 2):
        pair_t = jnp.concatenate([outs_t[2 * t], outs_t[2 * t + 1]], axis=0)
        o_ref[0, :, t * LANES:(t + 1) * LANES] = pair_t.T.astype(o_ref.dtype)


def _sb_kernel(q_ref, k_ref, vt_ref, o_ref, za_ref, zb_ref, acc_ref):
    i = pl.program_id(2)
    r = lax.broadcasted_iota(jnp.int32, (TK, TK), 0)
    c = lax.broadcasted_iota(jnp.int32, (TK, TK), 1)
    after = (c > r).astype(MXU_DTYPE)
    strict = _causal_tile_t(True)
    n_heads = HEADS_PER_STEP
    qhs = [_head_q(q_ref[0][:, (h // 2) * LANES:(h // 2 + 1) * LANES], h % 2) for h in range(n_heads)]

    def issue(j, buf):
        for h in range(n_heads):
            buf[h] = _dot_nt(k_ref[0, _rows(j), (h // 2) * LANES:(h // 2 + 1) * LANES], qhs[h])

    def update(tiles, rows, tails, lanes, mask):
        n_rows = tiles[0].shape[0]
        parts = []
        for z in tiles:
            log_beta, neg_l1m = _log_sigmoid_parts(z)
            if mask is not None:
                neg_l1m = jnp.where(mask, neg_l1m, 0.0)
            parts.append((log_beta, neg_l1m[0:1, :], neg_l1m.astype(MXU_DTYPE)))
        after_rows = after[:n_rows, :n_rows]
        laters = [_dot(after_rows, hl) for (_, _, hl) in parts]
        ws = []
        for (log_beta, _, _), later in zip(parts, laters):
            w = jnp.exp(log_beta - later)
            if mask is not None:
                w = jnp.where(mask, w, 0.0)
            ws.append(w.astype(MXU_DTYPE))
        out = []
        for h, (w, (_, first, _), later, tail) in enumerate(zip(ws, parts, laters, tails)):
            vtj = vt_ref[0, h * HEAD_DIM:(h + 1) * HEAD_DIM, rows]
            acc_ref[h, :, lanes] = acc_ref[h, :, lanes] + jnp.exp(-tail) * _dot(vtj, w)
            out.append(tail + (first + later[0:1, :]))
        return tuple(out)

    def step(j, buf, tails):
        return update([buf[h] for h in range(n_heads)], _rows(j), tails, slice(None), None)

    def diagonal_step(j, buf, tails):
        half_k, half_q = TK // 2, TQ // 2
        start = pl.multiple_of(j * TK, TK)
        early, late = pl.ds(start, half_k), pl.ds(pl.multiple_of(start + half_k, half_k), half_k)
        late_tails = update([buf[h, half_k:, half_q:] for h in range(n_heads)], late,
                            tuple(t[:, half_q:] for t in tails), slice(half_q, TQ), strict[half_k:, half_q:])
        tails = tuple(jnp.concatenate([t[:, :half_q], lt], axis=1) for t, lt in zip(tails, late_tails))
        return update([buf[h, :half_k, :] for h in range(n_heads)], early, tails, slice(None), strict[:half_k, :])

    acc_ref[...] = jnp.zeros(acc_ref.shape, F32)
    carry = tuple(jnp.zeros((1, TQ), F32) for _ in range(n_heads))
    issue(i, za_ref)

    def finish():
        _store_heads_t(o_ref, [acc_ref[h] for h in range(n_heads)])

    def with_earlier_blocks(cr):
        issue(i - 1, zb_ref)
        cr = diagonal_step(i, za_ref, cr)
        _lookahead_loop(i - 1, (zb_ref, za_ref), lambda n, buf: issue(i - 1 - n, buf),
                        lambda n, buf, c, last: step(i - 1 - n, buf, c), cr, finish, primed=True)
        return 0

    def diagonal_only(cr):
        diagonal_step(i, za_ref, cr)
        finish()
        return 0

    lax.cond(i > 0, with_earlier_blocks, diagonal_only, carry)


def _sb(main3d, vt3d):
    b, s, _ = main3d.shape
    nq = s // TQ
    w = HEADS_PER_STEP * HEAD_DIM
    qb, kb, vb = COL_QA // w, COL_KA // w, ROW_VA // w
    return pl.pallas_call(
        _sb_kernel,
        grid=(b, SB_HEADS // HEADS_PER_STEP, nq),
        in_specs=[
            pl.BlockSpec((1, TQ, w), lambda bi, hg, i: (bi, i, qb + hg)),
            pl.BlockSpec((1, s, w), lambda bi, hg, i: (bi, 0, kb + hg)),
            pl.BlockSpec((1, w, s), lambda bi, hg, i: (bi, vb + hg, 0)),
        ],
        out_specs=pl.BlockSpec((1, TQ, w), lambda bi, hg, i: (bi, i, hg)),
        out_shape=jax.ShapeDtypeStruct((b, s, BRANCH_WIDTH), MXU_DTYPE),
        scratch_shapes=[pltpu.VMEM((HEADS_PER_STEP, TK, TQ), F32), pltpu.VMEM((HEADS_PER_STEP, TK, TQ), F32),
                        pltpu.VMEM((HEADS_PER_STEP, HEAD_DIM, TQ), F32)],
        compiler_params=_cparams(("parallel", "parallel", "arbitrary"), "sb"),
        name="sb",
    )(main3d, main3d, vt3d)


def _fox_kernel(q_ref, k_ref, vt_ref, c_ref, o_ref, cb_ref, za_ref, zb_ref, acc_ref):
    hg = pl.program_id(1)
    i = pl.program_id(2)
    n_heads = HEADS_PER_STEP

    @pl.when(i == 0)
    def _():
        c = c_ref[0]
        lane = lax.broadcasted_iota(jnp.int32, c.shape, 1)
        for h in range(n_heads):
            if n_heads == FOX_HEADS:
                col = c[:, h:h + 1]
            else:
                col = jnp.sum(jnp.where(lane == hg * n_heads + h, c, 0.0), axis=1, keepdims=True)
            cb_ref[h] = jnp.broadcast_to(-col, c.shape)

    heads = []
    for h in range(n_heads):
        t = h // 2
        qh = _head_q(q_ref[0][:, t * LANES:(t + 1) * LANES], h % 2)
        k_at = lambda rows, t=t: k_ref[0, rows, t * LANES:(t + 1) * LANES]
        vt_at = lambda rows, h=h: vt_ref[0, h * HEAD_DIM:(h + 1) * HEAD_DIM, rows]
        bias_at = lambda rows, h=h: [cb_ref[h, rows, :]] * (TQ // LANES)
        heads.append((qh, k_at, vt_at, bias_at))
    _flash_heads(heads, i, _causal_tile_t(False), (za_ref, zb_ref), acc_ref, o_ref)


def _fox(main3d, vt3d, c3d):
    b, s, _ = main3d.shape
    nq = s // TQ
    w = HEADS_PER_STEP * HEAD_DIM
    qb, kb, vb = COL_QB // w, COL_KB // w, ROW_VB // w
    return pl.pallas_call(
        _fox_kernel,
        grid=(b, FOX_HEADS // HEADS_PER_STEP, nq),
        in_specs=[
            pl.BlockSpec((1, TQ, w), lambda bi, hg, i: (bi, i, qb + hg)),
            pl.BlockSpec((1, s, w), lambda bi, hg, i: (bi, 0, kb + hg)),
            pl.BlockSpec((1, w, s), lambda bi, hg, i: (bi, vb + hg, 0)),
            pl.BlockSpec((1, s, LANES), lambda bi, hg, i: (bi, 0, 0)),
        ],
        out_specs=pl.BlockSpec((1, TQ, w), lambda bi, hg, i: (bi, i, hg)),
        out_shape=jax.ShapeDtypeStruct((b, s, BRANCH_WIDTH), MXU_DTYPE),
        scratch_shapes=[pltpu.VMEM((HEADS_PER_STEP, s, LANES), F32),
                        pltpu.VMEM((HEADS_PER_STEP, TK + SUBLANES, TQ), F32),
                        pltpu.VMEM((HEADS_PER_STEP, TK + SUBLANES, TQ), F32),
                        pltpu.VMEM((HEADS_PER_STEP, HEAD_DIM + SUM_ROWS, TQ), F32)],
        compiler_params=_cparams(("parallel", "parallel", "arbitrary"), "fox"),
        name="fox",
    )(main3d, main3d, vt3d, c3d)


def _dsa_select(i, qi_ref, ki_ref, misc_ref, key_ref, dig_ref, bias_ref, top_k):
    diag = _causal_tile_t(False)

    wi_t = misc_ref[0].T[MISC_WI:MISC_WI + IDX_HEADS, :] * IDX_W_SCALE
    qih = [_head_q(qi_ref[0][:, (h // 2) * LANES:(h // 2 + 1) * LANES], h % 2) for h in range(IDX_HEADS)]

    def score_block(j, mask):
        kij = ki_ref[0, _rows(j), :]
        sc = None
        for h in range(IDX_HEADS):
            term = wi_t[h:h + 1, :] * jnp.maximum(_dot_nt(kij, qih[h]), 0.0)
            sc = term if sc is None else sc + term
        if mask is not None:
            sc = jnp.where(mask, sc, -jnp.inf)
        bits = lax.bitcast_convert_type(jnp.where(sc == 0.0, 0.0, sc), jnp.int32)
        key_ref[_rows(j), :] = bits ^ ((bits >> 31) & jnp.int32(0x7FFFFFFF))

    def score_pair(t, _):
        score_block(2 * t, None)
        score_block(2 * t + 1, None)
        return 0

    def score_odd_tail():
        score_block(i - 1, None)
        score_block(i, diag)
        return 0

    def score_even_tail():
        score_block(i, diag)
        return 0

    lax.fori_loop(0, i // 2, score_pair, 0)
    lax.cond(i % 2 == 1, score_odd_tail, score_even_tail)

    small = MXU_DTYPE
    group = 2 * SUBLANES

    assert (key_ref.shape[0] // group) <= 256

    def count_digits_ge(cand):
        def hits(j):
            hit = jnp.where(dig_ref[_rows(j), :] >= cand, jnp.ones((), small), jnp.zeros((), small))
            part = hit[0:group]
            for g in range(1, TK // group):
                part = part + hit[g * group:(g + 1) * group]
            return part

        acc = lax.fori_loop(0, (i + 1) // 2, lambda t, a: a + (hits(2 * t) + hits(2 * t + 1)),
                            jnp.zeros((group, TQ), small))
        acc = lax.cond((i + 1) % 2 == 1, lambda a: a + hits(i), lambda a: a, acc)
        return jnp.sum(acc.astype(F32), axis=0, keepdims=True)

    above = jnp.zeros((1, TQ), F32)
    prefix = None
    for shift in (24, 16, 8, 0):
        def fill(j, _, shift=shift, prefix=prefix):
            key = key_ref[_rows(j), :]
            if prefix is None:
                digit = (key >> shift) + 128
            else:
                digit = (key ^ (prefix << (shift + 8))) >> shift
                shares = lax.bitcast_convert_type(digit, jnp.uint32) < jnp.uint32(256)
                digit = jnp.where(shares, digit, -1)
            dig_ref[_rows(j), :] = digit.astype(F32).astype(small)
            return 0

        lax.fori_loop(0, i + 1, fill, 0)

        def search(_, st, above=above):
            lo, hi, n_hi = st
            mid = (lo + hi) * 0.5
            n_mid = count_digits_ge(mid.astype(small))
            ok = above + n_mid >= top_k
            return jnp.where(ok, mid, lo), jnp.where(ok, hi, mid), jnp.where(ok, n_hi, n_mid)

        start = (jnp.zeros((1, TQ), F32), jnp.full((1, TQ), 256.0, F32), jnp.zeros((1, TQ), F32))
        digit, _, n_hi = lax.fori_loop(0, 8, search, start)
        above = above + n_hi
        digit = digit.astype(jnp.int32)
        prefix = digit - 128 if prefix is None else (prefix << 8) | digit
    thr_key = prefix

    need = top_k - above
    kr = lax.broadcasted_iota(jnp.int32, (TK, TK), 0)
    kc = lax.broadcasted_iota(jnp.int32, (TK, TK), 1)
    upto = (kc <= kr).astype(MXU_DTYPE)

    def bias_block(j, mask, seen):
        key = key_ref[_rows(j), :]
        tie = key == thr_key
        rank = _dot(upto, jnp.where(tie, 1.0, 0.0).astype(MXU_DTYPE)) + seen
        sel = (key > thr_key) | (tie & (rank <= need))
        if mask is not None:
            sel = sel & mask
        bias_ref[_rows(j), :] = jnp.where(sel, 0.0, NEG_BIG)
        return rank[TK - 1:TK, :]

    def bias_pair(t, seen):
        return bias_block(2 * t + 1, None, bias_block(2 * t, None, seen))

    def bias_odd_tail(seen):
        bias_block(i, diag, bias_block(i - 1, None, seen))
        return 0

    def bias_even_tail(seen):
        bias_block(i, diag, seen)
        return 0

    seen = lax.fori_loop(0, i // 2, bias_pair, jnp.zeros((1, TQ), F32))
    lax.cond(i % 2 == 1, bias_odd_tail, bias_even_tail, seen)


def _dsa_kernel(q_ref, k_ref, vt_ref, qi_ref, ki_ref, misc_ref, o_ref, key_ref, dig_ref, bias_ref,
                za_ref, zb_ref, acc_ref, *, top_k):
    i = pl.program_id(1)

    def every_causal_key():
        def earlier_block(j, _):
            bias_ref[_rows(j), :] = jnp.zeros((TK, TQ), F32)
            return 0

        lax.fori_loop(0, i, earlier_block, 0)
        bias_ref[_rows(i), :] = jnp.where(_causal_tile_t(False), 0.0, NEG_BIG)
        return 0

    def searched_keys():
        _dsa_select(i, qi_ref, ki_ref, misc_ref, key_ref, dig_ref, bias_ref, top_k)
        return 0

    lax.cond((i + 1) * TQ <= top_k, every_causal_key, searched_keys)

    bias_at = lambda rows: bias_ref[rows, :]
    heads = []
    for h in range(DSA_HEADS):
        g = h // DSA_GROUP
        k_at = lambda rows, g=g: k_ref[0, rows, g * LANES:(g + 1) * LANES]
        vt_at = lambda rows, g=g: vt_ref[0, g * HEAD_DIM:(g + 1) * HEAD_DIM, rows]
        qh = _head_q(q_ref[0][:, (h // 2) * LANES:(h // 2 + 1) * LANES], h % 2)
        heads.append((qh, k_at, vt_at, bias_at))
    _flash_heads(heads, i, None, (za_ref, zb_ref), acc_ref, o_ref)


def _dsa(main3d, vt3d, misc3d, top_k):
    b, s, _ = main3d.shape
    nq = s // TQ
    assert top_k <= TK, "the first key block must hold at least top_k keys"
    return pl.pallas_call(
        functools.partial(_dsa_kernel, top_k=top_k),
        grid=(b, nq),
        in_specs=[
            pl.BlockSpec((1, TQ, 4 * LANES), lambda bi, i: (bi, i, COL_QC // (4 * LANES))),
            pl.BlockSpec((1, s, 2 * LANES), lambda bi, i: (bi, 0, COL_KC // (2 * LANES))),
            pl.BlockSpec((1, LANES, s), lambda bi, i: (bi, ROW_VC // LANES, 0)),
            pl.BlockSpec((1, TQ, 2 * LANES), lambda bi, i: (bi, i, COL_QI // (2 * LANES))),
            pl.BlockSpec((1, s, LANES), lambda bi, i: (bi, 0, COL_KI // LANES)),
            pl.BlockSpec((1, TQ, LANES), lambda bi, i: (bi, i, 0)),
        ],
        out_specs=pl.BlockSpec((1, TQ, BRANCH_WIDTH), lambda bi, i: (bi, i, 0)),
        out_shape=jax.ShapeDtypeStruct((b, s, BRANCH_WIDTH), MXU_DTYPE),
        scratch_shapes=[pltpu.VMEM((s, TQ), jnp.int32), pltpu.VMEM((s, TQ), MXU_DTYPE), pltpu.VMEM((s, TQ), F32),
                        pltpu.VMEM((DSA_HEADS, TK + SUBLANES, TQ), F32),
                        pltpu.VMEM((DSA_HEADS, TK + SUBLANES, TQ), F32),
                        pltpu.VMEM((DSA_HEADS, HEAD_DIM + SUM_ROWS, TQ), F32)],
        compiler_params=_cparams(("parallel", "arbitrary"), "dsa"),
        name="dsa",
    )(main3d, main3d, vt3d, main3d, main3d, misc3d)


def _merge_kernel(x_ref, a_ref, b_ref, c_ref, g_ref, wg_ref, wb_ref, wo_ref, o_ref):
    x = x_ref[...]
    d = x.shape[1]
    h = _rmsnorm(x, g_ref[...]).astype(MXU_DTYPE)
    y = None
    for n, br_ref in enumerate((a_ref, b_ref, c_ref)):
        gate = jax.nn.sigmoid(_dot(h, wg_ref[:, n * d:(n + 1) * d]))
        term = gate * _dot(br_ref[...], wb_ref[n])
        y = term if y is None else y + term
    o_ref[...] = x + _dot(y.astype(MXU_DTYPE), wo_ref[...])


def _merge(x2d, br_a, br_b, br_c, g, w_gate, w_branch, w_out, tm):
    n, d = x2d.shape
    bw = br_a.shape[1]
    row = lambda i: (i, 0)
    const = lambda i: (0, 0)
    return pl.pallas_call(
        _merge_kernel,
        grid=(n // tm,),
        in_specs=[
            pl.BlockSpec((tm, d), row),
            pl.BlockSpec((tm, bw), row),
            pl.BlockSpec((tm, bw), row),
            pl.BlockSpec((tm, bw), row),
            pl.BlockSpec((1, d), const),
            pl.BlockSpec((d, N_BRANCH * d), const),
            pl.BlockSpec((N_BRANCH, bw, d), lambda i: (0, 0, 0)),
            pl.BlockSpec((d, d), const),
        ],
        out_specs=pl.BlockSpec((tm, d), row),
        out_shape=jax.ShapeDtypeStruct((n, d), F32),
        compiler_params=_cparams(("parallel",), "merge"),
        name="merge",
    )(x2d, br_a, br_b, br_c, g, w_gate, w_branch, w_out)


def _mlp_kernel(x_ref, g_ref, wu_ref, wd_ref, gf_ref, o_ref, *, ff_chunk, final_norm):
    x = x_ref[...]
    h = _rmsnorm(x, g_ref[...]).astype(MXU_DTYPE)
    acc = x
    for c in range(wu_ref.shape[1] // ff_chunk):
        u = jnp.maximum(_dot(h, wu_ref[:, c * ff_chunk:(c + 1) * ff_chunk]), 0.0)
        acc = acc + _dot((u * u).astype(MXU_DTYPE), wd_ref[c * ff_chunk:(c + 1) * ff_chunk, :])
    if final_norm:
        acc = _rmsnorm(acc, gf_ref[...])
    o_ref[...] = acc


def _mlp(x2d, g, w_up, w_down, g_final, tm, final_norm):
    n, d = x2d.shape
    ff = w_up.shape[1]
    row = lambda i: (i, 0)
    const = lambda i: (0, 0)
    return pl.pallas_call(
        functools.partial(_mlp_kernel, ff_chunk=min(ff, 1024), final_norm=final_norm),
        grid=(n // tm,),
        in_specs=[
            pl.BlockSpec((tm, d), row),
            pl.BlockSpec((1, d), const),
            pl.BlockSpec((d, ff), const),
            pl.BlockSpec((ff, d), const),
            pl.BlockSpec((1, d), const),
        ],
        out_specs=pl.BlockSpec((tm, d), row),
        out_shape=jax.ShapeDtypeStruct((n, d), F32),
        compiler_params=_cparams(("parallel",), "mlp"),
        name="mlp",
    )(x2d, g, w_up, w_down, g_final)


def _w_in_columns(d_model):
    sizes = (("qa", BRANCH_WIDTH), ("ka", BRANCH_WIDTH), ("va", BRANCH_WIDTH),
             ("qb", BRANCH_WIDTH), ("kb", BRANCH_WIDTH), ("vb", BRANCH_WIDTH), ("fl", FOX_HEADS),
             ("qc", DSA_HEADS * HEAD_DIM), ("kc", DSA_KV_HEADS * HEAD_DIM), ("vc", DSA_KV_HEADS * HEAD_DIM),
             ("qi", IDX_HEADS * IDX_DIM), ("ki", IDX_DIM), ("wi", IDX_HEADS), ("gates", N_BRANCH * d_model))
    cols, o = {}, 0
    for name, width in sizes:
        cols[name] = (o, o + width)
        o += width
    return cols, o


def _pack_kernel(w_ref, main_ref, vals_ref, gates_ref):
    cols, _ = _w_in_columns(gates_ref.shape[2] // N_BRANCH)

    def src(name, lo=0, hi=None):
        a, b = cols[name]
        return w_ref[0, :, a + lo:(b if hi is None else a + hi)]

    def put(ref, start, value):
        ref[0, :, start:start + value.shape[1]] = value.astype(ref.dtype)

    put(main_ref, COL_QA, src("qa"))
    put(main_ref, COL_KA, src("ka"))
    put(main_ref, COL_QB, src("qb"))
    put(main_ref, COL_KB, src("kb"))
    put(main_ref, COL_QC, src("qc"))
    for g in range(DSA_KV_HEADS):
        kg = src("kc", g * HEAD_DIM, (g + 1) * HEAD_DIM)
        put(main_ref, COL_KC + g * LANES, jnp.concatenate([kg, kg], axis=1))
    put(main_ref, COL_QI, src("qi"))
    ki = src("ki")
    put(main_ref, COL_KI, jnp.concatenate([ki, ki], axis=1))
    put(vals_ref, ROW_VA, src("va"))
    put(vals_ref, ROW_VB, src("vb"))
    put(vals_ref, ROW_VC, src("vc"))
    pad = jnp.zeros((w_ref.shape[1], LANES - FOX_HEADS - IDX_HEADS), F32)
    put(vals_ref, VT_ROWS, jnp.concatenate([src("fl"), src("wi"), pad], axis=1))
    put(gates_ref, 0, src("gates"))


def _pack_w_in(w_in):
    depth, d, total = w_in.shape
    _, end = _w_in_columns(d)
    assert end == total
    rows = 256
    blk = lambda width: pl.BlockSpec((1, rows, width), lambda l, r: (l, r, 0))
    widths = (MAIN_COLS, VALS_COLS, N_BRANCH * d)
    return pl.pallas_call(
        _pack_kernel,
        grid=(depth, d // rows),
        in_specs=[blk(total)],
        out_specs=[blk(wd) for wd in widths],
        out_shape=[jax.ShapeDtypeStruct((depth, d, wd), MXU_DTYPE) for wd in widths],
        compiler_params=_cparams(("parallel", "parallel"), "pack"),
        name="pack",
    )(w_in)


def _rope_rows():
    lane = jnp.arange(LANES) % HEAD_DIM
    half = ROT_DIM // 2
    inv_freq = ROPE_THETA ** (-jnp.arange(0, ROT_DIM, 2, dtype=F32) / ROT_DIM)
    freq = jnp.where(lane < ROT_DIM, inv_freq[lane % half], 0.0).astype(F32)
    sg1 = jnp.where(lane < half, -1.0, 0.0).astype(F32)
    sg2 = jnp.where((lane >= half) & (lane < ROT_DIM), 1.0, 0.0).astype(F32)
    return freq[None, :], sg1[None, :], sg2[None, :]


def kernel(x, positions, g_mix, w_in, b_forget, w_branch, w_out, g_mlp, w_up, w_down, g_final):
    b, s, d = x.shape
    n = b * s
    depth = w_in.shape[0]
    top_k = min(TOPK_MAX, s // 4)
    tm = min(512, s)
    freq, sg1, sg2 = _rope_rows()
    pos2d = positions.reshape(n, 1)
    x2d = x.reshape(n, d)
    w_main_all, w_v_all, w_gate_all = _pack_w_in(w_in)
    for layer in range(depth):
        w_main, w_v, w_gate = w_main_all[layer], w_v_all[layer], w_gate_all[layer]
        bias_row = jnp.concatenate([b_forget[layer], jnp.zeros((LANES - FOX_HEADS,), F32)])[None, :]
        main, vt, misc = _proj(x2d, pos2d, g_mix[layer][None, :], freq, sg1, sg2, w_main, w_v, tm, b, s)
        main3d = main.reshape(b, s, MAIN_COLS)
        misc3d = misc.reshape(b, s, LANES)
        cum = _cumf(misc3d, bias_row)
        br_a = _sb(main3d, vt).reshape(n, BRANCH_WIDTH)
        br_b = _fox(main3d, vt, cum).reshape(n, BRANCH_WIDTH)
        br_c = _dsa(main3d, vt, misc3d, top_k).reshape(n, BRANCH_WIDTH)
        x2d = _merge(x2d, br_a, br_b, br_c, g_mix[layer][None, :], w_gate,
                     w_branch[layer].astype(MXU_DTYPE), w_out[layer].astype(MXU_DTYPE), tm)
        x2d = _mlp(x2d, g_mlp[layer][None, :], w_up[layer].astype(MXU_DTYPE),
                   w_down[layer].astype(MXU_DTYPE), g_final[None, :], tm, layer == depth - 1)
    return x2d.reshape(b, s, d)
```

```python
import functools

import jax
import jax.numpy as jnp
from jax import lax
from jax.experimental import pallas as pl
from jax.experimental.pallas import tpu as pltpu

F32 = jnp.float32
MXU_DTYPE = jnp.bfloat16

HEAD_DIM = 64
SB_HEADS = 8
FOX_HEADS = 8
DSA_HEADS = 8
DSA_KV_HEADS = 2
DSA_GROUP = DSA_HEADS // DSA_KV_HEADS
IDX_HEADS = 4
IDX_DIM = 64
N_BRANCH = 3
ROPE_THETA = 500000.0
ROT_DIM = HEAD_DIM // 4
TOPK_MAX = 256
NORM_EPS = 1e-6

LANES = 128
SUBLANES = 8
SUM_ROWS = 2 * SUBLANES
BRANCH_WIDTH = SB_HEADS * HEAD_DIM
ATTN_SCALE = HEAD_DIM ** -0.5
assert IDX_DIM == HEAD_DIM
IDX_W_SCALE = IDX_HEADS ** -0.5

COL_QA, COL_KA = 0, 512
COL_QB, COL_KB = 1024, 1536
COL_QC = 2048
COL_KC = 2560
COL_QI = 2816
COL_KI = 3072
MAIN_COLS = 3200
ROPE_START = COL_QC
ROW_VA, ROW_VB, ROW_VC = 0, 512, 1024
VT_ROWS = 1152
VALS_COLS = VT_ROWS + LANES
MISC_WI = FOX_HEADS

TQ = 256
TK = 256
HEADS_PER_STEP = 8
NEG_BIG = -1e30

VMEM_LIMIT = 56 * 1024 * 1024


def _cparams(sem):
    return pltpu.CompilerParams(dimension_semantics=sem, vmem_limit_bytes=VMEM_LIMIT)


def _rmsnorm(x, g):
    y = x * lax.rsqrt(jnp.mean(x * x, axis=-1, keepdims=True) + NORM_EPS)
    return y * g


def _dot(a, b):
    return jnp.dot(a, b, preferred_element_type=F32)


def _dot_nt(a, b):
    return lax.dot_general(a, b, (((1,), (1,)), ((), ())), preferred_element_type=F32)


def _split3(x):
    hi = x.astype(MXU_DTYPE)
    r = x - hi.astype(F32)
    mid = r.astype(MXU_DTYPE)
    lo = (r - mid.astype(F32)).astype(MXU_DTYPE)
    return hi, mid, lo


def _log_sigmoid_parts(z):
    neg_abs = lax.bitcast_convert_type(lax.bitcast_convert_type(z, jnp.int32) | jnp.int32(-2 ** 31), F32)
    sp = jnp.log(1.0 + jnp.exp(neg_abs))
    log_sig = jnp.minimum(z, 0.0) - sp
    return log_sig, z - log_sig


def _proj_kernel(x_ref, pos_ref, g_ref, freq_ref, sg1_ref, sg2_ref, wm_ref, wv_ref,
                 main_ref, vt_ref, misc_ref):
    h = _rmsnorm(x_ref[...], g_ref[...]).astype(MXU_DTYPE)
    ang = pos_ref[...].astype(F32) * freq_ref[...]
    cos = jnp.cos(ang)
    sin = jnp.sin(ang)
    s_up = sin * sg1_ref[...]
    s_dn = sin * sg2_ref[...]
    half = ROT_DIM // 2
    chunk = 4 * LANES
    for c0 in range(0, VALS_COLS, 2 * LANES):
        pv = _dot(h, wv_ref[:, c0:c0 + 2 * LANES])
        for r0 in (c0, c0 + LANES):
            piece = pv[:, r0 - c0:r0 - c0 + LANES]
            if r0 < VT_ROWS:
                vt_ref[0, r0:r0 + LANES, :] = piece.T.astype(vt_ref.dtype)
            else:
                misc_ref[...] = piece
    for c0 in list(range(ROPE_START, MAIN_COLS, chunk)) + list(range(0, ROPE_START, chunk)):
        c1 = min(c0 + chunk, MAIN_COLS)
        p = _dot(h, wm_ref[:, c0:c1])
        if c0 >= ROPE_START:
            for t in range((c1 - c0) // LANES):
                pt = p[:, t * LANES:(t + 1) * LANES]
                pt = (pt * cos + pltpu.roll(pt, LANES - half, 1) * s_up
                      + pltpu.roll(pt, half, 1) * s_dn)
                main_ref[:, c0 + t * LANES:c0 + (t + 1) * LANES] = pt.astype(main_ref.dtype)
        else:
            main_ref[:, c0:c1] = p.astype(main_ref.dtype)


def _proj(x2d, pos2d, g, freq, sg1, sg2, w_main, w_v, tm, batch, seq):
    n, d = x2d.shape
    nsb = seq // tm
    const = lambda i: (0, 0)
    return pl.pallas_call(
        _proj_kernel,
        grid=(n // tm,),
        in_specs=[
            pl.BlockSpec((tm, d), lambda i: (i, 0)),
            pl.BlockSpec((tm, 1), lambda i: (i, 0)),
            pl.BlockSpec((1, d), const),
            pl.BlockSpec((1, LANES), const),
            pl.BlockSpec((1, LANES), const),
            pl.BlockSpec((1, LANES), const),
            pl.BlockSpec((d, MAIN_COLS), const),
            pl.BlockSpec((d, VALS_COLS), const),
        ],
        out_specs=[
            pl.BlockSpec((tm, MAIN_COLS), lambda i: (i, 0)),
            pl.BlockSpec((1, VT_ROWS, tm), lambda i: (i // nsb, 0, i % nsb)),
            pl.BlockSpec((tm, LANES), lambda i: (i, 0)),
        ],
        out_shape=[
            jax.ShapeDtypeStruct((n, MAIN_COLS), MXU_DTYPE),
            jax.ShapeDtypeStruct((batch, VT_ROWS, seq), MXU_DTYPE),
            jax.ShapeDtypeStruct((n, LANES), F32),
        ],
        compiler_params=_cparams(("parallel",)),
        name="proj",
    )(x2d, pos2d, g, freq, sg1, sg2, w_main, w_v)


def _cumf_kernel(misc_ref, bias_ref, c_ref):
    s = misc_ref.shape[1]
    r = lax.broadcasted_iota(jnp.int32, (TK, TK), 0)
    c = lax.broadcasted_iota(jnp.int32, (TK, TK), 1)
    incl = (c <= r).astype(MXU_DTYPE)
    carry = jnp.zeros((1, LANES), F32)
    for b in range(s // TK):
        logit = misc_ref[0, b * TK:(b + 1) * TK, :] + bias_ref[...]
        logf, _ = _log_sigmoid_parts(logit)
        hi, mid, lo = _split3(logf)
        cs = (_dot(incl, lo) + _dot(incl, mid)) + _dot(incl, hi) + carry
        c_ref[0, b * TK:(b + 1) * TK, :] = cs
        carry = cs[TK - 1:TK, :]


def _cumf(misc3d, bias_row):
    b, s, _ = misc3d.shape
    return pl.pallas_call(
        _cumf_kernel,
        grid=(b,),
        in_specs=[
            pl.BlockSpec((1, s, LANES), lambda i: (i, 0, 0)),
            pl.BlockSpec((1, LANES), lambda i: (0, 0)),
        ],
        out_specs=pl.BlockSpec((1, s, LANES), lambda i: (i, 0, 0)),
        out_shape=jax.ShapeDtypeStruct((b, s, LANES), F32),
        compiler_params=_cparams(("parallel",)),
        name="cumf",
    )(misc3d, bias_row)


def _head_q(q_tile, half):
    lane = lax.broadcasted_iota(jnp.int32, q_tile.shape, 1)
    keep = (lane // HEAD_DIM) == half
    return jnp.where(keep, q_tile.astype(F32) * ATTN_SCALE, 0.0).astype(MXU_DTYPE)


def _causal_tile_t(strict):
    key = lax.broadcasted_iota(jnp.int32, (TK, TQ), 0)
    qry = lax.broadcasted_iota(jnp.int32, (TK, TQ), 1)
    return (key < qry) if strict else (key <= qry)


def _rows(j):
    return pl.ds(pl.multiple_of(j * TK, TK), TK)


def _lookahead_loop(n_last, bufs, issue, step, carry, finish, primed=False, fused=None):
    buf_a, buf_b = bufs

    def ahead_and_step(n, buf, other, carry):
        if fused is not None:
            return fused(n, buf, other, carry)
        issue(n + 1, other)
        return step(n, buf, carry, False)

    def pair(t, carry):
        n = 2 * t
        carry = ahead_and_step(n, buf_a, buf_b, carry)
        return ahead_and_step(n + 1, buf_b, buf_a, carry)

    def odd_tail(carry):
        carry = ahead_and_step(n_last - 1, buf_a, buf_b, carry)
        step(n_last, buf_b, carry, True)
        finish()
        return 0

    def even_tail(carry):
        step(n_last, buf_a, carry, True)
        finish()
        return 0

    if not primed:
        issue(0, buf_a)
    carry = lax.fori_loop(0, n_last // 2, pair, carry)
    lax.cond(n_last % 2 == 1, odd_tail, even_tail, carry)


def _flash_heads(heads, n_full, diag_mask, z_bufs, acc_ref, o_ref):
    half_k, half_q = TK // 2, TQ // 2

    def add_bias(s, bias):
        if isinstance(bias, (list, tuple)):
            return jnp.concatenate([s[:, t * LANES:(t + 1) * LANES] + bias[t] for t in range(len(bias))],
                                   axis=1)
        return s + bias

    ones = jnp.ones((SUM_ROWS, TK), MXU_DTYPE)

    def issue_head(h, j, buf):
        rows = _rows(j)
        qh, k_at, _, bias_at = heads[h]
        s = add_bias(_dot_nt(k_at(rows), qh), bias_at(rows))
        buf[h, :TK, :] = s
        buf[h, TK:, :] = jnp.max(s.reshape(TK // SUBLANES, SUBLANES, TQ), axis=0)

    def issue(j, buf):
        for h in range(len(heads)):
            issue_head(h, j, buf)

    def step_and_issue(j, buf, other, maxes):
        rows = _rows(j)
        out = []
        for h, ((_, _, vt_at, _), m) in enumerate(zip(heads, maxes)):
            issue_head(h, j + 1, other)
            m_new = jnp.maximum(m, jnp.max(buf[h, TK:, :], axis=0, keepdims=True))
            p = jnp.exp(buf[h, :TK, :] - m_new).astype(MXU_DTYPE)
            vt1 = jnp.concatenate([vt_at(rows), ones], axis=0)
            acc_ref[h] = jnp.exp(m - m_new) * acc_ref[h] + _dot(vt1, p)
            out.append(m_new)
        return tuple(out)

    def update(tiles, rows, maxes, lanes):
        stats = []
        for s, m in zip(tiles, maxes):
            m_new = jnp.maximum(m, jnp.max(s, axis=0, keepdims=True))
            stats.append((m_new, jnp.exp(m - m_new), jnp.exp(s - m_new).astype(MXU_DTYPE)))
        for h, ((_, alpha, p), (_, _, vt_at, _)) in enumerate(zip(stats, heads)):
            vt1 = jnp.concatenate([vt_at(rows), ones[:, :p.shape[0]]], axis=0)
            acc_ref[h, :, lanes] = alpha * acc_ref[h, :, lanes] + _dot(vt1, p)
        return tuple(m_new for m_new, _, _ in stats)

    def diagonal_step(j, buf, maxes):
        start = pl.multiple_of(j * TK, TK)
        early, late = pl.ds(start, half_k), pl.ds(pl.multiple_of(start + half_k, half_k), half_k)

        def masked(s, mask):
            return s if diag_mask is None else jnp.where(mask, s, NEG_BIG)

        tiles = [masked(buf[h, :half_k, :], None if diag_mask is None else diag_mask[:half_k, :])
                 for h in range(len(heads))]
        maxes = update(tiles, early, maxes, slice(None))
        tiles = [masked(buf[h, half_k:TK, half_q:], None if diag_mask is None else diag_mask[half_k:, half_q:])
                 for h in range(len(heads))]
        update(tiles, late, tuple(m[:, half_q:] for m in maxes), slice(half_q, TQ))
        return maxes

    def finish():
        _store_heads_t(o_ref, [acc_ref[h, :HEAD_DIM, :] / acc_ref[h, HEAD_DIM:HEAD_DIM + 1, :]
                               for h in range(len(heads))])

    acc_ref[...] = jnp.zeros(acc_ref.shape, F32)
    init = tuple(jnp.full((1, TQ), NEG_BIG, F32) for _ in heads)
    _lookahead_loop(n_full, z_bufs, issue, lambda j, buf, maxes, last: diagonal_step(j, buf, maxes), init,
                    finish, fused=step_and_issue)


def _store_heads_t(o_ref, outs_t):
    for t in range(len(outs_t) // 2):
        pair_t = jnp.concatenate([outs_t[2 * t], outs_t[2 * t + 1]], axis=0)
        o_ref[0, :, t * LANES:(t + 1) * LANES] = pair_t.T.astype(o_ref.dtype)


def _sb_kernel(q_ref, k_ref, vt_ref, o_ref, za_ref, zb_ref, acc_ref):
    i = pl.program_id(2)
    r = lax.broadcasted_iota(jnp.int32, (TK, TK), 0)
    c = lax.broadcasted_iota(jnp.int32, (TK, TK), 1)
    after = (c > r).astype(MXU_DTYPE)
    strict = _causal_tile_t(True)
    n_heads = HEADS_PER_STEP
    qhs = [_head_q(q_ref[0][:, (h // 2) * LANES:(h // 2 + 1) * LANES], h % 2) for h in range(n_heads)]

    def issue(j, buf):
        for h in range(n_heads):
            buf[h] = _dot_nt(k_ref[0, _rows(j), (h // 2) * LANES:(h // 2 + 1) * LANES], qhs[h])

    def update(tiles, rows, tails, lanes, mask):
        n_rows = tiles[0].shape[0]
        parts = []
        for z in tiles:
            log_beta, neg_l1m = _log_sigmoid_parts(z)
            if mask is not None:
                neg_l1m = jnp.where(mask, neg_l1m, 0.0)
            parts.append((log_beta, neg_l1m[0:1, :], neg_l1m.astype(MXU_DTYPE)))
        after_rows = after[:n_rows, :n_rows]
        laters = [_dot(after_rows, hl) for (_, _, hl) in parts]
        ws = []
        for (log_beta, _, _), later in zip(parts, laters):
            w = jnp.exp(log_beta - later)
            if mask is not None:
                w = jnp.where(mask, w, 0.0)
            ws.append(w.astype(MXU_DTYPE))
        out = []
        for h, (w, (_, first, _), later, tail) in enumerate(zip(ws, parts, laters, tails)):
            vtj = vt_ref[0, h * HEAD_DIM:(h + 1) * HEAD_DIM, rows]
            acc_ref[h, :, lanes] = acc_ref[h, :, lanes] + jnp.exp(-tail) * _dot(vtj, w)
            out.append(tail + (first + later[0:1, :]))
        return tuple(out)

    def step(j, buf, tails):
        return update([buf[h] for h in range(n_heads)], _rows(j), tails, slice(None), None)

    def diagonal_step(j, buf, tails):
        half_k, half_q = TK // 2, TQ // 2
        start = pl.multiple_of(j * TK, TK)
        early, late = pl.ds(start, half_k), pl.ds(pl.multiple_of(start + half_k, half_k), half_k)
        late_tails = update([buf[h, half_k:, half_q:] for h in range(n_heads)], late,
                            tuple(t[:, half_q:] for t in tails), slice(half_q, TQ), strict[half_k:, half_q:])
        tails = tuple(jnp.concatenate([t[:, :half_q], lt], axis=1) for t, lt in zip(tails, late_tails))
        return update([buf[h, :half_k, :] for h in range(n_heads)], early, tails, slice(None), strict[:half_k, :])

    acc_ref[...] = jnp.zeros(acc_ref.shape, F32)
    carry = tuple(jnp.zeros((1, TQ), F32) for _ in range(n_heads))
    issue(i, za_ref)

    def finish():
        _store_heads_t(o_ref, [acc_ref[h] for h in range(n_heads)])

    def with_earlier_blocks(cr):
        issue(i - 1, zb_ref)
        cr = diagonal_step(i, za_ref, cr)
        _lookahead_loop(i - 1, (zb_ref, za_ref), lambda n, buf: issue(i - 1 - n, buf),
                        lambda n, buf, c, last: step(i - 1 - n, buf, c), cr, finish, primed=True)
        return 0

    def diagonal_only(cr):
        diagonal_step(i, za_ref, cr)
        finish()
        return 0

    lax.cond(i > 0, with_earlier_blocks, diagonal_only, carry)


def _sb(main3d, vt3d):
    b, s, _ = main3d.shape
    nq = s // TQ
    w = HEADS_PER_STEP * HEAD_DIM
    qb, kb, vb = COL_QA // w, COL_KA // w, ROW_VA // w
    return pl.pallas_call(
        _sb_kernel,
        grid=(b, SB_HEADS // HEADS_PER_STEP, nq),
        in_specs=[
            pl.BlockSpec((1, TQ, w), lambda bi, hg, i: (bi, i, qb + hg)),
            pl.BlockSpec((1, s, w), lambda bi, hg, i: (bi, 0, kb + hg)),
            pl.BlockSpec((1, w, s), lambda bi, hg, i: (bi, vb + hg, 0)),
        ],
        out_specs=pl.BlockSpec((1, TQ, w), lambda bi, hg, i: (bi, i, hg)),
        out_shape=jax.ShapeDtypeStruct((b, s, BRANCH_WIDTH), MXU_DTYPE),
        scratch_shapes=[pltpu.VMEM((HEADS_PER_STEP, TK, TQ), F32), pltpu.VMEM((HEADS_PER_STEP, TK, TQ), F32),
                        pltpu.VMEM((HEADS_PER_STEP, HEAD_DIM, TQ), F32)],
        compiler_params=_cparams(("parallel", "parallel", "arbitrary")),
        name="sb",
    )(main3d, main3d, vt3d)


def _fox_kernel(q_ref, k_ref, vt_ref, c_ref, o_ref, cb_ref, za_ref, zb_ref, acc_ref):
    hg = pl.program_id(1)
    i = pl.program_id(2)
    n_heads = HEADS_PER_STEP

    @pl.when(i == 0)
    def _():
        c = c_ref[0]
        lane = lax.broadcasted_iota(jnp.int32, c.shape, 1)
        for h in range(n_heads):
            if n_heads == FOX_HEADS:
                col = c[:, h:h + 1]
            else:
                col = jnp.sum(jnp.where(lane == hg * n_heads + h, c, 0.0), axis=1, keepdims=True)
            cb_ref[h] = jnp.broadcast_to(-col, c.shape)

    heads = []
    for h in range(n_heads):
        t = h // 2
        qh = _head_q(q_ref[0][:, t * LANES:(t + 1) * LANES], h % 2)
        k_at = lambda rows, t=t: k_ref[0, rows, t * LANES:(t + 1) * LANES]
        vt_at = lambda rows, h=h: vt_ref[0, h * HEAD_DIM:(h + 1) * HEAD_DIM, rows]
        bias_at = lambda rows, h=h: [cb_ref[h, rows, :]] * (TQ // LANES)
        heads.append((qh, k_at, vt_at, bias_at))
    _flash_heads(heads, i, _causal_tile_t(False), (za_ref, zb_ref), acc_ref, o_ref)


def _fox(main3d, vt3d, c3d):
    b, s, _ = main3d.shape
    nq = s // TQ
    w = HEADS_PER_STEP * HEAD_DIM
    qb, kb, vb = COL_QB // w, COL_KB // w, ROW_VB // w
    return pl.pallas_call(
        _fox_kernel,
        grid=(b, FOX_HEADS // HEADS_PER_STEP, nq),
        in_specs=[
            pl.BlockSpec((1, TQ, w), lambda bi, hg, i: (bi, i, qb + hg)),
            pl.BlockSpec((1, s, w), lambda bi, hg, i: (bi, 0, kb + hg)),
            pl.BlockSpec((1, w, s), lambda bi, hg, i: (bi, vb + hg, 0)),
            pl.BlockSpec((1, s, LANES), lambda bi, hg, i: (bi, 0, 0)),
        ],
        out_specs=pl.BlockSpec((1, TQ, w), lambda bi, hg, i: (bi, i, hg)),
        out_shape=jax.ShapeDtypeStruct((b, s, BRANCH_WIDTH), MXU_DTYPE),
        scratch_shapes=[pltpu.VMEM((HEADS_PER_STEP, s, LANES), F32),
                        pltpu.VMEM((HEADS_PER_STEP, TK + SUBLANES, TQ), F32),
                        pltpu.VMEM((HEADS_PER_STEP, TK + SUBLANES, TQ), F32),
                        pltpu.VMEM((HEADS_PER_STEP, HEAD_DIM + SUM_ROWS, TQ), F32)],
        compiler_params=_cparams(("parallel", "parallel", "arbitrary")),
        name="fox",
    )(main3d, main3d, vt3d, c3d)


def _dsa_select(i, qi_ref, ki_ref, misc_ref, key_ref, dig_ref, bias_ref, top_k):
    diag = _causal_tile_t(False)

    wi_t = misc_ref[0].T[MISC_WI:MISC_WI + IDX_HEADS, :] * IDX_W_SCALE
    qih = [_head_q(qi_ref[0][:, (h // 2) * LANES:(h // 2 + 1) * LANES], h % 2) for h in range(IDX_HEADS)]

    def score_block(j, mask):
        kij = ki_ref[0, _rows(j), :]
        sc = None
        for h in range(IDX_HEADS):
            term = wi_t[h:h + 1, :] * jnp.maximum(_dot_nt(kij, qih[h]), 0.0)
            sc = term if sc is None else sc + term
        if mask is not None:
            sc = jnp.where(mask, sc, -jnp.inf)
        bits = lax.bitcast_convert_type(jnp.where(sc == 0.0, 0.0, sc), jnp.int32)
        key_ref[_rows(j), :] = bits ^ ((bits >> 31) & jnp.int32(0x7FFFFFFF))

    def score_pair(t, _):
        score_block(2 * t, None)
        score_block(2 * t + 1, None)
        return 0

    def score_odd_tail():
        score_block(i - 1, None)
        score_block(i, diag)
        return 0

    def score_even_tail():
        score_block(i, diag)
        return 0

    lax.fori_loop(0, i // 2, score_pair, 0)
    lax.cond(i % 2 == 1, score_odd_tail, score_even_tail)

    small = MXU_DTYPE
    group = 2 * SUBLANES

    assert (key_ref.shape[0] // group) <= 256

    def count_digits_ge(cand):
        def hits(j):
            hit = jnp.where(dig_ref[_rows(j), :] >= cand, jnp.ones((), small), jnp.zeros((), small))
            part = hit[0:group]
            for g in range(1, TK // group):
                part = part + hit[g * group:(g + 1) * group]
            return part

        acc = lax.fori_loop(0, (i + 1) // 2, lambda t, a: a + (hits(2 * t) + hits(2 * t + 1)),
                            jnp.zeros((group, TQ), small))
        acc = lax.cond((i + 1) % 2 == 1, lambda a: a + hits(i), lambda a: a, acc)
        return jnp.sum(acc.astype(F32), axis=0, keepdims=True)

    above = jnp.zeros((1, TQ), F32)
    prefix = None
    for shift in (24, 16, 8, 0):
        def fill(j, _, shift=shift, prefix=prefix):
            key = key_ref[_rows(j), :]
            if prefix is None:
                digit = (key >> shift) + 128
            else:
                digit = (key ^ (prefix << (shift + 8))) >> shift
                shares = lax.bitcast_convert_type(digit, jnp.uint32) < jnp.uint32(256)
                digit = jnp.where(shares, digit, -1)
            dig_ref[_rows(j), :] = digit.astype(F32).astype(small)
            return 0

        lax.fori_loop(0, i + 1, fill, 0)

        def search(_, st, above=above):
            lo, hi, n_hi = st
            mid = (lo + hi) * 0.5
            n_mid = count_digits_ge(mid.astype(small))
            ok = above + n_mid >= top_k
            return jnp.where(ok, mid, lo), jnp.where(ok, hi, mid), jnp.where(ok, n_hi, n_mid)

        start = (jnp.zeros((1, TQ), F32), jnp.full((1, TQ), 256.0, F32), jnp.zeros((1, TQ), F32))
        digit, _, n_hi = lax.fori_loop(0, 8, search, start)
        above = above + n_hi
        digit = digit.astype(jnp.int32)
        prefix = digit - 128 if prefix is None else (prefix << 8) | digit
    thr_key = prefix

    need = top_k - above
    kr = lax.broadcasted_iota(jnp.int32, (TK, TK), 0)
    kc = lax.broadcasted_iota(jnp.int32, (TK, TK), 1)
    upto = (kc <= kr).astype(MXU_DTYPE)

    def bias_block(j, mask, seen):
        key = key_ref[_rows(j), :]
        tie = key == thr_key
        rank = _dot(upto, jnp.where(tie, 1.0, 0.0).astype(MXU_DTYPE)) + seen
        sel = (key > thr_key) | (tie & (rank <= need))
        if mask is not None:
            sel = sel & mask
        bias_ref[_rows(j), :] = jnp.where(sel, 0.0, NEG_BIG)
        return rank[TK - 1:TK, :]

    def bias_pair(t, seen):
        return bias_block(2 * t + 1, None, bias_block(2 * t, None, seen))

    def bias_odd_tail(seen):
        bias_block(i, diag, bias_block(i - 1, None, seen))
        return 0

    def bias_even_tail(seen):
        bias_block(i, diag, seen)
        return 0

    seen = lax.fori_loop(0, i // 2, bias_pair, jnp.zeros((1, TQ), F32))
    lax.cond(i % 2 == 1, bias_odd_tail, bias_even_tail, seen)


def _dsa_kernel(q_ref, k_ref, vt_ref, qi_ref, ki_ref, misc_ref, o_ref, key_ref, dig_ref, bias_ref,
                za_ref, zb_ref, acc_ref, *, top_k):
    i = pl.program_id(1)

    def every_causal_key():
        def earlier_block(j, _):
            bias_ref[_rows(j), :] = jnp.zeros((TK, TQ), F32)
            return 0

        lax.fori_loop(0, i, earlier_block, 0)
        bias_ref[_rows(i), :] = jnp.where(_causal_tile_t(False), 0.0, NEG_BIG)
        return 0

    def searched_keys():
        _dsa_select(i, qi_ref, ki_ref, misc_ref, key_ref, dig_ref, bias_ref, top_k)
        return 0

    lax.cond((i + 1) * TQ <= top_k, every_causal_key, searched_keys)

    bias_at = lambda rows: bias_ref[rows, :]
    heads = []
    for h in range(DSA_HEADS):
        g = h // DSA_GROUP
        k_at = lambda rows, g=g: k_ref[0, rows, g * LANES:(g + 1) * LANES]
        vt_at = lambda rows, g=g: vt_ref[0, g * HEAD_DIM:(g + 1) * HEAD_DIM, rows]
        qh = _head_q(q_ref[0][:, (h // 2) * LANES:(h // 2 + 1) * LANES], h % 2)
        heads.append((qh, k_at, vt_at, bias_at))
    _flash_heads(heads, i, None, (za_ref, zb_ref), acc_ref, o_ref)


def _dsa(main3d, vt3d, misc3d, top_k):
    b, s, _ = main3d.shape
    nq = s // TQ
    assert top_k <= TK, "the first key block must hold at least top_k keys"
    return pl.pallas_call(
        functools.partial(_dsa_kernel, top_k=top_k),
        grid=(b, nq),
        in_specs=[
            pl.BlockSpec((1, TQ, 4 * LANES), lambda bi, i: (bi, i, COL_QC // (4 * LANES))),
            pl.BlockSpec((1, s, 2 * LANES), lambda bi, i: (bi, 0, COL_KC // (2 * LANES))),
            pl.BlockSpec((1, LANES, s), lambda bi, i: (bi, ROW_VC // LANES, 0)),
            pl.BlockSpec((1, TQ, 2 * LANES), lambda bi, i: (bi, i, COL_QI // (2 * LANES))),
            pl.BlockSpec((1, s, LANES), lambda bi, i: (bi, 0, COL_KI // LANES)),
            pl.BlockSpec((1, TQ, LANES), lambda bi, i: (bi, i, 0)),
        ],
        out_specs=pl.BlockSpec((1, TQ, BRANCH_WIDTH), lambda bi, i: (bi, i, 0)),
        out_shape=jax.ShapeDtypeStruct((b, s, BRANCH_WIDTH), MXU_DTYPE),
        scratch_shapes=[pltpu.VMEM((s, TQ), jnp.int32), pltpu.VMEM((s, TQ), MXU_DTYPE), pltpu.VMEM((s, TQ), F32),
                        pltpu.VMEM((DSA_HEADS, TK + SUBLANES, TQ), F32),
                        pltpu.VMEM((DSA_HEADS, TK + SUBLANES, TQ), F32),
                        pltpu.VMEM((DSA_HEADS, HEAD_DIM + SUM_ROWS, TQ), F32)],
        compiler_params=_cparams(("parallel", "arbitrary")),
        name="dsa",
    )(main3d, main3d, vt3d, main3d, main3d, misc3d)


def _merge_kernel(x_ref, a_ref, b_ref, c_ref, g_ref, wg_ref, wb_ref, wo_ref, o_ref):
    x = x_ref[...]
    d = x.shape[1]
    h = _rmsnorm(x, g_ref[...]).astype(MXU_DTYPE)
    y = None
    for n, br_ref in enumerate((a_ref, b_ref, c_ref)):
        gate = jax.nn.sigmoid(_dot(h, wg_ref[:, n * d:(n + 1) * d]))
        term = gate * _dot(br_ref[...], wb_ref[n])
        y = term if y is None else y + term
    o_ref[...] = x + _dot(y.astype(MXU_DTYPE), wo_ref[...])


def _merge(x2d, br_a, br_b, br_c, g, w_gate, w_branch, w_out, tm):
    n, d = x2d.shape
    bw = br_a.shape[1]
    row = lambda i: (i, 0)
    const = lambda i: (0, 0)
    return pl.pallas_call(
        _merge_kernel,
        grid=(n // tm,),
        in_specs=[
            pl.BlockSpec((tm, d), row),
            pl.BlockSpec((tm, bw), row),
            pl.BlockSpec((tm, bw), row),
            pl.BlockSpec((tm, bw), row),
            pl.BlockSpec((1, d), const),
            pl.BlockSpec((d, N_BRANCH * d), const),
            pl.BlockSpec((N_BRANCH, bw, d), lambda i: (0, 0, 0)),
            pl.BlockSpec((d, d), const),
        ],
        out_specs=pl.BlockSpec((tm, d), row),
        out_shape=jax.ShapeDtypeStruct((n, d), F32),
        compiler_params=_cparams(("parallel",)),
        name="merge",
    )(x2d, br_a, br_b, br_c, g, w_gate, w_branch, w_out)


def _mlp_kernel(x_ref, g_ref, wu_ref, wd_ref, gf_ref, o_ref, *, ff_chunk, final_norm):
    x = x_ref[...]
    h = _rmsnorm(x, g_ref[...]).astype(MXU_DTYPE)
    acc = x
    for c in range(wu_ref.shape[1] // ff_chunk):
        u = jnp.maximum(_dot(h, wu_ref[:, c * ff_chunk:(c + 1) * ff_chunk]), 0.0)
        acc = acc + _dot((u * u).astype(MXU_DTYPE), wd_ref[c * ff_chunk:(c + 1) * ff_chunk, :])
    if final_norm:
        acc = _rmsnorm(acc, gf_ref[...])
    o_ref[...] = acc


def _mlp(x2d, g, w_up, w_down, g_final, tm, final_norm):
    n, d = x2d.shape
    ff = w_up.shape[1]
    row = lambda i: (i, 0)
    const = lambda i: (0, 0)
    return pl.pallas_call(
        functools.partial(_mlp_kernel, ff_chunk=min(ff, 1024), final_norm=final_norm),
        grid=(n // tm,),
        in_specs=[
            pl.BlockSpec((tm, d), row),
            pl.BlockSpec((1, d), const),
            pl.BlockSpec((d, ff), const),
            pl.BlockSpec((ff, d), const),
            pl.BlockSpec((1, d), const),
        ],
        out_specs=pl.BlockSpec((tm, d), row),
        out_shape=jax.ShapeDtypeStruct((n, d), F32),
        compiler_params=_cparams(("parallel",)),
        name="mlp",
    )(x2d, g, w_up, w_down, g_final)


def _w_in_columns(d_model):
    sizes = (("qa", BRANCH_WIDTH), ("ka", BRANCH_WIDTH), ("va", BRANCH_WIDTH),
             ("qb", BRANCH_WIDTH), ("kb", BRANCH_WIDTH), ("vb", BRANCH_WIDTH), ("fl", FOX_HEADS),
             ("qc", DSA_HEADS * HEAD_DIM), ("kc", DSA_KV_HEADS * HEAD_DIM), ("vc", DSA_KV_HEADS * HEAD_DIM),
             ("qi", IDX_HEADS * IDX_DIM), ("ki", IDX_DIM), ("wi", IDX_HEADS), ("gates", N_BRANCH * d_model))
    cols, o = {}, 0
    for name, width in sizes:
        cols[name] = (o, o + width)
        o += width
    return cols, o


def _pack_kernel(w_ref, main_ref, vals_ref, gates_ref):
    cols, _ = _w_in_columns(gates_ref.shape[2] // N_BRANCH)

    def src(name, lo=0, hi=None):
        a, b = cols[name]
        return w_ref[0, :, a + lo:(b if hi is None else a + hi)]

    def put(ref, start, value):
        ref[0, :, start:start + value.shape[1]] = value.astype(ref.dtype)

    put(main_ref, COL_QA, src("qa"))
    put(main_ref, COL_KA, src("ka"))
    put(main_ref, COL_QB, src("qb"))
    put(main_ref, COL_KB, src("kb"))
    put(main_ref, COL_QC, src("qc"))
    for g in range(DSA_KV_HEADS):
        kg = src("kc", g * HEAD_DIM, (g + 1) * HEAD_DIM)
        put(main_ref, COL_KC + g * LANES, jnp.concatenate([kg, kg], axis=1))
    put(main_ref, COL_QI, src("qi"))
    ki = src("ki")
    put(main_ref, COL_KI, jnp.concatenate([ki, ki], axis=1))
    put(vals_ref, ROW_VA, src("va"))
    put(vals_ref, ROW_VB, src("vb"))
    put(vals_ref, ROW_VC, src("vc"))
    pad = jnp.zeros((w_ref.shape[1], LANES - FOX_HEADS - IDX_HEADS), F32)
    put(vals_ref, VT_ROWS, jnp.concatenate([src("fl"), src("wi"), pad], axis=1))
    put(gates_ref, 0, src("gates"))


def _pack_w_in(w_in):
    depth, d, total = w_in.shape
    _, end = _w_in_columns(d)
    assert end == total
    rows = 256
    blk = lambda width: pl.BlockSpec((1, rows, width), lambda l, r: (l, r, 0))
    widths = (MAIN_COLS, VALS_COLS, N_BRANCH * d)
    return pl.pallas_call(
        _pack_kernel,
        grid=(depth, d // rows),
        in_specs=[blk(total)],
        out_specs=[blk(wd) for wd in widths],
        out_shape=[jax.ShapeDtypeStruct((depth, d, wd), MXU_DTYPE) for wd in widths],
        compiler_params=_cparams(("parallel", "parallel")),
        name="pack",
    )(w_in)


def _rope_rows():
    lane = jnp.arange(LANES) % HEAD_DIM
    half = ROT_DIM // 2
    inv_freq = ROPE_THETA ** (-jnp.arange(0, ROT_DIM, 2, dtype=F32) / ROT_DIM)
    freq = jnp.where(lane < ROT_DIM, inv_freq[lane % half], 0.0).astype(F32)
    sg1 = jnp.where(lane < half, -1.0, 0.0).astype(F32)
    sg2 = jnp.where((lane >= half) & (lane < ROT_DIM), 1.0, 0.0).astype(F32)
    return freq[None, :], sg1[None, :], sg2[None, :]


def kernel(x, positions, g_mix, w_in, b_forget, w_branch, w_out, g_mlp, w_up, w_down, g_final):
    b, s, d = x.shape
    n = b * s
    depth = w_in.shape[0]
    top_k = min(TOPK_MAX, s // 4)
    tm = min(512, s)
    freq, sg1, sg2 = _rope_rows()
    pos2d = positions.reshape(n, 1)
    x2d = x.reshape(n, d)
    w_main_all, w_v_all, w_gate_all = _pack_w_in(w_in)
    for layer in range(depth):
        w_main, w_v, w_gate = w_main_all[layer], w_v_all[layer], w_gate_all[layer]
        bias_row = jnp.concatenate([b_forget[layer], jnp.zeros((LANES - FOX_HEADS,), F32)])[None, :]
        main, vt, misc = _proj(x2d, pos2d, g_mix[layer][None, :], freq, sg1, sg2, w_main, w_v, min(2 * tm, s),
                               b, s)
        main3d = main.reshape(b, s, MAIN_COLS)
        misc3d = misc.reshape(b, s, LANES)
        cum = _cumf(misc3d, bias_row)
        br_a = _sb(main3d, vt).reshape(n, BRANCH_WIDTH)
        br_b = _fox(main3d, vt, cum).reshape(n, BRANCH_WIDTH)
        br_c = _dsa(main3d, vt, misc3d, top_k).reshape(n, BRANCH_WIDTH)
        x2d = _merge(x2d, br_a, br_b, br_c, g_mix[layer][None, :], w_gate,
                     w_branch[layer].astype(MXU_DTYPE), w_out[layer].astype(MXU_DTYPE), 2 * tm)
        x2d = _mlp(x2d, g_mlp[layer][None, :], w_up[layer].astype(MXU_DTYPE),
                   w_down[layer].astype(MXU_DTYPE), g_final[None, :], tm, layer == depth - 1)
    return x2d.reshape(b, s, d)
```
